```python
import jax, jax.numpy as jnp
from jax import lax
import numpy as np

D_MODEL = 1024
BATCH = 32
SEQ = 2048
DEPTH = 1

MIX_WIDTH = D_MODEL
LRU_WIDTH = MIX_WIDTH // 2
LRU_HEADS = 8
LRU_HEAD_DIM = LRU_WIDTH // LRU_HEADS
LRU_CONV_K = 4
LRU_C = 8.0
CONV_WIDTH = MIX_WIDTH - LRU_WIDTH
CONV_HEADS = 8
CONV_HEAD_DIM = CONV_WIDTH // CONV_HEADS
CONV_K = 31
D_FF = ((8 * D_MODEL // 3 + 255) // 256) * 256
FFN_CONV_K = 3
ALPHA = (2 * DEPTH) ** 0.25
BETA = (8 * DEPTH) ** -0.25
LN_EPS = 1e-5

kernel_name = "hybrid_rglru_conformer_convglu_deepnorm_block"


def layer_norm(x, g, b):
    xf = x.astype(jnp.float32)
    mu = jnp.mean(xf, axis=-1, keepdims=True)
    var = jnp.mean(jnp.square(xf - mu), axis=-1, keepdims=True)
    y = (xf - mu) * lax.rsqrt(var + LN_EPS)
    return (y * g.astype(jnp.float32) + b.astype(jnp.float32)).astype(x.dtype)


def head_layer_norm(x, n_heads, g, b):
    bsz, seq, ch = x.shape
    xf = x.astype(jnp.float32).reshape(bsz, seq, n_heads, ch // n_heads)
    mu = jnp.mean(xf, axis=-1, keepdims=True)
    var = jnp.mean(jnp.square(xf - mu), axis=-1, keepdims=True)
    y = ((xf - mu) * lax.rsqrt(var + LN_EPS)).reshape(bsz, seq, ch)
    return (y * g.astype(jnp.float32) + b.astype(jnp.float32)).astype(x.dtype)


def causal_dwconv(x, w, b):
    k, ch = w.shape
    y = lax.conv_general_dilated(
        x, w[:, None, :].astype(x.dtype), window_strides=(1,),
        padding=[(k - 1, 0)], dimension_numbers=("NWC", "WIO", "NWC"),
        feature_group_count=ch)
    return y + b.astype(x.dtype)


def rg_lru(x, w_r, b_r, w_i, b_i, lam):
    bsz, seq, ch = x.shape
    xf = x.astype(jnp.float32)
    xh = xf.reshape(bsz, seq, LRU_HEADS, LRU_HEAD_DIM)
    gr = jnp.einsum("bshi,hij->bshj", xh, w_r.astype(jnp.float32)).reshape(bsz, seq, ch)
    gi = jnp.einsum("bshi,hij->bshj", xh, w_i.astype(jnp.float32)).reshape(bsz, seq, ch)
    r = jax.nn.sigmoid(gr + b_r.astype(jnp.float32))
    i = jax.nn.sigmoid(gi + b_i.astype(jnp.float32))
    log_a = -LRU_C * r * jax.nn.softplus(-lam.astype(jnp.float32))
    a = jnp.exp(log_a)
    mult = jnp.sqrt(-jnp.expm1(2.0 * log_a))
    u = mult * (i * xf)

    def combine(left, right):
        a_l, h_l = left
        a_r, h_r = right
        return a_l * a_r, a_r * h_l + h_r

    _, h = lax.associative_scan(combine, (a, u), axis=1)
    return h.astype(x.dtype)


def hybrid_mixer(u, w_in, lru_conv_w, lru_conv_b, lru_w_r, lru_b_r, lru_w_i, lru_b_i,
                 lru_lambda, conv_w, conv_b, conv_norm_g, conv_norm_b, w_out):
    proj = jnp.einsum("bsd,de->bse", u, w_in)
    xa, ga, vb, gb = jnp.split(
        proj, [LRU_WIDTH, 2 * LRU_WIDTH, 2 * LRU_WIDTH + CONV_WIDTH], axis=-1)
    xa = causal_dwconv(xa, lru_conv_w, lru_conv_b)
    ha = rg_lru(xa, lru_w_r, lru_b_r, lru_w_i, lru_b_i, lru_lambda)
    ya = jax.nn.gelu(ga) * ha
    vb = vb * jax.nn.sigmoid(gb)
    vb = causal_dwconv(vb, conv_w, conv_b)
    yb = jax.nn.silu(head_layer_norm(vb, CONV_HEADS, conv_norm_g, conv_norm_b))
    y = jnp.concatenate([ya, yb], axis=-1)
    return jnp.einsum("bse,ed->bsd", y, w_out)


def conv_glu_ffn(u, w_up, conv_w, conv_b, w_down):
    h = jnp.einsum("bsd,df->bsf", u, w_up)
    v, g = jnp.split(h, 2, axis=-1)
    g = causal_dwconv(g, conv_w, conv_b)
    return jnp.einsum("bsf,fd->bsd", jax.nn.silu(g) * v, w_down)


def _fwd_setup_inputs(seed: int = 0) -> dict:
    key = jax.random.key(seed)
    ks = jax.random.split(key, 26)
    L, D = DEPTH, D_MODEL

    def nrm(k, shape, scale):
        return jax.random.normal(k, shape, jnp.float32) * scale

    a_pow = jax.random.uniform(ks[11], (L, LRU_WIDTH), jnp.float32, 0.9, 0.999)
    a_base = a_pow ** (1.0 / LRU_C)
    lru_lambda = jnp.log(a_base) - jnp.log1p(-a_base)
    return {
        "x": nrm(ks[0], (BATCH, SEQ, D), 1.0),
        "c": nrm(ks[1], (BATCH, D), 1.0),
        "w_ada": nrm(ks[2], (L, D, 6 * D), 0.1 * D ** -0.5),
        "b_ada": nrm(ks[3], (L, 6 * D), 0.01),
        "w_in": nrm(ks[4], (L, D, 2 * LRU_WIDTH + 2 * CONV_WIDTH), D ** -0.5),
        "lru_conv_w": nrm(ks[5], (L, LRU_CONV_K, LRU_WIDTH), LRU_CONV_K ** -0.5),
        "lru_conv_b": nrm(ks[6], (L, LRU_WIDTH), 0.01),
        "lru_w_r": nrm(ks[7], (L, LRU_HEADS, LRU_HEAD_DIM, LRU_HEAD_DIM), LRU_HEAD_DIM ** -0.5),
        "lru_b_r": nrm(ks[8], (L, LRU_WIDTH), 0.01),
        "lru_w_i": nrm(ks[9], (L, LRU_HEADS, LRU_HEAD_DIM, LRU_HEAD_DIM), LRU_HEAD_DIM ** -0.5),
        "lru_b_i": nrm(ks[10], (L, LRU_WIDTH), 0.01),
        "lru_lambda": lru_lambda,
        "conv_w": nrm(ks[12], (L, CONV_K, CONV_WIDTH), CONV_K ** -0.5),
        "conv_b": nrm(ks[13], (L, CONV_WIDTH), 0.01),
        "conv_norm_g": 1.0 + nrm(ks[14], (L, CONV_WIDTH), 0.01),
        "conv_norm_b": nrm(ks[15], (L, CONV_WIDTH), 0.01),
        "w_out": nrm(ks[16], (L, MIX_WIDTH, D), BETA * MIX_WIDTH ** -0.5),
        "ln1_g": 1.0 + nrm(ks[17], (L, D), 0.01),
        "ln1_b": nrm(ks[18], (L, D), 0.01),
        "ffn_w_up": nrm(ks[19], (L, D, 2 * D_FF), D ** -0.5),
        "ffn_conv_w": nrm(ks[20], (L, FFN_CONV_K, D_FF), FFN_CONV_K ** -0.5),
        "ffn_conv_b": nrm(ks[21], (L, D_FF), 0.01),
        "ffn_w_down": nrm(ks[22], (L, D_FF, D), BETA * D_FF ** -0.5),
        "ln2_g": 1.0 + nrm(ks[23], (L, D), 0.01),
        "ln2_b": nrm(ks[24], (L, D), 0.01),
    }


def _fwd_reference(x, c, w_ada, b_ada, w_in, lru_conv_w, lru_conv_b, lru_w_r, lru_b_r,
              lru_w_i, lru_b_i, lru_lambda, conv_w, conv_b, conv_norm_g, conv_norm_b,
              w_out, ln1_g, ln1_b, ffn_w_up, ffn_conv_w, ffn_conv_b, ffn_w_down,
              ln2_g, ln2_b):
    c_act = jax.nn.silu(c)
    for l in range(DEPTH):
        mod = jnp.einsum("bd,de->be", c_act, w_ada[l]) + b_ada[l]
        sh1, sc1, gt1, sh2, sc2, gt2 = [m[:, None, :] for m in jnp.split(mod, 6, axis=-1)]
        u = x * (1.0 + sc1) + sh1
        y = hybrid_mixer(u, w_in[l], lru_conv_w[l], lru_conv_b[l], lru_w_r[l], lru_b_r[l],
                         lru_w_i[l], lru_b_i[l], lru_lambda[l], conv_w[l], conv_b[l],
                         conv_norm_g[l], conv_norm_b[l], w_out[l])
        x = layer_norm(ALPHA * x + (1.0 + gt1) * y, ln1_g[l], ln1_b[l])
        u = x * (1.0 + sc2) + sh2
        y = conv_glu_ffn(u, ffn_w_up[l], ffn_conv_w[l], ffn_conv_b[l], ffn_w_down[l])
        x = layer_norm(ALPHA * x + (1.0 + gt2) * y, ln2_g[l], ln2_b[l])
    return x


import jax as _jax
import jax.numpy as _jnp

TWIN_FORMAT = 'train_step'
FWD_PARAMS = ['x', 'c', 'w_ada', 'b_ada', 'w_in', 'lru_conv_w', 'lru_conv_b', 'lru_w_r', 'lru_b_r', 'lru_w_i', 'lru_b_i', 'lru_lambda', 'conv_w', 'conv_b', 'conv_norm_g', 'conv_norm_b', 'w_out', 'ln1_g', 'ln1_b', 'ffn_w_up', 'ffn_conv_w', 'ffn_conv_b', 'ffn_w_down', 'ln2_g', 'ln2_b']
TWIN_WEIGHTS = ['w_ada', 'b_ada', 'w_in', 'lru_conv_w', 'lru_conv_b', 'lru_w_r', 'lru_b_r', 'lru_w_i', 'lru_b_i', 'lru_lambda', 'conv_w', 'conv_b', 'conv_norm_g', 'conv_norm_b', 'w_out', 'ln1_g', 'ln1_b', 'ffn_w_up', 'ffn_conv_w', 'ffn_conv_b', 'ffn_w_down', 'ln2_g', 'ln2_b']
TWIN_DIFF_INPUT = 'x'
TWIN_INPUTS = ['x', 'c', 'w_ada', 'b_ada', 'w_in', 'lru_conv_w', 'lru_conv_b', 'lru_w_r', 'lru_b_r', 'lru_w_i', 'lru_b_i', 'lru_lambda', 'conv_w', 'conv_b', 'conv_norm_g', 'conv_norm_b', 'w_out', 'ln1_g', 'ln1_b', 'ffn_w_up', 'ffn_conv_w', 'ffn_conv_b', 'ffn_w_down', 'ln2_g', 'ln2_b', 'loss_target', 'm_w_ada', 'm_b_ada', 'm_w_in', 'm_lru_conv_w', 'm_lru_conv_b', 'm_lru_w_r', 'm_lru_b_r', 'm_lru_w_i', 'm_lru_b_i', 'm_lru_lambda', 'm_conv_w', 'm_conv_b', 'm_conv_norm_g', 'm_conv_norm_b', 'm_w_out', 'm_ln1_g', 'm_ln1_b', 'm_ffn_w_up', 'm_ffn_conv_w', 'm_ffn_conv_b', 'm_ffn_w_down', 'm_ln2_g', 'm_ln2_b', 'v_w_ada', 'v_b_ada', 'v_w_in', 'v_lru_conv_w', 'v_lru_conv_b', 'v_lru_w_r', 'v_lru_b_r', 'v_lru_w_i', 'v_lru_b_i', 'v_lru_lambda', 'v_conv_w', 'v_conv_b', 'v_conv_norm_g', 'v_conv_norm_b', 'v_w_out', 'v_ln1_g', 'v_ln1_b', 'v_ffn_w_up', 'v_ffn_conv_w', 'v_ffn_conv_b', 'v_ffn_w_down', 'v_ln2_g', 'v_ln2_b']
TWIN_OUTPUTS = ['loss', 'grad_x', 'grad_w_ada', 'grad_b_ada', 'grad_w_in', 'grad_lru_conv_w', 'grad_lru_conv_b', 'grad_lru_w_r', 'grad_lru_b_r', 'grad_lru_w_i', 'grad_lru_b_i', 'grad_lru_lambda', 'grad_conv_w', 'grad_conv_b', 'grad_conv_norm_g', 'grad_conv_norm_b', 'grad_w_out', 'grad_ln1_g', 'grad_ln1_b', 'grad_ffn_w_up', 'grad_ffn_conv_w', 'grad_ffn_conv_b', 'grad_ffn_w_down', 'grad_ln2_g', 'grad_ln2_b', 'delta_w_ada', 'delta_b_ada', 'delta_w_in', 'delta_lru_conv_w', 'delta_lru_conv_b', 'delta_lru_w_r', 'delta_lru_b_r', 'delta_lru_w_i', 'delta_lru_b_i', 'delta_lru_lambda', 'delta_conv_w', 'delta_conv_b', 'delta_conv_norm_g', 'delta_conv_norm_b', 'delta_w_out', 'delta_ln1_g', 'delta_ln1_b', 'delta_ffn_w_up', 'delta_ffn_conv_w', 'delta_ffn_conv_b', 'delta_ffn_w_down', 'delta_ln2_g', 'delta_ln2_b', 'new_m_w_ada', 'new_m_b_ada', 'new_m_w_in', 'new_m_lru_conv_w', 'new_m_lru_conv_b', 'new_m_lru_w_r', 'new_m_lru_b_r', 'new_m_lru_w_i', 'new_m_lru_b_i', 'new_m_lru_lambda', 'new_m_conv_w', 'new_m_conv_b', 'new_m_conv_norm_g', 'new_m_conv_norm_b', 'new_m_w_out', 'new_m_ln1_g', 'new_m_ln1_b', 'new_m_ffn_w_up', 'new_m_ffn_conv_w', 'new_m_ffn_conv_b', 'new_m_ffn_w_down', 'new_m_ln2_g', 'new_m_ln2_b', 'new_v_w_ada', 'new_v_b_ada', 'new_v_w_in', 'new_v_lru_conv_w', 'new_v_lru_conv_b', 'new_v_lru_w_r', 'new_v_lru_b_r', 'new_v_lru_w_i', 'new_v_lru_b_i', 'new_v_lru_lambda', 'new_v_conv_w', 'new_v_conv_b', 'new_v_conv_norm_g', 'new_v_conv_norm_b', 'new_v_w_out', 'new_v_ln1_g', 'new_v_ln1_b', 'new_v_ffn_w_up', 'new_v_ffn_conv_w', 'new_v_ffn_conv_b', 'new_v_ffn_w_down', 'new_v_ln2_g', 'new_v_ln2_b']
TWIN_LEAF_KINDS = {'loss': 'loss', 'grad_x': 'grad_x', 'grad_w_ada': 'grad_w', 'grad_b_ada': 'grad_w', 'grad_w_in': 'grad_w', 'grad_lru_conv_w': 'grad_w', 'grad_lru_conv_b': 'grad_w', 'grad_lru_w_r': 'grad_w', 'grad_lru_b_r': 'grad_w', 'grad_lru_w_i': 'grad_w', 'grad_lru_b_i': 'grad_w', 'grad_lru_lambda': 'grad_w', 'grad_conv_w': 'grad_w', 'grad_conv_b': 'grad_w', 'grad_conv_norm_g': 'grad_w', 'grad_conv_norm_b': 'grad_w', 'grad_w_out': 'grad_w', 'grad_ln1_g': 'grad_w', 'grad_ln1_b': 'grad_w', 'grad_ffn_w_up': 'grad_w', 'grad_ffn_conv_w': 'grad_w', 'grad_ffn_conv_b': 'grad_w', 'grad_ffn_w_down': 'grad_w', 'grad_ln2_g': 'grad_w', 'grad_ln2_b': 'grad_w', 'delta_w_ada': 'delta_w', 'delta_b_ada': 'delta_w', 'delta_w_in': 'delta_w', 'delta_lru_conv_w': 'delta_w', 'delta_lru_conv_b': 'delta_w', 'delta_lru_w_r': 'delta_w', 'delta_lru_b_r': 'delta_w', 'delta_lru_w_i': 'delta_w', 'delta_lru_b_i': 'delta_w', 'delta_lru_lambda': 'delta_w', 'delta_conv_w': 'delta_w', 'delta_conv_b': 'delta_w', 'delta_conv_norm_g': 'delta_w', 'delta_conv_norm_b': 'delta_w', 'delta_w_out': 'delta_w', 'delta_ln1_g': 'delta_w', 'delta_ln1_b': 'delta_w', 'delta_ffn_w_up': 'delta_w', 'delta_ffn_conv_w': 'delta_w', 'delta_ffn_conv_b': 'delta_w', 'delta_ffn_w_down': 'delta_w', 'delta_ln2_g': 'delta_w', 'delta_ln2_b': 'delta_w', 'new_m_w_ada': 'new_m', 'new_m_b_ada': 'new_m', 'new_m_w_in': 'new_m', 'new_m_lru_conv_w': 'new_m', 'new_m_lru_conv_b': 'new_m', 'new_m_lru_w_r': 'new_m', 'new_m_lru_b_r': 'new_m', 'new_m_lru_w_i': 'new_m', 'new_m_lru_b_i': 'new_m', 'new_m_lru_lambda': 'new_m', 'new_m_conv_w': 'new_m', 'new_m_conv_b': 'new_m', 'new_m_conv_norm_g': 'new_m', 'new_m_conv_norm_b': 'new_m', 'new_m_w_out': 'new_m', 'new_m_ln1_g': 'new_m', 'new_m_ln1_b': 'new_m', 'new_m_ffn_w_up': 'new_m', 'new_m_ffn_conv_w': 'new_m', 'new_m_ffn_conv_b': 'new_m', 'new_m_ffn_w_down': 'new_m', 'new_m_ln2_g': 'new_m', 'new_m_ln2_b': 'new_m', 'new_v_w_ada': 'new_v', 'new_v_b_ada': 'new_v', 'new_v_w_in': 'new_v', 'new_v_lru_conv_w': 'new_v', 'new_v_lru_conv_b': 'new_v', 'new_v_lru_w_r': 'new_v', 'new_v_lru_b_r': 'new_v', 'new_v_lru_w_i': 'new_v', 'new_v_lru_b_i': 'new_v', 'new_v_lru_lambda': 'new_v', 'new_v_conv_w': 'new_v', 'new_v_conv_b': 'new_v', 'new_v_conv_norm_g': 'new_v', 'new_v_conv_norm_b': 'new_v', 'new_v_w_out': 'new_v', 'new_v_ln1_g': 'new_v', 'new_v_ln1_b': 'new_v', 'new_v_ffn_w_up': 'new_v', 'new_v_ffn_conv_w': 'new_v', 'new_v_ffn_conv_b': 'new_v', 'new_v_ffn_w_down': 'new_v', 'new_v_ln2_g': 'new_v', 'new_v_ln2_b': 'new_v'}


def _forward(args):
    return _fwd_reference(*[args[k] for k in FWD_PARAMS])


def _output_shape():
    out = _jax.eval_shape(lambda: _forward(_fwd_setup_inputs(0)))
    return out.shape, out.dtype

N_MICROBATCH = 1
ADAM_LR = 0.001
ADAM_B1 = 0.9
ADAM_B2 = 0.999
ADAM_EPS = 1e-08
ADAM_WD = 0.01
ADAM_STEP = 10
PER_EXAMPLE_BATCH_AXIS = {'x': 0, 'c': 0, 'loss_target': 0}
SHARED_INPUTS = []
_WEIGHT_DTYPES = {'w_ada': _jnp.float32, 'b_ada': _jnp.float32, 'w_in': _jnp.float32, 'lru_conv_w': _jnp.float32, 'lru_conv_b': _jnp.float32, 'lru_w_r': _jnp.float32, 'lru_b_r': _jnp.float32, 'lru_w_i': _jnp.float32, 'lru_b_i': _jnp.float32, 'lru_lambda': _jnp.float32, 'conv_w': _jnp.float32, 'conv_b': _jnp.float32, 'conv_norm_g': _jnp.float32, 'conv_norm_b': _jnp.float32, 'w_out': _jnp.float32, 'ln1_g': _jnp.float32, 'ln1_b': _jnp.float32, 'ffn_w_up': _jnp.float32, 'ffn_conv_w': _jnp.float32, 'ffn_conv_b': _jnp.float32, 'ffn_w_down': _jnp.float32, 'ln2_g': _jnp.float32, 'ln2_b': _jnp.float32}
MOMENT_SCALE = {'w_ada': 1.496365e-01, 'b_ada': 2.308531e-01, 'w_in': 6.924285e-02, 'lru_conv_w': 8.678901e-02, 'lru_conv_b': 7.921437e-01, 'lru_w_r': 2.694608e-02, 'lru_b_r': 3.059444e-02, 'lru_w_i': 4.978445e-02, 'lru_b_i': 2.806240e-02, 'lru_lambda': 5.506505e-02, 'conv_w': 8.010879e-02, 'conv_b': 1.737862e-01, 'conv_norm_g': 9.406516e-02, 'conv_norm_b': 1.001419e-01, 'w_out': 1.388137e-01, 'ln1_g': 6.946662e-01, 'ln1_b': 3.736298e-01, 'ffn_w_up': 4.529591e-02, 'ffn_conv_w': 4.683146e-02, 'ffn_conv_b': 4.370144e-02, 'ffn_w_down': 1.243102e-01, 'ln2_g': 6.396809e+01, 'ln2_b': 5.174804e+00}


def _to_microbatches(a, axis):
    t = _jnp.moveaxis(a, axis, 0)
    t = t.reshape((N_MICROBATCH, t.shape[0] // N_MICROBATCH) + t.shape[1:])
    return _jnp.moveaxis(t, 1, axis + 1)


def setup_inputs(seed: int = 0) -> dict:
    inp = _fwd_setup_inputs(seed)
    key = _jax.random.fold_in(_jax.random.key(seed), 7919)
    shape, _ = _output_shape()
    out = dict(inp)
    out["loss_target"] = _jax.random.normal(_jax.random.fold_in(key, 0), shape, _jnp.float32)
    for i, name in enumerate(TWIN_WEIGHTS):
        w = inp[name].astype(_jnp.float32)
        if MOMENT_SCALE is None:
            s = _jnp.sqrt(_jnp.mean(_jnp.square(w)) + 1e-30)
        else:
            s = MOMENT_SCALE[name]
        km, kv = _jax.random.split(_jax.random.fold_in(key, i + 1))
        out[name] = w
        out["m_" + name] = s * _jax.random.normal(km, w.shape, _jnp.float32)
        out["v_" + name] = (s * s) * _jax.random.uniform(kv, w.shape, _jnp.float32, 0.5, 1.5)
    if N_MICROBATCH > 1:
        for name, axis in PER_EXAMPLE_BATCH_AXIS.items():
            out[name] = _to_microbatches(out[name], axis)
    return {'x': out['x'], 'c': out['c'], 'w_ada': out['w_ada'], 'b_ada': out['b_ada'], 'w_in': out['w_in'], 'lru_conv_w': out['lru_conv_w'], 'lru_conv_b': out['lru_conv_b'], 'lru_w_r': out['lru_w_r'], 'lru_b_r': out['lru_b_r'], 'lru_w_i': out['lru_w_i'], 'lru_b_i': out['lru_b_i'], 'lru_lambda': out['lru_lambda'], 'conv_w': out['conv_w'], 'conv_b': out['conv_b'], 'conv_norm_g': out['conv_norm_g'], 'conv_norm_b': out['conv_norm_b'], 'w_out': out['w_out'], 'ln1_g': out['ln1_g'], 'ln1_b': out['ln1_b'], 'ffn_w_up': out['ffn_w_up'], 'ffn_conv_w': out['ffn_conv_w'], 'ffn_conv_b': out['ffn_conv_b'], 'ffn_w_down': out['ffn_w_down'], 'ln2_g': out['ln2_g'], 'ln2_b': out['ln2_b'], 'loss_target': out['loss_target'], 'm_w_ada': out['m_w_ada'], 'm_b_ada': out['m_b_ada'], 'm_w_in': out['m_w_in'], 'm_lru_conv_w': out['m_lru_conv_w'], 'm_lru_conv_b': out['m_lru_conv_b'], 'm_lru_w_r': out['m_lru_w_r'], 'm_lru_b_r': out['m_lru_b_r'], 'm_lru_w_i': out['m_lru_w_i'], 'm_lru_b_i': out['m_lru_b_i'], 'm_lru_lambda': out['m_lru_lambda'], 'm_conv_w': out['m_conv_w'], 'm_conv_b': out['m_conv_b'], 'm_conv_norm_g': out['m_conv_norm_g'], 'm_conv_norm_b': out['m_conv_norm_b'], 'm_w_out': out['m_w_out'], 'm_ln1_g': out['m_ln1_g'], 'm_ln1_b': out['m_ln1_b'], 'm_ffn_w_up': out['m_ffn_w_up'], 'm_ffn_conv_w': out['m_ffn_conv_w'], 'm_ffn_conv_b': out['m_ffn_conv_b'], 'm_ffn_w_down': out['m_ffn_w_down'], 'm_ln2_g': out['m_ln2_g'], 'm_ln2_b': out['m_ln2_b'], 'v_w_ada': out['v_w_ada'], 'v_b_ada': out['v_b_ada'], 'v_w_in': out['v_w_in'], 'v_lru_conv_w': out['v_lru_conv_w'], 'v_lru_conv_b': out['v_lru_conv_b'], 'v_lru_w_r': out['v_lru_w_r'], 'v_lru_b_r': out['v_lru_b_r'], 'v_lru_w_i': out['v_lru_w_i'], 'v_lru_b_i': out['v_lru_b_i'], 'v_lru_lambda': out['v_lru_lambda'], 'v_conv_w': out['v_conv_w'], 'v_conv_b': out['v_conv_b'], 'v_conv_norm_g': out['v_conv_norm_g'], 'v_conv_norm_b': out['v_conv_norm_b'], 'v_w_out': out['v_w_out'], 'v_ln1_g': out['v_ln1_g'], 'v_ln1_b': out['v_ln1_b'], 'v_ffn_w_up': out['v_ffn_w_up'], 'v_ffn_conv_w': out['v_ffn_conv_w'], 'v_ffn_conv_b': out['v_ffn_conv_b'], 'v_ffn_w_down': out['v_ffn_w_down'], 'v_ln2_g': out['v_ln2_g'], 'v_ln2_b': out['v_ln2_b']}


def _loss(weights, diff, rest, loss_target):
    with _jax.named_scope("forward"):
        args = {**rest, TWIN_DIFF_INPUT: diff, **{k: w.astype(_WEIGHT_DTYPES[k]) for k, w in weights.items()}}
        y = _forward(args)
    with _jax.named_scope("loss_head"):
        err = _jnp.square(y.astype(_jnp.float32) - loss_target)
        return 0.5 * _jnp.sum(_jnp.mean(err, axis=-1)) if err.ndim else 0.5 * err


def _adamw(w, g, m, v):
    m = ADAM_B1 * m + (1.0 - ADAM_B1) * g
    v = ADAM_B2 * v + (1.0 - ADAM_B2) * _jnp.square(g)
    m_hat = m / (1.0 - ADAM_B1 ** ADAM_STEP)
    v_hat = v / (1.0 - ADAM_B2 ** ADAM_STEP)
    delta = -ADAM_LR * (m_hat / (_jnp.sqrt(v_hat) + ADAM_EPS) + ADAM_WD * w)
    return delta, m, v


def reference(x, c, w_ada, b_ada, w_in, lru_conv_w, lru_conv_b, lru_w_r, lru_b_r, lru_w_i, lru_b_i, lru_lambda, conv_w, conv_b, conv_norm_g, conv_norm_b, w_out, ln1_g, ln1_b, ffn_w_up, ffn_conv_w, ffn_conv_b, ffn_w_down, ln2_g, ln2_b, loss_target, m_w_ada, m_b_ada, m_w_in, m_lru_conv_w, m_lru_conv_b, m_lru_w_r, m_lru_b_r, m_lru_w_i, m_lru_b_i, m_lru_lambda, m_conv_w, m_conv_b, m_conv_norm_g, m_conv_norm_b, m_w_out, m_ln1_g, m_ln1_b, m_ffn_w_up, m_ffn_conv_w, m_ffn_conv_b, m_ffn_w_down, m_ln2_g, m_ln2_b, v_w_ada, v_b_ada, v_w_in, v_lru_conv_w, v_lru_conv_b, v_lru_w_r, v_lru_b_r, v_lru_w_i, v_lru_b_i, v_lru_lambda, v_conv_w, v_conv_b, v_conv_norm_g, v_conv_norm_b, v_w_out, v_ln1_g, v_ln1_b, v_ffn_w_up, v_ffn_conv_w, v_ffn_conv_b, v_ffn_w_down, v_ln2_g, v_ln2_b):
    given = dict(x=x, c=c, w_ada=w_ada, b_ada=b_ada, w_in=w_in, lru_conv_w=lru_conv_w, lru_conv_b=lru_conv_b, lru_w_r=lru_w_r, lru_b_r=lru_b_r, lru_w_i=lru_w_i, lru_b_i=lru_b_i, lru_lambda=lru_lambda, conv_w=conv_w, conv_b=conv_b, conv_norm_g=conv_norm_g, conv_norm_b=conv_norm_b, w_out=w_out, ln1_g=ln1_g, ln1_b=ln1_b, ffn_w_up=ffn_w_up, ffn_conv_w=ffn_conv_w, ffn_conv_b=ffn_conv_b, ffn_w_down=ffn_w_down, ln2_g=ln2_g, ln2_b=ln2_b, loss_target=loss_target, m_w_ada=m_w_ada, m_b_ada=m_b_ada, m_w_in=m_w_in, m_lru_conv_w=m_lru_conv_w, m_lru_conv_b=m_lru_conv_b, m_lru_w_r=m_lru_w_r, m_lru_b_r=m_lru_b_r, m_lru_w_i=m_lru_w_i, m_lru_b_i=m_lru_b_i, m_lru_lambda=m_lru_lambda, m_conv_w=m_conv_w, m_conv_b=m_conv_b, m_conv_norm_g=m_conv_norm_g, m_conv_norm_b=m_conv_norm_b, m_w_out=m_w_out, m_ln1_g=m_ln1_g, m_ln1_b=m_ln1_b, m_ffn_w_up=m_ffn_w_up, m_ffn_conv_w=m_ffn_conv_w, m_ffn_conv_b=m_ffn_conv_b, m_ffn_w_down=m_ffn_w_down, m_ln2_g=m_ln2_g, m_ln2_b=m_ln2_b, v_w_ada=v_w_ada, v_b_ada=v_b_ada, v_w_in=v_w_in, v_lru_conv_w=v_lru_conv_w, v_lru_conv_b=v_lru_conv_b, v_lru_w_r=v_lru_w_r, v_lru_b_r=v_lru_b_r, v_lru_w_i=v_lru_w_i, v_lru_b_i=v_lru_b_i, v_lru_lambda=v_lru_lambda, v_conv_w=v_conv_w, v_conv_b=v_conv_b, v_conv_norm_g=v_conv_norm_g, v_conv_norm_b=v_conv_norm_b, v_w_out=v_w_out, v_ln1_g=v_ln1_g, v_ln1_b=v_ln1_b, v_ffn_w_up=v_ffn_w_up, v_ffn_conv_w=v_ffn_conv_w, v_ffn_conv_b=v_ffn_conv_b, v_ffn_w_down=v_ffn_w_down, v_ln2_g=v_ln2_g, v_ln2_b=v_ln2_b)
    weights = {n: given[n] for n in TWIN_WEIGHTS}
    shared = {n: given[n] for n in SHARED_INPUTS}
    per_example = {n: given[n] for n in ['x', 'c']}
    grad_fn = _jax.value_and_grad(_loss, argnums=(0, 1))

    def one_microbatch(ex, loss_target):
        ex = dict(ex)
        diff = ex.pop(TWIN_DIFF_INPUT)
        return grad_fn(weights, diff, {**shared, **ex}, loss_target)

    if N_MICROBATCH == 1:
        loss, (grad_w, grad_x) = one_microbatch(per_example, given["loss_target"])
    else:
        def body(carry, xs):
            loss_sum, grad_sum = carry
            l_k, (gw_k, gx_k) = one_microbatch(xs[0], xs[1])
            with _jax.named_scope("update"):
                return (loss_sum + l_k, _jax.tree.map(_jnp.add, grad_sum, gw_k)), gx_k

        init = (_jnp.zeros((), _jnp.float32), _jax.tree.map(_jnp.zeros_like, weights))
        (loss, grad_w), grad_x = _jax.lax.scan(body, init, (per_example, given["loss_target"]))
    with _jax.named_scope("update"):
        delta_w, new_m, new_v = {}, {}, {}
        for n in TWIN_WEIGHTS:
            delta_w[n], new_m[n], new_v[n] = _adamw(weights[n], grad_w[n], given["m_" + n], given["v_" + n])
    return (loss, grad_x, *[grad_w[n] for n in TWIN_WEIGHTS], *[delta_w[n] for n in TWIN_WEIGHTS],
            *[new_m[n] for n in TWIN_WEIGHTS], *[new_v[n] for n in TWIN_WEIGHTS])
```

```python
import functools
import itertools
import math

import jax
import jax.numpy as jnp
from jax import lax
from jax.experimental import pallas as pl
from jax.experimental.pallas import tpu as pltpu

_MXU_DT = jnp.bfloat16
_F32 = jnp.float32
_VMEM_LIMIT = 56 * 1024 * 1024
_TT_MIX = 256
_TT_FFN = 256
_TK_WGRAD = 512
_HALO = 32

_LRU_C = 8.0
_LN_EPS = 1e-5
_N_HEADS = 8
_DEPTH = 1
_ALPHA = (2 * _DEPTH) ** 0.25
_ADAM_LR, _ADAM_B1, _ADAM_B2, _ADAM_EPS, _ADAM_WD, _ADAM_STEP = 0.001, 0.9, 0.999, 1e-08, 0.01, 10

_MESH = pl.DeviceIdType.MESH
_CHIP_DELTAS = ((1, 0), (0, 1), (1, 1))


def _cparams(sem):
    return pltpu.CompilerParams(dimension_semantics=sem, vmem_limit_bytes=_VMEM_LIMIT)


def _resident(shape):
    nd = len(shape)
    return pl.BlockSpec(shape, lambda *_: (0,) * nd, pipeline_mode=pl.Buffered(1))


def _dot(a, b):
    return jnp.dot(a, b, preferred_element_type=_F32)


def _dot_nt(a, b):
    return lax.dot_general(a, b, (((1,), (1,)), ((), ())), preferred_element_type=_F32)


def _dot_tn(a, b):
    return lax.dot_general(a, b, (((0,), (0,)), ((), ())), preferred_element_type=_F32)


def _mx(v):
    return v.astype(_MXU_DT)


def _expm1(v):
    series = v * (1.0 + v * (1.0 / 2 + v * (1.0 / 6 + v * (1.0 / 24 + v * (1.0 / 120 + v * (1.0 / 720))))))
    return jnp.where(jnp.abs(v) < 0.1, series, jnp.exp(v) - 1.0)


def _softplus(z):
    e = jnp.exp(-jnp.abs(z))
    u = 1.0 + e
    log1p = jnp.where(u == 1.0, e, jnp.log(u) * e / jnp.where(u == 1.0, 1.0, u - 1.0))
    return jnp.maximum(z, 0.0) + log1p


_GELU_C = math.sqrt(2.0 / math.pi)


def _gelu_and_grad(v):
    t = jnp.tanh(_GELU_C * (v + 0.044715 * v * v * v))
    val = 0.5 * v * (1.0 + t)
    grad = 0.5 * (1.0 + t) + 0.5 * v * (1.0 - t * t) * _GELU_C * (1.0 + 3 * 0.044715 * v * v)
    return val, grad


def _seg_sum(v, seg):
    hi = v.astype(jnp.bfloat16)
    r1 = v - hi.astype(_F32)
    mid = r1.astype(jnp.bfloat16)
    lo = (r1 - mid.astype(_F32)).astype(jnp.bfloat16)
    return _dot(hi, seg) + _dot(mid, seg) + _dot(lo, seg)


def _scan_fwd(a, u, h0):
    n = a.shape[0]
    row = lax.broadcasted_iota(jnp.int32, a.shape, 0)
    h, d = u, 1
    while d < n:
        keep = row >= d
        h = a * jnp.where(keep, pltpu.roll(h, d, 0), 0.0) + h
        a = a * jnp.where(keep, pltpu.roll(a, d, 0), 1.0)
        d *= 2
    return h + a * h0


def _scan_rev(c, g, g_end):
    n = c.shape[0]
    row = lax.broadcasted_iota(jnp.int32, c.shape, 0)
    d = 1
    while d < n:
        keep = row < n - d
        g = c * jnp.where(keep, pltpu.roll(g, n - d, 0), 0.0) + g
        c = c * jnp.where(keep, pltpu.roll(c, n - d, 0), 1.0)
        d *= 2
    return g + c * g_end


def _layer_norm_stats(z):
    mu = jnp.mean(z, axis=-1, keepdims=True)
    zc = z - mu
    var = jnp.mean(zc * zc, axis=-1, keepdims=True)
    rstd = lax.rsqrt(var + _LN_EPS)
    return zc * rstd, rstd


def _layer_norm_bwd(dn, n, rstd):
    return rstd * (dn - jnp.mean(dn, axis=-1, keepdims=True) - n * jnp.mean(dn * n, axis=-1, keepdims=True))


def _rowsum(v):
    return jnp.sum(v, axis=0, keepdims=True)


def _lru_gates(xc, wr_ref, wi_ref, br_ref, bi_ref, lam_ref):
    xcb = _mx(xc)
    r = jax.nn.sigmoid(_dot(xcb, wr_ref[...]) + br_ref[...])
    i = jax.nn.sigmoid(_dot(xcb, wi_ref[...]) + bi_ref[...])
    sp = _softplus(-lam_ref[...])
    log_a = -_LRU_C * r * sp
    a = jnp.exp(log_a)
    mult = jnp.sqrt(-_expm1(2.0 * log_a))
    return r, i, sp, a, mult


def _conv_taps(ext_ref, w_ref, first, n_taps, tt):
    acc = w_ref[0:1, :] * ext_ref[pl.ds(first, tt), :]
    for k in range(1, n_taps):
        acc = acc + w_ref[k:k + 1, :] * ext_ref[pl.ds(first + k, tt), :]
    return acc


def _mix_fwd(x, mod3, win, lcw, lcb, wr_bd, wi_bd, b_r, b_i, lam, cw, cb, ng, nb, seg, wout, ln1g, ln1b):
    bl, s_len, d = x.shape
    w = d // 2
    tt = min(_TT_MIX, s_len)
    ns = s_len // tt
    kc = cw.shape[0]

    def body(x_ref, mod_ref, win_ref, lcw_ref, lcb_ref, wr_ref, wi_ref, br_ref, bi_ref, lam_ref, cw_ref, cb_ref,
             ng_ref, nb_ref, seg_ref, wout_ref, g1_ref, b1_ref,
             proj_ref, h_ref, mix_ref, x1_ref, u1_ref, y_ref, ext4, ext31, hcar):
        @pl.when(pl.program_id(1) == 0)
        def _():
            ext4[0:8, :] = jnp.zeros((8, w), _F32)
            ext31[0:_HALO, :] = jnp.zeros((_HALO, w), _F32)
            hcar[...] = jnp.zeros_like(hcar)

        xt = x_ref[...]
        sh1, sc1, gt1 = mod_ref[:, 0:d], mod_ref[:, d:2 * d], mod_ref[:, 2 * d:3 * d]
        u1 = _mx(xt * (1.0 + sc1) + sh1)
        u1_ref[...] = u1
        xa, ga, vb, gb = (_dot(u1, win_ref[k]) for k in range(4))
        proj_ref[:, 0:w] = xa
        proj_ref[:, w:2 * w] = ga
        proj_ref[:, 2 * w:3 * w] = vb
        proj_ref[:, 3 * w:4 * w] = gb

        ext4[8:8 + tt, :] = xa
        xc = lcb_ref[...] + _conv_taps(ext4, lcw_ref, 5, 4, tt)
        ext4[0:8, :] = xa[tt - 8:tt, :]
        r, i, sp, a, mult = _lru_gates(xc, wr_ref, wi_ref, br_ref, bi_ref, lam_ref)
        h = _scan_fwd(a, mult * (i * xc), hcar[0:1, :])
        hcar[0:1, :] = h[tt - 1:tt, :]
        h_ref[...] = h
        gelu, _ = _gelu_and_grad(ga)
        y_ref[:, 0:w] = _mx(gelu * h)

        vbg = vb * jax.nn.sigmoid(gb)
        ext31[_HALO:_HALO + tt, :] = vbg
        vbc = cb_ref[...] + _conv_taps(ext31, cw_ref, _HALO - (kc - 1), kc, tt)
        ext31[0:_HALO, :] = vbg[tt - _HALO:tt, :]
        inv = 1.0 / (w // _N_HEADS)
        zc = vbc - _seg_sum(vbc, seg_ref[...]) * inv
        n = zc * lax.rsqrt(_seg_sum(zc * zc, seg_ref[...]) * inv + _LN_EPS)
        pre = n * ng_ref[...] + nb_ref[...]
        y_ref[:, w:2 * w] = _mx(pre * jax.nn.sigmoid(pre))

        mix = _dot(y_ref[...], wout_ref[...])
        mix_ref[...] = mix
        n1, _ = _layer_norm_stats(_ALPHA * xt + (1.0 + gt1) * mix)
        x1_ref[...] = n1 * g1_ref[...] + b1_ref[...]

    tok = lambda c: pl.BlockSpec((None, tt, c), lambda b, s: (b, s, 0))
    smalls = [lcw, lcb, wr_bd, wi_bd, b_r, b_i, lam, cw, cb, ng, nb, seg, wout, ln1g, ln1b]
    return pl.pallas_call(
        body, grid=(bl, ns),
        in_specs=[tok(d), pl.BlockSpec((None, 1, 6 * d), lambda b, s: (b, 0, 0)), _resident(win.shape)]
        + [_resident(t.shape) for t in smalls],
        out_specs=[tok(4 * w), tok(w), tok(d), tok(d), tok(d), tok(d)],
        out_shape=[jax.ShapeDtypeStruct((bl, s_len, 4 * w), _F32), jax.ShapeDtypeStruct((bl, s_len, w), _F32),
                   jax.ShapeDtypeStruct((bl, s_len, d), _F32), jax.ShapeDtypeStruct((bl, s_len, d), _F32),
                   jax.ShapeDtypeStruct((bl, s_len, d), _MXU_DT), jax.ShapeDtypeStruct((bl, s_len, d), _MXU_DT)],
        scratch_shapes=[pltpu.VMEM((tt + 8, w), _F32), pltpu.VMEM((tt + _HALO, w), _F32), pltpu.VMEM((8, w), _F32)],
        compiler_params=_cparams(("arbitrary", "arbitrary")), name="mix_fwd",
    )(x, mod3, win, *smalls)


def _ffn_fwd(x1, mod3, wup, fcw, fcb, wdn, ln2g, ln2b, target):
    bl, s_len, d = x1.shape
    nch, _, fc = wup.shape
    nch //= 2
    f = nch * fc
    tt = min(_TT_FFN, s_len)
    ns = s_len // tt

    def body(x1_ref, mod_ref, wup_ref, fcw_ref, fcb_ref, wdn_ref, g2_ref, b2_ref, tgt_ref,
             u2_ref, hh_ref, f_ref, dz2_ref, loss_ref, dln2_ref, dgt2_ref, ext3):
        first_tile = pl.program_id(1) == 0

        @pl.when(first_tile)
        def _():
            ext3[:, 0:8, :] = jnp.zeros((nch, 8, fc), _F32)
            dgt2_ref[...] = jnp.zeros_like(dgt2_ref)

        @pl.when(first_tile & (pl.program_id(0) == 0))
        def _():
            loss_ref[...] = jnp.zeros_like(loss_ref)
            dln2_ref[...] = jnp.zeros_like(dln2_ref)

        x1t = x1_ref[...]
        sh2, sc2, gt2 = mod_ref[:, 3 * d:4 * d], mod_ref[:, 4 * d:5 * d], mod_ref[:, 5 * d:6 * d]
        u2 = _mx(x1t * (1.0 + sc2) + sh2)
        u2_ref[...] = u2
        y2 = jnp.zeros((tt, d), _F32)
        for j in range(nch):
            lanes = slice(j * fc, (j + 1) * fc)
            v = _dot(u2, wup_ref[j])
            g = _dot(u2, wup_ref[nch + j])
            hh_ref[:, lanes] = v.astype(hh_ref.dtype)
            hh_ref[:, f + j * fc:f + (j + 1) * fc] = g.astype(hh_ref.dtype)
            ext = ext3.at[j]
            ext[8:8 + tt, :] = g
            gc = fcb_ref[:, lanes] + sum(fcw_ref[k:k + 1, lanes] * ext[pl.ds(6 + k, tt), :] for k in range(3))
            ext[0:8, :] = g[tt - 8:tt, :]
            fj = _mx(gc * jax.nn.sigmoid(gc) * v)
            f_ref[:, lanes] = fj
            y2 = y2 + _dot(fj, wdn_ref[lanes, :])

        n2, rstd = _layer_norm_stats(_ALPHA * x1t + (1.0 + gt2) * y2)
        err = n2 * g2_ref[...] + b2_ref[...] - tgt_ref[...]
        loss_ref[...] += jnp.sum(_rowsum(err * err), axis=1, keepdims=True)
        dout = err * (1.0 / d)
        dln2_ref[0:1, :] += _rowsum(dout * n2)
        dln2_ref[1:2, :] += _rowsum(dout)
        dz2 = _layer_norm_bwd(dout * g2_ref[...], n2, rstd)
        dz2_ref[...] = dz2
        dgt2_ref[...] += _rowsum(dz2 * y2)

    tok = lambda c: pl.BlockSpec((None, tt, c), lambda b, s: (b, s, 0))
    acc = lambda r: pl.BlockSpec((r, d), lambda b, s: (0, 0))
    smalls = [fcw, fcb, wdn, ln2g, ln2b]
    return pl.pallas_call(
        body, grid=(bl, ns),
        in_specs=[tok(d), pl.BlockSpec((None, 1, 6 * d), lambda b, s: (b, 0, 0)), _resident(wup.shape)]
        + [_resident(t.shape) for t in smalls] + [tok(d)],
        out_specs=[tok(d), tok(2 * f), tok(f), tok(d), acc(1), acc(2), pl.BlockSpec((None, 1, d), lambda b, s: (b, 0, 0))],
        out_shape=[jax.ShapeDtypeStruct((bl, s_len, d), _MXU_DT), jax.ShapeDtypeStruct((bl, s_len, 2 * f), _F32),
                   jax.ShapeDtypeStruct((bl, s_len, f), _MXU_DT), jax.ShapeDtypeStruct((bl, s_len, d), _F32),
                   jax.ShapeDtypeStruct((1, d), _F32), jax.ShapeDtypeStruct((2, d), _F32),
                   jax.ShapeDtypeStruct((bl, 1, d), _F32)],
        scratch_shapes=[pltpu.VMEM((nch, tt + 8, fc), _F32)],
        compiler_params=_cparams(("arbitrary", "arbitrary")), name="ffn_fwd",
    )(x1, mod3, wup, *smalls, target)


def _ffn_bwd(dz2, x1, hh, mod3, wup, wdn, fcw, fcb):
    bl, s_len, d = x1.shape
    nch, _, fc = wup.shape
    nch //= 2
    f = nch * fc
    tt = min(_TT_FFN, s_len)
    ns = s_len // tt

    def body(dz2_ref, x1_ref, hh_ref, halo_ref, mod_ref, wup_ref, wdn_ref, fcw_ref, fcb_ref,
             dx1_ref, dy2_ref, dh_ref, dfc_ref, dmod_ref, gext, dext, dcar):
        s = ns - 1 - pl.program_id(1)

        @pl.when(pl.program_id(1) == 0)
        def _():
            dcar[...] = jnp.zeros_like(dcar)
            dmod_ref[...] = jnp.zeros_like(dmod_ref)

        @pl.when((pl.program_id(1) == 0) & (pl.program_id(0) == 0))
        def _():
            dfc_ref[...] = jnp.zeros_like(dfc_ref)

        sc2, gt2 = mod_ref[:, 4 * d:5 * d], mod_ref[:, 5 * d:6 * d]
        dz2t = dz2_ref[...]
        dy2 = _mx((1.0 + gt2) * dz2t)
        dy2_ref[...] = dy2
        du2 = jnp.zeros((tt, d), _F32)
        for j in range(nch):
            lanes = slice(j * fc, (j + 1) * fc)
            glanes = slice(f + j * fc, f + (j + 1) * fc)
            v = hh_ref[:, lanes].astype(_F32)
            g = hh_ref[:, glanes].astype(_F32)
            gext[0:8, :] = jnp.where(s == 0, 0.0, halo_ref[:, glanes].astype(_F32))
            gext[8:8 + tt, :] = g
            gc = fcb_ref[:, lanes] + sum(fcw_ref[k:k + 1, lanes] * gext[pl.ds(6 + k, tt), :] for k in range(3))
            sg = jax.nn.sigmoid(gc)
            df = _dot_nt(dy2, wdn_ref[lanes, :])
            dv = df * (gc * sg)
            dgc = df * v * (sg * (1.0 + gc * (1.0 - sg)))
            dfc_ref[3:4, lanes] += _rowsum(dgc)
            dext[0:tt, :] = dgc
            dext[tt:tt + 8, :] = dcar[j]
            dcar[j] = dgc[0:8, :]
            dg = jnp.zeros((tt, fc), _F32)
            for k in range(3):
                dg = dg + fcw_ref[k:k + 1, lanes] * dext[pl.ds(2 - k, tt), :]
                dfc_ref[k:k + 1, lanes] += _rowsum(dgc * gext[pl.ds(6 + k, tt), :])
            dvb, dgb = _mx(dv), _mx(dg)
            dh_ref[:, lanes] = dvb
            dh_ref[:, glanes] = dgb
            du2 = du2 + _dot_nt(dvb, wup_ref[j]) + _dot_nt(dgb, wup_ref[nch + j])

        dx1_ref[...] = _ALPHA * dz2t + du2 * (1.0 + sc2)
        dmod_ref[0:1, :] += _rowsum(du2)
        dmod_ref[1:2, :] += _rowsum(du2 * x1_ref[...])

    tok = lambda c: pl.BlockSpec((None, tt, c), lambda b, i: (b, ns - 1 - i, 0))
    halo = pl.BlockSpec((None, 8, 2 * f), lambda b, i: (b, jnp.maximum((ns - 1 - i) * (tt // 8) - 1, 0), 0))
    return pl.pallas_call(
        body, grid=(bl, ns),
        in_specs=[tok(d), tok(d), tok(2 * f), halo, pl.BlockSpec((None, 1, 6 * d), lambda b, i: (b, 0, 0)),
                  _resident(wup.shape), _resident(wdn.shape), _resident(fcw.shape), _resident(fcb.shape)],
        out_specs=[tok(d), tok(d), tok(2 * f), pl.BlockSpec((4, f), lambda b, i: (0, 0)),
                   pl.BlockSpec((None, 2, d), lambda b, i: (b, 0, 0))],
        out_shape=[jax.ShapeDtypeStruct((bl, s_len, d), _F32), jax.ShapeDtypeStruct((bl, s_len, d), _MXU_DT),
                   jax.ShapeDtypeStruct((bl, s_len, 2 * f), _MXU_DT), jax.ShapeDtypeStruct((4, f), _F32),
                   jax.ShapeDtypeStruct((bl, 2, d), _F32)],
        scratch_shapes=[pltpu.VMEM((tt + 8, fc), _F32), pltpu.VMEM((tt + 8, fc), _F32), pltpu.VMEM((nch, 8, fc), _F32)],
        compiler_params=_cparams(("arbitrary", "arbitrary")), name="ffn_bwd",
    )(dz2, x1, hh, hh, mod3, wup, wdn, fcw, fcb)


def _mix_bwd(dx1, x, mix, proj, h, mod3, win, lcw, lcb, wr_bd, wi_bd, b_r, b_i, lam, cw, cb, ng, nb, seg, wout, ln1g):
    bl, s_len, d = x.shape
    w = d // 2
    tt = min(_TT_MIX, s_len)
    ns = s_len // tt
    kc = cw.shape[0]

    def body(dx1_ref, x_ref, mix_ref, proj_ref, phalo_ref, h_ref, hhalo_ref, mod_ref, win_ref, lcw_ref, lcb_ref,
             wr_ref, wi_ref, br_ref, bi_ref, lam_ref, cw_ref, cb_ref, ng_ref, nb_ref, seg_ref, wout_ref, g1_ref,
             gx_ref, dproj_ref, dmix_ref, xcg_ref, vecw_ref, dlcw_ref, dcw_ref, dln1_ref, dmod_ref,
             ext4, ext31, dext4, dext31, car4, car31, gcar):
        s = ns - 1 - pl.program_id(1)
        first = s == 0

        @pl.when(pl.program_id(1) == 0)
        def _():
            car4[...] = jnp.zeros_like(car4)
            car31[...] = jnp.zeros_like(car31)
            gcar[...] = jnp.zeros_like(gcar)
            dmod_ref[...] = jnp.zeros_like(dmod_ref)

        @pl.when((pl.program_id(1) == 0) & (pl.program_id(0) == 0))
        def _():
            for ref in (vecw_ref, dlcw_ref, dcw_ref, dln1_ref):
                ref[...] = jnp.zeros_like(ref)

        xt, mixt = x_ref[...], mix_ref[...]
        sh1, sc1, gt1 = mod_ref[:, 0:d], mod_ref[:, d:2 * d], mod_ref[:, 2 * d:3 * d]

        n1, rstd1 = _layer_norm_stats(_ALPHA * xt + (1.0 + gt1) * mixt)
        dx1t = dx1_ref[...]
        dln1_ref[0:1, :] += _rowsum(dx1t * n1)
        dln1_ref[1:2, :] += _rowsum(dx1t)
        dz1 = _layer_norm_bwd(dx1t * g1_ref[...], n1, rstd1)
        dmod_ref[2:3, :] += _rowsum(dz1 * mixt)
        dmix = _mx((1.0 + gt1) * dz1)
        dmix_ref[...] = dmix
        dya = _dot_nt(dmix, wout_ref[0:w, :])
        dyb = _dot_nt(dmix, wout_ref[w:2 * w, :])

        xa, ga = proj_ref[:, 0:w], proj_ref[:, w:2 * w]
        vb, gb = proj_ref[:, 2 * w:3 * w], proj_ref[:, 3 * w:4 * w]

        sgb = jax.nn.sigmoid(gb)
        vbg = vb * sgb
        hv, hg = phalo_ref[:, 2 * w:3 * w], phalo_ref[:, 3 * w:4 * w]
        ext31[0:_HALO, :] = jnp.where(first, 0.0, hv * jax.nn.sigmoid(hg))
        ext31[_HALO:_HALO + tt, :] = vbg
        vbc = cb_ref[...] + _conv_taps(ext31, cw_ref, _HALO - (kc - 1), kc, tt)
        inv = 1.0 / (w // _N_HEADS)
        zc = vbc - _seg_sum(vbc, seg_ref[...]) * inv
        rstd = lax.rsqrt(_seg_sum(zc * zc, seg_ref[...]) * inv + _LN_EPS)
        n = zc * rstd
        pre = n * ng_ref[...] + nb_ref[...]
        sgp = jax.nn.sigmoid(pre)
        dpre = dyb * (sgp * (1.0 + pre * (1.0 - sgp)))
        vecw_ref[5:6, :] += _rowsum(dpre * n)
        vecw_ref[6:7, :] += _rowsum(dpre)
        dn = dpre * ng_ref[...]
        dvbc = rstd * (dn - _seg_sum(dn, seg_ref[...]) * inv - n * (_seg_sum(dn * n, seg_ref[...]) * inv))
        vecw_ref[4:5, :] += _rowsum(dvbc)
        dext31[0:tt, :] = dvbc
        dext31[tt:tt + _HALO, :] = car31[...]
        car31[...] = dvbc[0:_HALO, :]
        dvbg = jnp.zeros((tt, w), _F32)
        for k in range(kc):
            dvbg = dvbg + cw_ref[k:k + 1, :] * dext31[pl.ds(kc - 1 - k, tt), :]
            dcw_ref[k:k + 1, :] += _rowsum(dvbc * ext31[pl.ds(_HALO - (kc - 1) + k, tt), :])
        dproj_ref[:, 2 * w:3 * w] = _mx(dvbg * sgb)
        dproj_ref[:, 3 * w:4 * w] = _mx(dvbg * vb * (sgb * (1.0 - sgb)))

        ext4[0:8, :] = jnp.where(first, 0.0, phalo_ref[_HALO - 8:_HALO, 0:w])
        ext4[8:8 + tt, :] = xa
        xc = lcb_ref[...] + _conv_taps(ext4, lcw_ref, 5, 4, tt)
        xcg_ref[:, 0:w] = _mx(xc)
        r, i, sp, a, mult = _lru_gates(xc, wr_ref, wi_ref, br_ref, bi_ref, lam_ref)
        ht = h_ref[...]
        row = lax.broadcasted_iota(jnp.int32, (tt, w), 0)
        h_before = jnp.where(first, 0.0, hhalo_ref[7:8, :])
        hprev = jnp.where(row == 0, h_before, pltpu.roll(ht, 1, 0))
        gelu, dgelu = _gelu_and_grad(ga)
        dproj_ref[:, w:2 * w] = _mx(dya * ht * dgelu)
        dh = dya * gelu
        coef = jnp.where(row == tt - 1, 1.0, pltpu.roll(a, tt - 1, 0))
        big_g = _scan_rev(coef, dh, gcar[0:1, :])
        gcar[0:1, :] = a[0:1, :] * big_g[0:1, :]
        da = big_g * hprev
        ixc = i * xc
        dlog_a = da * a - (big_g * ixc) * (a * a / mult)
        di = big_g * mult * xc
        dxc = big_g * mult * i
        vecw_ref[3:4, :] += _rowsum(dlog_a * r) * (_LRU_C * jax.nn.sigmoid(-lam_ref[...]))
        dgr_f = dlog_a * (-_LRU_C * sp) * (r * (1.0 - r))
        dgi_f = di * (i * (1.0 - i))
        vecw_ref[1:2, :] += _rowsum(dgr_f)
        vecw_ref[2:3, :] += _rowsum(dgi_f)
        dgr, dgi = _mx(dgr_f), _mx(dgi_f)
        xcg_ref[:, w:2 * w] = dgr
        xcg_ref[:, 2 * w:3 * w] = dgi
        dxc = dxc + _dot_nt(dgr, wr_ref[...]) + _dot_nt(dgi, wi_ref[...])
        vecw_ref[0:1, :] += _rowsum(dxc)
        dext4[0:tt, :] = dxc
        dext4[tt:tt + 8, :] = car4[...]
        car4[...] = dxc[0:8, :]
        dxa = jnp.zeros((tt, w), _F32)
        for k in range(4):
            dxa = dxa + lcw_ref[k:k + 1, :] * dext4[pl.ds(3 - k, tt), :]
            dlcw_ref[k:k + 1, :] += _rowsum(dxc * ext4[pl.ds(5 + k, tt), :])
        dproj_ref[:, 0:w] = _mx(dxa)

        du1 = sum(_dot_nt(dproj_ref[:, k * w:(k + 1) * w], win_ref[k]) for k in range(4))
        gx_ref[...] = _ALPHA * dz1 + du1 * (1.0 + sc1)
        dmod_ref[0:1, :] += _rowsum(du1)
        dmod_ref[1:2, :] += _rowsum(du1 * xt)

    tok = lambda c: pl.BlockSpec((None, tt, c), lambda b, i: (b, ns - 1 - i, 0))
    halo = lambda rows, c: pl.BlockSpec(
        (None, rows, c), lambda b, i: (b, jnp.maximum((ns - 1 - i) * (tt // rows) - 1, 0), 0))
    accw = lambda r, c: pl.BlockSpec((r, c), lambda b, i: (0, 0))
    smalls = [lcw, lcb, wr_bd, wi_bd, b_r, b_i, lam, cw, cb, ng, nb, seg, wout, ln1g]
    return pl.pallas_call(
        body, grid=(bl, ns),
        in_specs=[tok(d), tok(d), tok(d), tok(4 * w), halo(_HALO, 4 * w), tok(w), halo(8, w),
                  pl.BlockSpec((None, 1, 6 * d), lambda b, i: (b, 0, 0)), _resident(win.shape)]
        + [_resident(t.shape) for t in smalls],
        out_specs=[tok(d), tok(4 * w), tok(d), tok(3 * w), accw(8, w), accw(4, w), accw(kc, w), accw(2, d),
                   pl.BlockSpec((None, 3, d), lambda b, i: (b, 0, 0))],
        out_shape=[jax.ShapeDtypeStruct((bl, s_len, d), _F32), jax.ShapeDtypeStruct((bl, s_len, 4 * w), _MXU_DT),
                   jax.ShapeDtypeStruct((bl, s_len, d), _MXU_DT), jax.ShapeDtypeStruct((bl, s_len, 3 * w), _MXU_DT),
                   jax.ShapeDtypeStruct((8, w), _F32), jax.ShapeDtypeStruct((4, w), _F32),
                   jax.ShapeDtypeStruct((kc, w), _F32), jax.ShapeDtypeStruct((2, d), _F32),
                   jax.ShapeDtypeStruct((bl, 3, d), _F32)],
        scratch_shapes=[pltpu.VMEM((tt + 8, w), _F32), pltpu.VMEM((tt + _HALO, w), _F32),
                        pltpu.VMEM((tt + 8, w), _F32), pltpu.VMEM((tt + _HALO, w), _F32),
                        pltpu.VMEM((8, w), _F32), pltpu.VMEM((_HALO, w), _F32), pltpu.VMEM((8, w), _F32)],
        compiler_params=_cparams(("arbitrary", "arbitrary")), name="mix_bwd",
    )(dx1, x, mix, proj, proj, h, h, mod3, win, *smalls)


def _wgrad(a, b, ma, nbw, na, nb, a_off, b_off, name):
    t = a.shape[0]
    tk = min(_TK_WGRAD, t)

    def body(a_ref, b_ref, o_ref):
        @pl.when(pl.program_id(1) == 0)
        def _():
            o_ref[...] = jnp.zeros_like(o_ref)
        o_ref[...] += _dot_tn(a_ref[...], b_ref[...])

    return pl.pallas_call(
        body, grid=(na * nb, t // tk),
        in_specs=[pl.BlockSpec((tk, ma), lambda j, k: (k, j // nb + a_off)),
                  pl.BlockSpec((tk, nbw), lambda j, k: (k, j % nb + b_off))],
        out_specs=pl.BlockSpec((None, ma, nbw), lambda j, k: (j, 0, 0)),
        out_shape=jax.ShapeDtypeStruct((na * nb, ma, nbw), _F32),
        compiler_params=_cparams(("arbitrary", "arbitrary")), name=name,
    )(a, b)


_DEV_DELTAS = tuple(dl for dl in itertools.product((0, 1), repeat=3) if any(dl))
_HBM = pl.BlockSpec(memory_space=pltpu.HBM)
_VMEM = pl.BlockSpec(memory_space=pltpu.VMEM)


def _pos():
    return lax.axis_index("x"), lax.axis_index("y"), lax.axis_index("c")


def _flip(v, delta):
    return 1 - v if delta else v


def _remote(src, dst, ssem, rsem, dev):
    return pltpu.make_async_remote_copy(src_ref=src, dst_ref=dst, send_sem=ssem, recv_sem=rsem,
                                        device_id=dev, device_id_type=_MESH)


def _rows(ref, idx, n):
    return ref.at[pl.ds(pl.multiple_of(idx * n, 8), n)]


def _ada_fwd(c8, w_ada_k, b_ada_k):
    rows, d = c8.shape
    nk = w_ada_k.shape[1]

    def body(c_ref, w_ref, b_ref, call_ref, mod_ref, modloc, modrcv, s1, r1, s2, r2):
        xi, yi, ci = _pos()
        me, kme = 4 * xi + 2 * yi + ci, 2 * xi + yi
        call_ref[pl.ds(pl.multiple_of(me * rows, 8), rows), :] = c_ref[...]
        sends = []
        for p, (dx, dy, dc) in enumerate(_DEV_DELTAS):
            cp = _remote(c_ref, _rows(call_ref, me, rows), s1.at[p], r1.at[p], (_flip(xi, dx), _flip(yi, dy), _flip(ci, dc)))
            cp.start()
            sends.append(cp)
        for p, (dx, dy, dc) in enumerate(_DEV_DELTAS):
            src = 4 * _flip(xi, dx) + 2 * _flip(yi, dy) + _flip(ci, dc)
            _remote(c_ref, _rows(call_ref, src, rows), s1.at[p], r1.at[p], (xi, yi, ci)).wait_recv()
        for cp in sends:
            cp.wait_send()

        ca = call_ref[...]
        modloc[...] = _dot(_mx(ca * jax.nn.sigmoid(ca)), _mx(w_ref[...])) + b_ref[...]
        modrcv[kme] = modloc[pl.ds(pl.multiple_of(me * rows, 8), rows), :]
        sends = []
        for j, (dx, dy) in enumerate(_CHIP_DELTAS):
            tx, ty = _flip(xi, dx), _flip(yi, dy)
            cp = _remote(_rows(modloc, 4 * tx + 2 * ty + ci, rows), modrcv.at[kme], s2.at[j], r2.at[j], (tx, ty, ci))
            cp.start()
            sends.append(cp)
        for j, (dx, dy) in enumerate(_CHIP_DELTAS):
            ksrc = 2 * _flip(xi, dx) + _flip(yi, dy)
            _remote(_rows(modloc, me, rows), modrcv.at[ksrc], s2.at[j], r2.at[j], (xi, yi, ci)).wait_recv()
        for cp in sends:
            cp.wait_send()
        for j in range(4):
            mod_ref[:, j * nk:(j + 1) * nk] = modrcv[j]

    return pl.pallas_call(
        body, in_specs=[_VMEM, _VMEM, _VMEM], out_specs=[_VMEM, _VMEM],
        out_shape=[jax.ShapeDtypeStruct((8 * rows, d), _F32), jax.ShapeDtypeStruct((rows, 4 * nk), _F32)],
        scratch_shapes=[pltpu.VMEM((8 * rows, nk), _F32), pltpu.VMEM((4, rows, nk), _F32),
                        pltpu.SemaphoreType.DMA((7,)), pltpu.SemaphoreType.DMA((7,)),
                        pltpu.SemaphoreType.DMA((3,)), pltpu.SemaphoreType.DMA((3,))],
        compiler_params=pltpu.CompilerParams(vmem_limit_bytes=_VMEM_LIMIT), name="ada_fwd",
    )(c8, w_ada_k, b_ada_k)


def _wgather(arrs):
    n = len(arrs)

    def body(*refs):
        ins, outs = refs[:n], refs[n:2 * n]
        ssem, rsem, lsem = refs[2 * n:]
        xi, yi, ci = _pos()
        kme = 2 * xi + yi
        copies = [pltpu.make_async_copy(ins[a], outs[a].at[kme], lsem.at[a]) for a in range(n)]
        for j, (dx, dy) in enumerate(_CHIP_DELTAS):
            peer = (_flip(xi, dx), _flip(yi, dy), ci)
            copies += [_remote(ins[a], outs[a].at[kme], ssem.at[j, a], rsem.at[j, a], peer) for a in range(n)]
        for cp in copies:
            cp.start()
        for j, (dx, dy) in enumerate(_CHIP_DELTAS):
            ksrc = 2 * _flip(xi, dx) + _flip(yi, dy)
            for a in range(n):
                _remote(ins[a], outs[a].at[ksrc], ssem.at[j, a], rsem.at[j, a], (xi, yi, ci)).wait_recv()
        for cp in copies[n:]:
            cp.wait_send()
        for cp in copies[:n]:
            cp.wait()

    return pl.pallas_call(
        body, in_specs=[_HBM] * n, out_specs=[_HBM] * n,
        out_shape=[jax.ShapeDtypeStruct((4,) + a.shape, a.dtype) for a in arrs],
        scratch_shapes=[pltpu.SemaphoreType.DMA((3, n)), pltpu.SemaphoreType.DMA((3, n)), pltpu.SemaphoreType.DMA((n,))],
        name="wgather",
    )(*arrs)


def _pair_exchange(gs):
    n = len(gs)

    def body(*refs):
        ins, outs = refs[:n], refs[n:2 * n]
        ssem, rsem = refs[2 * n:]
        xi, yi, ci = _pos()
        sends = []
        for a in range(n):
            r2 = gs[a].shape[1] // 2
            src = ins[a].at[:, pl.ds(pl.multiple_of((1 - ci) * r2, 8), r2), :]
            cp = _remote(src, outs[a], ssem.at[a], rsem.at[a], (xi, yi, 1 - ci))
            cp.start()
            sends.append(cp)
        for cp in sends:
            cp.wait_recv()
        for cp in sends:
            cp.wait_send()

    return pl.pallas_call(
        body, in_specs=[_HBM] * n, out_specs=[_HBM] * n,
        out_shape=[jax.ShapeDtypeStruct((g.shape[0], g.shape[1] // 2, g.shape[2]), g.dtype) for g in gs],
        scratch_shapes=[pltpu.SemaphoreType.DMA((n,)), pltpu.SemaphoreType.DMA((n,))], name="grad_pair_exchange",
    )(*gs)


def _row_tile(r):
    return max(t for t in range(8, min(r, 256) + 1, 8) if r % t == 0)


def _pair_add(g, r, cidx, name):
    nk, r2, c = r.shape
    tr = _row_tile(r2)
    nt = r2 // tr

    def body(c_ref, g_ref, r_ref, o_ref):
        o_ref[...] = g_ref[...] + r_ref[...]

    return pl.pallas_call(
        body, grid_spec=pltpu.PrefetchScalarGridSpec(
            num_scalar_prefetch=1, grid=(nk, nt),
            in_specs=[pl.BlockSpec((None, tr, c), lambda k, i, cr: (k, cr[0] * nt + i, 0)),
                      pl.BlockSpec((None, tr, c), lambda k, i, cr: (k, i, 0))],
            out_specs=pl.BlockSpec((None, tr, c), lambda k, i, cr: (k, i, 0))),
        out_shape=jax.ShapeDtypeStruct(r.shape, _F32),
        compiler_params=_cparams(("arbitrary", "arbitrary")), name=name,
    )(cidx, g, r)


def _chip_exchange(ss):
    n = len(ss)

    def body(*refs):
        ins, outs = refs[:n], refs[n:2 * n]
        ssem, rsem = refs[2 * n:]
        xi, yi, ci = _pos()
        sends = []
        for j, (dx, dy) in enumerate(_CHIP_DELTAS):
            tx, ty = _flip(xi, dx), _flip(yi, dy)
            for a in range(n):
                cp = _remote(ins[a].at[2 * tx + ty], outs[a].at[j], ssem.at[j, a], rsem.at[j, a], (tx, ty, ci))
                cp.start()
                sends.append(cp)
        for cp in sends:
            cp.wait_recv()
        for cp in sends:
            cp.wait_send()

    return pl.pallas_call(
        body, in_specs=[_HBM] * n, out_specs=[_HBM] * n,
        out_shape=[jax.ShapeDtypeStruct((3,) + s.shape[1:], s.dtype) for s in ss],
        scratch_shapes=[pltpu.SemaphoreType.DMA((3, n)), pltpu.SemaphoreType.DMA((3, n))], name="grad_chip_exchange",
    )(*ss)


def _chip_add(s, r, kidx, name):
    _, r2, c = r.shape
    tr = _row_tile(r2)

    def body(k_ref, s_ref, r_ref, o_ref):
        o_ref[...] = ((s_ref[...] + r_ref[0]) + r_ref[1]) + r_ref[2]

    return pl.pallas_call(
        body, grid_spec=pltpu.PrefetchScalarGridSpec(
            num_scalar_prefetch=1, grid=(r2 // tr,),
            in_specs=[pl.BlockSpec((None, tr, c), lambda i, kr: (kr[0], i, 0)),
                      pl.BlockSpec((3, tr, c), lambda i, kr: (0, i, 0))],
            out_specs=pl.BlockSpec((tr, c), lambda i, kr: (i, 0))),
        out_shape=jax.ShapeDtypeStruct((r2, c), _F32),
        compiler_params=_cparams(("arbitrary",)), name=name,
    )(kidx, s, r)


def _pair_share(hs):
    n = len(hs)

    def body(*refs):
        ins, outs = refs[:n], refs[n:2 * n]
        ssem, rsem, lsem = refs[2 * n:]
        xi, yi, ci = _pos()
        local, sends, recvs = [], [], []
        for a in range(n):
            r2 = hs[a].shape[0]
            mine = outs[a].at[pl.ds(pl.multiple_of(ci * r2, 8), r2), :]
            theirs = outs[a].at[pl.ds(pl.multiple_of((1 - ci) * r2, 8), r2), :]
            local.append(pltpu.make_async_copy(ins[a], mine, lsem.at[a]))
            sends.append(_remote(ins[a], mine, ssem.at[a], rsem.at[a], (xi, yi, 1 - ci)))
            recvs.append(_remote(ins[a], theirs, ssem.at[a], rsem.at[a], (xi, yi, ci)))
        for cp in local + sends:
            cp.start()
        for cp in recvs:
            cp.wait_recv()
        for cp in sends:
            cp.wait_send()
        for cp in local:
            cp.wait()

    return pl.pallas_call(
        body, in_specs=[_HBM] * n, out_specs=[_HBM] * n,
        out_shape=[jax.ShapeDtypeStruct((2 * h.shape[0], h.shape[1]), h.dtype) for h in hs],
        scratch_shapes=[pltpu.SemaphoreType.DMA((n,)), pltpu.SemaphoreType.DMA((n,)), pltpu.SemaphoreType.DMA((n,))],
        name="grad_pair_share",
    )(*hs)


def _small_allreduce(pack):
    r, c = pack.shape

    def body(p_ref, all_ref, sum_ref, ssem, rsem):
        xi, yi, ci = _pos()
        me = 4 * xi + 2 * yi + ci
        all_ref[me] = p_ref[...]
        sends = []
        for p, (dx, dy, dc) in enumerate(_DEV_DELTAS):
            cp = _remote(p_ref, all_ref.at[me], ssem.at[p], rsem.at[p], (_flip(xi, dx), _flip(yi, dy), _flip(ci, dc)))
            cp.start()
            sends.append(cp)
        for p, (dx, dy, dc) in enumerate(_DEV_DELTAS):
            src = 4 * _flip(xi, dx) + 2 * _flip(yi, dy) + _flip(ci, dc)
            _remote(p_ref, all_ref.at[src], ssem.at[p], rsem.at[p], (xi, yi, ci)).wait_recv()
        for cp in sends:
            cp.wait_send()
        tot = all_ref[0]
        for dev in range(1, 8):
            tot = tot + all_ref[dev]
        sum_ref[...] = tot

    return pl.pallas_call(
        body, in_specs=[_VMEM], out_specs=[_VMEM, _VMEM],
        out_shape=[jax.ShapeDtypeStruct((8, r, c), _F32), jax.ShapeDtypeStruct((r, c), _F32)],
        scratch_shapes=[pltpu.SemaphoreType.DMA((7,)), pltpu.SemaphoreType.DMA((7,))],
        compiler_params=pltpu.CompilerParams(vmem_limit_bytes=_VMEM_LIMIT), name="small_allreduce",
    )(pack)


def _adamw(w, g, m, v):
    m = _ADAM_B1 * m + (1.0 - _ADAM_B1) * g
    v = _ADAM_B2 * v + (1.0 - _ADAM_B2) * (g * g)
    m_hat = m / (1.0 - _ADAM_B1 ** _ADAM_STEP)
    v_hat = v / (1.0 - _ADAM_B2 ** _ADAM_STEP)
    return -_ADAM_LR * (m_hat / (jnp.sqrt(v_hat) + _ADAM_EPS) + _ADAM_WD * w), m, v


def _adamw_big(w, g, m, v, name):
    r, c = w.shape
    tr = _row_tile(r)

    def body(w_ref, g_ref, m_ref, v_ref, d_ref, mo_ref, vo_ref):
        d_ref[...], mo_ref[...], vo_ref[...] = _adamw(w_ref[...], g_ref[...], m_ref[...], v_ref[...])

    spec = pl.BlockSpec((tr, c), lambda i: (i, 0))
    return pl.pallas_call(
        body, grid=(r // tr,), in_specs=[spec] * 4, out_specs=[spec] * 3,
        out_shape=[jax.ShapeDtypeStruct((r, c), _F32)] * 3,
        compiler_params=_cparams(("arbitrary",)), name=name,
    )(w, g, m, v)


def _adamw_small(ws, gs, ms, vs):
    n = len(ws)
    summed = [i for i in range(n) if gs[i].shape != ws[i].shape]

    def body(*refs):
        w_r, g_r, m_r, v_r = (refs[i * n:(i + 1) * n] for i in range(4))
        outs = refs[4 * n:]
        for i in range(n):
            g = g_r[i][...]
            if i in summed:
                g = _rowsum(g)
                outs[3 * n + summed.index(i)][...] = g
            outs[i][...], outs[n + i][...], outs[2 * n + i][...] = _adamw(w_r[i][...], g, m_r[i][...], v_r[i][...])

    shapes = [jax.ShapeDtypeStruct(w.shape, _F32) for w in ws]
    res = pl.pallas_call(
        body, in_specs=[_VMEM] * (4 * n), out_specs=[_VMEM] * (3 * n + len(summed)),
        out_shape=shapes * 3 + [shapes[i] for i in summed],
        compiler_params=pltpu.CompilerParams(vmem_limit_bytes=_VMEM_LIMIT), name="adamw_small",
    )(*ws, *gs, *ms, *vs)
    gs = list(gs)
    for pos, i in enumerate(summed):
        gs[i] = res[3 * n + pos]
    return gs, res[:n], res[n:2 * n], res[2 * n:3 * n]


def _ada_bwd(c_all, dmod_k, w, m, v):
    d, nk = w.shape
    tn = 512 if nk % 512 == 0 else nk

    def body(c_ref, dm_ref, w_ref, m_ref, v_ref, g_ref, d_ref, mo_ref, vo_ref):
        ca = c_ref[...]
        g = _dot_tn(_mx(ca * jax.nn.sigmoid(ca)), _mx(dm_ref[...]))
        g_ref[...] = g
        d_ref[...], mo_ref[...], vo_ref[...] = _adamw(w_ref[...], g, m_ref[...], v_ref[...])

    col = pl.BlockSpec((d, tn), lambda j: (0, j))
    return pl.pallas_call(
        body, grid=(nk // tn,),
        in_specs=[pl.BlockSpec(c_all.shape, lambda j: (0, 0)), pl.BlockSpec((c_all.shape[0], tn), lambda j: (0, j)), col, col, col],
        out_specs=[col] * 4, out_shape=[jax.ShapeDtypeStruct((d, nk), _F32)] * 4,
        compiler_params=_cparams(("arbitrary",)), name="ada_bwd",
    )(c_all, dmod_k, w, m, v)


def _block_diag(wh):
    hn, dh, _ = wh.shape
    eye = jnp.eye(hn, dtype=wh.dtype)
    return (eye[:, None, :, None] * wh[:, :, None, :]).reshape(hn * dh, hn * dh)


def _pack(pieces):
    out = []
    for p in pieces:
        flat = p.reshape(-1, 128)
        out.append(jnp.pad(flat, ((0, (-flat.shape[0]) % 8), (0, 0))))
    return jnp.concatenate(out, axis=0)


def _unpack(pack, shapes):
    out, off = [], 0
    for shp in shapes:
        rows = math.prod(shp) // 128
        out.append(pack[..., off:off + rows, :].reshape(pack.shape[:-2] + tuple(shp)))
        off += rows + (-rows) % 8
    return out


_WEIGHTS = ('w_ada', 'b_ada', 'w_in', 'lru_conv_w', 'lru_conv_b', 'lru_w_r', 'lru_b_r', 'lru_w_i', 'lru_b_i', 'lru_lambda',
            'conv_w', 'conv_b', 'conv_norm_g', 'conv_norm_b', 'w_out', 'ln1_g', 'ln1_b', 'ffn_w_up', 'ffn_conv_w',
            'ffn_conv_b', 'ffn_w_down', 'ln2_g', 'ln2_b')
_BIG = ('w_in', 'w_out', 'ffn_w_up', 'ffn_w_down')


def kernel(x, c, w_ada, b_ada, w_in, lru_conv_w, lru_conv_b, lru_w_r, lru_b_r, lru_w_i, lru_b_i, lru_lambda, conv_w, conv_b, conv_norm_g, conv_norm_b, w_out, ln1_g, ln1_b, ffn_w_up, ffn_conv_w, ffn_conv_b, ffn_w_down, ln2_g, ln2_b, loss_target, m_w_ada, m_b_ada, m_w_in, m_lru_conv_w, m_lru_conv_b, m_lru_w_r, m_lru_b_r, m_lru_w_i, m_lru_b_i, m_lru_lambda, m_conv_w, m_conv_b, m_conv_norm_g, m_conv_norm_b, m_w_out, m_ln1_g, m_ln1_b, m_ffn_w_up, m_ffn_conv_w, m_ffn_conv_b, m_ffn_w_down, m_ln2_g, m_ln2_b, v_w_ada, v_b_ada, v_w_in, v_lru_conv_w, v_lru_conv_b, v_lru_w_r, v_lru_b_r, v_lru_w_i, v_lru_b_i, v_lru_lambda, v_conv_w, v_conv_b, v_conv_norm_g, v_conv_norm_b, v_w_out, v_ln1_g, v_ln1_b, v_ffn_w_up, v_ffn_conv_w, v_ffn_conv_b, v_ffn_w_down, v_ln2_g, v_ln2_b):
    given = dict(locals())
    wt = {n: given[n] for n in _WEIGHTS}
    mo = {n: given["m_" + n] for n in _WEIGHTS}
    vo = {n: given["v_" + n] for n in _WEIGHTS}
    bl, s_len, d = x.shape
    wd = d // 2
    tokens = bl * s_len
    xi, yi, ci = _pos()
    kme = 2 * xi + yi
    kidx = jnp.reshape(kme, (1,)).astype(jnp.int32)
    cidx = jnp.reshape(ci, (1,)).astype(jnp.int32)

    nk = w_ada.shape[2]
    c8 = jnp.pad(c, ((0, 8 - bl), (0, 0)))
    c_all, mod8 = _ada_fwd(c8, w_ada[0], lax.dynamic_slice(b_ada, (0, kme * nk), (1, nk)))
    mod3 = mod8[:bl].reshape(bl, 1, 6 * d)

    win, wout_s, wup, wdn_s, lcw_s, cw_s, fcw_s = _wgather(
        [_mx(w_in[0]), _mx(w_out[0]), _mx(ffn_w_up[0]), _mx(ffn_w_down[0]), lru_conv_w[0], conv_w[0], ffn_conv_w[0]])
    wout = wout_s.reshape(d, d)
    f = wdn_s.shape[0] * wdn_s.shape[1]
    wdn = wdn_s.reshape(f, d)
    unshard = lambda t: jnp.transpose(t, (1, 0, 2)).reshape(t.shape[1], -1)
    lcw, cw, fcw = unshard(lcw_s), unshard(cw_s), unshard(fcw_s)
    wr_bd, wi_bd = _mx(_block_diag(lru_w_r[0])), _mx(_block_diag(lru_w_i[0]))
    seg = _block_diag(jnp.ones((_N_HEADS, wd // _N_HEADS, wd // _N_HEADS), jnp.bfloat16))
    mixer_small = (lcw, lru_conv_b, wr_bd, wi_bd, lru_b_r, lru_b_i, lru_lambda, cw, conv_b, conv_norm_g, conv_norm_b, seg, wout, ln1_g)

    proj, h, mix, x1, u1, y = _mix_fwd(x, mod3, win, *mixer_small, ln1_b)
    u2, hh, fact, dz2, loss_acc, dln2, dgt2 = _ffn_fwd(x1, mod3, wup, fcw, ffn_conv_b, wdn, ln2_g, ln2_b, loss_target)
    dx1, dy2, dh, dfc, dmod2 = _ffn_bwd(dz2, x1, hh, mod3, wup, wdn, fcw, ffn_conv_b)
    grad_x, dproj, dmix, xcg, vecw, dlcw, dcw, dln1, dmod1 = _mix_bwd(dx1, x, mix, proj, h, mod3, win, *mixer_small)
    loss = lax.psum(0.5 * loss_acc[0, 0] / d, ("x", "y", "c"))

    flat = lambda t: t.reshape(tokens, t.shape[-1])
    fc = wup.shape[2]
    g_up = _wgrad(flat(u2), flat(dh), d, fc, 1, 4, 0, 0, "wgrad_up")
    g_dn = _wgrad(flat(fact), flat(dy2), fc, d, f // fc, 1, 0, 0, "wgrad_down").reshape(4, f // 4, d)
    g_in = _wgrad(flat(u1), flat(dproj), d, wd, 1, 4, 0, 0, "wgrad_in")
    g_out = _wgrad(flat(y), flat(dmix), d, d, 1, 1, 0, 0, "wgrad_out").reshape(4, d // 4, d)
    g_ri = _wgrad(flat(xcg), flat(xcg), wd, wd, 1, 2, 0, 1, "wgrad_gates")
    dh_ = wd // _N_HEADS
    g_ri = jnp.stack([jnp.stack([g_ri[i, hd * dh_:(hd + 1) * dh_, hd * dh_:(hd + 1) * dh_] for hd in range(_N_HEADS)])
                      for i in range(2)])

    partial = [g_in, g_out, g_up, g_dn]
    recv = _pair_exchange(partial)
    chip_sum = [_pair_add(g, r, cidx, "grad_pair_add_" + n) for g, r, n in zip(partial, recv, _BIG)]
    recv = _chip_exchange(chip_sum)
    half = [_chip_add(s, r, kidx, "grad_chip_add_" + n) for s, r, n in zip(chip_sum, recv, _BIG)]
    grads, deltas, new_m, new_v = {}, {}, {}, {}
    for n, g in zip(_BIG, _pair_share(half)):
        dl, mm, vv = _adamw_big(wt[n][0], g, mo[n][0], vo[n][0], "adamw_" + n)
        grads[n], deltas[n], new_m[n], new_v[n] = g[None], dl[None], mm[None], vv[None]

    dmod = jnp.concatenate([dmod1.reshape(bl, 3 * d), dmod2.reshape(bl, 2 * d), dgt2.reshape(bl, d)], axis=1)
    pieces = [vecw, dlcw, dcw, jnp.concatenate([dln1, dln2], axis=0), dfc, g_ri, jnp.pad(dmod, ((0, 8 - bl), (0, 0)))]
    shapes = [p.shape for p in pieces]
    every, total = _small_allreduce(_pack(pieces))
    vecw, dlcw, dcw, dln, dfc, g_ri, dmod_sum = _unpack(total, shapes)
    dmod_all = _unpack(every, shapes)[-1].reshape(64, 6 * d)

    g_ada, dl, mm, vv = _ada_bwd(c_all, lax.dynamic_slice(dmod_all, (0, kme * nk), (64, nk)), w_ada[0], m_w_ada[0], v_w_ada[0])
    grads['w_ada'], deltas['w_ada'], new_m['w_ada'], new_v['w_ada'] = g_ada[None], dl[None], mm[None], vv[None]

    shard = lambda t, width: lax.dynamic_slice(t, (0, kme * width), (t.shape[0], width))
    small = {
        'b_ada': dmod_sum, 'lru_conv_w': shard(dlcw, wd // 4), 'lru_conv_b': vecw[0:1], 'lru_w_r': g_ri[0], 'lru_b_r': vecw[1:2],
        'lru_w_i': g_ri[1], 'lru_b_i': vecw[2:3], 'lru_lambda': vecw[3:4], 'conv_w': shard(dcw, wd // 4), 'conv_b': vecw[4:5],
        'conv_norm_g': vecw[5:6], 'conv_norm_b': vecw[6:7], 'ln1_g': dln[0:1], 'ln1_b': dln[1:2],
        'ffn_conv_w': shard(dfc[0:3], f // 4), 'ffn_conv_b': dfc[3:4], 'ln2_g': dln[2:3], 'ln2_b': dln[3:4]}
    names = list(small)
    gs = [small[n] if n == 'b_ada' else small[n].reshape(wt[n].shape) for n in names]
    gs, dls, mms, vvs = _adamw_small([wt[n] for n in names], gs, [mo[n] for n in names], [vo[n] for n in names])
    for n, g, dl, mm, vv in zip(names, gs, dls, mms, vvs):
        grads[n], deltas[n], new_m[n], new_v[n] = g, dl, mm, vv

    return (loss, grad_x, *[grads[n] for n in _WEIGHTS], *[deltas[n] for n in _WEIGHTS],
            *[new_m[n] for n in _WEIGHTS], *[new_v[n] for n in _WEIGHTS])
```

```python
import functools
import itertools
import math

import jax
import jax.numpy as jnp
from jax import lax
from jax.experimental import pallas as pl
from jax.experimental.pallas import tpu as pltpu

_MXU_DT = jnp.bfloat16
_F32 = jnp.float32
_VMEM_LIMIT = 56 * 1024 * 1024
_TT_MIX = 256
_TT_FFN = 256
_TK_WGRAD = 512
_HALO = 32

_LRU_C = 8.0
_LN_EPS = 1e-5
_N_HEADS = 8
_DEPTH = 1
_ALPHA = (2 * _DEPTH) ** 0.25
_ADAM_LR, _ADAM_B1, _ADAM_B2, _ADAM_EPS, _ADAM_WD, _ADAM_STEP = 0.001, 0.9, 0.999, 1e-08, 0.01, 10

_MESH = pl.DeviceIdType.MESH
_CHIP_DELTAS = ((1, 0), (0, 1), (1, 1))


def _cparams(sem):
    return pltpu.CompilerParams(dimension_semantics=sem, vmem_limit_bytes=_VMEM_LIMIT)


def _resident(shape):
    nd = len(shape)
    return pl.BlockSpec(shape, lambda *_: (0,) * nd, pipeline_mode=pl.Buffered(1))


def _dot(a, b):
    return jnp.dot(a, b, preferred_element_type=_F32)


def _dot_nt(a, b):
    return lax.dot_general(a, b, (((1,), (1,)), ((), ())), preferred_element_type=_F32)


def _dot_tn(a, b):
    return lax.dot_general(a, b, (((0,), (0,)), ((), ())), preferred_element_type=_F32)


def _mx(v):
    return v.astype(_MXU_DT)


def _expm1(v):
    series = v * (1.0 + v * (1.0 / 2 + v * (1.0 / 6 + v * (1.0 / 24 + v * (1.0 / 120)))))
    return jnp.where(jnp.abs(v) < 0.0625, series, jnp.exp(v) - 1.0)


def _softplus(z):
    e = jnp.exp(-jnp.abs(z))
    u = 1.0 + e
    log1p = jnp.where(u == 1.0, e, jnp.log(u) * e / jnp.where(u == 1.0, 1.0, u - 1.0))
    return jnp.maximum(z, 0.0) + log1p


_GELU_C = math.sqrt(2.0 / math.pi)


def _gelu_and_grad(v):
    t = jnp.tanh(_GELU_C * (v + 0.044715 * v * v * v))
    val = 0.5 * v * (1.0 + t)
    grad = 0.5 * (1.0 + t) + 0.5 * v * (1.0 - t * t) * _GELU_C * (1.0 + 3 * 0.044715 * v * v)
    return val, grad


def _seg_sum(v, seg, passes=3):
    hi = v.astype(jnp.bfloat16)
    r1 = v - hi.astype(_F32)
    mid = r1.astype(jnp.bfloat16)
    out = _dot(hi, seg) + _dot(mid, seg)
    if passes == 3:
        out = out + _dot((r1 - mid.astype(_F32)).astype(jnp.bfloat16), seg)
    return out


def _scan_fwd(a, u, h0):
    n = a.shape[0]
    row = lax.broadcasted_iota(jnp.int32, a.shape, 0)
    h, d = u, 1
    while d < n:
        keep = row >= d
        h = a * jnp.where(keep, pltpu.roll(h, d, 0), 0.0) + h
        a = a * jnp.where(keep, pltpu.roll(a, d, 0), 1.0)
        d *= 2
    return h + a * h0


def _scan_rev(c, g, g_end):
    n = c.shape[0]
    row = lax.broadcasted_iota(jnp.int32, c.shape, 0)
    d = 1
    while d < n:
        keep = row < n - d
        g = c * jnp.where(keep, pltpu.roll(g, n - d, 0), 0.0) + g
        c = c * jnp.where(keep, pltpu.roll(c, n - d, 0), 1.0)
        d *= 2
    return g + c * g_end


def _layer_norm_stats(z):
    mu = jnp.mean(z, axis=-1, keepdims=True)
    zc = z - mu
    var = jnp.mean(zc * zc, axis=-1, keepdims=True)
    rstd = lax.rsqrt(var + _LN_EPS)
    return zc * rstd, rstd


def _layer_norm_bwd(dn, n, rstd):
    return rstd * (dn - jnp.mean(dn, axis=-1, keepdims=True) - n * jnp.mean(dn * n, axis=-1, keepdims=True))


def _rowsum(v):
    return jnp.sum(v, axis=0, keepdims=True)


def _fused_exchange(body, n_in, n_out, n_scratch, n_xin, n_xout, plan, grid):
    def wrapped(*refs):
        o0 = n_in + n_xin
        s0 = o0 + n_out + n_xout
        start, finish = plan(refs[n_in:o0], refs[o0 + n_out:s0], *refs[s0 + n_scratch:])
        step = pl.program_id(0) * grid[1] + pl.program_id(1)

        @pl.when(step == 0)
        def _():
            start()

        body(*refs[:n_in], *refs[o0:o0 + n_out], *refs[s0:s0 + n_scratch])

        @pl.when(step == grid[0] * grid[1] - 1)
        def _():
            finish()

    return wrapped


def _lru_gates(xc, wr_ref, wi_ref, br_ref, bi_ref, lam_ref):
    xcb = _mx(xc)
    r = jax.nn.sigmoid(_dot(xcb, wr_ref[...]) + br_ref[...])
    i = jax.nn.sigmoid(_dot(xcb, wi_ref[...]) + bi_ref[...])
    sp = _softplus(-lam_ref[...])
    log_a = -_LRU_C * r * sp
    a = jnp.exp(log_a)
    mult = jnp.sqrt(-_expm1(2.0 * log_a))
    return r, i, sp, a, mult


def _conv_taps(ext_ref, w_ref, first, n_taps, tt):
    acc = w_ref[0:1, :] * ext_ref[pl.ds(first, tt), :]
    for k in range(1, n_taps):
        acc = acc + w_ref[k:k + 1, :] * ext_ref[pl.ds(first + k, tt), :]
    return acc


def _make_shifted(ext_ref, sh_ref):
    n = sh_ref.shape[1]
    for r in range(1, 8):
        sh_ref[r - 1] = ext_ref[pl.ds(r, n), :]


def _tap(ext_ref, sh_ref, off, tt):
    base = (off // 8) * 8
    if off % 8 == 0:
        return ext_ref[pl.ds(base, tt), :]
    return sh_ref[off % 8 - 1, pl.ds(base, tt), :]


def _conv_taps_shifted(ext_ref, sh_ref, w_ref, first, n_taps, tt):
    acc = w_ref[0:1, :] * _tap(ext_ref, sh_ref, first, tt)
    for k in range(1, n_taps):
        acc = acc + w_ref[k:k + 1, :] * _tap(ext_ref, sh_ref, first + k, tt)
    return acc


def _mix_fwd(x, mod3, win, lcw, lcb, wr_bd, wi_bd, b_r, b_i, lam, cw, cb, ng, nb, seg, wout, ln1g, ln1b, shards):
    bl, s_len, d = x.shape
    w = d // 2
    tt = min(_TT_MIX, s_len)
    ns = s_len // tt
    kc = cw.shape[0]

    def body(x_ref, mod_ref, win_ref, lcw_ref, lcb_ref, wr_ref, wi_ref, br_ref, bi_ref, lam_ref, cw_ref, cb_ref,
             ng_ref, nb_ref, seg_ref, wout_ref, g1_ref, b1_ref,
             proj_ref, h_ref, mix_ref, x1_ref, u1_ref, y_ref, ext4, ext31, sh31, hcar):
        @pl.when(pl.program_id(1) == 0)
        def _():
            ext4[0:8, :] = jnp.zeros((8, w), _F32)
            ext31[0:_HALO, :] = jnp.zeros((_HALO, w), _F32)
            hcar[...] = jnp.zeros_like(hcar)

        xt = x_ref[...]
        sh1, sc1, gt1 = mod_ref[:, 0:d], mod_ref[:, d:2 * d], mod_ref[:, 2 * d:3 * d]
        u1 = _mx(xt * (1.0 + sc1) + sh1)
        u1_ref[...] = u1
        xa, ga, vb, gb = (_dot(u1, win_ref[k]) for k in range(4))
        proj_ref[:, 0:w] = xa
        proj_ref[:, w:2 * w] = ga
        proj_ref[:, 2 * w:3 * w] = vb
        proj_ref[:, 3 * w:4 * w] = gb

        ext4[8:8 + tt, :] = xa
        xc = lcb_ref[...] + _conv_taps(ext4, lcw_ref, 5, 4, tt)
        ext4[0:8, :] = xa[tt - 8:tt, :]
        r, i, sp, a, mult = _lru_gates(xc, wr_ref, wi_ref, br_ref, bi_ref, lam_ref)
        h = _scan_fwd(a, mult * (i * xc), hcar[0:1, :])
        hcar[0:1, :] = h[tt - 1:tt, :]
        h_ref[...] = h
        gelu, _ = _gelu_and_grad(ga)
        y_ref[:, 0:w] = _mx(gelu * h)

        vbg = vb * jax.nn.sigmoid(gb)
        ext31[_HALO:_HALO + tt, :] = vbg
        _make_shifted(ext31, sh31)
        vbc = cb_ref[...] + _conv_taps_shifted(ext31, sh31, cw_ref, _HALO - (kc - 1), kc, tt)
        ext31[0:_HALO, :] = vbg[tt - _HALO:tt, :]
        inv = 1.0 / (w // _N_HEADS)
        zc = vbc - _seg_sum(vbc, seg_ref[...]) * inv
        n = zc * lax.rsqrt(_seg_sum(zc * zc, seg_ref[...]) * inv + _LN_EPS)
        pre = n * ng_ref[...] + nb_ref[...]
        y_ref[:, w:2 * w] = _mx(pre * jax.nn.sigmoid(pre))

        mix = _dot(y_ref[...], wout_ref[...])
        mix_ref[...] = mix
        n1, _ = _layer_norm_stats(_ALPHA * xt + (1.0 + gt1) * mix)
        x1_ref[...] = n1 * g1_ref[...] + b1_ref[...]

    tok = lambda c: pl.BlockSpec((None, tt, c), lambda b, s: (b, s, 0))
    smalls = [lcw, lcb, wr_bd, wi_bd, b_r, b_i, lam, cw, cb, ng, nb, seg, wout, ln1g, ln1b]
    nx = len(shards)
    return pl.pallas_call(
        _fused_exchange(body, 3 + len(smalls), 6, 4, nx, nx, _gather_plan, (bl, ns)), grid=(bl, ns),
        in_specs=[tok(d), pl.BlockSpec((None, 1, 6 * d), lambda b, s: (b, 0, 0)), _resident(win.shape)]
        + [_resident(t.shape) for t in smalls] + [_HBM] * nx,
        out_specs=[tok(4 * w), tok(w), tok(d), tok(d), tok(d), tok(d)] + [_HBM] * nx,
        out_shape=[jax.ShapeDtypeStruct((bl, s_len, 4 * w), _F32), jax.ShapeDtypeStruct((bl, s_len, w), _F32),
                   jax.ShapeDtypeStruct((bl, s_len, d), _F32), jax.ShapeDtypeStruct((bl, s_len, d), _F32),
                   jax.ShapeDtypeStruct((bl, s_len, d), _MXU_DT), jax.ShapeDtypeStruct((bl, s_len, d), _MXU_DT)]
        + [jax.ShapeDtypeStruct((4,) + t.shape, t.dtype) for t in shards],
        scratch_shapes=[pltpu.VMEM((tt + 8, w), _F32), pltpu.VMEM((tt + _HALO, w), _F32),
                        pltpu.VMEM((7, tt + _HALO - 8, w), _F32), pltpu.VMEM((8, w), _F32)] + _gather_sems(nx),
        compiler_params=_cparams(("arbitrary", "arbitrary")), name="mix_fwd",
    )(x, mod3, win, *smalls, *shards)


def _ffn_fwd(x1, mod3, wup, fcw, fcb, wdn, ln2g, ln2b, target):
    bl, s_len, d = x1.shape
    nch, _, fc = wup.shape
    nch //= 2
    f = nch * fc
    tt = min(_TT_FFN, s_len)
    ns = s_len // tt

    def body(x1_ref, mod_ref, wup_ref, fcw_ref, fcb_ref, wdn_ref, g2_ref, b2_ref, tgt_ref,
             u2_ref, hh_ref, f_ref, dz2_ref, loss_ref, dln2_ref, dgt2_ref, ext3):
        first_tile = pl.program_id(1) == 0

        @pl.when(first_tile)
        def _():
            ext3[:, 0:8, :] = jnp.zeros((nch, 8, fc), _F32)
            dgt2_ref[...] = jnp.zeros_like(dgt2_ref)

        @pl.when(first_tile & (pl.program_id(0) == 0))
        def _():
            loss_ref[...] = jnp.zeros_like(loss_ref)
            dln2_ref[...] = jnp.zeros_like(dln2_ref)

        x1t = x1_ref[...]
        sh2, sc2, gt2 = mod_ref[:, 3 * d:4 * d], mod_ref[:, 4 * d:5 * d], mod_ref[:, 5 * d:6 * d]
        u2 = _mx(x1t * (1.0 + sc2) + sh2)
        u2_ref[...] = u2
        y2 = jnp.zeros((tt, d), _F32)
        for j in range(nch):
            lanes = slice(j * fc, (j + 1) * fc)
            v = _dot(u2, wup_ref[j])
            g = _dot(u2, wup_ref[nch + j])
            hh_ref[:, lanes] = v.astype(hh_ref.dtype)
            hh_ref[:, f + j * fc:f + (j + 1) * fc] = g.astype(hh_ref.dtype)
            ext = ext3.at[j]
            ext[8:8 + tt, :] = g
            gc = fcb_ref[:, lanes] + sum(fcw_ref[k:k + 1, lanes] * ext[pl.ds(6 + k, tt), :] for k in range(3))
            ext[0:8, :] = g[tt - 8:tt, :]
            fj = _mx(gc * jax.nn.sigmoid(gc) * v)
            f_ref[:, lanes] = fj
            y2 = y2 + _dot(fj, wdn_ref[lanes, :])

        n2, rstd = _layer_norm_stats(_ALPHA * x1t + (1.0 + gt2) * y2)
        err = n2 * g2_ref[...] + b2_ref[...] - tgt_ref[...]
        loss_ref[...] += jnp.sum(_rowsum(err * err), axis=1, keepdims=True)
        dout = err * (1.0 / d)
        dln2_ref[0:1, :] += _rowsum(dout * n2)
        dln2_ref[1:2, :] += _rowsum(dout)
        dz2 = _layer_norm_bwd(dout * g2_ref[...], n2, rstd)
        dz2_ref[...] = dz2
        dgt2_ref[...] += _rowsum(dz2 * y2)

    tok = lambda c: pl.BlockSpec((None, tt, c), lambda b, s: (b, s, 0))
    acc = lambda r: pl.BlockSpec((r, d), lambda b, s: (0, 0))
    smalls = [fcw, fcb, wdn, ln2g, ln2b]
    return pl.pallas_call(
        body, grid=(bl, ns),
        in_specs=[tok(d), pl.BlockSpec((None, 1, 6 * d), lambda b, s: (b, 0, 0)), _resident(wup.shape)]
        + [_resident(t.shape) for t in smalls] + [tok(d)],
        out_specs=[tok(d), tok(2 * f), tok(f), tok(d), acc(1), acc(2), pl.BlockSpec((None, 1, d), lambda b, s: (b, 0, 0))],
        out_shape=[jax.ShapeDtypeStruct((bl, s_len, d), _MXU_DT), jax.ShapeDtypeStruct((bl, s_len, 2 * f), _F32),
                   jax.ShapeDtypeStruct((bl, s_len, f), _MXU_DT), jax.ShapeDtypeStruct((bl, s_len, d), _F32),
                   jax.ShapeDtypeStruct((1, d), _F32), jax.ShapeDtypeStruct((2, d), _F32),
                   jax.ShapeDtypeStruct((bl, 1, d), _F32)],
        scratch_shapes=[pltpu.VMEM((nch, tt + 8, fc), _F32)],
        compiler_params=_cparams(("arbitrary", "arbitrary")), name="ffn_fwd",
    )(x1, mod3, wup, *smalls, target)


def _ffn_bwd(dz2, x1, hh, mod3, wup, wdn, fcw, fcb):
    bl, s_len, d = x1.shape
    nch, _, fc = wup.shape
    nch //= 2
    f = nch * fc
    tt = min(_TT_FFN, s_len)
    ns = s_len // tt

    def body(dz2_ref, x1_ref, hh_ref, halo_ref, mod_ref, wup_ref, wdn_ref, fcw_ref, fcb_ref,
             dx1_ref, dy2_ref, dh_ref, dfc_ref, dmod_ref, gext, dext, dcar):
        s = ns - 1 - pl.program_id(1)

        @pl.when(pl.program_id(1) == 0)
        def _():
            dcar[...] = jnp.zeros_like(dcar)
            dmod_ref[...] = jnp.zeros_like(dmod_ref)

        @pl.when((pl.program_id(1) == 0) & (pl.program_id(0) == 0))
        def _():
            dfc_ref[...] = jnp.zeros_like(dfc_ref)

        sc2, gt2 = mod_ref[:, 4 * d:5 * d], mod_ref[:, 5 * d:6 * d]
        dz2t = dz2_ref[...]
        dy2 = _mx((1.0 + gt2) * dz2t)
        dy2_ref[...] = dy2
        du2 = jnp.zeros((tt, d), _F32)
        for j in range(nch):
            lanes = slice(j * fc, (j + 1) * fc)
            glanes = slice(f + j * fc, f + (j + 1) * fc)
            v = hh_ref[:, lanes].astype(_F32)
            g = hh_ref[:, glanes].astype(_F32)
            gext[0:8, :] = jnp.where(s == 0, 0.0, halo_ref[:, glanes].astype(_F32))
            gext[8:8 + tt, :] = g
            gc = fcb_ref[:, lanes] + sum(fcw_ref[k:k + 1, lanes] * gext[pl.ds(6 + k, tt), :] for k in range(3))
            sg = jax.nn.sigmoid(gc)
            df = _dot_nt(dy2, wdn_ref[lanes, :])
            dv = df * (gc * sg)
            dgc = df * v * (sg * (1.0 + gc * (1.0 - sg)))
            dfc_ref[3:4, lanes] += _rowsum(dgc)
            dext[0:tt, :] = dgc
            dext[tt:tt + 8, :] = dcar[j]
            dcar[j] = dgc[0:8, :]
            dg = jnp.zeros((tt, fc), _F32)
            for k in range(3):
                dg = dg + fcw_ref[k:k + 1, lanes] * dext[pl.ds(2 - k, tt), :]
                dfc_ref[k:k + 1, lanes] += _rowsum(dgc * gext[pl.ds(6 + k, tt), :])
            dvb, dgb = _mx(dv), _mx(dg)
            dh_ref[:, lanes] = dvb
            dh_ref[:, glanes] = dgb
            du2 = du2 + _dot_nt(dvb, wup_ref[j]) + _dot_nt(dgb, wup_ref[nch + j])

        dx1_ref[...] = _ALPHA * dz2t + du2 * (1.0 + sc2)
        dmod_ref[0:1, :] += _rowsum(du2)
        dmod_ref[1:2, :] += _rowsum(du2 * x1_ref[...])

    tok = lambda c: pl.BlockSpec((None, tt, c), lambda b, i: (b, ns - 1 - i, 0))
    halo = pl.BlockSpec((None, 8, 2 * f), lambda b, i: (b, jnp.maximum((ns - 1 - i) * (tt // 8) - 1, 0), 0))
    return pl.pallas_call(
        body, grid=(bl, ns),
        in_specs=[tok(d), tok(d), tok(2 * f), halo, pl.BlockSpec((None, 1, 6 * d), lambda b, i: (b, 0, 0)),
                  _resident(wup.shape), _resident(wdn.shape), _resident(fcw.shape), _resident(fcb.shape)],
        out_specs=[tok(d), tok(d), tok(2 * f), pl.BlockSpec((4, f), lambda b, i: (0, 0)),
                   pl.BlockSpec((None, 2, d), lambda b, i: (b, 0, 0))],
        out_shape=[jax.ShapeDtypeStruct((bl, s_len, d), _F32), jax.ShapeDtypeStruct((bl, s_len, d), _MXU_DT),
                   jax.ShapeDtypeStruct((bl, s_len, 2 * f), _MXU_DT), jax.ShapeDtypeStruct((4, f), _F32),
                   jax.ShapeDtypeStruct((bl, 2, d), _F32)],
        scratch_shapes=[pltpu.VMEM((tt + 8, fc), _F32), pltpu.VMEM((tt + 8, fc), _F32), pltpu.VMEM((nch, 8, fc), _F32)],
        compiler_params=_cparams(("arbitrary", "arbitrary")), name="ffn_bwd",
    )(dz2, x1, hh, hh, mod3, wup, wdn, fcw, fcb)


def _mix_bwd(dx1, x, mix, proj, h, mod3, win, lcw, lcb, wr_bd, wi_bd, b_r, b_i, lam, cw, cb, ng, nb, seg, wout, ln1g, chip_sums):
    bl, s_len, d = x.shape
    w = d // 2
    tt = min(_TT_MIX, s_len)
    ns = s_len // tt
    kc = cw.shape[0]

    def body(dx1_ref, x_ref, mix_ref, proj_ref, phalo_ref, h_ref, hhalo_ref, mod_ref, win_ref, lcw_ref, lcb_ref,
             wr_ref, wi_ref, br_ref, bi_ref, lam_ref, cw_ref, cb_ref, ng_ref, nb_ref, seg_ref, wout_ref, g1_ref,
             gx_ref, dproj_ref, dmix_ref, xcg_ref, vecw_ref, dlcw_ref, dcw_ref, dln1_ref, dmod_ref,
             ext4, ext31, dext4, dext31, sh31, dsh31, car4, car31, gcar):
        s = ns - 1 - pl.program_id(1)
        first = s == 0

        @pl.when(pl.program_id(1) == 0)
        def _():
            car4[...] = jnp.zeros_like(car4)
            car31[...] = jnp.zeros_like(car31)
            gcar[...] = jnp.zeros_like(gcar)
            dmod_ref[...] = jnp.zeros_like(dmod_ref)

        @pl.when((pl.program_id(1) == 0) & (pl.program_id(0) == 0))
        def _():
            for ref in (vecw_ref, dlcw_ref, dcw_ref, dln1_ref):
                ref[...] = jnp.zeros_like(ref)

        xt, mixt = x_ref[...], mix_ref[...]
        sh1, sc1, gt1 = mod_ref[:, 0:d], mod_ref[:, d:2 * d], mod_ref[:, 2 * d:3 * d]

        n1, rstd1 = _layer_norm_stats(_ALPHA * xt + (1.0 + gt1) * mixt)
        dx1t = dx1_ref[...]
        dln1_ref[0:1, :] += _rowsum(dx1t * n1)
        dln1_ref[1:2, :] += _rowsum(dx1t)
        dz1 = _layer_norm_bwd(dx1t * g1_ref[...], n1, rstd1)
        dmod_ref[2:3, :] += _rowsum(dz1 * mixt)
        dmix = _mx((1.0 + gt1) * dz1)
        dmix_ref[...] = dmix
        dya = _dot_nt(dmix, wout_ref[0:w, :])
        dyb = _dot_nt(dmix, wout_ref[w:2 * w, :])

        xa, ga = proj_ref[:, 0:w], proj_ref[:, w:2 * w]
        vb, gb = proj_ref[:, 2 * w:3 * w], proj_ref[:, 3 * w:4 * w]

        sgb = jax.nn.sigmoid(gb)
        vbg = vb * sgb
        hv, hg = phalo_ref[:, 2 * w:3 * w], phalo_ref[:, 3 * w:4 * w]
        ext31[0:_HALO, :] = jnp.where(first, 0.0, hv * jax.nn.sigmoid(hg))
        ext31[_HALO:_HALO + tt, :] = vbg
        _make_shifted(ext31, sh31)
        vbc = cb_ref[...] + _conv_taps_shifted(ext31, sh31, cw_ref, _HALO - (kc - 1), kc, tt)
        inv = 1.0 / (w // _N_HEADS)
        zc = vbc - _seg_sum(vbc, seg_ref[...]) * inv
        rstd = lax.rsqrt(_seg_sum(zc * zc, seg_ref[...]) * inv + _LN_EPS)
        n = zc * rstd
        pre = n * ng_ref[...] + nb_ref[...]
        sgp = jax.nn.sigmoid(pre)
        dpre = dyb * (sgp * (1.0 + pre * (1.0 - sgp)))
        vecw_ref[5:6, :] += _rowsum(dpre * n)
        vecw_ref[6:7, :] += _rowsum(dpre)
        dn = dpre * ng_ref[...]
        dvbc = rstd * (dn - _seg_sum(dn, seg_ref[...], 2) * inv - n * (_seg_sum(dn * n, seg_ref[...], 2) * inv))
        vecw_ref[4:5, :] += _rowsum(dvbc)
        dext31[0:tt, :] = dvbc
        dext31[tt:tt + _HALO, :] = car31[...]
        car31[...] = dvbc[0:_HALO, :]
        _make_shifted(dext31, dsh31)
        dvbg = jnp.zeros((tt, w), _F32)
        for k in range(kc):
            dvbg = dvbg + cw_ref[k:k + 1, :] * _tap(dext31, dsh31, kc - 1 - k, tt)
            dcw_ref[k:k + 1, :] += _rowsum(dvbc * _tap(ext31, sh31, _HALO - (kc - 1) + k, tt))
        dproj_ref[:, 2 * w:3 * w] = _mx(dvbg * sgb)
        dproj_ref[:, 3 * w:4 * w] = _mx(dvbg * vb * (sgb * (1.0 - sgb)))

        ext4[0:8, :] = jnp.where(first, 0.0, phalo_ref[_HALO - 8:_HALO, 0:w])
        ext4[8:8 + tt, :] = xa
        xc = lcb_ref[...] + _conv_taps(ext4, lcw_ref, 5, 4, tt)
        xcg_ref[:, 0:w] = _mx(xc)
        r, i, sp, a, mult = _lru_gates(xc, wr_ref, wi_ref, br_ref, bi_ref, lam_ref)
        ht = h_ref[...]
        row = lax.broadcasted_iota(jnp.int32, (tt, w), 0)
        h_before = jnp.where(first, 0.0, hhalo_ref[7:8, :])
        hprev = jnp.where(row == 0, h_before, pltpu.roll(ht, 1, 0))
        gelu, dgelu = _gelu_and_grad(ga)
        dproj_ref[:, w:2 * w] = _mx(dya * ht * dgelu)
        dh = dya * gelu
        coef = jnp.where(row == tt - 1, 1.0, pltpu.roll(a, tt - 1, 0))
        big_g = _scan_rev(coef, dh, gcar[0:1, :])
        gcar[0:1, :] = a[0:1, :] * big_g[0:1, :]
        da = big_g * hprev
        ixc = i * xc
        dlog_a = da * a - (big_g * ixc) * (a * a / mult)
        di = big_g * mult * xc
        dxc = big_g * mult * i
        vecw_ref[3:4, :] += _rowsum(dlog_a * r) * (_LRU_C * jax.nn.sigmoid(-lam_ref[...]))
        dgr_f = dlog_a * (-_LRU_C * sp) * (r * (1.0 - r))
        dgi_f = di * (i * (1.0 - i))
        vecw_ref[1:2, :] += _rowsum(dgr_f)
        vecw_ref[2:3, :] += _rowsum(dgi_f)
        dgr, dgi = _mx(dgr_f), _mx(dgi_f)
        xcg_ref[:, w:2 * w] = dgr
        xcg_ref[:, 2 * w:3 * w] = dgi
        dxc = dxc + _dot_nt(dgr, wr_ref[...]) + _dot_nt(dgi, wi_ref[...])
        vecw_ref[0:1, :] += _rowsum(dxc)
        dext4[0:tt, :] = dxc
        dext4[tt:tt + 8, :] = car4[...]
        car4[...] = dxc[0:8, :]
        dxa = jnp.zeros((tt, w), _F32)
        for k in range(4):
            dxa = dxa + lcw_ref[k:k + 1, :] * dext4[pl.ds(3 - k, tt), :]
            dlcw_ref[k:k + 1, :] += _rowsum(dxc * ext4[pl.ds(5 + k, tt), :])
        dproj_ref[:, 0:w] = _mx(dxa)

        du1 = sum(_dot_nt(dproj_ref[:, k * w:(k + 1) * w], win_ref[k]) for k in range(4))
        gx_ref[...] = _ALPHA * dz1 + du1 * (1.0 + sc1)
        dmod_ref[0:1, :] += _rowsum(du1)
        dmod_ref[1:2, :] += _rowsum(du1 * xt)

    tok = lambda c: pl.BlockSpec((None, tt, c), lambda b, i: (b, ns - 1 - i, 0))
    halo = lambda rows, c: pl.BlockSpec(
        (None, rows, c), lambda b, i: (b, jnp.maximum((ns - 1 - i) * (tt // rows) - 1, 0), 0))
    accw = lambda r, c: pl.BlockSpec((r, c), lambda b, i: (0, 0))
    smalls = [lcw, lcb, wr_bd, wi_bd, b_r, b_i, lam, cw, cb, ng, nb, seg, wout, ln1g]
    nx = len(chip_sums)
    return pl.pallas_call(
        _fused_exchange(body, 9 + len(smalls), 9, 9, nx, nx, _chip_reduce_plan, (bl, ns)), grid=(bl, ns),
        in_specs=[tok(d), tok(d), tok(d), tok(4 * w), halo(_HALO, 4 * w), tok(w), halo(8, w),
                  pl.BlockSpec((None, 1, 6 * d), lambda b, i: (b, 0, 0)), _resident(win.shape)]
        + [_resident(t.shape) for t in smalls] + [_HBM] * nx,
        out_specs=[tok(d), tok(4 * w), tok(d), tok(3 * w), accw(8, w), accw(4, w), accw(kc, w), accw(2, d),
                   pl.BlockSpec((None, 3, d), lambda b, i: (b, 0, 0))] + [_HBM] * nx,
        out_shape=[jax.ShapeDtypeStruct((bl, s_len, d), _F32), jax.ShapeDtypeStruct((bl, s_len, 4 * w), _MXU_DT),
                   jax.ShapeDtypeStruct((bl, s_len, d), _MXU_DT), jax.ShapeDtypeStruct((bl, s_len, 3 * w), _MXU_DT),
                   jax.ShapeDtypeStruct((8, w), _F32), jax.ShapeDtypeStruct((4, w), _F32),
                   jax.ShapeDtypeStruct((kc, w), _F32), jax.ShapeDtypeStruct((2, d), _F32),
                   jax.ShapeDtypeStruct((bl, 3, d), _F32)]
        + [jax.ShapeDtypeStruct((3,) + t.shape[1:], t.dtype) for t in chip_sums],
        scratch_shapes=[pltpu.VMEM((tt + 8, w), _F32), pltpu.VMEM((tt + _HALO, w), _F32),
                        pltpu.VMEM((tt + 8, w), _F32), pltpu.VMEM((tt + _HALO, w), _F32),
                        pltpu.VMEM((7, tt + _HALO - 8, w), _F32), pltpu.VMEM((7, tt + _HALO - 8, w), _F32),
                        pltpu.VMEM((8, w), _F32), pltpu.VMEM((_HALO, w), _F32), pltpu.VMEM((8, w), _F32)]
        + _chip_reduce_sems(nx),
        compiler_params=_cparams(("arbitrary", "arbitrary")), name="mix_bwd",
    )(dx1, x, mix, proj, proj, h, h, mod3, win, *smalls, *chip_sums)


def _wgrad(a, b, ma, nbw, na, nb, a_off, b_off, name):
    t = a.shape[0]
    tk = min(_TK_WGRAD, t)

    def body(a_ref, b_ref, o_ref):
        @pl.when(pl.program_id(1) == 0)
        def _():
            o_ref[...] = jnp.zeros_like(o_ref)
        o_ref[...] += _dot_tn(a_ref[...], b_ref[...])

    return pl.pallas_call(
        body, grid=(na * nb, t // tk),
        in_specs=[pl.BlockSpec((tk, ma), lambda j, k: (k, j // nb + a_off)),
                  pl.BlockSpec((tk, nbw), lambda j, k: (k, j % nb + b_off))],
        out_specs=pl.BlockSpec((None, ma, nbw), lambda j, k: (j, 0, 0)),
        out_shape=jax.ShapeDtypeStruct((na * nb, ma, nbw), _F32),
        compiler_params=_cparams(("arbitrary", "arbitrary")), name=name,
    )(a, b)


_DEV_DELTAS = tuple(dl for dl in itertools.product((0, 1), repeat=3) if any(dl))
_HBM = pl.BlockSpec(memory_space=pltpu.HBM)
_VMEM = pl.BlockSpec(memory_space=pltpu.VMEM)


def _pos():
    return lax.axis_index("x"), lax.axis_index("y"), lax.axis_index("c")


def _flip(v, delta):
    return 1 - v if delta else v


def _remote(src, dst, ssem, rsem, dev):
    return pltpu.make_async_remote_copy(src_ref=src, dst_ref=dst, send_sem=ssem, recv_sem=rsem,
                                        device_id=dev, device_id_type=_MESH)


def _rows(ref, idx, n):
    return ref.at[pl.ds(pl.multiple_of(idx * n, 8), n)]


def _ada_fwd(c8, w_ada_k, b_ada_k):
    rows, d = c8.shape
    nk = w_ada_k.shape[1]

    def body(c_ref, w_ref, b_ref, call_ref, mod_ref, modloc, modrcv, s1, r1, s2, r2):
        xi, yi, ci = _pos()
        me, kme = 4 * xi + 2 * yi + ci, 2 * xi + yi
        call_ref[pl.ds(pl.multiple_of(me * rows, 8), rows), :] = c_ref[...]
        sends = []
        for p, (dx, dy, dc) in enumerate(_DEV_DELTAS):
            cp = _remote(c_ref, _rows(call_ref, me, rows), s1.at[p], r1.at[p], (_flip(xi, dx), _flip(yi, dy), _flip(ci, dc)))
            cp.start()
            sends.append(cp)
        for p, (dx, dy, dc) in enumerate(_DEV_DELTAS):
            src = 4 * _flip(xi, dx) + 2 * _flip(yi, dy) + _flip(ci, dc)
            _remote(c_ref, _rows(call_ref, src, rows), s1.at[p], r1.at[p], (xi, yi, ci)).wait_recv()
        for cp in sends:
            cp.wait_send()

        ca = call_ref[...]
        modloc[...] = _dot(_mx(ca * jax.nn.sigmoid(ca)), _mx(w_ref[...])) + b_ref[...]
        modrcv[kme] = modloc[pl.ds(pl.multiple_of(me * rows, 8), rows), :]
        sends = []
        for j, (dx, dy) in enumerate(_CHIP_DELTAS):
            tx, ty = _flip(xi, dx), _flip(yi, dy)
            cp = _remote(_rows(modloc, 4 * tx + 2 * ty + ci, rows), modrcv.at[kme], s2.at[j], r2.at[j], (tx, ty, ci))
            cp.start()
            sends.append(cp)
        for j, (dx, dy) in enumerate(_CHIP_DELTAS):
            ksrc = 2 * _flip(xi, dx) + _flip(yi, dy)
            _remote(_rows(modloc, me, rows), modrcv.at[ksrc], s2.at[j], r2.at[j], (xi, yi, ci)).wait_recv()
        for cp in sends:
            cp.wait_send()
        for j in range(4):
            mod_ref[:, j * nk:(j + 1) * nk] = modrcv[j]

    return pl.pallas_call(
        body, in_specs=[_VMEM, _VMEM, _VMEM], out_specs=[_VMEM, _VMEM],
        out_shape=[jax.ShapeDtypeStruct((8 * rows, d), _F32), jax.ShapeDtypeStruct((rows, 4 * nk), _F32)],
        scratch_shapes=[pltpu.VMEM((8 * rows, nk), _F32), pltpu.VMEM((4, rows, nk), _F32),
                        pltpu.SemaphoreType.DMA((7,)), pltpu.SemaphoreType.DMA((7,)),
                        pltpu.SemaphoreType.DMA((3,)), pltpu.SemaphoreType.DMA((3,))],
        compiler_params=pltpu.CompilerParams(vmem_limit_bytes=_VMEM_LIMIT), name="ada_fwd",
    )(c8, w_ada_k, b_ada_k)


def _wgather(arrs):
    n = len(arrs)

    def body(*refs):
        start, finish = _gather_plan(refs[:n], refs[n:2 * n], *refs[2 * n:])
        start()
        finish()

    return pl.pallas_call(
        body, in_specs=[_HBM] * n, out_specs=[_HBM] * n,
        out_shape=[jax.ShapeDtypeStruct((4,) + a.shape, a.dtype) for a in arrs],
        scratch_shapes=_gather_sems(n), name="wgather",
    )(*arrs)


def _gather_sems(n):
    return [pltpu.SemaphoreType.DMA((3, n)), pltpu.SemaphoreType.DMA((3, n)), pltpu.SemaphoreType.DMA((n,))]


def _gather_plan(ins, outs, ssem, rsem, lsem):
    n = len(ins)
    xi, yi, ci = _pos()
    kme = 2 * xi + yi
    local = [pltpu.make_async_copy(ins[a], outs[a].at[kme], lsem.at[a]) for a in range(n)]
    sends, recvs = [], []
    for j, (dx, dy) in enumerate(_CHIP_DELTAS):
        tx, ty = _flip(xi, dx), _flip(yi, dy)
        for a in range(n):
            sends.append(_remote(ins[a], outs[a].at[kme], ssem.at[j, a], rsem.at[j, a], (tx, ty, ci)))
            recvs.append(_remote(ins[a], outs[a].at[2 * tx + ty], ssem.at[j, a], rsem.at[j, a], (xi, yi, ci)))

    def start():
        for cp in local + sends:
            cp.start()

    def finish():
        for cp in recvs:
            cp.wait_recv()
        for cp in sends:
            cp.wait_send()
        for cp in local:
            cp.wait()

    return start, finish


def _chip_reduce_sems(n):
    return [pltpu.SemaphoreType.DMA((3, n)), pltpu.SemaphoreType.DMA((3, n))]


def _chip_reduce_plan(ins, outs, ssem, rsem):
    xi, yi, ci = _pos()
    sends = []
    for j, (dx, dy) in enumerate(_CHIP_DELTAS):
        tx, ty = _flip(xi, dx), _flip(yi, dy)
        sends += [_remote(ins[a].at[2 * tx + ty], outs[a].at[j], ssem.at[j, a], rsem.at[j, a], (tx, ty, ci))
                  for a in range(len(ins))]

    def start():
        for cp in sends:
            cp.start()

    def finish():
        for cp in sends:
            cp.wait_recv()
        for cp in sends:
            cp.wait_send()

    return start, finish


def _pair_exchange(gs, name):
    n = len(gs)

    def body(*refs):
        ins, outs = refs[:n], refs[n:2 * n]
        ssem, rsem = refs[2 * n:]
        xi, yi, ci = _pos()
        sends = []
        for a in range(n):
            r2 = gs[a].shape[1] // 2
            src = ins[a].at[:, pl.ds(pl.multiple_of((1 - ci) * r2, 8), r2), :]
            cp = _remote(src, outs[a], ssem.at[a], rsem.at[a], (xi, yi, 1 - ci))
            cp.start()
            sends.append(cp)
        for cp in sends:
            cp.wait_recv()
        for cp in sends:
            cp.wait_send()

    return pl.pallas_call(
        body, in_specs=[_HBM] * n, out_specs=[_HBM] * n,
        out_shape=[jax.ShapeDtypeStruct((g.shape[0], g.shape[1] // 2, g.shape[2]), g.dtype) for g in gs],
        scratch_shapes=[pltpu.SemaphoreType.DMA((n,)), pltpu.SemaphoreType.DMA((n,))], name=name,
    )(*gs)


def _row_tile(r):
    return max(t for t in range(8, min(r, 256) + 1, 8) if r % t == 0)


def _pair_add(g, r, cidx, name):
    nk, r2, c = r.shape
    tr = _row_tile(r2)
    nt = r2 // tr

    def body(c_ref, g_ref, r_ref, o_ref):
        o_ref[...] = g_ref[...] + r_ref[...]

    return pl.pallas_call(
        body, grid_spec=pltpu.PrefetchScalarGridSpec(
            num_scalar_prefetch=1, grid=(nk, nt),
            in_specs=[pl.BlockSpec((None, tr, c), lambda k, i, cr: (k, cr[0] * nt + i, 0)),
                      pl.BlockSpec((None, tr, c), lambda k, i, cr: (k, i, 0))],
            out_specs=pl.BlockSpec((None, tr, c), lambda k, i, cr: (k, i, 0))),
        out_shape=jax.ShapeDtypeStruct(r.shape, _F32),
        compiler_params=_cparams(("arbitrary", "arbitrary")), name=name,
    )(cidx, g, r)


def _chip_exchange(ss):
    n = len(ss)

    def body(*refs):
        start, finish = _chip_reduce_plan(refs[:n], refs[n:2 * n], *refs[2 * n:])
        start()
        finish()

    return pl.pallas_call(
        body, in_specs=[_HBM] * n, out_specs=[_HBM] * n,
        out_shape=[jax.ShapeDtypeStruct((3,) + s.shape[1:], s.dtype) for s in ss],
        scratch_shapes=_chip_reduce_sems(n), name="grad_chip_exchange",
    )(*ss)


def _chip_add(s, r, kidx, name):
    _, r2, c = r.shape
    tr = _row_tile(r2)

    def body(k_ref, s_ref, r_ref, o_ref):
        o_ref[...] = ((s_ref[...] + r_ref[0]) + r_ref[1]) + r_ref[2]

    return pl.pallas_call(
        body, grid_spec=pltpu.PrefetchScalarGridSpec(
            num_scalar_prefetch=1, grid=(r2 // tr,),
            in_specs=[pl.BlockSpec((None, tr, c), lambda i, kr: (kr[0], i, 0)),
                      pl.BlockSpec((3, tr, c), lambda i, kr: (0, i, 0))],
            out_specs=pl.BlockSpec((tr, c), lambda i, kr: (i, 0))),
        out_shape=jax.ShapeDtypeStruct((r2, c), _F32),
        compiler_params=_cparams(("arbitrary",)), name=name,
    )(kidx, s, r)


def _pair_share(hs):
    n = len(hs)

    def body(*refs):
        ins, outs = refs[:n], refs[n:2 * n]
        ssem, rsem, lsem = refs[2 * n:]
        xi, yi, ci = _pos()
        local, sends, recvs = [], [], []
        for a in range(n):
            r2 = hs[a].shape[0]
            mine = outs[a].at[pl.ds(pl.multiple_of(ci * r2, 8), r2), :]
            theirs = outs[a].at[pl.ds(pl.multiple_of((1 - ci) * r2, 8), r2), :]
            local.append(pltpu.make_async_copy(ins[a], mine, lsem.at[a]))
            sends.append(_remote(ins[a], mine, ssem.at[a], rsem.at[a], (xi, yi, 1 - ci)))
            recvs.append(_remote(ins[a], theirs, ssem.at[a], rsem.at[a], (xi, yi, ci)))
        for cp in local + sends:
            cp.start()
        for cp in recvs:
            cp.wait_recv()
        for cp in sends:
            cp.wait_send()
        for cp in local:
            cp.wait()

    return pl.pallas_call(
        body, in_specs=[_HBM] * n, out_specs=[_HBM] * n,
        out_shape=[jax.ShapeDtypeStruct((2 * h.shape[0], h.shape[1]), h.dtype) for h in hs],
        scratch_shapes=[pltpu.SemaphoreType.DMA((n,)), pltpu.SemaphoreType.DMA((n,)), pltpu.SemaphoreType.DMA((n,))],
        name="grad_pair_share",
    )(*hs)


def _small_allreduce(pack):
    r, c = pack.shape

    def body(p_ref, all_ref, sum_ref, ssem, rsem):
        xi, yi, ci = _pos()
        me = 4 * xi + 2 * yi + ci
        all_ref[me] = p_ref[...]
        sends = []
        for p, (dx, dy, dc) in enumerate(_DEV_DELTAS):
            cp = _remote(p_ref, all_ref.at[me], ssem.at[p], rsem.at[p], (_flip(xi, dx), _flip(yi, dy), _flip(ci, dc)))
            cp.start()
            sends.append(cp)
        for p, (dx, dy, dc) in enumerate(_DEV_DELTAS):
            src = 4 * _flip(xi, dx) + 2 * _flip(yi, dy) + _flip(ci, dc)
            _remote(p_ref, all_ref.at[src], ssem.at[p], rsem.at[p], (xi, yi, ci)).wait_recv()
        for cp in sends:
            cp.wait_send()
        tot = all_ref[0]
        for dev in range(1, 8):
            tot = tot + all_ref[dev]
        sum_ref[...] = tot

    return pl.pallas_call(
        body, in_specs=[_VMEM], out_specs=[_VMEM, _VMEM],
        out_shape=[jax.ShapeDtypeStruct((8, r, c), _F32), jax.ShapeDtypeStruct((r, c), _F32)],
        scratch_shapes=[pltpu.SemaphoreType.DMA((7,)), pltpu.SemaphoreType.DMA((7,))],
        compiler_params=pltpu.CompilerParams(vmem_limit_bytes=_VMEM_LIMIT), name="small_allreduce",
    )(pack)


def _adamw(w, g, m, v):
    m = _ADAM_B1 * m + (1.0 - _ADAM_B1) * g
    v = _ADAM_B2 * v + (1.0 - _ADAM_B2) * (g * g)
    m_hat = m / (1.0 - _ADAM_B1 ** _ADAM_STEP)
    v_hat = v / (1.0 - _ADAM_B2 ** _ADAM_STEP)
    return -_ADAM_LR * (m_hat / (jnp.sqrt(v_hat) + _ADAM_EPS) + _ADAM_WD * w), m, v


def _adamw_big(w, g, m, v, name):
    r, c = w.shape
    tr = _row_tile(r)

    def body(w_ref, g_ref, m_ref, v_ref, d_ref, mo_ref, vo_ref):
        d_ref[...], mo_ref[...], vo_ref[...] = _adamw(w_ref[...], g_ref[...], m_ref[...], v_ref[...])

    spec = pl.BlockSpec((tr, c), lambda i: (i, 0))
    return pl.pallas_call(
        body, grid=(r // tr,), in_specs=[spec] * 4, out_specs=[spec] * 3,
        out_shape=[jax.ShapeDtypeStruct((r, c), _F32)] * 3,
        compiler_params=_cparams(("arbitrary",)), name=name,
    )(w, g, m, v)


def _adamw_small(ws, gs, ms, vs):
    n = len(ws)
    summed = [i for i in range(n) if gs[i].shape != ws[i].shape]

    def body(*refs):
        w_r, g_r, m_r, v_r = (refs[i * n:(i + 1) * n] for i in range(4))
        outs = refs[4 * n:]
        for i in range(n):
            g = g_r[i][...]
            if i in summed:
                g = _rowsum(g)
                outs[3 * n + summed.index(i)][...] = g
            outs[i][...], outs[n + i][...], outs[2 * n + i][...] = _adamw(w_r[i][...], g, m_r[i][...], v_r[i][...])

    shapes = [jax.ShapeDtypeStruct(w.shape, _F32) for w in ws]
    res = pl.pallas_call(
        body, in_specs=[_VMEM] * (4 * n), out_specs=[_VMEM] * (3 * n + len(summed)),
        out_shape=shapes * 3 + [shapes[i] for i in summed],
        compiler_params=pltpu.CompilerParams(vmem_limit_bytes=_VMEM_LIMIT), name="adamw_small",
    )(*ws, *gs, *ms, *vs)
    gs = list(gs)
    for pos, i in enumerate(summed):
        gs[i] = res[3 * n + pos]
    return gs, res[:n], res[n:2 * n], res[2 * n:3 * n]


def _ada_bwd(c_all, dmod_k, w, m, v):
    d, nk = w.shape
    tn = 512 if nk % 512 == 0 else nk

    def body(c_ref, dm_ref, w_ref, m_ref, v_ref, g_ref, d_ref, mo_ref, vo_ref):
        ca = c_ref[...]
        g = _dot_tn(_mx(ca * jax.nn.sigmoid(ca)), _mx(dm_ref[...]))
        g_ref[...] = g
        d_ref[...], mo_ref[...], vo_ref[...] = _adamw(w_ref[...], g, m_ref[...], v_ref[...])

    col = pl.BlockSpec((d, tn), lambda j: (0, j))
    return pl.pallas_call(
        body, grid=(nk // tn,),
        in_specs=[pl.BlockSpec(c_all.shape, lambda j: (0, 0)), pl.BlockSpec((c_all.shape[0], tn), lambda j: (0, j)), col, col, col],
        out_specs=[col] * 4, out_shape=[jax.ShapeDtypeStruct((d, nk), _F32)] * 4,
        compiler_params=_cparams(("arbitrary",)), name="ada_bwd",
    )(c_all, dmod_k, w, m, v)


def _block_diag(wh):
    hn, dh, _ = wh.shape
    eye = jnp.eye(hn, dtype=wh.dtype)
    return (eye[:, None, :, None] * wh[:, :, None, :]).reshape(hn * dh, hn * dh)


def _pack(pieces):
    out = []
    for p in pieces:
        flat = p.reshape(-1, 128)
        out.append(jnp.pad(flat, ((0, (-flat.shape[0]) % 8), (0, 0))))
    return jnp.concatenate(out, axis=0)


def _unpack(pack, shapes):
    out, off = [], 0
    for shp in shapes:
        rows = math.prod(shp) // 128
        out.append(pack[..., off:off + rows, :].reshape(pack.shape[:-2] + tuple(shp)))
        off += rows + (-rows) % 8
    return out


_WEIGHTS = ('w_ada', 'b_ada', 'w_in', 'lru_conv_w', 'lru_conv_b', 'lru_w_r', 'lru_b_r', 'lru_w_i', 'lru_b_i', 'lru_lambda',
            'conv_w', 'conv_b', 'conv_norm_g', 'conv_norm_b', 'w_out', 'ln1_g', 'ln1_b', 'ffn_w_up', 'ffn_conv_w',
            'ffn_conv_b', 'ffn_w_down', 'ln2_g', 'ln2_b')
_BIG = ('w_in', 'w_out', 'ffn_w_up', 'ffn_w_down')


def kernel(x, c, w_ada, b_ada, w_in, lru_conv_w, lru_conv_b, lru_w_r, lru_b_r, lru_w_i, lru_b_i, lru_lambda, conv_w, conv_b, conv_norm_g, conv_norm_b, w_out, ln1_g, ln1_b, ffn_w_up, ffn_conv_w, ffn_conv_b, ffn_w_down, ln2_g, ln2_b, loss_target, m_w_ada, m_b_ada, m_w_in, m_lru_conv_w, m_lru_conv_b, m_lru_w_r, m_lru_b_r, m_lru_w_i, m_lru_b_i, m_lru_lambda, m_conv_w, m_conv_b, m_conv_norm_g, m_conv_norm_b, m_w_out, m_ln1_g, m_ln1_b, m_ffn_w_up, m_ffn_conv_w, m_ffn_conv_b, m_ffn_w_down, m_ln2_g, m_ln2_b, v_w_ada, v_b_ada, v_w_in, v_lru_conv_w, v_lru_conv_b, v_lru_w_r, v_lru_b_r, v_lru_w_i, v_lru_b_i, v_lru_lambda, v_conv_w, v_conv_b, v_conv_norm_g, v_conv_norm_b, v_w_out, v_ln1_g, v_ln1_b, v_ffn_w_up, v_ffn_conv_w, v_ffn_conv_b, v_ffn_w_down, v_ln2_g, v_ln2_b):
    given = dict(locals())
    wt = {n: given[n] for n in _WEIGHTS}
    mo = {n: given["m_" + n] for n in _WEIGHTS}
    vo = {n: given["v_" + n] for n in _WEIGHTS}
    bl, s_len, d = x.shape
    wd = d // 2
    tokens = bl * s_len
    xi, yi, ci = _pos()
    kme = 2 * xi + yi
    kidx = jnp.reshape(kme, (1,)).astype(jnp.int32)
    cidx = jnp.reshape(ci, (1,)).astype(jnp.int32)

    nk = w_ada.shape[2]
    c8 = jnp.pad(c, ((0, 8 - bl), (0, 0)))
    c_all, mod8 = _ada_fwd(c8, w_ada[0], lax.dynamic_slice(b_ada, (0, kme * nk), (1, nk)))
    mod3 = mod8[:bl].reshape(bl, 1, 6 * d)

    win, wout_s, lcw_s, cw_s, fcw_s = _wgather([_mx(w_in[0]), _mx(w_out[0]), lru_conv_w[0], conv_w[0], ffn_conv_w[0]])
    wout = wout_s.reshape(d, d)
    f = 4 * ffn_w_down.shape[1]
    unshard = lambda t: jnp.transpose(t, (1, 0, 2)).reshape(t.shape[1], -1)
    lcw, cw, fcw = unshard(lcw_s), unshard(cw_s), unshard(fcw_s)
    wr_bd, wi_bd = _mx(_block_diag(lru_w_r[0])), _mx(_block_diag(lru_w_i[0]))
    seg = _block_diag(jnp.ones((_N_HEADS, wd // _N_HEADS, wd // _N_HEADS), jnp.bfloat16))
    mixer_small = (lcw, lru_conv_b, wr_bd, wi_bd, lru_b_r, lru_b_i, lru_lambda, cw, conv_b, conv_norm_g, conv_norm_b, seg, wout, ln1_g)

    proj, h, mix, x1, u1, y, wup, wdn_s = _mix_fwd(x, mod3, win, *mixer_small, ln1_b, [_mx(ffn_w_up[0]), _mx(ffn_w_down[0])])
    wdn = wdn_s.reshape(f, d)
    u2, hh, fact, dz2, loss_acc, dln2, dgt2 = _ffn_fwd(x1, mod3, wup, fcw, ffn_conv_b, wdn, ln2_g, ln2_b, loss_target)
    dx1, dy2, dh, dfc, dmod2 = _ffn_bwd(dz2, x1, hh, mod3, wup, wdn, fcw, ffn_conv_b)
    loss = lax.psum(0.5 * loss_acc[0, 0] / d, ("x", "y", "c"))

    flat = lambda t: t.reshape(tokens, t.shape[-1])
    fc = wup.shape[2]
    pair_sum = lambda gs, names: [_pair_add(g, r, cidx, "grad_pair_add_" + n)
                                  for g, r, n in zip(gs, _pair_exchange(gs, "grad_pair_exchange_" + names[0]), names)]
    g_up = _wgrad(flat(u2), flat(dh), d, fc, 1, 4, 0, 0, "wgrad_up")
    g_dn = _wgrad(flat(fact), flat(dy2), fc, d, f // fc, 1, 0, 0, "wgrad_down").reshape(4, f // 4, d)
    ffn_sum = pair_sum([g_up, g_dn], _BIG[2:])
    grad_x, dproj, dmix, xcg, vecw, dlcw, dcw, dln1, dmod1, *ffn_recv = _mix_bwd(
        dx1, x, mix, proj, h, mod3, win, *mixer_small, ffn_sum)
    g_in = _wgrad(flat(u1), flat(dproj), d, wd, 1, 4, 0, 0, "wgrad_in")
    g_out = _wgrad(flat(y), flat(dmix), d, d, 1, 1, 0, 0, "wgrad_out").reshape(4, d // 4, d)
    g_ri = _wgrad(flat(xcg), flat(xcg), wd, wd, 1, 2, 0, 1, "wgrad_gates")
    dh_ = wd // _N_HEADS
    g_ri = jnp.stack([jnp.stack([g_ri[i, hd * dh_:(hd + 1) * dh_, hd * dh_:(hd + 1) * dh_] for hd in range(_N_HEADS)])
                      for i in range(2)])
    mix_sum = pair_sum([g_in, g_out], _BIG[:2])
    chip_sum, recv = mix_sum + ffn_sum, list(_chip_exchange(mix_sum)) + list(ffn_recv)
    half = [_chip_add(s, r, kidx, "grad_chip_add_" + n) for s, r, n in zip(chip_sum, recv, _BIG)]
    grads, deltas, new_m, new_v = {}, {}, {}, {}
    for n, g in zip(_BIG, _pair_share(half)):
        dl, mm, vv = _adamw_big(wt[n][0], g, mo[n][0], vo[n][0], "adamw_" + n)
        grads[n], deltas[n], new_m[n], new_v[n] = g[None], dl[None], mm[None], vv[None]

    dmod = jnp.concatenate([dmod1.reshape(bl, 3 * d), dmod2.reshape(bl, 2 * d), dgt2.reshape(bl, d)], axis=1)
    pieces = [vecw, dlcw, dcw, jnp.concatenate([dln1, dln2], axis=0), dfc, g_ri, jnp.pad(dmod, ((0, 8 - bl), (0, 0)))]
    shapes = [p.shape for p in pieces]
    every, total = _small_allreduce(_pack(pieces))
    vecw, dlcw, dcw, dln, dfc, g_ri, dmod_sum = _unpack(total, shapes)
    dmod_all = _unpack(every, shapes)[-1].reshape(64, 6 * d)

    g_ada, dl, mm, vv = _ada_bwd(c_all, lax.dynamic_slice(dmod_all, (0, kme * nk), (64, nk)), w_ada[0], m_w_ada[0], v_w_ada[0])
    grads['w_ada'], deltas['w_ada'], new_m['w_ada'], new_v['w_ada'] = g_ada[None], dl[None], mm[None], vv[None]

    shard = lambda t, width: lax.dynamic_slice(t, (0, kme * width), (t.shape[0], width))
    small = {
        'b_ada': dmod_sum, 'lru_conv_w': shard(dlcw, wd // 4), 'lru_conv_b': vecw[0:1], 'lru_w_r': g_ri[0], 'lru_b_r': vecw[1:2],
        'lru_w_i': g_ri[1], 'lru_b_i': vecw[2:3], 'lru_lambda': vecw[3:4], 'conv_w': shard(dcw, wd // 4), 'conv_b': vecw[4:5],
        'conv_norm_g': vecw[5:6], 'conv_norm_b': vecw[6:7], 'ln1_g': dln[0:1], 'ln1_b': dln[1:2],
        'ffn_conv_w': shard(dfc[0:3], f // 4), 'ffn_conv_b': dfc[3:4], 'ln2_g': dln[2:3], 'ln2_b': dln[3:4]}
    names = list(small)
    gs = [small[n] if n == 'b_ada' else small[n].reshape(wt[n].shape) for n in names]
    gs, dls, mms, vvs = _adamw_small([wt[n] for n in names], gs, [mo[n] for n in names], [vo[n] for n in names])
    for n, g, dl, mm, vv in zip(names, gs, dls, mms, vvs):
        grads[n], deltas[n], new_m[n], new_v[n] = g, dl, mm, vv

    return (loss, grad_x, *[grads[n] for n in _WEIGHTS], *[deltas[n] for n in _WEIGHTS],
            *[new_m[n] for n in _WEIGHTS], *[new_v[n] for n in _WEIGHTS])
```

```python
import functools
import itertools
import math

import jax
import jax.numpy as jnp
from jax import lax
from jax.experimental import pallas as pl
from jax.experimental.pallas import tpu as pltpu

_MXU_DT = jnp.bfloat16
_F32 = jnp.float32
_VMEM_LIMIT = 56 * 1024 * 1024
_TT_MIX = 256
_TT_FFN = 256
_TK_WGRAD = 512
_HALO = 32

_LRU_C = 8.0
_LN_EPS = 1e-5
_N_HEADS = 8
_DEPTH = 1
_ALPHA = (2 * _DEPTH) ** 0.25
_ADAM_LR, _ADAM_B1, _ADAM_B2, _ADAM_EPS, _ADAM_WD, _ADAM_STEP = 0.001, 0.9, 0.999, 1e-08, 0.01, 10

_MESH = pl.DeviceIdType.MESH
_CHIP_DELTAS = ((1, 0), (0, 1), (1, 1))


def _cparams(sem):
    return pltpu.CompilerParams(dimension_semantics=sem, vmem_limit_bytes=_VMEM_LIMIT)


def _resident(shape):
    nd = len(shape)
    return pl.BlockSpec(shape, lambda *_: (0,) * nd, pipeline_mode=pl.Buffered(1))


def _dot(a, b):
    return jnp.dot(a, b, preferred_element_type=_F32)


def _dot_nt(a, b):
    return lax.dot_general(a, b, (((1,), (1,)), ((), ())), preferred_element_type=_F32)


def _dot_tn(a, b):
    return lax.dot_general(a, b, (((0,), (0,)), ((), ())), preferred_element_type=_F32)


def _mx(v):
    return v.astype(_MXU_DT)


def _expm1(v):
    series = v * (1.0 + v * (1.0 / 2 + v * (1.0 / 6 + v * (1.0 / 24 + v * (1.0 / 120)))))
    return jnp.where(jnp.abs(v) < 0.0625, series, jnp.exp(v) - 1.0)


def _softplus(z):
    e = jnp.exp(-jnp.abs(z))
    u = 1.0 + e
    log1p = jnp.where(u == 1.0, e, jnp.log(u) * e / jnp.where(u == 1.0, 1.0, u - 1.0))
    return jnp.maximum(z, 0.0) + log1p


_GELU_C = math.sqrt(2.0 / math.pi)


def _gelu_and_grad(v):
    t = jnp.tanh(_GELU_C * (v + 0.044715 * v * v * v))
    val = 0.5 * v * (1.0 + t)
    grad = 0.5 * (1.0 + t) + 0.5 * v * (1.0 - t * t) * _GELU_C * (1.0 + 3 * 0.044715 * v * v)
    return val, grad


def _seg_sum(v, seg, passes=3):
    hi = v.astype(jnp.bfloat16)
    r1 = v - hi.astype(_F32)
    mid = r1.astype(jnp.bfloat16)
    out = _dot(hi, seg) + _dot(mid, seg)
    if passes == 3:
        out = out + _dot((r1 - mid.astype(_F32)).astype(jnp.bfloat16), seg)
    return out


def _scan_fwd(a, u, h0):
    n = a.shape[0]
    row = lax.broadcasted_iota(jnp.int32, a.shape, 0)
    h, d = u, 1
    while d < n:
        keep = row >= d
        h = a * jnp.where(keep, pltpu.roll(h, d, 0), 0.0) + h
        a = a * jnp.where(keep, pltpu.roll(a, d, 0), 1.0)
        d *= 2
    return h + a * h0


def _scan_rev(c, g, g_end):
    n = c.shape[0]
    row = lax.broadcasted_iota(jnp.int32, c.shape, 0)
    d = 1
    while d < n:
        keep = row < n - d
        g = c * jnp.where(keep, pltpu.roll(g, n - d, 0), 0.0) + g
        c = c * jnp.where(keep, pltpu.roll(c, n - d, 0), 1.0)
        d *= 2
    return g + c * g_end


def _layer_norm_stats(z):
    mu = jnp.mean(z, axis=-1, keepdims=True)
    zc = z - mu
    var = jnp.mean(zc * zc, axis=-1, keepdims=True)
    rstd = lax.rsqrt(var + _LN_EPS)
    return zc * rstd, rstd


def _layer_norm_bwd(dn, n, rstd):
    return rstd * (dn - jnp.mean(dn, axis=-1, keepdims=True) - n * jnp.mean(dn * n, axis=-1, keepdims=True))


def _rowsum(v):
    return jnp.sum(v, axis=0, keepdims=True)


def _fused_exchange(body, n_in, n_out, n_scratch, n_xin, n_xout, plan, grid):
    def wrapped(*refs):
        o0 = n_in + n_xin
        s0 = o0 + n_out + n_xout
        start, finish = plan(refs[n_in:o0], refs[o0 + n_out:s0], *refs[s0 + n_scratch:])
        step = pl.program_id(0) * grid[1] + pl.program_id(1)

        @pl.when(step == 0)
        def _():
            start()

        body(*refs[:n_in], *refs[o0:o0 + n_out], *refs[s0:s0 + n_scratch])

        @pl.when(step == grid[0] * grid[1] - 1)
        def _():
            finish()

    return wrapped


def _lru_gates(xc, wr_ref, wi_ref, br_ref, bi_ref, lam_ref):
    xcb = _mx(xc)
    r = jax.nn.sigmoid(_dot(xcb, wr_ref[...]) + br_ref[...])
    i = jax.nn.sigmoid(_dot(xcb, wi_ref[...]) + bi_ref[...])
    sp = _softplus(-lam_ref[...])
    log_a = -_LRU_C * r * sp
    a = jnp.exp(log_a)
    mult = jnp.sqrt(-_expm1(2.0 * log_a))
    return r, i, sp, a, mult


def _conv_taps(ext_ref, w_ref, first, n_taps, tt):
    acc = w_ref[0:1, :] * ext_ref[pl.ds(first, tt), :]
    for k in range(1, n_taps):
        acc = acc + w_ref[k:k + 1, :] * ext_ref[pl.ds(first + k, tt), :]
    return acc


def _make_shifted(ext_ref, sh_ref):
    n = sh_ref.shape[1]
    for r in range(1, 8):
        sh_ref[r - 1] = ext_ref[pl.ds(r, n), :]


def _tap(ext_ref, sh_ref, off, tt):
    base = (off // 8) * 8
    if off % 8 == 0:
        return ext_ref[pl.ds(base, tt), :]
    return sh_ref[off % 8 - 1, pl.ds(base, tt), :]


def _conv_taps_shifted(ext_ref, sh_ref, w_ref, first, n_taps, tt):
    acc = w_ref[0:1, :] * _tap(ext_ref, sh_ref, first, tt)
    for k in range(1, n_taps):
        acc = acc + w_ref[k:k + 1, :] * _tap(ext_ref, sh_ref, first + k, tt)
    return acc


def _mix_fwd(x, mod3, win, lcw, lcb, wr_bd, wi_bd, b_r, b_i, lam, cw, cb, ng, nb, seg, wout, ln1g, ln1b, shards):
    bl, s_len, d = x.shape
    w = d // 2
    tt = min(_TT_MIX, s_len)
    ns = s_len // tt
    kc = cw.shape[0]

    def body(x_ref, mod_ref, win_ref, lcw_ref, lcb_ref, wr_ref, wi_ref, br_ref, bi_ref, lam_ref, cw_ref, cb_ref,
             ng_ref, nb_ref, seg_ref, wout_ref, g1_ref, b1_ref,
             proj_ref, h_ref, mix_ref, x1_ref, u1_ref, y_ref, vbc_ref, ext4, ext31, sh31, hcar):
        @pl.when(pl.program_id(1) == 0)
        def _():
            ext4[0:8, :] = jnp.zeros((8, w), _F32)
            ext31[0:_HALO, :] = jnp.zeros((_HALO, w), _F32)
            hcar[...] = jnp.zeros_like(hcar)

        xt = x_ref[...]
        sh1, sc1, gt1 = mod_ref[:, 0:d], mod_ref[:, d:2 * d], mod_ref[:, 2 * d:3 * d]
        u1 = _mx(xt * (1.0 + sc1) + sh1)
        u1_ref[...] = u1
        xa, ga, vb, gb = (_dot(u1, win_ref[k]) for k in range(4))
        proj_ref[:, 0:w] = xa
        proj_ref[:, w:2 * w] = ga
        proj_ref[:, 2 * w:3 * w] = vb
        proj_ref[:, 3 * w:4 * w] = gb

        ext4[8:8 + tt, :] = xa
        xc = lcb_ref[...] + _conv_taps(ext4, lcw_ref, 5, 4, tt)
        ext4[0:8, :] = xa[tt - 8:tt, :]
        r, i, sp, a, mult = _lru_gates(xc, wr_ref, wi_ref, br_ref, bi_ref, lam_ref)
        h = _scan_fwd(a, mult * (i * xc), hcar[0:1, :])
        hcar[0:1, :] = h[tt - 1:tt, :]
        h_ref[...] = h
        gelu, _ = _gelu_and_grad(ga)
        y_ref[:, 0:w] = _mx(gelu * h)

        vbg = vb * jax.nn.sigmoid(gb)
        ext31[_HALO:_HALO + tt, :] = vbg
        _make_shifted(ext31, sh31)
        vbc = cb_ref[...] + _conv_taps_shifted(ext31, sh31, cw_ref, _HALO - (kc - 1), kc, tt)
        vbc_ref[...] = vbc
        ext31[0:_HALO, :] = vbg[tt - _HALO:tt, :]
        inv = 1.0 / (w // _N_HEADS)
        zc = vbc - _seg_sum(vbc, seg_ref[...]) * inv
        n = zc * lax.rsqrt(_seg_sum(zc * zc, seg_ref[...]) * inv + _LN_EPS)
        pre = n * ng_ref[...] + nb_ref[...]
        y_ref[:, w:2 * w] = _mx(pre * jax.nn.sigmoid(pre))

        mix = _dot(y_ref[...], wout_ref[...])
        mix_ref[...] = mix
        n1, _ = _layer_norm_stats(_ALPHA * xt + (1.0 + gt1) * mix)
        x1_ref[...] = n1 * g1_ref[...] + b1_ref[...]

    tok = lambda c: pl.BlockSpec((None, tt, c), lambda b, s: (b, s, 0))
    smalls = [lcw, lcb, wr_bd, wi_bd, b_r, b_i, lam, cw, cb, ng, nb, seg, wout, ln1g, ln1b]
    nx = len(shards)
    return pl.pallas_call(
        _fused_exchange(body, 3 + len(smalls), 7, 4, nx, nx, _gather_plan, (bl, ns)), grid=(bl, ns),
        in_specs=[tok(d), pl.BlockSpec((None, 1, 6 * d), lambda b, s: (b, 0, 0)), _resident(win.shape)]
        + [_resident(t.shape) for t in smalls] + [_HBM] * nx,
        out_specs=[tok(4 * w), tok(w), tok(d), tok(d), tok(d), tok(d), tok(w)] + [_HBM] * nx,
        out_shape=[jax.ShapeDtypeStruct((bl, s_len, 4 * w), _F32), jax.ShapeDtypeStruct((bl, s_len, w), _F32),
                   jax.ShapeDtypeStruct((bl, s_len, d), _F32), jax.ShapeDtypeStruct((bl, s_len, d), _F32),
                   jax.ShapeDtypeStruct((bl, s_len, d), _MXU_DT), jax.ShapeDtypeStruct((bl, s_len, d), _MXU_DT),
                   jax.ShapeDtypeStruct((bl, s_len, w), _F32)]
        + [jax.ShapeDtypeStruct((4,) + t.shape, t.dtype) for t in shards],
        scratch_shapes=[pltpu.VMEM((tt + 8, w), _F32), pltpu.VMEM((tt + _HALO, w), _F32),
                        pltpu.VMEM((7, tt + _HALO - 8, w), _F32), pltpu.VMEM((8, w), _F32)] + _gather_sems(nx),
        compiler_params=_cparams(("arbitrary", "arbitrary")), name="mix_fwd",
    )(x, mod3, win, *smalls, *shards)


def _ffn_fwd(x1, mod3, wup, fcw, fcb, wdn, ln2g, ln2b, target):
    bl, s_len, d = x1.shape
    nch, _, fc = wup.shape
    nch //= 2
    f = nch * fc
    tt = min(_TT_FFN, s_len)
    ns = s_len // tt

    def body(x1_ref, mod_ref, wup_ref, fcw_ref, fcb_ref, wdn_ref, g2_ref, b2_ref, tgt_ref,
             u2_ref, hh_ref, f_ref, dz2_ref, loss_ref, dln2_ref, dgt2_ref, ext3):
        first_tile = pl.program_id(1) == 0

        @pl.when(first_tile)
        def _():
            ext3[:, 0:8, :] = jnp.zeros((nch, 8, fc), _F32)
            dgt2_ref[...] = jnp.zeros_like(dgt2_ref)

        @pl.when(first_tile & (pl.program_id(0) == 0))
        def _():
            loss_ref[...] = jnp.zeros_like(loss_ref)
            dln2_ref[...] = jnp.zeros_like(dln2_ref)

        x1t = x1_ref[...]
        sh2, sc2, gt2 = mod_ref[:, 3 * d:4 * d], mod_ref[:, 4 * d:5 * d], mod_ref[:, 5 * d:6 * d]
        u2 = _mx(x1t * (1.0 + sc2) + sh2)
        u2_ref[...] = u2
        y2 = jnp.zeros((tt, d), _F32)
        for j in range(nch):
            lanes = slice(j * fc, (j + 1) * fc)
            v = _dot(u2, wup_ref[j])
            g = _dot(u2, wup_ref[nch + j])
            hh_ref[:, lanes] = v.astype(hh_ref.dtype)
            hh_ref[:, f + j * fc:f + (j + 1) * fc] = g.astype(hh_ref.dtype)
            ext = ext3.at[j]
            ext[8:8 + tt, :] = g
            gc = fcb_ref[:, lanes] + sum(fcw_ref[k:k + 1, lanes] * ext[pl.ds(6 + k, tt), :] for k in range(3))
            ext[0:8, :] = g[tt - 8:tt, :]
            fj = _mx(gc * jax.nn.sigmoid(gc) * v)
            f_ref[:, lanes] = fj
            y2 = y2 + _dot(fj, wdn_ref[lanes, :])

        n2, rstd = _layer_norm_stats(_ALPHA * x1t + (1.0 + gt2) * y2)
        err = n2 * g2_ref[...] + b2_ref[...] - tgt_ref[...]
        loss_ref[...] += jnp.sum(_rowsum(err * err), axis=1, keepdims=True)
        dout = err * (1.0 / d)
        dln2_ref[0:1, :] += _rowsum(dout * n2)
        dln2_ref[1:2, :] += _rowsum(dout)
        dz2 = _layer_norm_bwd(dout * g2_ref[...], n2, rstd)
        dz2_ref[...] = dz2
        dgt2_ref[...] += _rowsum(dz2 * y2)

    tok = lambda c: pl.BlockSpec((None, tt, c), lambda b, s: (b, s, 0))
    acc = lambda r: pl.BlockSpec((r, d), lambda b, s: (0, 0))
    smalls = [fcw, fcb, wdn, ln2g, ln2b]
    return pl.pallas_call(
        body, grid=(bl, ns),
        in_specs=[tok(d), pl.BlockSpec((None, 1, 6 * d), lambda b, s: (b, 0, 0)), _resident(wup.shape)]
        + [_resident(t.shape) for t in smalls] + [tok(d)],
        out_specs=[tok(d), tok(2 * f), tok(f), tok(d), acc(1), acc(2), pl.BlockSpec((None, 1, d), lambda b, s: (b, 0, 0))],
        out_shape=[jax.ShapeDtypeStruct((bl, s_len, d), _MXU_DT), jax.ShapeDtypeStruct((bl, s_len, 2 * f), _F32),
                   jax.ShapeDtypeStruct((bl, s_len, f), _MXU_DT), jax.ShapeDtypeStruct((bl, s_len, d), _F32),
                   jax.ShapeDtypeStruct((1, d), _F32), jax.ShapeDtypeStruct((2, d), _F32),
                   jax.ShapeDtypeStruct((bl, 1, d), _F32)],
        scratch_shapes=[pltpu.VMEM((nch, tt + 8, fc), _F32)],
        compiler_params=_cparams(("arbitrary", "arbitrary")), name="ffn_fwd",
    )(x1, mod3, wup, *smalls, target)


def _ffn_bwd(dz2, x1, hh, mod3, wup, wdn, fcw, fcb):
    bl, s_len, d = x1.shape
    nch, _, fc = wup.shape
    nch //= 2
    f = nch * fc
    tt = min(_TT_FFN, s_len)
    ns = s_len // tt

    def body(dz2_ref, x1_ref, hh_ref, halo_ref, mod_ref, wup_ref, wdn_ref, fcw_ref, fcb_ref,
             dx1_ref, dy2_ref, dh_ref, dfc_ref, dmod_ref, gext, dext, dcar):
        s = ns - 1 - pl.program_id(1)

        @pl.when(pl.program_id(1) == 0)
        def _():
            dcar[...] = jnp.zeros_like(dcar)
            dmod_ref[...] = jnp.zeros_like(dmod_ref)

        @pl.when((pl.program_id(1) == 0) & (pl.program_id(0) == 0))
        def _():
            dfc_ref[...] = jnp.zeros_like(dfc_ref)

        sc2, gt2 = mod_ref[:, 4 * d:5 * d], mod_ref[:, 5 * d:6 * d]
        dz2t = dz2_ref[...]
        dy2 = _mx((1.0 + gt2) * dz2t)
        dy2_ref[...] = dy2
        du2 = jnp.zeros((tt, d), _F32)
        for j in range(nch):
            lanes = slice(j * fc, (j + 1) * fc)
            glanes = slice(f + j * fc, f + (j + 1) * fc)
            v = hh_ref[:, lanes].astype(_F32)
            g = hh_ref[:, glanes].astype(_F32)
            gext[0:8, :] = jnp.where(s == 0, 0.0, halo_ref[:, glanes].astype(_F32))
            gext[8:8 + tt, :] = g
            gc = fcb_ref[:, lanes] + sum(fcw_ref[k:k + 1, lanes] * gext[pl.ds(6 + k, tt), :] for k in range(3))
            sg = jax.nn.sigmoid(gc)
            df = _dot_nt(dy2, wdn_ref[lanes, :])
            dv = df * (gc * sg)
            dgc = df * v * (sg * (1.0 + gc * (1.0 - sg)))
            dfc_ref[3:4, lanes] += _rowsum(dgc)
            dext[0:tt, :] = dgc
            dext[tt:tt + 8, :] = dcar[j]
            dcar[j] = dgc[0:8, :]
            dg = jnp.zeros((tt, fc), _F32)
            for k in range(3):
                dg = dg + fcw_ref[k:k + 1, lanes] * dext[pl.ds(2 - k, tt), :]
                dfc_ref[k:k + 1, lanes] += _rowsum(dgc * gext[pl.ds(6 + k, tt), :])
            dvb, dgb = _mx(dv), _mx(dg)
            dh_ref[:, lanes] = dvb
            dh_ref[:, glanes] = dgb
            du2 = du2 + _dot_nt(dvb, wup_ref[j]) + _dot_nt(dgb, wup_ref[nch + j])

        dx1_ref[...] = _ALPHA * dz2t + du2 * (1.0 + sc2)
        dmod_ref[0:1, :] += _rowsum(du2)
        dmod_ref[1:2, :] += _rowsum(du2 * x1_ref[...])

    tok = lambda c: pl.BlockSpec((None, tt, c), lambda b, i: (b, ns - 1 - i, 0))
    halo = pl.BlockSpec((None, 8, 2 * f), lambda b, i: (b, jnp.maximum((ns - 1 - i) * (tt // 8) - 1, 0), 0))
    return pl.pallas_call(
        body, grid=(bl, ns),
        in_specs=[tok(d), tok(d), tok(2 * f), halo, pl.BlockSpec((None, 1, 6 * d), lambda b, i: (b, 0, 0)),
                  _resident(wup.shape), _resident(wdn.shape), _resident(fcw.shape), _resident(fcb.shape)],
        out_specs=[tok(d), tok(d), tok(2 * f), pl.BlockSpec((4, f), lambda b, i: (0, 0)),
                   pl.BlockSpec((None, 2, d), lambda b, i: (b, 0, 0))],
        out_shape=[jax.ShapeDtypeStruct((bl, s_len, d), _F32), jax.ShapeDtypeStruct((bl, s_len, d), _MXU_DT),
                   jax.ShapeDtypeStruct((bl, s_len, 2 * f), _MXU_DT), jax.ShapeDtypeStruct((4, f), _F32),
                   jax.ShapeDtypeStruct((bl, 2, d), _F32)],
        scratch_shapes=[pltpu.VMEM((tt + 8, fc), _F32), pltpu.VMEM((tt + 8, fc), _F32), pltpu.VMEM((nch, 8, fc), _F32)],
        compiler_params=_cparams(("arbitrary", "arbitrary")), name="ffn_bwd",
    )(dz2, x1, hh, hh, mod3, wup, wdn, fcw, fcb)


def _mix_bwd(dx1, x, mix, proj, h, vbc, mod3, win, lcw, lcb, wr_bd, wi_bd, b_r, b_i, lam, cw, cb, ng, nb, seg, wout, ln1g, chip_sums):
    bl, s_len, d = x.shape
    w = d // 2
    tt = min(_TT_MIX, s_len)
    ns = s_len // tt
    kc = cw.shape[0]

    def body(dx1_ref, x_ref, mix_ref, proj_ref, phalo_ref, h_ref, hhalo_ref, vbc_ref, mod_ref, win_ref, lcw_ref, lcb_ref,
             wr_ref, wi_ref, br_ref, bi_ref, lam_ref, cw_ref, cb_ref, ng_ref, nb_ref, seg_ref, wout_ref, g1_ref,
             gx_ref, dproj_ref, dmix_ref, xcg_ref, vecw_ref, dlcw_ref, dcw_ref, dln1_ref, dmod_ref,
             ext4, ext31, dext4, dext31, sh31, dsh31, car4, car31, gcar):
        s = ns - 1 - pl.program_id(1)
        first = s == 0

        @pl.when(pl.program_id(1) == 0)
        def _():
            car4[...] = jnp.zeros_like(car4)
            car31[...] = jnp.zeros_like(car31)
            gcar[...] = jnp.zeros_like(gcar)
            dmod_ref[...] = jnp.zeros_like(dmod_ref)

        @pl.when((pl.program_id(1) == 0) & (pl.program_id(0) == 0))
        def _():
            for ref in (vecw_ref, dlcw_ref, dcw_ref, dln1_ref):
                ref[...] = jnp.zeros_like(ref)

        xt, mixt = x_ref[...], mix_ref[...]
        sh1, sc1, gt1 = mod_ref[:, 0:d], mod_ref[:, d:2 * d], mod_ref[:, 2 * d:3 * d]

        n1, rstd1 = _layer_norm_stats(_ALPHA * xt + (1.0 + gt1) * mixt)
        dx1t = dx1_ref[...]
        dln1_ref[0:1, :] += _rowsum(dx1t * n1)
        dln1_ref[1:2, :] += _rowsum(dx1t)
        dz1 = _layer_norm_bwd(dx1t * g1_ref[...], n1, rstd1)
        dmod_ref[2:3, :] += _rowsum(dz1 * mixt)
        dmix = _mx((1.0 + gt1) * dz1)
        dmix_ref[...] = dmix
        dya = _dot_nt(dmix, wout_ref[0:w, :])
        dyb = _dot_nt(dmix, wout_ref[w:2 * w, :])

        xa, ga = proj_ref[:, 0:w], proj_ref[:, w:2 * w]
        vb, gb = proj_ref[:, 2 * w:3 * w], proj_ref[:, 3 * w:4 * w]

        sgb = jax.nn.sigmoid(gb)
        vbg = vb * sgb
        hv, hg = phalo_ref[:, 2 * w:3 * w], phalo_ref[:, 3 * w:4 * w]
        ext31[0:_HALO, :] = jnp.where(first, 0.0, hv * jax.nn.sigmoid(hg))
        ext31[_HALO:_HALO + tt, :] = vbg
        _make_shifted(ext31, sh31)
        vbc = vbc_ref[...]
        inv = 1.0 / (w // _N_HEADS)
        zc = vbc - _seg_sum(vbc, seg_ref[...]) * inv
        rstd = lax.rsqrt(_seg_sum(zc * zc, seg_ref[...]) * inv + _LN_EPS)
        n = zc * rstd
        pre = n * ng_ref[...] + nb_ref[...]
        sgp = jax.nn.sigmoid(pre)
        dpre = dyb * (sgp * (1.0 + pre * (1.0 - sgp)))
        vecw_ref[5:6, :] += _rowsum(dpre * n)
        vecw_ref[6:7, :] += _rowsum(dpre)
        dn = dpre * ng_ref[...]
        dvbc = rstd * (dn - _seg_sum(dn, seg_ref[...], 2) * inv - n * (_seg_sum(dn * n, seg_ref[...], 2) * inv))
        vecw_ref[4:5, :] += _rowsum(dvbc)
        dext31[0:tt, :] = dvbc
        dext31[tt:tt + _HALO, :] = car31[...]
        car31[...] = dvbc[0:_HALO, :]
        _make_shifted(dext31, dsh31)
        dvbg = jnp.zeros((tt, w), _F32)
        for k in range(kc):
            dvbg = dvbg + cw_ref[k:k + 1, :] * _tap(dext31, dsh31, kc - 1 - k, tt)
            dcw_ref[k:k + 1, :] += _rowsum(dvbc * _tap(ext31, sh31, _HALO - (kc - 1) + k, tt))
        dproj_ref[:, 2 * w:3 * w] = _mx(dvbg * sgb)
        dproj_ref[:, 3 * w:4 * w] = _mx(dvbg * vb * (sgb * (1.0 - sgb)))

        ext4[0:8, :] = jnp.where(first, 0.0, phalo_ref[_HALO - 8:_HALO, 0:w])
        ext4[8:8 + tt, :] = xa
        xc = lcb_ref[...] + _conv_taps(ext4, lcw_ref, 5, 4, tt)
        xcg_ref[:, 0:w] = _mx(xc)
        r, i, sp, a, mult = _lru_gates(xc, wr_ref, wi_ref, br_ref, bi_ref, lam_ref)
        ht = h_ref[...]
        row = lax.broadcasted_iota(jnp.int32, (tt, w), 0)
        h_before = jnp.where(first, 0.0, hhalo_ref[7:8, :])
        hprev = jnp.where(row == 0, h_before, pltpu.roll(ht, 1, 0))
        gelu, dgelu = _gelu_and_grad(ga)
        dproj_ref[:, w:2 * w] = _mx(dya * ht * dgelu)
        dh = dya * gelu
        coef = jnp.where(row == tt - 1, 1.0, pltpu.roll(a, tt - 1, 0))
        big_g = _scan_rev(coef, dh, gcar[0:1, :])
        gcar[0:1, :] = a[0:1, :] * big_g[0:1, :]
        da = big_g * hprev
        ixc = i * xc
        dlog_a = da * a - (big_g * ixc) * (a * a / mult)
        di = big_g * mult * xc
        dxc = big_g * mult * i
        vecw_ref[3:4, :] += _rowsum(dlog_a * r) * (_LRU_C * jax.nn.sigmoid(-lam_ref[...]))
        dgr_f = dlog_a * (-_LRU_C * sp) * (r * (1.0 - r))
        dgi_f = di * (i * (1.0 - i))
        vecw_ref[1:2, :] += _rowsum(dgr_f)
        vecw_ref[2:3, :] += _rowsum(dgi_f)
        dgr, dgi = _mx(dgr_f), _mx(dgi_f)
        xcg_ref[:, w:2 * w] = dgr
        xcg_ref[:, 2 * w:3 * w] = dgi
        dxc = dxc + _dot_nt(dgr, wr_ref[...]) + _dot_nt(dgi, wi_ref[...])
        vecw_ref[0:1, :] += _rowsum(dxc)
        dext4[0:tt, :] = dxc
        dext4[tt:tt + 8, :] = car4[...]
        car4[...] = dxc[0:8, :]
        dxa = jnp.zeros((tt, w), _F32)
        for k in range(4):
            dxa = dxa + lcw_ref[k:k + 1, :] * dext4[pl.ds(3 - k, tt), :]
            dlcw_ref[k:k + 1, :] += _rowsum(dxc * ext4[pl.ds(5 + k, tt), :])
        dproj_ref[:, 0:w] = _mx(dxa)

        du1 = sum(_dot_nt(dproj_ref[:, k * w:(k + 1) * w], win_ref[k]) for k in range(4))
        gx_ref[...] = _ALPHA * dz1 + du1 * (1.0 + sc1)
        dmod_ref[0:1, :] += _rowsum(du1)
        dmod_ref[1:2, :] += _rowsum(du1 * xt)

    tok = lambda c: pl.BlockSpec((None, tt, c), lambda b, i: (b, ns - 1 - i, 0))
    halo = lambda rows, c: pl.BlockSpec(
        (None, rows, c), lambda b, i: (b, jnp.maximum((ns - 1 - i) * (tt // rows) - 1, 0), 0))
    accw = lambda r, c: pl.BlockSpec((r, c), lambda b, i: (0, 0))
    smalls = [lcw, lcb, wr_bd, wi_bd, b_r, b_i, lam, cw, cb, ng, nb, seg, wout, ln1g]
    nx = len(chip_sums)
    return pl.pallas_call(
        _fused_exchange(body, 10 + len(smalls), 9, 9, nx, nx, _chip_reduce_plan, (bl, ns)), grid=(bl, ns),
        in_specs=[tok(d), tok(d), tok(d), tok(4 * w), halo(_HALO, 4 * w), tok(w), halo(8, w), tok(w),
                  pl.BlockSpec((None, 1, 6 * d), lambda b, i: (b, 0, 0)), _resident(win.shape)]
        + [_resident(t.shape) for t in smalls] + [_HBM] * nx,
        out_specs=[tok(d), tok(4 * w), tok(d), tok(3 * w), accw(8, w), accw(4, w), accw(kc, w), accw(2, d),
                   pl.BlockSpec((None, 3, d), lambda b, i: (b, 0, 0))] + [_HBM] * nx,
        out_shape=[jax.ShapeDtypeStruct((bl, s_len, d), _F32), jax.ShapeDtypeStruct((bl, s_len, 4 * w), _MXU_DT),
                   jax.ShapeDtypeStruct((bl, s_len, d), _MXU_DT), jax.ShapeDtypeStruct((bl, s_len, 3 * w), _MXU_DT),
                   jax.ShapeDtypeStruct((8, w), _F32), jax.ShapeDtypeStruct((4, w), _F32),
                   jax.ShapeDtypeStruct((kc, w), _F32), jax.ShapeDtypeStruct((2, d), _F32),
                   jax.ShapeDtypeStruct((bl, 3, d), _F32)]
        + [jax.ShapeDtypeStruct((3,) + t.shape[1:], t.dtype) for t in chip_sums],
        scratch_shapes=[pltpu.VMEM((tt + 8, w), _F32), pltpu.VMEM((tt + _HALO, w), _F32),
                        pltpu.VMEM((tt + 8, w), _F32), pltpu.VMEM((tt + _HALO, w), _F32),
                        pltpu.VMEM((7, tt + _HALO - 8, w), _F32), pltpu.VMEM((7, tt + _HALO - 8, w), _F32),
                        pltpu.VMEM((8, w), _F32), pltpu.VMEM((_HALO, w), _F32), pltpu.VMEM((8, w), _F32)]
        + _chip_reduce_sems(nx),
        compiler_params=_cparams(("arbitrary", "arbitrary")), name="mix_bwd",
    )(dx1, x, mix, proj, proj, h, h, vbc, mod3, win, *smalls, *chip_sums)


def _wgrad(a, b, ma, nbw, na, nb, a_off, b_off, name):
    t = a.shape[0]
    tk = min(_TK_WGRAD, t)

    def body(a_ref, b_ref, o_ref):
        @pl.when(pl.program_id(1) == 0)
        def _():
            o_ref[...] = jnp.zeros_like(o_ref)
        o_ref[...] += _dot_tn(a_ref[...], b_ref[...])

    return pl.pallas_call(
        body, grid=(na * nb, t // tk),
        in_specs=[pl.BlockSpec((tk, ma), lambda j, k: (k, j // nb + a_off)),
                  pl.BlockSpec((tk, nbw), lambda j, k: (k, j % nb + b_off))],
        out_specs=pl.BlockSpec((None, ma, nbw), lambda j, k: (j, 0, 0)),
        out_shape=jax.ShapeDtypeStruct((na * nb, ma, nbw), _F32),
        compiler_params=_cparams(("arbitrary", "arbitrary")), name=name,
    )(a, b)


_DEV_DELTAS = tuple(dl for dl in itertools.product((0, 1), repeat=3) if any(dl))
_HBM = pl.BlockSpec(memory_space=pltpu.HBM)
_VMEM = pl.BlockSpec(memory_space=pltpu.VMEM)


def _pos():
    return lax.axis_index("x"), lax.axis_index("y"), lax.axis_index("c")


def _flip(v, delta):
    return 1 - v if delta else v


def _remote(src, dst, ssem, rsem, dev):
    return pltpu.make_async_remote_copy(src_ref=src, dst_ref=dst, send_sem=ssem, recv_sem=rsem,
                                        device_id=dev, device_id_type=_MESH)


def _rows(ref, idx, n):
    return ref.at[pl.ds(pl.multiple_of(idx * n, 8), n)]


def _ada_fwd(c8, w_ada_k, b_ada_k):
    rows, d = c8.shape
    nk = w_ada_k.shape[1]

    def body(c_ref, w_ref, b_ref, call_ref, mod_ref, modloc, modrcv, s1, r1, s2, r2):
        xi, yi, ci = _pos()
        me, kme = 4 * xi + 2 * yi + ci, 2 * xi + yi
        call_ref[pl.ds(pl.multiple_of(me * rows, 8), rows), :] = c_ref[...]
        sends = []
        for p, (dx, dy, dc) in enumerate(_DEV_DELTAS):
            cp = _remote(c_ref, _rows(call_ref, me, rows), s1.at[p], r1.at[p], (_flip(xi, dx), _flip(yi, dy), _flip(ci, dc)))
            cp.start()
            sends.append(cp)
        for p, (dx, dy, dc) in enumerate(_DEV_DELTAS):
            src = 4 * _flip(xi, dx) + 2 * _flip(yi, dy) + _flip(ci, dc)
            _remote(c_ref, _rows(call_ref, src, rows), s1.at[p], r1.at[p], (xi, yi, ci)).wait_recv()
        for cp in sends:
            cp.wait_send()

        ca = call_ref[...]
        modloc[...] = _dot(_mx(ca * jax.nn.sigmoid(ca)), _mx(w_ref[...])) + b_ref[...]
        modrcv[kme] = modloc[pl.ds(pl.multiple_of(me * rows, 8), rows), :]
        sends = []
        for j, (dx, dy) in enumerate(_CHIP_DELTAS):
            tx, ty = _flip(xi, dx), _flip(yi, dy)
            cp = _remote(_rows(modloc, 4 * tx + 2 * ty + ci, rows), modrcv.at[kme], s2.at[j], r2.at[j], (tx, ty, ci))
            cp.start()
            sends.append(cp)
        for j, (dx, dy) in enumerate(_CHIP_DELTAS):
            ksrc = 2 * _flip(xi, dx) + _flip(yi, dy)
            _remote(_rows(modloc, me, rows), modrcv.at[ksrc], s2.at[j], r2.at[j], (xi, yi, ci)).wait_recv()
        for cp in sends:
            cp.wait_send()
        for j in range(4):
            mod_ref[:, j * nk:(j + 1) * nk] = modrcv[j]

    return pl.pallas_call(
        body, in_specs=[_VMEM, _VMEM, _VMEM], out_specs=[_VMEM, _VMEM],
        out_shape=[jax.ShapeDtypeStruct((8 * rows, d), _F32), jax.ShapeDtypeStruct((rows, 4 * nk), _F32)],
        scratch_shapes=[pltpu.VMEM((8 * rows, nk), _F32), pltpu.VMEM((4, rows, nk), _F32),
                        pltpu.SemaphoreType.DMA((7,)), pltpu.SemaphoreType.DMA((7,)),
                        pltpu.SemaphoreType.DMA((3,)), pltpu.SemaphoreType.DMA((3,))],
        compiler_params=pltpu.CompilerParams(vmem_limit_bytes=_VMEM_LIMIT), name="ada_fwd",
    )(c8, w_ada_k, b_ada_k)


def _wgather(arrs):
    n = len(arrs)

    def body(*refs):
        start, finish = _gather_plan(refs[:n], refs[n:2 * n], *refs[2 * n:])
        start()
        finish()

    return pl.pallas_call(
        body, in_specs=[_HBM] * n, out_specs=[_HBM] * n,
        out_shape=[jax.ShapeDtypeStruct((4,) + a.shape, a.dtype) for a in arrs],
        scratch_shapes=_gather_sems(n), name="wgather",
    )(*arrs)


def _gather_sems(n):
    return [pltpu.SemaphoreType.DMA((3, n)), pltpu.SemaphoreType.DMA((3, n)), pltpu.SemaphoreType.DMA((n,))]


def _gather_plan(ins, outs, ssem, rsem, lsem):
    n = len(ins)
    xi, yi, ci = _pos()
    kme = 2 * xi + yi
    local = [pltpu.make_async_copy(ins[a], outs[a].at[kme], lsem.at[a]) for a in range(n)]
    sends, recvs = [], []
    for j, (dx, dy) in enumerate(_CHIP_DELTAS):
        tx, ty = _flip(xi, dx), _flip(yi, dy)
        for a in range(n):
            sends.append(_remote(ins[a], outs[a].at[kme], ssem.at[j, a], rsem.at[j, a], (tx, ty, ci)))
            recvs.append(_remote(ins[a], outs[a].at[2 * tx + ty], ssem.at[j, a], rsem.at[j, a], (xi, yi, ci)))

    def start():
        for cp in local + sends:
            cp.start()

    def finish():
        for cp in recvs:
            cp.wait_recv()
        for cp in sends:
            cp.wait_send()
        for cp in local:
            cp.wait()

    return start, finish


def _chip_reduce_sems(n):
    return [pltpu.SemaphoreType.DMA((3, n)), pltpu.SemaphoreType.DMA((3, n))]


def _chip_reduce_plan(ins, outs, ssem, rsem):
    xi, yi, ci = _pos()
    sends = []
    for j, (dx, dy) in enumerate(_CHIP_DELTAS):
        tx, ty = _flip(xi, dx), _flip(yi, dy)
        sends += [_remote(ins[a].at[2 * tx + ty], outs[a].at[j], ssem.at[j, a], rsem.at[j, a], (tx, ty, ci))
                  for a in range(len(ins))]

    def start():
        for cp in sends:
            cp.start()

    def finish():
        for cp in sends:
            cp.wait_recv()
        for cp in sends:
            cp.wait_send()

    return start, finish


def _pair_exchange(gs, name):
    n = len(gs)

    def body(*refs):
        ins, outs = refs[:n], refs[n:2 * n]
        ssem, rsem = refs[2 * n:]
        xi, yi, ci = _pos()
        sends = []
        for a in range(n):
            r2 = gs[a].shape[1] // 2
            src = ins[a].at[:, pl.ds(pl.multiple_of((1 - ci) * r2, 8), r2), :]
            cp = _remote(src, outs[a], ssem.at[a], rsem.at[a], (xi, yi, 1 - ci))
            cp.start()
            sends.append(cp)
        for cp in sends:
            cp.wait_recv()
        for cp in sends:
            cp.wait_send()

    return pl.pallas_call(
        body, in_specs=[_HBM] * n, out_specs=[_HBM] * n,
        out_shape=[jax.ShapeDtypeStruct((g.shape[0], g.shape[1] // 2, g.shape[2]), g.dtype) for g in gs],
        scratch_shapes=[pltpu.SemaphoreType.DMA((n,)), pltpu.SemaphoreType.DMA((n,))], name=name,
    )(*gs)


def _row_tile(r):
    return max(t for t in range(8, min(r, 256) + 1, 8) if r % t == 0)


def _pair_add(g, r, cidx, name):
    nk, r2, c = r.shape
    tr = _row_tile(r2)
    nt = r2 // tr

    def body(c_ref, g_ref, r_ref, o_ref):
        o_ref[...] = g_ref[...] + r_ref[...]

    return pl.pallas_call(
        body, grid_spec=pltpu.PrefetchScalarGridSpec(
            num_scalar_prefetch=1, grid=(nk, nt),
            in_specs=[pl.BlockSpec((None, tr, c), lambda k, i, cr: (k, cr[0] * nt + i, 0)),
                      pl.BlockSpec((None, tr, c), lambda k, i, cr: (k, i, 0))],
            out_specs=pl.BlockSpec((None, tr, c), lambda k, i, cr: (k, i, 0))),
        out_shape=jax.ShapeDtypeStruct(r.shape, _F32),
        compiler_params=_cparams(("arbitrary", "arbitrary")), name=name,
    )(cidx, g, r)


def _chip_exchange(ss):
    n = len(ss)

    def body(*refs):
        start, finish = _chip_reduce_plan(refs[:n], refs[n:2 * n], *refs[2 * n:])
        start()
        finish()

    return pl.pallas_call(
        body, in_specs=[_HBM] * n, out_specs=[_HBM] * n,
        out_shape=[jax.ShapeDtypeStruct((3,) + s.shape[1:], s.dtype) for s in ss],
        scratch_shapes=_chip_reduce_sems(n), name="grad_chip_exchange",
    )(*ss)


def _chip_add(s, r, kidx, name):
    _, r2, c = r.shape
    tr = _row_tile(r2)

    def body(k_ref, s_ref, r_ref, o_ref):
        o_ref[...] = ((s_ref[...] + r_ref[0]) + r_ref[1]) + r_ref[2]

    return pl.pallas_call(
        body, grid_spec=pltpu.PrefetchScalarGridSpec(
            num_scalar_prefetch=1, grid=(r2 // tr,),
            in_specs=[pl.BlockSpec((None, tr, c), lambda i, kr: (kr[0], i, 0)),
                      pl.BlockSpec((3, tr, c), lambda i, kr: (0, i, 0))],
            out_specs=pl.BlockSpec((tr, c), lambda i, kr: (i, 0))),
        out_shape=jax.ShapeDtypeStruct((r2, c), _F32),
        compiler_params=_cparams(("arbitrary",)), name=name,
    )(kidx, s, r)


def _pair_swap(hs):
    n = len(hs)

    def body(*refs):
        ins, outs = refs[:n], refs[n:2 * n]
        ssem, rsem = refs[2 * n:]
        xi, yi, ci = _pos()
        sends = [_remote(ins[a], outs[a], ssem.at[a], rsem.at[a], (xi, yi, 1 - ci)) for a in range(n)]
        for cp in sends:
            cp.start()
        for cp in sends:
            cp.wait_recv()
        for cp in sends:
            cp.wait_send()

    return pl.pallas_call(
        body, in_specs=[_HBM] * n, out_specs=[_HBM] * n,
        out_shape=[jax.ShapeDtypeStruct(h.shape, h.dtype) for h in hs],
        scratch_shapes=[pltpu.SemaphoreType.DMA((n,)), pltpu.SemaphoreType.DMA((n,))], name="grad_pair_swap",
    )(*hs)


def _small_allreduce(pack):
    r, c = pack.shape

    def body(p_ref, all_ref, sum_ref, ssem, rsem):
        xi, yi, ci = _pos()
        me = 4 * xi + 2 * yi + ci
        all_ref[me] = p_ref[...]
        sends = []
        for p, (dx, dy, dc) in enumerate(_DEV_DELTAS):
            cp = _remote(p_ref, all_ref.at[me], ssem.at[p], rsem.at[p], (_flip(xi, dx), _flip(yi, dy), _flip(ci, dc)))
            cp.start()
            sends.append(cp)
        for p, (dx, dy, dc) in enumerate(_DEV_DELTAS):
            src = 4 * _flip(xi, dx) + 2 * _flip(yi, dy) + _flip(ci, dc)
            _remote(p_ref, all_ref.at[src], ssem.at[p], rsem.at[p], (xi, yi, ci)).wait_recv()
        for cp in sends:
            cp.wait_send()
        tot = all_ref[0]
        for dev in range(1, 8):
            tot = tot + all_ref[dev]
        sum_ref[...] = tot

    return pl.pallas_call(
        body, in_specs=[_VMEM], out_specs=[_VMEM, _VMEM],
        out_shape=[jax.ShapeDtypeStruct((8, r, c), _F32), jax.ShapeDtypeStruct((r, c), _F32)],
        scratch_shapes=[pltpu.SemaphoreType.DMA((7,)), pltpu.SemaphoreType.DMA((7,))],
        compiler_params=pltpu.CompilerParams(vmem_limit_bytes=_VMEM_LIMIT), name="small_allreduce",
    )(pack)


def _adamw(w, g, m, v):
    m = _ADAM_B1 * m + (1.0 - _ADAM_B1) * g
    v = _ADAM_B2 * v + (1.0 - _ADAM_B2) * (g * g)
    m_hat = m / (1.0 - _ADAM_B1 ** _ADAM_STEP)
    v_hat = v / (1.0 - _ADAM_B2 ** _ADAM_STEP)
    return -_ADAM_LR * (m_hat / (jnp.sqrt(v_hat) + _ADAM_EPS) + _ADAM_WD * w), m, v


def _adamw_big(w, g_mine, g_theirs, m, v, cidx, name):
    r, c = w.shape
    tr = _row_tile(r // 2)
    nt = r // 2 // tr

    def body(c_ref, w_ref, gm_ref, gt_ref, m_ref, v_ref, g_ref, d_ref, mo_ref, vo_ref):
        g = jnp.where(pl.program_id(0) // nt == c_ref[0], gm_ref[...], gt_ref[...])
        g_ref[...] = g
        d_ref[...], mo_ref[...], vo_ref[...] = _adamw(w_ref[...], g, m_ref[...], v_ref[...])

    spec = pl.BlockSpec((tr, c), lambda i, cr: (i, 0))
    half = pl.BlockSpec((tr, c), lambda i, cr: (i % nt, 0))
    return pl.pallas_call(
        body, grid_spec=pltpu.PrefetchScalarGridSpec(
            num_scalar_prefetch=1, grid=(2 * nt,), in_specs=[spec, half, half, spec, spec], out_specs=[spec] * 4),
        out_shape=[jax.ShapeDtypeStruct((r, c), _F32)] * 4,
        compiler_params=_cparams(("arbitrary",)), name=name,
    )(cidx, w, g_mine, g_theirs, m, v)


def _adamw_small(ws, gs, ms, vs):
    n = len(ws)
    summed = [i for i in range(n) if gs[i].shape != ws[i].shape]

    def body(*refs):
        w_r, g_r, m_r, v_r = (refs[i * n:(i + 1) * n] for i in range(4))
        outs = refs[4 * n:]
        for i in range(n):
            g = g_r[i][...]
            if i in summed:
                g = _rowsum(g)
                outs[3 * n + summed.index(i)][...] = g
            outs[i][...], outs[n + i][...], outs[2 * n + i][...] = _adamw(w_r[i][...], g, m_r[i][...], v_r[i][...])

    shapes = [jax.ShapeDtypeStruct(w.shape, _F32) for w in ws]
    res = pl.pallas_call(
        body, in_specs=[_VMEM] * (4 * n), out_specs=[_VMEM] * (3 * n + len(summed)),
        out_shape=shapes * 3 + [shapes[i] for i in summed],
        compiler_params=pltpu.CompilerParams(vmem_limit_bytes=_VMEM_LIMIT), name="adamw_small",
    )(*ws, *gs, *ms, *vs)
    gs = list(gs)
    for pos, i in enumerate(summed):
        gs[i] = res[3 * n + pos]
    return gs, res[:n], res[n:2 * n], res[2 * n:3 * n]


def _ada_bwd(c_all, dmod_k, w, m, v):
    d, nk = w.shape
    tn = 512 if nk % 512 == 0 else nk

    def body(c_ref, dm_ref, w_ref, m_ref, v_ref, g_ref, d_ref, mo_ref, vo_ref):
        ca = c_ref[...]
        g = _dot_tn(_mx(ca * jax.nn.sigmoid(ca)), _mx(dm_ref[...]))
        g_ref[...] = g
        d_ref[...], mo_ref[...], vo_ref[...] = _adamw(w_ref[...], g, m_ref[...], v_ref[...])

    col = pl.BlockSpec((d, tn), lambda j: (0, j))
    return pl.pallas_call(
        body, grid=(nk // tn,),
        in_specs=[pl.BlockSpec(c_all.shape, lambda j: (0, 0)), pl.BlockSpec((c_all.shape[0], tn), lambda j: (0, j)), col, col, col],
        out_specs=[col] * 4, out_shape=[jax.ShapeDtypeStruct((d, nk), _F32)] * 4,
        compiler_params=_cparams(("arbitrary",)), name="ada_bwd",
    )(c_all, dmod_k, w, m, v)


def _block_diag(wh):
    hn, dh, _ = wh.shape
    eye = jnp.eye(hn, dtype=wh.dtype)
    return (eye[:, None, :, None] * wh[:, :, None, :]).reshape(hn * dh, hn * dh)


def _pack(pieces):
    out = []
    for p in pieces:
        flat = p.reshape(-1, 128)
        out.append(jnp.pad(flat, ((0, (-flat.shape[0]) % 8), (0, 0))))
    return jnp.concatenate(out, axis=0)


def _unpack(pack, shapes):
    out, off = [], 0
    for shp in shapes:
        rows = math.prod(shp) // 128
        out.append(pack[..., off:off + rows, :].reshape(pack.shape[:-2] + tuple(shp)))
        off += rows + (-rows) % 8
    return out


_WEIGHTS = ('w_ada', 'b_ada', 'w_in', 'lru_conv_w', 'lru_conv_b', 'lru_w_r', 'lru_b_r', 'lru_w_i', 'lru_b_i', 'lru_lambda',
            'conv_w', 'conv_b', 'conv_norm_g', 'conv_norm_b', 'w_out', 'ln1_g', 'ln1_b', 'ffn_w_up', 'ffn_conv_w',
            'ffn_conv_b', 'ffn_w_down', 'ln2_g', 'ln2_b')
_BIG = ('w_in', 'w_out', 'ffn_w_up', 'ffn_w_down')


def kernel(x, c, w_ada, b_ada, w_in, lru_conv_w, lru_conv_b, lru_w_r, lru_b_r, lru_w_i, lru_b_i, lru_lambda, conv_w, conv_b, conv_norm_g, conv_norm_b, w_out, ln1_g, ln1_b, ffn_w_up, ffn_conv_w, ffn_conv_b, ffn_w_down, ln2_g, ln2_b, loss_target, m_w_ada, m_b_ada, m_w_in, m_lru_conv_w, m_lru_conv_b, m_lru_w_r, m_lru_b_r, m_lru_w_i, m_lru_b_i, m_lru_lambda, m_conv_w, m_conv_b, m_conv_norm_g, m_conv_norm_b, m_w_out, m_ln1_g, m_ln1_b, m_ffn_w_up, m_ffn_conv_w, m_ffn_conv_b, m_ffn_w_down, m_ln2_g, m_ln2_b, v_w_ada, v_b_ada, v_w_in, v_lru_conv_w, v_lru_conv_b, v_lru_w_r, v_lru_b_r, v_lru_w_i, v_lru_b_i, v_lru_lambda, v_conv_w, v_conv_b, v_conv_norm_g, v_conv_norm_b, v_w_out, v_ln1_g, v_ln1_b, v_ffn_w_up, v_ffn_conv_w, v_ffn_conv_b, v_ffn_w_down, v_ln2_g, v_ln2_b):
    given = dict(locals())
    wt = {n: given[n] for n in _WEIGHTS}
    mo = {n: given["m_" + n] for n in _WEIGHTS}
    vo = {n: given["v_" + n] for n in _WEIGHTS}
    bl, s_len, d = x.shape
    wd = d // 2
    tokens = bl * s_len
    xi, yi, ci = _pos()
    kme = 2 * xi + yi
    kidx = jnp.reshape(kme, (1,)).astype(jnp.int32)
    cidx = jnp.reshape(ci, (1,)).astype(jnp.int32)

    nk = w_ada.shape[2]
    c8 = jnp.pad(c, ((0, 8 - bl), (0, 0)))
    c_all, mod8 = _ada_fwd(c8, w_ada[0], lax.dynamic_slice(b_ada, (0, kme * nk), (1, nk)))
    mod3 = mod8[:bl].reshape(bl, 1, 6 * d)

    win, wout_s, lcw_s, cw_s, fcw_s = _wgather([_mx(w_in[0]), _mx(w_out[0]), lru_conv_w[0], conv_w[0], ffn_conv_w[0]])
    wout = wout_s.reshape(d, d)
    f = 4 * ffn_w_down.shape[1]
    unshard = lambda t: jnp.transpose(t, (1, 0, 2)).reshape(t.shape[1], -1)
    lcw, cw, fcw = unshard(lcw_s), unshard(cw_s), unshard(fcw_s)
    wr_bd, wi_bd = _mx(_block_diag(lru_w_r[0])), _mx(_block_diag(lru_w_i[0]))
    seg = _block_diag(jnp.ones((_N_HEADS, wd // _N_HEADS, wd // _N_HEADS), jnp.bfloat16))
    mixer_small = (lcw, lru_conv_b, wr_bd, wi_bd, lru_b_r, lru_b_i, lru_lambda, cw, conv_b, conv_norm_g, conv_norm_b, seg, wout, ln1_g)

    proj, h, mix, x1, u1, y, vbc, wup, wdn_s = _mix_fwd(x, mod3, win, *mixer_small, ln1_b, [_mx(ffn_w_up[0]), _mx(ffn_w_down[0])])
    wdn = wdn_s.reshape(f, d)
    u2, hh, fact, dz2, loss_acc, dln2, dgt2 = _ffn_fwd(x1, mod3, wup, fcw, ffn_conv_b, wdn, ln2_g, ln2_b, loss_target)
    dx1, dy2, dh, dfc, dmod2 = _ffn_bwd(dz2, x1, hh, mod3, wup, wdn, fcw, ffn_conv_b)
    loss = lax.psum(0.5 * loss_acc[0, 0] / d, ("x", "y", "c"))

    flat = lambda t: t.reshape(tokens, t.shape[-1])
    fc = wup.shape[2]
    pair_sum = lambda gs, names: [_pair_add(g, r, cidx, "grad_pair_add_" + n)
                                  for g, r, n in zip(gs, _pair_exchange(gs, "grad_pair_exchange_" + names[0]), names)]
    g_up = _wgrad(flat(u2), flat(dh), d, fc, 1, 4, 0, 0, "wgrad_up")
    g_dn = _wgrad(flat(fact), flat(dy2), fc, d, f // fc, 1, 0, 0, "wgrad_down").reshape(4, f // 4, d)
    ffn_sum = pair_sum([g_up, g_dn], _BIG[2:])
    grad_x, dproj, dmix, xcg, vecw, dlcw, dcw, dln1, dmod1, *ffn_recv = _mix_bwd(
        dx1, x, mix, proj, h, vbc, mod3, win, *mixer_small, ffn_sum)
    g_in = _wgrad(flat(u1), flat(dproj), d, wd, 1, 4, 0, 0, "wgrad_in")
    g_out = _wgrad(flat(y), flat(dmix), d, d, 1, 1, 0, 0, "wgrad_out").reshape(4, d // 4, d)
    g_ri = _wgrad(flat(xcg), flat(xcg), wd, wd, 1, 2, 0, 1, "wgrad_gates")
    dh_ = wd // _N_HEADS
    g_ri = jnp.stack([jnp.stack([g_ri[i, hd * dh_:(hd + 1) * dh_, hd * dh_:(hd + 1) * dh_] for hd in range(_N_HEADS)])
                      for i in range(2)])
    mix_sum = pair_sum([g_in, g_out], _BIG[:2])
    chip_sum, recv = mix_sum + ffn_sum, list(_chip_exchange(mix_sum)) + list(ffn_recv)
    half = [_chip_add(s, r, kidx, "grad_chip_add_" + n) for s, r, n in zip(chip_sum, recv, _BIG)]
    grads, deltas, new_m, new_v = {}, {}, {}, {}
    for n, mine, theirs in zip(_BIG, half, _pair_swap(half)):
        g, dl, mm, vv = _adamw_big(wt[n][0], mine, theirs, mo[n][0], vo[n][0], cidx, "adamw_" + n)
        grads[n], deltas[n], new_m[n], new_v[n] = g[None], dl[None], mm[None], vv[None]

    dmod = jnp.concatenate([dmod1.reshape(bl, 3 * d), dmod2.reshape(bl, 2 * d), dgt2.reshape(bl, d)], axis=1)
    pieces = [vecw, dlcw, dcw, jnp.concatenate([dln1, dln2], axis=0), dfc, g_ri, jnp.pad(dmod, ((0, 8 - bl), (0, 0)))]
    shapes = [p.shape for p in pieces]
    every, total = _small_allreduce(_pack(pieces))
    vecw, dlcw, dcw, dln, dfc, g_ri, dmod_sum = _unpack(total, shapes)
    dmod_all = _unpack(every, shapes)[-1].reshape(64, 6 * d)

    g_ada, dl, mm, vv = _ada_bwd(c_all, lax.dynamic_slice(dmod_all, (0, kme * nk), (64, nk)), w_ada[0], m_w_ada[0], v_w_ada[0])
    grads['w_ada'], deltas['w_ada'], new_m['w_ada'], new_v['w_ada'] = g_ada[None], dl[None], mm[None], vv[None]

    shard = lambda t, width: lax.dynamic_slice(t, (0, kme * width), (t.shape[0], width))
    small = {
        'b_ada': dmod_sum, 'lru_conv_w': shard(dlcw, wd // 4), 'lru_conv_b': vecw[0:1], 'lru_w_r': g_ri[0], 'lru_b_r': vecw[1:2],
        'lru_w_i': g_ri[1], 'lru_b_i': vecw[2:3], 'lru_lambda': vecw[3:4], 'conv_w': shard(dcw, wd // 4), 'conv_b': vecw[4:5],
        'conv_norm_g': vecw[5:6], 'conv_norm_b': vecw[6:7], 'ln1_g': dln[0:1], 'ln1_b': dln[1:2],
        'ffn_conv_w': shard(dfc[0:3], f // 4), 'ffn_conv_b': dfc[3:4], 'ln2_g': dln[2:3], 'ln2_b': dln[3:4]}
    names = list(small)
    gs = [small[n] if n == 'b_ada' else small[n].reshape(wt[n].shape) for n in names]
    gs, dls, mms, vvs = _adamw_small([wt[n] for n in names], gs, [mo[n] for n in names], [vo[n] for n in names])
    for n, g, dl, mm, vv in zip(names, gs, dls, mms, vvs):
        grads[n], deltas[n], new_m[n], new_v[n] = g, dl, mm, vv

    return (loss, grad_x, *[grads[n] for n in _WEIGHTS], *[deltas[n] for n in _WEIGHTS],
            *[new_m[n] for n in _WEIGHTS], *[new_v[n] for n in _WEIGHTS])
```

```python
import functools
import itertools
import math

import jax
import jax.numpy as jnp
from jax import lax
from jax.experimental import pallas as pl
from jax.experimental.pallas import tpu as pltpu

_MXU_DT = jnp.bfloat16
_F32 = jnp.float32
_VMEM_LIMIT = 56 * 1024 * 1024
_TT_MIX = 256
_TT_FFN = 256
_TK_WGRAD = 512
_HALO = 32

_LRU_C = 8.0
_LN_EPS = 1e-5
_N_HEADS = 8
_DEPTH = 1
_ALPHA = (2 * _DEPTH) ** 0.25
_ADAM_LR, _ADAM_B1, _ADAM_B2, _ADAM_EPS, _ADAM_WD, _ADAM_STEP = 0.001, 0.9, 0.999, 1e-08, 0.01, 10

_MESH = pl.DeviceIdType.MESH
_CHIP_DELTAS = ((1, 0), (0, 1), (1, 1))


def _cparams(sem):
    return pltpu.CompilerParams(dimension_semantics=sem, vmem_limit_bytes=_VMEM_LIMIT)


def _resident(shape):
    nd = len(shape)
    return pl.BlockSpec(shape, lambda *_: (0,) * nd, pipeline_mode=pl.Buffered(1))


def _dot(a, b):
    return jnp.dot(a, b, preferred_element_type=_F32)


def _dot_nt(a, b):
    return lax.dot_general(a, b, (((1,), (1,)), ((), ())), preferred_element_type=_F32)


def _dot_tn(a, b):
    return lax.dot_general(a, b, (((0,), (0,)), ((), ())), preferred_element_type=_F32)


def _mx(v):
    return v.astype(_MXU_DT)


def _expm1(v):
    series = v * (1.0 + v * (1.0 / 2 + v * (1.0 / 6 + v * (1.0 / 24 + v * (1.0 / 120)))))
    return jnp.where(jnp.abs(v) < 0.0625, series, jnp.exp(v) - 1.0)


def _softplus(z):
    e = jnp.exp(-jnp.abs(z))
    u = 1.0 + e
    log1p = jnp.where(u == 1.0, e, jnp.log(u) * e / jnp.where(u == 1.0, 1.0, u - 1.0))
    return jnp.maximum(z, 0.0) + log1p


_GELU_C = math.sqrt(2.0 / math.pi)


def _gelu_and_grad(v):
    t = jnp.tanh(_GELU_C * (v + 0.044715 * v * v * v))
    val = 0.5 * v * (1.0 + t)
    grad = 0.5 * (1.0 + t) + 0.5 * v * (1.0 - t * t) * _GELU_C * (1.0 + 3 * 0.044715 * v * v)
    return val, grad


def _seg_sum(v, seg, passes=3):
    hi = v.astype(jnp.bfloat16)
    r1 = v - hi.astype(_F32)
    mid = r1.astype(jnp.bfloat16)
    out = _dot(hi, seg) + _dot(mid, seg)
    if passes == 3:
        out = out + _dot((r1 - mid.astype(_F32)).astype(jnp.bfloat16), seg)
    return out


def _scan_fwd(a, u, h0):
    n = a.shape[0]
    row = lax.broadcasted_iota(jnp.int32, a.shape, 0)
    h, d = u, 1
    while d < n:
        keep = row >= d
        h = a * jnp.where(keep, pltpu.roll(h, d, 0), 0.0) + h
        a = a * jnp.where(keep, pltpu.roll(a, d, 0), 1.0)
        d *= 2
    return h + a * h0


def _scan_rev(c, g, g_end):
    n = c.shape[0]
    row = lax.broadcasted_iota(jnp.int32, c.shape, 0)
    d = 1
    while d < n:
        keep = row < n - d
        g = c * jnp.where(keep, pltpu.roll(g, n - d, 0), 0.0) + g
        c = c * jnp.where(keep, pltpu.roll(c, n - d, 0), 1.0)
        d *= 2
    return g + c * g_end


def _layer_norm_stats(z):
    mu = jnp.mean(z, axis=-1, keepdims=True)
    zc = z - mu
    var = jnp.mean(zc * zc, axis=-1, keepdims=True)
    rstd = lax.rsqrt(var + _LN_EPS)
    return zc * rstd, rstd


def _layer_norm_bwd(dn, n, rstd):
    return rstd * (dn - jnp.mean(dn, axis=-1, keepdims=True) - n * jnp.mean(dn * n, axis=-1, keepdims=True))


def _rowsum(v):
    return jnp.sum(v, axis=0, keepdims=True)


def _fused_exchange(body, n_in, n_out, n_scratch, n_xin, n_xout, plan, grid):
    def wrapped(*refs):
        o0 = n_in + n_xin
        s0 = o0 + n_out + n_xout
        start, finish = plan(refs[n_in:o0], refs[o0 + n_out:s0], *refs[s0 + n_scratch:])
        step = pl.program_id(0) * grid[1] + pl.program_id(1)

        @pl.when(step == 0)
        def _():
            start()

        body(*refs[:n_in], *refs[o0:o0 + n_out], *refs[s0:s0 + n_scratch])

        @pl.when(step == grid[0] * grid[1] - 1)
        def _():
            finish()

    return wrapped


def _lru_gates(xc, wr_ref, wi_ref, br_ref, bi_ref, lam_ref):
    xcb = _mx(xc)
    r = jax.nn.sigmoid(_dot(xcb, wr_ref[...]) + br_ref[...])
    i = jax.nn.sigmoid(_dot(xcb, wi_ref[...]) + bi_ref[...])
    sp = _softplus(-lam_ref[...])
    log_a = -_LRU_C * r * sp
    a = jnp.exp(log_a)
    mult = jnp.sqrt(-_expm1(2.0 * log_a))
    return r, i, sp, a, mult


def _conv_taps(ext_ref, w_ref, first, n_taps, tt):
    acc = w_ref[0:1, :] * ext_ref[pl.ds(first, tt), :]
    for k in range(1, n_taps):
        acc = acc + w_ref[k:k + 1, :] * ext_ref[pl.ds(first + k, tt), :]
    return acc


def _make_shifted(ext_ref, sh_ref):
    n = sh_ref.shape[1]
    for r in range(1, 8):
        sh_ref[r - 1] = ext_ref[pl.ds(r, n), :]


def _tap(ext_ref, sh_ref, off, tt):
    base = (off // 8) * 8
    if off % 8 == 0:
        return ext_ref[pl.ds(base, tt), :]
    return sh_ref[off % 8 - 1, pl.ds(base, tt), :]


def _conv_taps_shifted(ext_ref, sh_ref, w_ref, first, n_taps, tt):
    acc = w_ref[0:1, :] * _tap(ext_ref, sh_ref, first, tt)
    for k in range(1, n_taps):
        acc = acc + w_ref[k:k + 1, :] * _tap(ext_ref, sh_ref, first + k, tt)
    return acc


def _mix_fwd(x, mod3, win, lcw, lcb, wr_bd, wi_bd, b_r, b_i, lam, cw, cb, ng, nb, seg, wout, ln1g, ln1b, shards):
    bl, s_len, d = x.shape
    w = d // 2
    tt = min(_TT_MIX, s_len)
    ns = s_len // tt
    kc = cw.shape[0]

    def body(x_ref, mod_ref, win_ref, lcw_ref, lcb_ref, wr_ref, wi_ref, br_ref, bi_ref, lam_ref, cw_ref, cb_ref,
             ng_ref, nb_ref, seg_ref, wout_ref, g1_ref, b1_ref,
             proj_ref, h_ref, mix_ref, x1_ref, u1_ref, y_ref, vbc_ref, ext4, ext31, sh31, hcar):
        @pl.when(pl.program_id(1) == 0)
        def _():
            ext4[0:8, :] = jnp.zeros((8, w), _F32)
            ext31[0:_HALO, :] = jnp.zeros((_HALO, w), _F32)
            hcar[...] = jnp.zeros_like(hcar)

        xt = x_ref[...]
        sh1, sc1, gt1 = mod_ref[:, 0:d], mod_ref[:, d:2 * d], mod_ref[:, 2 * d:3 * d]
        u1 = _mx(xt * (1.0 + sc1) + sh1)
        u1_ref[...] = u1
        xa, ga, vb, gb = (_dot(u1, win_ref[k]) for k in range(4))
        proj_ref[:, 0:w] = xa
        proj_ref[:, w:2 * w] = ga
        proj_ref[:, 2 * w:3 * w] = vb
        proj_ref[:, 3 * w:4 * w] = gb

        ext4[8:8 + tt, :] = xa
        xc = lcb_ref[...] + _conv_taps(ext4, lcw_ref, 5, 4, tt)
        ext4[0:8, :] = xa[tt - 8:tt, :]
        r, i, sp, a, mult = _lru_gates(xc, wr_ref, wi_ref, br_ref, bi_ref, lam_ref)
        h = _scan_fwd(a, mult * (i * xc), hcar[0:1, :])
        hcar[0:1, :] = h[tt - 1:tt, :]
        h_ref[...] = h
        gelu, _ = _gelu_and_grad(ga)
        y_ref[:, 0:w] = _mx(gelu * h)

        vbg = vb * jax.nn.sigmoid(gb)
        ext31[_HALO:_HALO + tt, :] = vbg
        _make_shifted(ext31, sh31)
        vbc = cb_ref[...] + _conv_taps_shifted(ext31, sh31, cw_ref, _HALO - (kc - 1), kc, tt)
        vbc_ref[...] = vbc
        ext31[0:_HALO, :] = vbg[tt - _HALO:tt, :]
        inv = 1.0 / (w // _N_HEADS)
        zc = vbc - _seg_sum(vbc, seg_ref[...]) * inv
        n = zc * lax.rsqrt(_seg_sum(zc * zc, seg_ref[...]) * inv + _LN_EPS)
        pre = n * ng_ref[...] + nb_ref[...]
        y_ref[:, w:2 * w] = _mx(pre * jax.nn.sigmoid(pre))

        mix = _dot(y_ref[...], wout_ref[...])
        mix_ref[...] = mix
        n1, _ = _layer_norm_stats(_ALPHA * xt + (1.0 + gt1) * mix)
        x1_ref[...] = n1 * g1_ref[...] + b1_ref[...]

    tok = lambda c: pl.BlockSpec((None, tt, c), lambda b, s: (b, s, 0))
    smalls = [lcw, lcb, wr_bd, wi_bd, b_r, b_i, lam, cw, cb, ng, nb, seg, wout, ln1g, ln1b]
    nx = len(shards)
    return pl.pallas_call(
        _fused_exchange(body, 3 + len(smalls), 7, 4, nx, nx, _gather_plan, (bl, ns)), grid=(bl, ns),
        in_specs=[tok(d), pl.BlockSpec((None, 1, 6 * d), lambda b, s: (b, 0, 0)), _resident(win.shape)]
        + [_resident(t.shape) for t in smalls] + [_HBM] * nx,
        out_specs=[tok(4 * w), tok(w), tok(d), tok(d), tok(d), tok(d), tok(w)] + [_HBM] * nx,
        out_shape=[jax.ShapeDtypeStruct((bl, s_len, 4 * w), _F32), jax.ShapeDtypeStruct((bl, s_len, w), _F32),
                   jax.ShapeDtypeStruct((bl, s_len, d), _F32), jax.ShapeDtypeStruct((bl, s_len, d), _F32),
                   jax.ShapeDtypeStruct((bl, s_len, d), _MXU_DT), jax.ShapeDtypeStruct((bl, s_len, d), _MXU_DT),
                   jax.ShapeDtypeStruct((bl, s_len, w), _F32)]
        + [jax.ShapeDtypeStruct((4,) + t.shape, t.dtype) for t in shards],
        scratch_shapes=[pltpu.VMEM((tt + 8, w), _F32), pltpu.VMEM((tt + _HALO, w), _F32),
                        pltpu.VMEM((7, tt + _HALO - 8, w), _F32), pltpu.VMEM((8, w), _F32)] + _gather_sems(nx),
        compiler_params=_cparams(("arbitrary", "arbitrary")), name="mix_fwd",
    )(x, mod3, win, *smalls, *shards)


def _ffn_fwd(x1, mod3, wup, fcw, fcb, wdn, ln2g, ln2b, target):
    bl, s_len, d = x1.shape
    nch, _, fc = wup.shape
    nch //= 2
    f = nch * fc
    tt = min(_TT_FFN, s_len)
    ns = s_len // tt

    def body(x1_ref, mod_ref, wup_ref, fcw_ref, fcb_ref, wdn_ref, g2_ref, b2_ref, tgt_ref,
             u2_ref, hh_ref, f_ref, dz2_ref, loss_ref, dln2_ref, dgt2_ref, ext3):
        first_tile = pl.program_id(1) == 0

        @pl.when(first_tile)
        def _():
            ext3[:, 0:8, :] = jnp.zeros((nch, 8, fc), _F32)
            dgt2_ref[...] = jnp.zeros_like(dgt2_ref)

        @pl.when(first_tile & (pl.program_id(0) == 0))
        def _():
            loss_ref[...] = jnp.zeros_like(loss_ref)
            dln2_ref[...] = jnp.zeros_like(dln2_ref)

        x1t = x1_ref[...]
        sh2, sc2, gt2 = mod_ref[:, 3 * d:4 * d], mod_ref[:, 4 * d:5 * d], mod_ref[:, 5 * d:6 * d]
        u2 = _mx(x1t * (1.0 + sc2) + sh2)
        u2_ref[...] = u2
        y2 = jnp.zeros((tt, d), _F32)
        for j in range(nch):
            lanes = slice(j * fc, (j + 1) * fc)
            v = _dot(u2, wup_ref[j])
            g = _dot(u2, wup_ref[nch + j])
            hh_ref[:, lanes] = v.astype(hh_ref.dtype)
            hh_ref[:, f + j * fc:f + (j + 1) * fc] = g.astype(hh_ref.dtype)
            ext = ext3.at[j]
            ext[8:8 + tt, :] = g
            gc = fcb_ref[:, lanes] + sum(fcw_ref[k:k + 1, lanes] * ext[pl.ds(6 + k, tt), :] for k in range(3))
            ext[0:8, :] = g[tt - 8:tt, :]
            fj = _mx(gc * jax.nn.sigmoid(gc) * v)
            f_ref[:, lanes] = fj
            y2 = y2 + _dot(fj, wdn_ref[lanes, :])

        n2, rstd = _layer_norm_stats(_ALPHA * x1t + (1.0 + gt2) * y2)
        err = n2 * g2_ref[...] + b2_ref[...] - tgt_ref[...]
        loss_ref[...] += jnp.sum(_rowsum(err * err), axis=1, keepdims=True)
        dout = err * (1.0 / d)
        dln2_ref[0:1, :] += _rowsum(dout * n2)
        dln2_ref[1:2, :] += _rowsum(dout)
        dz2 = _layer_norm_bwd(dout * g2_ref[...], n2, rstd)
        dz2_ref[...] = dz2
        dgt2_ref[...] += _rowsum(dz2 * y2)

    tok = lambda c: pl.BlockSpec((None, tt, c), lambda b, s: (b, s, 0))
    acc = lambda r: pl.BlockSpec((r, d), lambda b, s: (0, 0))
    smalls = [fcw, fcb, wdn, ln2g, ln2b]
    return pl.pallas_call(
        body, grid=(bl, ns),
        in_specs=[tok(d), pl.BlockSpec((None, 1, 6 * d), lambda b, s: (b, 0, 0)), _resident(wup.shape)]
        + [_resident(t.shape) for t in smalls] + [tok(d)],
        out_specs=[tok(d), tok(2 * f), tok(f), tok(d), acc(1), acc(2), pl.BlockSpec((None, 1, d), lambda b, s: (b, 0, 0))],
        out_shape=[jax.ShapeDtypeStruct((bl, s_len, d), _MXU_DT), jax.ShapeDtypeStruct((bl, s_len, 2 * f), _F32),
                   jax.ShapeDtypeStruct((bl, s_len, f), _MXU_DT), jax.ShapeDtypeStruct((bl, s_len, d), _F32),
                   jax.ShapeDtypeStruct((1, d), _F32), jax.ShapeDtypeStruct((2, d), _F32),
                   jax.ShapeDtypeStruct((bl, 1, d), _F32)],
        scratch_shapes=[pltpu.VMEM((nch, tt + 8, fc), _F32)],
        compiler_params=_cparams(("arbitrary", "arbitrary")), name="ffn_fwd",
    )(x1, mod3, wup, *smalls, target)


def _ffn_bwd(dz2, x1, hh, mod3, wup, wdn, fcw, fcb):
    bl, s_len, d = x1.shape
    nch, _, fc = wup.shape
    nch //= 2
    f = nch * fc
    tt = min(_TT_FFN, s_len)
    ns = s_len // tt

    def body(dz2_ref, x1_ref, hh_ref, halo_ref, mod_ref, wup_ref, wdn_ref, fcw_ref, fcb_ref,
             dx1_ref, dy2_ref, dh_ref, dfc_ref, dmod_ref, gext, dext, dcar):
        s = ns - 1 - pl.program_id(1)

        @pl.when(pl.program_id(1) == 0)
        def _():
            dcar[...] = jnp.zeros_like(dcar)
            dmod_ref[...] = jnp.zeros_like(dmod_ref)

        @pl.when((pl.program_id(1) == 0) & (pl.program_id(0) == 0))
        def _():
            dfc_ref[...] = jnp.zeros_like(dfc_ref)

        sc2, gt2 = mod_ref[:, 4 * d:5 * d], mod_ref[:, 5 * d:6 * d]
        dz2t = dz2_ref[...]
        dy2 = _mx((1.0 + gt2) * dz2t)
        dy2_ref[...] = dy2
        du2 = jnp.zeros((tt, d), _F32)
        for j in range(nch):
            lanes = slice(j * fc, (j + 1) * fc)
            glanes = slice(f + j * fc, f + (j + 1) * fc)
            v = hh_ref[:, lanes].astype(_F32)
            g = hh_ref[:, glanes].astype(_F32)
            gext[0:8, :] = jnp.where(s == 0, 0.0, halo_ref[:, glanes].astype(_F32))
            gext[8:8 + tt, :] = g
            gc = fcb_ref[:, lanes] + sum(fcw_ref[k:k + 1, lanes] * gext[pl.ds(6 + k, tt), :] for k in range(3))
            sg = jax.nn.sigmoid(gc)
            df = _dot_nt(dy2, wdn_ref[lanes, :])
            dv = df * (gc * sg)
            dgc = df * v * (sg * (1.0 + gc * (1.0 - sg)))
            dfc_ref[3:4, lanes] += _rowsum(dgc)
            dext[0:tt, :] = dgc
            dext[tt:tt + 8, :] = dcar[j]
            dcar[j] = dgc[0:8, :]
            dg = jnp.zeros((tt, fc), _F32)
            for k in range(3):
                dg = dg + fcw_ref[k:k + 1, lanes] * dext[pl.ds(2 - k, tt), :]
                dfc_ref[k:k + 1, lanes] += _rowsum(dgc * gext[pl.ds(6 + k, tt), :])
            dvb, dgb = _mx(dv), _mx(dg)
            dh_ref[:, lanes] = dvb
            dh_ref[:, glanes] = dgb
            du2 = du2 + _dot_nt(dvb, wup_ref[j]) + _dot_nt(dgb, wup_ref[nch + j])

        dx1_ref[...] = _ALPHA * dz2t + du2 * (1.0 + sc2)
        dmod_ref[0:1, :] += _rowsum(du2)
        dmod_ref[1:2, :] += _rowsum(du2 * x1_ref[...])

    tok = lambda c: pl.BlockSpec((None, tt, c), lambda b, i: (b, ns - 1 - i, 0))
    halo = pl.BlockSpec((None, 8, 2 * f), lambda b, i: (b, jnp.maximum((ns - 1 - i) * (tt // 8) - 1, 0), 0))
    return pl.pallas_call(
        body, grid=(bl, ns),
        in_specs=[tok(d), tok(d), tok(2 * f), halo, pl.BlockSpec((None, 1, 6 * d), lambda b, i: (b, 0, 0)),
                  _resident(wup.shape), _resident(wdn.shape), _resident(fcw.shape), _resident(fcb.shape)],
        out_specs=[tok(d), tok(d), tok(2 * f), pl.BlockSpec((4, f), lambda b, i: (0, 0)),
                   pl.BlockSpec((None, 2, d), lambda b, i: (b, 0, 0))],
        out_shape=[jax.ShapeDtypeStruct((bl, s_len, d), _F32), jax.ShapeDtypeStruct((bl, s_len, d), _MXU_DT),
                   jax.ShapeDtypeStruct((bl, s_len, 2 * f), _MXU_DT), jax.ShapeDtypeStruct((4, f), _F32),
                   jax.ShapeDtypeStruct((bl, 2, d), _F32)],
        scratch_shapes=[pltpu.VMEM((tt + 8, fc), _F32), pltpu.VMEM((tt + 8, fc), _F32), pltpu.VMEM((nch, 8, fc), _F32)],
        compiler_params=_cparams(("arbitrary", "arbitrary")), name="ffn_bwd",
    )(dz2, x1, hh, hh, mod3, wup, wdn, fcw, fcb)


def _mix_bwd(dx1, x, mix, proj, h, vbc, mod3, win, lcw, lcb, wr_bd, wi_bd, b_r, b_i, lam, cw, cb, ng, nb, seg, wout, ln1g, chip_sums):
    bl, s_len, d = x.shape
    w = d // 2
    tt = min(_TT_MIX, s_len)
    ns = s_len // tt
    kc = cw.shape[0]

    def body(dx1_ref, x_ref, mix_ref, proj_ref, phalo_ref, h_ref, hhalo_ref, vbc_ref, mod_ref, win_ref, lcw_ref, lcb_ref,
             wr_ref, wi_ref, br_ref, bi_ref, lam_ref, cw_ref, cb_ref, ng_ref, nb_ref, seg_ref, wout_ref, g1_ref,
             gx_ref, dproj_ref, dmix_ref, xcg_ref, vecw_ref, dlcw_ref, dcw_ref, dln1_ref, dmod_ref,
             ext4, ext31, dext4, dext31, sh31, dsh31, car4, car31, gcar):
        s = ns - 1 - pl.program_id(1)
        first = s == 0

        @pl.when(pl.program_id(1) == 0)
        def _():
            car4[...] = jnp.zeros_like(car4)
            car31[...] = jnp.zeros_like(car31)
            gcar[...] = jnp.zeros_like(gcar)
            dmod_ref[...] = jnp.zeros_like(dmod_ref)

        @pl.when((pl.program_id(1) == 0) & (pl.program_id(0) == 0))
        def _():
            for ref in (vecw_ref, dlcw_ref, dcw_ref, dln1_ref):
                ref[...] = jnp.zeros_like(ref)

        xt, mixt = x_ref[...], mix_ref[...]
        sh1, sc1, gt1 = mod_ref[:, 0:d], mod_ref[:, d:2 * d], mod_ref[:, 2 * d:3 * d]

        n1, rstd1 = _layer_norm_stats(_ALPHA * xt + (1.0 + gt1) * mixt)
        dx1t = dx1_ref[...]
        dln1_ref[0:1, :] += _rowsum(dx1t * n1)
        dln1_ref[1:2, :] += _rowsum(dx1t)
        dz1 = _layer_norm_bwd(dx1t * g1_ref[...], n1, rstd1)
        dmod_ref[2:3, :] += _rowsum(dz1 * mixt)
        dmix = _mx((1.0 + gt1) * dz1)
        dmix_ref[...] = dmix
        dya = _dot_nt(dmix, wout_ref[0:w, :])
        dyb = _dot_nt(dmix, wout_ref[w:2 * w, :])

        xa, ga = proj_ref[:, 0:w], proj_ref[:, w:2 * w]
        vb, gb = proj_ref[:, 2 * w:3 * w], proj_ref[:, 3 * w:4 * w]

        sgb = jax.nn.sigmoid(gb)
        vbg = vb * sgb
        hv, hg = phalo_ref[:, 2 * w:3 * w], phalo_ref[:, 3 * w:4 * w]
        ext31[0:_HALO, :] = jnp.where(first, 0.0, hv * jax.nn.sigmoid(hg))
        ext31[_HALO:_HALO + tt, :] = vbg
        _make_shifted(ext31, sh31)
        vbc = vbc_ref[...]
        inv = 1.0 / (w // _N_HEADS)
        zc = vbc - _seg_sum(vbc, seg_ref[...]) * inv
        rstd = lax.rsqrt(_seg_sum(zc * zc, seg_ref[...]) * inv + _LN_EPS)
        n = zc * rstd
        pre = n * ng_ref[...] + nb_ref[...]
        sgp = jax.nn.sigmoid(pre)
        dpre = dyb * (sgp * (1.0 + pre * (1.0 - sgp)))
        vecw_ref[5:6, :] += _rowsum(dpre * n)
        vecw_ref[6:7, :] += _rowsum(dpre)
        dn = dpre * ng_ref[...]
        dvbc = rstd * (dn - _seg_sum(dn, seg_ref[...], 2) * inv - n * (_seg_sum(dn * n, seg_ref[...], 2) * inv))
        vecw_ref[4:5, :] += _rowsum(dvbc)
        dext31[0:tt, :] = dvbc
        dext31[tt:tt + _HALO, :] = car31[...]
        car31[...] = dvbc[0:_HALO, :]
        _make_shifted(dext31, dsh31)
        dvbg = jnp.zeros((tt, w), _F32)
        for k in range(kc):
            dvbg = dvbg + cw_ref[k:k + 1, :] * _tap(dext31, dsh31, kc - 1 - k, tt)
            dcw_ref[k:k + 1, :] += _rowsum(dvbc * _tap(ext31, sh31, _HALO - (kc - 1) + k, tt))
        dproj_ref[:, 2 * w:3 * w] = _mx(dvbg * sgb)
        dproj_ref[:, 3 * w:4 * w] = _mx(dvbg * vb * (sgb * (1.0 - sgb)))

        ext4[0:8, :] = jnp.where(first, 0.0, phalo_ref[_HALO - 8:_HALO, 0:w])
        ext4[8:8 + tt, :] = xa
        xc = lcb_ref[...] + _conv_taps(ext4, lcw_ref, 5, 4, tt)
        xcg_ref[:, 0:w] = _mx(xc)
        r, i, sp, a, mult = _lru_gates(xc, wr_ref, wi_ref, br_ref, bi_ref, lam_ref)
        ht = h_ref[...]
        row = lax.broadcasted_iota(jnp.int32, (tt, w), 0)
        h_before = jnp.where(first, 0.0, hhalo_ref[7:8, :])
        hprev = jnp.where(row == 0, h_before, pltpu.roll(ht, 1, 0))
        gelu, dgelu = _gelu_and_grad(ga)
        dproj_ref[:, w:2 * w] = _mx(dya * ht * dgelu)
        dh = dya * gelu
        coef = jnp.where(row == tt - 1, 1.0, pltpu.roll(a, tt - 1, 0))
        big_g = _scan_rev(coef, dh, gcar[0:1, :])
        gcar[0:1, :] = a[0:1, :] * big_g[0:1, :]
        da = big_g * hprev
        ixc = i * xc
        dlog_a = da * a - (big_g * ixc) * (a * a / mult)
        di = big_g * mult * xc
        dxc = big_g * mult * i
        vecw_ref[3:4, :] += _rowsum(dlog_a * r) * (_LRU_C * jax.nn.sigmoid(-lam_ref[...]))
        dgr_f = dlog_a * (-_LRU_C * sp) * (r * (1.0 - r))
        dgi_f = di * (i * (1.0 - i))
        vecw_ref[1:2, :] += _rowsum(dgr_f)
        vecw_ref[2:3, :] += _rowsum(dgi_f)
        dgr, dgi = _mx(dgr_f), _mx(dgi_f)
        xcg_ref[:, w:2 * w] = dgr
        xcg_ref[:, 2 * w:3 * w] = dgi
        dxc = dxc + _dot_nt(dgr, wr_ref[...]) + _dot_nt(dgi, wi_ref[...])
        vecw_ref[0:1, :] += _rowsum(dxc)
        dext4[0:tt, :] = dxc
        dext4[tt:tt + 8, :] = car4[...]
        car4[...] = dxc[0:8, :]
        dxa = jnp.zeros((tt, w), _F32)
        for k in range(4):
            dxa = dxa + lcw_ref[k:k + 1, :] * dext4[pl.ds(3 - k, tt), :]
            dlcw_ref[k:k + 1, :] += _rowsum(dxc * ext4[pl.ds(5 + k, tt), :])
        dproj_ref[:, 0:w] = _mx(dxa)

        du1 = sum(_dot_nt(dproj_ref[:, k * w:(k + 1) * w], win_ref[k]) for k in range(4))
        gx_ref[...] = _ALPHA * dz1 + du1 * (1.0 + sc1)
        dmod_ref[0:1, :] += _rowsum(du1)
        dmod_ref[1:2, :] += _rowsum(du1 * xt)

    tok = lambda c: pl.BlockSpec((None, tt, c), lambda b, i: (b, ns - 1 - i, 0))
    halo = lambda rows, c: pl.BlockSpec(
        (None, rows, c), lambda b, i: (b, jnp.maximum((ns - 1 - i) * (tt // rows) - 1, 0), 0))
    accw = lambda r, c: pl.BlockSpec((r, c), lambda b, i: (0, 0))
    smalls = [lcw, lcb, wr_bd, wi_bd, b_r, b_i, lam, cw, cb, ng, nb, seg, wout, ln1g]
    nx = len(chip_sums)
    return pl.pallas_call(
        _fused_exchange(body, 10 + len(smalls), 9, 9, nx, nx, _chip_reduce_plan, (bl, ns)), grid=(bl, ns),
        in_specs=[tok(d), tok(d), tok(d), tok(4 * w), halo(_HALO, 4 * w), tok(w), halo(8, w), tok(w),
                  pl.BlockSpec((None, 1, 6 * d), lambda b, i: (b, 0, 0)), _resident(win.shape)]
        + [_resident(t.shape) for t in smalls] + [_HBM] * nx,
        out_specs=[tok(d), tok(4 * w), tok(d), tok(3 * w), accw(8, w), accw(4, w), accw(kc, w), accw(2, d),
                   pl.BlockSpec((None, 3, d), lambda b, i: (b, 0, 0))] + [_HBM] * nx,
        out_shape=[jax.ShapeDtypeStruct((bl, s_len, d), _F32), jax.ShapeDtypeStruct((bl, s_len, 4 * w), _MXU_DT),
                   jax.ShapeDtypeStruct((bl, s_len, d), _MXU_DT), jax.ShapeDtypeStruct((bl, s_len, 3 * w), _MXU_DT),
                   jax.ShapeDtypeStruct((8, w), _F32), jax.ShapeDtypeStruct((4, w), _F32),
                   jax.ShapeDtypeStruct((kc, w), _F32), jax.ShapeDtypeStruct((2, d), _F32),
                   jax.ShapeDtypeStruct((bl, 3, d), _F32)]
        + [jax.ShapeDtypeStruct((3,) + t.shape[1:], t.dtype) for t in chip_sums],
        scratch_shapes=[pltpu.VMEM((tt + 8, w), _F32), pltpu.VMEM((tt + _HALO, w), _F32),
                        pltpu.VMEM((tt + 8, w), _F32), pltpu.VMEM((tt + _HALO, w), _F32),
                        pltpu.VMEM((7, tt + _HALO - 8, w), _F32), pltpu.VMEM((7, tt + _HALO - 8, w), _F32),
                        pltpu.VMEM((8, w), _F32), pltpu.VMEM((_HALO, w), _F32), pltpu.VMEM((8, w), _F32)]
        + _chip_reduce_sems(nx),
        compiler_params=_cparams(("arbitrary", "arbitrary")), name="mix_bwd",
    )(dx1, x, mix, proj, proj, h, h, vbc, mod3, win, *smalls, *chip_sums)


def _wgrad(a, b, ma, nbw, na, nb, a_off, b_off, name, exchange=None):
    t = a.shape[0]
    tk = min(_TK_WGRAD, t)
    grid = (na * nb, t // tk)

    def body(a_ref, b_ref, o_ref):
        @pl.when(pl.program_id(1) == 0)
        def _():
            o_ref[...] = jnp.zeros_like(o_ref)
        o_ref[...] += _dot_tn(a_ref[...], b_ref[...])

    xin, xshapes, plan, sems = exchange if exchange else ([], [], None, [])
    nx = len(xin)
    res = pl.pallas_call(
        _fused_exchange(body, 2, 1, 0, nx, len(xshapes), plan, grid) if exchange else body, grid=grid,
        in_specs=[pl.BlockSpec((tk, ma), lambda j, k: (k, j // nb + a_off)),
                  pl.BlockSpec((tk, nbw), lambda j, k: (k, j % nb + b_off))] + [_HBM] * nx,
        out_specs=[pl.BlockSpec((None, ma, nbw), lambda j, k: (j, 0, 0))] + [_HBM] * len(xshapes),
        out_shape=[jax.ShapeDtypeStruct((na * nb, ma, nbw), _F32)] + list(xshapes),
        scratch_shapes=list(sems),
        compiler_params=_cparams(("arbitrary", "arbitrary")), name=name,
    )(a, b, *xin)
    return res if exchange else res[0]


_DEV_DELTAS = tuple(dl for dl in itertools.product((0, 1), repeat=3) if any(dl))
_HBM = pl.BlockSpec(memory_space=pltpu.HBM)
_VMEM = pl.BlockSpec(memory_space=pltpu.VMEM)


def _pos():
    return lax.axis_index("x"), lax.axis_index("y"), lax.axis_index("c")


def _flip(v, delta):
    return 1 - v if delta else v


def _remote(src, dst, ssem, rsem, dev):
    return pltpu.make_async_remote_copy(src_ref=src, dst_ref=dst, send_sem=ssem, recv_sem=rsem,
                                        device_id=dev, device_id_type=_MESH)


def _rows(ref, idx, n):
    return ref.at[pl.ds(pl.multiple_of(idx * n, 8), n)]


def _ada_fwd(c8, w_ada_k, b_ada_k):
    rows, d = c8.shape
    nk = w_ada_k.shape[1]

    def body(c_ref, w_ref, b_ref, call_ref, mod_ref, modloc, modrcv, s1, r1, s2, r2):
        xi, yi, ci = _pos()
        me, kme = 4 * xi + 2 * yi + ci, 2 * xi + yi
        call_ref[pl.ds(pl.multiple_of(me * rows, 8), rows), :] = c_ref[...]
        sends = []
        for p, (dx, dy, dc) in enumerate(_DEV_DELTAS):
            cp = _remote(c_ref, _rows(call_ref, me, rows), s1.at[p], r1.at[p], (_flip(xi, dx), _flip(yi, dy), _flip(ci, dc)))
            cp.start()
            sends.append(cp)
        for p, (dx, dy, dc) in enumerate(_DEV_DELTAS):
            src = 4 * _flip(xi, dx) + 2 * _flip(yi, dy) + _flip(ci, dc)
            _remote(c_ref, _rows(call_ref, src, rows), s1.at[p], r1.at[p], (xi, yi, ci)).wait_recv()
        for cp in sends:
            cp.wait_send()

        ca = call_ref[...]
        modloc[...] = _dot(_mx(ca * jax.nn.sigmoid(ca)), _mx(w_ref[...])) + b_ref[...]
        modrcv[kme] = modloc[pl.ds(pl.multiple_of(me * rows, 8), rows), :]
        sends = []
        for j, (dx, dy) in enumerate(_CHIP_DELTAS):
            tx, ty = _flip(xi, dx), _flip(yi, dy)
            cp = _remote(_rows(modloc, 4 * tx + 2 * ty + ci, rows), modrcv.at[kme], s2.at[j], r2.at[j], (tx, ty, ci))
            cp.start()
            sends.append(cp)
        for j, (dx, dy) in enumerate(_CHIP_DELTAS):
            ksrc = 2 * _flip(xi, dx) + _flip(yi, dy)
            _remote(_rows(modloc, me, rows), modrcv.at[ksrc], s2.at[j], r2.at[j], (xi, yi, ci)).wait_recv()
        for cp in sends:
            cp.wait_send()
        for j in range(4):
            mod_ref[:, j * nk:(j + 1) * nk] = modrcv[j]

    return pl.pallas_call(
        body, in_specs=[_VMEM, _VMEM, _VMEM], out_specs=[_VMEM, _VMEM],
        out_shape=[jax.ShapeDtypeStruct((8 * rows, d), _F32), jax.ShapeDtypeStruct((rows, 4 * nk), _F32)],
        scratch_shapes=[pltpu.VMEM((8 * rows, nk), _F32), pltpu.VMEM((4, rows, nk), _F32),
                        pltpu.SemaphoreType.DMA((7,)), pltpu.SemaphoreType.DMA((7,)),
                        pltpu.SemaphoreType.DMA((3,)), pltpu.SemaphoreType.DMA((3,))],
        compiler_params=pltpu.CompilerParams(vmem_limit_bytes=_VMEM_LIMIT), name="ada_fwd",
    )(c8, w_ada_k, b_ada_k)


def _wgather(arrs):
    n = len(arrs)

    def body(*refs):
        start, finish = _gather_plan(refs[:n], refs[n:2 * n], *refs[2 * n:2 * n + 3], bounce=refs[2 * n + 3:])
        start()
        finish()

    return pl.pallas_call(
        body, in_specs=[_HBM] * n, out_specs=[_HBM] * n,
        out_shape=[jax.ShapeDtypeStruct((4,) + a.shape, a.dtype) for a in arrs],
        scratch_shapes=_gather_sems(n) + [pltpu.VMEM(a.shape, a.dtype) for a in arrs],
        compiler_params=pltpu.CompilerParams(vmem_limit_bytes=_VMEM_LIMIT), name="wgather",
    )(*arrs)


def _gather_sems(n):
    return [pltpu.SemaphoreType.DMA((3, n)), pltpu.SemaphoreType.DMA((3, n)), pltpu.SemaphoreType.DMA((n,))]


def _gather_plan(ins, outs, ssem, rsem, lsem, bounce=()):
    n = len(ins)
    xi, yi, ci = _pos()
    kme = 2 * xi + yi
    staged = [pltpu.make_async_copy(ins[a], bounce[a], lsem.at[a]) for a in range(len(bounce))]
    local = [pltpu.make_async_copy(bounce[a] if bounce else ins[a], outs[a].at[kme], lsem.at[a]) for a in range(n)]
    sends, recvs = [], []
    for j, (dx, dy) in enumerate(_CHIP_DELTAS):
        tx, ty = _flip(xi, dx), _flip(yi, dy)
        for a in range(n):
            sends.append(_remote(ins[a], outs[a].at[kme], ssem.at[j, a], rsem.at[j, a], (tx, ty, ci)))
            recvs.append(_remote(ins[a], outs[a].at[2 * tx + ty], ssem.at[j, a], rsem.at[j, a], (xi, yi, ci)))

    def start():
        for cp in sends + staged:
            cp.start()
        for cp in staged:
            cp.wait()
        for cp in local:
            cp.start()

    def finish():
        for cp in recvs:
            cp.wait_recv()
        for cp in sends:
            cp.wait_send()
        for cp in local:
            cp.wait()

    return start, finish


def _dev_gather_sems(n):
    return [pltpu.SemaphoreType.DMA((7, n)), pltpu.SemaphoreType.DMA((7, n)), pltpu.SemaphoreType.DMA((n,))]


def _dev_gather_plan(ins, outs, ssem, rsem, lsem):
    n = len(ins)
    xi, yi, ci = _pos()
    me = 4 * xi + 2 * yi + ci
    local = [pltpu.make_async_copy(ins[a], outs[a].at[me], lsem.at[a]) for a in range(n)]
    sends, recvs = [], []
    for p, (dx, dy, dc) in enumerate(_DEV_DELTAS):
        tx, ty, tc = _flip(xi, dx), _flip(yi, dy), _flip(ci, dc)
        for a in range(n):
            sends.append(_remote(ins[a], outs[a].at[me], ssem.at[p, a], rsem.at[p, a], (tx, ty, tc)))
            recvs.append(_remote(ins[a], outs[a].at[4 * tx + 2 * ty + tc], ssem.at[p, a], rsem.at[p, a], (xi, yi, ci)))

    def start():
        for cp in local + sends:
            cp.start()

    def finish():
        for cp in recvs:
            cp.wait_recv()
        for cp in sends:
            cp.wait_send()
        for cp in local:
            cp.wait()

    return start, finish


def _pair_sems(n):
    return [pltpu.SemaphoreType.DMA((n,)), pltpu.SemaphoreType.DMA((n,))]


def _pair_plan(ins, outs, ssem, rsem):
    xi, yi, ci = _pos()
    sends = []
    for a in range(len(ins)):
        r2 = ins[a].shape[1] // 2
        src = ins[a].at[:, pl.ds(pl.multiple_of((1 - ci) * r2, 8), r2), :]
        sends.append(_remote(src, outs[a], ssem.at[a], rsem.at[a], (xi, yi, 1 - ci)))

    def start():
        for cp in sends:
            cp.start()

    def finish():
        for cp in sends:
            cp.wait_recv()
        for cp in sends:
            cp.wait_send()

    return start, finish


def _chip_reduce_sems(n):
    return [pltpu.SemaphoreType.DMA((3, n)), pltpu.SemaphoreType.DMA((3, n))]


def _chip_reduce_plan(ins, outs, ssem, rsem):
    xi, yi, ci = _pos()
    sends = []
    for j, (dx, dy) in enumerate(_CHIP_DELTAS):
        tx, ty = _flip(xi, dx), _flip(yi, dy)
        sends += [_remote(ins[a].at[2 * tx + ty], outs[a].at[j], ssem.at[j, a], rsem.at[j, a], (tx, ty, ci))
                  for a in range(len(ins))]

    def start():
        for cp in sends:
            cp.start()

    def finish():
        for cp in sends:
            cp.wait_recv()
        for cp in sends:
            cp.wait_send()

    return start, finish


def _pair_exchange(gs, name):
    n = len(gs)

    def body(*refs):
        start, finish = _pair_plan(refs[:n], refs[n:2 * n], *refs[2 * n:])
        start()
        finish()

    return pl.pallas_call(
        body, in_specs=[_HBM] * n, out_specs=[_HBM] * n, out_shape=_pair_out_shapes(gs),
        scratch_shapes=_pair_sems(n), name=name,
    )(*gs)


def _pair_out_shapes(gs):
    return [jax.ShapeDtypeStruct((g.shape[0], g.shape[1] // 2, g.shape[2]), g.dtype) for g in gs]


def _row_tile(r):
    return max(t for t in range(8, min(r, 256) + 1, 8) if r % t == 0)


def _pair_add(g, r, cidx, name):
    nk, r2, c = r.shape
    tr = _row_tile(r2)
    nt = r2 // tr

    def body(c_ref, g_ref, r_ref, o_ref):
        o_ref[...] = g_ref[...] + r_ref[...]

    return pl.pallas_call(
        body, grid_spec=pltpu.PrefetchScalarGridSpec(
            num_scalar_prefetch=1, grid=(nk, nt),
            in_specs=[pl.BlockSpec((None, tr, c), lambda k, i, cr: (k, cr[0] * nt + i, 0)),
                      pl.BlockSpec((None, tr, c), lambda k, i, cr: (k, i, 0))],
            out_specs=pl.BlockSpec((None, tr, c), lambda k, i, cr: (k, i, 0))),
        out_shape=jax.ShapeDtypeStruct(r.shape, _F32),
        compiler_params=_cparams(("arbitrary", "arbitrary")), name=name,
    )(cidx, g, r)


def _chip_exchange(ss):
    n = len(ss)

    def body(*refs):
        start, finish = _chip_reduce_plan(refs[:n], refs[n:2 * n], *refs[2 * n:])
        start()
        finish()

    return pl.pallas_call(
        body, in_specs=[_HBM] * n, out_specs=[_HBM] * n,
        out_shape=[jax.ShapeDtypeStruct((3,) + s.shape[1:], s.dtype) for s in ss],
        scratch_shapes=_chip_reduce_sems(n), name="grad_chip_exchange",
    )(*ss)


def _chip_add(s, r, kidx, name):
    _, r2, c = r.shape
    tr = _row_tile(r2)

    def body(k_ref, s_ref, r_ref, o_ref):
        o_ref[...] = ((s_ref[...] + r_ref[0]) + r_ref[1]) + r_ref[2]

    return pl.pallas_call(
        body, grid_spec=pltpu.PrefetchScalarGridSpec(
            num_scalar_prefetch=1, grid=(r2 // tr,),
            in_specs=[pl.BlockSpec((None, tr, c), lambda i, kr: (kr[0], i, 0)),
                      pl.BlockSpec((3, tr, c), lambda i, kr: (0, i, 0))],
            out_specs=pl.BlockSpec((tr, c), lambda i, kr: (i, 0))),
        out_shape=jax.ShapeDtypeStruct((r2, c), _F32),
        compiler_params=_cparams(("arbitrary",)), name=name,
    )(kidx, s, r)


def _pair_swap(hs):
    n = len(hs)

    def body(*refs):
        ins, outs = refs[:n], refs[n:2 * n]
        ssem, rsem = refs[2 * n:]
        xi, yi, ci = _pos()
        sends = [_remote(ins[a], outs[a], ssem.at[a], rsem.at[a], (xi, yi, 1 - ci)) for a in range(n)]
        for cp in sends:
            cp.start()
        for cp in sends:
            cp.wait_recv()
        for cp in sends:
            cp.wait_send()

    return pl.pallas_call(
        body, in_specs=[_HBM] * n, out_specs=[_HBM] * n,
        out_shape=[jax.ShapeDtypeStruct(h.shape, h.dtype) for h in hs],
        scratch_shapes=[pltpu.SemaphoreType.DMA((n,)), pltpu.SemaphoreType.DMA((n,))], name="grad_pair_swap",
    )(*hs)


def _small_sum(every):
    def body(all_ref, sum_ref):
        tot = all_ref[0]
        for dev in range(1, 8):
            tot = tot + all_ref[dev]
        sum_ref[...] = tot

    return pl.pallas_call(
        body, in_specs=[_VMEM], out_specs=_VMEM, out_shape=jax.ShapeDtypeStruct(every.shape[1:], _F32),
        compiler_params=pltpu.CompilerParams(vmem_limit_bytes=_VMEM_LIMIT), name="small_sum",
    )(every)


def _adamw(w, g, m, v):
    m = _ADAM_B1 * m + (1.0 - _ADAM_B1) * g
    v = _ADAM_B2 * v + (1.0 - _ADAM_B2) * (g * g)
    m_hat = m / (1.0 - _ADAM_B1 ** _ADAM_STEP)
    v_hat = v / (1.0 - _ADAM_B2 ** _ADAM_STEP)
    return -_ADAM_LR * (m_hat / (jnp.sqrt(v_hat) + _ADAM_EPS) + _ADAM_WD * w), m, v


def _adamw_big(w, g_mine, g_theirs, m, v, cidx, name):
    r, c = w.shape
    tr = _row_tile(r // 2)
    nt = r // 2 // tr

    def body(c_ref, w_ref, gm_ref, gt_ref, m_ref, v_ref, g_ref, d_ref, mo_ref, vo_ref):
        g = jnp.where(pl.program_id(0) // nt == c_ref[0], gm_ref[...], gt_ref[...])
        g_ref[...] = g
        d_ref[...], mo_ref[...], vo_ref[...] = _adamw(w_ref[...], g, m_ref[...], v_ref[...])

    spec = pl.BlockSpec((tr, c), lambda i, cr: (i, 0))
    half = pl.BlockSpec((tr, c), lambda i, cr: (i % nt, 0))
    return pl.pallas_call(
        body, grid_spec=pltpu.PrefetchScalarGridSpec(
            num_scalar_prefetch=1, grid=(2 * nt,), in_specs=[spec, half, half, spec, spec], out_specs=[spec] * 4),
        out_shape=[jax.ShapeDtypeStruct((r, c), _F32)] * 4,
        compiler_params=_cparams(("arbitrary",)), name=name,
    )(cidx, w, g_mine, g_theirs, m, v)


def _adamw_small(ws, gs, ms, vs):
    n = len(ws)
    summed = [i for i in range(n) if gs[i].shape != ws[i].shape]

    def body(*refs):
        w_r, g_r, m_r, v_r = (refs[i * n:(i + 1) * n] for i in range(4))
        outs = refs[4 * n:]
        for i in range(n):
            g = g_r[i][...]
            if i in summed:
                g = _rowsum(g)
                outs[3 * n + summed.index(i)][...] = g
            outs[i][...], outs[n + i][...], outs[2 * n + i][...] = _adamw(w_r[i][...], g, m_r[i][...], v_r[i][...])

    shapes = [jax.ShapeDtypeStruct(w.shape, _F32) for w in ws]
    res = pl.pallas_call(
        body, in_specs=[_VMEM] * (4 * n), out_specs=[_VMEM] * (3 * n + len(summed)),
        out_shape=shapes * 3 + [shapes[i] for i in summed],
        compiler_params=pltpu.CompilerParams(vmem_limit_bytes=_VMEM_LIMIT), name="adamw_small",
    )(*ws, *gs, *ms, *vs)
    gs = list(gs)
    for pos, i in enumerate(summed):
        gs[i] = res[3 * n + pos]
    return gs, res[:n], res[n:2 * n], res[2 * n:3 * n]


def _ada_bwd(c_all, dmod_k, w, m, v):
    d, nk = w.shape
    tn = 512 if nk % 512 == 0 else nk

    def body(c_ref, dm_ref, w_ref, m_ref, v_ref, g_ref, d_ref, mo_ref, vo_ref):
        ca = c_ref[...]
        g = _dot_tn(_mx(ca * jax.nn.sigmoid(ca)), _mx(dm_ref[...]))
        g_ref[...] = g
        d_ref[...], mo_ref[...], vo_ref[...] = _adamw(w_ref[...], g, m_ref[...], v_ref[...])

    col = pl.BlockSpec((d, tn), lambda j: (0, j))
    return pl.pallas_call(
        body, grid=(nk // tn,),
        in_specs=[pl.BlockSpec(c_all.shape, lambda j: (0, 0)), pl.BlockSpec((c_all.shape[0], tn), lambda j: (0, j)), col, col, col],
        out_specs=[col] * 4, out_shape=[jax.ShapeDtypeStruct((d, nk), _F32)] * 4,
        compiler_params=_cparams(("arbitrary",)), name="ada_bwd",
    )(c_all, dmod_k, w, m, v)


def _block_diag(wh):
    hn, dh, _ = wh.shape
    eye = jnp.eye(hn, dtype=wh.dtype)
    return (eye[:, None, :, None] * wh[:, :, None, :]).reshape(hn * dh, hn * dh)


def _pack(pieces):
    out = []
    for p in pieces:
        flat = p.reshape(-1, 128)
        out.append(jnp.pad(flat, ((0, (-flat.shape[0]) % 8), (0, 0))))
    return jnp.concatenate(out, axis=0)


def _unpack(pack, shapes):
    out, off = [], 0
    for shp in shapes:
        rows = math.prod(shp) // 128
        out.append(pack[..., off:off + rows, :].reshape(pack.shape[:-2] + tuple(shp)))
        off += rows + (-rows) % 8
    return out


_WEIGHTS = ('w_ada', 'b_ada', 'w_in', 'lru_conv_w', 'lru_conv_b', 'lru_w_r', 'lru_b_r', 'lru_w_i', 'lru_b_i', 'lru_lambda',
            'conv_w', 'conv_b', 'conv_norm_g', 'conv_norm_b', 'w_out', 'ln1_g', 'ln1_b', 'ffn_w_up', 'ffn_conv_w',
            'ffn_conv_b', 'ffn_w_down', 'ln2_g', 'ln2_b')
_BIG = ('w_in', 'w_out', 'ffn_w_up', 'ffn_w_down')


def kernel(x, c, w_ada, b_ada, w_in, lru_conv_w, lru_conv_b, lru_w_r, lru_b_r, lru_w_i, lru_b_i, lru_lambda, conv_w, conv_b, conv_norm_g, conv_norm_b, w_out, ln1_g, ln1_b, ffn_w_up, ffn_conv_w, ffn_conv_b, ffn_w_down, ln2_g, ln2_b, loss_target, m_w_ada, m_b_ada, m_w_in, m_lru_conv_w, m_lru_conv_b, m_lru_w_r, m_lru_b_r, m_lru_w_i, m_lru_b_i, m_lru_lambda, m_conv_w, m_conv_b, m_conv_norm_g, m_conv_norm_b, m_w_out, m_ln1_g, m_ln1_b, m_ffn_w_up, m_ffn_conv_w, m_ffn_conv_b, m_ffn_w_down, m_ln2_g, m_ln2_b, v_w_ada, v_b_ada, v_w_in, v_lru_conv_w, v_lru_conv_b, v_lru_w_r, v_lru_b_r, v_lru_w_i, v_lru_b_i, v_lru_lambda, v_conv_w, v_conv_b, v_conv_norm_g, v_conv_norm_b, v_w_out, v_ln1_g, v_ln1_b, v_ffn_w_up, v_ffn_conv_w, v_ffn_conv_b, v_ffn_w_down, v_ln2_g, v_ln2_b):
    given = dict(locals())
    wt = {n: given[n] for n in _WEIGHTS}
    mo = {n: given["m_" + n] for n in _WEIGHTS}
    vo = {n: given["v_" + n] for n in _WEIGHTS}
    bl, s_len, d = x.shape
    wd = d // 2
    tokens = bl * s_len
    xi, yi, ci = _pos()
    kme = 2 * xi + yi
    kidx = jnp.reshape(kme, (1,)).astype(jnp.int32)
    cidx = jnp.reshape(ci, (1,)).astype(jnp.int32)

    nk = w_ada.shape[2]
    c8 = jnp.pad(c, ((0, 8 - bl), (0, 0)))
    c_all, mod8 = _ada_fwd(c8, w_ada[0], lax.dynamic_slice(b_ada, (0, kme * nk), (1, nk)))
    mod3 = mod8[:bl].reshape(bl, 1, 6 * d)

    win, wout_s, lcw_s, cw_s, fcw_s = _wgather([_mx(w_in[0]), _mx(w_out[0]), lru_conv_w[0], conv_w[0], ffn_conv_w[0]])
    wout = wout_s.reshape(d, d)
    f = 4 * ffn_w_down.shape[1]
    unshard = lambda t: jnp.transpose(t, (1, 0, 2)).reshape(t.shape[1], -1)
    lcw, cw, fcw = unshard(lcw_s), unshard(cw_s), unshard(fcw_s)
    wr_bd, wi_bd = _mx(_block_diag(lru_w_r[0])), _mx(_block_diag(lru_w_i[0]))
    seg = _block_diag(jnp.ones((_N_HEADS, wd // _N_HEADS, wd // _N_HEADS), jnp.bfloat16))
    mixer_small = (lcw, lru_conv_b, wr_bd, wi_bd, lru_b_r, lru_b_i, lru_lambda, cw, conv_b, conv_norm_g, conv_norm_b, seg, wout, ln1_g)

    proj, h, mix, x1, u1, y, vbc, wup, wdn_s = _mix_fwd(x, mod3, win, *mixer_small, ln1_b, [_mx(ffn_w_up[0]), _mx(ffn_w_down[0])])
    wdn = wdn_s.reshape(f, d)
    u2, hh, fact, dz2, loss_acc, dln2, dgt2 = _ffn_fwd(x1, mod3, wup, fcw, ffn_conv_b, wdn, ln2_g, ln2_b, loss_target)
    dx1, dy2, dh, dfc, dmod2 = _ffn_bwd(dz2, x1, hh, mod3, wup, wdn, fcw, ffn_conv_b)
    loss = lax.psum(0.5 * loss_acc[0, 0] / d, ("x", "y", "c"))

    flat = lambda t: t.reshape(tokens, t.shape[-1])
    fc = wup.shape[2]
    g_up = _wgrad(flat(u2), flat(dh), d, fc, 1, 4, 0, 0, "wgrad_up")
    g_dn, r_up = _wgrad(flat(fact), flat(dy2), fc, d, f // fc, 1, 0, 0, "wgrad_down",
                        exchange=([g_up], _pair_out_shapes([g_up]), _pair_plan, _pair_sems(1)))
    g_dn = g_dn.reshape(4, f // 4, d)
    r_dn, = _pair_exchange([g_dn], "grad_pair_exchange_ffn_w_down")
    ffn_sum = [_pair_add(g, r, cidx, "grad_pair_add_" + n) for g, r, n in zip([g_up, g_dn], [r_up, r_dn], _BIG[2:])]
    grad_x, dproj, dmix, xcg, vecw, dlcw, dcw, dln1, dmod1, *ffn_recv = _mix_bwd(
        dx1, x, mix, proj, h, vbc, mod3, win, *mixer_small, ffn_sum)
    g_ri = _wgrad(flat(xcg), flat(xcg), wd, wd, 1, 2, 0, 1, "wgrad_gates")
    dh_ = wd // _N_HEADS
    g_ri = jnp.stack([jnp.stack([g_ri[i, hd * dh_:(hd + 1) * dh_, hd * dh_:(hd + 1) * dh_] for hd in range(_N_HEADS)])
                      for i in range(2)])

    dmod = jnp.concatenate([dmod1.reshape(bl, 3 * d), dmod2.reshape(bl, 2 * d), dgt2.reshape(bl, d)], axis=1)
    pieces = [vecw, dlcw, dcw, jnp.concatenate([dln1, dln2], axis=0), dfc, g_ri, jnp.pad(dmod, ((0, 8 - bl), (0, 0)))]
    shapes = [p.shape for p in pieces]
    pack = _pack(pieces)
    g_out = _wgrad(flat(y), flat(dmix), d, d, 1, 1, 0, 0, "wgrad_out").reshape(4, d // 4, d)
    g_in, every = _wgrad(flat(u1), flat(dproj), d, wd, 1, 4, 0, 0, "wgrad_in", exchange=(
        [pack], [jax.ShapeDtypeStruct((8,) + pack.shape, _F32)], _dev_gather_plan, _dev_gather_sems(1)))
    mix_sum = [_pair_add(g, r, cidx, "grad_pair_add_" + n)
               for g, r, n in zip([g_in, g_out], _pair_exchange([g_in, g_out], "grad_pair_exchange_w_in"), _BIG[:2])]
    chip_sum, recv = mix_sum + ffn_sum, list(_chip_exchange(mix_sum)) + list(ffn_recv)
    half = [_chip_add(s, r, kidx, "grad_chip_add_" + n) for s, r, n in zip(chip_sum, recv, _BIG)]
    grads, deltas, new_m, new_v = {}, {}, {}, {}
    for n, mine, theirs in zip(_BIG, half, _pair_swap(half)):
        g, dl, mm, vv = _adamw_big(wt[n][0], mine, theirs, mo[n][0], vo[n][0], cidx, "adamw_" + n)
        grads[n], deltas[n], new_m[n], new_v[n] = g[None], dl[None], mm[None], vv[None]

    total = _small_sum(every)
    vecw, dlcw, dcw, dln, dfc, g_ri, dmod_sum = _unpack(total, shapes)
    dmod_all = _unpack(every, shapes)[-1].reshape(64, 6 * d)

    g_ada, dl, mm, vv = _ada_bwd(c_all, lax.dynamic_slice(dmod_all, (0, kme * nk), (64, nk)), w_ada[0], m_w_ada[0], v_w_ada[0])
    grads['w_ada'], deltas['w_ada'], new_m['w_ada'], new_v['w_ada'] = g_ada[None], dl[None], mm[None], vv[None]

    shard = lambda t, width: lax.dynamic_slice(t, (0, kme * width), (t.shape[0], width))
    small = {
        'b_ada': dmod_sum, 'lru_conv_w': shard(dlcw, wd // 4), 'lru_conv_b': vecw[0:1], 'lru_w_r': g_ri[0], 'lru_b_r': vecw[1:2],
        'lru_w_i': g_ri[1], 'lru_b_i': vecw[2:3], 'lru_lambda': vecw[3:4], 'conv_w': shard(dcw, wd // 4), 'conv_b': vecw[4:5],
        'conv_norm_g': vecw[5:6], 'conv_norm_b': vecw[6:7], 'ln1_g': dln[0:1], 'ln1_b': dln[1:2],
        'ffn_conv_w': shard(dfc[0:3], f // 4), 'ffn_conv_b': dfc[3:4], 'ln2_g': dln[2:3], 'ln2_b': dln[3:4]}
    names = list(small)
    gs = [small[n] if n == 'b_ada' else small[n].reshape(wt[n].shape) for n in names]
    gs, dls, mms, vvs = _adamw_small([wt[n] for n in names], gs, [mo[n] for n in names], [vo[n] for n in names])
    for n, g, dl, mm, vv in zip(names, gs, dls, mms, vvs):
        grads[n], deltas[n], new_m[n], new_v[n] = g, dl, mm, vv

    return (loss, grad_x, *[grads[n] for n in _WEIGHTS], *[deltas[n] for n in _WEIGHTS],
            *[new_m[n] for n in _WEIGHTS], *[new_v[n] for n in _WEIGHTS])
```

```python
import functools
import itertools
import math

import jax
import jax.numpy as jnp
from jax import lax
from jax.experimental import pallas as pl
from jax.experimental.pallas import tpu as pltpu

_MXU_DT = jnp.bfloat16
_F32 = jnp.float32
_VMEM_LIMIT = 56 * 1024 * 1024
_TT_MIX = 256
_TT_FFN = 256
_TK_WGRAD = 2048
_HALO = 32

_LRU_C = 8.0
_LN_EPS = 1e-5
_N_HEADS = 8
_DEPTH = 1
_ALPHA = (2 * _DEPTH) ** 0.25
_ADAM_LR, _ADAM_B1, _ADAM_B2, _ADAM_EPS, _ADAM_WD, _ADAM_STEP = 0.001, 0.9, 0.999, 1e-08, 0.01, 10

_MESH = pl.DeviceIdType.MESH
_CHIP_DELTAS = ((1, 0), (0, 1), (1, 1))


def _cparams(sem):
    return pltpu.CompilerParams(dimension_semantics=sem, vmem_limit_bytes=_VMEM_LIMIT)


def _resident(shape):
    nd = len(shape)
    return pl.BlockSpec(shape, lambda *_: (0,) * nd, pipeline_mode=pl.Buffered(1))


def _dot(a, b):
    return jnp.dot(a, b, preferred_element_type=_F32)


def _dot_nt(a, b):
    return lax.dot_general(a, b, (((1,), (1,)), ((), ())), preferred_element_type=_F32)


def _dot_tn(a, b):
    return lax.dot_general(a, b, (((0,), (0,)), ((), ())), preferred_element_type=_F32)


def _mx(v):
    return v.astype(_MXU_DT)


def _expm1(v):
    series = v * (1.0 + v * (1.0 / 2 + v * (1.0 / 6 + v * (1.0 / 24 + v * (1.0 / 120)))))
    return jnp.where(jnp.abs(v) < 0.0625, series, jnp.exp(v) - 1.0)


def _softplus(z):
    e = jnp.exp(-jnp.abs(z))
    u = 1.0 + e
    log1p = jnp.where(u == 1.0, e, jnp.log(u) * e / jnp.where(u == 1.0, 1.0, u - 1.0))
    return jnp.maximum(z, 0.0) + log1p


_GELU_C = math.sqrt(2.0 / math.pi)


def _gelu_and_grad(v):
    t = jnp.tanh(_GELU_C * (v + 0.044715 * v * v * v))
    val = 0.5 * v * (1.0 + t)
    grad = 0.5 * (1.0 + t) + 0.5 * v * (1.0 - t * t) * _GELU_C * (1.0 + 3 * 0.044715 * v * v)
    return val, grad


def _seg_sum(v, seg, passes=3):
    hi = v.astype(jnp.bfloat16)
    r1 = v - hi.astype(_F32)
    mid = r1.astype(jnp.bfloat16)
    out = _dot(hi, seg) + _dot(mid, seg)
    if passes == 3:
        out = out + _dot((r1 - mid.astype(_F32)).astype(jnp.bfloat16), seg)
    return out


def _scan_fwd(a, u, h0):
    n = a.shape[0]
    row = lax.broadcasted_iota(jnp.int32, a.shape, 0)
    h, d = u, 1
    while d < n:
        keep = row >= d
        h = a * jnp.where(keep, pltpu.roll(h, d, 0), 0.0) + h
        a = a * jnp.where(keep, pltpu.roll(a, d, 0), 1.0)
        d *= 2
    return h + a * h0


def _scan_rev(c, g, g_end):
    n = c.shape[0]
    row = lax.broadcasted_iota(jnp.int32, c.shape, 0)
    d = 1
    while d < n:
        keep = row < n - d
        g = c * jnp.where(keep, pltpu.roll(g, n - d, 0), 0.0) + g
        c = c * jnp.where(keep, pltpu.roll(c, n - d, 0), 1.0)
        d *= 2
    return g + c * g_end


def _layer_norm_stats(z):
    mu = jnp.mean(z, axis=-1, keepdims=True)
    zc = z - mu
    var = jnp.mean(zc * zc, axis=-1, keepdims=True)
    rstd = lax.rsqrt(var + _LN_EPS)
    return zc * rstd, rstd


def _layer_norm_bwd(dn, n, rstd):
    return rstd * (dn - jnp.mean(dn, axis=-1, keepdims=True) - n * jnp.mean(dn * n, axis=-1, keepdims=True))


def _rowsum(v):
    return jnp.sum(v, axis=0, keepdims=True)


def _fused_exchange(body, n_in, n_out, n_scratch, n_xin, n_xout, plan, grid):
    def wrapped(*refs):
        o0 = n_in + n_xin
        s0 = o0 + n_out + n_xout
        start, finish = plan(refs[n_in:o0], refs[o0 + n_out:s0], *refs[s0 + n_scratch:])
        step = pl.program_id(0) * grid[1] + pl.program_id(1)

        @pl.when(step == 0)
        def _():
            start()

        body(*refs[:n_in], *refs[o0:o0 + n_out], *refs[s0:s0 + n_scratch])

        @pl.when(step == grid[0] * grid[1] - 1)
        def _():
            finish()

    return wrapped


def _lru_gates(xc, wr_ref, wi_ref, br_ref, bi_ref, lam_ref):
    xcb = _mx(xc)
    r = jax.nn.sigmoid(_dot(xcb, wr_ref[...]) + br_ref[...])
    i = jax.nn.sigmoid(_dot(xcb, wi_ref[...]) + bi_ref[...])
    sp = _softplus(-lam_ref[...])
    log_a = -_LRU_C * r * sp
    a = jnp.exp(log_a)
    mult = jnp.sqrt(-_expm1(2.0 * log_a))
    return r, i, sp, a, mult


def _conv_taps(ext_ref, w_ref, first, n_taps, tt):
    acc = w_ref[0:1, :] * ext_ref[pl.ds(first, tt), :]
    for k in range(1, n_taps):
        acc = acc + w_ref[k:k + 1, :] * ext_ref[pl.ds(first + k, tt), :]
    return acc


def _make_shifted(ext_ref, sh_ref):
    n = sh_ref.shape[1]
    for r in range(1, 8):
        sh_ref[r - 1] = ext_ref[pl.ds(r, n), :]


def _tap(ext_ref, sh_ref, off, tt):
    base = (off // 8) * 8
    if off % 8 == 0:
        return ext_ref[pl.ds(base, tt), :]
    return sh_ref[off % 8 - 1, pl.ds(base, tt), :]


def _conv_taps_shifted(ext_ref, sh_ref, w_ref, first, n_taps, tt):
    acc = w_ref[0:1, :] * _tap(ext_ref, sh_ref, first, tt)
    for k in range(1, n_taps):
        acc = acc + w_ref[k:k + 1, :] * _tap(ext_ref, sh_ref, first + k, tt)
    return acc


def _mix_fwd(x, mod3, win, lcw, lcb, wr_bd, wi_bd, b_r, b_i, lam, cw, cb, ng, nb, seg, wout, ln1g, ln1b, shards):
    bl, s_len, d = x.shape
    w = d // 2
    tt = min(_TT_MIX, s_len)
    ns = s_len // tt
    kc = cw.shape[0]

    def body(x_ref, mod_ref, win_ref, lcw_ref, lcb_ref, wr_ref, wi_ref, br_ref, bi_ref, lam_ref, cw_ref, cb_ref,
             ng_ref, nb_ref, seg_ref, wout_ref, g1_ref, b1_ref,
             proj_ref, h_ref, mix_ref, x1_ref, u1_ref, y_ref, vbc_ref, ext4, ext31, sh31, hcar):
        @pl.when(pl.program_id(1) == 0)
        def _():
            ext4[0:8, :] = jnp.zeros((8, w), _F32)
            ext31[0:_HALO, :] = jnp.zeros((_HALO, w), _F32)
            hcar[...] = jnp.zeros_like(hcar)

        xt = x_ref[...]
        sh1, sc1, gt1 = mod_ref[:, 0:d], mod_ref[:, d:2 * d], mod_ref[:, 2 * d:3 * d]
        u1 = _mx(xt * (1.0 + sc1) + sh1)
        u1_ref[...] = u1
        xa, ga, vb, gb = (_dot(u1, win_ref[k]) for k in range(4))
        proj_ref[:, 0:w] = xa
        proj_ref[:, w:2 * w] = ga
        proj_ref[:, 2 * w:3 * w] = vb
        proj_ref[:, 3 * w:4 * w] = gb

        ext4[8:8 + tt, :] = xa
        xc = lcb_ref[...] + _conv_taps(ext4, lcw_ref, 5, 4, tt)
        ext4[0:8, :] = xa[tt - 8:tt, :]
        r, i, sp, a, mult = _lru_gates(xc, wr_ref, wi_ref, br_ref, bi_ref, lam_ref)
        h = _scan_fwd(a, mult * (i * xc), hcar[0:1, :])
        hcar[0:1, :] = h[tt - 1:tt, :]
        h_ref[...] = h
        gelu, _ = _gelu_and_grad(ga)
        y_ref[:, 0:w] = _mx(gelu * h)

        vbg = vb * jax.nn.sigmoid(gb)
        ext31[_HALO:_HALO + tt, :] = vbg
        _make_shifted(ext31, sh31)
        vbc = cb_ref[...] + _conv_taps_shifted(ext31, sh31, cw_ref, _HALO - (kc - 1), kc, tt)
        vbc_ref[...] = vbc
        ext31[0:_HALO, :] = vbg[tt - _HALO:tt, :]
        inv = 1.0 / (w // _N_HEADS)
        zc = vbc - _seg_sum(vbc, seg_ref[...]) * inv
        n = zc * lax.rsqrt(_seg_sum(zc * zc, seg_ref[...]) * inv + _LN_EPS)
        pre = n * ng_ref[...] + nb_ref[...]
        y_ref[:, w:2 * w] = _mx(pre * jax.nn.sigmoid(pre))

        mix = _dot(y_ref[...], wout_ref[...])
        mix_ref[...] = mix
        n1, _ = _layer_norm_stats(_ALPHA * xt + (1.0 + gt1) * mix)
        x1_ref[...] = n1 * g1_ref[...] + b1_ref[...]

    tok = lambda c: pl.BlockSpec((None, tt, c), lambda b, s: (b, s, 0))
    smalls = [lcw, lcb, wr_bd, wi_bd, b_r, b_i, lam, cw, cb, ng, nb, seg, wout, ln1g, ln1b]
    nx = len(shards)
    return pl.pallas_call(
        _fused_exchange(body, 3 + len(smalls), 7, 4, nx, nx, _gather_plan, (bl, ns)), grid=(bl, ns),
        in_specs=[tok(d), pl.BlockSpec((None, 1, 6 * d), lambda b, s: (b, 0, 0)), _resident(win.shape)]
        + [_resident(t.shape) for t in smalls] + [_HBM] * nx,
        out_specs=[tok(4 * w), tok(w), tok(d), tok(d), tok(d), tok(d), tok(w)] + [_HBM] * nx,
        out_shape=[jax.ShapeDtypeStruct((bl, s_len, 4 * w), _F32), jax.ShapeDtypeStruct((bl, s_len, w), _F32),
                   jax.ShapeDtypeStruct((bl, s_len, d), _F32), jax.ShapeDtypeStruct((bl, s_len, d), _F32),
                   jax.ShapeDtypeStruct((bl, s_len, d), _MXU_DT), jax.ShapeDtypeStruct((bl, s_len, d), _MXU_DT),
                   jax.ShapeDtypeStruct((bl, s_len, w), _F32)]
        + [jax.ShapeDtypeStruct((4,) + t.shape, t.dtype) for t in shards],
        scratch_shapes=[pltpu.VMEM((tt + 8, w), _F32), pltpu.VMEM((tt + _HALO, w), _F32),
                        pltpu.VMEM((7, tt + _HALO - 8, w), _F32), pltpu.VMEM((8, w), _F32)] + _gather_sems(nx),
        compiler_params=_cparams(("arbitrary", "arbitrary")), name="mix_fwd",
    )(x, mod3, win, *smalls, *shards)


def _ffn_fwd(x1, mod3, wup, fcw, fcb, wdn, ln2g, ln2b, target):
    bl, s_len, d = x1.shape
    nch, _, fc = wup.shape
    nch //= 2
    f = nch * fc
    tt = min(_TT_FFN, s_len)
    ns = s_len // tt

    def body(x1_ref, mod_ref, wup_ref, fcw_ref, fcb_ref, wdn_ref, g2_ref, b2_ref, tgt_ref,
             u2_ref, hh_ref, f_ref, dz2_ref, loss_ref, dln2_ref, dgt2_ref, ext3):
        first_tile = pl.program_id(1) == 0

        @pl.when(first_tile)
        def _():
            ext3[:, 0:8, :] = jnp.zeros((nch, 8, fc), _F32)
            dgt2_ref[...] = jnp.zeros_like(dgt2_ref)

        @pl.when(first_tile & (pl.program_id(0) == 0))
        def _():
            loss_ref[...] = jnp.zeros_like(loss_ref)
            dln2_ref[...] = jnp.zeros_like(dln2_ref)

        x1t = x1_ref[...]
        sh2, sc2, gt2 = mod_ref[:, 3 * d:4 * d], mod_ref[:, 4 * d:5 * d], mod_ref[:, 5 * d:6 * d]
        u2 = _mx(x1t * (1.0 + sc2) + sh2)
        u2_ref[...] = u2
        y2 = jnp.zeros((tt, d), _F32)
        for j in range(nch):
            lanes = slice(j * fc, (j + 1) * fc)
            v = _dot(u2, wup_ref[j])
            g = _dot(u2, wup_ref[nch + j])
            hh_ref[:, lanes] = v.astype(hh_ref.dtype)
            hh_ref[:, f + j * fc:f + (j + 1) * fc] = g.astype(hh_ref.dtype)
            ext = ext3.at[j]
            ext[8:8 + tt, :] = g
            gc = fcb_ref[:, lanes] + sum(fcw_ref[k:k + 1, lanes] * ext[pl.ds(6 + k, tt), :] for k in range(3))
            ext[0:8, :] = g[tt - 8:tt, :]
            fj = _mx(gc * jax.nn.sigmoid(gc) * v)
            f_ref[:, lanes] = fj
            y2 = y2 + _dot(fj, wdn_ref[lanes, :])

        n2, rstd = _layer_norm_stats(_ALPHA * x1t + (1.0 + gt2) * y2)
        err = n2 * g2_ref[...] + b2_ref[...] - tgt_ref[...]
        loss_ref[...] += jnp.sum(_rowsum(err * err), axis=1, keepdims=True)
        dout = err * (1.0 / d)
        dln2_ref[0:1, :] += _rowsum(dout * n2)
        dln2_ref[1:2, :] += _rowsum(dout)
        dz2 = _layer_norm_bwd(dout * g2_ref[...], n2, rstd)
        dz2_ref[...] = dz2
        dgt2_ref[...] += _rowsum(dz2 * y2)

    tok = lambda c: pl.BlockSpec((None, tt, c), lambda b, s: (b, s, 0))
    acc = lambda r: pl.BlockSpec((r, d), lambda b, s: (0, 0))
    smalls = [fcw, fcb, wdn, ln2g, ln2b]
    return pl.pallas_call(
        body, grid=(bl, ns),
        in_specs=[tok(d), pl.BlockSpec((None, 1, 6 * d), lambda b, s: (b, 0, 0)), _resident(wup.shape)]
        + [_resident(t.shape) for t in smalls] + [tok(d)],
        out_specs=[tok(d), tok(2 * f), tok(f), tok(d), acc(1), acc(2), pl.BlockSpec((None, 1, d), lambda b, s: (b, 0, 0))],
        out_shape=[jax.ShapeDtypeStruct((bl, s_len, d), _MXU_DT), jax.ShapeDtypeStruct((bl, s_len, 2 * f), _F32),
                   jax.ShapeDtypeStruct((bl, s_len, f), _MXU_DT), jax.ShapeDtypeStruct((bl, s_len, d), _F32),
                   jax.ShapeDtypeStruct((1, d), _F32), jax.ShapeDtypeStruct((2, d), _F32),
                   jax.ShapeDtypeStruct((bl, 1, d), _F32)],
        scratch_shapes=[pltpu.VMEM((nch, tt + 8, fc), _F32)],
        compiler_params=_cparams(("arbitrary", "arbitrary")), name="ffn_fwd",
    )(x1, mod3, wup, *smalls, target)


def _ffn_bwd(dz2, x1, hh, mod3, wup, wdn, fcw, fcb):
    bl, s_len, d = x1.shape
    nch, _, fc = wup.shape
    nch //= 2
    f = nch * fc
    tt = min(_TT_FFN, s_len)
    ns = s_len // tt

    def body(dz2_ref, x1_ref, hh_ref, halo_ref, mod_ref, wup_ref, wdn_ref, fcw_ref, fcb_ref,
             dx1_ref, dy2_ref, dh_ref, dfc_ref, dmod_ref, gext, dext, dcar):
        s = ns - 1 - pl.program_id(1)

        @pl.when(pl.program_id(1) == 0)
        def _():
            dcar[...] = jnp.zeros_like(dcar)
            dmod_ref[...] = jnp.zeros_like(dmod_ref)

        @pl.when((pl.program_id(1) == 0) & (pl.program_id(0) == 0))
        def _():
            dfc_ref[...] = jnp.zeros_like(dfc_ref)

        sc2, gt2 = mod_ref[:, 4 * d:5 * d], mod_ref[:, 5 * d:6 * d]
        dz2t = dz2_ref[...]
        dy2 = _mx((1.0 + gt2) * dz2t)
        dy2_ref[...] = dy2
        du2 = jnp.zeros((tt, d), _F32)
        for j in range(nch):
            lanes = slice(j * fc, (j + 1) * fc)
            glanes = slice(f + j * fc, f + (j + 1) * fc)
            v = hh_ref[:, lanes].astype(_F32)
            g = hh_ref[:, glanes].astype(_F32)
            gext[0:8, :] = jnp.where(s == 0, 0.0, halo_ref[:, glanes].astype(_F32))
            gext[8:8 + tt, :] = g
            gc = fcb_ref[:, lanes] + sum(fcw_ref[k:k + 1, lanes] * gext[pl.ds(6 + k, tt), :] for k in range(3))
            sg = jax.nn.sigmoid(gc)
            df = _dot_nt(dy2, wdn_ref[lanes, :])
            dv = df * (gc * sg)
            dgc = df * v * (sg * (1.0 + gc * (1.0 - sg)))
            dfc_ref[3:4, lanes] += _rowsum(dgc)
            dext[0:tt, :] = dgc
            dext[tt:tt + 8, :] = dcar[j]
            dcar[j] = dgc[0:8, :]
            dg = jnp.zeros((tt, fc), _F32)
            for k in range(3):
                dg = dg + fcw_ref[k:k + 1, lanes] * dext[pl.ds(2 - k, tt), :]
                dfc_ref[k:k + 1, lanes] += _rowsum(dgc * gext[pl.ds(6 + k, tt), :])
            dvb, dgb = _mx(dv), _mx(dg)
            dh_ref[:, lanes] = dvb
            dh_ref[:, glanes] = dgb
            du2 = du2 + _dot_nt(dvb, wup_ref[j]) + _dot_nt(dgb, wup_ref[nch + j])

        dx1_ref[...] = _ALPHA * dz2t + du2 * (1.0 + sc2)
        dmod_ref[0:1, :] += _rowsum(du2)
        dmod_ref[1:2, :] += _rowsum(du2 * x1_ref[...])

    tok = lambda c: pl.BlockSpec((None, tt, c), lambda b, i: (b, ns - 1 - i, 0))
    halo = pl.BlockSpec((None, 8, 2 * f), lambda b, i: (b, jnp.maximum((ns - 1 - i) * (tt // 8) - 1, 0), 0))
    return pl.pallas_call(
        body, grid=(bl, ns),
        in_specs=[tok(d), tok(d), tok(2 * f), halo, pl.BlockSpec((None, 1, 6 * d), lambda b, i: (b, 0, 0)),
                  _resident(wup.shape), _resident(wdn.shape), _resident(fcw.shape), _resident(fcb.shape)],
        out_specs=[tok(d), tok(d), tok(2 * f), pl.BlockSpec((4, f), lambda b, i: (0, 0)),
                   pl.BlockSpec((None, 2, d), lambda b, i: (b, 0, 0))],
        out_shape=[jax.ShapeDtypeStruct((bl, s_len, d), _F32), jax.ShapeDtypeStruct((bl, s_len, d), _MXU_DT),
                   jax.ShapeDtypeStruct((bl, s_len, 2 * f), _MXU_DT), jax.ShapeDtypeStruct((4, f), _F32),
                   jax.ShapeDtypeStruct((bl, 2, d), _F32)],
        scratch_shapes=[pltpu.VMEM((tt + 8, fc), _F32), pltpu.VMEM((tt + 8, fc), _F32), pltpu.VMEM((nch, 8, fc), _F32)],
        compiler_params=_cparams(("arbitrary", "arbitrary")), name="ffn_bwd",
    )(dz2, x1, hh, hh, mod3, wup, wdn, fcw, fcb)


def _mix_bwd(dx1, x, mix, proj, h, vbc, mod3, win, lcw, lcb, wr_bd, wi_bd, b_r, b_i, lam, cw, cb, ng, nb, seg, wout, ln1g, chip_sums):
    bl, s_len, d = x.shape
    w = d // 2
    tt = min(_TT_MIX, s_len)
    ns = s_len // tt
    kc = cw.shape[0]

    def body(dx1_ref, x_ref, mix_ref, proj_ref, phalo_ref, h_ref, hhalo_ref, vbc_ref, mod_ref, win_ref, lcw_ref, lcb_ref,
             wr_ref, wi_ref, br_ref, bi_ref, lam_ref, cw_ref, cb_ref, ng_ref, nb_ref, seg_ref, wout_ref, g1_ref,
             gx_ref, dproj_ref, dmix_ref, xcg_ref, vecw_ref, dlcw_ref, dcw_ref, dln1_ref, dmod_ref,
             ext4, ext31, dext4, dext31, sh31, dsh31, car4, car31, gcar):
        s = ns - 1 - pl.program_id(1)
        first = s == 0

        @pl.when(pl.program_id(1) == 0)
        def _():
            car4[...] = jnp.zeros_like(car4)
            car31[...] = jnp.zeros_like(car31)
            gcar[...] = jnp.zeros_like(gcar)
            dmod_ref[...] = jnp.zeros_like(dmod_ref)

        @pl.when((pl.program_id(1) == 0) & (pl.program_id(0) == 0))
        def _():
            for ref in (vecw_ref, dlcw_ref, dcw_ref, dln1_ref):
                ref[...] = jnp.zeros_like(ref)

        xt, mixt = x_ref[...], mix_ref[...]
        sh1, sc1, gt1 = mod_ref[:, 0:d], mod_ref[:, d:2 * d], mod_ref[:, 2 * d:3 * d]

        n1, rstd1 = _layer_norm_stats(_ALPHA * xt + (1.0 + gt1) * mixt)
        dx1t = dx1_ref[...]
        dln1_ref[0:1, :] += _rowsum(dx1t * n1)
        dln1_ref[1:2, :] += _rowsum(dx1t)
        dz1 = _layer_norm_bwd(dx1t * g1_ref[...], n1, rstd1)
        dmod_ref[2:3, :] += _rowsum(dz1 * mixt)
        dmix = _mx((1.0 + gt1) * dz1)
        dmix_ref[...] = dmix
        dya = _dot_nt(dmix, wout_ref[0:w, :])
        dyb = _dot_nt(dmix, wout_ref[w:2 * w, :])

        xa, ga = proj_ref[:, 0:w], proj_ref[:, w:2 * w]
        vb, gb = proj_ref[:, 2 * w:3 * w], proj_ref[:, 3 * w:4 * w]

        sgb = jax.nn.sigmoid(gb)
        vbg = vb * sgb
        hv, hg = phalo_ref[:, 2 * w:3 * w], phalo_ref[:, 3 * w:4 * w]
        ext31[0:_HALO, :] = jnp.where(first, 0.0, hv * jax.nn.sigmoid(hg))
        ext31[_HALO:_HALO + tt, :] = vbg
        _make_shifted(ext31, sh31)
        vbc = vbc_ref[...]
        inv = 1.0 / (w // _N_HEADS)
        zc = vbc - _seg_sum(vbc, seg_ref[...]) * inv
        rstd = lax.rsqrt(_seg_sum(zc * zc, seg_ref[...]) * inv + _LN_EPS)
        n = zc * rstd
        pre = n * ng_ref[...] + nb_ref[...]
        sgp = jax.nn.sigmoid(pre)
        dpre = dyb * (sgp * (1.0 + pre * (1.0 - sgp)))
        vecw_ref[5:6, :] += _rowsum(dpre * n)
        vecw_ref[6:7, :] += _rowsum(dpre)
        dn = dpre * ng_ref[...]
        dvbc = rstd * (dn - _seg_sum(dn, seg_ref[...], 2) * inv - n * (_seg_sum(dn * n, seg_ref[...], 2) * inv))
        vecw_ref[4:5, :] += _rowsum(dvbc)
        dext31[0:tt, :] = dvbc
        dext31[tt:tt + _HALO, :] = car31[...]
        car31[...] = dvbc[0:_HALO, :]
        _make_shifted(dext31, dsh31)
        dvbg = jnp.zeros((tt, w), _F32)
        for k in range(kc):
            dvbg = dvbg + cw_ref[k:k + 1, :] * _tap(dext31, dsh31, kc - 1 - k, tt)
            dcw_ref[k:k + 1, :] += _rowsum(dvbc * _tap(ext31, sh31, _HALO - (kc - 1) + k, tt))
        dproj_ref[:, 2 * w:3 * w] = _mx(dvbg * sgb)
        dproj_ref[:, 3 * w:4 * w] = _mx(dvbg * vb * (sgb * (1.0 - sgb)))

        ext4[0:8, :] = jnp.where(first, 0.0, phalo_ref[_HALO - 8:_HALO, 0:w])
        ext4[8:8 + tt, :] = xa
        xc = lcb_ref[...] + _conv_taps(ext4, lcw_ref, 5, 4, tt)
        xcg_ref[:, 0:w] = _mx(xc)
        r, i, sp, a, mult = _lru_gates(xc, wr_ref, wi_ref, br_ref, bi_ref, lam_ref)
        ht = h_ref[...]
        row = lax.broadcasted_iota(jnp.int32, (tt, w), 0)
        h_before = jnp.where(first, 0.0, hhalo_ref[7:8, :])
        hprev = jnp.where(row == 0, h_before, pltpu.roll(ht, 1, 0))
        gelu, dgelu = _gelu_and_grad(ga)
        dproj_ref[:, w:2 * w] = _mx(dya * ht * dgelu)
        dh = dya * gelu
        coef = jnp.where(row == tt - 1, 1.0, pltpu.roll(a, tt - 1, 0))
        big_g = _scan_rev(coef, dh, gcar[0:1, :])
        gcar[0:1, :] = a[0:1, :] * big_g[0:1, :]
        da = big_g * hprev
        ixc = i * xc
        dlog_a = da * a - (big_g * ixc) * (a * a / mult)
        di = big_g * mult * xc
        dxc = big_g * mult * i
        vecw_ref[3:4, :] += _rowsum(dlog_a * r) * (_LRU_C * jax.nn.sigmoid(-lam_ref[...]))
        dgr_f = dlog_a * (-_LRU_C * sp) * (r * (1.0 - r))
        dgi_f = di * (i * (1.0 - i))
        vecw_ref[1:2, :] += _rowsum(dgr_f)
        vecw_ref[2:3, :] += _rowsum(dgi_f)
        dgr, dgi = _mx(dgr_f), _mx(dgi_f)
        xcg_ref[:, w:2 * w] = dgr
        xcg_ref[:, 2 * w:3 * w] = dgi
        dxc = dxc + _dot_nt(dgr, wr_ref[...]) + _dot_nt(dgi, wi_ref[...])
        vecw_ref[0:1, :] += _rowsum(dxc)
        dext4[0:tt, :] = dxc
        dext4[tt:tt + 8, :] = car4[...]
        car4[...] = dxc[0:8, :]
        dxa = jnp.zeros((tt, w), _F32)
        for k in range(4):
            dxa = dxa + lcw_ref[k:k + 1, :] * dext4[pl.ds(3 - k, tt), :]
            dlcw_ref[k:k + 1, :] += _rowsum(dxc * ext4[pl.ds(5 + k, tt), :])
        dproj_ref[:, 0:w] = _mx(dxa)

        du1 = sum(_dot_nt(dproj_ref[:, k * w:(k + 1) * w], win_ref[k]) for k in range(4))
        gx_ref[...] = _ALPHA * dz1 + du1 * (1.0 + sc1)
        dmod_ref[0:1, :] += _rowsum(du1)
        dmod_ref[1:2, :] += _rowsum(du1 * xt)

    tok = lambda c: pl.BlockSpec((None, tt, c), lambda b, i: (b, ns - 1 - i, 0))
    halo = lambda rows, c: pl.BlockSpec(
        (None, rows, c), lambda b, i: (b, jnp.maximum((ns - 1 - i) * (tt // rows) - 1, 0), 0))
    accw = lambda r, c: pl.BlockSpec((r, c), lambda b, i: (0, 0))
    smalls = [lcw, lcb, wr_bd, wi_bd, b_r, b_i, lam, cw, cb, ng, nb, seg, wout, ln1g]
    nx = len(chip_sums)
    return pl.pallas_call(
        _fused_exchange(body, 10 + len(smalls), 9, 9, nx, nx, _chip_reduce_plan, (bl, ns)), grid=(bl, ns),
        in_specs=[tok(d), tok(d), tok(d), tok(4 * w), halo(_HALO, 4 * w), tok(w), halo(8, w), tok(w),
                  pl.BlockSpec((None, 1, 6 * d), lambda b, i: (b, 0, 0)), _resident(win.shape)]
        + [_resident(t.shape) for t in smalls] + [_HBM] * nx,
        out_specs=[tok(d), tok(4 * w), tok(d), tok(3 * w), accw(8, w), accw(4, w), accw(kc, w), accw(2, d),
                   pl.BlockSpec((None, 3, d), lambda b, i: (b, 0, 0))] + [_HBM] * nx,
        out_shape=[jax.ShapeDtypeStruct((bl, s_len, d), _F32), jax.ShapeDtypeStruct((bl, s_len, 4 * w), _MXU_DT),
                   jax.ShapeDtypeStruct((bl, s_len, d), _MXU_DT), jax.ShapeDtypeStruct((bl, s_len, 3 * w), _MXU_DT),
                   jax.ShapeDtypeStruct((8, w), _F32), jax.ShapeDtypeStruct((4, w), _F32),
                   jax.ShapeDtypeStruct((kc, w), _F32), jax.ShapeDtypeStruct((2, d), _F32),
                   jax.ShapeDtypeStruct((bl, 3, d), _F32)]
        + [jax.ShapeDtypeStruct((3,) + t.shape[1:], t.dtype) for t in chip_sums],
        scratch_shapes=[pltpu.VMEM((tt + 8, w), _F32), pltpu.VMEM((tt + _HALO, w), _F32),
                        pltpu.VMEM((tt + 8, w), _F32), pltpu.VMEM((tt + _HALO, w), _F32),
                        pltpu.VMEM((7, tt + _HALO - 8, w), _F32), pltpu.VMEM((7, tt + _HALO - 8, w), _F32),
                        pltpu.VMEM((8, w), _F32), pltpu.VMEM((_HALO, w), _F32), pltpu.VMEM((8, w), _F32)]
        + _chip_reduce_sems(nx),
        compiler_params=_cparams(("arbitrary", "arbitrary")), name="mix_bwd",
    )(dx1, x, mix, proj, proj, h, h, vbc, mod3, win, *smalls, *chip_sums)


def _wgrad(a, b, ma, nbw, na, nb, a_off, b_off, name, exchange=None):
    t = a.shape[0]
    tk = min(_TK_WGRAD, t)
    grid = (na * nb, t // tk)

    def body(a_ref, b_ref, o_ref):
        @pl.when(pl.program_id(1) == 0)
        def _():
            o_ref[...] = jnp.zeros_like(o_ref)
        o_ref[...] += _dot_tn(a_ref[...], b_ref[...])

    xin, xshapes, plan, sems = exchange if exchange else ([], [], None, [])
    nx = len(xin)
    res = pl.pallas_call(
        _fused_exchange(body, 2, 1, 0, nx, len(xshapes), plan, grid) if exchange else body, grid=grid,
        in_specs=[pl.BlockSpec((tk, ma), lambda j, k: (k, j // nb + a_off)),
                  pl.BlockSpec((tk, nbw), lambda j, k: (k, j % nb + b_off))] + [_HBM] * nx,
        out_specs=[pl.BlockSpec((None, ma, nbw), lambda j, k: (j, 0, 0))] + [_HBM] * len(xshapes),
        out_shape=[jax.ShapeDtypeStruct((na * nb, ma, nbw), _F32)] + list(xshapes),
        scratch_shapes=list(sems),
        compiler_params=_cparams(("arbitrary", "arbitrary")), name=name,
    )(a, b, *xin)
    return res if exchange else res[0]


_DEV_DELTAS = tuple(dl for dl in itertools.product((0, 1), repeat=3) if any(dl))
_HBM = pl.BlockSpec(memory_space=pltpu.HBM)
_VMEM = pl.BlockSpec(memory_space=pltpu.VMEM)


def _pos():
    return lax.axis_index("x"), lax.axis_index("y"), lax.axis_index("c")


def _flip(v, delta):
    return 1 - v if delta else v


def _remote(src, dst, ssem, rsem, dev):
    return pltpu.make_async_remote_copy(src_ref=src, dst_ref=dst, send_sem=ssem, recv_sem=rsem,
                                        device_id=dev, device_id_type=_MESH)


def _rows(ref, idx, n):
    return ref.at[pl.ds(pl.multiple_of(idx * n, 8), n)]


def _ada_fwd(c8, w_ada_k, b_ada_k, shards):
    rows, d = c8.shape
    nk = w_ada_k.shape[1]
    n = len(shards)

    def body(*refs):
        c_ref, w_ref, b_ref = refs[:3]
        call_ref, mod_ref = refs[3 + n:5 + n]
        modloc, modrcv, s1, r1, s2, r2 = refs[5 + 2 * n:11 + 2 * n]
        gather_start, gather_finish = _gather_plan(refs[3:3 + n], refs[5 + n:5 + 2 * n], *refs[11 + 2 * n:14 + 2 * n],
                                                   bounce=refs[14 + 2 * n:])
        gather_start()
        xi, yi, ci = _pos()
        me, kme = 4 * xi + 2 * yi + ci, 2 * xi + yi
        call_ref[pl.ds(pl.multiple_of(me * rows, 8), rows), :] = c_ref[...]
        sends = []
        for p, (dx, dy, dc) in enumerate(_DEV_DELTAS):
            cp = _remote(c_ref, _rows(call_ref, me, rows), s1.at[p], r1.at[p], (_flip(xi, dx), _flip(yi, dy), _flip(ci, dc)))
            cp.start()
            sends.append(cp)
        for p, (dx, dy, dc) in enumerate(_DEV_DELTAS):
            src = 4 * _flip(xi, dx) + 2 * _flip(yi, dy) + _flip(ci, dc)
            _remote(c_ref, _rows(call_ref, src, rows), s1.at[p], r1.at[p], (xi, yi, ci)).wait_recv()
        for cp in sends:
            cp.wait_send()

        ca = call_ref[...]
        modloc[...] = _dot(_mx(ca * jax.nn.sigmoid(ca)), _mx(w_ref[...])) + b_ref[...]
        modrcv[kme] = modloc[pl.ds(pl.multiple_of(me * rows, 8), rows), :]
        sends = []
        for j, (dx, dy) in enumerate(_CHIP_DELTAS):
            tx, ty = _flip(xi, dx), _flip(yi, dy)
            cp = _remote(_rows(modloc, 4 * tx + 2 * ty + ci, rows), modrcv.at[kme], s2.at[j], r2.at[j], (tx, ty, ci))
            cp.start()
            sends.append(cp)
        for j, (dx, dy) in enumerate(_CHIP_DELTAS):
            ksrc = 2 * _flip(xi, dx) + _flip(yi, dy)
            _remote(_rows(modloc, me, rows), modrcv.at[ksrc], s2.at[j], r2.at[j], (xi, yi, ci)).wait_recv()
        for cp in sends:
            cp.wait_send()
        for j in range(4):
            mod_ref[:, j * nk:(j + 1) * nk] = modrcv[j]
        gather_finish()

    return pl.pallas_call(
        body, in_specs=[_VMEM, _VMEM, _VMEM] + [_HBM] * n, out_specs=[_VMEM, _VMEM] + [_HBM] * n,
        out_shape=[jax.ShapeDtypeStruct((8 * rows, d), _F32), jax.ShapeDtypeStruct((rows, 4 * nk), _F32)]
        + [jax.ShapeDtypeStruct((4,) + a.shape, a.dtype) for a in shards],
        scratch_shapes=[pltpu.VMEM((8 * rows, nk), _F32), pltpu.VMEM((4, rows, nk), _F32),
                        pltpu.SemaphoreType.DMA((7,)), pltpu.SemaphoreType.DMA((7,)),
                        pltpu.SemaphoreType.DMA((3,)), pltpu.SemaphoreType.DMA((3,))]
        + _gather_sems(n) + [pltpu.VMEM(a.shape, a.dtype) for a in shards],
        compiler_params=pltpu.CompilerParams(vmem_limit_bytes=_VMEM_LIMIT), name="ada_fwd",
    )(c8, w_ada_k, b_ada_k, *shards)


def _gather_sems(n):
    return [pltpu.SemaphoreType.DMA((3, n)), pltpu.SemaphoreType.DMA((3, n)), pltpu.SemaphoreType.DMA((n,))]


def _gather_plan(ins, outs, ssem, rsem, lsem, bounce=()):
    n = len(ins)
    xi, yi, ci = _pos()
    kme = 2 * xi + yi
    staged = [pltpu.make_async_copy(ins[a], bounce[a], lsem.at[a]) for a in range(len(bounce))]
    local = [pltpu.make_async_copy(bounce[a] if bounce else ins[a], outs[a].at[kme], lsem.at[a]) for a in range(n)]
    sends, recvs = [], []
    for j, (dx, dy) in enumerate(_CHIP_DELTAS):
        tx, ty = _flip(xi, dx), _flip(yi, dy)
        for a in range(n):
            sends.append(_remote(ins[a], outs[a].at[kme], ssem.at[j, a], rsem.at[j, a], (tx, ty, ci)))
            recvs.append(_remote(ins[a], outs[a].at[2 * tx + ty], ssem.at[j, a], rsem.at[j, a], (xi, yi, ci)))

    def start():
        for cp in sends + staged:
            cp.start()
        for cp in staged:
            cp.wait()
        for cp in local:
            cp.start()

    def finish():
        for cp in recvs:
            cp.wait_recv()
        for cp in sends:
            cp.wait_send()
        for cp in local:
            cp.wait()

    return start, finish


def _dev_gather_sems(n):
    return [pltpu.SemaphoreType.DMA((7, n)), pltpu.SemaphoreType.DMA((7, n)), pltpu.SemaphoreType.DMA((n,))]


def _dev_gather_plan(ins, outs, ssem, rsem, lsem):
    n = len(ins)
    xi, yi, ci = _pos()
    me = 4 * xi + 2 * yi + ci
    local = [pltpu.make_async_copy(ins[a], outs[a].at[me], lsem.at[a]) for a in range(n)]
    sends, recvs = [], []
    for p, (dx, dy, dc) in enumerate(_DEV_DELTAS):
        tx, ty, tc = _flip(xi, dx), _flip(yi, dy), _flip(ci, dc)
        for a in range(n):
            sends.append(_remote(ins[a], outs[a].at[me], ssem.at[p, a], rsem.at[p, a], (tx, ty, tc)))
            recvs.append(_remote(ins[a], outs[a].at[4 * tx + 2 * ty + tc], ssem.at[p, a], rsem.at[p, a], (xi, yi, ci)))

    def start():
        for cp in local + sends:
            cp.start()

    def finish():
        for cp in recvs:
            cp.wait_recv()
        for cp in sends:
            cp.wait_send()
        for cp in local:
            cp.wait()

    return start, finish


def _pair_sems(n):
    return [pltpu.SemaphoreType.DMA((n,)), pltpu.SemaphoreType.DMA((n,))]


def _pair_plan(ins, outs, ssem, rsem):
    xi, yi, ci = _pos()
    sends = []
    for a in range(len(ins)):
        r2 = ins[a].shape[1] // 2
        src = ins[a].at[:, pl.ds(pl.multiple_of((1 - ci) * r2, 8), r2), :]
        sends.append(_remote(src, outs[a], ssem.at[a], rsem.at[a], (xi, yi, 1 - ci)))

    def start():
        for cp in sends:
            cp.start()

    def finish():
        for cp in sends:
            cp.wait_recv()
        for cp in sends:
            cp.wait_send()

    return start, finish


def _chip_reduce_sems(n):
    return [pltpu.SemaphoreType.DMA((3, n)), pltpu.SemaphoreType.DMA((3, n))]


def _chip_reduce_plan(ins, outs, ssem, rsem):
    xi, yi, ci = _pos()
    sends = []
    for j, (dx, dy) in enumerate(_CHIP_DELTAS):
        tx, ty = _flip(xi, dx), _flip(yi, dy)
        sends += [_remote(ins[a].at[2 * tx + ty], outs[a].at[j], ssem.at[j, a], rsem.at[j, a], (tx, ty, ci))
                  for a in range(len(ins))]

    def start():
        for cp in sends:
            cp.start()

    def finish():
        for cp in sends:
            cp.wait_recv()
        for cp in sends:
            cp.wait_send()

    return start, finish


def _pair_exchange(gs, name):
    n = len(gs)

    def body(*refs):
        start, finish = _pair_plan(refs[:n], refs[n:2 * n], *refs[2 * n:])
        start()
        finish()

    return pl.pallas_call(
        body, in_specs=[_HBM] * n, out_specs=[_HBM] * n, out_shape=_pair_out_shapes(gs),
        scratch_shapes=_pair_sems(n), name=name,
    )(*gs)


def _pair_out_shapes(gs):
    return [jax.ShapeDtypeStruct((g.shape[0], g.shape[1] // 2, g.shape[2]), g.dtype) for g in gs]


def _row_tile(r):
    return max(t for t in range(8, min(r, 256) + 1, 8) if r % t == 0)


def _pair_add(g, r, cidx, name, wire_dtype=None):
    nk, r2, c = r.shape
    tr = _row_tile(r2)
    nt = r2 // tr

    def body(c_ref, g_ref, r_ref, *o_refs):
        s = g_ref[...] + r_ref[...]
        for o_ref in o_refs:
            o_ref[...] = s.astype(o_ref.dtype)

    out_spec = pl.BlockSpec((None, tr, c), lambda k, i, cr: (k, i, 0))
    dtypes = [_F32] + ([wire_dtype] if wire_dtype else [])
    res = pl.pallas_call(
        body, grid_spec=pltpu.PrefetchScalarGridSpec(
            num_scalar_prefetch=1, grid=(nk, nt),
            in_specs=[pl.BlockSpec((None, tr, c), lambda k, i, cr: (k, cr[0] * nt + i, 0)), out_spec],
            out_specs=[out_spec] * len(dtypes)),
        out_shape=[jax.ShapeDtypeStruct(r.shape, dt) for dt in dtypes],
        compiler_params=_cparams(("arbitrary", "arbitrary")), name=name,
    )(cidx, g, r)
    return res if wire_dtype else res[0]


def _chip_exchange(ss):
    n = len(ss)

    def body(*refs):
        start, finish = _chip_reduce_plan(refs[:n], refs[n:2 * n], *refs[2 * n:])
        start()
        finish()

    return pl.pallas_call(
        body, in_specs=[_HBM] * n, out_specs=[_HBM] * n,
        out_shape=[jax.ShapeDtypeStruct((3,) + s.shape[1:], s.dtype) for s in ss],
        scratch_shapes=_chip_reduce_sems(n), name="grad_chip_exchange",
    )(*ss)


def _chip_add(s, r, kidx, name):
    _, r2, c = r.shape
    tr = _row_tile(r2)

    def body(k_ref, s_ref, r_ref, o_ref):
        o_ref[...] = ((s_ref[...] + r_ref[0].astype(_F32)) + r_ref[1].astype(_F32)) + r_ref[2].astype(_F32)

    return pl.pallas_call(
        body, grid_spec=pltpu.PrefetchScalarGridSpec(
            num_scalar_prefetch=1, grid=(r2 // tr,),
            in_specs=[pl.BlockSpec((None, tr, c), lambda i, kr: (kr[0], i, 0)),
                      pl.BlockSpec((3, tr, c), lambda i, kr: (0, i, 0))],
            out_specs=pl.BlockSpec((tr, c), lambda i, kr: (i, 0))),
        out_shape=jax.ShapeDtypeStruct((r2, c), _F32),
        compiler_params=_cparams(("arbitrary",)), name=name,
    )(kidx, s, r)


def _pair_swap(hs):
    n = len(hs)

    def body(*refs):
        ins, outs = refs[:n], refs[n:2 * n]
        ssem, rsem = refs[2 * n:]
        xi, yi, ci = _pos()
        sends = [_remote(ins[a], outs[a], ssem.at[a], rsem.at[a], (xi, yi, 1 - ci)) for a in range(n)]
        for cp in sends:
            cp.start()
        for cp in sends:
            cp.wait_recv()
        for cp in sends:
            cp.wait_send()

    return pl.pallas_call(
        body, in_specs=[_HBM] * n, out_specs=[_HBM] * n,
        out_shape=[jax.ShapeDtypeStruct(h.shape, h.dtype) for h in hs],
        scratch_shapes=[pltpu.SemaphoreType.DMA((n,)), pltpu.SemaphoreType.DMA((n,))], name="grad_pair_swap",
    )(*hs)


def _small_sum(every):
    def body(all_ref, sum_ref):
        tot = all_ref[0]
        for dev in range(1, 8):
            tot = tot + all_ref[dev]
        sum_ref[...] = tot

    return pl.pallas_call(
        body, in_specs=[_VMEM], out_specs=_VMEM, out_shape=jax.ShapeDtypeStruct(every.shape[1:], _F32),
        compiler_params=pltpu.CompilerParams(vmem_limit_bytes=_VMEM_LIMIT), name="small_sum",
    )(every)


def _adamw(w, g, m, v):
    m = _ADAM_B1 * m + (1.0 - _ADAM_B1) * g
    v = _ADAM_B2 * v + (1.0 - _ADAM_B2) * (g * g)
    m_hat = m / (1.0 - _ADAM_B1 ** _ADAM_STEP)
    v_hat = v / (1.0 - _ADAM_B2 ** _ADAM_STEP)
    return -_ADAM_LR * (m_hat / (jnp.sqrt(v_hat) + _ADAM_EPS) + _ADAM_WD * w), m, v


def _adamw_big(w, g_mine, g_theirs, m, v, cidx, name):
    r, c = w.shape
    tr = _row_tile(r // 2)
    nt = r // 2 // tr

    def body(c_ref, w_ref, gm_ref, gt_ref, m_ref, v_ref, g_ref, d_ref, mo_ref, vo_ref):
        g = jnp.where(pl.program_id(0) // nt == c_ref[0], gm_ref[...], gt_ref[...])
        g_ref[...] = g
        d_ref[...], mo_ref[...], vo_ref[...] = _adamw(w_ref[...], g, m_ref[...], v_ref[...])

    spec = pl.BlockSpec((tr, c), lambda i, cr: (i, 0))
    half = pl.BlockSpec((tr, c), lambda i, cr: (i % nt, 0))
    return pl.pallas_call(
        body, grid_spec=pltpu.PrefetchScalarGridSpec(
            num_scalar_prefetch=1, grid=(2 * nt,), in_specs=[spec, half, half, spec, spec], out_specs=[spec] * 4),
        out_shape=[jax.ShapeDtypeStruct((r, c), _F32)] * 4,
        compiler_params=_cparams(("arbitrary",)), name=name,
    )(cidx, w, g_mine, g_theirs, m, v)


def _adamw_small(ws, gs, ms, vs):
    n = len(ws)
    summed = [i for i in range(n) if gs[i].shape != ws[i].shape]

    def body(*refs):
        w_r, g_r, m_r, v_r = (refs[i * n:(i + 1) * n] for i in range(4))
        outs = refs[4 * n:]
        for i in range(n):
            g = g_r[i][...]
            if i in summed:
                g = _rowsum(g)
                outs[3 * n + summed.index(i)][...] = g
            outs[i][...], outs[n + i][...], outs[2 * n + i][...] = _adamw(w_r[i][...], g, m_r[i][...], v_r[i][...])

    shapes = [jax.ShapeDtypeStruct(w.shape, _F32) for w in ws]
    res = pl.pallas_call(
        body, in_specs=[_VMEM] * (4 * n), out_specs=[_VMEM] * (3 * n + len(summed)),
        out_shape=shapes * 3 + [shapes[i] for i in summed],
        compiler_params=pltpu.CompilerParams(vmem_limit_bytes=_VMEM_LIMIT), name="adamw_small",
    )(*ws, *gs, *ms, *vs)
    gs = list(gs)
    for pos, i in enumerate(summed):
        gs[i] = res[3 * n + pos]
    return gs, res[:n], res[n:2 * n], res[2 * n:3 * n]


def _ada_bwd(c_all, dmod_k, w, m, v):
    d, nk = w.shape
    tn = 512 if nk % 512 == 0 else nk

    def body(c_ref, dm_ref, w_ref, m_ref, v_ref, g_ref, d_ref, mo_ref, vo_ref):
        ca = c_ref[...]
        g = _dot_tn(_mx(ca * jax.nn.sigmoid(ca)), _mx(dm_ref[...]))
        g_ref[...] = g
        d_ref[...], mo_ref[...], vo_ref[...] = _adamw(w_ref[...], g, m_ref[...], v_ref[...])

    col = pl.BlockSpec((d, tn), lambda j: (0, j))
    return pl.pallas_call(
        body, grid=(nk // tn,),
        in_specs=[pl.BlockSpec(c_all.shape, lambda j: (0, 0)), pl.BlockSpec((c_all.shape[0], tn), lambda j: (0, j)), col, col, col],
        out_specs=[col] * 4, out_shape=[jax.ShapeDtypeStruct((d, nk), _F32)] * 4,
        compiler_params=_cparams(("arbitrary",)), name="ada_bwd",
    )(c_all, dmod_k, w, m, v)


def _block_diag(wh):
    hn, dh, _ = wh.shape
    eye = jnp.eye(hn, dtype=wh.dtype)
    return (eye[:, None, :, None] * wh[:, :, None, :]).reshape(hn * dh, hn * dh)


def _pack(pieces):
    out = []
    for p in pieces:
        flat = p.reshape(-1, 128)
        out.append(jnp.pad(flat, ((0, (-flat.shape[0]) % 8), (0, 0))))
    return jnp.concatenate(out, axis=0)


def _unpack(pack, shapes):
    out, off = [], 0
    for shp in shapes:
        rows = math.prod(shp) // 128
        out.append(pack[..., off:off + rows, :].reshape(pack.shape[:-2] + tuple(shp)))
        off += rows + (-rows) % 8
    return out


_WEIGHTS = ('w_ada', 'b_ada', 'w_in', 'lru_conv_w', 'lru_conv_b', 'lru_w_r', 'lru_b_r', 'lru_w_i', 'lru_b_i', 'lru_lambda',
            'conv_w', 'conv_b', 'conv_norm_g', 'conv_norm_b', 'w_out', 'ln1_g', 'ln1_b', 'ffn_w_up', 'ffn_conv_w',
            'ffn_conv_b', 'ffn_w_down', 'ln2_g', 'ln2_b')
_BIG = ('w_in', 'w_out', 'ffn_w_up', 'ffn_w_down')


def kernel(x, c, w_ada, b_ada, w_in, lru_conv_w, lru_conv_b, lru_w_r, lru_b_r, lru_w_i, lru_b_i, lru_lambda, conv_w, conv_b, conv_norm_g, conv_norm_b, w_out, ln1_g, ln1_b, ffn_w_up, ffn_conv_w, ffn_conv_b, ffn_w_down, ln2_g, ln2_b, loss_target, m_w_ada, m_b_ada, m_w_in, m_lru_conv_w, m_lru_conv_b, m_lru_w_r, m_lru_b_r, m_lru_w_i, m_lru_b_i, m_lru_lambda, m_conv_w, m_conv_b, m_conv_norm_g, m_conv_norm_b, m_w_out, m_ln1_g, m_ln1_b, m_ffn_w_up, m_ffn_conv_w, m_ffn_conv_b, m_ffn_w_down, m_ln2_g, m_ln2_b, v_w_ada, v_b_ada, v_w_in, v_lru_conv_w, v_lru_conv_b, v_lru_w_r, v_lru_b_r, v_lru_w_i, v_lru_b_i, v_lru_lambda, v_conv_w, v_conv_b, v_conv_norm_g, v_conv_norm_b, v_w_out, v_ln1_g, v_ln1_b, v_ffn_w_up, v_ffn_conv_w, v_ffn_conv_b, v_ffn_w_down, v_ln2_g, v_ln2_b):
    given = dict(locals())
    wt = {n: given[n] for n in _WEIGHTS}
    mo = {n: given["m_" + n] for n in _WEIGHTS}
    vo = {n: given["v_" + n] for n in _WEIGHTS}
    bl, s_len, d = x.shape
    wd = d // 2
    tokens = bl * s_len
    xi, yi, ci = _pos()
    kme = 2 * xi + yi
    kidx = jnp.reshape(kme, (1,)).astype(jnp.int32)
    cidx = jnp.reshape(ci, (1,)).astype(jnp.int32)

    nk = w_ada.shape[2]
    c8 = jnp.pad(c, ((0, 8 - bl), (0, 0)))
    c_all, mod8, win, wout_s, lcw_s, cw_s, fcw_s = _ada_fwd(
        c8, w_ada[0], lax.dynamic_slice(b_ada, (0, kme * nk), (1, nk)),
        [_mx(w_in[0]), _mx(w_out[0]), lru_conv_w[0], conv_w[0], ffn_conv_w[0]])
    mod3 = mod8[:bl].reshape(bl, 1, 6 * d)
    wout = wout_s.reshape(d, d)
    f = 4 * ffn_w_down.shape[1]
    unshard = lambda t: jnp.transpose(t, (1, 0, 2)).reshape(t.shape[1], -1)
    lcw, cw, fcw = unshard(lcw_s), unshard(cw_s), unshard(fcw_s)
    wr_bd, wi_bd = _mx(_block_diag(lru_w_r[0])), _mx(_block_diag(lru_w_i[0]))
    seg = _block_diag(jnp.ones((_N_HEADS, wd // _N_HEADS, wd // _N_HEADS), jnp.bfloat16))
    mixer_small = (lcw, lru_conv_b, wr_bd, wi_bd, lru_b_r, lru_b_i, lru_lambda, cw, conv_b, conv_norm_g, conv_norm_b, seg, wout, ln1_g)

    proj, h, mix, x1, u1, y, vbc, wup, wdn_s = _mix_fwd(x, mod3, win, *mixer_small, ln1_b, [_mx(ffn_w_up[0]), _mx(ffn_w_down[0])])
    wdn = wdn_s.reshape(f, d)
    u2, hh, fact, dz2, loss_acc, dln2, dgt2 = _ffn_fwd(x1, mod3, wup, fcw, ffn_conv_b, wdn, ln2_g, ln2_b, loss_target)
    dx1, dy2, dh, dfc, dmod2 = _ffn_bwd(dz2, x1, hh, mod3, wup, wdn, fcw, ffn_conv_b)
    loss = lax.psum(0.5 * loss_acc[0, 0] / d, ("x", "y", "c"))

    flat = lambda t: t.reshape(tokens, t.shape[-1])
    fc = wup.shape[2]
    g_up = _wgrad(flat(u2), flat(dh), d, fc, 1, 4, 0, 0, "wgrad_up")
    g_dn, r_up = _wgrad(flat(fact), flat(dy2), fc, d, f // fc, 1, 0, 0, "wgrad_down",
                        exchange=([g_up], _pair_out_shapes([g_up]), _pair_plan, _pair_sems(1)))
    g_dn = g_dn.reshape(4, f // 4, d)
    r_dn, = _pair_exchange([g_dn], "grad_pair_exchange_ffn_w_down")
    ffn_sum = [_pair_add(g, r, cidx, "grad_pair_add_" + n) for g, r, n in zip([g_up, g_dn], [r_up, r_dn], _BIG[2:])]
    grad_x, dproj, dmix, xcg, vecw, dlcw, dcw, dln1, dmod1, *ffn_recv = _mix_bwd(
        dx1, x, mix, proj, h, vbc, mod3, win, *mixer_small, ffn_sum)
    g_ri = _wgrad(flat(xcg), flat(xcg), wd, wd, 1, 2, 0, 1, "wgrad_gates")
    dh_ = wd // _N_HEADS
    g_ri = jnp.stack([jnp.stack([g_ri[i, hd * dh_:(hd + 1) * dh_, hd * dh_:(hd + 1) * dh_] for hd in range(_N_HEADS)])
                      for i in range(2)])

    dmod = jnp.concatenate([dmod1.reshape(bl, 3 * d), dmod2.reshape(bl, 2 * d), dgt2.reshape(bl, d)], axis=1)
    pieces = [vecw, dlcw, dcw, jnp.concatenate([dln1, dln2], axis=0), dfc, g_ri, jnp.pad(dmod, ((0, 8 - bl), (0, 0)))]
    shapes = [p.shape for p in pieces]
    pack = _pack(pieces)
    g_out = _wgrad(flat(y), flat(dmix), d, d, 1, 1, 0, 0, "wgrad_out").reshape(4, d // 4, d)
    g_in, every = _wgrad(flat(u1), flat(dproj), d, wd, 1, 4, 0, 0, "wgrad_in", exchange=(
        [pack], [jax.ShapeDtypeStruct((8,) + pack.shape, _F32)], _dev_gather_plan, _dev_gather_sems(1)))
    mix_sum, mix_wire = zip(*[_pair_add(g, r, cidx, "grad_pair_add_" + n, jnp.bfloat16) for g, r, n in zip(
        [g_in, g_out], _pair_exchange([g_in, g_out], "grad_pair_exchange_w_in"), _BIG[:2])])
    chip_sum, recv = list(mix_sum) + ffn_sum, list(_chip_exchange(mix_wire)) + list(ffn_recv)
    half = [_chip_add(s, r, kidx, "grad_chip_add_" + n) for s, r, n in zip(chip_sum, recv, _BIG)]
    grads, deltas, new_m, new_v = {}, {}, {}, {}
    for n, mine, theirs in zip(_BIG, half, _pair_swap(half)):
        g, dl, mm, vv = _adamw_big(wt[n][0], mine, theirs, mo[n][0], vo[n][0], cidx, "adamw_" + n)
        grads[n], deltas[n], new_m[n], new_v[n] = g[None], dl[None], mm[None], vv[None]

    total = _small_sum(every)
    vecw, dlcw, dcw, dln, dfc, g_ri, dmod_sum = _unpack(total, shapes)
    dmod_all = _unpack(every, shapes)[-1].reshape(64, 6 * d)

    g_ada, dl, mm, vv = _ada_bwd(c_all, lax.dynamic_slice(dmod_all, (0, kme * nk), (64, nk)), w_ada[0], m_w_ada[0], v_w_ada[0])
    grads['w_ada'], deltas['w_ada'], new_m['w_ada'], new_v['w_ada'] = g_ada[None], dl[None], mm[None], vv[None]

    shard = lambda t, width: lax.dynamic_slice(t, (0, kme * width), (t.shape[0], width))
    small = {
        'b_ada': dmod_sum, 'lru_conv_w': shard(dlcw, wd // 4), 'lru_conv_b': vecw[0:1], 'lru_w_r': g_ri[0], 'lru_b_r': vecw[1:2],
        'lru_w_i': g_ri[1], 'lru_b_i': vecw[2:3], 'lru_lambda': vecw[3:4], 'conv_w': shard(dcw, wd // 4), 'conv_b': vecw[4:5],
        'conv_norm_g': vecw[5:6], 'conv_norm_b': vecw[6:7], 'ln1_g': dln[0:1], 'ln1_b': dln[1:2],
        'ffn_conv_w': shard(dfc[0:3], f // 4), 'ffn_conv_b': dfc[3:4], 'ln2_g': dln[2:3], 'ln2_b': dln[3:4]}
    names = list(small)
    gs = [small[n] if n == 'b_ada' else small[n].reshape(wt[n].shape) for n in names]
    gs, dls, mms, vvs = _adamw_small([wt[n] for n in names], gs, [mo[n] for n in names], [vo[n] for n in names])
    for n, g, dl, mm, vv in zip(names, gs, dls, mms, vvs):
        grads[n], deltas[n], new_m[n], new_v[n] = g, dl, mm, vv

    return (loss, grad_x, *[grads[n] for n in _WEIGHTS], *[deltas[n] for n in _WEIGHTS],
            *[new_m[n] for n in _WEIGHTS], *[new_v[n] for n in _WEIGHTS])
```

```python
import functools
import itertools
import math

import jax
import jax.numpy as jnp
from jax import lax
from jax.experimental import pallas as pl
from jax.experimental.pallas import tpu as pltpu

_MXU_DT = jnp.bfloat16
_F32 = jnp.float32
_VMEM_LIMIT = 56 * 1024 * 1024
_TT_MIX = 256
_TT_FFN = 256
_TK_WGRAD = 2048
_HALO = 32

_LRU_C = 8.0
_LN_EPS = 1e-5
_N_HEADS = 8
_DEPTH = 1
_ALPHA = (2 * _DEPTH) ** 0.25
_ADAM_LR, _ADAM_B1, _ADAM_B2, _ADAM_EPS, _ADAM_WD, _ADAM_STEP = 0.001, 0.9, 0.999, 1e-08, 0.01, 10

_MESH = pl.DeviceIdType.MESH
_CHIP_DELTAS = ((1, 0), (0, 1), (1, 1))


def _cparams(sem):
    return pltpu.CompilerParams(dimension_semantics=sem, vmem_limit_bytes=_VMEM_LIMIT)


def _resident(shape):
    nd = len(shape)
    return pl.BlockSpec(shape, lambda *_: (0,) * nd, pipeline_mode=pl.Buffered(1))


def _dot(a, b):
    return jnp.dot(a, b, preferred_element_type=_F32)


def _dot_nt(a, b):
    return lax.dot_general(a, b, (((1,), (1,)), ((), ())), preferred_element_type=_F32)


def _dot_tn(a, b):
    return lax.dot_general(a, b, (((0,), (0,)), ((), ())), preferred_element_type=_F32)


def _mx(v):
    return v.astype(_MXU_DT)


def _expm1(v):
    series = v * (1.0 + v * (1.0 / 2 + v * (1.0 / 6 + v * (1.0 / 24 + v * (1.0 / 120)))))
    return jnp.where(jnp.abs(v) < 0.0625, series, jnp.exp(v) - 1.0)


def _softplus(z):
    e = jnp.exp(-jnp.abs(z))
    u = 1.0 + e
    log1p = jnp.where(u == 1.0, e, jnp.log(u) * e / jnp.where(u == 1.0, 1.0, u - 1.0))
    return jnp.maximum(z, 0.0) + log1p


_GELU_C = math.sqrt(2.0 / math.pi)


def _gelu_and_grad(v):
    t = jnp.tanh(_GELU_C * (v + 0.044715 * v * v * v))
    val = 0.5 * v * (1.0 + t)
    grad = 0.5 * (1.0 + t) + 0.5 * v * (1.0 - t * t) * _GELU_C * (1.0 + 3 * 0.044715 * v * v)
    return val, grad


def _seg_sum(v, seg, passes=3):
    hi = v.astype(jnp.bfloat16)
    r1 = v - hi.astype(_F32)
    mid = r1.astype(jnp.bfloat16)
    out = _dot(hi, seg) + _dot(mid, seg)
    if passes == 3:
        out = out + _dot((r1 - mid.astype(_F32)).astype(jnp.bfloat16), seg)
    return out


def _scan_fwd(a, u, h0):
    n = a.shape[0]
    row = lax.broadcasted_iota(jnp.int32, a.shape, 0)
    h, d = u, 1
    while d < n:
        keep = row >= d
        h = a * jnp.where(keep, pltpu.roll(h, d, 0), 0.0) + h
        a = a * jnp.where(keep, pltpu.roll(a, d, 0), 1.0)
        d *= 2
    return h + a * h0


def _scan_rev(c, g, g_end):
    n = c.shape[0]
    row = lax.broadcasted_iota(jnp.int32, c.shape, 0)
    d = 1
    while d < n:
        keep = row < n - d
        g = c * jnp.where(keep, pltpu.roll(g, n - d, 0), 0.0) + g
        c = c * jnp.where(keep, pltpu.roll(c, n - d, 0), 1.0)
        d *= 2
    return g + c * g_end


def _layer_norm_stats(z):
    mu = jnp.mean(z, axis=-1, keepdims=True)
    zc = z - mu
    var = jnp.mean(zc * zc, axis=-1, keepdims=True)
    rstd = lax.rsqrt(var + _LN_EPS)
    return zc * rstd, rstd


def _layer_norm_bwd(dn, n, rstd):
    return rstd * (dn - jnp.mean(dn, axis=-1, keepdims=True) - n * jnp.mean(dn * n, axis=-1, keepdims=True))


def _rowsum(v):
    return jnp.sum(v, axis=0, keepdims=True)


def _fused_exchange(body, n_in, n_out, n_scratch, n_xin, n_xout, plan, grid):
    def wrapped(*refs):
        o0 = n_in + n_xin
        s0 = o0 + n_out + n_xout
        start, finish = plan(refs[n_in:o0], refs[o0 + n_out:s0], *refs[s0 + n_scratch:])
        step = pl.program_id(0) * grid[1] + pl.program_id(1)

        @pl.when(step == 0)
        def _():
            start()

        body(*refs[:n_in], *refs[o0:o0 + n_out], *refs[s0:s0 + n_scratch])

        @pl.when(step == grid[0] * grid[1] - 1)
        def _():
            finish()

    return wrapped


def _lru_gates(xc, wr_ref, wi_ref, br_ref, bi_ref, lam_ref):
    xcb = _mx(xc)
    r = jax.nn.sigmoid(_dot(xcb, wr_ref[...]) + br_ref[...])
    i = jax.nn.sigmoid(_dot(xcb, wi_ref[...]) + bi_ref[...])
    sp = _softplus(-lam_ref[...])
    log_a = -_LRU_C * r * sp
    a = jnp.exp(log_a)
    mult = jnp.sqrt(-_expm1(2.0 * log_a))
    return r, i, sp, a, mult


def _conv_taps(ext_ref, w_ref, first, n_taps, tt):
    acc = w_ref[0:1, :] * ext_ref[pl.ds(first, tt), :]
    for k in range(1, n_taps):
        acc = acc + w_ref[k:k + 1, :] * ext_ref[pl.ds(first + k, tt), :]
    return acc


def _make_shifted(ext_ref, sh_ref):
    n = sh_ref.shape[1]
    for r in range(1, 8):
        sh_ref[r - 1] = ext_ref[pl.ds(r, n), :]


def _tap(ext_ref, sh_ref, off, tt):
    base = (off // 8) * 8
    if off % 8 == 0:
        return ext_ref[pl.ds(base, tt), :]
    return sh_ref[off % 8 - 1, pl.ds(base, tt), :]


def _conv_taps_shifted(ext_ref, sh_ref, w_ref, first, n_taps, tt):
    acc = w_ref[0:1, :] * _tap(ext_ref, sh_ref, first, tt)
    for k in range(1, n_taps):
        acc = acc + w_ref[k:k + 1, :] * _tap(ext_ref, sh_ref, first + k, tt)
    return acc


def _mix_fwd(x, mod3, win, lcw, lcb, wr_bd, wi_bd, b_r, b_i, lam, cw, cb, ng, nb, seg, wout, ln1g, ln1b, shards):
    bl, s_len, d = x.shape
    w = d // 2
    tt = min(_TT_MIX, s_len)
    ns = s_len // tt
    kc = cw.shape[0]

    def body(x_ref, mod_ref, win_ref, lcw_ref, lcb_ref, wr_ref, wi_ref, br_ref, bi_ref, lam_ref, cw_ref, cb_ref,
             ng_ref, nb_ref, seg_ref, wout_ref, g1_ref, b1_ref,
             proj_ref, h_ref, mix_ref, x1_ref, u1_ref, y_ref, vbc_ref, lru_ref, ext4, ext31, sh31, hcar):
        @pl.when(pl.program_id(1) == 0)
        def _():
            ext4[0:8, :] = jnp.zeros((8, w), _F32)
            ext31[0:_HALO, :] = jnp.zeros((_HALO, w), _F32)
            hcar[...] = jnp.zeros_like(hcar)

        xt = x_ref[...]
        sh1, sc1, gt1 = mod_ref[:, 0:d], mod_ref[:, d:2 * d], mod_ref[:, 2 * d:3 * d]
        u1 = _mx(xt * (1.0 + sc1) + sh1)
        u1_ref[...] = u1
        xa, ga, vb, gb = (_dot(u1, win_ref[k]) for k in range(4))
        proj_ref[:, 0:w] = xa
        proj_ref[:, w:2 * w] = ga
        proj_ref[:, 2 * w:3 * w] = vb
        proj_ref[:, 3 * w:4 * w] = gb

        ext4[8:8 + tt, :] = xa
        xc = lcb_ref[...] + _conv_taps(ext4, lcw_ref, 5, 4, tt)
        ext4[0:8, :] = xa[tt - 8:tt, :]
        r, i, sp, a, mult = _lru_gates(xc, wr_ref, wi_ref, br_ref, bi_ref, lam_ref)
        for k, val in enumerate((xc, r, i, a, mult)):
            lru_ref[:, k * w:(k + 1) * w] = val
        h = _scan_fwd(a, mult * (i * xc), hcar[0:1, :])
        hcar[0:1, :] = h[tt - 1:tt, :]
        h_ref[...] = h
        gelu, _ = _gelu_and_grad(ga)
        y_ref[:, 0:w] = _mx(gelu * h)

        vbg = vb * jax.nn.sigmoid(gb)
        ext31[_HALO:_HALO + tt, :] = vbg
        _make_shifted(ext31, sh31)
        vbc = cb_ref[...] + _conv_taps_shifted(ext31, sh31, cw_ref, _HALO - (kc - 1), kc, tt)
        vbc_ref[...] = vbc
        ext31[0:_HALO, :] = vbg[tt - _HALO:tt, :]
        inv = 1.0 / (w // _N_HEADS)
        zc = vbc - _seg_sum(vbc, seg_ref[...]) * inv
        n = zc * lax.rsqrt(_seg_sum(zc * zc, seg_ref[...]) * inv + _LN_EPS)
        pre = n * ng_ref[...] + nb_ref[...]
        y_ref[:, w:2 * w] = _mx(pre * jax.nn.sigmoid(pre))

        mix = _dot(y_ref[...], wout_ref[...])
        mix_ref[...] = mix
        n1, _ = _layer_norm_stats(_ALPHA * xt + (1.0 + gt1) * mix)
        x1_ref[...] = n1 * g1_ref[...] + b1_ref[...]

    tok = lambda c: pl.BlockSpec((None, tt, c), lambda b, s: (b, s, 0))
    smalls = [lcw, lcb, wr_bd, wi_bd, b_r, b_i, lam, cw, cb, ng, nb, seg, wout, ln1g, ln1b]
    nx = len(shards)
    return pl.pallas_call(
        _fused_exchange(body, 3 + len(smalls), 8, 4, nx, nx, _gather_plan, (bl, ns)), grid=(bl, ns),
        in_specs=[tok(d), pl.BlockSpec((None, 1, 6 * d), lambda b, s: (b, 0, 0)), _resident(win.shape)]
        + [_resident(t.shape) for t in smalls] + [_HBM] * nx,
        out_specs=[tok(4 * w), tok(w), tok(d), tok(d), tok(d), tok(d), tok(w), tok(5 * w)] + [_HBM] * nx,
        out_shape=[jax.ShapeDtypeStruct((bl, s_len, 4 * w), _F32), jax.ShapeDtypeStruct((bl, s_len, w), _F32),
                   jax.ShapeDtypeStruct((bl, s_len, d), _F32), jax.ShapeDtypeStruct((bl, s_len, d), _F32),
                   jax.ShapeDtypeStruct((bl, s_len, d), _MXU_DT), jax.ShapeDtypeStruct((bl, s_len, d), _MXU_DT),
                   jax.ShapeDtypeStruct((bl, s_len, w), _F32), jax.ShapeDtypeStruct((bl, s_len, 5 * w), _F32)]
        + [jax.ShapeDtypeStruct((4,) + t.shape, t.dtype) for t in shards],
        scratch_shapes=[pltpu.VMEM((tt + 8, w), _F32), pltpu.VMEM((tt + _HALO, w), _F32),
                        pltpu.VMEM((7, tt + _HALO - 8, w), _F32), pltpu.VMEM((8, w), _F32)] + _gather_sems(nx),
        compiler_params=_cparams(("arbitrary", "arbitrary")), name="mix_fwd",
    )(x, mod3, win, *smalls, *shards)


def _ffn_fwd(x1, mod3, wup, fcw, fcb, wdn, ln2g, ln2b, target):
    bl, s_len, d = x1.shape
    nch, _, fc = wup.shape
    nch //= 2
    f = nch * fc
    tt = min(_TT_FFN, s_len)
    ns = s_len // tt

    def body(x1_ref, mod_ref, wup_ref, fcw_ref, fcb_ref, wdn_ref, g2_ref, b2_ref, tgt_ref,
             u2_ref, hh_ref, f_ref, gc_ref, dz2_ref, loss_ref, dln2_ref, dgt2_ref, ext3):
        first_tile = pl.program_id(1) == 0

        @pl.when(first_tile)
        def _():
            ext3[:, 0:8, :] = jnp.zeros((nch, 8, fc), _F32)
            dgt2_ref[...] = jnp.zeros_like(dgt2_ref)

        @pl.when(first_tile & (pl.program_id(0) == 0))
        def _():
            loss_ref[...] = jnp.zeros_like(loss_ref)
            dln2_ref[...] = jnp.zeros_like(dln2_ref)

        x1t = x1_ref[...]
        sh2, sc2, gt2 = mod_ref[:, 3 * d:4 * d], mod_ref[:, 4 * d:5 * d], mod_ref[:, 5 * d:6 * d]
        u2 = _mx(x1t * (1.0 + sc2) + sh2)
        u2_ref[...] = u2
        y2 = jnp.zeros((tt, d), _F32)
        for j in range(nch):
            lanes = slice(j * fc, (j + 1) * fc)
            v = _dot(u2, wup_ref[j])
            g = _dot(u2, wup_ref[nch + j])
            hh_ref[:, lanes] = v.astype(hh_ref.dtype)
            hh_ref[:, f + j * fc:f + (j + 1) * fc] = g.astype(hh_ref.dtype)
            ext = ext3.at[j]
            ext[8:8 + tt, :] = g
            gc = fcb_ref[:, lanes] + sum(fcw_ref[k:k + 1, lanes] * ext[pl.ds(6 + k, tt), :] for k in range(3))
            gc_ref[:, lanes] = gc
            ext[0:8, :] = g[tt - 8:tt, :]
            fj = _mx(gc * jax.nn.sigmoid(gc) * v)
            f_ref[:, lanes] = fj
            y2 = y2 + _dot(fj, wdn_ref[lanes, :])

        n2, rstd = _layer_norm_stats(_ALPHA * x1t + (1.0 + gt2) * y2)
        err = n2 * g2_ref[...] + b2_ref[...] - tgt_ref[...]
        loss_ref[...] += jnp.sum(_rowsum(err * err), axis=1, keepdims=True)
        dout = err * (1.0 / d)
        dln2_ref[0:1, :] += _rowsum(dout * n2)
        dln2_ref[1:2, :] += _rowsum(dout)
        dz2 = _layer_norm_bwd(dout * g2_ref[...], n2, rstd)
        dz2_ref[...] = dz2
        dgt2_ref[...] += _rowsum(dz2 * y2)

    tok = lambda c: pl.BlockSpec((None, tt, c), lambda b, s: (b, s, 0))
    acc = lambda r: pl.BlockSpec((r, d), lambda b, s: (0, 0))
    smalls = [fcw, fcb, wdn, ln2g, ln2b]
    return pl.pallas_call(
        body, grid=(bl, ns),
        in_specs=[tok(d), pl.BlockSpec((None, 1, 6 * d), lambda b, s: (b, 0, 0)), _resident(wup.shape)]
        + [_resident(t.shape) for t in smalls] + [tok(d)],
        out_specs=[tok(d), tok(2 * f), tok(f), tok(f), tok(d), acc(1), acc(2), pl.BlockSpec((None, 1, d), lambda b, s: (b, 0, 0))],
        out_shape=[jax.ShapeDtypeStruct((bl, s_len, d), _MXU_DT), jax.ShapeDtypeStruct((bl, s_len, 2 * f), _F32),
                   jax.ShapeDtypeStruct((bl, s_len, f), _MXU_DT), jax.ShapeDtypeStruct((bl, s_len, f), _F32),
                   jax.ShapeDtypeStruct((bl, s_len, d), _F32), jax.ShapeDtypeStruct((1, d), _F32), jax.ShapeDtypeStruct((2, d), _F32),
                   jax.ShapeDtypeStruct((bl, 1, d), _F32)],
        scratch_shapes=[pltpu.VMEM((nch, tt + 8, fc), _F32)],
        compiler_params=_cparams(("arbitrary", "arbitrary")), name="ffn_fwd",
    )(x1, mod3, wup, *smalls, target)


def _ffn_bwd(dz2, x1, hh, gc_all, mod3, wup, wdn, fcw, fcb):
    bl, s_len, d = x1.shape
    nch, _, fc = wup.shape
    nch //= 2
    f = nch * fc
    tt = min(_TT_FFN, s_len)
    ns = s_len // tt

    def body(dz2_ref, x1_ref, hh_ref, halo_ref, gc_ref, mod_ref, wup_ref, wdn_ref, fcw_ref, fcb_ref,
             dx1_ref, dy2_ref, dh_ref, dfc_ref, dmod_ref, gext, dext, dcar):
        s = ns - 1 - pl.program_id(1)

        @pl.when(pl.program_id(1) == 0)
        def _():
            dcar[...] = jnp.zeros_like(dcar)
            dmod_ref[...] = jnp.zeros_like(dmod_ref)

        @pl.when((pl.program_id(1) == 0) & (pl.program_id(0) == 0))
        def _():
            dfc_ref[...] = jnp.zeros_like(dfc_ref)

        sc2, gt2 = mod_ref[:, 4 * d:5 * d], mod_ref[:, 5 * d:6 * d]
        dz2t = dz2_ref[...]
        dy2 = _mx((1.0 + gt2) * dz2t)
        dy2_ref[...] = dy2
        du2 = jnp.zeros((tt, d), _F32)
        for j in range(nch):
            lanes = slice(j * fc, (j + 1) * fc)
            glanes = slice(f + j * fc, f + (j + 1) * fc)
            v = hh_ref[:, lanes].astype(_F32)
            g = hh_ref[:, glanes].astype(_F32)
            gext[0:8, :] = jnp.where(s == 0, 0.0, halo_ref[:, glanes].astype(_F32))
            gext[8:8 + tt, :] = g
            gc = gc_ref[:, lanes]
            sg = jax.nn.sigmoid(gc)
            df = _dot_nt(dy2, wdn_ref[lanes, :])
            dv = df * (gc * sg)
            dgc = df * v * (sg * (1.0 + gc * (1.0 - sg)))
            dfc_ref[3:4, lanes] += _rowsum(dgc)
            dext[0:tt, :] = dgc
            dext[tt:tt + 8, :] = dcar[j]
            dcar[j] = dgc[0:8, :]
            dg = jnp.zeros((tt, fc), _F32)
            for k in range(3):
                dg = dg + fcw_ref[k:k + 1, lanes] * dext[pl.ds(2 - k, tt), :]
                dfc_ref[k:k + 1, lanes] += _rowsum(dgc * gext[pl.ds(6 + k, tt), :])
            dvb, dgb = _mx(dv), _mx(dg)
            dh_ref[:, lanes] = dvb
            dh_ref[:, glanes] = dgb
            du2 = du2 + _dot_nt(dvb, wup_ref[j]) + _dot_nt(dgb, wup_ref[nch + j])

        dx1_ref[...] = _ALPHA * dz2t + du2 * (1.0 + sc2)
        dmod_ref[0:1, :] += _rowsum(du2)
        dmod_ref[1:2, :] += _rowsum(du2 * x1_ref[...])

    tok = lambda c: pl.BlockSpec((None, tt, c), lambda b, i: (b, ns - 1 - i, 0))
    halo = pl.BlockSpec((None, 8, 2 * f), lambda b, i: (b, jnp.maximum((ns - 1 - i) * (tt // 8) - 1, 0), 0))
    return pl.pallas_call(
        body, grid=(bl, ns),
        in_specs=[tok(d), tok(d), tok(2 * f), halo, tok(f), pl.BlockSpec((None, 1, 6 * d), lambda b, i: (b, 0, 0)),
                  _resident(wup.shape), _resident(wdn.shape), _resident(fcw.shape), _resident(fcb.shape)],
        out_specs=[tok(d), tok(d), tok(2 * f), pl.BlockSpec((4, f), lambda b, i: (0, 0)),
                   pl.BlockSpec((None, 2, d), lambda b, i: (b, 0, 0))],
        out_shape=[jax.ShapeDtypeStruct((bl, s_len, d), _F32), jax.ShapeDtypeStruct((bl, s_len, d), _MXU_DT),
                   jax.ShapeDtypeStruct((bl, s_len, 2 * f), _MXU_DT), jax.ShapeDtypeStruct((4, f), _F32),
                   jax.ShapeDtypeStruct((bl, 2, d), _F32)],
        scratch_shapes=[pltpu.VMEM((tt + 8, fc), _F32), pltpu.VMEM((tt + 8, fc), _F32), pltpu.VMEM((nch, 8, fc), _F32)],
        compiler_params=_cparams(("arbitrary", "arbitrary")), name="ffn_bwd",
    )(dz2, x1, hh, hh, gc_all, mod3, wup, wdn, fcw, fcb)


def _mix_bwd(dx1, x, mix, proj, h, vbc, lru, mod3, win, lcw, lcb, wr_bd, wi_bd, b_r, b_i, lam, cw, cb, ng, nb, seg, wout, ln1g, chip_sums):
    bl, s_len, d = x.shape
    w = d // 2
    tt = min(_TT_MIX, s_len)
    ns = s_len // tt
    kc = cw.shape[0]

    def body(dx1_ref, x_ref, mix_ref, proj_ref, phalo_ref, h_ref, hhalo_ref, vbc_ref, lru_ref, mod_ref, win_ref, lcw_ref, lcb_ref,
             wr_ref, wi_ref, br_ref, bi_ref, lam_ref, cw_ref, cb_ref, ng_ref, nb_ref, seg_ref, wout_ref, g1_ref,
             gx_ref, dproj_ref, dmix_ref, xcg_ref, vecw_ref, dlcw_ref, dcw_ref, dln1_ref, dmod_ref,
             ext4, ext31, dext4, dext31, sh31, dsh31, car4, car31, gcar):
        s = ns - 1 - pl.program_id(1)
        first = s == 0

        @pl.when(pl.program_id(1) == 0)
        def _():
            car4[...] = jnp.zeros_like(car4)
            car31[...] = jnp.zeros_like(car31)
            gcar[...] = jnp.zeros_like(gcar)
            dmod_ref[...] = jnp.zeros_like(dmod_ref)

        @pl.when((pl.program_id(1) == 0) & (pl.program_id(0) == 0))
        def _():
            for ref in (vecw_ref, dlcw_ref, dcw_ref, dln1_ref):
                ref[...] = jnp.zeros_like(ref)

        xt, mixt = x_ref[...], mix_ref[...]
        sh1, sc1, gt1 = mod_ref[:, 0:d], mod_ref[:, d:2 * d], mod_ref[:, 2 * d:3 * d]

        n1, rstd1 = _layer_norm_stats(_ALPHA * xt + (1.0 + gt1) * mixt)
        dx1t = dx1_ref[...]
        dln1_ref[0:1, :] += _rowsum(dx1t * n1)
        dln1_ref[1:2, :] += _rowsum(dx1t)
        dz1 = _layer_norm_bwd(dx1t * g1_ref[...], n1, rstd1)
        dmod_ref[2:3, :] += _rowsum(dz1 * mixt)
        dmix = _mx((1.0 + gt1) * dz1)
        dmix_ref[...] = dmix
        dya = _dot_nt(dmix, wout_ref[0:w, :])
        dyb = _dot_nt(dmix, wout_ref[w:2 * w, :])

        xa, ga = proj_ref[:, 0:w], proj_ref[:, w:2 * w]
        vb, gb = proj_ref[:, 2 * w:3 * w], proj_ref[:, 3 * w:4 * w]

        sgb = jax.nn.sigmoid(gb)
        vbg = vb * sgb
        hv, hg = phalo_ref[:, 2 * w:3 * w], phalo_ref[:, 3 * w:4 * w]
        ext31[0:_HALO, :] = jnp.where(first, 0.0, hv * jax.nn.sigmoid(hg))
        ext31[_HALO:_HALO + tt, :] = vbg
        _make_shifted(ext31, sh31)
        vbc = vbc_ref[...]
        inv = 1.0 / (w // _N_HEADS)
        zc = vbc - _seg_sum(vbc, seg_ref[...]) * inv
        rstd = lax.rsqrt(_seg_sum(zc * zc, seg_ref[...]) * inv + _LN_EPS)
        n = zc * rstd
        pre = n * ng_ref[...] + nb_ref[...]
        sgp = jax.nn.sigmoid(pre)
        dpre = dyb * (sgp * (1.0 + pre * (1.0 - sgp)))
        vecw_ref[5:6, :] += _rowsum(dpre * n)
        vecw_ref[6:7, :] += _rowsum(dpre)
        dn = dpre * ng_ref[...]
        dvbc = rstd * (dn - _seg_sum(dn, seg_ref[...], 2) * inv - n * (_seg_sum(dn * n, seg_ref[...], 2) * inv))
        vecw_ref[4:5, :] += _rowsum(dvbc)
        dext31[0:tt, :] = dvbc
        dext31[tt:tt + _HALO, :] = car31[...]
        car31[...] = dvbc[0:_HALO, :]
        _make_shifted(dext31, dsh31)
        dvbg = jnp.zeros((tt, w), _F32)
        for k in range(kc):
            dvbg = dvbg + cw_ref[k:k + 1, :] * _tap(dext31, dsh31, kc - 1 - k, tt)
            dcw_ref[k:k + 1, :] += _rowsum(dvbc * _tap(ext31, sh31, _HALO - (kc - 1) + k, tt))
        dproj_ref[:, 2 * w:3 * w] = _mx(dvbg * sgb)
        dproj_ref[:, 3 * w:4 * w] = _mx(dvbg * vb * (sgb * (1.0 - sgb)))

        ext4[0:8, :] = jnp.where(first, 0.0, phalo_ref[_HALO - 8:_HALO, 0:w])
        ext4[8:8 + tt, :] = xa
        xc, r, i, a, mult = (lru_ref[:, k * w:(k + 1) * w] for k in range(5))
        xcg_ref[:, 0:w] = _mx(xc)
        sp = _softplus(-lam_ref[...])
        ht = h_ref[...]
        row = lax.broadcasted_iota(jnp.int32, (tt, w), 0)
        h_before = jnp.where(first, 0.0, hhalo_ref[7:8, :])
        hprev = jnp.where(row == 0, h_before, pltpu.roll(ht, 1, 0))
        gelu, dgelu = _gelu_and_grad(ga)
        dproj_ref[:, w:2 * w] = _mx(dya * ht * dgelu)
        dh = dya * gelu
        coef = jnp.where(row == tt - 1, 1.0, pltpu.roll(a, tt - 1, 0))
        big_g = _scan_rev(coef, dh, gcar[0:1, :])
        gcar[0:1, :] = a[0:1, :] * big_g[0:1, :]
        da = big_g * hprev
        ixc = i * xc
        dlog_a = da * a - (big_g * ixc) * (a * a / mult)
        di = big_g * mult * xc
        dxc = big_g * mult * i
        vecw_ref[3:4, :] += _rowsum(dlog_a * r) * (_LRU_C * jax.nn.sigmoid(-lam_ref[...]))
        dgr_f = dlog_a * (-_LRU_C * sp) * (r * (1.0 - r))
        dgi_f = di * (i * (1.0 - i))
        vecw_ref[1:2, :] += _rowsum(dgr_f)
        vecw_ref[2:3, :] += _rowsum(dgi_f)
        dgr, dgi = _mx(dgr_f), _mx(dgi_f)
        xcg_ref[:, w:2 * w] = dgr
        xcg_ref[:, 2 * w:3 * w] = dgi
        dxc = dxc + _dot_nt(dgr, wr_ref[...]) + _dot_nt(dgi, wi_ref[...])
        vecw_ref[0:1, :] += _rowsum(dxc)
        dext4[0:tt, :] = dxc
        dext4[tt:tt + 8, :] = car4[...]
        car4[...] = dxc[0:8, :]
        dxa = jnp.zeros((tt, w), _F32)
        for k in range(4):
            dxa = dxa + lcw_ref[k:k + 1, :] * dext4[pl.ds(3 - k, tt), :]
            dlcw_ref[k:k + 1, :] += _rowsum(dxc * ext4[pl.ds(5 + k, tt), :])
        dproj_ref[:, 0:w] = _mx(dxa)

        du1 = sum(_dot_nt(dproj_ref[:, k * w:(k + 1) * w], win_ref[k]) for k in range(4))
        gx_ref[...] = _ALPHA * dz1 + du1 * (1.0 + sc1)
        dmod_ref[0:1, :] += _rowsum(du1)
        dmod_ref[1:2, :] += _rowsum(du1 * xt)

    tok = lambda c: pl.BlockSpec((None, tt, c), lambda b, i: (b, ns - 1 - i, 0))
    halo = lambda rows, c: pl.BlockSpec(
        (None, rows, c), lambda b, i: (b, jnp.maximum((ns - 1 - i) * (tt // rows) - 1, 0), 0))
    accw = lambda r, c: pl.BlockSpec((r, c), lambda b, i: (0, 0))
    smalls = [lcw, lcb, wr_bd, wi_bd, b_r, b_i, lam, cw, cb, ng, nb, seg, wout, ln1g]
    nx = len(chip_sums)
    return pl.pallas_call(
        _fused_exchange(body, 11 + len(smalls), 9, 9, nx, nx, _chip_reduce_plan, (bl, ns)), grid=(bl, ns),
        in_specs=[tok(d), tok(d), tok(d), tok(4 * w), halo(_HALO, 4 * w), tok(w), halo(8, w), tok(w), tok(5 * w),
                  pl.BlockSpec((None, 1, 6 * d), lambda b, i: (b, 0, 0)), _resident(win.shape)]
        + [_resident(t.shape) for t in smalls] + [_HBM] * nx,
        out_specs=[tok(d), tok(4 * w), tok(d), tok(3 * w), accw(8, w), accw(4, w), accw(kc, w), accw(2, d),
                   pl.BlockSpec((None, 3, d), lambda b, i: (b, 0, 0))] + [_HBM] * nx,
        out_shape=[jax.ShapeDtypeStruct((bl, s_len, d), _F32), jax.ShapeDtypeStruct((bl, s_len, 4 * w), _MXU_DT),
                   jax.ShapeDtypeStruct((bl, s_len, d), _MXU_DT), jax.ShapeDtypeStruct((bl, s_len, 3 * w), _MXU_DT),
                   jax.ShapeDtypeStruct((8, w), _F32), jax.ShapeDtypeStruct((4, w), _F32),
                   jax.ShapeDtypeStruct((kc, w), _F32), jax.ShapeDtypeStruct((2, d), _F32),
                   jax.ShapeDtypeStruct((bl, 3, d), _F32)]
        + [jax.ShapeDtypeStruct((3,) + t.shape[1:], t.dtype) for t in chip_sums],
        scratch_shapes=[pltpu.VMEM((tt + 8, w), _F32), pltpu.VMEM((tt + _HALO, w), _F32),
                        pltpu.VMEM((tt + 8, w), _F32), pltpu.VMEM((tt + _HALO, w), _F32),
                        pltpu.VMEM((7, tt + _HALO - 8, w), _F32), pltpu.VMEM((7, tt + _HALO - 8, w), _F32),
                        pltpu.VMEM((8, w), _F32), pltpu.VMEM((_HALO, w), _F32), pltpu.VMEM((8, w), _F32)]
        + _chip_reduce_sems(nx),
        compiler_params=_cparams(("arbitrary", "arbitrary")), name="mix_bwd",
    )(dx1, x, mix, proj, proj, h, h, vbc, lru, mod3, win, *smalls, *chip_sums)


def _wgrad(a, b, ma, nbw, na, nb, a_off, b_off, name, exchange=None):
    t = a.shape[0]
    tk = min(_TK_WGRAD, t)
    grid = (na * nb, t // tk)

    def body(a_ref, b_ref, o_ref):
        @pl.when(pl.program_id(1) == 0)
        def _():
            o_ref[...] = jnp.zeros_like(o_ref)
        o_ref[...] += _dot_tn(a_ref[...], b_ref[...])

    xin, xshapes, plan, sems = exchange if exchange else ([], [], None, [])
    nx = len(xin)
    res = pl.pallas_call(
        _fused_exchange(body, 2, 1, 0, nx, len(xshapes), plan, grid) if exchange else body, grid=grid,
        in_specs=[pl.BlockSpec((tk, ma), lambda j, k: (k, j // nb + a_off)),
                  pl.BlockSpec((tk, nbw), lambda j, k: (k, j % nb + b_off))] + [_HBM] * nx,
        out_specs=[pl.BlockSpec((None, ma, nbw), lambda j, k: (j, 0, 0))] + [_HBM] * len(xshapes),
        out_shape=[jax.ShapeDtypeStruct((na * nb, ma, nbw), _F32)] + list(xshapes),
        scratch_shapes=list(sems),
        compiler_params=_cparams(("arbitrary", "arbitrary")), name=name,
    )(a, b, *xin)
    return res if exchange else res[0]


_DEV_DELTAS = tuple(dl for dl in itertools.product((0, 1), repeat=3) if any(dl))
_HBM = pl.BlockSpec(memory_space=pltpu.HBM)
_VMEM = pl.BlockSpec(memory_space=pltpu.VMEM)


def _pos():
    return lax.axis_index("x"), lax.axis_index("y"), lax.axis_index("c")


def _flip(v, delta):
    return 1 - v if delta else v


def _remote(src, dst, ssem, rsem, dev):
    return pltpu.make_async_remote_copy(src_ref=src, dst_ref=dst, send_sem=ssem, recv_sem=rsem,
                                        device_id=dev, device_id_type=_MESH)


def _rows(ref, idx, n):
    return ref.at[pl.ds(pl.multiple_of(idx * n, 8), n)]


def _ada_fwd(c8, w_ada_k, b_ada_k, shards):
    rows, d = c8.shape
    nk = w_ada_k.shape[1]
    n = len(shards)

    def body(*refs):
        c_ref, w_ref, b_ref = refs[:3]
        call_ref, mod_ref = refs[3 + n:5 + n]
        modloc, modrcv, s1, r1, s2, r2 = refs[5 + 2 * n:11 + 2 * n]
        gather_start, gather_finish = _gather_plan(refs[3:3 + n], refs[5 + n:5 + 2 * n], *refs[11 + 2 * n:14 + 2 * n],
                                                   bounce=refs[14 + 2 * n:])
        gather_start()
        xi, yi, ci = _pos()
        me, kme = 4 * xi + 2 * yi + ci, 2 * xi + yi
        call_ref[pl.ds(pl.multiple_of(me * rows, 8), rows), :] = c_ref[...]
        sends = []
        for p, (dx, dy, dc) in enumerate(_DEV_DELTAS):
            cp = _remote(c_ref, _rows(call_ref, me, rows), s1.at[p], r1.at[p], (_flip(xi, dx), _flip(yi, dy), _flip(ci, dc)))
            cp.start()
            sends.append(cp)
        for p, (dx, dy, dc) in enumerate(_DEV_DELTAS):
            src = 4 * _flip(xi, dx) + 2 * _flip(yi, dy) + _flip(ci, dc)
            _remote(c_ref, _rows(call_ref, src, rows), s1.at[p], r1.at[p], (xi, yi, ci)).wait_recv()
        for cp in sends:
            cp.wait_send()

        ca = call_ref[...]
        modloc[...] = _dot(_mx(ca * jax.nn.sigmoid(ca)), _mx(w_ref[...])) + b_ref[...]
        modrcv[kme] = modloc[pl.ds(pl.multiple_of(me * rows, 8), rows), :]
        sends = []
        for j, (dx, dy) in enumerate(_CHIP_DELTAS):
            tx, ty = _flip(xi, dx), _flip(yi, dy)
            cp = _remote(_rows(modloc, 4 * tx + 2 * ty + ci, rows), modrcv.at[kme], s2.at[j], r2.at[j], (tx, ty, ci))
            cp.start()
            sends.append(cp)
        for j, (dx, dy) in enumerate(_CHIP_DELTAS):
            ksrc = 2 * _flip(xi, dx) + _flip(yi, dy)
            _remote(_rows(modloc, me, rows), modrcv.at[ksrc], s2.at[j], r2.at[j], (xi, yi, ci)).wait_recv()
        for cp in sends:
            cp.wait_send()
        for j in range(4):
            mod_ref[:, j * nk:(j + 1) * nk] = modrcv[j]
        gather_finish()

    return pl.pallas_call(
        body, in_specs=[_VMEM, _VMEM, _VMEM] + [_HBM] * n, out_specs=[_VMEM, _VMEM] + [_HBM] * n,
        out_shape=[jax.ShapeDtypeStruct((8 * rows, d), _F32), jax.ShapeDtypeStruct((rows, 4 * nk), _F32)]
        + [jax.ShapeDtypeStruct((4,) + a.shape, a.dtype) for a in shards],
        scratch_shapes=[pltpu.VMEM((8 * rows, nk), _F32), pltpu.VMEM((4, rows, nk), _F32),
                        pltpu.SemaphoreType.DMA((7,)), pltpu.SemaphoreType.DMA((7,)),
                        pltpu.SemaphoreType.DMA((3,)), pltpu.SemaphoreType.DMA((3,))]
        + _gather_sems(n) + [pltpu.VMEM(a.shape, a.dtype) for a in shards],
        compiler_params=pltpu.CompilerParams(vmem_limit_bytes=_VMEM_LIMIT), name="ada_fwd",
    )(c8, w_ada_k, b_ada_k, *shards)


def _gather_sems(n):
    return [pltpu.SemaphoreType.DMA((3, n)), pltpu.SemaphoreType.DMA((3, n)), pltpu.SemaphoreType.DMA((n,))]


def _gather_plan(ins, outs, ssem, rsem, lsem, bounce=()):
    n = len(ins)
    xi, yi, ci = _pos()
    kme = 2 * xi + yi
    staged = [pltpu.make_async_copy(ins[a], bounce[a], lsem.at[a]) for a in range(len(bounce))]
    local = [pltpu.make_async_copy(bounce[a] if bounce else ins[a], outs[a].at[kme], lsem.at[a]) for a in range(n)]
    sends, recvs = [], []
    for j, (dx, dy) in enumerate(_CHIP_DELTAS):
        tx, ty = _flip(xi, dx), _flip(yi, dy)
        for a in range(n):
            sends.append(_remote(ins[a], outs[a].at[kme], ssem.at[j, a], rsem.at[j, a], (tx, ty, ci)))
            recvs.append(_remote(ins[a], outs[a].at[2 * tx + ty], ssem.at[j, a], rsem.at[j, a], (xi, yi, ci)))

    def start():
        for cp in sends + staged:
            cp.start()
        for cp in staged:
            cp.wait()
        for cp in local:
            cp.start()

    def finish():
        for cp in recvs:
            cp.wait_recv()
        for cp in sends:
            cp.wait_send()
        for cp in local:
            cp.wait()

    return start, finish


def _dev_gather_sems(n):
    return [pltpu.SemaphoreType.DMA((7, n)), pltpu.SemaphoreType.DMA((7, n)), pltpu.SemaphoreType.DMA((n,))]


def _dev_gather_plan(ins, outs, ssem, rsem, lsem):
    n = len(ins)
    xi, yi, ci = _pos()
    me = 4 * xi + 2 * yi + ci
    local = [pltpu.make_async_copy(ins[a], outs[a].at[me], lsem.at[a]) for a in range(n)]
    sends, recvs = [], []
    for p, (dx, dy, dc) in enumerate(_DEV_DELTAS):
        tx, ty, tc = _flip(xi, dx), _flip(yi, dy), _flip(ci, dc)
        for a in range(n):
            sends.append(_remote(ins[a], outs[a].at[me], ssem.at[p, a], rsem.at[p, a], (tx, ty, tc)))
            recvs.append(_remote(ins[a], outs[a].at[4 * tx + 2 * ty + tc], ssem.at[p, a], rsem.at[p, a], (xi, yi, ci)))

    def start():
        for cp in local + sends:
            cp.start()

    def finish():
        for cp in recvs:
            cp.wait_recv()
        for cp in sends:
            cp.wait_send()
        for cp in local:
            cp.wait()

    return start, finish


def _pair_sems(n):
    return [pltpu.SemaphoreType.DMA((n,)), pltpu.SemaphoreType.DMA((n,))]


def _pair_plan(ins, outs, ssem, rsem):
    xi, yi, ci = _pos()
    sends = []
    for a in range(len(ins)):
        r2 = ins[a].shape[1] // 2
        src = ins[a].at[:, pl.ds(pl.multiple_of((1 - ci) * r2, 8), r2), :]
        sends.append(_remote(src, outs[a], ssem.at[a], rsem.at[a], (xi, yi, 1 - ci)))

    def start():
        for cp in sends:
            cp.start()

    def finish():
        for cp in sends:
            cp.wait_recv()
        for cp in sends:
            cp.wait_send()

    return start, finish


def _chip_reduce_sems(n):
    return [pltpu.SemaphoreType.DMA((3, n)), pltpu.SemaphoreType.DMA((3, n))]


def _chip_reduce_plan(ins, outs, ssem, rsem):
    xi, yi, ci = _pos()
    sends = []
    for j, (dx, dy) in enumerate(_CHIP_DELTAS):
        tx, ty = _flip(xi, dx), _flip(yi, dy)
        sends += [_remote(ins[a].at[2 * tx + ty], outs[a].at[j], ssem.at[j, a], rsem.at[j, a], (tx, ty, ci))
                  for a in range(len(ins))]

    def start():
        for cp in sends:
            cp.start()

    def finish():
        for cp in sends:
            cp.wait_recv()
        for cp in sends:
            cp.wait_send()

    return start, finish


def _pair_exchange(gs, name):
    n = len(gs)

    def body(*refs):
        start, finish = _pair_plan(refs[:n], refs[n:2 * n], *refs[2 * n:])
        start()
        finish()

    return pl.pallas_call(
        body, in_specs=[_HBM] * n, out_specs=[_HBM] * n, out_shape=_pair_out_shapes(gs),
        scratch_shapes=_pair_sems(n), name=name,
    )(*gs)


def _pair_out_shapes(gs):
    return [jax.ShapeDtypeStruct((g.shape[0], g.shape[1] // 2, g.shape[2]), g.dtype) for g in gs]


def _row_tile(r):
    return max(t for t in range(8, min(r, 256) + 1, 8) if r % t == 0)


def _pair_add(g, r, cidx, name, wire_dtype=None):
    nk, r2, c = r.shape
    tr = _row_tile(r2)
    nt = r2 // tr

    def body(c_ref, g_ref, r_ref, *o_refs):
        s = g_ref[...] + r_ref[...]
        for o_ref in o_refs:
            o_ref[...] = s.astype(o_ref.dtype)

    out_spec = pl.BlockSpec((None, tr, c), lambda k, i, cr: (k, i, 0))
    dtypes = [_F32] + ([wire_dtype] if wire_dtype else [])
    res = pl.pallas_call(
        body, grid_spec=pltpu.PrefetchScalarGridSpec(
            num_scalar_prefetch=1, grid=(nk, nt),
            in_specs=[pl.BlockSpec((None, tr, c), lambda k, i, cr: (k, cr[0] * nt + i, 0)), out_spec],
            out_specs=[out_spec] * len(dtypes)),
        out_shape=[jax.ShapeDtypeStruct(r.shape, dt) for dt in dtypes],
        compiler_params=_cparams(("arbitrary", "arbitrary")), name=name,
    )(cidx, g, r)
    return res if wire_dtype else res[0]


def _chip_exchange(ss):
    n = len(ss)

    def body(*refs):
        start, finish = _chip_reduce_plan(refs[:n], refs[n:2 * n], *refs[2 * n:])
        start()
        finish()

    return pl.pallas_call(
        body, in_specs=[_HBM] * n, out_specs=[_HBM] * n,
        out_shape=[jax.ShapeDtypeStruct((3,) + s.shape[1:], s.dtype) for s in ss],
        scratch_shapes=_chip_reduce_sems(n), name="grad_chip_exchange",
    )(*ss)


def _chip_add(s, r, kidx, name):
    _, r2, c = r.shape
    tr = _row_tile(r2)

    def body(k_ref, s_ref, r_ref, o_ref):
        o_ref[...] = ((s_ref[...] + r_ref[0].astype(_F32)) + r_ref[1].astype(_F32)) + r_ref[2].astype(_F32)

    return pl.pallas_call(
        body, grid_spec=pltpu.PrefetchScalarGridSpec(
            num_scalar_prefetch=1, grid=(r2 // tr,),
            in_specs=[pl.BlockSpec((None, tr, c), lambda i, kr: (kr[0], i, 0)),
                      pl.BlockSpec((3, tr, c), lambda i, kr: (0, i, 0))],
            out_specs=pl.BlockSpec((tr, c), lambda i, kr: (i, 0))),
        out_shape=jax.ShapeDtypeStruct((r2, c), _F32),
        compiler_params=_cparams(("arbitrary",)), name=name,
    )(kidx, s, r)


def _pair_swap(hs):
    n = len(hs)

    def body(*refs):
        ins, outs = refs[:n], refs[n:2 * n]
        ssem, rsem = refs[2 * n:]
        xi, yi, ci = _pos()
        sends = [_remote(ins[a], outs[a], ssem.at[a], rsem.at[a], (xi, yi, 1 - ci)) for a in range(n)]
        for cp in sends:
            cp.start()
        for cp in sends:
            cp.wait_recv()
        for cp in sends:
            cp.wait_send()

    return pl.pallas_call(
        body, in_specs=[_HBM] * n, out_specs=[_HBM] * n,
        out_shape=[jax.ShapeDtypeStruct(h.shape, h.dtype) for h in hs],
        scratch_shapes=[pltpu.SemaphoreType.DMA((n,)), pltpu.SemaphoreType.DMA((n,))], name="grad_pair_swap",
    )(*hs)


def _small_sum(every):
    def body(all_ref, sum_ref):
        tot = all_ref[0]
        for dev in range(1, 8):
            tot = tot + all_ref[dev]
        sum_ref[...] = tot

    return pl.pallas_call(
        body, in_specs=[_VMEM], out_specs=_VMEM, out_shape=jax.ShapeDtypeStruct(every.shape[1:], _F32),
        compiler_params=pltpu.CompilerParams(vmem_limit_bytes=_VMEM_LIMIT), name="small_sum",
    )(every)


def _adamw(w, g, m, v):
    m = _ADAM_B1 * m + (1.0 - _ADAM_B1) * g
    v = _ADAM_B2 * v + (1.0 - _ADAM_B2) * (g * g)
    m_hat = m / (1.0 - _ADAM_B1 ** _ADAM_STEP)
    v_hat = v / (1.0 - _ADAM_B2 ** _ADAM_STEP)
    return -_ADAM_LR * (m_hat / (jnp.sqrt(v_hat) + _ADAM_EPS) + _ADAM_WD * w), m, v


def _adamw_big(w, g_mine, g_theirs, m, v, cidx, name):
    r, c = w.shape
    tr = _row_tile(r // 2)
    nt = r // 2 // tr

    def body(c_ref, w_ref, gm_ref, gt_ref, m_ref, v_ref, g_ref, d_ref, mo_ref, vo_ref):
        g = jnp.where(pl.program_id(0) // nt == c_ref[0], gm_ref[...], gt_ref[...])
        g_ref[...] = g
        d_ref[...], mo_ref[...], vo_ref[...] = _adamw(w_ref[...], g, m_ref[...], v_ref[...])

    spec = pl.BlockSpec((tr, c), lambda i, cr: (i, 0))
    half = pl.BlockSpec((tr, c), lambda i, cr: (i % nt, 0))
    return pl.pallas_call(
        body, grid_spec=pltpu.PrefetchScalarGridSpec(
            num_scalar_prefetch=1, grid=(2 * nt,), in_specs=[spec, half, half, spec, spec], out_specs=[spec] * 4),
        out_shape=[jax.ShapeDtypeStruct((r, c), _F32)] * 4,
        compiler_params=_cparams(("arbitrary",)), name=name,
    )(cidx, w, g_mine, g_theirs, m, v)


def _adamw_small(ws, gs, ms, vs):
    n = len(ws)
    summed = [i for i in range(n) if gs[i].shape != ws[i].shape]

    def body(*refs):
        w_r, g_r, m_r, v_r = (refs[i * n:(i + 1) * n] for i in range(4))
        outs = refs[4 * n:]
        for i in range(n):
            g = g_r[i][...]
            if i in summed:
                g = _rowsum(g)
                outs[3 * n + summed.index(i)][...] = g
            outs[i][...], outs[n + i][...], outs[2 * n + i][...] = _adamw(w_r[i][...], g, m_r[i][...], v_r[i][...])

    shapes = [jax.ShapeDtypeStruct(w.shape, _F32) for w in ws]
    res = pl.pallas_call(
        body, in_specs=[_VMEM] * (4 * n), out_specs=[_VMEM] * (3 * n + len(summed)),
        out_shape=shapes * 3 + [shapes[i] for i in summed],
        compiler_params=pltpu.CompilerParams(vmem_limit_bytes=_VMEM_LIMIT), name="adamw_small",
    )(*ws, *gs, *ms, *vs)
    gs = list(gs)
    for pos, i in enumerate(summed):
        gs[i] = res[3 * n + pos]
    return gs, res[:n], res[n:2 * n], res[2 * n:3 * n]


def _ada_bwd(c_all, dmod_k, w, m, v):
    d, nk = w.shape
    tn = 512 if nk % 512 == 0 else nk

    def body(c_ref, dm_ref, w_ref, m_ref, v_ref, g_ref, d_ref, mo_ref, vo_ref):
        ca = c_ref[...]
        g = _dot_tn(_mx(ca * jax.nn.sigmoid(ca)), _mx(dm_ref[...]))
        g_ref[...] = g
        d_ref[...], mo_ref[...], vo_ref[...] = _adamw(w_ref[...], g, m_ref[...], v_ref[...])

    col = pl.BlockSpec((d, tn), lambda j: (0, j))
    return pl.pallas_call(
        body, grid=(nk // tn,),
        in_specs=[pl.BlockSpec(c_all.shape, lambda j: (0, 0)), pl.BlockSpec((c_all.shape[0], tn), lambda j: (0, j)), col, col, col],
        out_specs=[col] * 4, out_shape=[jax.ShapeDtypeStruct((d, nk), _F32)] * 4,
        compiler_params=_cparams(("arbitrary",)), name="ada_bwd",
    )(c_all, dmod_k, w, m, v)


def _block_diag(wh):
    hn, dh, _ = wh.shape
    eye = jnp.eye(hn, dtype=wh.dtype)
    return (eye[:, None, :, None] * wh[:, :, None, :]).reshape(hn * dh, hn * dh)


def _pack(pieces):
    out = []
    for p in pieces:
        flat = p.reshape(-1, 128)
        out.append(jnp.pad(flat, ((0, (-flat.shape[0]) % 8), (0, 0))))
    return jnp.concatenate(out, axis=0)


def _unpack(pack, shapes):
    out, off = [], 0
    for shp in shapes:
        rows = math.prod(shp) // 128
        out.append(pack[..., off:off + rows, :].reshape(pack.shape[:-2] + tuple(shp)))
        off += rows + (-rows) % 8
    return out


_WEIGHTS = ('w_ada', 'b_ada', 'w_in', 'lru_conv_w', 'lru_conv_b', 'lru_w_r', 'lru_b_r', 'lru_w_i', 'lru_b_i', 'lru_lambda',
            'conv_w', 'conv_b', 'conv_norm_g', 'conv_norm_b', 'w_out', 'ln1_g', 'ln1_b', 'ffn_w_up', 'ffn_conv_w',
            'ffn_conv_b', 'ffn_w_down', 'ln2_g', 'ln2_b')
_BIG = ('w_in', 'w_out', 'ffn_w_up', 'ffn_w_down')


def kernel(x, c, w_ada, b_ada, w_in, lru_conv_w, lru_conv_b, lru_w_r, lru_b_r, lru_w_i, lru_b_i, lru_lambda, conv_w, conv_b, conv_norm_g, conv_norm_b, w_out, ln1_g, ln1_b, ffn_w_up, ffn_conv_w, ffn_conv_b, ffn_w_down, ln2_g, ln2_b, loss_target, m_w_ada, m_b_ada, m_w_in, m_lru_conv_w, m_lru_conv_b, m_lru_w_r, m_lru_b_r, m_lru_w_i, m_lru_b_i, m_lru_lambda, m_conv_w, m_conv_b, m_conv_norm_g, m_conv_norm_b, m_w_out, m_ln1_g, m_ln1_b, m_ffn_w_up, m_ffn_conv_w, m_ffn_conv_b, m_ffn_w_down, m_ln2_g, m_ln2_b, v_w_ada, v_b_ada, v_w_in, v_lru_conv_w, v_lru_conv_b, v_lru_w_r, v_lru_b_r, v_lru_w_i, v_lru_b_i, v_lru_lambda, v_conv_w, v_conv_b, v_conv_norm_g, v_conv_norm_b, v_w_out, v_ln1_g, v_ln1_b, v_ffn_w_up, v_ffn_conv_w, v_ffn_conv_b, v_ffn_w_down, v_ln2_g, v_ln2_b):
    given = dict(locals())
    wt = {n: given[n] for n in _WEIGHTS}
    mo = {n: given["m_" + n] for n in _WEIGHTS}
    vo = {n: given["v_" + n] for n in _WEIGHTS}
    bl, s_len, d = x.shape
    wd = d // 2
    tokens = bl * s_len
    xi, yi, ci = _pos()
    kme = 2 * xi + yi
    kidx = jnp.reshape(kme, (1,)).astype(jnp.int32)
    cidx = jnp.reshape(ci, (1,)).astype(jnp.int32)

    nk = w_ada.shape[2]
    c8 = jnp.pad(c, ((0, 8 - bl), (0, 0)))
    c_all, mod8, win, wout_s, lcw_s, cw_s, fcw_s = _ada_fwd(
        c8, w_ada[0], lax.dynamic_slice(b_ada, (0, kme * nk), (1, nk)),
        [_mx(w_in[0]), _mx(w_out[0]), lru_conv_w[0], conv_w[0], ffn_conv_w[0]])
    mod3 = mod8[:bl].reshape(bl, 1, 6 * d)
    wout = wout_s.reshape(d, d)
    f = 4 * ffn_w_down.shape[1]
    unshard = lambda t: jnp.transpose(t, (1, 0, 2)).reshape(t.shape[1], -1)
    lcw, cw, fcw = unshard(lcw_s), unshard(cw_s), unshard(fcw_s)
    wr_bd, wi_bd = _mx(_block_diag(lru_w_r[0])), _mx(_block_diag(lru_w_i[0]))
    seg = _block_diag(jnp.ones((_N_HEADS, wd // _N_HEADS, wd // _N_HEADS), jnp.bfloat16))
    mixer_small = (lcw, lru_conv_b, wr_bd, wi_bd, lru_b_r, lru_b_i, lru_lambda, cw, conv_b, conv_norm_g, conv_norm_b, seg, wout, ln1_g)

    proj, h, mix, x1, u1, y, vbc, lru, wup, wdn_s = _mix_fwd(x, mod3, win, *mixer_small, ln1_b, [_mx(ffn_w_up[0]), _mx(ffn_w_down[0])])
    wdn = wdn_s.reshape(f, d)
    u2, hh, fact, gc_all, dz2, loss_acc, dln2, dgt2 = _ffn_fwd(x1, mod3, wup, fcw, ffn_conv_b, wdn, ln2_g, ln2_b, loss_target)
    dx1, dy2, dh, dfc, dmod2 = _ffn_bwd(dz2, x1, hh, gc_all, mod3, wup, wdn, fcw, ffn_conv_b)
    loss = lax.psum(0.5 * loss_acc[0, 0] / d, ("x", "y", "c"))

    flat = lambda t: t.reshape(tokens, t.shape[-1])
    fc = wup.shape[2]
    g_up = _wgrad(flat(u2), flat(dh), d, fc, 1, 4, 0, 0, "wgrad_up")
    g_dn, r_up = _wgrad(flat(fact), flat(dy2), fc, d, f // fc, 1, 0, 0, "wgrad_down",
                        exchange=([g_up], _pair_out_shapes([g_up]), _pair_plan, _pair_sems(1)))
    g_dn = g_dn.reshape(4, f // 4, d)
    r_dn, = _pair_exchange([g_dn], "grad_pair_exchange_ffn_w_down")
    ffn_sum = [_pair_add(g, r, cidx, "grad_pair_add_" + n) for g, r, n in zip([g_up, g_dn], [r_up, r_dn], _BIG[2:])]
    grad_x, dproj, dmix, xcg, vecw, dlcw, dcw, dln1, dmod1, *ffn_recv = _mix_bwd(
        dx1, x, mix, proj, h, vbc, lru, mod3, win, *mixer_small, ffn_sum)
    g_ri = _wgrad(flat(xcg), flat(xcg), wd, wd, 1, 2, 0, 1, "wgrad_gates")
    dh_ = wd // _N_HEADS
    g_ri = jnp.stack([jnp.stack([g_ri[i, hd * dh_:(hd + 1) * dh_, hd * dh_:(hd + 1) * dh_] for hd in range(_N_HEADS)])
                      for i in range(2)])

    dmod = jnp.concatenate([dmod1.reshape(bl, 3 * d), dmod2.reshape(bl, 2 * d), dgt2.reshape(bl, d)], axis=1)
    pieces = [vecw, dlcw, dcw, jnp.concatenate([dln1, dln2], axis=0), dfc, g_ri, jnp.pad(dmod, ((0, 8 - bl), (0, 0)))]
    shapes = [p.shape for p in pieces]
    pack = _pack(pieces)
    g_out = _wgrad(flat(y), flat(dmix), d, d, 1, 1, 0, 0, "wgrad_out").reshape(4, d // 4, d)
    g_in, every = _wgrad(flat(u1), flat(dproj), d, wd, 1, 4, 0, 0, "wgrad_in", exchange=(
        [pack], [jax.ShapeDtypeStruct((8,) + pack.shape, _F32)], _dev_gather_plan, _dev_gather_sems(1)))
    mix_sum, mix_wire = zip(*[_pair_add(g, r, cidx, "grad_pair_add_" + n, jnp.bfloat16) for g, r, n in zip(
        [g_in, g_out], _pair_exchange([g_in, g_out], "grad_pair_exchange_w_in"), _BIG[:2])])
    chip_sum, recv = list(mix_sum) + ffn_sum, list(_chip_exchange(mix_wire)) + list(ffn_recv)
    half = [_chip_add(s, r, kidx, "grad_chip_add_" + n) for s, r, n in zip(chip_sum, recv, _BIG)]
    grads, deltas, new_m, new_v = {}, {}, {}, {}
    for n, mine, theirs in zip(_BIG, half, _pair_swap(half)):
        g, dl, mm, vv = _adamw_big(wt[n][0], mine, theirs, mo[n][0], vo[n][0], cidx, "adamw_" + n)
        grads[n], deltas[n], new_m[n], new_v[n] = g[None], dl[None], mm[None], vv[None]

    total = _small_sum(every)
    vecw, dlcw, dcw, dln, dfc, g_ri, dmod_sum = _unpack(total, shapes)
    dmod_all = _unpack(every, shapes)[-1].reshape(64, 6 * d)

    g_ada, dl, mm, vv = _ada_bwd(c_all, lax.dynamic_slice(dmod_all, (0, kme * nk), (64, nk)), w_ada[0], m_w_ada[0], v_w_ada[0])
    grads['w_ada'], deltas['w_ada'], new_m['w_ada'], new_v['w_ada'] = g_ada[None], dl[None], mm[None], vv[None]

    shard = lambda t, width: lax.dynamic_slice(t, (0, kme * width), (t.shape[0], width))
    small = {
        'b_ada': dmod_sum, 'lru_conv_w': shard(dlcw, wd // 4), 'lru_conv_b': vecw[0:1], 'lru_w_r': g_ri[0], 'lru_b_r': vecw[1:2],
        'lru_w_i': g_ri[1], 'lru_b_i': vecw[2:3], 'lru_lambda': vecw[3:4], 'conv_w': shard(dcw, wd // 4), 'conv_b': vecw[4:5],
        'conv_norm_g': vecw[5:6], 'conv_norm_b': vecw[6:7], 'ln1_g': dln[0:1], 'ln1_b': dln[1:2],
        'ffn_conv_w': shard(dfc[0:3], f // 4), 'ffn_conv_b': dfc[3:4], 'ln2_g': dln[2:3], 'ln2_b': dln[3:4]}
    names = list(small)
    gs = [small[n] if n == 'b_ada' else small[n].reshape(wt[n].shape) for n in names]
    gs, dls, mms, vvs = _adamw_small([wt[n] for n in names], gs, [mo[n] for n in names], [vo[n] for n in names])
    for n, g, dl, mm, vv in zip(names, gs, dls, mms, vvs):
        grads[n], deltas[n], new_m[n], new_v[n] = g, dl, mm, vv

    return (loss, grad_x, *[grads[n] for n in _WEIGHTS], *[deltas[n] for n in _WEIGHTS],
            *[new_m[n] for n in _WEIGHTS], *[new_v[n] for n in _WEIGHTS])
```

```python
import functools
import itertools
import math

import jax
import jax.numpy as jnp
from jax import lax
from jax.experimental import pallas as pl
from jax.experimental.pallas import tpu as pltpu

_MXU_DT = jnp.bfloat16
_F32 = jnp.float32
_VMEM_LIMIT = 56 * 1024 * 1024
_TT_MIX = 256
_TT_FFN = 256
_TK_WGRAD = 2048
_HALO = 32

_LRU_C = 8.0
_LN_EPS = 1e-5
_N_HEADS = 8
_DEPTH = 1
_ALPHA = (2 * _DEPTH) ** 0.25
_ADAM_LR, _ADAM_B1, _ADAM_B2, _ADAM_EPS, _ADAM_WD, _ADAM_STEP = 0.001, 0.9, 0.999, 1e-08, 0.01, 10

_MESH = pl.DeviceIdType.MESH
_CHIP_DELTAS = ((1, 0), (0, 1), (1, 1))


def _cparams(sem):
    return pltpu.CompilerParams(dimension_semantics=sem, vmem_limit_bytes=_VMEM_LIMIT)


def _resident(shape):
    nd = len(shape)
    return pl.BlockSpec(shape, lambda *_: (0,) * nd, pipeline_mode=pl.Buffered(1))


def _dot(a, b):
    return jnp.dot(a, b, preferred_element_type=_F32)


def _dot_nt(a, b):
    return lax.dot_general(a, b, (((1,), (1,)), ((), ())), preferred_element_type=_F32)


def _dot_tn(a, b):
    return lax.dot_general(a, b, (((0,), (0,)), ((), ())), preferred_element_type=_F32)


def _mx(v):
    return v.astype(_MXU_DT)


def _expm1(v):
    series = v * (1.0 + v * (1.0 / 2 + v * (1.0 / 6 + v * (1.0 / 24 + v * (1.0 / 120)))))
    return jnp.where(jnp.abs(v) < 0.0625, series, jnp.exp(v) - 1.0)


def _softplus(z):
    e = jnp.exp(-jnp.abs(z))
    u = 1.0 + e
    log1p = jnp.where(u == 1.0, e, jnp.log(u) * e / jnp.where(u == 1.0, 1.0, u - 1.0))
    return jnp.maximum(z, 0.0) + log1p


_GELU_C = math.sqrt(2.0 / math.pi)


def _gelu_and_grad(v):
    t = jnp.tanh(_GELU_C * (v + 0.044715 * v * v * v))
    val = 0.5 * v * (1.0 + t)
    grad = 0.5 * (1.0 + t) + 0.5 * v * (1.0 - t * t) * _GELU_C * (1.0 + 3 * 0.044715 * v * v)
    return val, grad


def _seg_sum(v, seg, passes=3):
    hi = v.astype(jnp.bfloat16)
    r1 = v - hi.astype(_F32)
    mid = r1.astype(jnp.bfloat16)
    out = _dot(hi, seg) + _dot(mid, seg)
    if passes == 3:
        out = out + _dot((r1 - mid.astype(_F32)).astype(jnp.bfloat16), seg)
    return out


def _scan_fwd(a, u, h0):
    n = a.shape[0]
    row = lax.broadcasted_iota(jnp.int32, a.shape, 0)
    h, d = u, 1
    while d < n:
        keep = row >= d
        h = a * jnp.where(keep, pltpu.roll(h, d, 0), 0.0) + h
        a = a * jnp.where(keep, pltpu.roll(a, d, 0), 1.0)
        d *= 2
    return h + a * h0


def _scan_rev(c, g, g_end):
    n = c.shape[0]
    row = lax.broadcasted_iota(jnp.int32, c.shape, 0)
    d = 1
    while d < n:
        keep = row < n - d
        g = c * jnp.where(keep, pltpu.roll(g, n - d, 0), 0.0) + g
        c = c * jnp.where(keep, pltpu.roll(c, n - d, 0), 1.0)
        d *= 2
    return g + c * g_end


def _layer_norm_stats(z):
    mu = jnp.mean(z, axis=-1, keepdims=True)
    zc = z - mu
    var = jnp.mean(zc * zc, axis=-1, keepdims=True)
    rstd = lax.rsqrt(var + _LN_EPS)
    return zc * rstd, rstd


def _layer_norm_bwd(dn, n, rstd):
    return rstd * (dn - jnp.mean(dn, axis=-1, keepdims=True) - n * jnp.mean(dn * n, axis=-1, keepdims=True))


def _rowsum(v):
    return jnp.sum(v, axis=0, keepdims=True)


def _fused_exchange(body, n_in, n_out, n_scratch, n_xin, n_xout, plan, grid):
    def wrapped(*refs):
        o0 = n_in + n_xin
        s0 = o0 + n_out + n_xout
        start, finish = plan(refs[n_in:o0], refs[o0 + n_out:s0], *refs[s0 + n_scratch:])
        step = pl.program_id(0) * grid[1] + pl.program_id(1)

        @pl.when(step == 0)
        def _():
            start()

        body(*refs[:n_in], *refs[o0:o0 + n_out], *refs[s0:s0 + n_scratch])

        @pl.when(step == grid[0] * grid[1] - 1)
        def _():
            finish()

    return wrapped


def _lru_gates(xc, wr_ref, wi_ref, br_ref, bi_ref, lam_ref):
    xcb = _mx(xc)
    r = jax.nn.sigmoid(_dot(xcb, wr_ref[...]) + br_ref[...])
    i = jax.nn.sigmoid(_dot(xcb, wi_ref[...]) + bi_ref[...])
    sp = _softplus(-lam_ref[...])
    log_a = -_LRU_C * r * sp
    a = jnp.exp(log_a)
    mult = jnp.sqrt(-_expm1(2.0 * log_a))
    return r, i, sp, a, mult


def _conv_taps(ext_ref, w_ref, first, n_taps, tt):
    acc = w_ref[0:1, :] * ext_ref[pl.ds(first, tt), :]
    for k in range(1, n_taps):
        acc = acc + w_ref[k:k + 1, :] * ext_ref[pl.ds(first + k, tt), :]
    return acc


def _make_shifted(ext_ref, sh_ref):
    n = sh_ref.shape[1]
    for r in range(1, 8):
        sh_ref[r - 1] = ext_ref[pl.ds(r, n), :]


def _tap(ext_ref, sh_ref, off, tt):
    base = (off // 8) * 8
    if off % 8 == 0:
        return ext_ref[pl.ds(base, tt), :]
    return sh_ref[off % 8 - 1, pl.ds(base, tt), :]


def _conv_taps_shifted(ext_ref, sh_ref, w_ref, first, n_taps, tt):
    acc = w_ref[0:1, :] * _tap(ext_ref, sh_ref, first, tt)
    for k in range(1, n_taps):
        acc = acc + w_ref[k:k + 1, :] * _tap(ext_ref, sh_ref, first + k, tt)
    return acc


def _mix_fwd(x, mod3, win, lcw, lcb, wr_bd, wi_bd, b_r, b_i, lam, cw, cb, ng, nb, seg, wout, ln1g, ln1b, shards):
    bl, s_len, d = x.shape
    w = d // 2
    tt = min(_TT_MIX, s_len)
    ns = s_len // tt
    kc = cw.shape[0]

    def body(x_ref, mod_ref, win_ref, lcw_ref, lcb_ref, wr_ref, wi_ref, br_ref, bi_ref, lam_ref, cw_ref, cb_ref,
             ng_ref, nb_ref, seg_ref, wout_ref, g1_ref, b1_ref,
             proj_ref, h_ref, mix_ref, x1_ref, u1_ref, y_ref, vbc_ref, lru_ref, ext4, ext31, sh31, hcar):
        @pl.when(pl.program_id(1) == 0)
        def _():
            ext4[0:8, :] = jnp.zeros((8, w), _F32)
            ext31[0:_HALO, :] = jnp.zeros((_HALO, w), _F32)
            hcar[...] = jnp.zeros_like(hcar)

        xt = x_ref[...]
        sh1, sc1, gt1 = mod_ref[:, 0:d], mod_ref[:, d:2 * d], mod_ref[:, 2 * d:3 * d]
        u1 = _mx(xt * (1.0 + sc1) + sh1)
        u1_ref[...] = u1
        xa, ga, vb, gb = (_dot(u1, win_ref[k]) for k in range(4))
        proj_ref[:, 0:w] = xa
        proj_ref[:, w:2 * w] = ga
        proj_ref[:, 2 * w:3 * w] = vb
        proj_ref[:, 3 * w:4 * w] = gb

        ext4[8:8 + tt, :] = xa
        xc = lcb_ref[...] + _conv_taps(ext4, lcw_ref, 5, 4, tt)
        ext4[0:8, :] = xa[tt - 8:tt, :]
        r, i, sp, a, mult = _lru_gates(xc, wr_ref, wi_ref, br_ref, bi_ref, lam_ref)
        for k, val in enumerate((xc, r, i, a, mult)):
            lru_ref[:, k * w:(k + 1) * w] = val
        h = _scan_fwd(a, mult * (i * xc), hcar[0:1, :])
        hcar[0:1, :] = h[tt - 1:tt, :]
        h_ref[...] = h
        gelu, _ = _gelu_and_grad(ga)
        y_ref[:, 0:w] = _mx(gelu * h)

        vbg = vb * jax.nn.sigmoid(gb)
        ext31[_HALO:_HALO + tt, :] = vbg
        _make_shifted(ext31, sh31)
        vbc = cb_ref[...] + _conv_taps_shifted(ext31, sh31, cw_ref, _HALO - (kc - 1), kc, tt)
        vbc_ref[...] = vbc
        ext31[0:_HALO, :] = vbg[tt - _HALO:tt, :]
        inv = 1.0 / (w // _N_HEADS)
        zc = vbc - _seg_sum(vbc, seg_ref[...]) * inv
        n = zc * lax.rsqrt(_seg_sum(zc * zc, seg_ref[...]) * inv + _LN_EPS)
        pre = n * ng_ref[...] + nb_ref[...]
        y_ref[:, w:2 * w] = _mx(pre * jax.nn.sigmoid(pre))

        mix = _dot(y_ref[...], wout_ref[...])
        mix_ref[...] = mix
        n1, _ = _layer_norm_stats(_ALPHA * xt + (1.0 + gt1) * mix)
        x1_ref[...] = n1 * g1_ref[...] + b1_ref[...]

    tok = lambda c: pl.BlockSpec((None, tt, c), lambda b, s: (b, s, 0))
    smalls = [lcw, lcb, wr_bd, wi_bd, b_r, b_i, lam, cw, cb, ng, nb, seg, wout, ln1g, ln1b]
    nx = len(shards)
    return pl.pallas_call(
        _fused_exchange(body, 3 + len(smalls), 8, 4, nx, nx, _gather_plan, (bl, ns)), grid=(bl, ns),
        in_specs=[tok(d), pl.BlockSpec((None, 1, 6 * d), lambda b, s: (b, 0, 0)), _resident(win.shape)]
        + [_resident(t.shape) for t in smalls] + [_HBM] * nx,
        out_specs=[tok(4 * w), tok(w), tok(d), tok(d), tok(d), tok(d), tok(w), tok(5 * w)] + [_HBM] * nx,
        out_shape=[jax.ShapeDtypeStruct((bl, s_len, 4 * w), _F32), jax.ShapeDtypeStruct((bl, s_len, w), _F32),
                   jax.ShapeDtypeStruct((bl, s_len, d), _F32), jax.ShapeDtypeStruct((bl, s_len, d), _F32),
                   jax.ShapeDtypeStruct((bl, s_len, d), _MXU_DT), jax.ShapeDtypeStruct((bl, s_len, d), _MXU_DT),
                   jax.ShapeDtypeStruct((bl, s_len, w), _F32), jax.ShapeDtypeStruct((bl, s_len, 5 * w), _F32)]
        + [jax.ShapeDtypeStruct((4,) + t.shape, t.dtype) for t in shards],
        scratch_shapes=[pltpu.VMEM((tt + 8, w), _F32), pltpu.VMEM((tt + _HALO, w), _F32),
                        pltpu.VMEM((7, tt + _HALO - 8, w), _F32), pltpu.VMEM((8, w), _F32)] + _gather_sems(nx),
        compiler_params=_cparams(("arbitrary", "arbitrary")), name="mix_fwd",
    )(x, mod3, win, *smalls, *shards)


def _ffn_fwd(x1, mod3, wup, fcw, fcb, wdn, ln2g, ln2b, target):
    bl, s_len, d = x1.shape
    nch, _, fc = wup.shape
    nch //= 2
    f = nch * fc
    tt = min(_TT_FFN, s_len)
    ns = s_len // tt

    def body(x1_ref, mod_ref, wup_ref, fcw_ref, fcb_ref, wdn_ref, g2_ref, b2_ref, tgt_ref,
             u2_ref, hh_ref, f_ref, gc_ref, dz2_ref, loss_ref, dln2_ref, dgt2_ref, ext3):
        first_tile = pl.program_id(1) == 0

        @pl.when(first_tile)
        def _():
            ext3[:, 0:8, :] = jnp.zeros((nch, 8, fc), _F32)
            dgt2_ref[...] = jnp.zeros_like(dgt2_ref)

        @pl.when(first_tile & (pl.program_id(0) == 0))
        def _():
            loss_ref[...] = jnp.zeros_like(loss_ref)
            dln2_ref[...] = jnp.zeros_like(dln2_ref)

        x1t = x1_ref[...]
        sh2, sc2, gt2 = mod_ref[:, 3 * d:4 * d], mod_ref[:, 4 * d:5 * d], mod_ref[:, 5 * d:6 * d]
        u2 = _mx(x1t * (1.0 + sc2) + sh2)
        u2_ref[...] = u2
        y2 = jnp.zeros((tt, d), _F32)
        for j in range(nch):
            lanes = slice(j * fc, (j + 1) * fc)
            v = _dot(u2, wup_ref[j])
            g = _dot(u2, wup_ref[nch + j])
            hh_ref[:, lanes] = v.astype(hh_ref.dtype)
            hh_ref[:, f + j * fc:f + (j + 1) * fc] = g.astype(hh_ref.dtype)
            ext = ext3.at[j]
            ext[8:8 + tt, :] = g
            gc = fcb_ref[:, lanes] + sum(fcw_ref[k:k + 1, lanes] * ext[pl.ds(6 + k, tt), :] for k in range(3))
            gc_ref[:, lanes] = gc
            ext[0:8, :] = g[tt - 8:tt, :]
            fj = _mx(gc * jax.nn.sigmoid(gc) * v)
            f_ref[:, lanes] = fj
            y2 = y2 + _dot(fj, wdn_ref[lanes, :])

        n2, rstd = _layer_norm_stats(_ALPHA * x1t + (1.0 + gt2) * y2)
        err = n2 * g2_ref[...] + b2_ref[...] - tgt_ref[...]
        loss_ref[...] += jnp.sum(_rowsum(err * err), axis=1, keepdims=True)
        dout = err * (1.0 / d)
        dln2_ref[0:1, :] += _rowsum(dout * n2)
        dln2_ref[1:2, :] += _rowsum(dout)
        dz2 = _layer_norm_bwd(dout * g2_ref[...], n2, rstd)
        dz2_ref[...] = dz2
        dgt2_ref[...] += _rowsum(dz2 * y2)

    tok = lambda c: pl.BlockSpec((None, tt, c), lambda b, s: (b, s, 0))
    acc = lambda r: pl.BlockSpec((r, d), lambda b, s: (0, 0))
    smalls = [fcw, fcb, wdn, ln2g, ln2b]
    return pl.pallas_call(
        body, grid=(bl, ns),
        in_specs=[tok(d), pl.BlockSpec((None, 1, 6 * d), lambda b, s: (b, 0, 0)), _resident(wup.shape)]
        + [_resident(t.shape) for t in smalls] + [tok(d)],
        out_specs=[tok(d), tok(2 * f), tok(f), tok(f), tok(d), acc(1), acc(2), pl.BlockSpec((None, 1, d), lambda b, s: (b, 0, 0))],
        out_shape=[jax.ShapeDtypeStruct((bl, s_len, d), _MXU_DT), jax.ShapeDtypeStruct((bl, s_len, 2 * f), _F32),
                   jax.ShapeDtypeStruct((bl, s_len, f), _MXU_DT), jax.ShapeDtypeStruct((bl, s_len, f), _F32),
                   jax.ShapeDtypeStruct((bl, s_len, d), _F32), jax.ShapeDtypeStruct((1, d), _F32), jax.ShapeDtypeStruct((2, d), _F32),
                   jax.ShapeDtypeStruct((bl, 1, d), _F32)],
        scratch_shapes=[pltpu.VMEM((nch, tt + 8, fc), _F32)],
        compiler_params=_cparams(("arbitrary", "arbitrary")), name="ffn_fwd",
    )(x1, mod3, wup, *smalls, target)


def _ffn_bwd(dz2, x1, hh, gc_all, mod3, wup, wdn, fcw, fcb):
    bl, s_len, d = x1.shape
    nch, _, fc = wup.shape
    nch //= 2
    f = nch * fc
    tt = min(_TT_FFN, s_len)
    ns = s_len // tt

    def body(dz2_ref, x1_ref, hh_ref, gc_ref, mod_ref, wup_ref, wdn_ref, fcw_ref, fcb_ref,
             dx1_ref, dy2_ref, dh_ref, dfc_ref, dmod_ref, dext, dcar):
        @pl.when(pl.program_id(1) == 0)
        def _():
            dcar[...] = jnp.zeros_like(dcar)
            dmod_ref[...] = jnp.zeros_like(dmod_ref)

        @pl.when((pl.program_id(1) == 0) & (pl.program_id(0) == 0))
        def _():
            dfc_ref[...] = jnp.zeros_like(dfc_ref)

        sc2, gt2 = mod_ref[:, 4 * d:5 * d], mod_ref[:, 5 * d:6 * d]
        dz2t = dz2_ref[...]
        dy2 = _mx((1.0 + gt2) * dz2t)
        dy2_ref[...] = dy2
        du2 = jnp.zeros((tt, d), _F32)
        for j in range(nch):
            lanes = slice(j * fc, (j + 1) * fc)
            glanes = slice(f + j * fc, f + (j + 1) * fc)
            v = hh_ref[:, lanes].astype(_F32)
            g = hh_ref[:, glanes].astype(_F32)
            gc = gc_ref[:, lanes]
            sg = jax.nn.sigmoid(gc)
            df = _dot_nt(dy2, wdn_ref[lanes, :])
            dv = df * (gc * sg)
            dgc = df * v * (sg * (1.0 + gc * (1.0 - sg)))
            dfc_ref[3:4, lanes] += _rowsum(dgc)
            dext[0:tt, :] = dgc
            dext[tt:tt + 8, :] = dcar[j]
            dcar[j] = dgc[0:8, :]
            dg = jnp.zeros((tt, fc), _F32)
            for k in range(3):
                shifted = dext[pl.ds(2 - k, tt), :]
                dg = dg + fcw_ref[k:k + 1, lanes] * shifted
                dfc_ref[k:k + 1, lanes] += _rowsum(shifted * g)
            dvb, dgb = _mx(dv), _mx(dg)
            dh_ref[:, lanes] = dvb
            dh_ref[:, glanes] = dgb
            du2 = du2 + _dot_nt(dvb, wup_ref[j]) + _dot_nt(dgb, wup_ref[nch + j])

        dx1_ref[...] = _ALPHA * dz2t + du2 * (1.0 + sc2)
        dmod_ref[0:1, :] += _rowsum(du2)
        dmod_ref[1:2, :] += _rowsum(du2 * x1_ref[...])

    tok = lambda c: pl.BlockSpec((None, tt, c), lambda b, i: (b, ns - 1 - i, 0))
    return pl.pallas_call(
        body, grid=(bl, ns),
        in_specs=[tok(d), tok(d), tok(2 * f), tok(f), pl.BlockSpec((None, 1, 6 * d), lambda b, i: (b, 0, 0)),
                  _resident(wup.shape), _resident(wdn.shape), _resident(fcw.shape), _resident(fcb.shape)],
        out_specs=[tok(d), tok(d), tok(2 * f), pl.BlockSpec((4, f), lambda b, i: (0, 0)),
                   pl.BlockSpec((None, 2, d), lambda b, i: (b, 0, 0))],
        out_shape=[jax.ShapeDtypeStruct((bl, s_len, d), _F32), jax.ShapeDtypeStruct((bl, s_len, d), _MXU_DT),
                   jax.ShapeDtypeStruct((bl, s_len, 2 * f), _MXU_DT), jax.ShapeDtypeStruct((4, f), _F32),
                   jax.ShapeDtypeStruct((bl, 2, d), _F32)],
        scratch_shapes=[pltpu.VMEM((tt + 8, fc), _F32), pltpu.VMEM((nch, 8, fc), _F32)],
        compiler_params=_cparams(("arbitrary", "arbitrary")), name="ffn_bwd",
    )(dz2, x1, hh, gc_all, mod3, wup, wdn, fcw, fcb)


def _mix_bwd(dx1, x, mix, proj, h, vbc, lru, mod3, win, lcw, lcb, wr_bd, wi_bd, b_r, b_i, lam, cw, cb, ng, nb, seg, wout, ln1g, chip_sums):
    bl, s_len, d = x.shape
    w = d // 2
    tt = min(_TT_MIX, s_len)
    ns = s_len // tt
    kc = cw.shape[0]

    def body(dx1_ref, x_ref, mix_ref, proj_ref, phalo_ref, h_ref, hhalo_ref, vbc_ref, lru_ref, mod_ref, win_ref, lcw_ref, lcb_ref,
             wr_ref, wi_ref, br_ref, bi_ref, lam_ref, cw_ref, cb_ref, ng_ref, nb_ref, seg_ref, wout_ref, g1_ref,
             gx_ref, dproj_ref, dmix_ref, xcg_ref, vecw_ref, dlcw_ref, dcw_ref, dln1_ref, dmod_ref,
             ext4, ext31, dext4, dext31, sh31, dsh31, car4, car31, gcar):
        s = ns - 1 - pl.program_id(1)
        first = s == 0

        @pl.when(pl.program_id(1) == 0)
        def _():
            car4[...] = jnp.zeros_like(car4)
            car31[...] = jnp.zeros_like(car31)
            gcar[...] = jnp.zeros_like(gcar)
            dmod_ref[...] = jnp.zeros_like(dmod_ref)

        @pl.when((pl.program_id(1) == 0) & (pl.program_id(0) == 0))
        def _():
            for ref in (vecw_ref, dlcw_ref, dcw_ref, dln1_ref):
                ref[...] = jnp.zeros_like(ref)

        xt, mixt = x_ref[...], mix_ref[...]
        sh1, sc1, gt1 = mod_ref[:, 0:d], mod_ref[:, d:2 * d], mod_ref[:, 2 * d:3 * d]

        n1, rstd1 = _layer_norm_stats(_ALPHA * xt + (1.0 + gt1) * mixt)
        dx1t = dx1_ref[...]
        dln1_ref[0:1, :] += _rowsum(dx1t * n1)
        dln1_ref[1:2, :] += _rowsum(dx1t)
        dz1 = _layer_norm_bwd(dx1t * g1_ref[...], n1, rstd1)
        dmod_ref[2:3, :] += _rowsum(dz1 * mixt)
        dmix = _mx((1.0 + gt1) * dz1)
        dmix_ref[...] = dmix
        dya = _dot_nt(dmix, wout_ref[0:w, :])
        dyb = _dot_nt(dmix, wout_ref[w:2 * w, :])

        xa, ga = proj_ref[:, 0:w], proj_ref[:, w:2 * w]
        vb, gb = proj_ref[:, 2 * w:3 * w], proj_ref[:, 3 * w:4 * w]

        sgb = jax.nn.sigmoid(gb)
        vbg = vb * sgb
        hv, hg = phalo_ref[:, 2 * w:3 * w], phalo_ref[:, 3 * w:4 * w]
        ext31[0:_HALO, :] = jnp.where(first, 0.0, hv * jax.nn.sigmoid(hg))
        ext31[_HALO:_HALO + tt, :] = vbg
        _make_shifted(ext31, sh31)
        vbc = vbc_ref[...]
        inv = 1.0 / (w // _N_HEADS)
        zc = vbc - _seg_sum(vbc, seg_ref[...]) * inv
        rstd = lax.rsqrt(_seg_sum(zc * zc, seg_ref[...]) * inv + _LN_EPS)
        n = zc * rstd
        pre = n * ng_ref[...] + nb_ref[...]
        sgp = jax.nn.sigmoid(pre)
        dpre = dyb * (sgp * (1.0 + pre * (1.0 - sgp)))
        vecw_ref[5:6, :] += _rowsum(dpre * n)
        vecw_ref[6:7, :] += _rowsum(dpre)
        dn = dpre * ng_ref[...]
        dvbc = rstd * (dn - _seg_sum(dn, seg_ref[...], 2) * inv - n * (_seg_sum(dn * n, seg_ref[...], 2) * inv))
        vecw_ref[4:5, :] += _rowsum(dvbc)
        dext31[0:tt, :] = dvbc
        dext31[tt:tt + _HALO, :] = car31[...]
        car31[...] = dvbc[0:_HALO, :]
        _make_shifted(dext31, dsh31)
        dvbg = jnp.zeros((tt, w), _F32)
        for k in range(kc):
            dvbg = dvbg + cw_ref[k:k + 1, :] * _tap(dext31, dsh31, kc - 1 - k, tt)
            dcw_ref[k:k + 1, :] += _rowsum(dvbc * _tap(ext31, sh31, _HALO - (kc - 1) + k, tt))
        dproj_ref[:, 2 * w:3 * w] = _mx(dvbg * sgb)
        dproj_ref[:, 3 * w:4 * w] = _mx(dvbg * vb * (sgb * (1.0 - sgb)))

        ext4[0:8, :] = jnp.where(first, 0.0, phalo_ref[_HALO - 8:_HALO, 0:w])
        ext4[8:8 + tt, :] = xa
        xc, r, i, a, mult = (lru_ref[:, k * w:(k + 1) * w] for k in range(5))
        xcg_ref[:, 0:w] = _mx(xc)
        sp = _softplus(-lam_ref[...])
        ht = h_ref[...]
        row = lax.broadcasted_iota(jnp.int32, (tt, w), 0)
        h_before = jnp.where(first, 0.0, hhalo_ref[7:8, :])
        hprev = jnp.where(row == 0, h_before, pltpu.roll(ht, 1, 0))
        gelu, dgelu = _gelu_and_grad(ga)
        dproj_ref[:, w:2 * w] = _mx(dya * ht * dgelu)
        dh = dya * gelu
        coef = jnp.where(row == tt - 1, 1.0, pltpu.roll(a, tt - 1, 0))
        big_g = _scan_rev(coef, dh, gcar[0:1, :])
        gcar[0:1, :] = a[0:1, :] * big_g[0:1, :]
        da = big_g * hprev
        ixc = i * xc
        dlog_a = da * a - (big_g * ixc) * (a * a / mult)
        di = big_g * mult * xc
        dxc = big_g * mult * i
        vecw_ref[3:4, :] += _rowsum(dlog_a * r) * (_LRU_C * jax.nn.sigmoid(-lam_ref[...]))
        dgr_f = dlog_a * (-_LRU_C * sp) * (r * (1.0 - r))
        dgi_f = di * (i * (1.0 - i))
        vecw_ref[1:2, :] += _rowsum(dgr_f)
        vecw_ref[2:3, :] += _rowsum(dgi_f)
        dgr, dgi = _mx(dgr_f), _mx(dgi_f)
        xcg_ref[:, w:2 * w] = dgr
        xcg_ref[:, 2 * w:3 * w] = dgi
        dxc = dxc + _dot_nt(dgr, wr_ref[...]) + _dot_nt(dgi, wi_ref[...])
        vecw_ref[0:1, :] += _rowsum(dxc)
        dext4[0:tt, :] = dxc
        dext4[tt:tt + 8, :] = car4[...]
        car4[...] = dxc[0:8, :]
        dxa = jnp.zeros((tt, w), _F32)
        for k in range(4):
            dxa = dxa + lcw_ref[k:k + 1, :] * dext4[pl.ds(3 - k, tt), :]
            dlcw_ref[k:k + 1, :] += _rowsum(dxc * ext4[pl.ds(5 + k, tt), :])
        dproj_ref[:, 0:w] = _mx(dxa)

        du1 = sum(_dot_nt(dproj_ref[:, k * w:(k + 1) * w], win_ref[k]) for k in range(4))
        gx_ref[...] = _ALPHA * dz1 + du1 * (1.0 + sc1)
        dmod_ref[0:1, :] += _rowsum(du1)
        dmod_ref[1:2, :] += _rowsum(du1 * xt)

    tok = lambda c: pl.BlockSpec((None, tt, c), lambda b, i: (b, ns - 1 - i, 0))
    halo = lambda rows, c: pl.BlockSpec(
        (None, rows, c), lambda b, i: (b, jnp.maximum((ns - 1 - i) * (tt // rows) - 1, 0), 0))
    accw = lambda r, c: pl.BlockSpec((r, c), lambda b, i: (0, 0))
    smalls = [lcw, lcb, wr_bd, wi_bd, b_r, b_i, lam, cw, cb, ng, nb, seg, wout, ln1g]
    nx = len(chip_sums)
    return pl.pallas_call(
        _fused_exchange(body, 11 + len(smalls), 9, 9, nx, nx, _chip_reduce_plan, (bl, ns)), grid=(bl, ns),
        in_specs=[tok(d), tok(d), tok(d), tok(4 * w), halo(_HALO, 4 * w), tok(w), halo(8, w), tok(w), tok(5 * w),
                  pl.BlockSpec((None, 1, 6 * d), lambda b, i: (b, 0, 0)), _resident(win.shape)]
        + [_resident(t.shape) for t in smalls] + [_HBM] * nx,
        out_specs=[tok(d), tok(4 * w), tok(d), tok(3 * w), accw(8, w), accw(4, w), accw(kc, w), accw(2, d),
                   pl.BlockSpec((None, 3, d), lambda b, i: (b, 0, 0))] + [_HBM] * nx,
        out_shape=[jax.ShapeDtypeStruct((bl, s_len, d), _F32), jax.ShapeDtypeStruct((bl, s_len, 4 * w), _MXU_DT),
                   jax.ShapeDtypeStruct((bl, s_len, d), _MXU_DT), jax.ShapeDtypeStruct((bl, s_len, 3 * w), _MXU_DT),
                   jax.ShapeDtypeStruct((8, w), _F32), jax.ShapeDtypeStruct((4, w), _F32),
                   jax.ShapeDtypeStruct((kc, w), _F32), jax.ShapeDtypeStruct((2, d), _F32),
                   jax.ShapeDtypeStruct((bl, 3, d), _F32)]
        + [jax.ShapeDtypeStruct((3,) + t.shape[1:], t.dtype) for t in chip_sums],
        scratch_shapes=[pltpu.VMEM((tt + 8, w), _F32), pltpu.VMEM((tt + _HALO, w), _F32),
                        pltpu.VMEM((tt + 8, w), _F32), pltpu.VMEM((tt + _HALO, w), _F32),
                        pltpu.VMEM((7, tt + _HALO - 8, w), _F32), pltpu.VMEM((7, tt + _HALO - 8, w), _F32),
                        pltpu.VMEM((8, w), _F32), pltpu.VMEM((_HALO, w), _F32), pltpu.VMEM((8, w), _F32)]
        + _chip_reduce_sems(nx),
        compiler_params=_cparams(("arbitrary", "arbitrary")), name="mix_bwd",
    )(dx1, x, mix, proj, proj, h, h, vbc, lru, mod3, win, *smalls, *chip_sums)


def _wgrad(a, b, ma, nbw, na, nb, a_off, b_off, name, exchange=None):
    t = a.shape[0]
    tk = min(_TK_WGRAD, t)
    grid = (na * nb, t // tk)

    def body(a_ref, b_ref, o_ref):
        @pl.when(pl.program_id(1) == 0)
        def _():
            o_ref[...] = jnp.zeros_like(o_ref)
        o_ref[...] += _dot_tn(a_ref[...], b_ref[...])

    xin, xshapes, plan, sems = exchange if exchange else ([], [], None, [])
    nx = len(xin)
    res = pl.pallas_call(
        _fused_exchange(body, 2, 1, 0, nx, len(xshapes), plan, grid) if exchange else body, grid=grid,
        in_specs=[pl.BlockSpec((tk, ma), lambda j, k: (k, j // nb + a_off)),
                  pl.BlockSpec((tk, nbw), lambda j, k: (k, j % nb + b_off))] + [_HBM] * nx,
        out_specs=[pl.BlockSpec((None, ma, nbw), lambda j, k: (j, 0, 0))] + [_HBM] * len(xshapes),
        out_shape=[jax.ShapeDtypeStruct((na * nb, ma, nbw), _F32)] + list(xshapes),
        scratch_shapes=list(sems),
        compiler_params=_cparams(("arbitrary", "arbitrary")), name=name,
    )(a, b, *xin)
    return res if exchange else res[0]


_DEV_DELTAS = tuple(dl for dl in itertools.product((0, 1), repeat=3) if any(dl))
_HBM = pl.BlockSpec(memory_space=pltpu.HBM)
_VMEM = pl.BlockSpec(memory_space=pltpu.VMEM)


def _pos():
    return lax.axis_index("x"), lax.axis_index("y"), lax.axis_index("c")


def _flip(v, delta):
    return 1 - v if delta else v


def _remote(src, dst, ssem, rsem, dev):
    return pltpu.make_async_remote_copy(src_ref=src, dst_ref=dst, send_sem=ssem, recv_sem=rsem,
                                        device_id=dev, device_id_type=_MESH)


def _rows(ref, idx, n):
    return ref.at[pl.ds(pl.multiple_of(idx * n, 8), n)]


def _ada_fwd(c8, w_ada_k, b_ada_k, shards):
    rows, d = c8.shape
    nk = w_ada_k.shape[1]
    n = len(shards)

    def body(*refs):
        c_ref, w_ref, b_ref = refs[:3]
        call_ref, mod_ref = refs[3 + n:5 + n]
        modloc, modrcv, s1, r1, s2, r2 = refs[5 + 2 * n:11 + 2 * n]
        gather_start, gather_finish = _gather_plan(refs[3:3 + n], refs[5 + n:5 + 2 * n], *refs[11 + 2 * n:14 + 2 * n],
                                                   fsem=refs[14 + 2 * n], frsem=refs[15 + 2 * n], bounce=refs[16 + 2 * n:])
        gather_start()
        xi, yi, ci = _pos()
        me, kme = 4 * xi + 2 * yi + ci, 2 * xi + yi
        call_ref[pl.ds(pl.multiple_of(me * rows, 8), rows), :] = c_ref[...]
        sends = []
        for p, (dx, dy, dc) in enumerate(_DEV_DELTAS):
            cp = _remote(c_ref, _rows(call_ref, me, rows), s1.at[p], r1.at[p], (_flip(xi, dx), _flip(yi, dy), _flip(ci, dc)))
            cp.start()
            sends.append(cp)
        for p, (dx, dy, dc) in enumerate(_DEV_DELTAS):
            src = 4 * _flip(xi, dx) + 2 * _flip(yi, dy) + _flip(ci, dc)
            _remote(c_ref, _rows(call_ref, src, rows), s1.at[p], r1.at[p], (xi, yi, ci)).wait_recv()
        for cp in sends:
            cp.wait_send()

        ca = call_ref[...]
        modloc[...] = _dot(_mx(ca * jax.nn.sigmoid(ca)), _mx(w_ref[...])) + b_ref[...]
        modrcv[kme] = modloc[pl.ds(pl.multiple_of(me * rows, 8), rows), :]
        sends = []
        for j, (dx, dy) in enumerate(_CHIP_DELTAS):
            tx, ty = _flip(xi, dx), _flip(yi, dy)
            cp = _remote(_rows(modloc, 4 * tx + 2 * ty + ci, rows), modrcv.at[kme], s2.at[j], r2.at[j], (tx, ty, ci))
            cp.start()
            sends.append(cp)
        for j, (dx, dy) in enumerate(_CHIP_DELTAS):
            ksrc = 2 * _flip(xi, dx) + _flip(yi, dy)
            _remote(_rows(modloc, me, rows), modrcv.at[ksrc], s2.at[j], r2.at[j], (xi, yi, ci)).wait_recv()
        for cp in sends:
            cp.wait_send()
        for j in range(4):
            mod_ref[:, j * nk:(j + 1) * nk] = modrcv[j]
        gather_finish()

    return pl.pallas_call(
        body, in_specs=[_VMEM, _VMEM, _VMEM] + [_HBM] * n, out_specs=[_VMEM, _VMEM] + [_HBM] * n,
        out_shape=[jax.ShapeDtypeStruct((8 * rows, d), _F32), jax.ShapeDtypeStruct((rows, 4 * nk), _F32)]
        + [jax.ShapeDtypeStruct((4,) + a.shape, a.dtype) for a in shards],
        scratch_shapes=[pltpu.VMEM((8 * rows, nk), _F32), pltpu.VMEM((4, rows, nk), _F32),
                        pltpu.SemaphoreType.DMA((7,)), pltpu.SemaphoreType.DMA((7,)),
                        pltpu.SemaphoreType.DMA((3,)), pltpu.SemaphoreType.DMA((3,))]
        + _gather_sems(n) + [pltpu.SemaphoreType.DMA((3, n)), pltpu.SemaphoreType.DMA((3, n))]
        + [pltpu.VMEM(a.shape, a.dtype) for a in shards],
        compiler_params=pltpu.CompilerParams(vmem_limit_bytes=_VMEM_LIMIT), name="ada_fwd",
    )(c8, w_ada_k, b_ada_k, *shards)


def _gather_sems(n):
    return [pltpu.SemaphoreType.DMA((3, n)), pltpu.SemaphoreType.DMA((3, n)), pltpu.SemaphoreType.DMA((n,))]


def _gather_plan(ins, outs, ssem, rsem, lsem, bounce=(), fsem=None, frsem=None):
    n = len(ins)
    xi, yi, ci = _pos()
    kme = 2 * xi + yi
    split = [fsem is not None and ins[a].shape[0] % 32 == 0 for a in range(n)]

    def half(ref, a, which):
        r2 = ins[a].shape[0] // 2
        return ref.at[pl.ds(pl.multiple_of(which * r2, 16), r2)]

    staged = [pltpu.make_async_copy(ins[a], bounce[a], lsem.at[a]) for a in range(len(bounce))]
    local = [pltpu.make_async_copy(bounce[a] if bounce else ins[a], outs[a].at[kme], lsem.at[a]) for a in range(n)]
    sends, recvs, forwards, handed = [], [], [], []
    for j, (dx, dy) in enumerate(_CHIP_DELTAS):
        tx, ty = _flip(xi, dx), _flip(yi, dy)
        for a in range(n):
            sems = (ssem.at[j, a], rsem.at[j, a])
            landing = outs[a].at[2 * tx + ty]
            if split[a]:
                sends.append(_remote(half(ins[a], a, ci), half(outs[a].at[kme], a, ci), *sems, (tx, ty, ci)))
                recvs.append(_remote(half(ins[a], a, ci), half(landing, a, ci), *sems, (xi, yi, ci)))
                fsems = (fsem.at[j, a], frsem.at[j, a])
                forwards.append(_remote(half(landing, a, ci), half(landing, a, ci), *fsems, (xi, yi, 1 - ci)))
                handed.append(_remote(half(ins[a], a, 1 - ci), half(landing, a, 1 - ci), *fsems, (xi, yi, ci)))
            else:
                sends.append(_remote(ins[a], outs[a].at[kme], *sems, (tx, ty, ci)))
                recvs.append(_remote(ins[a], landing, *sems, (xi, yi, ci)))
                forwards.append(None)

    def start():
        for cp in sends + staged:
            cp.start()
        for cp in staged:
            cp.wait()
        for cp in local:
            cp.start()

    def finish():
        for arrived, forward in zip(recvs, forwards):
            arrived.wait_recv()
            if forward is not None:
                forward.start()
        for cp in handed:
            cp.wait_recv()
        for cp in sends + [f for f in forwards if f is not None]:
            cp.wait_send()
        for cp in local:
            cp.wait()

    return start, finish


def _dev_gather_sems(n):
    return [pltpu.SemaphoreType.DMA((7, n)), pltpu.SemaphoreType.DMA((7, n)), pltpu.SemaphoreType.DMA((n,))]


def _dev_gather_plan(ins, outs, ssem, rsem, lsem):
    n = len(ins)
    xi, yi, ci = _pos()
    me = 4 * xi + 2 * yi + ci
    local = [pltpu.make_async_copy(ins[a], outs[a].at[me], lsem.at[a]) for a in range(n)]
    sends, recvs = [], []
    for p, (dx, dy, dc) in enumerate(_DEV_DELTAS):
        tx, ty, tc = _flip(xi, dx), _flip(yi, dy), _flip(ci, dc)
        for a in range(n):
            sends.append(_remote(ins[a], outs[a].at[me], ssem.at[p, a], rsem.at[p, a], (tx, ty, tc)))
            recvs.append(_remote(ins[a], outs[a].at[4 * tx + 2 * ty + tc], ssem.at[p, a], rsem.at[p, a], (xi, yi, ci)))

    def start():
        for cp in local + sends:
            cp.start()

    def finish():
        for cp in recvs:
            cp.wait_recv()
        for cp in sends:
            cp.wait_send()
        for cp in local:
            cp.wait()

    return start, finish


def _pair_sems(n):
    return [pltpu.SemaphoreType.DMA((n,)), pltpu.SemaphoreType.DMA((n,))]


def _pair_plan(ins, outs, ssem, rsem):
    xi, yi, ci = _pos()
    sends = []
    for a in range(len(ins)):
        r2 = ins[a].shape[1] // 2
        src = ins[a].at[:, pl.ds(pl.multiple_of((1 - ci) * r2, 8), r2), :]
        sends.append(_remote(src, outs[a], ssem.at[a], rsem.at[a], (xi, yi, 1 - ci)))

    def start():
        for cp in sends:
            cp.start()

    def finish():
        for cp in sends:
            cp.wait_recv()
        for cp in sends:
            cp.wait_send()

    return start, finish


def _chip_reduce_sems(n):
    return [pltpu.SemaphoreType.DMA((3, n)), pltpu.SemaphoreType.DMA((3, n))]


def _chip_reduce_plan(ins, outs, ssem, rsem):
    xi, yi, ci = _pos()
    sends = []
    for j, (dx, dy) in enumerate(_CHIP_DELTAS):
        tx, ty = _flip(xi, dx), _flip(yi, dy)
        sends += [_remote(ins[a].at[2 * tx + ty], outs[a].at[j], ssem.at[j, a], rsem.at[j, a], (tx, ty, ci))
                  for a in range(len(ins))]

    def start():
        for cp in sends:
            cp.start()

    def finish():
        for cp in sends:
            cp.wait_recv()
        for cp in sends:
            cp.wait_send()

    return start, finish


def _pair_exchange(gs, name):
    n = len(gs)

    def body(*refs):
        start, finish = _pair_plan(refs[:n], refs[n:2 * n], *refs[2 * n:])
        start()
        finish()

    return pl.pallas_call(
        body, in_specs=[_HBM] * n, out_specs=[_HBM] * n, out_shape=_pair_out_shapes(gs),
        scratch_shapes=_pair_sems(n), name=name,
    )(*gs)


def _pair_out_shapes(gs):
    return [jax.ShapeDtypeStruct((g.shape[0], g.shape[1] // 2, g.shape[2]), g.dtype) for g in gs]


def _row_tile(r):
    return max(t for t in range(8, min(r, 256) + 1, 8) if r % t == 0)


def _pair_add(g, r, cidx, name, wire_dtype=None):
    nk, r2, c = r.shape
    tr = _row_tile(r2)
    nt = r2 // tr

    def body(c_ref, g_ref, r_ref, *o_refs):
        s = g_ref[...] + r_ref[...]
        for o_ref in o_refs:
            o_ref[...] = s.astype(o_ref.dtype)

    out_spec = pl.BlockSpec((None, tr, c), lambda k, i, cr: (k, i, 0))
    dtypes = [_F32] + ([wire_dtype] if wire_dtype else [])
    res = pl.pallas_call(
        body, grid_spec=pltpu.PrefetchScalarGridSpec(
            num_scalar_prefetch=1, grid=(nk, nt),
            in_specs=[pl.BlockSpec((None, tr, c), lambda k, i, cr: (k, cr[0] * nt + i, 0)), out_spec],
            out_specs=[out_spec] * len(dtypes)),
        out_shape=[jax.ShapeDtypeStruct(r.shape, dt) for dt in dtypes],
        compiler_params=_cparams(("arbitrary", "arbitrary")), name=name,
    )(cidx, g, r)
    return res if wire_dtype else res[0]


def _chip_exchange(ss):
    n = len(ss)

    def body(*refs):
        start, finish = _chip_reduce_plan(refs[:n], refs[n:2 * n], *refs[2 * n:])
        start()
        finish()

    return pl.pallas_call(
        body, in_specs=[_HBM] * n, out_specs=[_HBM] * n,
        out_shape=[jax.ShapeDtypeStruct((3,) + s.shape[1:], s.dtype) for s in ss],
        scratch_shapes=_chip_reduce_sems(n), name="grad_chip_exchange",
    )(*ss)


def _chip_add(s, r, kidx, name):
    _, r2, c = r.shape
    tr = _row_tile(r2)

    def body(k_ref, s_ref, r_ref, o_ref):
        o_ref[...] = ((s_ref[...] + r_ref[0].astype(_F32)) + r_ref[1].astype(_F32)) + r_ref[2].astype(_F32)

    return pl.pallas_call(
        body, grid_spec=pltpu.PrefetchScalarGridSpec(
            num_scalar_prefetch=1, grid=(r2 // tr,),
            in_specs=[pl.BlockSpec((None, tr, c), lambda i, kr: (kr[0], i, 0)),
                      pl.BlockSpec((3, tr, c), lambda i, kr: (0, i, 0))],
            out_specs=pl.BlockSpec((tr, c), lambda i, kr: (i, 0))),
        out_shape=jax.ShapeDtypeStruct((r2, c), _F32),
        compiler_params=_cparams(("arbitrary",)), name=name,
    )(kidx, s, r)


def _pair_swap(hs):
    n = len(hs)

    def body(*refs):
        ins, outs = refs[:n], refs[n:2 * n]
        ssem, rsem = refs[2 * n:]
        xi, yi, ci = _pos()
        sends = [_remote(ins[a], outs[a], ssem.at[a], rsem.at[a], (xi, yi, 1 - ci)) for a in range(n)]
        for cp in sends:
            cp.start()
        for cp in sends:
            cp.wait_recv()
        for cp in sends:
            cp.wait_send()

    return pl.pallas_call(
        body, in_specs=[_HBM] * n, out_specs=[_HBM] * n,
        out_shape=[jax.ShapeDtypeStruct(h.shape, h.dtype) for h in hs],
        scratch_shapes=[pltpu.SemaphoreType.DMA((n,)), pltpu.SemaphoreType.DMA((n,))], name="grad_pair_swap",
    )(*hs)


def _small_sum(every):
    def body(all_ref, sum_ref):
        tot = all_ref[0]
        for dev in range(1, 8):
            tot = tot + all_ref[dev]
        sum_ref[...] = tot

    return pl.pallas_call(
        body, in_specs=[_VMEM], out_specs=_VMEM, out_shape=jax.ShapeDtypeStruct(every.shape[1:], _F32),
        compiler_params=pltpu.CompilerParams(vmem_limit_bytes=_VMEM_LIMIT), name="small_sum",
    )(every)


def _adamw(w, g, m, v):
    m = _ADAM_B1 * m + (1.0 - _ADAM_B1) * g
    v = _ADAM_B2 * v + (1.0 - _ADAM_B2) * (g * g)
    m_hat = m / (1.0 - _ADAM_B1 ** _ADAM_STEP)
    v_hat = v / (1.0 - _ADAM_B2 ** _ADAM_STEP)
    return -_ADAM_LR * (m_hat / (jnp.sqrt(v_hat) + _ADAM_EPS) + _ADAM_WD * w), m, v


def _adamw_big(w, g_mine, g_theirs, m, v, cidx, name):
    r, c = w.shape
    tr = _row_tile(r // 2)
    nt = r // 2 // tr

    def body(c_ref, w_ref, gm_ref, gt_ref, m_ref, v_ref, g_ref, d_ref, mo_ref, vo_ref):
        g = jnp.where(pl.program_id(0) // nt == c_ref[0], gm_ref[...], gt_ref[...])
        g_ref[...] = g
        d_ref[...], mo_ref[...], vo_ref[...] = _adamw(w_ref[...], g, m_ref[...], v_ref[...])

    spec = pl.BlockSpec((tr, c), lambda i, cr: (i, 0))
    half = pl.BlockSpec((tr, c), lambda i, cr: (i % nt, 0))
    return pl.pallas_call(
        body, grid_spec=pltpu.PrefetchScalarGridSpec(
            num_scalar_prefetch=1, grid=(2 * nt,), in_specs=[spec, half, half, spec, spec], out_specs=[spec] * 4),
        out_shape=[jax.ShapeDtypeStruct((r, c), _F32)] * 4,
        compiler_params=_cparams(("arbitrary",)), name=name,
    )(cidx, w, g_mine, g_theirs, m, v)


def _adamw_small(ws, gs, ms, vs):
    n = len(ws)
    summed = [i for i in range(n) if gs[i].shape != ws[i].shape]

    def body(*refs):
        w_r, g_r, m_r, v_r = (refs[i * n:(i + 1) * n] for i in range(4))
        outs = refs[4 * n:]
        for i in range(n):
            g = g_r[i][...]
            if i in summed:
                g = _rowsum(g)
                outs[3 * n + summed.index(i)][...] = g
            outs[i][...], outs[n + i][...], outs[2 * n + i][...] = _adamw(w_r[i][...], g, m_r[i][...], v_r[i][...])

    shapes = [jax.ShapeDtypeStruct(w.shape, _F32) for w in ws]
    res = pl.pallas_call(
        body, in_specs=[_VMEM] * (4 * n), out_specs=[_VMEM] * (3 * n + len(summed)),
        out_shape=shapes * 3 + [shapes[i] for i in summed],
        compiler_params=pltpu.CompilerParams(vmem_limit_bytes=_VMEM_LIMIT), name="adamw_small",
    )(*ws, *gs, *ms, *vs)
    gs = list(gs)
    for pos, i in enumerate(summed):
        gs[i] = res[3 * n + pos]
    return gs, res[:n], res[n:2 * n], res[2 * n:3 * n]


def _ada_bwd(c_all, dmod_k, w, m, v):
    d, nk = w.shape
    tn = 512 if nk % 512 == 0 else nk

    def body(c_ref, dm_ref, w_ref, m_ref, v_ref, g_ref, d_ref, mo_ref, vo_ref):
        ca = c_ref[...]
        g = _dot_tn(_mx(ca * jax.nn.sigmoid(ca)), _mx(dm_ref[...]))
        g_ref[...] = g
        d_ref[...], mo_ref[...], vo_ref[...] = _adamw(w_ref[...], g, m_ref[...], v_ref[...])

    col = pl.BlockSpec((d, tn), lambda j: (0, j))
    return pl.pallas_call(
        body, grid=(nk // tn,),
        in_specs=[pl.BlockSpec(c_all.shape, lambda j: (0, 0)), pl.BlockSpec((c_all.shape[0], tn), lambda j: (0, j)), col, col, col],
        out_specs=[col] * 4, out_shape=[jax.ShapeDtypeStruct((d, nk), _F32)] * 4,
        compiler_params=_cparams(("arbitrary",)), name="ada_bwd",
    )(c_all, dmod_k, w, m, v)


def _block_diag(wh):
    hn, dh, _ = wh.shape
    eye = jnp.eye(hn, dtype=wh.dtype)
    return (eye[:, None, :, None] * wh[:, :, None, :]).reshape(hn * dh, hn * dh)


def _pack(pieces):
    out = []
    for p in pieces:
        flat = p.reshape(-1, 128)
        out.append(jnp.pad(flat, ((0, (-flat.shape[0]) % 8), (0, 0))))
    return jnp.concatenate(out, axis=0)


def _unpack(pack, shapes):
    out, off = [], 0
    for shp in shapes:
        rows = math.prod(shp) // 128
        out.append(pack[..., off:off + rows, :].reshape(pack.shape[:-2] + tuple(shp)))
        off += rows + (-rows) % 8
    return out


_WEIGHTS = ('w_ada', 'b_ada', 'w_in', 'lru_conv_w', 'lru_conv_b', 'lru_w_r', 'lru_b_r', 'lru_w_i', 'lru_b_i', 'lru_lambda',
            'conv_w', 'conv_b', 'conv_norm_g', 'conv_norm_b', 'w_out', 'ln1_g', 'ln1_b', 'ffn_w_up', 'ffn_conv_w',
            'ffn_conv_b', 'ffn_w_down', 'ln2_g', 'ln2_b')
_BIG = ('w_in', 'w_out', 'ffn_w_up', 'ffn_w_down')


def kernel(x, c, w_ada, b_ada, w_in, lru_conv_w, lru_conv_b, lru_w_r, lru_b_r, lru_w_i, lru_b_i, lru_lambda, conv_w, conv_b, conv_norm_g, conv_norm_b, w_out, ln1_g, ln1_b, ffn_w_up, ffn_conv_w, ffn_conv_b, ffn_w_down, ln2_g, ln2_b, loss_target, m_w_ada, m_b_ada, m_w_in, m_lru_conv_w, m_lru_conv_b, m_lru_w_r, m_lru_b_r, m_lru_w_i, m_lru_b_i, m_lru_lambda, m_conv_w, m_conv_b, m_conv_norm_g, m_conv_norm_b, m_w_out, m_ln1_g, m_ln1_b, m_ffn_w_up, m_ffn_conv_w, m_ffn_conv_b, m_ffn_w_down, m_ln2_g, m_ln2_b, v_w_ada, v_b_ada, v_w_in, v_lru_conv_w, v_lru_conv_b, v_lru_w_r, v_lru_b_r, v_lru_w_i, v_lru_b_i, v_lru_lambda, v_conv_w, v_conv_b, v_conv_norm_g, v_conv_norm_b, v_w_out, v_ln1_g, v_ln1_b, v_ffn_w_up, v_ffn_conv_w, v_ffn_conv_b, v_ffn_w_down, v_ln2_g, v_ln2_b):
    given = dict(locals())
    wt = {n: given[n] for n in _WEIGHTS}
    mo = {n: given["m_" + n] for n in _WEIGHTS}
    vo = {n: given["v_" + n] for n in _WEIGHTS}
    bl, s_len, d = x.shape
    wd = d // 2
    tokens = bl * s_len
    xi, yi, ci = _pos()
    kme = 2 * xi + yi
    kidx = jnp.reshape(kme, (1,)).astype(jnp.int32)
    cidx = jnp.reshape(ci, (1,)).astype(jnp.int32)

    nk = w_ada.shape[2]
    c8 = jnp.pad(c, ((0, 8 - bl), (0, 0)))
    c_all, mod8, win, wout_s, lcw_s, cw_s, fcw_s = _ada_fwd(
        c8, w_ada[0], lax.dynamic_slice(b_ada, (0, kme * nk), (1, nk)),
        [_mx(w_in[0]), _mx(w_out[0]), lru_conv_w[0], conv_w[0], ffn_conv_w[0]])
    mod3 = mod8[:bl].reshape(bl, 1, 6 * d)
    wout = wout_s.reshape(d, d)
    f = 4 * ffn_w_down.shape[1]
    unshard = lambda t: jnp.transpose(t, (1, 0, 2)).reshape(t.shape[1], -1)
    lcw, cw, fcw = unshard(lcw_s), unshard(cw_s), unshard(fcw_s)
    wr_bd, wi_bd = _mx(_block_diag(lru_w_r[0])), _mx(_block_diag(lru_w_i[0]))
    seg = _block_diag(jnp.ones((_N_HEADS, wd // _N_HEADS, wd // _N_HEADS), jnp.bfloat16))
    mixer_small = (lcw, lru_conv_b, wr_bd, wi_bd, lru_b_r, lru_b_i, lru_lambda, cw, conv_b, conv_norm_g, conv_norm_b, seg, wout, ln1_g)

    proj, h, mix, x1, u1, y, vbc, lru, wup, wdn_s = _mix_fwd(x, mod3, win, *mixer_small, ln1_b, [_mx(ffn_w_up[0]), _mx(ffn_w_down[0])])
    wdn = wdn_s.reshape(f, d)
    u2, hh, fact, gc_all, dz2, loss_acc, dln2, dgt2 = _ffn_fwd(x1, mod3, wup, fcw, ffn_conv_b, wdn, ln2_g, ln2_b, loss_target)
    dx1, dy2, dh, dfc, dmod2 = _ffn_bwd(dz2, x1, hh, gc_all, mod3, wup, wdn, fcw, ffn_conv_b)

    flat = lambda t: t.reshape(tokens, t.shape[-1])
    fc = wup.shape[2]
    g_up = _wgrad(flat(u2), flat(dh), d, fc, 1, 4, 0, 0, "wgrad_up")
    g_dn, r_up = _wgrad(flat(fact), flat(dy2), fc, d, f // fc, 1, 0, 0, "wgrad_down",
                        exchange=([g_up], _pair_out_shapes([g_up]), _pair_plan, _pair_sems(1)))
    g_dn = g_dn.reshape(4, f // 4, d)
    r_dn, = _pair_exchange([g_dn], "grad_pair_exchange_ffn_w_down")
    ffn_sum = [_pair_add(g, r, cidx, "grad_pair_add_" + n) for g, r, n in zip([g_up, g_dn], [r_up, r_dn], _BIG[2:])]
    grad_x, dproj, dmix, xcg, vecw, dlcw, dcw, dln1, dmod1, *ffn_recv = _mix_bwd(
        dx1, x, mix, proj, h, vbc, lru, mod3, win, *mixer_small, ffn_sum)
    g_ri = _wgrad(flat(xcg), flat(xcg), wd, wd, 1, 2, 0, 1, "wgrad_gates")
    dh_ = wd // _N_HEADS
    g_ri = jnp.stack([jnp.stack([g_ri[i, hd * dh_:(hd + 1) * dh_, hd * dh_:(hd + 1) * dh_] for hd in range(_N_HEADS)])
                      for i in range(2)])

    dmod = jnp.concatenate([dmod1.reshape(bl, 3 * d), dmod2.reshape(bl, 2 * d), dgt2.reshape(bl, d)], axis=1)
    pieces = [vecw, dlcw, dcw, jnp.concatenate([dln1, dln2], axis=0), dfc, g_ri, loss_acc[:, 0:128],
              jnp.pad(dmod, ((0, 8 - bl), (0, 0)))]
    shapes = [p.shape for p in pieces]
    pack = _pack(pieces)
    g_out = _wgrad(flat(y), flat(dmix), d, d, 1, 1, 0, 0, "wgrad_out").reshape(4, d // 4, d)
    g_in, every = _wgrad(flat(u1), flat(dproj), d, wd, 1, 4, 0, 0, "wgrad_in", exchange=(
        [pack], [jax.ShapeDtypeStruct((8,) + pack.shape, _F32)], _dev_gather_plan, _dev_gather_sems(1)))
    mix_sum, mix_wire = zip(*[_pair_add(g, r, cidx, "grad_pair_add_" + n, jnp.bfloat16) for g, r, n in zip(
        [g_in, g_out], _pair_exchange([g_in, g_out], "grad_pair_exchange_w_in"), _BIG[:2])])
    chip_sum, recv = list(mix_sum) + ffn_sum, list(_chip_exchange(mix_wire)) + list(ffn_recv)
    half = [_chip_add(s, r, kidx, "grad_chip_add_" + n) for s, r, n in zip(chip_sum, recv, _BIG)]
    grads, deltas, new_m, new_v = {}, {}, {}, {}
    for n, mine, theirs in zip(_BIG, half, _pair_swap(half)):
        g, dl, mm, vv = _adamw_big(wt[n][0], mine, theirs, mo[n][0], vo[n][0], cidx, "adamw_" + n)
        grads[n], deltas[n], new_m[n], new_v[n] = g[None], dl[None], mm[None], vv[None]

    total = _small_sum(every)
    vecw, dlcw, dcw, dln, dfc, g_ri, loss_sum, dmod_sum = _unpack(total, shapes)
    loss = 0.5 * loss_sum[0, 0] / d
    dmod_all = _unpack(every, shapes)[-1].reshape(64, 6 * d)

    g_ada, dl, mm, vv = _ada_bwd(c_all, lax.dynamic_slice(dmod_all, (0, kme * nk), (64, nk)), w_ada[0], m_w_ada[0], v_w_ada[0])
    grads['w_ada'], deltas['w_ada'], new_m['w_ada'], new_v['w_ada'] = g_ada[None], dl[None], mm[None], vv[None]

    shard = lambda t, width: lax.dynamic_slice(t, (0, kme * width), (t.shape[0], width))
    small = {
        'b_ada': dmod_sum, 'lru_conv_w': shard(dlcw, wd // 4), 'lru_conv_b': vecw[0:1], 'lru_w_r': g_ri[0], 'lru_b_r': vecw[1:2],
        'lru_w_i': g_ri[1], 'lru_b_i': vecw[2:3], 'lru_lambda': vecw[3:4], 'conv_w': shard(dcw, wd // 4), 'conv_b': vecw[4:5],
        'conv_norm_g': vecw[5:6], 'conv_norm_b': vecw[6:7], 'ln1_g': dln[0:1], 'ln1_b': dln[1:2],
        'ffn_conv_w': shard(dfc[0:3], f // 4), 'ffn_conv_b': dfc[3:4], 'ln2_g': dln[2:3], 'ln2_b': dln[3:4]}
    names = list(small)
    gs = [small[n] if n == 'b_ada' else small[n].reshape(wt[n].shape) for n in names]
    gs, dls, mms, vvs = _adamw_small([wt[n] for n in names], gs, [mo[n] for n in names], [vo[n] for n in names])
    for n, g, dl, mm, vv in zip(names, gs, dls, mms, vvs):
        grads[n], deltas[n], new_m[n], new_v[n] = g, dl, mm, vv

    return (loss, grad_x, *[grads[n] for n in _WEIGHTS], *[deltas[n] for n in _WEIGHTS],
            *[new_m[n] for n in _WEIGHTS], *[new_v[n] for n in _WEIGHTS])
```

```python
import functools
import itertools
import math

import jax
import jax.numpy as jnp
from jax import lax
from jax.experimental import pallas as pl
from jax.experimental.pallas import tpu as pltpu

_MXU_DT = jnp.bfloat16
_F32 = jnp.float32
_VMEM_LIMIT = 56 * 1024 * 1024
_TT_MIX = 256
_TT_FFN = 256
_TK_WGRAD = 2048
_HALO = 32

_LRU_C = 8.0
_LN_EPS = 1e-5
_N_HEADS = 8
_DEPTH = 1
_ALPHA = (2 * _DEPTH) ** 0.25
_ADAM_LR, _ADAM_B1, _ADAM_B2, _ADAM_EPS, _ADAM_WD, _ADAM_STEP = 0.001, 0.9, 0.999, 1e-08, 0.01, 10

_MESH = pl.DeviceIdType.MESH
_CHIP_DELTAS = ((1, 0), (0, 1), (1, 1))


def _cparams(sem):
    return pltpu.CompilerParams(dimension_semantics=sem, vmem_limit_bytes=_VMEM_LIMIT)


def _resident(shape):
    nd = len(shape)
    return pl.BlockSpec(shape, lambda *_: (0,) * nd, pipeline_mode=pl.Buffered(1))


def _dot(a, b):
    return jnp.dot(a, b, preferred_element_type=_F32)


def _dot_nt(a, b):
    return lax.dot_general(a, b, (((1,), (1,)), ((), ())), preferred_element_type=_F32)


def _dot_tn(a, b):
    return lax.dot_general(a, b, (((0,), (0,)), ((), ())), preferred_element_type=_F32)


def _mx(v):
    return v.astype(_MXU_DT)


def _expm1(v):
    series = v * (1.0 + v * (1.0 / 2 + v * (1.0 / 6 + v * (1.0 / 24 + v * (1.0 / 120)))))
    return jnp.where(jnp.abs(v) < 0.0625, series, jnp.exp(v) - 1.0)


def _softplus(z):
    e = jnp.exp(-jnp.abs(z))
    u = 1.0 + e
    log1p = jnp.where(u == 1.0, e, jnp.log(u) * e / jnp.where(u == 1.0, 1.0, u - 1.0))
    return jnp.maximum(z, 0.0) + log1p


_GELU_C = math.sqrt(2.0 / math.pi)


def _gelu_and_grad(v):
    t = jnp.tanh(_GELU_C * (v + 0.044715 * v * v * v))
    val = 0.5 * v * (1.0 + t)
    grad = 0.5 * (1.0 + t) + 0.5 * v * (1.0 - t * t) * _GELU_C * (1.0 + 3 * 0.044715 * v * v)
    return val, grad


def _seg_sum(v, seg, passes=3):
    hi = v.astype(jnp.bfloat16)
    r1 = v - hi.astype(_F32)
    mid = r1.astype(jnp.bfloat16)
    out = _dot(hi, seg) + _dot(mid, seg)
    if passes == 3:
        out = out + _dot((r1 - mid.astype(_F32)).astype(jnp.bfloat16), seg)
    return out


def _scan_fwd(a, u, h0):
    n = a.shape[0]
    row = lax.broadcasted_iota(jnp.int32, a.shape, 0)
    h, d = u, 1
    while d < n:
        keep = row >= d
        h = a * jnp.where(keep, pltpu.roll(h, d, 0), 0.0) + h
        a = a * jnp.where(keep, pltpu.roll(a, d, 0), 1.0)
        d *= 2
    return h + a * h0


def _scan_rev(c, g, g_end):
    n = c.shape[0]
    row = lax.broadcasted_iota(jnp.int32, c.shape, 0)
    d = 1
    while d < n:
        keep = row < n - d
        g = c * jnp.where(keep, pltpu.roll(g, n - d, 0), 0.0) + g
        c = c * jnp.where(keep, pltpu.roll(c, n - d, 0), 1.0)
        d *= 2
    return g + c * g_end


def _layer_norm_stats(z):
    mu = jnp.mean(z, axis=-1, keepdims=True)
    zc = z - mu
    var = jnp.mean(zc * zc, axis=-1, keepdims=True)
    rstd = lax.rsqrt(var + _LN_EPS)
    return zc * rstd, rstd


def _layer_norm_bwd(dn, n, rstd):
    return rstd * (dn - jnp.mean(dn, axis=-1, keepdims=True) - n * jnp.mean(dn * n, axis=-1, keepdims=True))


def _rowsum(v):
    return jnp.sum(v, axis=0, keepdims=True)


def _fused_exchange(body, n_in, n_out, n_scratch, n_xin, n_xout, plan, grid):
    def wrapped(*refs):
        o0 = n_in + n_xin
        s0 = o0 + n_out + n_xout
        start, finish = plan(refs[n_in:o0], refs[o0 + n_out:s0], *refs[s0 + n_scratch:])
        step = 0
        for axis, size in enumerate(grid):
            step = step * size + pl.program_id(axis)

        @pl.when(step == 0)
        def _():
            start()

        body(*refs[:n_in], *refs[o0:o0 + n_out], *refs[s0:s0 + n_scratch])

        @pl.when(step == math.prod(grid) - 1)
        def _():
            finish()

    return wrapped


def _lru_gates(xc, wr_ref, wi_ref, br_ref, bi_ref, lam_ref):
    xcb = _mx(xc)
    r = jax.nn.sigmoid(_dot(xcb, wr_ref[...]) + br_ref[...])
    i = jax.nn.sigmoid(_dot(xcb, wi_ref[...]) + bi_ref[...])
    sp = _softplus(-lam_ref[...])
    log_a = -_LRU_C * r * sp
    a = jnp.exp(log_a)
    mult = jnp.sqrt(-_expm1(2.0 * log_a))
    return r, i, sp, a, mult


def _conv_taps(ext_ref, w_ref, first, n_taps, tt):
    acc = w_ref[0:1, :] * ext_ref[pl.ds(first, tt), :]
    for k in range(1, n_taps):
        acc = acc + w_ref[k:k + 1, :] * ext_ref[pl.ds(first + k, tt), :]
    return acc


def _make_shifted(ext_ref, sh_ref):
    n = sh_ref.shape[1]
    for r in range(1, 8):
        sh_ref[r - 1] = ext_ref[pl.ds(r, n), :]


def _tap(ext_ref, sh_ref, off, tt):
    base = (off // 8) * 8
    if off % 8 == 0:
        return ext_ref[pl.ds(base, tt), :]
    return sh_ref[off % 8 - 1, pl.ds(base, tt), :]


def _conv_taps_shifted(ext_ref, sh_ref, w_ref, first, n_taps, tt):
    acc = w_ref[0:1, :] * _tap(ext_ref, sh_ref, first, tt)
    for k in range(1, n_taps):
        acc = acc + w_ref[k:k + 1, :] * _tap(ext_ref, sh_ref, first + k, tt)
    return acc


def _mix_fwd(x, mod3, win, lcw, lcb, wr_bd, wi_bd, b_r, b_i, lam, cw, cb, ng, nb, seg, wout, ln1g, ln1b, shards):
    bl, s_len, d = x.shape
    w = d // 2
    tt = min(_TT_MIX, s_len)
    ns = s_len // tt
    kc = cw.shape[0]

    def body(x_ref, mod_ref, win_ref, lcw_ref, lcb_ref, wr_ref, wi_ref, br_ref, bi_ref, lam_ref, cw_ref, cb_ref,
             ng_ref, nb_ref, seg_ref, wout_ref, g1_ref, b1_ref,
             proj_ref, h_ref, mix_ref, x1_ref, u1_ref, y_ref, vbc_ref, lru_ref, ext4, ext31, sh31, hcar):
        @pl.when(pl.program_id(1) == 0)
        def _():
            ext4[0:8, :] = jnp.zeros((8, w), _F32)
            ext31[0:_HALO, :] = jnp.zeros((_HALO, w), _F32)
            hcar[...] = jnp.zeros_like(hcar)

        xt = x_ref[...]
        sh1, sc1, gt1 = mod_ref[:, 0:d], mod_ref[:, d:2 * d], mod_ref[:, 2 * d:3 * d]
        u1 = _mx(xt * (1.0 + sc1) + sh1)
        u1_ref[...] = u1
        xa, ga, vb, gb = (_dot(u1, win_ref[k]) for k in range(4))
        proj_ref[:, 0:w] = xa
        proj_ref[:, w:2 * w] = ga
        proj_ref[:, 2 * w:3 * w] = vb
        proj_ref[:, 3 * w:4 * w] = gb

        ext4[8:8 + tt, :] = xa
        xc = lcb_ref[...] + _conv_taps(ext4, lcw_ref, 5, 4, tt)
        ext4[0:8, :] = xa[tt - 8:tt, :]
        r, i, sp, a, mult = _lru_gates(xc, wr_ref, wi_ref, br_ref, bi_ref, lam_ref)
        for k, val in enumerate((xc, r, i, a, mult)):
            lru_ref[:, k * w:(k + 1) * w] = val
        h = _scan_fwd(a, mult * (i * xc), hcar[0:1, :])
        hcar[0:1, :] = h[tt - 1:tt, :]
        h_ref[...] = h
        gelu, _ = _gelu_and_grad(ga)
        y_ref[:, 0:w] = _mx(gelu * h)

        vbg = vb * jax.nn.sigmoid(gb)
        ext31[_HALO:_HALO + tt, :] = vbg
        _make_shifted(ext31, sh31)
        vbc = cb_ref[...] + _conv_taps_shifted(ext31, sh31, cw_ref, _HALO - (kc - 1), kc, tt)
        vbc_ref[...] = vbc
        ext31[0:_HALO, :] = vbg[tt - _HALO:tt, :]
        inv = 1.0 / (w // _N_HEADS)
        zc = vbc - _seg_sum(vbc, seg_ref[...]) * inv
        n = zc * lax.rsqrt(_seg_sum(zc * zc, seg_ref[...]) * inv + _LN_EPS)
        pre = n * ng_ref[...] + nb_ref[...]
        y_ref[:, w:2 * w] = _mx(pre * jax.nn.sigmoid(pre))

        mix = _dot(y_ref[...], wout_ref[...])
        mix_ref[...] = mix
        n1, _ = _layer_norm_stats(_ALPHA * xt + (1.0 + gt1) * mix)
        x1_ref[...] = n1 * g1_ref[...] + b1_ref[...]

    tok = lambda c: pl.BlockSpec((None, tt, c), lambda b, s: (b, s, 0))
    smalls = [lcw, lcb, wr_bd, wi_bd, b_r, b_i, lam, cw, cb, ng, nb, seg, wout, ln1g, ln1b]
    nx = len(shards)
    return pl.pallas_call(
        _fused_exchange(body, 3 + len(smalls), 8, 4, nx, nx, _gather_plan, (bl, ns)), grid=(bl, ns),
        in_specs=[tok(d), pl.BlockSpec((None, 1, 6 * d), lambda b, s: (b, 0, 0)), _resident(win.shape)]
        + [_resident(t.shape) for t in smalls] + [_HBM] * nx,
        out_specs=[tok(4 * w), tok(w), tok(d), tok(d), tok(d), tok(d), tok(w), tok(5 * w)] + [_HBM] * nx,
        out_shape=[jax.ShapeDtypeStruct((bl, s_len, 4 * w), _F32), jax.ShapeDtypeStruct((bl, s_len, w), _F32),
                   jax.ShapeDtypeStruct((bl, s_len, d), _F32), jax.ShapeDtypeStruct((bl, s_len, d), _F32),
                   jax.ShapeDtypeStruct((bl, s_len, d), _MXU_DT), jax.ShapeDtypeStruct((bl, s_len, d), _MXU_DT),
                   jax.ShapeDtypeStruct((bl, s_len, w), _F32), jax.ShapeDtypeStruct((bl, s_len, 5 * w), _F32)]
        + [jax.ShapeDtypeStruct((4,) + t.shape, t.dtype) for t in shards],
        scratch_shapes=[pltpu.VMEM((tt + 8, w), _F32), pltpu.VMEM((tt + _HALO, w), _F32),
                        pltpu.VMEM((7, tt + _HALO - 8, w), _F32), pltpu.VMEM((8, w), _F32)] + _gather_sems(nx),
        compiler_params=_cparams(("arbitrary", "arbitrary")), name="mix_fwd",
    )(x, mod3, win, *smalls, *shards)


def _ffn_fwd(x1, mod3, wup, fcw, fcb, wdn, ln2g, ln2b, target):
    bl, s_len, d = x1.shape
    nch, _, fc = wup.shape
    nch //= 2
    f = nch * fc
    tt = min(_TT_FFN, s_len)
    ns = s_len // tt

    def body(x1_ref, mod_ref, wup_ref, fcw_ref, fcb_ref, wdn_ref, g2_ref, b2_ref, tgt_ref,
             u2_ref, hh_ref, f_ref, gc_ref, dz2_ref, loss_ref, dln2_ref, dgt2_ref, ext3):
        first_tile = pl.program_id(1) == 0

        @pl.when(first_tile)
        def _():
            ext3[:, 0:8, :] = jnp.zeros((nch, 8, fc), _F32)
            dgt2_ref[...] = jnp.zeros_like(dgt2_ref)

        @pl.when(first_tile & (pl.program_id(0) == 0))
        def _():
            loss_ref[...] = jnp.zeros_like(loss_ref)
            dln2_ref[...] = jnp.zeros_like(dln2_ref)

        x1t = x1_ref[...]
        sh2, sc2, gt2 = mod_ref[:, 3 * d:4 * d], mod_ref[:, 4 * d:5 * d], mod_ref[:, 5 * d:6 * d]
        u2 = _mx(x1t * (1.0 + sc2) + sh2)
        u2_ref[...] = u2
        y2 = jnp.zeros((tt, d), _F32)
        for j in range(nch):
            lanes = slice(j * fc, (j + 1) * fc)
            v = _dot(u2, wup_ref[j])
            g = _dot(u2, wup_ref[nch + j])
            hh_ref[:, lanes] = v.astype(hh_ref.dtype)
            hh_ref[:, f + j * fc:f + (j + 1) * fc] = g.astype(hh_ref.dtype)
            ext = ext3.at[j]
            ext[8:8 + tt, :] = g
            gc = fcb_ref[:, lanes] + sum(fcw_ref[k:k + 1, lanes] * ext[pl.ds(6 + k, tt), :] for k in range(3))
            gc_ref[:, lanes] = gc
            ext[0:8, :] = g[tt - 8:tt, :]
            fj = _mx(gc * jax.nn.sigmoid(gc) * v)
            f_ref[:, lanes] = fj
            y2 = y2 + _dot(fj, wdn_ref[lanes, :])

        n2, rstd = _layer_norm_stats(_ALPHA * x1t + (1.0 + gt2) * y2)
        err = n2 * g2_ref[...] + b2_ref[...] - tgt_ref[...]
        loss_ref[...] += jnp.sum(_rowsum(err * err), axis=1, keepdims=True)
        dout = err * (1.0 / d)
        dln2_ref[0:1, :] += _rowsum(dout * n2)
        dln2_ref[1:2, :] += _rowsum(dout)
        dz2 = _layer_norm_bwd(dout * g2_ref[...], n2, rstd)
        dz2_ref[...] = dz2
        dgt2_ref[...] += _rowsum(dz2 * y2)

    tok = lambda c: pl.BlockSpec((None, tt, c), lambda b, s: (b, s, 0))
    acc = lambda r: pl.BlockSpec((r, d), lambda b, s: (0, 0))
    smalls = [fcw, fcb, wdn, ln2g, ln2b]
    return pl.pallas_call(
        body, grid=(bl, ns),
        in_specs=[tok(d), pl.BlockSpec((None, 1, 6 * d), lambda b, s: (b, 0, 0)), _resident(wup.shape)]
        + [_resident(t.shape) for t in smalls] + [tok(d)],
        out_specs=[tok(d), tok(2 * f), tok(f), tok(f), tok(d), acc(1), acc(2), pl.BlockSpec((None, 1, d), lambda b, s: (b, 0, 0))],
        out_shape=[jax.ShapeDtypeStruct((bl, s_len, d), _MXU_DT), jax.ShapeDtypeStruct((bl, s_len, 2 * f), _F32),
                   jax.ShapeDtypeStruct((bl, s_len, f), _MXU_DT), jax.ShapeDtypeStruct((bl, s_len, f), _F32),
                   jax.ShapeDtypeStruct((bl, s_len, d), _F32), jax.ShapeDtypeStruct((1, d), _F32), jax.ShapeDtypeStruct((2, d), _F32),
                   jax.ShapeDtypeStruct((bl, 1, d), _F32)],
        scratch_shapes=[pltpu.VMEM((nch, tt + 8, fc), _F32)],
        compiler_params=_cparams(("arbitrary", "arbitrary")), name="ffn_fwd",
    )(x1, mod3, wup, *smalls, target)


def _ffn_bwd(dz2, x1, hh, gc_all, mod3, wup, wdn, fcw, fcb):
    bl, s_len, d = x1.shape
    nch, _, fc = wup.shape
    nch //= 2
    f = nch * fc
    tt = min(_TT_FFN, s_len)
    ns = s_len // tt

    def body(dz2_ref, x1_ref, hh_ref, gc_ref, mod_ref, wup_ref, wdn_ref, fcw_ref, fcb_ref,
             dx1_ref, dy2_ref, dh_ref, dfc_ref, dmod_ref, dext, dcar):
        @pl.when(pl.program_id(1) == 0)
        def _():
            dcar[...] = jnp.zeros_like(dcar)
            dmod_ref[...] = jnp.zeros_like(dmod_ref)

        @pl.when((pl.program_id(1) == 0) & (pl.program_id(0) == 0))
        def _():
            dfc_ref[...] = jnp.zeros_like(dfc_ref)

        sc2, gt2 = mod_ref[:, 4 * d:5 * d], mod_ref[:, 5 * d:6 * d]
        dz2t = dz2_ref[...]
        dy2 = _mx((1.0 + gt2) * dz2t)
        dy2_ref[...] = dy2
        du2 = jnp.zeros((tt, d), _F32)
        for j in range(nch):
            lanes = slice(j * fc, (j + 1) * fc)
            glanes = slice(f + j * fc, f + (j + 1) * fc)
            v = hh_ref[:, lanes].astype(_F32)
            g = hh_ref[:, glanes].astype(_F32)
            gc = gc_ref[:, lanes]
            sg = jax.nn.sigmoid(gc)
            df = _dot_nt(dy2, wdn_ref[lanes, :])
            dv = df * (gc * sg)
            dgc = df * v * (sg * (1.0 + gc * (1.0 - sg)))
            dfc_ref[3:4, lanes] += _rowsum(dgc)
            dext[0:tt, :] = dgc
            dext[tt:tt + 8, :] = dcar[j]
            dcar[j] = dgc[0:8, :]
            dg = jnp.zeros((tt, fc), _F32)
            for k in range(3):
                shifted = dext[pl.ds(2 - k, tt), :]
                dg = dg + fcw_ref[k:k + 1, lanes] * shifted
                dfc_ref[k:k + 1, lanes] += _rowsum(shifted * g)
            dvb, dgb = _mx(dv), _mx(dg)
            dh_ref[:, lanes] = dvb
            dh_ref[:, glanes] = dgb
            du2 = du2 + _dot_nt(dvb, wup_ref[j]) + _dot_nt(dgb, wup_ref[nch + j])

        dx1_ref[...] = _ALPHA * dz2t + du2 * (1.0 + sc2)
        dmod_ref[0:1, :] += _rowsum(du2)
        dmod_ref[1:2, :] += _rowsum(du2 * x1_ref[...])

    tok = lambda c: pl.BlockSpec((None, tt, c), lambda b, i: (b, ns - 1 - i, 0))
    return pl.pallas_call(
        body, grid=(bl, ns),
        in_specs=[tok(d), tok(d), tok(2 * f), tok(f), pl.BlockSpec((None, 1, 6 * d), lambda b, i: (b, 0, 0)),
                  _resident(wup.shape), _resident(wdn.shape), _resident(fcw.shape), _resident(fcb.shape)],
        out_specs=[tok(d), tok(d), tok(2 * f), pl.BlockSpec((4, f), lambda b, i: (0, 0)),
                   pl.BlockSpec((None, 2, d), lambda b, i: (b, 0, 0))],
        out_shape=[jax.ShapeDtypeStruct((bl, s_len, d), _F32), jax.ShapeDtypeStruct((bl, s_len, d), _MXU_DT),
                   jax.ShapeDtypeStruct((bl, s_len, 2 * f), _MXU_DT), jax.ShapeDtypeStruct((4, f), _F32),
                   jax.ShapeDtypeStruct((bl, 2, d), _F32)],
        scratch_shapes=[pltpu.VMEM((tt + 8, fc), _F32), pltpu.VMEM((nch, 8, fc), _F32)],
        compiler_params=_cparams(("arbitrary", "arbitrary")), name="ffn_bwd",
    )(dz2, x1, hh, gc_all, mod3, wup, wdn, fcw, fcb)


def _mix_bwd(dx1, x, mix, proj, h, vbc, lru, mod3, win, lcw, lcb, wr_bd, wi_bd, b_r, b_i, lam, cw, cb, ng, nb, seg, wout, ln1g, chip_sums):
    bl, s_len, d = x.shape
    w = d // 2
    tt = min(_TT_MIX, s_len)
    ns = s_len // tt
    kc = cw.shape[0]

    def body(dx1_ref, x_ref, mix_ref, proj_ref, phalo_ref, h_ref, hhalo_ref, vbc_ref, lru_ref, mod_ref, win_ref, lcw_ref, lcb_ref,
             wr_ref, wi_ref, br_ref, bi_ref, lam_ref, cw_ref, cb_ref, ng_ref, nb_ref, seg_ref, wout_ref, g1_ref,
             gx_ref, dproj_ref, dmix_ref, xcg_ref, vecw_ref, dlcw_ref, dcw_ref, dln1_ref, dmod_ref,
             ext4, ext31, dext4, dext31, sh31, dsh31, car4, car31, gcar):
        s = ns - 1 - pl.program_id(1)
        first = s == 0

        @pl.when(pl.program_id(1) == 0)
        def _():
            car4[...] = jnp.zeros_like(car4)
            car31[...] = jnp.zeros_like(car31)
            gcar[...] = jnp.zeros_like(gcar)
            dmod_ref[...] = jnp.zeros_like(dmod_ref)

        @pl.when((pl.program_id(1) == 0) & (pl.program_id(0) == 0))
        def _():
            for ref in (vecw_ref, dlcw_ref, dcw_ref, dln1_ref):
                ref[...] = jnp.zeros_like(ref)

        xt, mixt = x_ref[...], mix_ref[...]
        sh1, sc1, gt1 = mod_ref[:, 0:d], mod_ref[:, d:2 * d], mod_ref[:, 2 * d:3 * d]

        n1, rstd1 = _layer_norm_stats(_ALPHA * xt + (1.0 + gt1) * mixt)
        dx1t = dx1_ref[...]
        dln1_ref[0:1, :] += _rowsum(dx1t * n1)
        dln1_ref[1:2, :] += _rowsum(dx1t)
        dz1 = _layer_norm_bwd(dx1t * g1_ref[...], n1, rstd1)
        dmod_ref[2:3, :] += _rowsum(dz1 * mixt)
        dmix = _mx((1.0 + gt1) * dz1)
        dmix_ref[...] = dmix
        dya = _dot_nt(dmix, wout_ref[0:w, :])
        dyb = _dot_nt(dmix, wout_ref[w:2 * w, :])

        xa, ga = proj_ref[:, 0:w], proj_ref[:, w:2 * w]
        vb, gb = proj_ref[:, 2 * w:3 * w], proj_ref[:, 3 * w:4 * w]

        sgb = jax.nn.sigmoid(gb)
        vbg = vb * sgb
        hv, hg = phalo_ref[:, 2 * w:3 * w], phalo_ref[:, 3 * w:4 * w]
        ext31[0:_HALO, :] = jnp.where(first, 0.0, hv * jax.nn.sigmoid(hg))
        ext31[_HALO:_HALO + tt, :] = vbg
        _make_shifted(ext31, sh31)
        vbc = vbc_ref[...]
        inv = 1.0 / (w // _N_HEADS)
        zc = vbc - _seg_sum(vbc, seg_ref[...]) * inv
        rstd = lax.rsqrt(_seg_sum(zc * zc, seg_ref[...]) * inv + _LN_EPS)
        n = zc * rstd
        pre = n * ng_ref[...] + nb_ref[...]
        sgp = jax.nn.sigmoid(pre)
        dpre = dyb * (sgp * (1.0 + pre * (1.0 - sgp)))
        vecw_ref[5:6, :] += _rowsum(dpre * n)
        vecw_ref[6:7, :] += _rowsum(dpre)
        dn = dpre * ng_ref[...]
        dvbc = rstd * (dn - _seg_sum(dn, seg_ref[...], 2) * inv - n * (_seg_sum(dn * n, seg_ref[...], 2) * inv))
        vecw_ref[4:5, :] += _rowsum(dvbc)
        dext31[0:tt, :] = dvbc
        dext31[tt:tt + _HALO, :] = car31[...]
        car31[...] = dvbc[0:_HALO, :]
        _make_shifted(dext31, dsh31)
        dvbg = jnp.zeros((tt, w), _F32)
        for k in range(kc):
            dvbg = dvbg + cw_ref[k:k + 1, :] * _tap(dext31, dsh31, kc - 1 - k, tt)
            dcw_ref[k:k + 1, :] += _rowsum(dvbc * _tap(ext31, sh31, _HALO - (kc - 1) + k, tt))
        dproj_ref[:, 2 * w:3 * w] = _mx(dvbg * sgb)
        dproj_ref[:, 3 * w:4 * w] = _mx(dvbg * vb * (sgb * (1.0 - sgb)))

        ext4[0:8, :] = jnp.where(first, 0.0, phalo_ref[_HALO - 8:_HALO, 0:w])
        ext4[8:8 + tt, :] = xa
        xc, r, i, a, mult = (lru_ref[:, k * w:(k + 1) * w] for k in range(5))
        xcg_ref[:, 0:w] = _mx(xc)
        sp = _softplus(-lam_ref[...])
        ht = h_ref[...]
        row = lax.broadcasted_iota(jnp.int32, (tt, w), 0)
        h_before = jnp.where(first, 0.0, hhalo_ref[7:8, :])
        hprev = jnp.where(row == 0, h_before, pltpu.roll(ht, 1, 0))
        gelu, dgelu = _gelu_and_grad(ga)
        dproj_ref[:, w:2 * w] = _mx(dya * ht * dgelu)
        dh = dya * gelu
        coef = jnp.where(row == tt - 1, 1.0, pltpu.roll(a, tt - 1, 0))
        big_g = _scan_rev(coef, dh, gcar[0:1, :])
        gcar[0:1, :] = a[0:1, :] * big_g[0:1, :]
        da = big_g * hprev
        ixc = i * xc
        dlog_a = da * a - (big_g * ixc) * (a * a / mult)
        di = big_g * mult * xc
        dxc = big_g * mult * i
        vecw_ref[3:4, :] += _rowsum(dlog_a * r) * (_LRU_C * jax.nn.sigmoid(-lam_ref[...]))
        dgr_f = dlog_a * (-_LRU_C * sp) * (r * (1.0 - r))
        dgi_f = di * (i * (1.0 - i))
        vecw_ref[1:2, :] += _rowsum(dgr_f)
        vecw_ref[2:3, :] += _rowsum(dgi_f)
        dgr, dgi = _mx(dgr_f), _mx(dgi_f)
        xcg_ref[:, w:2 * w] = dgr
        xcg_ref[:, 2 * w:3 * w] = dgi
        dxc = dxc + _dot_nt(dgr, wr_ref[...]) + _dot_nt(dgi, wi_ref[...])
        vecw_ref[0:1, :] += _rowsum(dxc)
        dext4[0:tt, :] = dxc
        dext4[tt:tt + 8, :] = car4[...]
        car4[...] = dxc[0:8, :]
        dxa = jnp.zeros((tt, w), _F32)
        for k in range(4):
            dxa = dxa + lcw_ref[k:k + 1, :] * dext4[pl.ds(3 - k, tt), :]
            dlcw_ref[k:k + 1, :] += _rowsum(dxc * ext4[pl.ds(5 + k, tt), :])
        dproj_ref[:, 0:w] = _mx(dxa)

        du1 = sum(_dot_nt(dproj_ref[:, k * w:(k + 1) * w], win_ref[k]) for k in range(4))
        gx_ref[...] = _ALPHA * dz1 + du1 * (1.0 + sc1)
        dmod_ref[0:1, :] += _rowsum(du1)
        dmod_ref[1:2, :] += _rowsum(du1 * xt)

    tok = lambda c: pl.BlockSpec((None, tt, c), lambda b, i: (b, ns - 1 - i, 0))
    halo = lambda rows, c: pl.BlockSpec(
        (None, rows, c), lambda b, i: (b, jnp.maximum((ns - 1 - i) * (tt // rows) - 1, 0), 0))
    accw = lambda r, c: pl.BlockSpec((r, c), lambda b, i: (0, 0))
    smalls = [lcw, lcb, wr_bd, wi_bd, b_r, b_i, lam, cw, cb, ng, nb, seg, wout, ln1g]
    nx = len(chip_sums)
    return pl.pallas_call(
        _fused_exchange(body, 11 + len(smalls), 9, 9, nx, nx, _chip_reduce_plan, (bl, ns)), grid=(bl, ns),
        in_specs=[tok(d), tok(d), tok(d), tok(4 * w), halo(_HALO, 4 * w), tok(w), halo(8, w), tok(w), tok(5 * w),
                  pl.BlockSpec((None, 1, 6 * d), lambda b, i: (b, 0, 0)), _resident(win.shape)]
        + [_resident(t.shape) for t in smalls] + [_HBM] * nx,
        out_specs=[tok(d), tok(4 * w), tok(d), tok(3 * w), accw(8, w), accw(4, w), accw(kc, w), accw(2, d),
                   pl.BlockSpec((None, 3, d), lambda b, i: (b, 0, 0))] + [_HBM] * nx,
        out_shape=[jax.ShapeDtypeStruct((bl, s_len, d), _F32), jax.ShapeDtypeStruct((bl, s_len, 4 * w), _MXU_DT),
                   jax.ShapeDtypeStruct((bl, s_len, d), _MXU_DT), jax.ShapeDtypeStruct((bl, s_len, 3 * w), _MXU_DT),
                   jax.ShapeDtypeStruct((8, w), _F32), jax.ShapeDtypeStruct((4, w), _F32),
                   jax.ShapeDtypeStruct((kc, w), _F32), jax.ShapeDtypeStruct((2, d), _F32),
                   jax.ShapeDtypeStruct((bl, 3, d), _F32)]
        + [jax.ShapeDtypeStruct((3,) + t.shape[1:], t.dtype) for t in chip_sums],
        scratch_shapes=[pltpu.VMEM((tt + 8, w), _F32), pltpu.VMEM((tt + _HALO, w), _F32),
                        pltpu.VMEM((tt + 8, w), _F32), pltpu.VMEM((tt + _HALO, w), _F32),
                        pltpu.VMEM((7, tt + _HALO - 8, w), _F32), pltpu.VMEM((7, tt + _HALO - 8, w), _F32),
                        pltpu.VMEM((8, w), _F32), pltpu.VMEM((_HALO, w), _F32), pltpu.VMEM((8, w), _F32)]
        + _chip_reduce_sems(nx),
        compiler_params=_cparams(("arbitrary", "arbitrary")), name="mix_bwd",
    )(dx1, x, mix, proj, proj, h, h, vbc, lru, mod3, win, *smalls, *chip_sums)


def _wgrad(a, b, ma, nbw, na, nb, a_off, b_off, name, exchange=None):
    t = a.shape[0]
    tk = min(_TK_WGRAD, t)
    grid = (na * nb, t // tk)

    def body(a_ref, b_ref, o_ref):
        @pl.when(pl.program_id(1) == 0)
        def _():
            o_ref[...] = jnp.zeros_like(o_ref)
        o_ref[...] += _dot_tn(a_ref[...], b_ref[...])

    xin, xshapes, plan, sems = exchange if exchange else ([], [], None, [])
    nx = len(xin)
    res = pl.pallas_call(
        _fused_exchange(body, 2, 1, 0, nx, len(xshapes), plan, grid) if exchange else body, grid=grid,
        in_specs=[pl.BlockSpec((tk, ma), lambda j, k: (k, j // nb + a_off)),
                  pl.BlockSpec((tk, nbw), lambda j, k: (k, j % nb + b_off))] + [_HBM] * nx,
        out_specs=[pl.BlockSpec((None, ma, nbw), lambda j, k: (j, 0, 0))] + [_HBM] * len(xshapes),
        out_shape=[jax.ShapeDtypeStruct((na * nb, ma, nbw), _F32)] + list(xshapes),
        scratch_shapes=list(sems),
        compiler_params=_cparams(("arbitrary", "arbitrary")), name=name,
    )(a, b, *xin)
    return res if exchange else res[0]


_DEV_DELTAS = tuple(dl for dl in itertools.product((0, 1), repeat=3) if any(dl))
_HBM = pl.BlockSpec(memory_space=pltpu.HBM)
_VMEM = pl.BlockSpec(memory_space=pltpu.VMEM)


def _pos():
    return lax.axis_index("x"), lax.axis_index("y"), lax.axis_index("c")


def _flip(v, delta):
    return 1 - v if delta else v


def _remote(src, dst, ssem, rsem, dev):
    return pltpu.make_async_remote_copy(src_ref=src, dst_ref=dst, send_sem=ssem, recv_sem=rsem,
                                        device_id=dev, device_id_type=_MESH)


def _rows(ref, idx, n):
    return ref.at[pl.ds(pl.multiple_of(idx * n, 8), n)]


def _ada_fwd(c8, w_ada_k, b_ada_k, shards):
    rows, d = c8.shape
    nk = w_ada_k.shape[1]
    n = len(shards)

    def body(*refs):
        c_ref, w_ref, b_ref = refs[:3]
        call_ref, mod_ref = refs[3 + n:5 + n]
        modloc, modrcv, s1, r1, s2, r2 = refs[5 + 2 * n:11 + 2 * n]
        gather_start, gather_finish = _gather_plan(refs[3:3 + n], refs[5 + n:5 + 2 * n], *refs[11 + 2 * n:14 + 2 * n],
                                                   fsem=refs[14 + 2 * n], frsem=refs[15 + 2 * n], bounce=refs[16 + 2 * n:])
        gather_start()
        xi, yi, ci = _pos()
        me, kme = 4 * xi + 2 * yi + ci, 2 * xi + yi
        call_ref[pl.ds(pl.multiple_of(me * rows, 8), rows), :] = c_ref[...]
        sends = []
        for p, (dx, dy, dc) in enumerate(_DEV_DELTAS):
            cp = _remote(c_ref, _rows(call_ref, me, rows), s1.at[p], r1.at[p], (_flip(xi, dx), _flip(yi, dy), _flip(ci, dc)))
            cp.start()
            sends.append(cp)
        for p, (dx, dy, dc) in enumerate(_DEV_DELTAS):
            src = 4 * _flip(xi, dx) + 2 * _flip(yi, dy) + _flip(ci, dc)
            _remote(c_ref, _rows(call_ref, src, rows), s1.at[p], r1.at[p], (xi, yi, ci)).wait_recv()
        for cp in sends:
            cp.wait_send()

        ca = call_ref[...]
        modloc[...] = _dot(_mx(ca * jax.nn.sigmoid(ca)), _mx(w_ref[...])) + b_ref[...]
        modrcv[kme] = modloc[pl.ds(pl.multiple_of(me * rows, 8), rows), :]
        sends = []
        for j, (dx, dy) in enumerate(_CHIP_DELTAS):
            tx, ty = _flip(xi, dx), _flip(yi, dy)
            cp = _remote(_rows(modloc, 4 * tx + 2 * ty + ci, rows), modrcv.at[kme], s2.at[j], r2.at[j], (tx, ty, ci))
            cp.start()
            sends.append(cp)
        for j, (dx, dy) in enumerate(_CHIP_DELTAS):
            ksrc = 2 * _flip(xi, dx) + _flip(yi, dy)
            _remote(_rows(modloc, me, rows), modrcv.at[ksrc], s2.at[j], r2.at[j], (xi, yi, ci)).wait_recv()
        for cp in sends:
            cp.wait_send()
        for j in range(4):
            mod_ref[:, j * nk:(j + 1) * nk] = modrcv[j]
        gather_finish()

    return pl.pallas_call(
        body, in_specs=[_VMEM, _VMEM, _VMEM] + [_HBM] * n, out_specs=[_VMEM, _VMEM] + [_HBM] * n,
        out_shape=[jax.ShapeDtypeStruct((8 * rows, d), _F32), jax.ShapeDtypeStruct((rows, 4 * nk), _F32)]
        + [jax.ShapeDtypeStruct((4,) + a.shape, a.dtype) for a in shards],
        scratch_shapes=[pltpu.VMEM((8 * rows, nk), _F32), pltpu.VMEM((4, rows, nk), _F32),
                        pltpu.SemaphoreType.DMA((7,)), pltpu.SemaphoreType.DMA((7,)),
                        pltpu.SemaphoreType.DMA((3,)), pltpu.SemaphoreType.DMA((3,))]
        + _gather_sems(n) + [pltpu.SemaphoreType.DMA((3, n)), pltpu.SemaphoreType.DMA((3, n))]
        + [pltpu.VMEM(a.shape, a.dtype) for a in shards],
        compiler_params=pltpu.CompilerParams(vmem_limit_bytes=_VMEM_LIMIT), name="ada_fwd",
    )(c8, w_ada_k, b_ada_k, *shards)


def _gather_sems(n):
    return [pltpu.SemaphoreType.DMA((3, n)), pltpu.SemaphoreType.DMA((3, n)), pltpu.SemaphoreType.DMA((n,))]


def _gather_plan(ins, outs, ssem, rsem, lsem, bounce=(), fsem=None, frsem=None):
    n = len(ins)
    xi, yi, ci = _pos()
    kme = 2 * xi + yi
    split = [fsem is not None and ins[a].shape[0] % 32 == 0 for a in range(n)]

    def half(ref, a, which):
        r2 = ins[a].shape[0] // 2
        return ref.at[pl.ds(pl.multiple_of(which * r2, 16), r2)]

    staged = [pltpu.make_async_copy(ins[a], bounce[a], lsem.at[a]) for a in range(len(bounce))]
    local = [pltpu.make_async_copy(bounce[a] if bounce else ins[a], outs[a].at[kme], lsem.at[a]) for a in range(n)]
    sends, recvs, forwards, handed = [], [], [], []
    for j, (dx, dy) in enumerate(_CHIP_DELTAS):
        tx, ty = _flip(xi, dx), _flip(yi, dy)
        for a in range(n):
            sems = (ssem.at[j, a], rsem.at[j, a])
            landing = outs[a].at[2 * tx + ty]
            if split[a]:
                sends.append(_remote(half(ins[a], a, ci), half(outs[a].at[kme], a, ci), *sems, (tx, ty, ci)))
                recvs.append(_remote(half(ins[a], a, ci), half(landing, a, ci), *sems, (xi, yi, ci)))
                fsems = (fsem.at[j, a], frsem.at[j, a])
                forwards.append(_remote(half(landing, a, ci), half(landing, a, ci), *fsems, (xi, yi, 1 - ci)))
                handed.append(_remote(half(ins[a], a, 1 - ci), half(landing, a, 1 - ci), *fsems, (xi, yi, ci)))
            else:
                sends.append(_remote(ins[a], outs[a].at[kme], *sems, (tx, ty, ci)))
                recvs.append(_remote(ins[a], landing, *sems, (xi, yi, ci)))
                forwards.append(None)

    def start():
        for cp in sends + staged:
            cp.start()
        for cp in staged:
            cp.wait()
        for cp in local:
            cp.start()

    def finish():
        for arrived, forward in zip(recvs, forwards):
            arrived.wait_recv()
            if forward is not None:
                forward.start()
        for cp in handed:
            cp.wait_recv()
        for cp in sends + [f for f in forwards if f is not None]:
            cp.wait_send()
        for cp in local:
            cp.wait()

    return start, finish


def _dev_gather_sems(n):
    return [pltpu.SemaphoreType.DMA((7, n)), pltpu.SemaphoreType.DMA((7, n)), pltpu.SemaphoreType.DMA((n,))]


def _dev_gather_plan(ins, outs, ssem, rsem, lsem):
    n = len(ins)
    xi, yi, ci = _pos()
    me = 4 * xi + 2 * yi + ci
    local = [pltpu.make_async_copy(ins[a], outs[a].at[me], lsem.at[a]) for a in range(n)]
    sends, recvs = [], []
    for p, (dx, dy, dc) in enumerate(_DEV_DELTAS):
        tx, ty, tc = _flip(xi, dx), _flip(yi, dy), _flip(ci, dc)
        for a in range(n):
            sends.append(_remote(ins[a], outs[a].at[me], ssem.at[p, a], rsem.at[p, a], (tx, ty, tc)))
            recvs.append(_remote(ins[a], outs[a].at[4 * tx + 2 * ty + tc], ssem.at[p, a], rsem.at[p, a], (xi, yi, ci)))

    def start():
        for cp in local + sends:
            cp.start()

    def finish():
        for cp in recvs:
            cp.wait_recv()
        for cp in sends:
            cp.wait_send()
        for cp in local:
            cp.wait()

    return start, finish


def _pair_sems(n):
    return [pltpu.SemaphoreType.DMA((n,)), pltpu.SemaphoreType.DMA((n,))]


def _pair_plan(ins, outs, ssem, rsem):
    xi, yi, ci = _pos()
    sends = []
    for a in range(len(ins)):
        r2 = ins[a].shape[1] // 2
        src = ins[a].at[:, pl.ds(pl.multiple_of((1 - ci) * r2, 8), r2), :]
        sends.append(_remote(src, outs[a], ssem.at[a], rsem.at[a], (xi, yi, 1 - ci)))

    def start():
        for cp in sends:
            cp.start()

    def finish():
        for cp in sends:
            cp.wait_recv()
        for cp in sends:
            cp.wait_send()

    return start, finish


def _chip_reduce_sems(n):
    return [pltpu.SemaphoreType.DMA((3, n)), pltpu.SemaphoreType.DMA((3, n))]


def _chip_reduce_plan(ins, outs, ssem, rsem):
    xi, yi, ci = _pos()
    sends = []
    for j, (dx, dy) in enumerate(_CHIP_DELTAS):
        tx, ty = _flip(xi, dx), _flip(yi, dy)
        sends += [_remote(ins[a].at[2 * tx + ty], outs[a].at[j], ssem.at[j, a], rsem.at[j, a], (tx, ty, ci))
                  for a in range(len(ins))]

    def start():
        for cp in sends:
            cp.start()

    def finish():
        for cp in sends:
            cp.wait_recv()
        for cp in sends:
            cp.wait_send()

    return start, finish


def _pair_exchange(gs, name):
    n = len(gs)

    def body(*refs):
        start, finish = _pair_plan(refs[:n], refs[n:2 * n], *refs[2 * n:])
        start()
        finish()

    return pl.pallas_call(
        body, in_specs=[_HBM] * n, out_specs=[_HBM] * n, out_shape=_pair_out_shapes(gs),
        scratch_shapes=_pair_sems(n), name=name,
    )(*gs)


def _pair_out_shapes(gs):
    return [jax.ShapeDtypeStruct((g.shape[0], g.shape[1] // 2, g.shape[2]), g.dtype) for g in gs]


def _row_tile(r):
    return max(t for t in range(8, min(r, 256) + 1, 8) if r % t == 0)


def _pair_add(g, r, cidx, name, wire_dtype=None):
    nk, r2, c = r.shape
    tr = _row_tile(r2)
    nt = r2 // tr

    def body(c_ref, g_ref, r_ref, *o_refs):
        s = g_ref[...] + r_ref[...]
        for o_ref in o_refs:
            o_ref[...] = s.astype(o_ref.dtype)

    out_spec = pl.BlockSpec((None, tr, c), lambda k, i, cr: (k, i, 0))
    dtypes = [_F32] + ([wire_dtype] if wire_dtype else [])
    res = pl.pallas_call(
        body, grid_spec=pltpu.PrefetchScalarGridSpec(
            num_scalar_prefetch=1, grid=(nk, nt),
            in_specs=[pl.BlockSpec((None, tr, c), lambda k, i, cr: (k, cr[0] * nt + i, 0)), out_spec],
            out_specs=[out_spec] * len(dtypes)),
        out_shape=[jax.ShapeDtypeStruct(r.shape, dt) for dt in dtypes],
        compiler_params=_cparams(("arbitrary", "arbitrary")), name=name,
    )(cidx, g, r)
    return res if wire_dtype else res[0]


def _chip_exchange(ss):
    n = len(ss)

    def body(*refs):
        start, finish = _chip_reduce_plan(refs[:n], refs[n:2 * n], *refs[2 * n:])
        start()
        finish()

    return pl.pallas_call(
        body, in_specs=[_HBM] * n, out_specs=[_HBM] * n,
        out_shape=[jax.ShapeDtypeStruct((3,) + s.shape[1:], s.dtype) for s in ss],
        scratch_shapes=_chip_reduce_sems(n), name="grad_chip_exchange",
    )(*ss)


def _chip_add(s, r, kidx, name):
    _, r2, c = r.shape
    tr = _row_tile(r2)

    def body(k_ref, s_ref, r_ref, o_ref):
        o_ref[...] = ((s_ref[...] + r_ref[0].astype(_F32)) + r_ref[1].astype(_F32)) + r_ref[2].astype(_F32)

    return pl.pallas_call(
        body, grid_spec=pltpu.PrefetchScalarGridSpec(
            num_scalar_prefetch=1, grid=(r2 // tr,),
            in_specs=[pl.BlockSpec((None, tr, c), lambda i, kr: (kr[0], i, 0)),
                      pl.BlockSpec((3, tr, c), lambda i, kr: (0, i, 0))],
            out_specs=pl.BlockSpec((tr, c), lambda i, kr: (i, 0))),
        out_shape=jax.ShapeDtypeStruct((r2, c), _F32),
        compiler_params=_cparams(("arbitrary",)), name=name,
    )(kidx, s, r)


def _pair_swap(hs, name):
    n = len(hs)

    def body(*refs):
        ins, outs = refs[:n], refs[n:2 * n]
        ssem, rsem = refs[2 * n:]
        xi, yi, ci = _pos()
        sends = [_remote(ins[a], outs[a], ssem.at[a], rsem.at[a], (xi, yi, 1 - ci)) for a in range(n)]
        for cp in sends:
            cp.start()
        for cp in sends:
            cp.wait_recv()
        for cp in sends:
            cp.wait_send()

    return pl.pallas_call(
        body, in_specs=[_HBM] * n, out_specs=[_HBM] * n,
        out_shape=[jax.ShapeDtypeStruct(h.shape, h.dtype) for h in hs],
        scratch_shapes=[pltpu.SemaphoreType.DMA((n,)), pltpu.SemaphoreType.DMA((n,))], name=name,
    )(*hs)


def _small_sum(every):
    def body(all_ref, sum_ref):
        tot = all_ref[0]
        for dev in range(1, 8):
            tot = tot + all_ref[dev]
        sum_ref[...] = tot

    return pl.pallas_call(
        body, in_specs=[_VMEM], out_specs=_VMEM, out_shape=jax.ShapeDtypeStruct(every.shape[1:], _F32),
        compiler_params=pltpu.CompilerParams(vmem_limit_bytes=_VMEM_LIMIT), name="small_sum",
    )(every)


def _adamw(w, g, m, v):
    m = _ADAM_B1 * m + (1.0 - _ADAM_B1) * g
    v = _ADAM_B2 * v + (1.0 - _ADAM_B2) * (g * g)
    m_hat = m / (1.0 - _ADAM_B1 ** _ADAM_STEP)
    v_hat = v / (1.0 - _ADAM_B2 ** _ADAM_STEP)
    return -_ADAM_LR * (m_hat / (jnp.sqrt(v_hat) + _ADAM_EPS) + _ADAM_WD * w), m, v


def _adamw_big(w, g_mine, g_theirs, m, v, cidx, name, chip_sums=()):
    r, c = w.shape
    tr = _row_tile(r // 2)
    nt = r // 2 // tr
    nx = len(chip_sums)

    def body(c_ref, w_ref, gm_ref, gt_ref, m_ref, v_ref, g_ref, d_ref, mo_ref, vo_ref):
        g = jnp.where(pl.program_id(0) // nt == c_ref[0], gm_ref[...], gt_ref[...])
        g_ref[...] = g
        d_ref[...], mo_ref[...], vo_ref[...] = _adamw(w_ref[...], g, m_ref[...], v_ref[...])

    spec = pl.BlockSpec((tr, c), lambda i, cr: (i, 0))
    half = pl.BlockSpec((tr, c), lambda i, cr: (i % nt, 0))
    return pl.pallas_call(
        _fused_exchange(body, 6, 4, 0, nx, nx, _chip_reduce_plan, (2 * nt,)) if nx else body,
        grid_spec=pltpu.PrefetchScalarGridSpec(
            num_scalar_prefetch=1, grid=(2 * nt,), in_specs=[spec, half, half, spec, spec] + [_HBM] * nx,
            out_specs=[spec] * 4 + [_HBM] * nx, scratch_shapes=_chip_reduce_sems(nx) if nx else []),
        out_shape=[jax.ShapeDtypeStruct((r, c), _F32)] * 4
        + [jax.ShapeDtypeStruct((3,) + t.shape[1:], t.dtype) for t in chip_sums],
        compiler_params=_cparams(("arbitrary",)), name=name,
    )(cidx, w, g_mine, g_theirs, m, v, *chip_sums)


def _adamw_small(ws, gs, ms, vs):
    n = len(ws)
    summed = [i for i in range(n) if gs[i].shape != ws[i].shape]

    def body(*refs):
        w_r, g_r, m_r, v_r = (refs[i * n:(i + 1) * n] for i in range(4))
        outs = refs[4 * n:]
        for i in range(n):
            g = g_r[i][...]
            if i in summed:
                g = _rowsum(g)
                outs[3 * n + summed.index(i)][...] = g
            outs[i][...], outs[n + i][...], outs[2 * n + i][...] = _adamw(w_r[i][...], g, m_r[i][...], v_r[i][...])

    shapes = [jax.ShapeDtypeStruct(w.shape, _F32) for w in ws]
    res = pl.pallas_call(
        body, in_specs=[_VMEM] * (4 * n), out_specs=[_VMEM] * (3 * n + len(summed)),
        out_shape=shapes * 3 + [shapes[i] for i in summed],
        compiler_params=pltpu.CompilerParams(vmem_limit_bytes=_VMEM_LIMIT), name="adamw_small",
    )(*ws, *gs, *ms, *vs)
    gs = list(gs)
    for pos, i in enumerate(summed):
        gs[i] = res[3 * n + pos]
    return gs, res[:n], res[n:2 * n], res[2 * n:3 * n]


def _ada_bwd(c_all, dmod_k, w, m, v, chip_sums=()):
    d, nk = w.shape
    tn = 512 if nk % 512 == 0 else nk
    nx = len(chip_sums)

    def body(c_ref, dm_ref, w_ref, m_ref, v_ref, g_ref, d_ref, mo_ref, vo_ref):
        ca = c_ref[...]
        g = _dot_tn(_mx(ca * jax.nn.sigmoid(ca)), _mx(dm_ref[...]))
        g_ref[...] = g
        d_ref[...], mo_ref[...], vo_ref[...] = _adamw(w_ref[...], g, m_ref[...], v_ref[...])

    col = pl.BlockSpec((d, tn), lambda j: (0, j))
    return pl.pallas_call(
        _fused_exchange(body, 5, 4, 0, nx, nx, _chip_reduce_plan, (nk // tn,)) if nx else body, grid=(nk // tn,),
        in_specs=[pl.BlockSpec(c_all.shape, lambda j: (0, 0)), pl.BlockSpec((c_all.shape[0], tn), lambda j: (0, j)),
                  col, col, col] + [_HBM] * nx,
        out_specs=[col] * 4 + [_HBM] * nx,
        out_shape=[jax.ShapeDtypeStruct((d, nk), _F32)] * 4
        + [jax.ShapeDtypeStruct((3,) + t.shape[1:], t.dtype) for t in chip_sums],
        scratch_shapes=_chip_reduce_sems(nx) if nx else [],
        compiler_params=_cparams(("arbitrary",)), name="ada_bwd",
    )(c_all, dmod_k, w, m, v, *chip_sums)


def _block_diag(wh):
    hn, dh, _ = wh.shape
    eye = jnp.eye(hn, dtype=wh.dtype)
    return (eye[:, None, :, None] * wh[:, :, None, :]).reshape(hn * dh, hn * dh)


def _pack(pieces):
    out = []
    for p in pieces:
        flat = p.reshape(-1, 128)
        out.append(jnp.pad(flat, ((0, (-flat.shape[0]) % 8), (0, 0))))
    return jnp.concatenate(out, axis=0)


def _unpack(pack, shapes):
    out, off = [], 0
    for shp in shapes:
        rows = math.prod(shp) // 128
        out.append(pack[..., off:off + rows, :].reshape(pack.shape[:-2] + tuple(shp)))
        off += rows + (-rows) % 8
    return out


_WEIGHTS = ('w_ada', 'b_ada', 'w_in', 'lru_conv_w', 'lru_conv_b', 'lru_w_r', 'lru_b_r', 'lru_w_i', 'lru_b_i', 'lru_lambda',
            'conv_w', 'conv_b', 'conv_norm_g', 'conv_norm_b', 'w_out', 'ln1_g', 'ln1_b', 'ffn_w_up', 'ffn_conv_w',
            'ffn_conv_b', 'ffn_w_down', 'ln2_g', 'ln2_b')
_BIG = ('w_in', 'w_out', 'ffn_w_up', 'ffn_w_down')


def kernel(x, c, w_ada, b_ada, w_in, lru_conv_w, lru_conv_b, lru_w_r, lru_b_r, lru_w_i, lru_b_i, lru_lambda, conv_w, conv_b, conv_norm_g, conv_norm_b, w_out, ln1_g, ln1_b, ffn_w_up, ffn_conv_w, ffn_conv_b, ffn_w_down, ln2_g, ln2_b, loss_target, m_w_ada, m_b_ada, m_w_in, m_lru_conv_w, m_lru_conv_b, m_lru_w_r, m_lru_b_r, m_lru_w_i, m_lru_b_i, m_lru_lambda, m_conv_w, m_conv_b, m_conv_norm_g, m_conv_norm_b, m_w_out, m_ln1_g, m_ln1_b, m_ffn_w_up, m_ffn_conv_w, m_ffn_conv_b, m_ffn_w_down, m_ln2_g, m_ln2_b, v_w_ada, v_b_ada, v_w_in, v_lru_conv_w, v_lru_conv_b, v_lru_w_r, v_lru_b_r, v_lru_w_i, v_lru_b_i, v_lru_lambda, v_conv_w, v_conv_b, v_conv_norm_g, v_conv_norm_b, v_w_out, v_ln1_g, v_ln1_b, v_ffn_w_up, v_ffn_conv_w, v_ffn_conv_b, v_ffn_w_down, v_ln2_g, v_ln2_b):
    given = dict(locals())
    wt = {n: given[n] for n in _WEIGHTS}
    mo = {n: given["m_" + n] for n in _WEIGHTS}
    vo = {n: given["v_" + n] for n in _WEIGHTS}
    bl, s_len, d = x.shape
    wd = d // 2
    tokens = bl * s_len
    xi, yi, ci = _pos()
    kme = 2 * xi + yi
    kidx = jnp.reshape(kme, (1,)).astype(jnp.int32)
    cidx = jnp.reshape(ci, (1,)).astype(jnp.int32)

    nk = w_ada.shape[2]
    c8 = jnp.pad(c, ((0, 8 - bl), (0, 0)))
    c_all, mod8, win, wout_s, lcw_s, cw_s, fcw_s = _ada_fwd(
        c8, w_ada[0], lax.dynamic_slice(b_ada, (0, kme * nk), (1, nk)),
        [_mx(w_in[0]), _mx(w_out[0]), lru_conv_w[0], conv_w[0], ffn_conv_w[0]])
    mod3 = mod8[:bl].reshape(bl, 1, 6 * d)
    wout = wout_s.reshape(d, d)
    f = 4 * ffn_w_down.shape[1]
    unshard = lambda t: jnp.transpose(t, (1, 0, 2)).reshape(t.shape[1], -1)
    lcw, cw, fcw = unshard(lcw_s), unshard(cw_s), unshard(fcw_s)
    wr_bd, wi_bd = _mx(_block_diag(lru_w_r[0])), _mx(_block_diag(lru_w_i[0]))
    seg = _block_diag(jnp.ones((_N_HEADS, wd // _N_HEADS, wd // _N_HEADS), jnp.bfloat16))
    mixer_small = (lcw, lru_conv_b, wr_bd, wi_bd, lru_b_r, lru_b_i, lru_lambda, cw, conv_b, conv_norm_g, conv_norm_b, seg, wout, ln1_g)

    proj, h, mix, x1, u1, y, vbc, lru, wup, wdn_s = _mix_fwd(x, mod3, win, *mixer_small, ln1_b, [_mx(ffn_w_up[0]), _mx(ffn_w_down[0])])
    wdn = wdn_s.reshape(f, d)
    u2, hh, fact, gc_all, dz2, loss_acc, dln2, dgt2 = _ffn_fwd(x1, mod3, wup, fcw, ffn_conv_b, wdn, ln2_g, ln2_b, loss_target)
    dx1, dy2, dh, dfc, dmod2 = _ffn_bwd(dz2, x1, hh, gc_all, mod3, wup, wdn, fcw, ffn_conv_b)

    flat = lambda t: t.reshape(tokens, t.shape[-1])
    fc = wup.shape[2]
    g_up = _wgrad(flat(u2), flat(dh), d, fc, 1, 4, 0, 0, "wgrad_up")
    g_dn, r_up = _wgrad(flat(fact), flat(dy2), fc, d, f // fc, 1, 0, 0, "wgrad_down",
                        exchange=([g_up], _pair_out_shapes([g_up]), _pair_plan, _pair_sems(1)))
    g_dn = g_dn.reshape(4, f // 4, d)
    r_dn, = _pair_exchange([g_dn], "grad_pair_exchange_ffn_w_down")
    ffn_sum = [_pair_add(g, r, cidx, "grad_pair_add_" + n) for g, r, n in zip([g_up, g_dn], [r_up, r_dn], _BIG[2:])]
    grad_x, dproj, dmix, xcg, vecw, dlcw, dcw, dln1, dmod1, *ffn_recv = _mix_bwd(
        dx1, x, mix, proj, h, vbc, lru, mod3, win, *mixer_small, ffn_sum)
    g_ri = _wgrad(flat(xcg), flat(xcg), wd, wd, 1, 2, 0, 1, "wgrad_gates")
    dh_ = wd // _N_HEADS
    g_ri = jnp.stack([jnp.stack([g_ri[i, hd * dh_:(hd + 1) * dh_, hd * dh_:(hd + 1) * dh_] for hd in range(_N_HEADS)])
                      for i in range(2)])

    dmod = jnp.concatenate([dmod1.reshape(bl, 3 * d), dmod2.reshape(bl, 2 * d), dgt2.reshape(bl, d)], axis=1)
    pieces = [vecw, dlcw, dcw, jnp.concatenate([dln1, dln2], axis=0), dfc, g_ri, loss_acc[:, 0:128],
              jnp.pad(dmod, ((0, 8 - bl), (0, 0)))]
    shapes = [p.shape for p in pieces]
    pack = _pack(pieces)
    g_out = _wgrad(flat(y), flat(dmix), d, d, 1, 1, 0, 0, "wgrad_out").reshape(4, d // 4, d)
    g_in, every = _wgrad(flat(u1), flat(dproj), d, wd, 1, 4, 0, 0, "wgrad_in", exchange=(
        [pack], [jax.ShapeDtypeStruct((8,) + pack.shape, _F32)], _dev_gather_plan, _dev_gather_sems(1)))
    mix_sum, mix_wire = zip(*[_pair_add(g, r, cidx, "grad_pair_add_" + n, jnp.bfloat16) for g, r, n in zip(
        [g_in, g_out], _pair_exchange([g_in, g_out], "grad_pair_exchange_w_in"), _BIG[:2])])
    grads, deltas, new_m, new_v = {}, {}, {}, {}

    def update(names, halves, swap_name, riders):
        received = []
        for n, mine, theirs, rider in zip(names, halves, _pair_swap(halves, swap_name), riders):
            g, dl, mm, vv, *got = _adamw_big(wt[n][0], mine, theirs, mo[n][0], vo[n][0], cidx, "adamw_" + n, rider)
            grads[n], deltas[n], new_m[n], new_v[n] = g[None], dl[None], mm[None], vv[None]
            received += got
        return received

    ffn_half = [_chip_add(s, r, kidx, "grad_chip_add_" + n) for s, r, n in zip(ffn_sum, ffn_recv, _BIG[2:])]
    recv_in = update(_BIG[2:], ffn_half, "grad_pair_swap_ffn", [[mix_wire[0]], []])

    total = _small_sum(every)
    vecw, dlcw, dcw, dln, dfc, g_ri, loss_sum, dmod_sum = _unpack(total, shapes)
    loss = 0.5 * loss_sum[0, 0] / d
    dmod_all = _unpack(every, shapes)[-1].reshape(64, 6 * d)

    g_ada, dl, mm, vv, *recv_out = _ada_bwd(c_all, lax.dynamic_slice(dmod_all, (0, kme * nk), (64, nk)),
                                            w_ada[0], m_w_ada[0], v_w_ada[0], [mix_wire[1]])
    grads['w_ada'], deltas['w_ada'], new_m['w_ada'], new_v['w_ada'] = g_ada[None], dl[None], mm[None], vv[None]
    mix_half = [_chip_add(s, r, kidx, "grad_chip_add_" + n) for s, r, n in zip(mix_sum, recv_in + recv_out, _BIG[:2])]
    update(_BIG[:2], mix_half, "grad_pair_swap_mixer", [[], []])

    shard = lambda t, width: lax.dynamic_slice(t, (0, kme * width), (t.shape[0], width))
    small = {
        'b_ada': dmod_sum, 'lru_conv_w': shard(dlcw, wd // 4), 'lru_conv_b': vecw[0:1], 'lru_w_r': g_ri[0], 'lru_b_r': vecw[1:2],
        'lru_w_i': g_ri[1], 'lru_b_i': vecw[2:3], 'lru_lambda': vecw[3:4], 'conv_w': shard(dcw, wd // 4), 'conv_b': vecw[4:5],
        'conv_norm_g': vecw[5:6], 'conv_norm_b': vecw[6:7], 'ln1_g': dln[0:1], 'ln1_b': dln[1:2],
        'ffn_conv_w': shard(dfc[0:3], f // 4), 'ffn_conv_b': dfc[3:4], 'ln2_g': dln[2:3], 'ln2_b': dln[3:4]}
    names = list(small)
    gs = [small[n] if n == 'b_ada' else small[n].reshape(wt[n].shape) for n in names]
    gs, dls, mms, vvs = _adamw_small([wt[n] for n in names], gs, [mo[n] for n in names], [vo[n] for n in names])
    for n, g, dl, mm, vv in zip(names, gs, dls, mms, vvs):
        grads[n], deltas[n], new_m[n], new_v[n] = g, dl, mm, vv

    return (loss, grad_x, *[grads[n] for n in _WEIGHTS], *[deltas[n] for n in _WEIGHTS],
            *[new_m[n] for n in _WEIGHTS], *[new_v[n] for n in _WEIGHTS])
```

```python
import functools
import itertools
import math

import jax
import jax.numpy as jnp
from jax import lax
from jax.experimental import pallas as pl
from jax.experimental.pallas import tpu as pltpu

_MXU_DT = jnp.bfloat16
_F32 = jnp.float32
_VMEM_LIMIT = 56 * 1024 * 1024
_TT_MIX = 256
_TT_MIX_FWD = 512
_TT_FFN = 256
_TK_WGRAD = 2048
_HALO = 32

_LRU_C = 8.0
_LN_EPS = 1e-5
_N_HEADS = 8
_DEPTH = 1
_ALPHA = (2 * _DEPTH) ** 0.25
_ADAM_LR, _ADAM_B1, _ADAM_B2, _ADAM_EPS, _ADAM_WD, _ADAM_STEP = 0.001, 0.9, 0.999, 1e-08, 0.01, 10

_MESH = pl.DeviceIdType.MESH
_CHIP_DELTAS = ((1, 0), (0, 1), (1, 1))


def _cparams(sem):
    return pltpu.CompilerParams(dimension_semantics=sem, vmem_limit_bytes=_VMEM_LIMIT)


def _resident(shape):
    nd = len(shape)
    return pl.BlockSpec(shape, lambda *_: (0,) * nd, pipeline_mode=pl.Buffered(1))


def _dot(a, b):
    return jnp.dot(a, b, preferred_element_type=_F32)


def _dot_nt(a, b):
    return lax.dot_general(a, b, (((1,), (1,)), ((), ())), preferred_element_type=_F32)


def _dot_tn(a, b):
    return lax.dot_general(a, b, (((0,), (0,)), ((), ())), preferred_element_type=_F32)


def _mx(v):
    return v.astype(_MXU_DT)


def _expm1(v):
    series = v * (1.0 + v * (1.0 / 2 + v * (1.0 / 6 + v * (1.0 / 24 + v * (1.0 / 120)))))
    return jnp.where(jnp.abs(v) < 0.0625, series, jnp.exp(v) - 1.0)


def _softplus(z):
    e = jnp.exp(-jnp.abs(z))
    u = 1.0 + e
    log1p = jnp.where(u == 1.0, e, jnp.log(u) * e / jnp.where(u == 1.0, 1.0, u - 1.0))
    return jnp.maximum(z, 0.0) + log1p


_GELU_C = math.sqrt(2.0 / math.pi)


def _gelu_and_grad(v):
    t = jnp.tanh(_GELU_C * (v + 0.044715 * v * v * v))
    val = 0.5 * v * (1.0 + t)
    grad = 0.5 * (1.0 + t) + 0.5 * v * (1.0 - t * t) * _GELU_C * (1.0 + 3 * 0.044715 * v * v)
    return val, grad


def _seg_sum(v, seg, passes=3):
    hi = v.astype(jnp.bfloat16)
    r1 = v - hi.astype(_F32)
    mid = r1.astype(jnp.bfloat16)
    out = _dot(hi, seg) + _dot(mid, seg)
    if passes == 3:
        out = out + _dot((r1 - mid.astype(_F32)).astype(jnp.bfloat16), seg)
    return out


def _scan_fwd(a, u, h0):
    n = a.shape[0]
    row = lax.broadcasted_iota(jnp.int32, a.shape, 0)
    h, d = u, 1
    while d < n:
        keep = row >= d
        h = a * jnp.where(keep, pltpu.roll(h, d, 0), 0.0) + h
        a = a * jnp.where(keep, pltpu.roll(a, d, 0), 1.0)
        d *= 2
    return h + a * h0


def _scan_rev(c, g, g_end):
    n = c.shape[0]
    row = lax.broadcasted_iota(jnp.int32, c.shape, 0)
    d = 1
    while d < n:
        keep = row < n - d
        g = c * jnp.where(keep, pltpu.roll(g, n - d, 0), 0.0) + g
        c = c * jnp.where(keep, pltpu.roll(c, n - d, 0), 1.0)
        d *= 2
    return g + c * g_end


def _layer_norm_stats(z):
    mu = jnp.mean(z, axis=-1, keepdims=True)
    zc = z - mu
    var = jnp.mean(zc * zc, axis=-1, keepdims=True)
    rstd = lax.rsqrt(var + _LN_EPS)
    return zc * rstd, rstd


def _layer_norm_bwd(dn, n, rstd):
    return rstd * (dn - jnp.mean(dn, axis=-1, keepdims=True) - n * jnp.mean(dn * n, axis=-1, keepdims=True))


def _rowsum(v):
    return jnp.sum(v, axis=0, keepdims=True)


def _fused_exchange(body, n_in, n_out, n_scratch, n_xin, n_xout, plan, grid):
    def wrapped(*refs):
        o0 = n_in + n_xin
        s0 = o0 + n_out + n_xout
        start, finish = plan(refs[n_in:o0], refs[o0 + n_out:s0], *refs[s0 + n_scratch:])
        step = 0
        for axis, size in enumerate(grid):
            step = step * size + pl.program_id(axis)

        @pl.when(step == 0)
        def _():
            start()

        body(*refs[:n_in], *refs[o0:o0 + n_out], *refs[s0:s0 + n_scratch])

        @pl.when(step == math.prod(grid) - 1)
        def _():
            finish()

    return wrapped


def _lru_gates(xc, wr_ref, wi_ref, br_ref, bi_ref, lam_ref):
    xcb = _mx(xc)
    r = jax.nn.sigmoid(_dot(xcb, wr_ref[...]) + br_ref[...])
    i = jax.nn.sigmoid(_dot(xcb, wi_ref[...]) + bi_ref[...])
    sp = _softplus(-lam_ref[...])
    log_a = -_LRU_C * r * sp
    a = jnp.exp(log_a)
    mult = jnp.sqrt(-_expm1(2.0 * log_a))
    return r, i, sp, a, mult


def _conv_taps(ext_ref, w_ref, first, n_taps, tt):
    acc = w_ref[0:1, :] * ext_ref[pl.ds(first, tt), :]
    for k in range(1, n_taps):
        acc = acc + w_ref[k:k + 1, :] * ext_ref[pl.ds(first + k, tt), :]
    return acc


def _make_shifted(ext_ref, sh_ref):
    n = sh_ref.shape[1]
    for r in range(1, 8):
        sh_ref[r - 1] = ext_ref[pl.ds(r, n), :]


def _tap(ext_ref, sh_ref, off, tt):
    base = (off // 8) * 8
    if off % 8 == 0:
        return ext_ref[pl.ds(base, tt), :]
    return sh_ref[off % 8 - 1, pl.ds(base, tt), :]


def _conv_taps_shifted(ext_ref, sh_ref, w_ref, first, n_taps, tt):
    acc = w_ref[0:1, :] * _tap(ext_ref, sh_ref, first, tt)
    for k in range(1, n_taps):
        acc = acc + w_ref[k:k + 1, :] * _tap(ext_ref, sh_ref, first + k, tt)
    return acc


def _mix_fwd(x, mod3, win, lcw, lcb, wr_bd, wi_bd, b_r, b_i, lam, cw, cb, ng, nb, seg, wout, ln1g, ln1b, shards):
    bl, s_len, d = x.shape
    w = d // 2
    tt = min(_TT_MIX_FWD, s_len)
    ns = s_len // tt
    kc = cw.shape[0]

    def body(x_ref, mod_ref, win_ref, lcw_ref, lcb_ref, wr_ref, wi_ref, br_ref, bi_ref, lam_ref, cw_ref, cb_ref,
             ng_ref, nb_ref, seg_ref, wout_ref, g1_ref, b1_ref,
             proj_ref, h_ref, mix_ref, x1_ref, u1_ref, y_ref, vbc_ref, lru_ref, ext4, ext31, sh31, hcar):
        @pl.when(pl.program_id(1) == 0)
        def _():
            ext4[0:8, :] = jnp.zeros((8, w), _F32)
            ext31[0:_HALO, :] = jnp.zeros((_HALO, w), _F32)
            hcar[...] = jnp.zeros_like(hcar)

        xt = x_ref[...]
        sh1, sc1, gt1 = mod_ref[:, 0:d], mod_ref[:, d:2 * d], mod_ref[:, 2 * d:3 * d]
        u1 = _mx(xt * (1.0 + sc1) + sh1)
        u1_ref[...] = u1
        xa, ga, vb, gb = (_dot(u1, win_ref[k]) for k in range(4))
        proj_ref[:, 0:w] = xa
        proj_ref[:, w:2 * w] = ga
        proj_ref[:, 2 * w:3 * w] = vb
        proj_ref[:, 3 * w:4 * w] = gb

        ext4[8:8 + tt, :] = xa
        xc = lcb_ref[...] + _conv_taps(ext4, lcw_ref, 5, 4, tt)
        ext4[0:8, :] = xa[tt - 8:tt, :]
        r, i, sp, a, mult = _lru_gates(xc, wr_ref, wi_ref, br_ref, bi_ref, lam_ref)
        for k, val in enumerate((xc, r, i, a, mult)):
            lru_ref[:, k * w:(k + 1) * w] = val
        h = _scan_fwd(a, mult * (i * xc), hcar[0:1, :])
        hcar[0:1, :] = h[tt - 1:tt, :]
        h_ref[...] = h
        gelu, _ = _gelu_and_grad(ga)
        y_ref[:, 0:w] = _mx(gelu * h)

        vbg = vb * jax.nn.sigmoid(gb)
        ext31[_HALO:_HALO + tt, :] = vbg
        _make_shifted(ext31, sh31)
        vbc = cb_ref[...] + _conv_taps_shifted(ext31, sh31, cw_ref, _HALO - (kc - 1), kc, tt)
        vbc_ref[...] = vbc
        ext31[0:_HALO, :] = vbg[tt - _HALO:tt, :]
        inv = 1.0 / (w // _N_HEADS)
        zc = vbc - _seg_sum(vbc, seg_ref[...]) * inv
        n = zc * lax.rsqrt(_seg_sum(zc * zc, seg_ref[...]) * inv + _LN_EPS)
        pre = n * ng_ref[...] + nb_ref[...]
        y_ref[:, w:2 * w] = _mx(pre * jax.nn.sigmoid(pre))

        mix = _dot(y_ref[...], wout_ref[...])
        mix_ref[...] = mix
        n1, _ = _layer_norm_stats(_ALPHA * xt + (1.0 + gt1) * mix)
        x1_ref[...] = n1 * g1_ref[...] + b1_ref[...]

    tok = lambda c: pl.BlockSpec((None, tt, c), lambda b, s: (b, s, 0))
    smalls = [lcw, lcb, wr_bd, wi_bd, b_r, b_i, lam, cw, cb, ng, nb, seg, wout, ln1g, ln1b]
    nx = len(shards)
    return pl.pallas_call(
        _fused_exchange(body, 3 + len(smalls), 8, 4, nx, nx, _gather_plan, (bl, ns)), grid=(bl, ns),
        in_specs=[tok(d), pl.BlockSpec((None, 1, 6 * d), lambda b, s: (b, 0, 0)), _resident(win.shape)]
        + [_resident(t.shape) for t in smalls] + [_HBM] * nx,
        out_specs=[tok(4 * w), tok(w), tok(d), tok(d), tok(d), tok(d), tok(w), tok(5 * w)] + [_HBM] * nx,
        out_shape=[jax.ShapeDtypeStruct((bl, s_len, 4 * w), _F32), jax.ShapeDtypeStruct((bl, s_len, w), _F32),
                   jax.ShapeDtypeStruct((bl, s_len, d), _F32), jax.ShapeDtypeStruct((bl, s_len, d), _F32),
                   jax.ShapeDtypeStruct((bl, s_len, d), _MXU_DT), jax.ShapeDtypeStruct((bl, s_len, d), _MXU_DT),
                   jax.ShapeDtypeStruct((bl, s_len, w), _F32), jax.ShapeDtypeStruct((bl, s_len, 5 * w), _F32)]
        + [jax.ShapeDtypeStruct((4,) + t.shape, t.dtype) for t in shards],
        scratch_shapes=[pltpu.VMEM((tt + 8, w), _F32), pltpu.VMEM((tt + _HALO, w), _F32),
                        pltpu.VMEM((7, tt + _HALO - 8, w), _F32), pltpu.VMEM((8, w), _F32)] + _gather_sems(nx),
        compiler_params=_cparams(("arbitrary", "arbitrary")), name="mix_fwd",
    )(x, mod3, win, *smalls, *shards)


def _ffn_fwd(x1, mod3, wup, fcw, fcb, wdn, ln2g, ln2b, target):
    bl, s_len, d = x1.shape
    nch, _, fc = wup.shape
    nch //= 2
    f = nch * fc
    tt = min(_TT_FFN, s_len)
    ns = s_len // tt

    def body(x1_ref, mod_ref, wup_ref, fcw_ref, fcb_ref, wdn_ref, g2_ref, b2_ref, tgt_ref,
             u2_ref, hh_ref, f_ref, gc_ref, dz2_ref, loss_ref, dln2_ref, dgt2_ref, ext3):
        first_tile = pl.program_id(1) == 0

        @pl.when(first_tile)
        def _():
            ext3[:, 0:8, :] = jnp.zeros((nch, 8, fc), _F32)
            dgt2_ref[...] = jnp.zeros_like(dgt2_ref)

        @pl.when(first_tile & (pl.program_id(0) == 0))
        def _():
            loss_ref[...] = jnp.zeros_like(loss_ref)
            dln2_ref[...] = jnp.zeros_like(dln2_ref)

        x1t = x1_ref[...]
        sh2, sc2, gt2 = mod_ref[:, 3 * d:4 * d], mod_ref[:, 4 * d:5 * d], mod_ref[:, 5 * d:6 * d]
        u2 = _mx(x1t * (1.0 + sc2) + sh2)
        u2_ref[...] = u2
        y2 = jnp.zeros((tt, d), _F32)
        for j in range(nch):
            lanes = slice(j * fc, (j + 1) * fc)
            v = _dot(u2, wup_ref[j])
            g = _dot(u2, wup_ref[nch + j])
            hh_ref[:, lanes] = v.astype(hh_ref.dtype)
            hh_ref[:, f + j * fc:f + (j + 1) * fc] = g.astype(hh_ref.dtype)
            ext = ext3.at[j]
            ext[8:8 + tt, :] = g
            gc = fcb_ref[:, lanes] + sum(fcw_ref[k:k + 1, lanes] * ext[pl.ds(6 + k, tt), :] for k in range(3))
            gc_ref[:, lanes] = gc
            ext[0:8, :] = g[tt - 8:tt, :]
            fj = _mx(gc * jax.nn.sigmoid(gc) * v)
            f_ref[:, lanes] = fj
            y2 = y2 + _dot(fj, wdn_ref[lanes, :])

        n2, rstd = _layer_norm_stats(_ALPHA * x1t + (1.0 + gt2) * y2)
        err = n2 * g2_ref[...] + b2_ref[...] - tgt_ref[...]
        loss_ref[...] += jnp.sum(_rowsum(err * err), axis=1, keepdims=True)
        dout = err * (1.0 / d)
        dln2_ref[0:1, :] += _rowsum(dout * n2)
        dln2_ref[1:2, :] += _rowsum(dout)
        dz2 = _layer_norm_bwd(dout * g2_ref[...], n2, rstd)
        dz2_ref[...] = dz2
        dgt2_ref[...] += _rowsum(dz2 * y2)

    tok = lambda c: pl.BlockSpec((None, tt, c), lambda b, s: (b, s, 0))
    acc = lambda r: pl.BlockSpec((r, d), lambda b, s: (0, 0))
    smalls = [fcw, fcb, wdn, ln2g, ln2b]
    return pl.pallas_call(
        body, grid=(bl, ns),
        in_specs=[tok(d), pl.BlockSpec((None, 1, 6 * d), lambda b, s: (b, 0, 0)), _resident(wup.shape)]
        + [_resident(t.shape) for t in smalls] + [tok(d)],
        out_specs=[tok(d), tok(2 * f), tok(f), tok(f), tok(d), acc(1), acc(2), pl.BlockSpec((None, 1, d), lambda b, s: (b, 0, 0))],
        out_shape=[jax.ShapeDtypeStruct((bl, s_len, d), _MXU_DT), jax.ShapeDtypeStruct((bl, s_len, 2 * f), _F32),
                   jax.ShapeDtypeStruct((bl, s_len, f), _MXU_DT), jax.ShapeDtypeStruct((bl, s_len, f), _F32),
                   jax.ShapeDtypeStruct((bl, s_len, d), _F32), jax.ShapeDtypeStruct((1, d), _F32), jax.ShapeDtypeStruct((2, d), _F32),
                   jax.ShapeDtypeStruct((bl, 1, d), _F32)],
        scratch_shapes=[pltpu.VMEM((nch, tt + 8, fc), _F32)],
        compiler_params=_cparams(("arbitrary", "arbitrary")), name="ffn_fwd",
    )(x1, mod3, wup, *smalls, target)


def _ffn_bwd(dz2, x1, hh, gc_all, mod3, wup, wdn, fcw, fcb):
    bl, s_len, d = x1.shape
    nch, _, fc = wup.shape
    nch //= 2
    f = nch * fc
    tt = min(_TT_FFN, s_len)
    ns = s_len // tt

    def body(dz2_ref, x1_ref, hh_ref, gc_ref, mod_ref, wup_ref, wdn_ref, fcw_ref, fcb_ref,
             dx1_ref, dy2_ref, dh_ref, dfc_ref, dmod_ref, dext, dcar):
        @pl.when(pl.program_id(1) == 0)
        def _():
            dcar[...] = jnp.zeros_like(dcar)
            dmod_ref[...] = jnp.zeros_like(dmod_ref)

        @pl.when((pl.program_id(1) == 0) & (pl.program_id(0) == 0))
        def _():
            dfc_ref[...] = jnp.zeros_like(dfc_ref)

        sc2, gt2 = mod_ref[:, 4 * d:5 * d], mod_ref[:, 5 * d:6 * d]
        dz2t = dz2_ref[...]
        dy2 = _mx((1.0 + gt2) * dz2t)
        dy2_ref[...] = dy2
        du2 = jnp.zeros((tt, d), _F32)
        for j in range(nch):
            lanes = slice(j * fc, (j + 1) * fc)
            glanes = slice(f + j * fc, f + (j + 1) * fc)
            v = hh_ref[:, lanes].astype(_F32)
            g = hh_ref[:, glanes].astype(_F32)
            gc = gc_ref[:, lanes]
            sg = jax.nn.sigmoid(gc)
            df = _dot_nt(dy2, wdn_ref[lanes, :])
            dv = df * (gc * sg)
            dgc = df * v * (sg * (1.0 + gc * (1.0 - sg)))
            dfc_ref[3:4, lanes] += _rowsum(dgc)
            dext[0:tt, :] = dgc
            dext[tt:tt + 8, :] = dcar[j]
            dcar[j] = dgc[0:8, :]
            dg = jnp.zeros((tt, fc), _F32)
            for k in range(3):
                shifted = dext[pl.ds(2 - k, tt), :]
                dg = dg + fcw_ref[k:k + 1, lanes] * shifted
                dfc_ref[k:k + 1, lanes] += _rowsum(shifted * g)
            dvb, dgb = _mx(dv), _mx(dg)
            dh_ref[:, lanes] = dvb
            dh_ref[:, glanes] = dgb
            du2 = du2 + _dot_nt(dvb, wup_ref[j]) + _dot_nt(dgb, wup_ref[nch + j])

        dx1_ref[...] = _ALPHA * dz2t + du2 * (1.0 + sc2)
        dmod_ref[0:1, :] += _rowsum(du2)
        dmod_ref[1:2, :] += _rowsum(du2 * x1_ref[...])

    tok = lambda c: pl.BlockSpec((None, tt, c), lambda b, i: (b, ns - 1 - i, 0))
    return pl.pallas_call(
        body, grid=(bl, ns),
        in_specs=[tok(d), tok(d), tok(2 * f), tok(f), pl.BlockSpec((None, 1, 6 * d), lambda b, i: (b, 0, 0)),
                  _resident(wup.shape), _resident(wdn.shape), _resident(fcw.shape), _resident(fcb.shape)],
        out_specs=[tok(d), tok(d), tok(2 * f), pl.BlockSpec((4, f), lambda b, i: (0, 0)),
                   pl.BlockSpec((None, 2, d), lambda b, i: (b, 0, 0))],
        out_shape=[jax.ShapeDtypeStruct((bl, s_len, d), _F32), jax.ShapeDtypeStruct((bl, s_len, d), _MXU_DT),
                   jax.ShapeDtypeStruct((bl, s_len, 2 * f), _MXU_DT), jax.ShapeDtypeStruct((4, f), _F32),
                   jax.ShapeDtypeStruct((bl, 2, d), _F32)],
        scratch_shapes=[pltpu.VMEM((tt + 8, fc), _F32), pltpu.VMEM((nch, 8, fc), _F32)],
        compiler_params=_cparams(("arbitrary", "arbitrary")), name="ffn_bwd",
    )(dz2, x1, hh, gc_all, mod3, wup, wdn, fcw, fcb)


def _mix_bwd(dx1, x, mix, proj, h, vbc, lru, mod3, win, lcw, lcb, wr_bd, wi_bd, b_r, b_i, lam, cw, cb, ng, nb, seg, wout, ln1g, chip_sums):
    bl, s_len, d = x.shape
    w = d // 2
    tt = min(_TT_MIX, s_len)
    ns = s_len // tt
    kc = cw.shape[0]

    def body(dx1_ref, x_ref, mix_ref, proj_ref, phalo_ref, h_ref, hhalo_ref, vbc_ref, lru_ref, mod_ref, win_ref, lcw_ref, lcb_ref,
             wr_ref, wi_ref, br_ref, bi_ref, lam_ref, cw_ref, cb_ref, ng_ref, nb_ref, seg_ref, wout_ref, g1_ref,
             gx_ref, dproj_ref, dmix_ref, xcg_ref, vecw_ref, dlcw_ref, dcw_ref, dln1_ref, dmod_ref,
             ext4, ext31, dext4, dext31, sh31, dsh31, car4, car31, gcar):
        s = ns - 1 - pl.program_id(1)
        first = s == 0

        @pl.when(pl.program_id(1) == 0)
        def _():
            car4[...] = jnp.zeros_like(car4)
            car31[...] = jnp.zeros_like(car31)
            gcar[...] = jnp.zeros_like(gcar)
            dmod_ref[...] = jnp.zeros_like(dmod_ref)

        @pl.when((pl.program_id(1) == 0) & (pl.program_id(0) == 0))
        def _():
            for ref in (vecw_ref, dlcw_ref, dcw_ref, dln1_ref):
                ref[...] = jnp.zeros_like(ref)

        xt, mixt = x_ref[...], mix_ref[...]
        sh1, sc1, gt1 = mod_ref[:, 0:d], mod_ref[:, d:2 * d], mod_ref[:, 2 * d:3 * d]

        n1, rstd1 = _layer_norm_stats(_ALPHA * xt + (1.0 + gt1) * mixt)
        dx1t = dx1_ref[...]
        dln1_ref[0:1, :] += _rowsum(dx1t * n1)
        dln1_ref[1:2, :] += _rowsum(dx1t)
        dz1 = _layer_norm_bwd(dx1t * g1_ref[...], n1, rstd1)
        dmod_ref[2:3, :] += _rowsum(dz1 * mixt)
        dmix = _mx((1.0 + gt1) * dz1)
        dmix_ref[...] = dmix
        dya = _dot_nt(dmix, wout_ref[0:w, :])
        dyb = _dot_nt(dmix, wout_ref[w:2 * w, :])

        xa, ga = proj_ref[:, 0:w], proj_ref[:, w:2 * w]
        vb, gb = proj_ref[:, 2 * w:3 * w], proj_ref[:, 3 * w:4 * w]

        sgb = jax.nn.sigmoid(gb)
        vbg = vb * sgb
        hv, hg = phalo_ref[:, 2 * w:3 * w], phalo_ref[:, 3 * w:4 * w]
        ext31[0:_HALO, :] = jnp.where(first, 0.0, hv * jax.nn.sigmoid(hg))
        ext31[_HALO:_HALO + tt, :] = vbg
        _make_shifted(ext31, sh31)
        vbc = vbc_ref[...]
        inv = 1.0 / (w // _N_HEADS)
        zc = vbc - _seg_sum(vbc, seg_ref[...]) * inv
        rstd = lax.rsqrt(_seg_sum(zc * zc, seg_ref[...]) * inv + _LN_EPS)
        n = zc * rstd
        pre = n * ng_ref[...] + nb_ref[...]
        sgp = jax.nn.sigmoid(pre)
        dpre = dyb * (sgp * (1.0 + pre * (1.0 - sgp)))
        vecw_ref[5:6, :] += _rowsum(dpre * n)
        vecw_ref[6:7, :] += _rowsum(dpre)
        dn = dpre * ng_ref[...]
        dvbc = rstd * (dn - _seg_sum(dn, seg_ref[...], 2) * inv - n * (_seg_sum(dn * n, seg_ref[...], 2) * inv))
        vecw_ref[4:5, :] += _rowsum(dvbc)
        dext31[0:tt, :] = dvbc
        dext31[tt:tt + _HALO, :] = car31[...]
        car31[...] = dvbc[0:_HALO, :]
        _make_shifted(dext31, dsh31)
        dvbg = jnp.zeros((tt, w), _F32)
        for k in range(kc):
            dvbg = dvbg + cw_ref[k:k + 1, :] * _tap(dext31, dsh31, kc - 1 - k, tt)
            dcw_ref[k:k + 1, :] += _rowsum(dvbc * _tap(ext31, sh31, _HALO - (kc - 1) + k, tt))
        dproj_ref[:, 2 * w:3 * w] = _mx(dvbg * sgb)
        dproj_ref[:, 3 * w:4 * w] = _mx(dvbg * vb * (sgb * (1.0 - sgb)))

        ext4[0:8, :] = jnp.where(first, 0.0, phalo_ref[_HALO - 8:_HALO, 0:w])
        ext4[8:8 + tt, :] = xa
        xc, r, i, a, mult = (lru_ref[:, k * w:(k + 1) * w] for k in range(5))
        xcg_ref[:, 0:w] = _mx(xc)
        sp = _softplus(-lam_ref[...])
        ht = h_ref[...]
        row = lax.broadcasted_iota(jnp.int32, (tt, w), 0)
        h_before = jnp.where(first, 0.0, hhalo_ref[7:8, :])
        hprev = jnp.where(row == 0, h_before, pltpu.roll(ht, 1, 0))
        gelu, dgelu = _gelu_and_grad(ga)
        dproj_ref[:, w:2 * w] = _mx(dya * ht * dgelu)
        dh = dya * gelu
        coef = jnp.where(row == tt - 1, 1.0, pltpu.roll(a, tt - 1, 0))
        big_g = _scan_rev(coef, dh, gcar[0:1, :])
        gcar[0:1, :] = a[0:1, :] * big_g[0:1, :]
        da = big_g * hprev
        ixc = i * xc
        dlog_a = da * a - (big_g * ixc) * (a * a / mult)
        di = big_g * mult * xc
        dxc = big_g * mult * i
        vecw_ref[3:4, :] += _rowsum(dlog_a * r) * (_LRU_C * jax.nn.sigmoid(-lam_ref[...]))
        dgr_f = dlog_a * (-_LRU_C * sp) * (r * (1.0 - r))
        dgi_f = di * (i * (1.0 - i))
        vecw_ref[1:2, :] += _rowsum(dgr_f)
        vecw_ref[2:3, :] += _rowsum(dgi_f)
        dgr, dgi = _mx(dgr_f), _mx(dgi_f)
        xcg_ref[:, w:2 * w] = dgr
        xcg_ref[:, 2 * w:3 * w] = dgi
        dxc = dxc + _dot_nt(dgr, wr_ref[...]) + _dot_nt(dgi, wi_ref[...])
        vecw_ref[0:1, :] += _rowsum(dxc)
        dext4[0:tt, :] = dxc
        dext4[tt:tt + 8, :] = car4[...]
        car4[...] = dxc[0:8, :]
        dxa = jnp.zeros((tt, w), _F32)
        for k in range(4):
            dxa = dxa + lcw_ref[k:k + 1, :] * dext4[pl.ds(3 - k, tt), :]
            dlcw_ref[k:k + 1, :] += _rowsum(dxc * ext4[pl.ds(5 + k, tt), :])
        dproj_ref[:, 0:w] = _mx(dxa)

        du1 = sum(_dot_nt(dproj_ref[:, k * w:(k + 1) * w], win_ref[k]) for k in range(4))
        gx_ref[...] = _ALPHA * dz1 + du1 * (1.0 + sc1)
        dmod_ref[0:1, :] += _rowsum(du1)
        dmod_ref[1:2, :] += _rowsum(du1 * xt)

    tok = lambda c: pl.BlockSpec((None, tt, c), lambda b, i: (b, ns - 1 - i, 0))
    halo = lambda rows, c: pl.BlockSpec(
        (None, rows, c), lambda b, i: (b, jnp.maximum((ns - 1 - i) * (tt // rows) - 1, 0), 0))
    accw = lambda r, c: pl.BlockSpec((r, c), lambda b, i: (0, 0))
    smalls = [lcw, lcb, wr_bd, wi_bd, b_r, b_i, lam, cw, cb, ng, nb, seg, wout, ln1g]
    nx = len(chip_sums)
    return pl.pallas_call(
        _fused_exchange(body, 11 + len(smalls), 9, 9, nx, nx, _chip_reduce_plan, (bl, ns)), grid=(bl, ns),
        in_specs=[tok(d), tok(d), tok(d), tok(4 * w), halo(_HALO, 4 * w), tok(w), halo(8, w), tok(w), tok(5 * w),
                  pl.BlockSpec((None, 1, 6 * d), lambda b, i: (b, 0, 0)), _resident(win.shape)]
        + [_resident(t.shape) for t in smalls] + [_HBM] * nx,
        out_specs=[tok(d), tok(4 * w), tok(d), tok(3 * w), accw(8, w), accw(4, w), accw(kc, w), accw(2, d),
                   pl.BlockSpec((None, 3, d), lambda b, i: (b, 0, 0))] + [_HBM] * nx,
        out_shape=[jax.ShapeDtypeStruct((bl, s_len, d), _F32), jax.ShapeDtypeStruct((bl, s_len, 4 * w), _MXU_DT),
                   jax.ShapeDtypeStruct((bl, s_len, d), _MXU_DT), jax.ShapeDtypeStruct((bl, s_len, 3 * w), _MXU_DT),
                   jax.ShapeDtypeStruct((8, w), _F32), jax.ShapeDtypeStruct((4, w), _F32),
                   jax.ShapeDtypeStruct((kc, w), _F32), jax.ShapeDtypeStruct((2, d), _F32),
                   jax.ShapeDtypeStruct((bl, 3, d), _F32)]
        + [jax.ShapeDtypeStruct((3,) + t.shape[1:], t.dtype) for t in chip_sums],
        scratch_shapes=[pltpu.VMEM((tt + 8, w), _F32), pltpu.VMEM((tt + _HALO, w), _F32),
                        pltpu.VMEM((tt + 8, w), _F32), pltpu.VMEM((tt + _HALO, w), _F32),
                        pltpu.VMEM((7, tt + _HALO - 8, w), _F32), pltpu.VMEM((7, tt + _HALO - 8, w), _F32),
                        pltpu.VMEM((8, w), _F32), pltpu.VMEM((_HALO, w), _F32), pltpu.VMEM((8, w), _F32)]
        + _chip_reduce_sems(nx),
        compiler_params=_cparams(("arbitrary", "arbitrary")), name="mix_bwd",
    )(dx1, x, mix, proj, proj, h, h, vbc, lru, mod3, win, *smalls, *chip_sums)


def _wgrad(a, b, ma, nbw, na, nb, a_off, b_off, name, exchange=None):
    t = a.shape[0]
    tk = min(_TK_WGRAD, t)
    grid = (na * nb, t // tk)

    def body(a_ref, b_ref, o_ref):
        @pl.when(pl.program_id(1) == 0)
        def _():
            o_ref[...] = jnp.zeros_like(o_ref)
        o_ref[...] += _dot_tn(a_ref[...], b_ref[...])

    xin, xshapes, plan, sems = exchange if exchange else ([], [], None, [])
    nx = len(xin)
    res = pl.pallas_call(
        _fused_exchange(body, 2, 1, 0, nx, len(xshapes), plan, grid) if exchange else body, grid=grid,
        in_specs=[pl.BlockSpec((tk, ma), lambda j, k: (k, j // nb + a_off)),
                  pl.BlockSpec((tk, nbw), lambda j, k: (k, j % nb + b_off))] + [_HBM] * nx,
        out_specs=[pl.BlockSpec((None, ma, nbw), lambda j, k: (j, 0, 0))] + [_HBM] * len(xshapes),
        out_shape=[jax.ShapeDtypeStruct((na * nb, ma, nbw), _F32)] + list(xshapes),
        scratch_shapes=list(sems),
        compiler_params=_cparams(("arbitrary", "arbitrary")), name=name,
    )(a, b, *xin)
    return res if exchange else res[0]


_DEV_DELTAS = tuple(dl for dl in itertools.product((0, 1), repeat=3) if any(dl))
_HBM = pl.BlockSpec(memory_space=pltpu.HBM)
_VMEM = pl.BlockSpec(memory_space=pltpu.VMEM)


def _pos():
    return lax.axis_index("x"), lax.axis_index("y"), lax.axis_index("c")


def _flip(v, delta):
    return 1 - v if delta else v


def _remote(src, dst, ssem, rsem, dev):
    return pltpu.make_async_remote_copy(src_ref=src, dst_ref=dst, send_sem=ssem, recv_sem=rsem,
                                        device_id=dev, device_id_type=_MESH)


def _rows(ref, idx, n):
    return ref.at[pl.ds(pl.multiple_of(idx * n, 8), n)]


def _ada_fwd(c8, w_ada_k, b_ada_k, shards):
    rows, d = c8.shape
    nk = w_ada_k.shape[1]
    n = len(shards)

    def body(*refs):
        c_ref, w_ref, b_ref = refs[:3]
        call_ref, mod_ref = refs[3 + n:5 + n]
        modloc, modrcv, s1, r1, s2, r2 = refs[5 + 2 * n:11 + 2 * n]
        gather_start, gather_finish = _gather_plan(refs[3:3 + n], refs[5 + n:5 + 2 * n], *refs[11 + 2 * n:14 + 2 * n],
                                                   fsem=refs[14 + 2 * n], frsem=refs[15 + 2 * n], bounce=refs[16 + 2 * n:])
        gather_start()
        xi, yi, ci = _pos()
        me, kme = 4 * xi + 2 * yi + ci, 2 * xi + yi
        call_ref[pl.ds(pl.multiple_of(me * rows, 8), rows), :] = c_ref[...]
        sends = []
        for p, (dx, dy, dc) in enumerate(_DEV_DELTAS):
            cp = _remote(c_ref, _rows(call_ref, me, rows), s1.at[p], r1.at[p], (_flip(xi, dx), _flip(yi, dy), _flip(ci, dc)))
            cp.start()
            sends.append(cp)
        for p, (dx, dy, dc) in enumerate(_DEV_DELTAS):
            src = 4 * _flip(xi, dx) + 2 * _flip(yi, dy) + _flip(ci, dc)
            _remote(c_ref, _rows(call_ref, src, rows), s1.at[p], r1.at[p], (xi, yi, ci)).wait_recv()
        for cp in sends:
            cp.wait_send()

        ca = call_ref[...]
        modloc[...] = _dot(_mx(ca * jax.nn.sigmoid(ca)), _mx(w_ref[...])) + b_ref[...]
        modrcv[kme] = modloc[pl.ds(pl.multiple_of(me * rows, 8), rows), :]
        sends = []
        for j, (dx, dy) in enumerate(_CHIP_DELTAS):
            tx, ty = _flip(xi, dx), _flip(yi, dy)
            cp = _remote(_rows(modloc, 4 * tx + 2 * ty + ci, rows), modrcv.at[kme], s2.at[j], r2.at[j], (tx, ty, ci))
            cp.start()
            sends.append(cp)
        for j, (dx, dy) in enumerate(_CHIP_DELTAS):
            ksrc = 2 * _flip(xi, dx) + _flip(yi, dy)
            _remote(_rows(modloc, me, rows), modrcv.at[ksrc], s2.at[j], r2.at[j], (xi, yi, ci)).wait_recv()
        for cp in sends:
            cp.wait_send()
        for j in range(4):
            mod_ref[:, j * nk:(j + 1) * nk] = modrcv[j]
        gather_finish()

    return pl.pallas_call(
        body, in_specs=[_VMEM, _VMEM, _VMEM] + [_HBM] * n, out_specs=[_VMEM, _VMEM] + [_HBM] * n,
        out_shape=[jax.ShapeDtypeStruct((8 * rows, d), _F32), jax.ShapeDtypeStruct((rows, 4 * nk), _F32)]
        + [jax.ShapeDtypeStruct((4,) + a.shape, a.dtype) for a in shards],
        scratch_shapes=[pltpu.VMEM((8 * rows, nk), _F32), pltpu.VMEM((4, rows, nk), _F32),
                        pltpu.SemaphoreType.DMA((7,)), pltpu.SemaphoreType.DMA((7,)),
                        pltpu.SemaphoreType.DMA((3,)), pltpu.SemaphoreType.DMA((3,))]
        + _gather_sems(n) + [pltpu.SemaphoreType.DMA((3, n)), pltpu.SemaphoreType.DMA((3, n))]
        + [pltpu.VMEM(a.shape, a.dtype) for a in shards],
        compiler_params=pltpu.CompilerParams(vmem_limit_bytes=_VMEM_LIMIT), name="ada_fwd",
    )(c8, w_ada_k, b_ada_k, *shards)


def _gather_sems(n):
    return [pltpu.SemaphoreType.DMA((3, n)), pltpu.SemaphoreType.DMA((3, n)), pltpu.SemaphoreType.DMA((n,))]


def _gather_plan(ins, outs, ssem, rsem, lsem, bounce=(), fsem=None, frsem=None):
    n = len(ins)
    xi, yi, ci = _pos()
    kme = 2 * xi + yi
    split = [fsem is not None and ins[a].shape[0] % 32 == 0 for a in range(n)]

    def half(ref, a, which):
        r2 = ins[a].shape[0] // 2
        return ref.at[pl.ds(pl.multiple_of(which * r2, 16), r2)]

    staged = [pltpu.make_async_copy(ins[a], bounce[a], lsem.at[a]) for a in range(len(bounce))]
    local = [pltpu.make_async_copy(bounce[a] if bounce else ins[a], outs[a].at[kme], lsem.at[a]) for a in range(n)]
    sends, recvs, forwards, handed = [], [], [], []
    for j, (dx, dy) in enumerate(_CHIP_DELTAS):
        tx, ty = _flip(xi, dx), _flip(yi, dy)
        for a in range(n):
            sems = (ssem.at[j, a], rsem.at[j, a])
            landing = outs[a].at[2 * tx + ty]
            if split[a]:
                sends.append(_remote(half(ins[a], a, ci), half(outs[a].at[kme], a, ci), *sems, (tx, ty, ci)))
                recvs.append(_remote(half(ins[a], a, ci), half(landing, a, ci), *sems, (xi, yi, ci)))
                fsems = (fsem.at[j, a], frsem.at[j, a])
                forwards.append(_remote(half(landing, a, ci), half(landing, a, ci), *fsems, (xi, yi, 1 - ci)))
                handed.append(_remote(half(ins[a], a, 1 - ci), half(landing, a, 1 - ci), *fsems, (xi, yi, ci)))
            else:
                sends.append(_remote(ins[a], outs[a].at[kme], *sems, (tx, ty, ci)))
                recvs.append(_remote(ins[a], landing, *sems, (xi, yi, ci)))
                forwards.append(None)

    def start():
        for cp in sends + staged:
            cp.start()
        for cp in staged:
            cp.wait()
        for cp in local:
            cp.start()

    def finish():
        for arrived, forward in zip(recvs, forwards):
            arrived.wait_recv()
            if forward is not None:
                forward.start()
        for cp in handed:
            cp.wait_recv()
        for cp in sends + [f for f in forwards if f is not None]:
            cp.wait_send()
        for cp in local:
            cp.wait()

    return start, finish


def _dev_gather_sems(n):
    return [pltpu.SemaphoreType.DMA((7, n)), pltpu.SemaphoreType.DMA((7, n)), pltpu.SemaphoreType.DMA((n,))]


def _dev_gather_plan(ins, outs, ssem, rsem, lsem):
    n = len(ins)
    xi, yi, ci = _pos()
    me = 4 * xi + 2 * yi + ci
    local = [pltpu.make_async_copy(ins[a], outs[a].at[me], lsem.at[a]) for a in range(n)]
    sends, recvs = [], []
    for p, (dx, dy, dc) in enumerate(_DEV_DELTAS):
        tx, ty, tc = _flip(xi, dx), _flip(yi, dy), _flip(ci, dc)
        for a in range(n):
            sends.append(_remote(ins[a], outs[a].at[me], ssem.at[p, a], rsem.at[p, a], (tx, ty, tc)))
            recvs.append(_remote(ins[a], outs[a].at[4 * tx + 2 * ty + tc], ssem.at[p, a], rsem.at[p, a], (xi, yi, ci)))

    def start():
        for cp in local + sends:
            cp.start()

    def finish():
        for cp in recvs:
            cp.wait_recv()
        for cp in sends:
            cp.wait_send()
        for cp in local:
            cp.wait()

    return start, finish


def _pair_sems(n):
    return [pltpu.SemaphoreType.DMA((n,)), pltpu.SemaphoreType.DMA((n,))]


def _pair_plan(ins, outs, ssem, rsem):
    xi, yi, ci = _pos()
    sends = []
    for a in range(len(ins)):
        r2 = ins[a].shape[1] // 2
        src = ins[a].at[:, pl.ds(pl.multiple_of((1 - ci) * r2, 8), r2), :]
        sends.append(_remote(src, outs[a], ssem.at[a], rsem.at[a], (xi, yi, 1 - ci)))

    def start():
        for cp in sends:
            cp.start()

    def finish():
        for cp in sends:
            cp.wait_recv()
        for cp in sends:
            cp.wait_send()

    return start, finish


def _chip_reduce_sems(n):
    return [pltpu.SemaphoreType.DMA((3, n)), pltpu.SemaphoreType.DMA((3, n))]


def _chip_reduce_plan(ins, outs, ssem, rsem):
    xi, yi, ci = _pos()
    sends = []
    for j, (dx, dy) in enumerate(_CHIP_DELTAS):
        tx, ty = _flip(xi, dx), _flip(yi, dy)
        sends += [_remote(ins[a].at[2 * tx + ty], outs[a].at[j], ssem.at[j, a], rsem.at[j, a], (tx, ty, ci))
                  for a in range(len(ins))]

    def start():
        for cp in sends:
            cp.start()

    def finish():
        for cp in sends:
            cp.wait_recv()
        for cp in sends:
            cp.wait_send()

    return start, finish


def _pair_exchange(gs, name):
    n = len(gs)

    def body(*refs):
        start, finish = _pair_plan(refs[:n], refs[n:2 * n], *refs[2 * n:])
        start()
        finish()

    return pl.pallas_call(
        body, in_specs=[_HBM] * n, out_specs=[_HBM] * n, out_shape=_pair_out_shapes(gs),
        scratch_shapes=_pair_sems(n), name=name,
    )(*gs)


def _pair_out_shapes(gs):
    return [jax.ShapeDtypeStruct((g.shape[0], g.shape[1] // 2, g.shape[2]), g.dtype) for g in gs]


def _row_tile(r):
    return max(t for t in range(8, min(r, 256) + 1, 8) if r % t == 0)


def _pair_add(g, r, cidx, name, wire_dtype=None):
    nk, r2, c = r.shape
    tr = _row_tile(r2)
    nt = r2 // tr

    def body(c_ref, g_ref, r_ref, *o_refs):
        s = g_ref[...] + r_ref[...]
        for o_ref in o_refs:
            o_ref[...] = s.astype(o_ref.dtype)

    out_spec = pl.BlockSpec((None, tr, c), lambda k, i, cr: (k, i, 0))
    dtypes = [_F32] + ([wire_dtype] if wire_dtype else [])
    res = pl.pallas_call(
        body, grid_spec=pltpu.PrefetchScalarGridSpec(
            num_scalar_prefetch=1, grid=(nk, nt),
            in_specs=[pl.BlockSpec((None, tr, c), lambda k, i, cr: (k, cr[0] * nt + i, 0)), out_spec],
            out_specs=[out_spec] * len(dtypes)),
        out_shape=[jax.ShapeDtypeStruct(r.shape, dt) for dt in dtypes],
        compiler_params=_cparams(("arbitrary", "arbitrary")), name=name,
    )(cidx, g, r)
    return res if wire_dtype else res[0]


def _chip_exchange(ss):
    n = len(ss)

    def body(*refs):
        start, finish = _chip_reduce_plan(refs[:n], refs[n:2 * n], *refs[2 * n:])
        start()
        finish()

    return pl.pallas_call(
        body, in_specs=[_HBM] * n, out_specs=[_HBM] * n,
        out_shape=[jax.ShapeDtypeStruct((3,) + s.shape[1:], s.dtype) for s in ss],
        scratch_shapes=_chip_reduce_sems(n), name="grad_chip_exchange",
    )(*ss)


def _chip_add(s, r, kidx, name):
    _, r2, c = r.shape
    tr = _row_tile(r2)

    def body(k_ref, s_ref, r_ref, o_ref):
        o_ref[...] = ((s_ref[...] + r_ref[0].astype(_F32)) + r_ref[1].astype(_F32)) + r_ref[2].astype(_F32)

    return pl.pallas_call(
        body, grid_spec=pltpu.PrefetchScalarGridSpec(
            num_scalar_prefetch=1, grid=(r2 // tr,),
            in_specs=[pl.BlockSpec((None, tr, c), lambda i, kr: (kr[0], i, 0)),
                      pl.BlockSpec((3, tr, c), lambda i, kr: (0, i, 0))],
            out_specs=pl.BlockSpec((tr, c), lambda i, kr: (i, 0))),
        out_shape=jax.ShapeDtypeStruct((r2, c), _F32),
        compiler_params=_cparams(("arbitrary",)), name=name,
    )(kidx, s, r)


def _pair_swap(hs, name):
    n = len(hs)

    def body(*refs):
        ins, outs = refs[:n], refs[n:2 * n]
        ssem, rsem = refs[2 * n:]
        xi, yi, ci = _pos()
        sends = [_remote(ins[a], outs[a], ssem.at[a], rsem.at[a], (xi, yi, 1 - ci)) for a in range(n)]
        for cp in sends:
            cp.start()
        for cp in sends:
            cp.wait_recv()
        for cp in sends:
            cp.wait_send()

    return pl.pallas_call(
        body, in_specs=[_HBM] * n, out_specs=[_HBM] * n,
        out_shape=[jax.ShapeDtypeStruct(h.shape, h.dtype) for h in hs],
        scratch_shapes=[pltpu.SemaphoreType.DMA((n,)), pltpu.SemaphoreType.DMA((n,))], name=name,
    )(*hs)


def _small_sum(everys):
    n = len(everys)

    def body(*refs):
        for all_ref, sum_ref in zip(refs[:n], refs[n:]):
            tot = all_ref[0]
            for dev in range(1, 8):
                tot = tot + all_ref[dev]
            sum_ref[...] = tot

    return pl.pallas_call(
        body, in_specs=[_VMEM] * n, out_specs=[_VMEM] * n,
        out_shape=[jax.ShapeDtypeStruct(e.shape[1:], _F32) for e in everys],
        compiler_params=pltpu.CompilerParams(vmem_limit_bytes=_VMEM_LIMIT), name="small_sum",
    )(*everys)


def _adamw(w, g, m, v):
    m = _ADAM_B1 * m + (1.0 - _ADAM_B1) * g
    v = _ADAM_B2 * v + (1.0 - _ADAM_B2) * (g * g)
    m_hat = m / (1.0 - _ADAM_B1 ** _ADAM_STEP)
    v_hat = v / (1.0 - _ADAM_B2 ** _ADAM_STEP)
    return -_ADAM_LR * (m_hat / (jnp.sqrt(v_hat) + _ADAM_EPS) + _ADAM_WD * w), m, v


def _adamw_big(w, g_mine, g_theirs, m, v, cidx, name, chip_sums=()):
    r, c = w.shape
    tr = _row_tile(r // 2)
    nt = r // 2 // tr
    nx = len(chip_sums)

    def body(c_ref, w_ref, gm_ref, gt_ref, m_ref, v_ref, g_ref, d_ref, mo_ref, vo_ref):
        g = jnp.where(pl.program_id(0) // nt == c_ref[0], gm_ref[...], gt_ref[...])
        g_ref[...] = g
        d_ref[...], mo_ref[...], vo_ref[...] = _adamw(w_ref[...], g, m_ref[...], v_ref[...])

    spec = pl.BlockSpec((tr, c), lambda i, cr: (i, 0))
    half = pl.BlockSpec((tr, c), lambda i, cr: (i % nt, 0))
    return pl.pallas_call(
        _fused_exchange(body, 6, 4, 0, nx, nx, _chip_reduce_plan, (2 * nt,)) if nx else body,
        grid_spec=pltpu.PrefetchScalarGridSpec(
            num_scalar_prefetch=1, grid=(2 * nt,), in_specs=[spec, half, half, spec, spec] + [_HBM] * nx,
            out_specs=[spec] * 4 + [_HBM] * nx, scratch_shapes=_chip_reduce_sems(nx) if nx else []),
        out_shape=[jax.ShapeDtypeStruct((r, c), _F32)] * 4
        + [jax.ShapeDtypeStruct((3,) + t.shape[1:], t.dtype) for t in chip_sums],
        compiler_params=_cparams(("arbitrary",)), name=name,
    )(cidx, w, g_mine, g_theirs, m, v, *chip_sums)


def _adamw_small(ws, gs, ms, vs):
    n = len(ws)
    summed = [i for i in range(n) if gs[i].shape != ws[i].shape]

    def body(*refs):
        w_r, g_r, m_r, v_r = (refs[i * n:(i + 1) * n] for i in range(4))
        outs = refs[4 * n:]
        for i in range(n):
            g = g_r[i][...]
            if i in summed:
                g = _rowsum(g)
                outs[3 * n + summed.index(i)][...] = g
            outs[i][...], outs[n + i][...], outs[2 * n + i][...] = _adamw(w_r[i][...], g, m_r[i][...], v_r[i][...])

    shapes = [jax.ShapeDtypeStruct(w.shape, _F32) for w in ws]
    res = pl.pallas_call(
        body, in_specs=[_VMEM] * (4 * n), out_specs=[_VMEM] * (3 * n + len(summed)),
        out_shape=shapes * 3 + [shapes[i] for i in summed],
        compiler_params=pltpu.CompilerParams(vmem_limit_bytes=_VMEM_LIMIT), name="adamw_small",
    )(*ws, *gs, *ms, *vs)
    gs = list(gs)
    for pos, i in enumerate(summed):
        gs[i] = res[3 * n + pos]
    return gs, res[:n], res[n:2 * n], res[2 * n:3 * n]


def _ada_bwd(c_all, dmod_k, w, m, v, chip_sums=()):
    d, nk = w.shape
    tn = 512 if nk % 512 == 0 else nk
    nx = len(chip_sums)

    def body(c_ref, dm_ref, w_ref, m_ref, v_ref, g_ref, d_ref, mo_ref, vo_ref):
        ca = c_ref[...]
        g = _dot_tn(_mx(ca * jax.nn.sigmoid(ca)), _mx(dm_ref[...]))
        g_ref[...] = g
        d_ref[...], mo_ref[...], vo_ref[...] = _adamw(w_ref[...], g, m_ref[...], v_ref[...])

    col = pl.BlockSpec((d, tn), lambda j: (0, j))
    return pl.pallas_call(
        _fused_exchange(body, 5, 4, 0, nx, nx, _chip_reduce_plan, (nk // tn,)) if nx else body, grid=(nk // tn,),
        in_specs=[pl.BlockSpec(c_all.shape, lambda j: (0, 0)), pl.BlockSpec((c_all.shape[0], tn), lambda j: (0, j)),
                  col, col, col] + [_HBM] * nx,
        out_specs=[col] * 4 + [_HBM] * nx,
        out_shape=[jax.ShapeDtypeStruct((d, nk), _F32)] * 4
        + [jax.ShapeDtypeStruct((3,) + t.shape[1:], t.dtype) for t in chip_sums],
        scratch_shapes=_chip_reduce_sems(nx) if nx else [],
        compiler_params=_cparams(("arbitrary",)), name="ada_bwd",
    )(c_all, dmod_k, w, m, v, *chip_sums)


def _block_diag(wh):
    hn, dh, _ = wh.shape
    eye = jnp.eye(hn, dtype=wh.dtype)
    return (eye[:, None, :, None] * wh[:, :, None, :]).reshape(hn * dh, hn * dh)


_WEIGHTS = ('w_ada', 'b_ada', 'w_in', 'lru_conv_w', 'lru_conv_b', 'lru_w_r', 'lru_b_r', 'lru_w_i', 'lru_b_i', 'lru_lambda',
            'conv_w', 'conv_b', 'conv_norm_g', 'conv_norm_b', 'w_out', 'ln1_g', 'ln1_b', 'ffn_w_up', 'ffn_conv_w',
            'ffn_conv_b', 'ffn_w_down', 'ln2_g', 'ln2_b')
_BIG = ('w_in', 'w_out', 'ffn_w_up', 'ffn_w_down')


def kernel(x, c, w_ada, b_ada, w_in, lru_conv_w, lru_conv_b, lru_w_r, lru_b_r, lru_w_i, lru_b_i, lru_lambda, conv_w, conv_b, conv_norm_g, conv_norm_b, w_out, ln1_g, ln1_b, ffn_w_up, ffn_conv_w, ffn_conv_b, ffn_w_down, ln2_g, ln2_b, loss_target, m_w_ada, m_b_ada, m_w_in, m_lru_conv_w, m_lru_conv_b, m_lru_w_r, m_lru_b_r, m_lru_w_i, m_lru_b_i, m_lru_lambda, m_conv_w, m_conv_b, m_conv_norm_g, m_conv_norm_b, m_w_out, m_ln1_g, m_ln1_b, m_ffn_w_up, m_ffn_conv_w, m_ffn_conv_b, m_ffn_w_down, m_ln2_g, m_ln2_b, v_w_ada, v_b_ada, v_w_in, v_lru_conv_w, v_lru_conv_b, v_lru_w_r, v_lru_b_r, v_lru_w_i, v_lru_b_i, v_lru_lambda, v_conv_w, v_conv_b, v_conv_norm_g, v_conv_norm_b, v_w_out, v_ln1_g, v_ln1_b, v_ffn_w_up, v_ffn_conv_w, v_ffn_conv_b, v_ffn_w_down, v_ln2_g, v_ln2_b):
    given = dict(locals())
    wt = {n: given[n] for n in _WEIGHTS}
    mo = {n: given["m_" + n] for n in _WEIGHTS}
    vo = {n: given["v_" + n] for n in _WEIGHTS}
    bl, s_len, d = x.shape
    wd = d // 2
    tokens = bl * s_len
    xi, yi, ci = _pos()
    kme = 2 * xi + yi
    kidx = jnp.reshape(kme, (1,)).astype(jnp.int32)
    cidx = jnp.reshape(ci, (1,)).astype(jnp.int32)

    nk = w_ada.shape[2]
    c8 = jnp.pad(c, ((0, 8 - bl), (0, 0)))
    c_all, mod8, win, wout_s, lcw_s, cw_s, fcw_s = _ada_fwd(
        c8, w_ada[0], lax.dynamic_slice(b_ada, (0, kme * nk), (1, nk)),
        [_mx(w_in[0]), _mx(w_out[0]), lru_conv_w[0], conv_w[0], ffn_conv_w[0]])
    mod3 = mod8[:bl].reshape(bl, 1, 6 * d)
    wout = wout_s.reshape(d, d)
    f = 4 * ffn_w_down.shape[1]
    unshard = lambda t: jnp.transpose(t, (1, 0, 2)).reshape(t.shape[1], -1)
    lcw, cw, fcw = unshard(lcw_s), unshard(cw_s), unshard(fcw_s)
    wr_bd, wi_bd = _mx(_block_diag(lru_w_r[0])), _mx(_block_diag(lru_w_i[0]))
    seg = _block_diag(jnp.ones((_N_HEADS, wd // _N_HEADS, wd // _N_HEADS), jnp.bfloat16))
    mixer_small = (lcw, lru_conv_b, wr_bd, wi_bd, lru_b_r, lru_b_i, lru_lambda, cw, conv_b, conv_norm_g, conv_norm_b, seg, wout, ln1_g)

    proj, h, mix, x1, u1, y, vbc, lru, wup, wdn_s = _mix_fwd(x, mod3, win, *mixer_small, ln1_b, [_mx(ffn_w_up[0]), _mx(ffn_w_down[0])])
    wdn = wdn_s.reshape(f, d)
    u2, hh, fact, gc_all, dz2, loss_acc, dln2, dgt2 = _ffn_fwd(x1, mod3, wup, fcw, ffn_conv_b, wdn, ln2_g, ln2_b, loss_target)
    dx1, dy2, dh, dfc, dmod2 = _ffn_bwd(dz2, x1, hh, gc_all, mod3, wup, wdn, fcw, ffn_conv_b)

    flat = lambda t: t.reshape(tokens, t.shape[-1])
    fc = wup.shape[2]
    g_up = _wgrad(flat(u2), flat(dh), d, fc, 1, 4, 0, 0, "wgrad_up")
    g_dn, r_up = _wgrad(flat(fact), flat(dy2), fc, d, f // fc, 1, 0, 0, "wgrad_down",
                        exchange=([g_up], _pair_out_shapes([g_up]), _pair_plan, _pair_sems(1)))
    g_dn = g_dn.reshape(4, f // 4, d)
    r_dn, = _pair_exchange([g_dn], "grad_pair_exchange_ffn_w_down")
    ffn_sum = [_pair_add(g, r, cidx, "grad_pair_add_" + n) for g, r, n in zip([g_up, g_dn], [r_up, r_dn], _BIG[2:])]
    grad_x, dproj, dmix, xcg, vecw, dlcw, dcw, dln1, dmod1, *ffn_recv = _mix_bwd(
        dx1, x, mix, proj, h, vbc, lru, mod3, win, *mixer_small, ffn_sum)
    g_ri = _wgrad(flat(xcg), flat(xcg), wd, wd, 1, 2, 0, 1, "wgrad_gates")
    dh_ = wd // _N_HEADS
    on_diagonal = jnp.eye(_N_HEADS, dtype=_F32)[None, :, None, :, None]
    g_ri = jnp.sum(g_ri.reshape(2, _N_HEADS, dh_, _N_HEADS, dh_) * on_diagonal, axis=3)

    dmod = jnp.concatenate([dmod1.reshape(bl, 3 * d), dmod2.reshape(bl, 2 * d), dgt2.reshape(bl, d)], axis=1)
    pieces = [vecw, dlcw, dcw, jnp.concatenate([dln1, dln2], axis=0), dfc, g_ri, loss_acc[:, 0:128],
              jnp.pad(dmod, ((0, 8 - bl), (0, 0)))]
    g_out = _wgrad(flat(y), flat(dmix), d, d, 1, 1, 0, 0, "wgrad_out").reshape(4, d // 4, d)
    g_in, *everys = _wgrad(flat(u1), flat(dproj), d, wd, 1, 4, 0, 0, "wgrad_in", exchange=(
        pieces, [jax.ShapeDtypeStruct((8,) + p.shape, _F32) for p in pieces], _dev_gather_plan,
        _dev_gather_sems(len(pieces))))
    mix_sum, mix_wire = zip(*[_pair_add(g, r, cidx, "grad_pair_add_" + n, jnp.bfloat16) for g, r, n in zip(
        [g_in, g_out], _pair_exchange([g_in, g_out], "grad_pair_exchange_w_in"), _BIG[:2])])
    chip_sum, recv = list(mix_sum) + ffn_sum, list(_chip_exchange(mix_wire)) + list(ffn_recv)
    half = [_chip_add(s, r, kidx, "grad_chip_add_" + n) for s, r, n in zip(chip_sum, recv, _BIG)]
    grads, deltas, new_m, new_v = {}, {}, {}, {}
    for n, mine, theirs in zip(_BIG, half, _pair_swap(half, "grad_pair_swap")):
        g, dl, mm, vv = _adamw_big(wt[n][0], mine, theirs, mo[n][0], vo[n][0], cidx, "adamw_" + n)
        grads[n], deltas[n], new_m[n], new_v[n] = g[None], dl[None], mm[None], vv[None]

    vecw, dlcw, dcw, dln, dfc, g_ri, loss_sum, dmod_sum = _small_sum(everys)
    loss = 0.5 * loss_sum[0, 0] / d
    dmod_all = everys[-1].reshape(64, 6 * d)

    g_ada, dl, mm, vv = _ada_bwd(c_all, lax.dynamic_slice(dmod_all, (0, kme * nk), (64, nk)), w_ada[0], m_w_ada[0], v_w_ada[0])
    grads['w_ada'], deltas['w_ada'], new_m['w_ada'], new_v['w_ada'] = g_ada[None], dl[None], mm[None], vv[None]

    shard = lambda t, width: lax.dynamic_slice(t, (0, kme * width), (t.shape[0], width))
    small = {
        'b_ada': dmod_sum, 'lru_conv_w': shard(dlcw, wd // 4), 'lru_conv_b': vecw[0:1], 'lru_w_r': g_ri[0], 'lru_b_r': vecw[1:2],
        'lru_w_i': g_ri[1], 'lru_b_i': vecw[2:3], 'lru_lambda': vecw[3:4], 'conv_w': shard(dcw, wd // 4), 'conv_b': vecw[4:5],
        'conv_norm_g': vecw[5:6], 'conv_norm_b': vecw[6:7], 'ln1_g': dln[0:1], 'ln1_b': dln[1:2],
        'ffn_conv_w': shard(dfc[0:3], f // 4), 'ffn_conv_b': dfc[3:4], 'ln2_g': dln[2:3], 'ln2_b': dln[3:4]}
    names = list(small)
    gs = [small[n] if n == 'b_ada' else small[n].reshape(wt[n].shape) for n in names]
    gs, dls, mms, vvs = _adamw_small([wt[n] for n in names], gs, [mo[n] for n in names], [vo[n] for n in names])
    for n, g, dl, mm, vv in zip(names, gs, dls, mms, vvs):
        grads[n], deltas[n], new_m[n], new_v[n] = g, dl, mm, vv

    return (loss, grad_x, *[grads[n] for n in _WEIGHTS], *[deltas[n] for n in _WEIGHTS],
            *[new_m[n] for n in _WEIGHTS], *[new_v[n] for n in _WEIGHTS])
```

```python
import functools
import itertools
import math

import jax
import jax.numpy as jnp
from jax import lax
from jax.experimental import pallas as pl
from jax.experimental.pallas import tpu as pltpu

_MXU_DT = jnp.bfloat16
_F32 = jnp.float32
_VMEM_LIMIT = 56 * 1024 * 1024
_TT_MIX = 256
_TT_MIX_FWD = 512
_TT_FFN = 256
_TK_WGRAD = 2048
_HALO = 32

_LRU_C = 8.0
_LN_EPS = 1e-5
_N_HEADS = 8
_DEPTH = 1
_ALPHA = (2 * _DEPTH) ** 0.25
_ADAM_LR, _ADAM_B1, _ADAM_B2, _ADAM_EPS, _ADAM_WD, _ADAM_STEP = 0.001, 0.9, 0.999, 1e-08, 0.01, 10

_MESH = pl.DeviceIdType.MESH
_CHIP_DELTAS = ((1, 0), (0, 1), (1, 1))


def _cparams(sem):
    return pltpu.CompilerParams(dimension_semantics=sem, vmem_limit_bytes=_VMEM_LIMIT)


def _resident(shape):
    nd = len(shape)
    return pl.BlockSpec(shape, lambda *_: (0,) * nd, pipeline_mode=pl.Buffered(1))


def _dot(a, b):
    return jnp.dot(a, b, preferred_element_type=_F32)


def _dot_nt(a, b):
    return lax.dot_general(a, b, (((1,), (1,)), ((), ())), preferred_element_type=_F32)


def _dot_tn(a, b):
    return lax.dot_general(a, b, (((0,), (0,)), ((), ())), preferred_element_type=_F32)


def _mx(v):
    return v.astype(_MXU_DT)


def _expm1(v):
    series = v * (1.0 + v * (1.0 / 2 + v * (1.0 / 6 + v * (1.0 / 24 + v * (1.0 / 120)))))
    return jnp.where(jnp.abs(v) < 0.0625, series, jnp.exp(v) - 1.0)


def _softplus(z):
    e = jnp.exp(-jnp.abs(z))
    u = 1.0 + e
    log1p = jnp.where(u == 1.0, e, jnp.log(u) * e / jnp.where(u == 1.0, 1.0, u - 1.0))
    return jnp.maximum(z, 0.0) + log1p


_GELU_C = math.sqrt(2.0 / math.pi)


def _gelu_and_grad(v):
    t = jnp.tanh(_GELU_C * (v + 0.044715 * v * v * v))
    val = 0.5 * v * (1.0 + t)
    grad = 0.5 * (1.0 + t) + 0.5 * v * (1.0 - t * t) * _GELU_C * (1.0 + 3 * 0.044715 * v * v)
    return val, grad


def _seg_sum(v, seg, passes=3):
    hi = v.astype(jnp.bfloat16)
    r1 = v - hi.astype(_F32)
    mid = r1.astype(jnp.bfloat16)
    out = _dot(hi, seg) + _dot(mid, seg)
    if passes == 3:
        out = out + _dot((r1 - mid.astype(_F32)).astype(jnp.bfloat16), seg)
    return out


def _scan_fwd(a, u, h0):
    n = a.shape[0]
    row = lax.broadcasted_iota(jnp.int32, a.shape, 0)
    h, d = u, 1
    while d < n:
        keep = row >= d
        h = a * jnp.where(keep, pltpu.roll(h, d, 0), 0.0) + h
        a = a * jnp.where(keep, pltpu.roll(a, d, 0), 1.0)
        d *= 2
    return h + a * h0


def _scan_rev(c, g, g_end):
    n = c.shape[0]
    row = lax.broadcasted_iota(jnp.int32, c.shape, 0)
    d = 1
    while d < n:
        keep = row < n - d
        g = c * jnp.where(keep, pltpu.roll(g, n - d, 0), 0.0) + g
        c = c * jnp.where(keep, pltpu.roll(c, n - d, 0), 1.0)
        d *= 2
    return g + c * g_end


def _layer_norm_stats(z):
    mu = jnp.mean(z, axis=-1, keepdims=True)
    zc = z - mu
    var = jnp.mean(zc * zc, axis=-1, keepdims=True)
    rstd = lax.rsqrt(var + _LN_EPS)
    return zc * rstd, rstd


def _layer_norm_bwd(dn, n, rstd):
    return rstd * (dn - jnp.mean(dn, axis=-1, keepdims=True) - n * jnp.mean(dn * n, axis=-1, keepdims=True))


def _rowsum(v):
    return jnp.sum(v, axis=0, keepdims=True)


def _fused_exchange(body, n_in, n_out, n_scratch, n_xin, n_xout, plan, grid):
    def wrapped(*refs):
        o0 = n_in + n_xin
        s0 = o0 + n_out + n_xout
        start, finish = plan(refs[n_in:o0], refs[o0 + n_out:s0], *refs[s0 + n_scratch:])
        step = 0
        for axis, size in enumerate(grid):
            step = step * size + pl.program_id(axis)

        @pl.when(step == 0)
        def _():
            start()

        body(*refs[:n_in], *refs[o0:o0 + n_out], *refs[s0:s0 + n_scratch])

        @pl.when(step == math.prod(grid) - 1)
        def _():
            finish()

    return wrapped


def _lru_gates(xc, wr_ref, wi_ref, br_ref, bi_ref, lam_ref):
    xcb = _mx(xc)
    r = jax.nn.sigmoid(_dot(xcb, wr_ref[...]) + br_ref[...])
    i = jax.nn.sigmoid(_dot(xcb, wi_ref[...]) + bi_ref[...])
    sp = _softplus(-lam_ref[...])
    log_a = -_LRU_C * r * sp
    a = jnp.exp(log_a)
    mult = jnp.sqrt(-_expm1(2.0 * log_a))
    return r, i, sp, a, mult


def _conv_taps(ext_ref, w_ref, first, n_taps, tt):
    acc = w_ref[0:1, :] * ext_ref[pl.ds(first, tt), :]
    for k in range(1, n_taps):
        acc = acc + w_ref[k:k + 1, :] * ext_ref[pl.ds(first + k, tt), :]
    return acc


def _make_shifted(ext_ref, sh_ref):
    n = sh_ref.shape[1]
    for r in range(1, 8):
        sh_ref[r - 1] = ext_ref[pl.ds(r, n), :]


def _tap(ext_ref, sh_ref, off, tt):
    base = (off // 8) * 8
    if off % 8 == 0:
        return ext_ref[pl.ds(base, tt), :]
    return sh_ref[off % 8 - 1, pl.ds(base, tt), :]


def _conv_taps_shifted(ext_ref, sh_ref, w_ref, first, n_taps, tt):
    acc = w_ref[0:1, :] * _tap(ext_ref, sh_ref, first, tt)
    for k in range(1, n_taps):
        acc = acc + w_ref[k:k + 1, :] * _tap(ext_ref, sh_ref, first + k, tt)
    return acc


def _mix_fwd(x, mod3, win, lcw, lcb, wr_bd, wi_bd, b_r, b_i, lam, cw, cb, ng, nb, seg, wout, ln1g, ln1b, shards):
    bl, s_len, d = x.shape
    w = d // 2
    tt = min(_TT_MIX_FWD, s_len)
    ns = s_len // tt
    kc = cw.shape[0]

    def body(x_ref, mod_ref, win_ref, lcw_ref, lcb_ref, wr_ref, wi_ref, br_ref, bi_ref, lam_ref, cw_ref, cb_ref,
             ng_ref, nb_ref, seg_ref, wout_ref, g1_ref, b1_ref,
             proj_ref, h_ref, mix_ref, x1_ref, u1_ref, y_ref, vbc_ref, lru_ref, ext4, ext31, sh31, hcar):
        @pl.when(pl.program_id(1) == 0)
        def _():
            ext4[0:8, :] = jnp.zeros((8, w), _F32)
            ext31[0:_HALO, :] = jnp.zeros((_HALO, w), _F32)
            hcar[...] = jnp.zeros_like(hcar)

        xt = x_ref[...]
        sh1, sc1, gt1 = mod_ref[:, 0:d], mod_ref[:, d:2 * d], mod_ref[:, 2 * d:3 * d]
        u1 = _mx(xt * (1.0 + sc1) + sh1)
        u1_ref[...] = u1
        xa, ga, vb, gb = (_dot(u1, win_ref[k]) for k in range(4))
        proj_ref[:, 0:w] = xa
        proj_ref[:, w:2 * w] = ga
        proj_ref[:, 2 * w:3 * w] = vb
        proj_ref[:, 3 * w:4 * w] = gb

        ext4[8:8 + tt, :] = xa
        xc = lcb_ref[...] + _conv_taps(ext4, lcw_ref, 5, 4, tt)
        ext4[0:8, :] = xa[tt - 8:tt, :]
        r, i, sp, a, mult = _lru_gates(xc, wr_ref, wi_ref, br_ref, bi_ref, lam_ref)
        for k, val in enumerate((xc, r, i, a, mult)):
            lru_ref[:, k * w:(k + 1) * w] = val
        h = _scan_fwd(a, mult * (i * xc), hcar[0:1, :])
        hcar[0:1, :] = h[tt - 1:tt, :]
        h_ref[...] = h
        gelu, _ = _gelu_and_grad(ga)
        y_ref[:, 0:w] = _mx(gelu * h)

        vbg = vb * jax.nn.sigmoid(gb)
        ext31[_HALO:_HALO + tt, :] = vbg
        _make_shifted(ext31, sh31)
        vbc = cb_ref[...] + _conv_taps_shifted(ext31, sh31, cw_ref, _HALO - (kc - 1), kc, tt)
        vbc_ref[...] = vbc
        ext31[0:_HALO, :] = vbg[tt - _HALO:tt, :]
        inv = 1.0 / (w // _N_HEADS)
        zc = vbc - _seg_sum(vbc, seg_ref[...]) * inv
        n = zc * lax.rsqrt(_seg_sum(zc * zc, seg_ref[...]) * inv + _LN_EPS)
        pre = n * ng_ref[...] + nb_ref[...]
        y_ref[:, w:2 * w] = _mx(pre * jax.nn.sigmoid(pre))

        mix = _dot(y_ref[...], wout_ref[...])
        mix_ref[...] = mix
        n1, _ = _layer_norm_stats(_ALPHA * xt + (1.0 + gt1) * mix)
        x1_ref[...] = n1 * g1_ref[...] + b1_ref[...]

    tok = lambda c: pl.BlockSpec((None, tt, c), lambda b, s: (b, s, 0))
    smalls = [lcw, lcb, wr_bd, wi_bd, b_r, b_i, lam, cw, cb, ng, nb, seg, wout, ln1g, ln1b]
    nx = len(shards)
    return pl.pallas_call(
        _fused_exchange(body, 3 + len(smalls), 8, 4, nx, nx, _gather_plan, (bl, ns)), grid=(bl, ns),
        in_specs=[tok(d), pl.BlockSpec((None, 1, 6 * d), lambda b, s: (b, 0, 0)), _resident(win.shape)]
        + [_resident(t.shape) for t in smalls] + [_HBM] * nx,
        out_specs=[tok(4 * w), tok(w), tok(d), tok(d), tok(d), tok(d), tok(w), tok(5 * w)] + [_HBM] * nx,
        out_shape=[jax.ShapeDtypeStruct((bl, s_len, 4 * w), _F32), jax.ShapeDtypeStruct((bl, s_len, w), _F32),
                   jax.ShapeDtypeStruct((bl, s_len, d), _F32), jax.ShapeDtypeStruct((bl, s_len, d), _F32),
                   jax.ShapeDtypeStruct((bl, s_len, d), _MXU_DT), jax.ShapeDtypeStruct((bl, s_len, d), _MXU_DT),
                   jax.ShapeDtypeStruct((bl, s_len, w), _F32), jax.ShapeDtypeStruct((bl, s_len, 5 * w), _F32)]
        + [jax.ShapeDtypeStruct((4,) + t.shape, t.dtype) for t in shards],
        scratch_shapes=[pltpu.VMEM((tt + 8, w), _F32), pltpu.VMEM((tt + _HALO, w), _F32),
                        pltpu.VMEM((7, tt + _HALO - 8, w), _F32), pltpu.VMEM((8, w), _F32)] + _gather_sems(nx),
        compiler_params=_cparams(("arbitrary", "arbitrary")), name="mix_fwd",
    )(x, mod3, win, *smalls, *shards)


def _ffn_fwd(x1, mod3, wup, fcw, fcb, wdn, ln2g, ln2b, target):
    bl, s_len, d = x1.shape
    nch, _, fc = wup.shape
    nch //= 2
    f = nch * fc
    tt = min(_TT_FFN, s_len)
    ns = s_len // tt

    def body(x1_ref, mod_ref, wup_ref, fcw_ref, fcb_ref, wdn_ref, g2_ref, b2_ref, tgt_ref,
             u2_ref, hh_ref, f_ref, gc_ref, dz2_ref, loss_ref, dln2_ref, dgt2_ref, ext3):
        first_tile = pl.program_id(1) == 0

        @pl.when(first_tile)
        def _():
            ext3[:, 0:8, :] = jnp.zeros((nch, 8, fc), _F32)
            dgt2_ref[...] = jnp.zeros_like(dgt2_ref)

        @pl.when(first_tile & (pl.program_id(0) == 0))
        def _():
            loss_ref[...] = jnp.zeros_like(loss_ref)
            dln2_ref[...] = jnp.zeros_like(dln2_ref)

        x1t = x1_ref[...]
        sh2, sc2, gt2 = mod_ref[:, 3 * d:4 * d], mod_ref[:, 4 * d:5 * d], mod_ref[:, 5 * d:6 * d]
        u2 = _mx(x1t * (1.0 + sc2) + sh2)
        u2_ref[...] = u2
        y2 = jnp.zeros((tt, d), _F32)
        for j in range(nch):
            lanes = slice(j * fc, (j + 1) * fc)
            v = _dot(u2, wup_ref[j])
            g = _dot(u2, wup_ref[nch + j])
            hh_ref[:, lanes] = v.astype(hh_ref.dtype)
            hh_ref[:, f + j * fc:f + (j + 1) * fc] = g.astype(hh_ref.dtype)
            ext = ext3.at[j]
            ext[8:8 + tt, :] = g
            gc = fcb_ref[:, lanes] + sum(fcw_ref[k:k + 1, lanes] * ext[pl.ds(6 + k, tt), :] for k in range(3))
            gc_ref[:, lanes] = gc
            ext[0:8, :] = g[tt - 8:tt, :]
            fj = _mx(gc * jax.nn.sigmoid(gc) * v)
            f_ref[:, lanes] = fj
            y2 = y2 + _dot(fj, wdn_ref[lanes, :])

        n2, rstd = _layer_norm_stats(_ALPHA * x1t + (1.0 + gt2) * y2)
        err = n2 * g2_ref[...] + b2_ref[...] - tgt_ref[...]
        loss_ref[...] += jnp.sum(_rowsum(err * err), axis=1, keepdims=True)
        dout = err * (1.0 / d)
        dln2_ref[0:1, :] += _rowsum(dout * n2)
        dln2_ref[1:2, :] += _rowsum(dout)
        dz2 = _layer_norm_bwd(dout * g2_ref[...], n2, rstd)
        dz2_ref[...] = dz2
        dgt2_ref[...] += _rowsum(dz2 * y2)

    tok = lambda c: pl.BlockSpec((None, tt, c), lambda b, s: (b, s, 0))
    acc = lambda r: pl.BlockSpec((r, d), lambda b, s: (0, 0))
    smalls = [fcw, fcb, wdn, ln2g, ln2b]
    return pl.pallas_call(
        body, grid=(bl, ns),
        in_specs=[tok(d), pl.BlockSpec((None, 1, 6 * d), lambda b, s: (b, 0, 0)), _resident(wup.shape)]
        + [_resident(t.shape) for t in smalls] + [tok(d)],
        out_specs=[tok(d), tok(2 * f), tok(f), tok(f), tok(d), acc(1), acc(2), pl.BlockSpec((None, 1, d), lambda b, s: (b, 0, 0))],
        out_shape=[jax.ShapeDtypeStruct((bl, s_len, d), _MXU_DT), jax.ShapeDtypeStruct((bl, s_len, 2 * f), _F32),
                   jax.ShapeDtypeStruct((bl, s_len, f), _MXU_DT), jax.ShapeDtypeStruct((bl, s_len, f), _F32),
                   jax.ShapeDtypeStruct((bl, s_len, d), _F32), jax.ShapeDtypeStruct((1, d), _F32), jax.ShapeDtypeStruct((2, d), _F32),
                   jax.ShapeDtypeStruct((bl, 1, d), _F32)],
        scratch_shapes=[pltpu.VMEM((nch, tt + 8, fc), _F32)],
        compiler_params=_cparams(("arbitrary", "arbitrary")), name="ffn_fwd",
    )(x1, mod3, wup, *smalls, target)


def _ffn_bwd(dz2, x1, hh, gc_all, mod3, wup, wdn, fcw, fcb):
    bl, s_len, d = x1.shape
    nch, _, fc = wup.shape
    nch //= 2
    f = nch * fc
    tt = min(_TT_FFN, s_len)
    ns = s_len // tt

    def body(dz2_ref, x1_ref, hh_ref, gc_ref, mod_ref, wup_ref, wdn_ref, fcw_ref, fcb_ref,
             dx1_ref, dy2_ref, dh_ref, dfc_ref, dmod_ref, dext, dcar):
        @pl.when(pl.program_id(1) == 0)
        def _():
            dcar[...] = jnp.zeros_like(dcar)
            dmod_ref[...] = jnp.zeros_like(dmod_ref)

        @pl.when((pl.program_id(1) == 0) & (pl.program_id(0) == 0))
        def _():
            dfc_ref[...] = jnp.zeros_like(dfc_ref)

        sc2, gt2 = mod_ref[:, 4 * d:5 * d], mod_ref[:, 5 * d:6 * d]
        dz2t = dz2_ref[...]
        dy2 = _mx((1.0 + gt2) * dz2t)
        dy2_ref[...] = dy2
        du2 = jnp.zeros((tt, d), _F32)
        for j in range(nch):
            lanes = slice(j * fc, (j + 1) * fc)
            glanes = slice(f + j * fc, f + (j + 1) * fc)
            v = hh_ref[:, lanes].astype(_F32)
            g = hh_ref[:, glanes].astype(_F32)
            gc = gc_ref[:, lanes]
            sg = jax.nn.sigmoid(gc)
            df = _dot_nt(dy2, wdn_ref[lanes, :])
            dv = df * (gc * sg)
            dgc = df * v * (sg * (1.0 + gc * (1.0 - sg)))
            dfc_ref[3:4, lanes] += _rowsum(dgc)
            dext[0:tt, :] = dgc
            dext[tt:tt + 8, :] = dcar[j]
            dcar[j] = dgc[0:8, :]
            dg = jnp.zeros((tt, fc), _F32)
            for k in range(3):
                shifted = dext[pl.ds(2 - k, tt), :]
                dg = dg + fcw_ref[k:k + 1, lanes] * shifted
                dfc_ref[k:k + 1, lanes] += _rowsum(shifted * g)
            dvb, dgb = _mx(dv), _mx(dg)
            dh_ref[:, lanes] = dvb
            dh_ref[:, glanes] = dgb
            du2 = du2 + _dot_nt(dvb, wup_ref[j]) + _dot_nt(dgb, wup_ref[nch + j])

        dx1_ref[...] = _ALPHA * dz2t + du2 * (1.0 + sc2)
        dmod_ref[0:1, :] += _rowsum(du2)
        dmod_ref[1:2, :] += _rowsum(du2 * x1_ref[...])

    tok = lambda c: pl.BlockSpec((None, tt, c), lambda b, i: (b, ns - 1 - i, 0))
    return pl.pallas_call(
        body, grid=(bl, ns),
        in_specs=[tok(d), tok(d), tok(2 * f), tok(f), pl.BlockSpec((None, 1, 6 * d), lambda b, i: (b, 0, 0)),
                  _resident(wup.shape), _resident(wdn.shape), _resident(fcw.shape), _resident(fcb.shape)],
        out_specs=[tok(d), tok(d), tok(2 * f), pl.BlockSpec((4, f), lambda b, i: (0, 0)),
                   pl.BlockSpec((None, 2, d), lambda b, i: (b, 0, 0))],
        out_shape=[jax.ShapeDtypeStruct((bl, s_len, d), _F32), jax.ShapeDtypeStruct((bl, s_len, d), _MXU_DT),
                   jax.ShapeDtypeStruct((bl, s_len, 2 * f), _MXU_DT), jax.ShapeDtypeStruct((4, f), _F32),
                   jax.ShapeDtypeStruct((bl, 2, d), _F32)],
        scratch_shapes=[pltpu.VMEM((tt + 8, fc), _F32), pltpu.VMEM((nch, 8, fc), _F32)],
        compiler_params=_cparams(("arbitrary", "arbitrary")), name="ffn_bwd",
    )(dz2, x1, hh, gc_all, mod3, wup, wdn, fcw, fcb)


def _mix_bwd(dx1, x, mix, proj, h, vbc, lru, mod3, win, lcw, lcb, wr_bd, wi_bd, b_r, b_i, lam, cw, cb, ng, nb, seg, wout, ln1g, chip_sums):
    bl, s_len, d = x.shape
    w = d // 2
    tt = min(_TT_MIX, s_len)
    ns = s_len // tt
    kc = cw.shape[0]

    def body(dx1_ref, x_ref, mix_ref, proj_ref, phalo_ref, h_ref, hhalo_ref, vbc_ref, lru_ref, mod_ref, win_ref, lcw_ref, lcb_ref,
             wr_ref, wi_ref, br_ref, bi_ref, lam_ref, cw_ref, cb_ref, ng_ref, nb_ref, seg_ref, wout_ref, g1_ref,
             gx_ref, dproj_ref, dmix_ref, xcg_ref, vecw_ref, dlcw_ref, dcw_ref, dln1_ref, dmod_ref,
             ext4, ext31, dext4, dext31, sh31, dsh31, car4, car31, gcar):
        s = ns - 1 - pl.program_id(1)
        first = s == 0

        @pl.when(pl.program_id(1) == 0)
        def _():
            car4[...] = jnp.zeros_like(car4)
            car31[...] = jnp.zeros_like(car31)
            gcar[...] = jnp.zeros_like(gcar)
            dmod_ref[...] = jnp.zeros_like(dmod_ref)

        @pl.when((pl.program_id(1) == 0) & (pl.program_id(0) == 0))
        def _():
            for ref in (vecw_ref, dlcw_ref, dcw_ref, dln1_ref):
                ref[...] = jnp.zeros_like(ref)

        xt, mixt = x_ref[...], mix_ref[...]
        sh1, sc1, gt1 = mod_ref[:, 0:d], mod_ref[:, d:2 * d], mod_ref[:, 2 * d:3 * d]

        n1, rstd1 = _layer_norm_stats(_ALPHA * xt + (1.0 + gt1) * mixt)
        dx1t = dx1_ref[...]
        dln1_ref[0:1, :] += _rowsum(dx1t * n1)
        dln1_ref[1:2, :] += _rowsum(dx1t)
        dz1 = _layer_norm_bwd(dx1t * g1_ref[...], n1, rstd1)
        dmod_ref[2:3, :] += _rowsum(dz1 * mixt)
        dmix = _mx((1.0 + gt1) * dz1)
        dmix_ref[...] = dmix
        dya = _dot_nt(dmix, wout_ref[0:w, :])
        dyb = _dot_nt(dmix, wout_ref[w:2 * w, :])

        xa, ga = proj_ref[:, 0:w], proj_ref[:, w:2 * w]
        vb, gb = proj_ref[:, 2 * w:3 * w], proj_ref[:, 3 * w:4 * w]

        sgb = jax.nn.sigmoid(gb)
        vbg = vb * sgb
        hv, hg = phalo_ref[:, 2 * w:3 * w], phalo_ref[:, 3 * w:4 * w]
        ext31[0:_HALO, :] = jnp.where(first, 0.0, hv * jax.nn.sigmoid(hg))
        ext31[_HALO:_HALO + tt, :] = vbg
        _make_shifted(ext31, sh31)
        vbc = vbc_ref[...]
        inv = 1.0 / (w // _N_HEADS)
        zc = vbc - _seg_sum(vbc, seg_ref[...]) * inv
        rstd = lax.rsqrt(_seg_sum(zc * zc, seg_ref[...]) * inv + _LN_EPS)
        n = zc * rstd
        pre = n * ng_ref[...] + nb_ref[...]
        sgp = jax.nn.sigmoid(pre)
        dpre = dyb * (sgp * (1.0 + pre * (1.0 - sgp)))
        vecw_ref[5:6, :] += _rowsum(dpre * n)
        vecw_ref[6:7, :] += _rowsum(dpre)
        dn = dpre * ng_ref[...]
        dvbc = rstd * (dn - _seg_sum(dn, seg_ref[...], 2) * inv - n * (_seg_sum(dn * n, seg_ref[...], 2) * inv))
        vecw_ref[4:5, :] += _rowsum(dvbc)
        dext31[0:tt, :] = dvbc
        dext31[tt:tt + _HALO, :] = car31[...]
        car31[...] = dvbc[0:_HALO, :]
        _make_shifted(dext31, dsh31)
        dvbg = jnp.zeros((tt, w), _F32)
        for k in range(kc):
            dvbg = dvbg + cw_ref[k:k + 1, :] * _tap(dext31, dsh31, kc - 1 - k, tt)
            dcw_ref[k:k + 1, :] += _rowsum(dvbc * _tap(ext31, sh31, _HALO - (kc - 1) + k, tt))
        dproj_ref[:, 2 * w:3 * w] = _mx(dvbg * sgb)
        dproj_ref[:, 3 * w:4 * w] = _mx(dvbg * vb * (sgb * (1.0 - sgb)))

        ext4[0:8, :] = jnp.where(first, 0.0, phalo_ref[_HALO - 8:_HALO, 0:w])
        ext4[8:8 + tt, :] = xa
        xc, r, i, a, mult = (lru_ref[:, k * w:(k + 1) * w] for k in range(5))
        xcg_ref[:, 0:w] = _mx(xc)
        sp = _softplus(-lam_ref[...])
        ht = h_ref[...]
        row = lax.broadcasted_iota(jnp.int32, (tt, w), 0)
        h_before = jnp.where(first, 0.0, hhalo_ref[7:8, :])
        hprev = jnp.where(row == 0, h_before, pltpu.roll(ht, 1, 0))
        gelu, dgelu = _gelu_and_grad(ga)
        dproj_ref[:, w:2 * w] = _mx(dya * ht * dgelu)
        dh = dya * gelu
        coef = jnp.where(row == tt - 1, 1.0, pltpu.roll(a, tt - 1, 0))
        big_g = _scan_rev(coef, dh, gcar[0:1, :])
        gcar[0:1, :] = a[0:1, :] * big_g[0:1, :]
        da = big_g * hprev
        ixc = i * xc
        dlog_a = da * a - (big_g * ixc) * (a * a / mult)
        di = big_g * mult * xc
        dxc = big_g * mult * i
        vecw_ref[3:4, :] += _rowsum(dlog_a * r) * (_LRU_C * jax.nn.sigmoid(-lam_ref[...]))
        dgr_f = dlog_a * (-_LRU_C * sp) * (r * (1.0 - r))
        dgi_f = di * (i * (1.0 - i))
        vecw_ref[1:2, :] += _rowsum(dgr_f)
        vecw_ref[2:3, :] += _rowsum(dgi_f)
        dgr, dgi = _mx(dgr_f), _mx(dgi_f)
        xcg_ref[:, w:2 * w] = dgr
        xcg_ref[:, 2 * w:3 * w] = dgi
        dxc = dxc + _dot_nt(dgr, wr_ref[...]) + _dot_nt(dgi, wi_ref[...])
        vecw_ref[0:1, :] += _rowsum(dxc)
        dext4[0:tt, :] = dxc
        dext4[tt:tt + 8, :] = car4[...]
        car4[...] = dxc[0:8, :]
        dxa = jnp.zeros((tt, w), _F32)
        for k in range(4):
            dxa = dxa + lcw_ref[k:k + 1, :] * dext4[pl.ds(3 - k, tt), :]
            dlcw_ref[k:k + 1, :] += _rowsum(dxc * ext4[pl.ds(5 + k, tt), :])
        dproj_ref[:, 0:w] = _mx(dxa)

        du1 = sum(_dot_nt(dproj_ref[:, k * w:(k + 1) * w], win_ref[k]) for k in range(4))
        gx_ref[...] = _ALPHA * dz1 + du1 * (1.0 + sc1)
        dmod_ref[0:1, :] += _rowsum(du1)
        dmod_ref[1:2, :] += _rowsum(du1 * xt)

    tok = lambda c: pl.BlockSpec((None, tt, c), lambda b, i: (b, ns - 1 - i, 0))
    halo = lambda rows, c: pl.BlockSpec(
        (None, rows, c), lambda b, i: (b, jnp.maximum((ns - 1 - i) * (tt // rows) - 1, 0), 0))
    accw = lambda r, c: pl.BlockSpec((r, c), lambda b, i: (0, 0))
    smalls = [lcw, lcb, wr_bd, wi_bd, b_r, b_i, lam, cw, cb, ng, nb, seg, wout, ln1g]
    nx = len(chip_sums)
    return pl.pallas_call(
        _fused_exchange(body, 11 + len(smalls), 9, 9, nx, nx, _chip_reduce_plan, (bl, ns)), grid=(bl, ns),
        in_specs=[tok(d), tok(d), tok(d), tok(4 * w), halo(_HALO, 4 * w), tok(w), halo(8, w), tok(w), tok(5 * w),
                  pl.BlockSpec((None, 1, 6 * d), lambda b, i: (b, 0, 0)), _resident(win.shape)]
        + [_resident(t.shape) for t in smalls] + [_HBM] * nx,
        out_specs=[tok(d), tok(4 * w), tok(d), tok(3 * w), accw(8, w), accw(4, w), accw(kc, w), accw(2, d),
                   pl.BlockSpec((None, 3, d), lambda b, i: (b, 0, 0))] + [_HBM] * nx,
        out_shape=[jax.ShapeDtypeStruct((bl, s_len, d), _F32), jax.ShapeDtypeStruct((bl, s_len, 4 * w), _MXU_DT),
                   jax.ShapeDtypeStruct((bl, s_len, d), _MXU_DT), jax.ShapeDtypeStruct((bl, s_len, 3 * w), _MXU_DT),
                   jax.ShapeDtypeStruct((8, w), _F32), jax.ShapeDtypeStruct((4, w), _F32),
                   jax.ShapeDtypeStruct((kc, w), _F32), jax.ShapeDtypeStruct((2, d), _F32),
                   jax.ShapeDtypeStruct((bl, 3, d), _F32)]
        + [jax.ShapeDtypeStruct((3,) + t.shape[1:], t.dtype) for t in chip_sums],
        scratch_shapes=[pltpu.VMEM((tt + 8, w), _F32), pltpu.VMEM((tt + _HALO, w), _F32),
                        pltpu.VMEM((tt + 8, w), _F32), pltpu.VMEM((tt + _HALO, w), _F32),
                        pltpu.VMEM((7, tt + _HALO - 8, w), _F32), pltpu.VMEM((7, tt + _HALO - 8, w), _F32),
                        pltpu.VMEM((8, w), _F32), pltpu.VMEM((_HALO, w), _F32), pltpu.VMEM((8, w), _F32)]
        + _chip_reduce_sems(nx),
        compiler_params=_cparams(("arbitrary", "arbitrary")), name="mix_bwd",
    )(dx1, x, mix, proj, proj, h, h, vbc, lru, mod3, win, *smalls, *chip_sums)


def _wgrad(a, b, ma, nbw, na, nb, a_off, b_off, name, exchange=None):
    t = a.shape[0]
    tk = min(_TK_WGRAD, t)
    grid = (na * nb, t // tk)

    def body(a_ref, b_ref, o_ref):
        @pl.when(pl.program_id(1) == 0)
        def _():
            o_ref[...] = jnp.zeros_like(o_ref)
        o_ref[...] += _dot_tn(a_ref[...], b_ref[...])

    xin, xshapes, plan, sems = exchange if exchange else ([], [], None, [])
    nx = len(xin)
    res = pl.pallas_call(
        _fused_exchange(body, 2, 1, 0, nx, len(xshapes), plan, grid) if exchange else body, grid=grid,
        in_specs=[pl.BlockSpec((tk, ma), lambda j, k: (k, j // nb + a_off)),
                  pl.BlockSpec((tk, nbw), lambda j, k: (k, j % nb + b_off))] + [_HBM] * nx,
        out_specs=[pl.BlockSpec((None, ma, nbw), lambda j, k: (j, 0, 0))] + [_HBM] * len(xshapes),
        out_shape=[jax.ShapeDtypeStruct((na * nb, ma, nbw), _F32)] + list(xshapes),
        scratch_shapes=list(sems),
        compiler_params=_cparams(("arbitrary", "arbitrary")), name=name,
    )(a, b, *xin)
    return res if exchange else res[0]


_DEV_DELTAS = tuple(dl for dl in itertools.product((0, 1), repeat=3) if any(dl))
_HBM = pl.BlockSpec(memory_space=pltpu.HBM)
_VMEM = pl.BlockSpec(memory_space=pltpu.VMEM)


def _pos():
    return lax.axis_index("x"), lax.axis_index("y"), lax.axis_index("c")


def _flip(v, delta):
    return 1 - v if delta else v


def _remote(src, dst, ssem, rsem, dev):
    return pltpu.make_async_remote_copy(src_ref=src, dst_ref=dst, send_sem=ssem, recv_sem=rsem,
                                        device_id=dev, device_id_type=_MESH)


def _rows(ref, idx, n):
    return ref.at[pl.ds(pl.multiple_of(idx * n, 8), n)]


def _ada_fwd(c8, w_ada_k, b_ada_k, shards):
    rows, d = c8.shape
    nk = w_ada_k.shape[1]
    n = len(shards)

    def body(*refs):
        c_ref, w_ref, b_ref = refs[:3]
        call_ref, mod_ref = refs[3 + n:5 + n]
        modloc, modrcv, s1, r1, s2, r2 = refs[5 + 2 * n:11 + 2 * n]
        gather_start, gather_finish = _gather_plan(refs[3:3 + n], refs[5 + n:5 + 2 * n], *refs[11 + 2 * n:14 + 2 * n],
                                                   fsem=refs[14 + 2 * n], frsem=refs[15 + 2 * n], bounce=refs[16 + 2 * n:])
        gather_start()
        xi, yi, ci = _pos()
        me, kme = 4 * xi + 2 * yi + ci, 2 * xi + yi
        call_ref[pl.ds(pl.multiple_of(me * rows, 8), rows), :] = c_ref[...]
        sends = []
        for p, (dx, dy, dc) in enumerate(_DEV_DELTAS):
            cp = _remote(c_ref, _rows(call_ref, me, rows), s1.at[p], r1.at[p], (_flip(xi, dx), _flip(yi, dy), _flip(ci, dc)))
            cp.start()
            sends.append(cp)
        for p, (dx, dy, dc) in enumerate(_DEV_DELTAS):
            src = 4 * _flip(xi, dx) + 2 * _flip(yi, dy) + _flip(ci, dc)
            _remote(c_ref, _rows(call_ref, src, rows), s1.at[p], r1.at[p], (xi, yi, ci)).wait_recv()
        for cp in sends:
            cp.wait_send()

        ca = call_ref[...]
        modloc[...] = _dot(_mx(ca * jax.nn.sigmoid(ca)), _mx(w_ref[...])) + b_ref[...]
        modrcv[kme] = modloc[pl.ds(pl.multiple_of(me * rows, 8), rows), :]
        sends = []
        for j, (dx, dy) in enumerate(_CHIP_DELTAS):
            tx, ty = _flip(xi, dx), _flip(yi, dy)
            cp = _remote(_rows(modloc, 4 * tx + 2 * ty + ci, rows), modrcv.at[kme], s2.at[j], r2.at[j], (tx, ty, ci))
            cp.start()
            sends.append(cp)
        for j, (dx, dy) in enumerate(_CHIP_DELTAS):
            ksrc = 2 * _flip(xi, dx) + _flip(yi, dy)
            _remote(_rows(modloc, me, rows), modrcv.at[ksrc], s2.at[j], r2.at[j], (xi, yi, ci)).wait_recv()
        for cp in sends:
            cp.wait_send()
        for j in range(4):
            mod_ref[:, j * nk:(j + 1) * nk] = modrcv[j]
        gather_finish()

    return pl.pallas_call(
        body, in_specs=[_VMEM, _VMEM, _VMEM] + [_HBM] * n, out_specs=[_VMEM, _VMEM] + [_HBM] * n,
        out_shape=[jax.ShapeDtypeStruct((8 * rows, d), _F32), jax.ShapeDtypeStruct((rows, 4 * nk), _F32)]
        + [jax.ShapeDtypeStruct((4,) + a.shape, a.dtype) for a in shards],
        scratch_shapes=[pltpu.VMEM((8 * rows, nk), _F32), pltpu.VMEM((4, rows, nk), _F32),
                        pltpu.SemaphoreType.DMA((7,)), pltpu.SemaphoreType.DMA((7,)),
                        pltpu.SemaphoreType.DMA((3,)), pltpu.SemaphoreType.DMA((3,))]
        + _gather_sems(n) + [pltpu.SemaphoreType.DMA((3, n)), pltpu.SemaphoreType.DMA((3, n))]
        + [pltpu.VMEM(a.shape, a.dtype) for a in shards],
        compiler_params=pltpu.CompilerParams(vmem_limit_bytes=_VMEM_LIMIT), name="ada_fwd",
    )(c8, w_ada_k, b_ada_k, *shards)


def _gather_sems(n):
    return [pltpu.SemaphoreType.DMA((3, n)), pltpu.SemaphoreType.DMA((3, n)), pltpu.SemaphoreType.DMA((n,))]


def _gather_plan(ins, outs, ssem, rsem, lsem, bounce=(), fsem=None, frsem=None):
    n = len(ins)
    xi, yi, ci = _pos()
    kme = 2 * xi + yi
    split = [fsem is not None and ins[a].shape[0] % 32 == 0 for a in range(n)]

    def half(ref, a, which):
        r2 = ins[a].shape[0] // 2
        return ref.at[pl.ds(pl.multiple_of(which * r2, 16), r2)]

    staged = [pltpu.make_async_copy(ins[a], bounce[a], lsem.at[a]) for a in range(len(bounce))]
    local = [pltpu.make_async_copy(bounce[a] if bounce else ins[a], outs[a].at[kme], lsem.at[a]) for a in range(n)]
    sends, recvs, forwards, handed = [], [], [], []
    for j, (dx, dy) in enumerate(_CHIP_DELTAS):
        tx, ty = _flip(xi, dx), _flip(yi, dy)
        for a in range(n):
            sems = (ssem.at[j, a], rsem.at[j, a])
            landing = outs[a].at[2 * tx + ty]
            if split[a]:
                sends.append(_remote(half(ins[a], a, ci), half(outs[a].at[kme], a, ci), *sems, (tx, ty, ci)))
                recvs.append(_remote(half(ins[a], a, ci), half(landing, a, ci), *sems, (xi, yi, ci)))
                fsems = (fsem.at[j, a], frsem.at[j, a])
                forwards.append(_remote(half(landing, a, ci), half(landing, a, ci), *fsems, (xi, yi, 1 - ci)))
                handed.append(_remote(half(ins[a], a, 1 - ci), half(landing, a, 1 - ci), *fsems, (xi, yi, ci)))
            else:
                sends.append(_remote(ins[a], outs[a].at[kme], *sems, (tx, ty, ci)))
                recvs.append(_remote(ins[a], landing, *sems, (xi, yi, ci)))
                forwards.append(None)

    def start():
        for cp in sends + staged:
            cp.start()
        for cp in staged:
            cp.wait()
        for cp in local:
            cp.start()

    def finish():
        for arrived, forward in zip(recvs, forwards):
            arrived.wait_recv()
            if forward is not None:
                forward.start()
        for cp in handed:
            cp.wait_recv()
        for cp in sends + [f for f in forwards if f is not None]:
            cp.wait_send()
        for cp in local:
            cp.wait()

    return start, finish


def _dev_gather_sems(n):
    return [pltpu.SemaphoreType.DMA((7, n)), pltpu.SemaphoreType.DMA((7, n)), pltpu.SemaphoreType.DMA((n,))]


def _dev_gather_plan(ins, outs, ssem, rsem, lsem):
    n = len(ins)
    xi, yi, ci = _pos()
    me = 4 * xi + 2 * yi + ci
    local = [pltpu.make_async_copy(ins[a], outs[a].at[me], lsem.at[a]) for a in range(n)]
    sends, recvs = [], []
    for p, (dx, dy, dc) in enumerate(_DEV_DELTAS):
        tx, ty, tc = _flip(xi, dx), _flip(yi, dy), _flip(ci, dc)
        for a in range(n):
            sends.append(_remote(ins[a], outs[a].at[me], ssem.at[p, a], rsem.at[p, a], (tx, ty, tc)))
            recvs.append(_remote(ins[a], outs[a].at[4 * tx + 2 * ty + tc], ssem.at[p, a], rsem.at[p, a], (xi, yi, ci)))

    def start():
        for cp in local + sends:
            cp.start()

    def finish():
        for cp in recvs:
            cp.wait_recv()
        for cp in sends:
            cp.wait_send()
        for cp in local:
            cp.wait()

    return start, finish


def _pair_sems(n):
    return [pltpu.SemaphoreType.DMA((n,)), pltpu.SemaphoreType.DMA((n,))]


def _pair_plan(ins, outs, ssem, rsem):
    xi, yi, ci = _pos()
    sends = []
    for a in range(len(ins)):
        r2 = ins[a].shape[1] // 2
        src = ins[a].at[:, pl.ds(pl.multiple_of((1 - ci) * r2, 8), r2), :]
        sends.append(_remote(src, outs[a], ssem.at[a], rsem.at[a], (xi, yi, 1 - ci)))

    def start():
        for cp in sends:
            cp.start()

    def finish():
        for cp in sends:
            cp.wait_recv()
        for cp in sends:
            cp.wait_send()

    return start, finish


def _chip_reduce_sems(n):
    return [pltpu.SemaphoreType.DMA((3, n)), pltpu.SemaphoreType.DMA((3, n))]


def _chip_reduce_plan(ins, outs, ssem, rsem):
    xi, yi, ci = _pos()
    sends = []
    for j, (dx, dy) in enumerate(_CHIP_DELTAS):
        tx, ty = _flip(xi, dx), _flip(yi, dy)
        sends += [_remote(ins[a].at[2 * tx + ty], outs[a].at[j], ssem.at[j, a], rsem.at[j, a], (tx, ty, ci))
                  for a in range(len(ins))]

    def start():
        for cp in sends:
            cp.start()

    def finish():
        for cp in sends:
            cp.wait_recv()
        for cp in sends:
            cp.wait_send()

    return start, finish


def _pair_exchange(gs, name):
    n = len(gs)

    def body(*refs):
        start, finish = _pair_plan(refs[:n], refs[n:2 * n], *refs[2 * n:])
        start()
        finish()

    return pl.pallas_call(
        body, in_specs=[_HBM] * n, out_specs=[_HBM] * n, out_shape=_pair_out_shapes(gs),
        scratch_shapes=_pair_sems(n), name=name,
    )(*gs)


def _pair_out_shapes(gs):
    return [jax.ShapeDtypeStruct((g.shape[0], g.shape[1] // 2, g.shape[2]), g.dtype) for g in gs]


def _row_tile(r):
    return max(t for t in range(8, min(r, 256) + 1, 8) if r % t == 0)


def _pair_add(g, r, cidx, name, wire_dtype=None):
    nk, r2, c = r.shape
    tr = _row_tile(r2)
    nt = r2 // tr

    def body(c_ref, g_ref, r_ref, *o_refs):
        s = g_ref[...] + r_ref[...]
        for o_ref in o_refs:
            o_ref[...] = s.astype(o_ref.dtype)

    out_spec = pl.BlockSpec((None, tr, c), lambda k, i, cr: (k, i, 0))
    dtypes = [_F32] + ([wire_dtype] if wire_dtype else [])
    res = pl.pallas_call(
        body, grid_spec=pltpu.PrefetchScalarGridSpec(
            num_scalar_prefetch=1, grid=(nk, nt),
            in_specs=[pl.BlockSpec((None, tr, c), lambda k, i, cr: (k, cr[0] * nt + i, 0)), out_spec],
            out_specs=[out_spec] * len(dtypes)),
        out_shape=[jax.ShapeDtypeStruct(r.shape, dt) for dt in dtypes],
        compiler_params=_cparams(("arbitrary", "arbitrary")), name=name,
    )(cidx, g, r)
    return res if wire_dtype else res[0]


def _chip_exchange(ss):
    n = len(ss)

    def body(*refs):
        start, finish = _chip_reduce_plan(refs[:n], refs[n:2 * n], *refs[2 * n:])
        start()
        finish()

    return pl.pallas_call(
        body, in_specs=[_HBM] * n, out_specs=[_HBM] * n,
        out_shape=[jax.ShapeDtypeStruct((3,) + s.shape[1:], s.dtype) for s in ss],
        scratch_shapes=_chip_reduce_sems(n), name="grad_chip_exchange",
    )(*ss)


def _chip_add(s, r, kidx, name):
    _, r2, c = r.shape
    tr = _row_tile(r2)

    def body(k_ref, s_ref, r_ref, o_ref):
        o_ref[...] = ((s_ref[...] + r_ref[0].astype(_F32)) + r_ref[1].astype(_F32)) + r_ref[2].astype(_F32)

    return pl.pallas_call(
        body, grid_spec=pltpu.PrefetchScalarGridSpec(
            num_scalar_prefetch=1, grid=(r2 // tr,),
            in_specs=[pl.BlockSpec((None, tr, c), lambda i, kr: (kr[0], i, 0)),
                      pl.BlockSpec((3, tr, c), lambda i, kr: (0, i, 0))],
            out_specs=pl.BlockSpec((tr, c), lambda i, kr: (i, 0))),
        out_shape=jax.ShapeDtypeStruct((r2, c), _F32),
        compiler_params=_cparams(("arbitrary",)), name=name,
    )(kidx, s, r)


def _pair_swap(hs, name):
    n = len(hs)

    def body(*refs):
        ins, outs = refs[:n], refs[n:2 * n]
        ssem, rsem = refs[2 * n:]
        xi, yi, ci = _pos()
        sends = [_remote(ins[a], outs[a], ssem.at[a], rsem.at[a], (xi, yi, 1 - ci)) for a in range(n)]
        for cp in sends:
            cp.start()
        for cp in sends:
            cp.wait_recv()
        for cp in sends:
            cp.wait_send()

    return pl.pallas_call(
        body, in_specs=[_HBM] * n, out_specs=[_HBM] * n,
        out_shape=[jax.ShapeDtypeStruct(h.shape, h.dtype) for h in hs],
        scratch_shapes=[pltpu.SemaphoreType.DMA((n,)), pltpu.SemaphoreType.DMA((n,))], name=name,
    )(*hs)


def _small_sum(every):
    def body(all_ref, sum_ref):
        tot = all_ref[0]
        for dev in range(1, 8):
            tot = tot + all_ref[dev]
        sum_ref[...] = tot

    return pl.pallas_call(
        body, in_specs=[_VMEM], out_specs=_VMEM, out_shape=jax.ShapeDtypeStruct(every.shape[1:], _F32),
        compiler_params=pltpu.CompilerParams(vmem_limit_bytes=_VMEM_LIMIT), name="small_sum",
    )(every)


def _adamw(w, g, m, v):
    m = _ADAM_B1 * m + (1.0 - _ADAM_B1) * g
    v = _ADAM_B2 * v + (1.0 - _ADAM_B2) * (g * g)
    m_hat = m / (1.0 - _ADAM_B1 ** _ADAM_STEP)
    v_hat = v / (1.0 - _ADAM_B2 ** _ADAM_STEP)
    return -_ADAM_LR * (m_hat / (jnp.sqrt(v_hat) + _ADAM_EPS) + _ADAM_WD * w), m, v


def _adamw_big(w, g_mine, g_theirs, m, v, cidx, name, chip_sums=()):
    r, c = w.shape
    tr = _row_tile(r // 2)
    nt = r // 2 // tr
    nx = len(chip_sums)

    def body(c_ref, w_ref, gm_ref, gt_ref, m_ref, v_ref, g_ref, d_ref, mo_ref, vo_ref):
        g = jnp.where(pl.program_id(0) // nt == c_ref[0], gm_ref[...], gt_ref[...])
        g_ref[...] = g
        d_ref[...], mo_ref[...], vo_ref[...] = _adamw(w_ref[...], g, m_ref[...], v_ref[...])

    spec = pl.BlockSpec((tr, c), lambda i, cr: (i, 0))
    half = pl.BlockSpec((tr, c), lambda i, cr: (i % nt, 0))
    return pl.pallas_call(
        _fused_exchange(body, 6, 4, 0, nx, nx, _chip_reduce_plan, (2 * nt,)) if nx else body,
        grid_spec=pltpu.PrefetchScalarGridSpec(
            num_scalar_prefetch=1, grid=(2 * nt,), in_specs=[spec, half, half, spec, spec] + [_HBM] * nx,
            out_specs=[spec] * 4 + [_HBM] * nx, scratch_shapes=_chip_reduce_sems(nx) if nx else []),
        out_shape=[jax.ShapeDtypeStruct((r, c), _F32)] * 4
        + [jax.ShapeDtypeStruct((3,) + t.shape[1:], t.dtype) for t in chip_sums],
        compiler_params=_cparams(("arbitrary",)), name=name,
    )(cidx, w, g_mine, g_theirs, m, v, *chip_sums)


def _adamw_small(ws, gs, ms, vs):
    n = len(ws)
    summed = [i for i in range(n) if gs[i].shape != ws[i].shape]

    def body(*refs):
        w_r, g_r, m_r, v_r = (refs[i * n:(i + 1) * n] for i in range(4))
        outs = refs[4 * n:]
        for i in range(n):
            g = g_r[i][...]
            if i in summed:
                g = _rowsum(g)
                outs[3 * n + summed.index(i)][...] = g
            outs[i][...], outs[n + i][...], outs[2 * n + i][...] = _adamw(w_r[i][...], g, m_r[i][...], v_r[i][...])

    shapes = [jax.ShapeDtypeStruct(w.shape, _F32) for w in ws]
    res = pl.pallas_call(
        body, in_specs=[_VMEM] * (4 * n), out_specs=[_VMEM] * (3 * n + len(summed)),
        out_shape=shapes * 3 + [shapes[i] for i in summed],
        compiler_params=pltpu.CompilerParams(vmem_limit_bytes=_VMEM_LIMIT), name="adamw_small",
    )(*ws, *gs, *ms, *vs)
    gs = list(gs)
    for pos, i in enumerate(summed):
        gs[i] = res[3 * n + pos]
    return gs, res[:n], res[n:2 * n], res[2 * n:3 * n]


def _ada_bwd(c_all, dmod_k, w, m, v, chip_sums=()):
    d, nk = w.shape
    tn = 512 if nk % 512 == 0 else nk
    nx = len(chip_sums)

    def body(c_ref, dm_ref, w_ref, m_ref, v_ref, g_ref, d_ref, mo_ref, vo_ref):
        ca = c_ref[...]
        g = _dot_tn(_mx(ca * jax.nn.sigmoid(ca)), _mx(dm_ref[...]))
        g_ref[...] = g
        d_ref[...], mo_ref[...], vo_ref[...] = _adamw(w_ref[...], g, m_ref[...], v_ref[...])

    col = pl.BlockSpec((d, tn), lambda j: (0, j))
    return pl.pallas_call(
        _fused_exchange(body, 5, 4, 0, nx, nx, _chip_reduce_plan, (nk // tn,)) if nx else body, grid=(nk // tn,),
        in_specs=[pl.BlockSpec(c_all.shape, lambda j: (0, 0)), pl.BlockSpec((c_all.shape[0], tn), lambda j: (0, j)),
                  col, col, col] + [_HBM] * nx,
        out_specs=[col] * 4 + [_HBM] * nx,
        out_shape=[jax.ShapeDtypeStruct((d, nk), _F32)] * 4
        + [jax.ShapeDtypeStruct((3,) + t.shape[1:], t.dtype) for t in chip_sums],
        scratch_shapes=_chip_reduce_sems(nx) if nx else [],
        compiler_params=_cparams(("arbitrary",)), name="ada_bwd",
    )(c_all, dmod_k, w, m, v, *chip_sums)


def _block_diag(wh):
    hn, dh, _ = wh.shape
    eye = jnp.eye(hn, dtype=wh.dtype)
    return (eye[:, None, :, None] * wh[:, :, None, :]).reshape(hn * dh, hn * dh)


def _pack(pieces):
    out = []
    for p in pieces:
        flat = p.reshape(-1, 128)
        out.append(jnp.pad(flat, ((0, (-flat.shape[0]) % 8), (0, 0))))
    return jnp.concatenate(out, axis=0)


def _unpack(pack, shapes):
    out, off = [], 0
    for shp in shapes:
        rows = math.prod(shp) // 128
        out.append(pack[..., off:off + rows, :].reshape(pack.shape[:-2] + tuple(shp)))
        off += rows + (-rows) % 8
    return out


_WEIGHTS = ('w_ada', 'b_ada', 'w_in', 'lru_conv_w', 'lru_conv_b', 'lru_w_r', 'lru_b_r', 'lru_w_i', 'lru_b_i', 'lru_lambda',
            'conv_w', 'conv_b', 'conv_norm_g', 'conv_norm_b', 'w_out', 'ln1_g', 'ln1_b', 'ffn_w_up', 'ffn_conv_w',
            'ffn_conv_b', 'ffn_w_down', 'ln2_g', 'ln2_b')
_BIG = ('w_in', 'w_out', 'ffn_w_up', 'ffn_w_down')


def kernel(x, c, w_ada, b_ada, w_in, lru_conv_w, lru_conv_b, lru_w_r, lru_b_r, lru_w_i, lru_b_i, lru_lambda, conv_w, conv_b, conv_norm_g, conv_norm_b, w_out, ln1_g, ln1_b, ffn_w_up, ffn_conv_w, ffn_conv_b, ffn_w_down, ln2_g, ln2_b, loss_target, m_w_ada, m_b_ada, m_w_in, m_lru_conv_w, m_lru_conv_b, m_lru_w_r, m_lru_b_r, m_lru_w_i, m_lru_b_i, m_lru_lambda, m_conv_w, m_conv_b, m_conv_norm_g, m_conv_norm_b, m_w_out, m_ln1_g, m_ln1_b, m_ffn_w_up, m_ffn_conv_w, m_ffn_conv_b, m_ffn_w_down, m_ln2_g, m_ln2_b, v_w_ada, v_b_ada, v_w_in, v_lru_conv_w, v_lru_conv_b, v_lru_w_r, v_lru_b_r, v_lru_w_i, v_lru_b_i, v_lru_lambda, v_conv_w, v_conv_b, v_conv_norm_g, v_conv_norm_b, v_w_out, v_ln1_g, v_ln1_b, v_ffn_w_up, v_ffn_conv_w, v_ffn_conv_b, v_ffn_w_down, v_ln2_g, v_ln2_b):
    given = dict(locals())
    wt = {n: given[n] for n in _WEIGHTS}
    mo = {n: given["m_" + n] for n in _WEIGHTS}
    vo = {n: given["v_" + n] for n in _WEIGHTS}
    bl, s_len, d = x.shape
    wd = d // 2
    tokens = bl * s_len
    xi, yi, ci = _pos()
    kme = 2 * xi + yi
    kidx = jnp.reshape(kme, (1,)).astype(jnp.int32)
    cidx = jnp.reshape(ci, (1,)).astype(jnp.int32)

    nk = w_ada.shape[2]
    c8 = jnp.pad(c, ((0, 8 - bl), (0, 0)))
    c_all, mod8, win, wout_s, lcw_s, cw_s, fcw_s = _ada_fwd(
        c8, w_ada[0], lax.dynamic_slice(b_ada, (0, kme * nk), (1, nk)),
        [_mx(w_in[0]), _mx(w_out[0]), lru_conv_w[0], conv_w[0], ffn_conv_w[0]])
    mod3 = mod8[:bl].reshape(bl, 1, 6 * d)
    wout = wout_s.reshape(d, d)
    f = 4 * ffn_w_down.shape[1]
    unshard = lambda t: jnp.transpose(t, (1, 0, 2)).reshape(t.shape[1], -1)
    lcw, cw, fcw = unshard(lcw_s), unshard(cw_s), unshard(fcw_s)
    wr_bd, wi_bd = _mx(_block_diag(lru_w_r[0])), _mx(_block_diag(lru_w_i[0]))
    seg = _block_diag(jnp.ones((_N_HEADS, wd // _N_HEADS, wd // _N_HEADS), jnp.bfloat16))
    mixer_small = (lcw, lru_conv_b, wr_bd, wi_bd, lru_b_r, lru_b_i, lru_lambda, cw, conv_b, conv_norm_g, conv_norm_b, seg, wout, ln1_g)

    proj, h, mix, x1, u1, y, vbc, lru, wup, wdn_s = _mix_fwd(x, mod3, win, *mixer_small, ln1_b, [_mx(ffn_w_up[0]), _mx(ffn_w_down[0])])
    wdn = wdn_s.reshape(f, d)
    u2, hh, fact, gc_all, dz2, loss_acc, dln2, dgt2 = _ffn_fwd(x1, mod3, wup, fcw, ffn_conv_b, wdn, ln2_g, ln2_b, loss_target)
    dx1, dy2, dh, dfc, dmod2 = _ffn_bwd(dz2, x1, hh, gc_all, mod3, wup, wdn, fcw, ffn_conv_b)

    flat = lambda t: t.reshape(tokens, t.shape[-1])
    fc = wup.shape[2]
    g_up = _wgrad(flat(u2), flat(dh), d, fc, 1, 4, 0, 0, "wgrad_up")
    g_dn, r_up = _wgrad(flat(fact), flat(dy2), fc, d, f // fc, 1, 0, 0, "wgrad_down",
                        exchange=([g_up], _pair_out_shapes([g_up]), _pair_plan, _pair_sems(1)))
    g_dn = g_dn.reshape(4, f // 4, d)
    r_dn, = _pair_exchange([g_dn], "grad_pair_exchange_ffn_w_down")
    ffn_sum = [_pair_add(g, r, cidx, "grad_pair_add_" + n) for g, r, n in zip([g_up, g_dn], [r_up, r_dn], _BIG[2:])]
    grad_x, dproj, dmix, xcg, vecw, dlcw, dcw, dln1, dmod1, *ffn_recv = _mix_bwd(
        dx1, x, mix, proj, h, vbc, lru, mod3, win, *mixer_small, ffn_sum)
    g_ri = _wgrad(flat(xcg), flat(xcg), wd, wd, 1, 2, 0, 1, "wgrad_gates")
    dh_ = wd // _N_HEADS
    on_diagonal = jnp.eye(_N_HEADS, dtype=_F32)[None, :, None, :, None]
    g_ri = jnp.sum(g_ri.reshape(2, _N_HEADS, dh_, _N_HEADS, dh_) * on_diagonal, axis=3)

    dmod = jnp.concatenate([dmod1.reshape(bl, 3 * d), dmod2.reshape(bl, 2 * d), dgt2.reshape(bl, d)], axis=1)
    pieces = [vecw, dlcw, dcw, jnp.concatenate([dln1, dln2], axis=0), dfc, g_ri, loss_acc[:, 0:128],
              jnp.pad(dmod, ((0, 8 - bl), (0, 0)))]
    shapes = [p.shape for p in pieces]
    pack = _pack(pieces)
    g_out = _wgrad(flat(y), flat(dmix), d, d, 1, 1, 0, 0, "wgrad_out").reshape(4, d // 4, d)
    g_in, every = _wgrad(flat(u1), flat(dproj), d, wd, 1, 4, 0, 0, "wgrad_in", exchange=(
        [pack], [jax.ShapeDtypeStruct((8,) + pack.shape, _F32)], _dev_gather_plan, _dev_gather_sems(1)))
    mix_sum, mix_wire = zip(*[_pair_add(g, r, cidx, "grad_pair_add_" + n, jnp.bfloat16) for g, r, n in zip(
        [g_in, g_out], _pair_exchange([g_in, g_out], "grad_pair_exchange_w_in"), _BIG[:2])])
    chip_sum, recv = list(mix_sum) + ffn_sum, list(_chip_exchange(mix_wire)) + list(ffn_recv)
    half = [_chip_add(s, r, kidx, "grad_chip_add_" + n) for s, r, n in zip(chip_sum, recv, _BIG)]
    grads, deltas, new_m, new_v = {}, {}, {}, {}
    for n, mine, theirs in zip(_BIG, half, _pair_swap(half, "grad_pair_swap")):
        g, dl, mm, vv = _adamw_big(wt[n][0], mine, theirs, mo[n][0], vo[n][0], cidx, "adamw_" + n)
        grads[n], deltas[n], new_m[n], new_v[n] = g[None], dl[None], mm[None], vv[None]

    vecw, dlcw, dcw, dln, dfc, g_ri, loss_sum, dmod_sum = _unpack(_small_sum(every), shapes)
    loss = 0.5 * loss_sum[0, 0] / d
    dmod_all = _unpack(every, shapes)[-1].reshape(64, 6 * d)

    g_ada, dl, mm, vv = _ada_bwd(c_all, lax.dynamic_slice(dmod_all, (0, kme * nk), (64, nk)), w_ada[0], m_w_ada[0], v_w_ada[0])
    grads['w_ada'], deltas['w_ada'], new_m['w_ada'], new_v['w_ada'] = g_ada[None], dl[None], mm[None], vv[None]

    shard = lambda t, width: lax.dynamic_slice(t, (0, kme * width), (t.shape[0], width))
    small = {
        'b_ada': dmod_sum, 'lru_conv_w': shard(dlcw, wd // 4), 'lru_conv_b': vecw[0:1], 'lru_w_r': g_ri[0], 'lru_b_r': vecw[1:2],
        'lru_w_i': g_ri[1], 'lru_b_i': vecw[2:3], 'lru_lambda': vecw[3:4], 'conv_w': shard(dcw, wd // 4), 'conv_b': vecw[4:5],
        'conv_norm_g': vecw[5:6], 'conv_norm_b': vecw[6:7], 'ln1_g': dln[0:1], 'ln1_b': dln[1:2],
        'ffn_conv_w': shard(dfc[0:3], f // 4), 'ffn_conv_b': dfc[3:4], 'ln2_g': dln[2:3], 'ln2_b': dln[3:4]}
    names = list(small)
    gs = [small[n] if n == 'b_ada' else small[n].reshape(wt[n].shape) for n in names]
    gs, dls, mms, vvs = _adamw_small([wt[n] for n in names], gs, [mo[n] for n in names], [vo[n] for n in names])
    for n, g, dl, mm, vv in zip(names, gs, dls, mms, vvs):
        grads[n], deltas[n], new_m[n], new_v[n] = g, dl, mm, vv

    return (loss, grad_x, *[grads[n] for n in _WEIGHTS], *[deltas[n] for n in _WEIGHTS],
            *[new_m[n] for n in _WEIGHTS], *[new_v[n] for n in _WEIGHTS])
```

```python
import functools
import itertools
import math

import jax
import jax.numpy as jnp
from jax import lax
from jax.experimental import pallas as pl
from jax.experimental.pallas import tpu as pltpu

_MXU_DT = jnp.bfloat16
_F32 = jnp.float32
_VMEM_LIMIT = 56 * 1024 * 1024
_TT_MIX = 256
_TT_MIX_FWD = 512
_TT_FFN = 256
_TK_WGRAD = 2048
_HALO = 32

_LRU_C = 8.0
_LN_EPS = 1e-5
_N_HEADS = 8
_DEPTH = 1
_ALPHA = (2 * _DEPTH) ** 0.25
_ADAM_LR, _ADAM_B1, _ADAM_B2, _ADAM_EPS, _ADAM_WD, _ADAM_STEP = 0.001, 0.9, 0.999, 1e-08, 0.01, 10

_MESH = pl.DeviceIdType.MESH
_CHIP_DELTAS = ((1, 0), (0, 1), (1, 1))


def _cparams(sem):
    return pltpu.CompilerParams(dimension_semantics=sem, vmem_limit_bytes=_VMEM_LIMIT)


def _resident(shape):
    nd = len(shape)
    return pl.BlockSpec(shape, lambda *_: (0,) * nd, pipeline_mode=pl.Buffered(1))


def _dot(a, b):
    return jnp.dot(a, b, preferred_element_type=_F32)


def _dot_nt(a, b):
    return lax.dot_general(a, b, (((1,), (1,)), ((), ())), preferred_element_type=_F32)


def _dot_tn(a, b):
    return lax.dot_general(a, b, (((0,), (0,)), ((), ())), preferred_element_type=_F32)


def _mx(v):
    return v.astype(_MXU_DT)


def _expm1(v):
    series = v * (1.0 + v * (1.0 / 2 + v * (1.0 / 6 + v * (1.0 / 24 + v * (1.0 / 120)))))
    return jnp.where(jnp.abs(v) < 0.0625, series, jnp.exp(v) - 1.0)


def _softplus(z):
    e = jnp.exp(-jnp.abs(z))
    u = 1.0 + e
    log1p = jnp.where(u == 1.0, e, jnp.log(u) * e / jnp.where(u == 1.0, 1.0, u - 1.0))
    return jnp.maximum(z, 0.0) + log1p


_GELU_C = math.sqrt(2.0 / math.pi)


def _gelu_and_grad(v):
    t = jnp.tanh(_GELU_C * (v + 0.044715 * v * v * v))
    val = 0.5 * v * (1.0 + t)
    grad = 0.5 * (1.0 + t) + 0.5 * v * (1.0 - t * t) * _GELU_C * (1.0 + 3 * 0.044715 * v * v)
    return val, grad


def _seg_sum(v, seg, passes=3):
    hi = v.astype(jnp.bfloat16)
    r1 = v - hi.astype(_F32)
    mid = r1.astype(jnp.bfloat16)
    out = _dot(hi, seg) + _dot(mid, seg)
    if passes == 3:
        out = out + _dot((r1 - mid.astype(_F32)).astype(jnp.bfloat16), seg)
    return out


def _scan_fwd(a, u, h0):
    n = a.shape[0]
    row = lax.broadcasted_iota(jnp.int32, a.shape, 0)
    h, d = u, 1
    while d < n:
        keep = row >= d
        h = a * jnp.where(keep, pltpu.roll(h, d, 0), 0.0) + h
        a = a * jnp.where(keep, pltpu.roll(a, d, 0), 1.0)
        d *= 2
    return h + a * h0


def _scan_rev(c, g, g_end):
    n = c.shape[0]
    row = lax.broadcasted_iota(jnp.int32, c.shape, 0)
    d = 1
    while d < n:
        keep = row < n - d
        g = c * jnp.where(keep, pltpu.roll(g, n - d, 0), 0.0) + g
        c = c * jnp.where(keep, pltpu.roll(c, n - d, 0), 1.0)
        d *= 2
    return g + c * g_end


def _layer_norm_stats(z):
    mu = jnp.mean(z, axis=-1, keepdims=True)
    zc = z - mu
    var = jnp.mean(zc * zc, axis=-1, keepdims=True)
    rstd = lax.rsqrt(var + _LN_EPS)
    return zc * rstd, rstd


def _layer_norm_bwd(dn, n, rstd):
    return rstd * (dn - jnp.mean(dn, axis=-1, keepdims=True) - n * jnp.mean(dn * n, axis=-1, keepdims=True))


def _rowsum(v):
    return jnp.sum(v, axis=0, keepdims=True)


def _fused_exchange(body, n_in, n_out, n_scratch, n_xin, n_xout, plan, grid):
    def wrapped(*refs):
        o0 = n_in + n_xin
        s0 = o0 + n_out + n_xout
        start, finish = plan(refs[n_in:o0], refs[o0 + n_out:s0], *refs[s0 + n_scratch:])
        step = 0
        for axis, size in enumerate(grid):
            step = step * size + pl.program_id(axis)

        @pl.when(step == 0)
        def _():
            start()

        body(*refs[:n_in], *refs[o0:o0 + n_out], *refs[s0:s0 + n_scratch])

        @pl.when(step == math.prod(grid) - 1)
        def _():
            finish()

    return wrapped


def _lru_gates(xc, wr_ref, wi_ref, br_ref, bi_ref, lam_ref):
    xcb = _mx(xc)
    r = jax.nn.sigmoid(_dot(xcb, wr_ref[...]) + br_ref[...])
    i = jax.nn.sigmoid(_dot(xcb, wi_ref[...]) + bi_ref[...])
    sp = _softplus(-lam_ref[...])
    log_a = -_LRU_C * r * sp
    a = jnp.exp(log_a)
    mult = jnp.sqrt(-_expm1(2.0 * log_a))
    return r, i, sp, a, mult


def _conv_taps(ext_ref, w_ref, first, n_taps, tt):
    acc = w_ref[0:1, :] * ext_ref[pl.ds(first, tt), :]
    for k in range(1, n_taps):
        acc = acc + w_ref[k:k + 1, :] * ext_ref[pl.ds(first + k, tt), :]
    return acc


def _make_shifted(ext_ref, sh_ref):
    n = sh_ref.shape[1]
    for r in range(1, 8):
        sh_ref[r - 1] = ext_ref[pl.ds(r, n), :]


def _tap(ext_ref, sh_ref, off, tt):
    base = (off // 8) * 8
    if off % 8 == 0:
        return ext_ref[pl.ds(base, tt), :]
    return sh_ref[off % 8 - 1, pl.ds(base, tt), :]


def _conv_taps_shifted(ext_ref, sh_ref, w_ref, first, n_taps, tt):
    acc = w_ref[0:1, :] * _tap(ext_ref, sh_ref, first, tt)
    for k in range(1, n_taps):
        acc = acc + w_ref[k:k + 1, :] * _tap(ext_ref, sh_ref, first + k, tt)
    return acc


def _mix_fwd(x, mod3, win, lcw, lcb, wr_bd, wi_bd, b_r, b_i, lam, cw, cb, ng, nb, seg, wout, ln1g, ln1b, shards):
    bl, s_len, d = x.shape
    w = d // 2
    tt = min(_TT_MIX_FWD, s_len)
    ns = s_len // tt
    kc = cw.shape[0]

    def body(x_ref, mod_ref, win_ref, lcw_ref, lcb_ref, wr_ref, wi_ref, br_ref, bi_ref, lam_ref, cw_ref, cb_ref,
             ng_ref, nb_ref, seg_ref, wout_ref, g1_ref, b1_ref,
             proj_ref, h_ref, mix_ref, x1_ref, u1_ref, y_ref, vbc_ref, lru_ref, ext4, ext31, sh31, hcar):
        @pl.when(pl.program_id(1) == 0)
        def _():
            ext4[0:8, :] = jnp.zeros((8, w), _F32)
            ext31[0:_HALO, :] = jnp.zeros((_HALO, w), _F32)
            hcar[...] = jnp.zeros_like(hcar)

        xt = x_ref[...]
        sh1, sc1, gt1 = mod_ref[:, 0:d], mod_ref[:, d:2 * d], mod_ref[:, 2 * d:3 * d]
        u1 = _mx(xt * (1.0 + sc1) + sh1)
        u1_ref[...] = u1
        xa, ga, vb, gb = (_dot(u1, win_ref[k]) for k in range(4))
        proj_ref[:, 0:w] = xa
        proj_ref[:, w:2 * w] = ga
        proj_ref[:, 2 * w:3 * w] = vb
        proj_ref[:, 3 * w:4 * w] = gb

        ext4[8:8 + tt, :] = xa
        xc = lcb_ref[...] + _conv_taps(ext4, lcw_ref, 5, 4, tt)
        ext4[0:8, :] = xa[tt - 8:tt, :]
        r, i, sp, a, mult = _lru_gates(xc, wr_ref, wi_ref, br_ref, bi_ref, lam_ref)
        for k, val in enumerate((xc, r, i, a, mult)):
            lru_ref[:, k * w:(k + 1) * w] = val
        h = _scan_fwd(a, mult * (i * xc), hcar[0:1, :])
        hcar[0:1, :] = h[tt - 1:tt, :]
        h_ref[...] = h
        gelu, _ = _gelu_and_grad(ga)
        y_ref[:, 0:w] = _mx(gelu * h)

        vbg = vb * jax.nn.sigmoid(gb)
        ext31[_HALO:_HALO + tt, :] = vbg
        _make_shifted(ext31, sh31)
        vbc = cb_ref[...] + _conv_taps_shifted(ext31, sh31, cw_ref, _HALO - (kc - 1), kc, tt)
        vbc_ref[...] = vbc
        ext31[0:_HALO, :] = vbg[tt - _HALO:tt, :]
        inv = 1.0 / (w // _N_HEADS)
        zc = vbc - _seg_sum(vbc, seg_ref[...]) * inv
        n = zc * lax.rsqrt(_seg_sum(zc * zc, seg_ref[...]) * inv + _LN_EPS)
        pre = n * ng_ref[...] + nb_ref[...]
        y_ref[:, w:2 * w] = _mx(pre * jax.nn.sigmoid(pre))

        mix = _dot(y_ref[...], wout_ref[...])
        mix_ref[...] = mix
        n1, _ = _layer_norm_stats(_ALPHA * xt + (1.0 + gt1) * mix)
        x1_ref[...] = n1 * g1_ref[...] + b1_ref[...]

    tok = lambda c: pl.BlockSpec((None, tt, c), lambda b, s: (b, s, 0))
    smalls = [lcw, lcb, wr_bd, wi_bd, b_r, b_i, lam, cw, cb, ng, nb, seg, wout, ln1g, ln1b]
    nx = len(shards)
    return pl.pallas_call(
        _fused_exchange(body, 3 + len(smalls), 8, 4, nx, nx, _gather_plan, (bl, ns)), grid=(bl, ns),
        in_specs=[tok(d), pl.BlockSpec((None, 1, 6 * d), lambda b, s: (b, 0, 0)), _resident(win.shape)]
        + [_resident(t.shape) for t in smalls] + [_HBM] * nx,
        out_specs=[tok(4 * w), tok(w), tok(d), tok(d), tok(d), tok(d), tok(w), tok(5 * w)] + [_HBM] * nx,
        out_shape=[jax.ShapeDtypeStruct((bl, s_len, 4 * w), _F32), jax.ShapeDtypeStruct((bl, s_len, w), _F32),
                   jax.ShapeDtypeStruct((bl, s_len, d), _F32), jax.ShapeDtypeStruct((bl, s_len, d), _F32),
                   jax.ShapeDtypeStruct((bl, s_len, d), _MXU_DT), jax.ShapeDtypeStruct((bl, s_len, d), _MXU_DT),
                   jax.ShapeDtypeStruct((bl, s_len, w), _F32), jax.ShapeDtypeStruct((bl, s_len, 5 * w), _F32)]
        + [jax.ShapeDtypeStruct((4,) + t.shape, t.dtype) for t in shards],
        scratch_shapes=[pltpu.VMEM((tt + 8, w), _F32), pltpu.VMEM((tt + _HALO, w), _F32),
                        pltpu.VMEM((7, tt + _HALO - 8, w), _F32), pltpu.VMEM((8, w), _F32)] + _gather_sems(nx),
        compiler_params=_cparams(("arbitrary", "arbitrary")), name="mix_fwd",
    )(x, mod3, win, *smalls, *shards)


def _ffn_fwd(x1, mod3, wup, fcw, fcb, wdn, ln2g, ln2b, target):
    bl, s_len, d = x1.shape
    nch, _, fc = wup.shape
    nch //= 2
    f = nch * fc
    tt = min(_TT_FFN, s_len)
    ns = s_len // tt

    def body(x1_ref, mod_ref, wup_ref, fcw_ref, fcb_ref, wdn_ref, g2_ref, b2_ref, tgt_ref,
             u2_ref, hh_ref, f_ref, gc_ref, dz2_ref, loss_ref, dln2_ref, dgt2_ref, ext3):
        first_tile = pl.program_id(1) == 0

        @pl.when(first_tile)
        def _():
            ext3[:, 0:8, :] = jnp.zeros((nch, 8, fc), _F32)
            dgt2_ref[...] = jnp.zeros_like(dgt2_ref)

        @pl.when(first_tile & (pl.program_id(0) == 0))
        def _():
            loss_ref[...] = jnp.zeros_like(loss_ref)
            dln2_ref[...] = jnp.zeros_like(dln2_ref)

        x1t = x1_ref[...]
        sh2, sc2, gt2 = mod_ref[:, 3 * d:4 * d], mod_ref[:, 4 * d:5 * d], mod_ref[:, 5 * d:6 * d]
        u2 = _mx(x1t * (1.0 + sc2) + sh2)
        u2_ref[...] = u2
        y2 = jnp.zeros((tt, d), _F32)
        for j in range(nch):
            lanes = slice(j * fc, (j + 1) * fc)
            v = _dot(u2, wup_ref[j])
            g = _dot(u2, wup_ref[nch + j])
            hh_ref[:, lanes] = v.astype(hh_ref.dtype)
            hh_ref[:, f + j * fc:f + (j + 1) * fc] = g.astype(hh_ref.dtype)
            ext = ext3.at[j]
            ext[8:8 + tt, :] = g
            gc = fcb_ref[:, lanes] + sum(fcw_ref[k:k + 1, lanes] * ext[pl.ds(6 + k, tt), :] for k in range(3))
            gc_ref[:, lanes] = gc
            ext[0:8, :] = g[tt - 8:tt, :]
            fj = _mx(gc * jax.nn.sigmoid(gc) * v)
            f_ref[:, lanes] = fj
            y2 = y2 + _dot(fj, wdn_ref[lanes, :])

        n2, rstd = _layer_norm_stats(_ALPHA * x1t + (1.0 + gt2) * y2)
        err = n2 * g2_ref[...] + b2_ref[...] - tgt_ref[...]
        loss_ref[...] += jnp.sum(_rowsum(err * err), axis=1, keepdims=True)
        dout = err * (1.0 / d)
        dln2_ref[0:1, :] += _rowsum(dout * n2)
        dln2_ref[1:2, :] += _rowsum(dout)
        dz2 = _layer_norm_bwd(dout * g2_ref[...], n2, rstd)
        dz2_ref[...] = dz2
        dgt2_ref[...] += _rowsum(dz2 * y2)

    tok = lambda c: pl.BlockSpec((None, tt, c), lambda b, s: (b, s, 0))
    acc = lambda r: pl.BlockSpec((r, d), lambda b, s: (0, 0))
    smalls = [fcw, fcb, wdn, ln2g, ln2b]
    return pl.pallas_call(
        body, grid=(bl, ns),
        in_specs=[tok(d), pl.BlockSpec((None, 1, 6 * d), lambda b, s: (b, 0, 0)), _resident(wup.shape)]
        + [_resident(t.shape) for t in smalls] + [tok(d)],
        out_specs=[tok(d), tok(2 * f), tok(f), tok(f), tok(d), acc(1), acc(2), pl.BlockSpec((None, 1, d), lambda b, s: (b, 0, 0))],
        out_shape=[jax.ShapeDtypeStruct((bl, s_len, d), _MXU_DT), jax.ShapeDtypeStruct((bl, s_len, 2 * f), _F32),
                   jax.ShapeDtypeStruct((bl, s_len, f), _MXU_DT), jax.ShapeDtypeStruct((bl, s_len, f), _F32),
                   jax.ShapeDtypeStruct((bl, s_len, d), _F32), jax.ShapeDtypeStruct((1, d), _F32), jax.ShapeDtypeStruct((2, d), _F32),
                   jax.ShapeDtypeStruct((bl, 1, d), _F32)],
        scratch_shapes=[pltpu.VMEM((nch, tt + 8, fc), _F32)],
        compiler_params=_cparams(("arbitrary", "arbitrary")), name="ffn_fwd",
    )(x1, mod3, wup, *smalls, target)


def _ffn_bwd(dz2, x1, hh, gc_all, mod3, wup, wdn, fcw, fcb):
    bl, s_len, d = x1.shape
    nch, _, fc = wup.shape
    nch //= 2
    f = nch * fc
    tt = min(_TT_FFN, s_len)
    ns = s_len // tt

    def body(dz2_ref, x1_ref, hh_ref, gc_ref, mod_ref, wup_ref, wdn_ref, fcw_ref, fcb_ref,
             dx1_ref, dy2_ref, dh_ref, dfc_ref, dmod_ref, dext, dcar):
        @pl.when(pl.program_id(1) == 0)
        def _():
            dcar[...] = jnp.zeros_like(dcar)
            dmod_ref[...] = jnp.zeros_like(dmod_ref)

        @pl.when((pl.program_id(1) == 0) & (pl.program_id(0) == 0))
        def _():
            dfc_ref[...] = jnp.zeros_like(dfc_ref)

        sc2, gt2 = mod_ref[:, 4 * d:5 * d], mod_ref[:, 5 * d:6 * d]
        dz2t = dz2_ref[...]
        dy2 = _mx((1.0 + gt2) * dz2t)
        dy2_ref[...] = dy2
        du2 = jnp.zeros((tt, d), _F32)
        for j in range(nch):
            lanes = slice(j * fc, (j + 1) * fc)
            glanes = slice(f + j * fc, f + (j + 1) * fc)
            v = hh_ref[:, lanes].astype(_F32)
            g = hh_ref[:, glanes].astype(_F32)
            gc = gc_ref[:, lanes]
            sg = jax.nn.sigmoid(gc)
            df = _dot_nt(dy2, wdn_ref[lanes, :])
            dv = df * (gc * sg)
            dgc = df * v * (sg * (1.0 + gc * (1.0 - sg)))
            dfc_ref[3:4, lanes] += _rowsum(dgc)
            dext[0:tt, :] = dgc
            dext[tt:tt + 8, :] = dcar[j]
            dcar[j] = dgc[0:8, :]
            dg = jnp.zeros((tt, fc), _F32)
            for k in range(3):
                shifted = dext[pl.ds(2 - k, tt), :]
                dg = dg + fcw_ref[k:k + 1, lanes] * shifted
                dfc_ref[k:k + 1, lanes] += _rowsum(shifted * g)
            dvb, dgb = _mx(dv), _mx(dg)
            dh_ref[:, lanes] = dvb
            dh_ref[:, glanes] = dgb
            du2 = du2 + _dot_nt(dvb, wup_ref[j]) + _dot_nt(dgb, wup_ref[nch + j])

        dx1_ref[...] = _ALPHA * dz2t + du2 * (1.0 + sc2)
        dmod_ref[0:1, :] += _rowsum(du2)
        dmod_ref[1:2, :] += _rowsum(du2 * x1_ref[...])

    tok = lambda c: pl.BlockSpec((None, tt, c), lambda b, i: (b, ns - 1 - i, 0))
    return pl.pallas_call(
        body, grid=(bl, ns),
        in_specs=[tok(d), tok(d), tok(2 * f), tok(f), pl.BlockSpec((None, 1, 6 * d), lambda b, i: (b, 0, 0)),
                  _resident(wup.shape), _resident(wdn.shape), _resident(fcw.shape), _resident(fcb.shape)],
        out_specs=[tok(d), tok(d), tok(2 * f), pl.BlockSpec((4, f), lambda b, i: (0, 0)),
                   pl.BlockSpec((None, 2, d), lambda b, i: (b, 0, 0))],
        out_shape=[jax.ShapeDtypeStruct((bl, s_len, d), _F32), jax.ShapeDtypeStruct((bl, s_len, d), _MXU_DT),
                   jax.ShapeDtypeStruct((bl, s_len, 2 * f), _MXU_DT), jax.ShapeDtypeStruct((4, f), _F32),
                   jax.ShapeDtypeStruct((bl, 2, d), _F32)],
        scratch_shapes=[pltpu.VMEM((tt + 8, fc), _F32), pltpu.VMEM((nch, 8, fc), _F32)],
        compiler_params=_cparams(("arbitrary", "arbitrary")), name="ffn_bwd",
    )(dz2, x1, hh, gc_all, mod3, wup, wdn, fcw, fcb)


def _mix_bwd(dx1, x, mix, proj, h, vbc, lru, mod3, win, lcw, lcb, wr_bd, wi_bd, b_r, b_i, lam, cw, cb, ng, nb, seg, wout, ln1g, chip_sums):
    bl, s_len, d = x.shape
    w = d // 2
    tt = min(_TT_MIX, s_len)
    ns = s_len // tt
    kc = cw.shape[0]

    def body(dx1_ref, x_ref, mix_ref, proj_ref, phalo_ref, h_ref, hhalo_ref, vbc_ref, lru_ref, mod_ref, win_ref, lcw_ref, lcb_ref,
             wr_ref, wi_ref, br_ref, bi_ref, lam_ref, cw_ref, cb_ref, ng_ref, nb_ref, seg_ref, wout_ref, g1_ref,
             gx_ref, dproj_ref, dmix_ref, xcg_ref, vecw_ref, dlcw_ref, dcw_ref, dln1_ref, dmod_ref,
             ext4, ext31, dext4, dext31, sh31, dsh31, car4, car31, gcar):
        s = ns - 1 - pl.program_id(1)
        first = s == 0

        @pl.when(pl.program_id(1) == 0)
        def _():
            car4[...] = jnp.zeros_like(car4)
            car31[...] = jnp.zeros_like(car31)
            gcar[...] = jnp.zeros_like(gcar)
            dmod_ref[...] = jnp.zeros_like(dmod_ref)

        @pl.when((pl.program_id(1) == 0) & (pl.program_id(0) == 0))
        def _():
            for ref in (vecw_ref, dlcw_ref, dcw_ref, dln1_ref):
                ref[...] = jnp.zeros_like(ref)

        xt, mixt = x_ref[...], mix_ref[...]
        sh1, sc1, gt1 = mod_ref[:, 0:d], mod_ref[:, d:2 * d], mod_ref[:, 2 * d:3 * d]

        n1, rstd1 = _layer_norm_stats(_ALPHA * xt + (1.0 + gt1) * mixt)
        dx1t = dx1_ref[...]
        dln1_ref[0:1, :] += _rowsum(dx1t * n1)
        dln1_ref[1:2, :] += _rowsum(dx1t)
        dz1 = _layer_norm_bwd(dx1t * g1_ref[...], n1, rstd1)
        dmod_ref[2:3, :] += _rowsum(dz1 * mixt)
        dmix = _mx((1.0 + gt1) * dz1)
        dmix_ref[...] = dmix
        dya = _dot_nt(dmix, wout_ref[0:w, :])
        dyb = _dot_nt(dmix, wout_ref[w:2 * w, :])

        xa, ga = proj_ref[:, 0:w], proj_ref[:, w:2 * w]
        vb, gb = proj_ref[:, 2 * w:3 * w], proj_ref[:, 3 * w:4 * w]

        sgb = jax.nn.sigmoid(gb)
        vbg = vb * sgb
        hv, hg = phalo_ref[:, 2 * w:3 * w], phalo_ref[:, 3 * w:4 * w]
        ext31[0:_HALO, :] = jnp.where(first, 0.0, hv * jax.nn.sigmoid(hg))
        ext31[_HALO:_HALO + tt, :] = vbg
        _make_shifted(ext31, sh31)
        vbc = vbc_ref[...]
        inv = 1.0 / (w // _N_HEADS)
        zc = vbc - _seg_sum(vbc, seg_ref[...]) * inv
        rstd = lax.rsqrt(_seg_sum(zc * zc, seg_ref[...]) * inv + _LN_EPS)
        n = zc * rstd
        pre = n * ng_ref[...] + nb_ref[...]
        sgp = jax.nn.sigmoid(pre)
        dpre = dyb * (sgp * (1.0 + pre * (1.0 - sgp)))
        vecw_ref[5:6, :] += _rowsum(dpre * n)
        vecw_ref[6:7, :] += _rowsum(dpre)
        dn = dpre * ng_ref[...]
        dvbc = rstd * (dn - _seg_sum(dn, seg_ref[...], 2) * inv - n * (_seg_sum(dn * n, seg_ref[...], 2) * inv))
        vecw_ref[4:5, :] += _rowsum(dvbc)
        dext31[0:tt, :] = dvbc
        dext31[tt:tt + _HALO, :] = car31[...]
        car31[...] = dvbc[0:_HALO, :]
        _make_shifted(dext31, dsh31)
        dvbg = jnp.zeros((tt, w), _F32)
        for k in range(kc):
            dvbg = dvbg + cw_ref[k:k + 1, :] * _tap(dext31, dsh31, kc - 1 - k, tt)
            dcw_ref[k:k + 1, :] += _rowsum(dvbc * _tap(ext31, sh31, _HALO - (kc - 1) + k, tt))
        dproj_ref[:, 2 * w:3 * w] = _mx(dvbg * sgb)
        dproj_ref[:, 3 * w:4 * w] = _mx(dvbg * vb * (sgb * (1.0 - sgb)))

        ext4[0:8, :] = jnp.where(first, 0.0, phalo_ref[_HALO - 8:_HALO, 0:w])
        ext4[8:8 + tt, :] = xa
        xc, r, i, a, mult = (lru_ref[:, k * w:(k + 1) * w] for k in range(5))
        xcg_ref[:, 0:w] = _mx(xc)
        sp = _softplus(-lam_ref[...])
        ht = h_ref[...]
        row = lax.broadcasted_iota(jnp.int32, (tt, w), 0)
        h_before = jnp.where(first, 0.0, hhalo_ref[7:8, :])
        hprev = jnp.where(row == 0, h_before, pltpu.roll(ht, 1, 0))
        gelu, dgelu = _gelu_and_grad(ga)
        dproj_ref[:, w:2 * w] = _mx(dya * ht * dgelu)
        dh = dya * gelu
        coef = jnp.where(row == tt - 1, 1.0, pltpu.roll(a, tt - 1, 0))
        big_g = _scan_rev(coef, dh, gcar[0:1, :])
        gcar[0:1, :] = a[0:1, :] * big_g[0:1, :]
        da = big_g * hprev
        ixc = i * xc
        dlog_a = da * a - (big_g * ixc) * (a * a / mult)
        di = big_g * mult * xc
        dxc = big_g * mult * i
        vecw_ref[3:4, :] += _rowsum(dlog_a * r) * (_LRU_C * jax.nn.sigmoid(-lam_ref[...]))
        dgr_f = dlog_a * (-_LRU_C * sp) * (r * (1.0 - r))
        dgi_f = di * (i * (1.0 - i))
        vecw_ref[1:2, :] += _rowsum(dgr_f)
        vecw_ref[2:3, :] += _rowsum(dgi_f)
        dgr, dgi = _mx(dgr_f), _mx(dgi_f)
        xcg_ref[:, w:2 * w] = dgr
        xcg_ref[:, 2 * w:3 * w] = dgi
        dxc = dxc + _dot_nt(dgr, wr_ref[...]) + _dot_nt(dgi, wi_ref[...])
        vecw_ref[0:1, :] += _rowsum(dxc)
        dext4[0:tt, :] = dxc
        dext4[tt:tt + 8, :] = car4[...]
        car4[...] = dxc[0:8, :]
        dxa = jnp.zeros((tt, w), _F32)
        for k in range(4):
            dxa = dxa + lcw_ref[k:k + 1, :] * dext4[pl.ds(3 - k, tt), :]
            dlcw_ref[k:k + 1, :] += _rowsum(dxc * ext4[pl.ds(5 + k, tt), :])
        dproj_ref[:, 0:w] = _mx(dxa)

        du1 = sum(_dot_nt(dproj_ref[:, k * w:(k + 1) * w], win_ref[k]) for k in range(4))
        gx_ref[...] = _ALPHA * dz1 + du1 * (1.0 + sc1)
        dmod_ref[0:1, :] += _rowsum(du1)
        dmod_ref[1:2, :] += _rowsum(du1 * xt)

    tok = lambda c: pl.BlockSpec((None, tt, c), lambda b, i: (b, ns - 1 - i, 0))
    halo = lambda rows, c: pl.BlockSpec(
        (None, rows, c), lambda b, i: (b, jnp.maximum((ns - 1 - i) * (tt // rows) - 1, 0), 0))
    accw = lambda r, c: pl.BlockSpec((r, c), lambda b, i: (0, 0))
    smalls = [lcw, lcb, wr_bd, wi_bd, b_r, b_i, lam, cw, cb, ng, nb, seg, wout, ln1g]
    nx = len(chip_sums)
    return pl.pallas_call(
        _fused_exchange(body, 11 + len(smalls), 9, 9, nx, nx, _chip_reduce_plan, (bl, ns)), grid=(bl, ns),
        in_specs=[tok(d), tok(d), tok(d), tok(4 * w), halo(_HALO, 4 * w), tok(w), halo(8, w), tok(w), tok(5 * w),
                  pl.BlockSpec((None, 1, 6 * d), lambda b, i: (b, 0, 0)), _resident(win.shape)]
        + [_resident(t.shape) for t in smalls] + [_HBM] * nx,
        out_specs=[tok(d), tok(4 * w), tok(d), tok(3 * w), accw(8, w), accw(4, w), accw(kc, w), accw(2, d),
                   pl.BlockSpec((None, 3, d), lambda b, i: (b, 0, 0))] + [_HBM] * nx,
        out_shape=[jax.ShapeDtypeStruct((bl, s_len, d), _F32), jax.ShapeDtypeStruct((bl, s_len, 4 * w), _MXU_DT),
                   jax.ShapeDtypeStruct((bl, s_len, d), _MXU_DT), jax.ShapeDtypeStruct((bl, s_len, 3 * w), _MXU_DT),
                   jax.ShapeDtypeStruct((8, w), _F32), jax.ShapeDtypeStruct((4, w), _F32),
                   jax.ShapeDtypeStruct((kc, w), _F32), jax.ShapeDtypeStruct((2, d), _F32),
                   jax.ShapeDtypeStruct((bl, 3, d), _F32)]
        + [jax.ShapeDtypeStruct((3,) + t.shape[1:], t.dtype) for t in chip_sums],
        scratch_shapes=[pltpu.VMEM((tt + 8, w), _F32), pltpu.VMEM((tt + _HALO, w), _F32),
                        pltpu.VMEM((tt + 8, w), _F32), pltpu.VMEM((tt + _HALO, w), _F32),
                        pltpu.VMEM((7, tt + _HALO - 8, w), _F32), pltpu.VMEM((7, tt + _HALO - 8, w), _F32),
                        pltpu.VMEM((8, w), _F32), pltpu.VMEM((_HALO, w), _F32), pltpu.VMEM((8, w), _F32)]
        + _chip_reduce_sems(nx),
        compiler_params=_cparams(("arbitrary", "arbitrary")), name="mix_bwd",
    )(dx1, x, mix, proj, proj, h, h, vbc, lru, mod3, win, *smalls, *chip_sums)


def _wgrad(a, b, ma, nbw, na, nb, a_off, b_off, name, exchange=None):
    t = a.shape[0]
    tk = min(_TK_WGRAD, t)
    grid = (na * nb, t // tk)

    def body(a_ref, b_ref, o_ref):
        @pl.when(pl.program_id(1) == 0)
        def _():
            o_ref[...] = jnp.zeros_like(o_ref)
        o_ref[...] += _dot_tn(a_ref[...], b_ref[...])

    xin, xshapes, plan, sems = exchange if exchange else ([], [], None, [])
    nx = len(xin)
    res = pl.pallas_call(
        _fused_exchange(body, 2, 1, 0, nx, len(xshapes), plan, grid) if exchange else body, grid=grid,
        in_specs=[pl.BlockSpec((tk, ma), lambda j, k: (k, j // nb + a_off)),
                  pl.BlockSpec((tk, nbw), lambda j, k: (k, j % nb + b_off))] + [_HBM] * nx,
        out_specs=[pl.BlockSpec((None, ma, nbw), lambda j, k: (j, 0, 0))] + [_HBM] * len(xshapes),
        out_shape=[jax.ShapeDtypeStruct((na * nb, ma, nbw), _F32)] + list(xshapes),
        scratch_shapes=list(sems),
        compiler_params=_cparams(("arbitrary", "arbitrary")), name=name,
    )(a, b, *xin)
    return res if exchange else res[0]


_DEV_DELTAS = tuple(dl for dl in itertools.product((0, 1), repeat=3) if any(dl))
_HBM = pl.BlockSpec(memory_space=pltpu.HBM)
_VMEM = pl.BlockSpec(memory_space=pltpu.VMEM)


def _pos():
    return lax.axis_index("x"), lax.axis_index("y"), lax.axis_index("c")


def _flip(v, delta):
    return 1 - v if delta else v


def _remote(src, dst, ssem, rsem, dev):
    return pltpu.make_async_remote_copy(src_ref=src, dst_ref=dst, send_sem=ssem, recv_sem=rsem,
                                        device_id=dev, device_id_type=_MESH)


def _rows(ref, idx, n):
    return ref.at[pl.ds(pl.multiple_of(idx * n, 8), n)]


def _ada_fwd(c8, w_ada_k, b_ada_k, shards):
    rows, d = c8.shape
    nk = w_ada_k.shape[1]
    n = len(shards)

    def body(*refs):
        c_ref, w_ref, b_ref = refs[:3]
        call_ref, mod_ref = refs[3 + n:5 + n]
        modloc, modrcv, s1, r1, s2, r2 = refs[5 + 2 * n:11 + 2 * n]
        gather_start, gather_finish = _gather_plan(refs[3:3 + n], refs[5 + n:5 + 2 * n], *refs[11 + 2 * n:14 + 2 * n],
                                                   fsem=refs[14 + 2 * n], frsem=refs[15 + 2 * n], bounce=refs[16 + 2 * n:])
        gather_start()
        xi, yi, ci = _pos()
        me, kme = 4 * xi + 2 * yi + ci, 2 * xi + yi
        call_ref[pl.ds(pl.multiple_of(me * rows, 8), rows), :] = c_ref[...]
        sends = []
        for p, (dx, dy, dc) in enumerate(_DEV_DELTAS):
            cp = _remote(c_ref, _rows(call_ref, me, rows), s1.at[p], r1.at[p], (_flip(xi, dx), _flip(yi, dy), _flip(ci, dc)))
            cp.start()
            sends.append(cp)
        for p, (dx, dy, dc) in enumerate(_DEV_DELTAS):
            src = 4 * _flip(xi, dx) + 2 * _flip(yi, dy) + _flip(ci, dc)
            _remote(c_ref, _rows(call_ref, src, rows), s1.at[p], r1.at[p], (xi, yi, ci)).wait_recv()
        for cp in sends:
            cp.wait_send()

        ca = call_ref[...]
        modloc[...] = _dot(_mx(ca * jax.nn.sigmoid(ca)), _mx(w_ref[...])) + b_ref[...]
        modrcv[kme] = modloc[pl.ds(pl.multiple_of(me * rows, 8), rows), :]
        sends = []
        for j, (dx, dy) in enumerate(_CHIP_DELTAS):
            tx, ty = _flip(xi, dx), _flip(yi, dy)
            cp = _remote(_rows(modloc, 4 * tx + 2 * ty + ci, rows), modrcv.at[kme], s2.at[j], r2.at[j], (tx, ty, ci))
            cp.start()
            sends.append(cp)
        for j, (dx, dy) in enumerate(_CHIP_DELTAS):
            ksrc = 2 * _flip(xi, dx) + _flip(yi, dy)
            _remote(_rows(modloc, me, rows), modrcv.at[ksrc], s2.at[j], r2.at[j], (xi, yi, ci)).wait_recv()
        for cp in sends:
            cp.wait_send()
        for j in range(4):
            mod_ref[:, j * nk:(j + 1) * nk] = modrcv[j]
        gather_finish()

    return pl.pallas_call(
        body, in_specs=[_VMEM, _VMEM, _VMEM] + [_HBM] * n, out_specs=[_VMEM, _VMEM] + [_HBM] * n,
        out_shape=[jax.ShapeDtypeStruct((8 * rows, d), _F32), jax.ShapeDtypeStruct((rows, 4 * nk), _F32)]
        + [jax.ShapeDtypeStruct((4,) + a.shape, a.dtype) for a in shards],
        scratch_shapes=[pltpu.VMEM((8 * rows, nk), _F32), pltpu.VMEM((4, rows, nk), _F32),
                        pltpu.SemaphoreType.DMA((7,)), pltpu.SemaphoreType.DMA((7,)),
                        pltpu.SemaphoreType.DMA((3,)), pltpu.SemaphoreType.DMA((3,))]
        + _gather_sems(n) + [pltpu.SemaphoreType.DMA((3, n)), pltpu.SemaphoreType.DMA((3, n))]
        + [pltpu.VMEM(a.shape, a.dtype) for a in shards],
        compiler_params=pltpu.CompilerParams(vmem_limit_bytes=_VMEM_LIMIT), name="ada_fwd",
    )(c8, w_ada_k, b_ada_k, *shards)


def _gather_sems(n):
    return [pltpu.SemaphoreType.DMA((3, n)), pltpu.SemaphoreType.DMA((3, n)), pltpu.SemaphoreType.DMA((n,))]


def _gather_plan(ins, outs, ssem, rsem, lsem, bounce=(), fsem=None, frsem=None):
    n = len(ins)
    xi, yi, ci = _pos()
    kme = 2 * xi + yi
    split = [fsem is not None and ins[a].shape[0] % 32 == 0 for a in range(n)]

    def half(ref, a, which):
        r2 = ins[a].shape[0] // 2
        return ref.at[pl.ds(pl.multiple_of(which * r2, 16), r2)]

    staged = [pltpu.make_async_copy(ins[a], bounce[a], lsem.at[a]) for a in range(len(bounce))]
    local = [pltpu.make_async_copy(bounce[a] if bounce else ins[a], outs[a].at[kme], lsem.at[a]) for a in range(n)]
    sends, recvs, forwards, handed = [], [], [], []
    for j, (dx, dy) in enumerate(_CHIP_DELTAS):
        tx, ty = _flip(xi, dx), _flip(yi, dy)
        for a in range(n):
            sems = (ssem.at[j, a], rsem.at[j, a])
            landing = outs[a].at[2 * tx + ty]
            if split[a]:
                sends.append(_remote(half(ins[a], a, ci), half(outs[a].at[kme], a, ci), *sems, (tx, ty, ci)))
                recvs.append(_remote(half(ins[a], a, ci), half(landing, a, ci), *sems, (xi, yi, ci)))
                fsems = (fsem.at[j, a], frsem.at[j, a])
                forwards.append(_remote(half(landing, a, ci), half(landing, a, ci), *fsems, (xi, yi, 1 - ci)))
                handed.append(_remote(half(ins[a], a, 1 - ci), half(landing, a, 1 - ci), *fsems, (xi, yi, ci)))
            else:
                sends.append(_remote(ins[a], outs[a].at[kme], *sems, (tx, ty, ci)))
                recvs.append(_remote(ins[a], landing, *sems, (xi, yi, ci)))
                forwards.append(None)

    def start():
        for cp in sends + staged:
            cp.start()
        for cp in staged:
            cp.wait()
        for cp in local:
            cp.start()

    def finish():
        for arrived, forward in zip(recvs, forwards):
            arrived.wait_recv()
            if forward is not None:
                forward.start()
        for cp in handed:
            cp.wait_recv()
        for cp in sends + [f for f in forwards if f is not None]:
            cp.wait_send()
        for cp in local:
            cp.wait()

    return start, finish


def _dev_gather_sems(n):
    return [pltpu.SemaphoreType.DMA((7, n)), pltpu.SemaphoreType.DMA((7, n)), pltpu.SemaphoreType.DMA((n,))]


def _dev_gather_plan(ins, outs, ssem, rsem, lsem):
    n = len(ins)
    xi, yi, ci = _pos()
    me = 4 * xi + 2 * yi + ci
    local = [pltpu.make_async_copy(ins[a], outs[a].at[me], lsem.at[a]) for a in range(n)]
    sends, recvs = [], []
    for p, (dx, dy, dc) in enumerate(_DEV_DELTAS):
        tx, ty, tc = _flip(xi, dx), _flip(yi, dy), _flip(ci, dc)
        for a in range(n):
            sends.append(_remote(ins[a], outs[a].at[me], ssem.at[p, a], rsem.at[p, a], (tx, ty, tc)))
            recvs.append(_remote(ins[a], outs[a].at[4 * tx + 2 * ty + tc], ssem.at[p, a], rsem.at[p, a], (xi, yi, ci)))

    def start():
        for cp in local + sends:
            cp.start()

    def finish():
        for cp in recvs:
            cp.wait_recv()
        for cp in sends:
            cp.wait_send()
        for cp in local:
            cp.wait()

    return start, finish


def _pair_sems(n):
    return [pltpu.SemaphoreType.DMA((n,)), pltpu.SemaphoreType.DMA((n,))]


def _pair_plan(ins, outs, ssem, rsem):
    xi, yi, ci = _pos()
    sends = []
    for a in range(len(ins)):
        r2 = ins[a].shape[1] // 2
        src = ins[a].at[:, pl.ds(pl.multiple_of((1 - ci) * r2, 8), r2), :]
        sends.append(_remote(src, outs[a], ssem.at[a], rsem.at[a], (xi, yi, 1 - ci)))

    def start():
        for cp in sends:
            cp.start()

    def finish():
        for cp in sends:
            cp.wait_recv()
        for cp in sends:
            cp.wait_send()

    return start, finish


def _chip_reduce_sems(n):
    return [pltpu.SemaphoreType.DMA((3, n)), pltpu.SemaphoreType.DMA((3, n))]


def _chip_reduce_plan(ins, outs, ssem, rsem):
    xi, yi, ci = _pos()
    sends = []
    for j, (dx, dy) in enumerate(_CHIP_DELTAS):
        tx, ty = _flip(xi, dx), _flip(yi, dy)
        sends += [_remote(ins[a].at[2 * tx + ty], outs[a].at[j], ssem.at[j, a], rsem.at[j, a], (tx, ty, ci))
                  for a in range(len(ins))]

    def start():
        for cp in sends:
            cp.start()

    def finish():
        for cp in sends:
            cp.wait_recv()
        for cp in sends:
            cp.wait_send()

    return start, finish


def _pair_exchange(gs, name):
    n = len(gs)

    def body(*refs):
        start, finish = _pair_plan(refs[:n], refs[n:2 * n], *refs[2 * n:])
        start()
        finish()

    return pl.pallas_call(
        body, in_specs=[_HBM] * n, out_specs=[_HBM] * n, out_shape=_pair_out_shapes(gs),
        scratch_shapes=_pair_sems(n), name=name,
    )(*gs)


def _pair_out_shapes(gs):
    return [jax.ShapeDtypeStruct((g.shape[0], g.shape[1] // 2, g.shape[2]), g.dtype) for g in gs]


def _row_tile(r):
    return max(t for t in range(8, min(r, 256) + 1, 8) if r % t == 0)


def _pair_add(g, r, cidx, name, wire_dtype=None):
    nk, r2, c = r.shape
    tr = _row_tile(r2)
    nt = r2 // tr

    def body(c_ref, g_ref, r_ref, *o_refs):
        s = g_ref[...] + r_ref[...]
        for o_ref in o_refs:
            o_ref[...] = s.astype(o_ref.dtype)

    out_spec = pl.BlockSpec((None, tr, c), lambda k, i, cr: (k, i, 0))
    dtypes = [_F32] + ([wire_dtype] if wire_dtype else [])
    res = pl.pallas_call(
        body, grid_spec=pltpu.PrefetchScalarGridSpec(
            num_scalar_prefetch=1, grid=(nk, nt),
            in_specs=[pl.BlockSpec((None, tr, c), lambda k, i, cr: (k, cr[0] * nt + i, 0)), out_spec],
            out_specs=[out_spec] * len(dtypes)),
        out_shape=[jax.ShapeDtypeStruct(r.shape, dt) for dt in dtypes],
        compiler_params=_cparams(("arbitrary", "arbitrary")), name=name,
    )(cidx, g, r)
    return res if wire_dtype else res[0]


def _chip_exchange(ss):
    n = len(ss)

    def body(*refs):
        start, finish = _chip_reduce_plan(refs[:n], refs[n:2 * n], *refs[2 * n:])
        start()
        finish()

    return pl.pallas_call(
        body, in_specs=[_HBM] * n, out_specs=[_HBM] * n,
        out_shape=[jax.ShapeDtypeStruct((3,) + s.shape[1:], s.dtype) for s in ss],
        scratch_shapes=_chip_reduce_sems(n), name="grad_chip_exchange",
    )(*ss)


def _chip_add(s, r, kidx, name):
    _, r2, c = r.shape
    tr = _row_tile(r2)

    def body(k_ref, s_ref, r_ref, o_ref):
        o_ref[...] = ((s_ref[...] + r_ref[0].astype(_F32)) + r_ref[1].astype(_F32)) + r_ref[2].astype(_F32)

    return pl.pallas_call(
        body, grid_spec=pltpu.PrefetchScalarGridSpec(
            num_scalar_prefetch=1, grid=(r2 // tr,),
            in_specs=[pl.BlockSpec((None, tr, c), lambda i, kr: (kr[0], i, 0)),
                      pl.BlockSpec((3, tr, c), lambda i, kr: (0, i, 0))],
            out_specs=pl.BlockSpec((tr, c), lambda i, kr: (i, 0))),
        out_shape=jax.ShapeDtypeStruct((r2, c), _F32),
        compiler_params=_cparams(("arbitrary",)), name=name,
    )(kidx, s, r)


def _pair_swap(hs, name):
    n = len(hs)

    def body(*refs):
        ins, outs = refs[:n], refs[n:2 * n]
        ssem, rsem = refs[2 * n:]
        xi, yi, ci = _pos()
        sends = [_remote(ins[a], outs[a], ssem.at[a], rsem.at[a], (xi, yi, 1 - ci)) for a in range(n)]
        for cp in sends:
            cp.start()
        for cp in sends:
            cp.wait_recv()
        for cp in sends:
            cp.wait_send()

    return pl.pallas_call(
        body, in_specs=[_HBM] * n, out_specs=[_HBM] * n,
        out_shape=[jax.ShapeDtypeStruct(h.shape, h.dtype) for h in hs],
        scratch_shapes=[pltpu.SemaphoreType.DMA((n,)), pltpu.SemaphoreType.DMA((n,))], name=name,
    )(*hs)


def _small_sum(every):
    def body(all_ref, sum_ref):
        tot = all_ref[0]
        for dev in range(1, 8):
            tot = tot + all_ref[dev]
        sum_ref[...] = tot

    return pl.pallas_call(
        body, in_specs=[_VMEM], out_specs=_VMEM, out_shape=jax.ShapeDtypeStruct(every.shape[1:], _F32),
        compiler_params=pltpu.CompilerParams(vmem_limit_bytes=_VMEM_LIMIT), name="small_sum",
    )(every)


def _adamw(w, g, m, v):
    m = _ADAM_B1 * m + (1.0 - _ADAM_B1) * g
    v = _ADAM_B2 * v + (1.0 - _ADAM_B2) * (g * g)
    m_hat = m / (1.0 - _ADAM_B1 ** _ADAM_STEP)
    v_hat = v / (1.0 - _ADAM_B2 ** _ADAM_STEP)
    return -_ADAM_LR * (m_hat / (jnp.sqrt(v_hat) + _ADAM_EPS) + _ADAM_WD * w), m, v


def _adamw_big(w, g_mine, g_theirs, m, v, cidx, name, chip_sums=()):
    r, c = w.shape
    tr = _row_tile(r // 2)
    nt = r // 2 // tr
    nx = len(chip_sums)

    def body(c_ref, w_ref, gm_ref, gt_ref, m_ref, v_ref, g_ref, d_ref, mo_ref, vo_ref):
        g = jnp.where(pl.program_id(0) // nt == c_ref[0], gm_ref[...], gt_ref[...])
        g_ref[...] = g
        d_ref[...], mo_ref[...], vo_ref[...] = _adamw(w_ref[...], g, m_ref[...], v_ref[...])

    spec = pl.BlockSpec((tr, c), lambda i, cr: (i, 0))
    half = pl.BlockSpec((tr, c), lambda i, cr: (i % nt, 0))
    return pl.pallas_call(
        _fused_exchange(body, 6, 4, 0, nx, nx, _chip_reduce_plan, (2 * nt,)) if nx else body,
        grid_spec=pltpu.PrefetchScalarGridSpec(
            num_scalar_prefetch=1, grid=(2 * nt,), in_specs=[spec, half, half, spec, spec] + [_HBM] * nx,
            out_specs=[spec] * 4 + [_HBM] * nx, scratch_shapes=_chip_reduce_sems(nx) if nx else []),
        out_shape=[jax.ShapeDtypeStruct((r, c), _F32)] * 4
        + [jax.ShapeDtypeStruct((3,) + t.shape[1:], t.dtype) for t in chip_sums],
        compiler_params=_cparams(("arbitrary",)), name=name,
    )(cidx, w, g_mine, g_theirs, m, v, *chip_sums)


def _adamw_small(ws, gs, ms, vs):
    n = len(ws)
    summed = [i for i in range(n) if gs[i].shape != ws[i].shape]

    def body(*refs):
        w_r, g_r, m_r, v_r = (refs[i * n:(i + 1) * n] for i in range(4))
        outs = refs[4 * n:]
        for i in range(n):
            g = g_r[i][...]
            if i in summed:
                g = _rowsum(g)
                outs[3 * n + summed.index(i)][...] = g
            outs[i][...], outs[n + i][...], outs[2 * n + i][...] = _adamw(w_r[i][...], g, m_r[i][...], v_r[i][...])

    shapes = [jax.ShapeDtypeStruct(w.shape, _F32) for w in ws]
    res = pl.pallas_call(
        body, in_specs=[_VMEM] * (4 * n), out_specs=[_VMEM] * (3 * n + len(summed)),
        out_shape=shapes * 3 + [shapes[i] for i in summed],
        compiler_params=pltpu.CompilerParams(vmem_limit_bytes=_VMEM_LIMIT), name="adamw_small",
    )(*ws, *gs, *ms, *vs)
    gs = list(gs)
    for pos, i in enumerate(summed):
        gs[i] = res[3 * n + pos]
    return gs, res[:n], res[n:2 * n], res[2 * n:3 * n]


def _ada_bwd(c_all, dmod_k, w, m, v, chip_sums=()):
    d, nk = w.shape
    tn = 512 if nk % 512 == 0 else nk
    nx = len(chip_sums)

    def body(c_ref, dm_ref, w_ref, m_ref, v_ref, g_ref, d_ref, mo_ref, vo_ref):
        ca = c_ref[...]
        g = _dot_tn(_mx(ca * jax.nn.sigmoid(ca)), _mx(dm_ref[...]))
        g_ref[...] = g
        d_ref[...], mo_ref[...], vo_ref[...] = _adamw(w_ref[...], g, m_ref[...], v_ref[...])

    col = pl.BlockSpec((d, tn), lambda j: (0, j))
    return pl.pallas_call(
        _fused_exchange(body, 5, 4, 0, nx, nx, _chip_reduce_plan, (nk // tn,)) if nx else body, grid=(nk // tn,),
        in_specs=[pl.BlockSpec(c_all.shape, lambda j: (0, 0)), pl.BlockSpec((c_all.shape[0], tn), lambda j: (0, j)),
                  col, col, col] + [_HBM] * nx,
        out_specs=[col] * 4 + [_HBM] * nx,
        out_shape=[jax.ShapeDtypeStruct((d, nk), _F32)] * 4
        + [jax.ShapeDtypeStruct((3,) + t.shape[1:], t.dtype) for t in chip_sums],
        scratch_shapes=_chip_reduce_sems(nx) if nx else [],
        compiler_params=_cparams(("arbitrary",)), name="ada_bwd",
    )(c_all, dmod_k, w, m, v, *chip_sums)


def _block_diag(wh):
    hn, dh, _ = wh.shape
    eye = jnp.eye(hn, dtype=wh.dtype)
    return (eye[:, None, :, None] * wh[:, :, None, :]).reshape(hn * dh, hn * dh)


def _pack(pieces):
    out = []
    for p in pieces:
        flat = p.reshape(-1, 128)
        out.append(jnp.pad(flat, ((0, (-flat.shape[0]) % 8), (0, 0))))
    return jnp.concatenate(out, axis=0)


def _unpack(pack, shapes):
    out, off = [], 0
    for shp in shapes:
        rows = math.prod(shp) // 128
        out.append(pack[..., off:off + rows, :].reshape(pack.shape[:-2] + tuple(shp)))
        off += rows + (-rows) % 8
    return out


_WEIGHTS = ('w_ada', 'b_ada', 'w_in', 'lru_conv_w', 'lru_conv_b', 'lru_w_r', 'lru_b_r', 'lru_w_i', 'lru_b_i', 'lru_lambda',
            'conv_w', 'conv_b', 'conv_norm_g', 'conv_norm_b', 'w_out', 'ln1_g', 'ln1_b', 'ffn_w_up', 'ffn_conv_w',
            'ffn_conv_b', 'ffn_w_down', 'ln2_g', 'ln2_b')
_BIG = ('w_in', 'w_out', 'ffn_w_up', 'ffn_w_down')


def kernel(x, c, w_ada, b_ada, w_in, lru_conv_w, lru_conv_b, lru_w_r, lru_b_r, lru_w_i, lru_b_i, lru_lambda, conv_w, conv_b, conv_norm_g, conv_norm_b, w_out, ln1_g, ln1_b, ffn_w_up, ffn_conv_w, ffn_conv_b, ffn_w_down, ln2_g, ln2_b, loss_target, m_w_ada, m_b_ada, m_w_in, m_lru_conv_w, m_lru_conv_b, m_lru_w_r, m_lru_b_r, m_lru_w_i, m_lru_b_i, m_lru_lambda, m_conv_w, m_conv_b, m_conv_norm_g, m_conv_norm_b, m_w_out, m_ln1_g, m_ln1_b, m_ffn_w_up, m_ffn_conv_w, m_ffn_conv_b, m_ffn_w_down, m_ln2_g, m_ln2_b, v_w_ada, v_b_ada, v_w_in, v_lru_conv_w, v_lru_conv_b, v_lru_w_r, v_lru_b_r, v_lru_w_i, v_lru_b_i, v_lru_lambda, v_conv_w, v_conv_b, v_conv_norm_g, v_conv_norm_b, v_w_out, v_ln1_g, v_ln1_b, v_ffn_w_up, v_ffn_conv_w, v_ffn_conv_b, v_ffn_w_down, v_ln2_g, v_ln2_b):
    given = dict(locals())
    wt = {n: given[n] for n in _WEIGHTS}
    mo = {n: given["m_" + n] for n in _WEIGHTS}
    vo = {n: given["v_" + n] for n in _WEIGHTS}
    bl, s_len, d = x.shape
    wd = d // 2
    tokens = bl * s_len
    xi, yi, ci = _pos()
    kme = 2 * xi + yi
    kidx = jnp.reshape(kme, (1,)).astype(jnp.int32)
    cidx = jnp.reshape(ci, (1,)).astype(jnp.int32)

    nk = w_ada.shape[2]
    c8 = jnp.pad(c, ((0, 8 - bl), (0, 0)))
    c_all, mod8, win, wout_s, lcw_s, cw_s, fcw_s = _ada_fwd(
        c8, w_ada[0], lax.dynamic_slice(b_ada, (0, kme * nk), (1, nk)),
        [_mx(w_in[0]), _mx(w_out[0]), lru_conv_w[0], conv_w[0], ffn_conv_w[0]])
    mod3 = mod8[:bl].reshape(bl, 1, 6 * d)
    wout = wout_s.reshape(d, d)
    f = 4 * ffn_w_down.shape[1]
    unshard = lambda t: jnp.transpose(t, (1, 0, 2)).reshape(t.shape[1], -1)
    lcw, cw, fcw = unshard(lcw_s), unshard(cw_s), unshard(fcw_s)
    wr_bd, wi_bd = _mx(_block_diag(lru_w_r[0])), _mx(_block_diag(lru_w_i[0]))
    seg = _block_diag(jnp.ones((_N_HEADS, wd // _N_HEADS, wd // _N_HEADS), jnp.bfloat16))
    mixer_small = (lcw, lru_conv_b, wr_bd, wi_bd, lru_b_r, lru_b_i, lru_lambda, cw, conv_b, conv_norm_g, conv_norm_b, seg, wout, ln1_g)

    proj, h, mix, x1, u1, y, vbc, lru, wup, wdn_s = _mix_fwd(x, mod3, win, *mixer_small, ln1_b, [_mx(ffn_w_up[0]), _mx(ffn_w_down[0])])
    wdn = wdn_s.reshape(f, d)
    u2, hh, fact, gc_all, dz2, loss_acc, dln2, dgt2 = _ffn_fwd(x1, mod3, wup, fcw, ffn_conv_b, wdn, ln2_g, ln2_b, loss_target)
    dx1, dy2, dh, dfc, dmod2 = _ffn_bwd(dz2, x1, hh, gc_all, mod3, wup, wdn, fcw, ffn_conv_b)

    flat = lambda t: t.reshape(tokens, t.shape[-1])
    fc = wup.shape[2]
    g_up = _wgrad(flat(u2), flat(dh), d, fc, 1, 4, 0, 0, "wgrad_up")
    g_dn, r_up = _wgrad(flat(fact), flat(dy2), fc, d, f // fc, 1, 0, 0, "wgrad_down",
                        exchange=([g_up], _pair_out_shapes([g_up]), _pair_plan, _pair_sems(1)))
    g_dn = g_dn.reshape(4, f // 4, d)
    r_dn, = _pair_exchange([g_dn], "grad_pair_exchange_ffn_w_down")
    ffn_sum = [_pair_add(g, r, cidx, "grad_pair_add_" + n) for g, r, n in zip([g_up, g_dn], [r_up, r_dn], _BIG[2:])]
    grad_x, dproj, dmix, xcg, vecw, dlcw, dcw, dln1, dmod1, *ffn_recv = _mix_bwd(
        dx1, x, mix, proj, h, vbc, lru, mod3, win, *mixer_small, ffn_sum)
    g_ri = _wgrad(flat(xcg), flat(xcg), wd, wd, 1, 2, 0, 1, "wgrad_gates")
    dh_ = wd // _N_HEADS
    on_diagonal = jnp.eye(_N_HEADS, dtype=_F32)[None, :, None, :, None]
    g_ri = jnp.sum(g_ri.reshape(2, _N_HEADS, dh_, _N_HEADS, dh_) * on_diagonal, axis=3)

    dmod = jnp.concatenate([dmod1.reshape(bl, 3 * d), dmod2.reshape(bl, 2 * d), dgt2.reshape(bl, d)], axis=1)
    pieces = [vecw, dlcw, dcw, jnp.concatenate([dln1, dln2], axis=0), dfc, g_ri, loss_acc[:, 0:128],
              jnp.pad(dmod, ((0, 8 - bl), (0, 0)))]
    shapes = [p.shape for p in pieces]
    pack = _pack(pieces)
    g_in, every = _wgrad(flat(u1), flat(dproj), d, wd, 1, 4, 0, 0, "wgrad_in", exchange=(
        [pack], [jax.ShapeDtypeStruct((8,) + pack.shape, _F32)], _dev_gather_plan, _dev_gather_sems(1)))
    wire_shape = lambda t: [jax.ShapeDtypeStruct((3,) + t.shape[1:], t.dtype)]
    r_in, = _pair_exchange([g_in], "grad_pair_exchange_w_in")
    s_in, wire_in = _pair_add(g_in, r_in, cidx, "grad_pair_add_w_in", jnp.bfloat16)
    g_out, recv_in = _wgrad(flat(y), flat(dmix), d, d, 1, 1, 0, 0, "wgrad_out", exchange=(
        [wire_in], wire_shape(wire_in), _chip_reduce_plan, _chip_reduce_sems(1)))
    g_out = g_out.reshape(4, d // 4, d)
    r_out, = _pair_exchange([g_out], "grad_pair_exchange_w_out")
    s_out, wire_out = _pair_add(g_out, r_out, cidx, "grad_pair_add_w_out", jnp.bfloat16)
    recv_out, = _chip_exchange([wire_out])
    chip_sum, recv = [s_in, s_out] + ffn_sum, [recv_in, recv_out] + list(ffn_recv)
    half = [_chip_add(s, r, kidx, "grad_chip_add_" + n) for s, r, n in zip(chip_sum, recv, _BIG)]
    grads, deltas, new_m, new_v = {}, {}, {}, {}
    for n, mine, theirs in zip(_BIG, half, _pair_swap(half, "grad_pair_swap")):
        g, dl, mm, vv = _adamw_big(wt[n][0], mine, theirs, mo[n][0], vo[n][0], cidx, "adamw_" + n)
        grads[n], deltas[n], new_m[n], new_v[n] = g[None], dl[None], mm[None], vv[None]

    vecw, dlcw, dcw, dln, dfc, g_ri, loss_sum, dmod_sum = _unpack(_small_sum(every), shapes)
    loss = 0.5 * loss_sum[0, 0] / d
    dmod_all = _unpack(every, shapes)[-1].reshape(64, 6 * d)

    g_ada, dl, mm, vv = _ada_bwd(c_all, lax.dynamic_slice(dmod_all, (0, kme * nk), (64, nk)), w_ada[0], m_w_ada[0], v_w_ada[0])
    grads['w_ada'], deltas['w_ada'], new_m['w_ada'], new_v['w_ada'] = g_ada[None], dl[None], mm[None], vv[None]

    shard = lambda t, width: lax.dynamic_slice(t, (0, kme * width), (t.shape[0], width))
    small = {
        'b_ada': dmod_sum, 'lru_conv_w': shard(dlcw, wd // 4), 'lru_conv_b': vecw[0:1], 'lru_w_r': g_ri[0], 'lru_b_r': vecw[1:2],
        'lru_w_i': g_ri[1], 'lru_b_i': vecw[2:3], 'lru_lambda': vecw[3:4], 'conv_w': shard(dcw, wd // 4), 'conv_b': vecw[4:5],
        'conv_norm_g': vecw[5:6], 'conv_norm_b': vecw[6:7], 'ln1_g': dln[0:1], 'ln1_b': dln[1:2],
        'ffn_conv_w': shard(dfc[0:3], f // 4), 'ffn_conv_b': dfc[3:4], 'ln2_g': dln[2:3], 'ln2_b': dln[3:4]}
    names = list(small)
    gs = [small[n] if n == 'b_ada' else small[n].reshape(wt[n].shape) for n in names]
    gs, dls, mms, vvs = _adamw_small([wt[n] for n in names], gs, [mo[n] for n in names], [vo[n] for n in names])
    for n, g, dl, mm, vv in zip(names, gs, dls, mms, vvs):
        grads[n], deltas[n], new_m[n], new_v[n] = g, dl, mm, vv

    return (loss, grad_x, *[grads[n] for n in _WEIGHTS], *[deltas[n] for n in _WEIGHTS],
            *[new_m[n] for n in _WEIGHTS], *[new_v[n] for n in _WEIGHTS])
```

```python
import functools
import itertools
import math

import jax
import jax.numpy as jnp
from jax import lax
from jax.experimental import pallas as pl
from jax.experimental.pallas import tpu as pltpu

_MXU_DT = jnp.bfloat16
_F32 = jnp.float32
_VMEM_LIMIT = 56 * 1024 * 1024
_TT_MIX = 256
_TT_MIX_FWD = 512
_TT_FFN = 256
_TK_WGRAD = 2048
_HALO = 32

_LRU_C = 8.0
_LN_EPS = 1e-5
_N_HEADS = 8
_DEPTH = 1
_ALPHA = (2 * _DEPTH) ** 0.25
_ADAM_LR, _ADAM_B1, _ADAM_B2, _ADAM_EPS, _ADAM_WD, _ADAM_STEP = 0.001, 0.9, 0.999, 1e-08, 0.01, 10

_MESH = pl.DeviceIdType.MESH
_CHIP_DELTAS = ((1, 0), (0, 1), (1, 1))


def _cparams(sem):
    return pltpu.CompilerParams(dimension_semantics=sem, vmem_limit_bytes=_VMEM_LIMIT)


def _resident(shape):
    nd = len(shape)
    return pl.BlockSpec(shape, lambda *_: (0,) * nd, pipeline_mode=pl.Buffered(1))


def _dot(a, b):
    return jnp.dot(a, b, preferred_element_type=_F32)


def _dot_nt(a, b):
    return lax.dot_general(a, b, (((1,), (1,)), ((), ())), preferred_element_type=_F32)


def _dot_tn(a, b):
    return lax.dot_general(a, b, (((0,), (0,)), ((), ())), preferred_element_type=_F32)


def _mx(v):
    return v.astype(_MXU_DT)


def _expm1(v):
    series = v * (1.0 + v * (1.0 / 2 + v * (1.0 / 6 + v * (1.0 / 24 + v * (1.0 / 120)))))
    return jnp.where(jnp.abs(v) < 0.0625, series, jnp.exp(v) - 1.0)


def _softplus(z):
    e = jnp.exp(-jnp.abs(z))
    u = 1.0 + e
    log1p = jnp.where(u == 1.0, e, jnp.log(u) * e / jnp.where(u == 1.0, 1.0, u - 1.0))
    return jnp.maximum(z, 0.0) + log1p


_GELU_C = math.sqrt(2.0 / math.pi)


def _gelu_and_grad(v):
    t = jnp.tanh(_GELU_C * (v + 0.044715 * v * v * v))
    val = 0.5 * v * (1.0 + t)
    grad = 0.5 * (1.0 + t) + 0.5 * v * (1.0 - t * t) * _GELU_C * (1.0 + 3 * 0.044715 * v * v)
    return val, grad


def _seg_sum(v, seg, passes=3):
    hi = v.astype(jnp.bfloat16)
    r1 = v - hi.astype(_F32)
    mid = r1.astype(jnp.bfloat16)
    out = _dot(hi, seg) + _dot(mid, seg)
    if passes == 3:
        out = out + _dot((r1 - mid.astype(_F32)).astype(jnp.bfloat16), seg)
    return out


def _scan_fwd(a, u, h0):
    n = a.shape[0]
    row = lax.broadcasted_iota(jnp.int32, a.shape, 0)
    h, d = u, 1
    while d < n:
        keep = row >= d
        h = a * jnp.where(keep, pltpu.roll(h, d, 0), 0.0) + h
        a = a * jnp.where(keep, pltpu.roll(a, d, 0), 1.0)
        d *= 2
    return h + a * h0


def _scan_rev(c, g, g_end):
    n = c.shape[0]
    row = lax.broadcasted_iota(jnp.int32, c.shape, 0)
    d = 1
    while d < n:
        keep = row < n - d
        g = c * jnp.where(keep, pltpu.roll(g, n - d, 0), 0.0) + g
        c = c * jnp.where(keep, pltpu.roll(c, n - d, 0), 1.0)
        d *= 2
    return g + c * g_end


def _layer_norm_stats(z):
    mu = jnp.mean(z, axis=-1, keepdims=True)
    zc = z - mu
    var = jnp.mean(zc * zc, axis=-1, keepdims=True)
    rstd = lax.rsqrt(var + _LN_EPS)
    return zc * rstd, rstd


def _layer_norm_bwd(dn, n, rstd):
    return rstd * (dn - jnp.mean(dn, axis=-1, keepdims=True) - n * jnp.mean(dn * n, axis=-1, keepdims=True))


def _rowsum(v):
    return jnp.sum(v, axis=0, keepdims=True)


def _fused_exchange(body, n_in, n_out, n_scratch, n_xin, n_xout, plan, grid):
    def wrapped(*refs):
        o0 = n_in + n_xin
        s0 = o0 + n_out + n_xout
        start, finish = plan(refs[n_in:o0], refs[o0 + n_out:s0], *refs[s0 + n_scratch:])
        step = 0
        for axis, size in enumerate(grid):
            step = step * size + pl.program_id(axis)

        @pl.when(step == 0)
        def _():
            start()

        body(*refs[:n_in], *refs[o0:o0 + n_out], *refs[s0:s0 + n_scratch])

        @pl.when(step == math.prod(grid) - 1)
        def _():
            finish()

    return wrapped


def _lru_gates(xc, wr_ref, wi_ref, br_ref, bi_ref, lam_ref):
    xcb = _mx(xc)
    r = jax.nn.sigmoid(_dot(xcb, wr_ref[...]) + br_ref[...])
    i = jax.nn.sigmoid(_dot(xcb, wi_ref[...]) + bi_ref[...])
    sp = _softplus(-lam_ref[...])
    log_a = -_LRU_C * r * sp
    a = jnp.exp(log_a)
    mult = jnp.sqrt(-_expm1(2.0 * log_a))
    return r, i, sp, a, mult


def _conv_taps(ext_ref, w_ref, first, n_taps, tt):
    acc = w_ref[0:1, :] * ext_ref[pl.ds(first, tt), :]
    for k in range(1, n_taps):
        acc = acc + w_ref[k:k + 1, :] * ext_ref[pl.ds(first + k, tt), :]
    return acc


def _make_shifted(ext_ref, sh_ref):
    n = sh_ref.shape[1]
    for r in range(1, 8):
        sh_ref[r - 1] = ext_ref[pl.ds(r, n), :]


def _tap(ext_ref, sh_ref, off, tt):
    base = (off // 8) * 8
    if off % 8 == 0:
        return ext_ref[pl.ds(base, tt), :]
    return sh_ref[off % 8 - 1, pl.ds(base, tt), :]


def _conv_taps_shifted(ext_ref, sh_ref, w_ref, first, n_taps, tt):
    acc = w_ref[0:1, :] * _tap(ext_ref, sh_ref, first, tt)
    for k in range(1, n_taps):
        acc = acc + w_ref[k:k + 1, :] * _tap(ext_ref, sh_ref, first + k, tt)
    return acc


def _mix_fwd(x, mod3, win, lcw, lcb, wr_bd, wi_bd, b_r, b_i, lam, cw, cb, ng, nb, seg, wout, ln1g, ln1b, shards):
    bl, s_len, d = x.shape
    w = d // 2
    tt = min(_TT_MIX_FWD, s_len)
    ns = s_len // tt
    kc = cw.shape[0]

    def body(x_ref, mod_ref, win_ref, lcw_ref, lcb_ref, wr_ref, wi_ref, br_ref, bi_ref, lam_ref, cw_ref, cb_ref,
             ng_ref, nb_ref, seg_ref, wout_ref, g1_ref, b1_ref,
             proj_ref, h_ref, mix_ref, x1_ref, u1_ref, y_ref, vbc_ref, lru_ref, ext4, ext31, sh31, hcar):
        @pl.when(pl.program_id(1) == 0)
        def _():
            ext4[0:8, :] = jnp.zeros((8, w), _F32)
            ext31[0:_HALO, :] = jnp.zeros((_HALO, w), _F32)
            hcar[...] = jnp.zeros_like(hcar)

        xt = x_ref[...]
        sh1, sc1, gt1 = mod_ref[:, 0:d], mod_ref[:, d:2 * d], mod_ref[:, 2 * d:3 * d]
        u1 = _mx(xt * (1.0 + sc1) + sh1)
        u1_ref[...] = u1
        xa, ga, vb, gb = (_dot(u1, win_ref[k]) for k in range(4))
        proj_ref[:, 0:w] = xa
        proj_ref[:, w:2 * w] = ga
        proj_ref[:, 2 * w:3 * w] = vb
        proj_ref[:, 3 * w:4 * w] = gb

        ext4[8:8 + tt, :] = xa
        xc = lcb_ref[...] + _conv_taps(ext4, lcw_ref, 5, 4, tt)
        ext4[0:8, :] = xa[tt - 8:tt, :]
        r, i, sp, a, mult = _lru_gates(xc, wr_ref, wi_ref, br_ref, bi_ref, lam_ref)
        for k, val in enumerate((xc, r, i, a, mult)):
            lru_ref[:, k * w:(k + 1) * w] = val
        h = _scan_fwd(a, mult * (i * xc), hcar[0:1, :])
        hcar[0:1, :] = h[tt - 1:tt, :]
        h_ref[...] = h
        gelu, _ = _gelu_and_grad(ga)
        y_ref[:, 0:w] = _mx(gelu * h)

        vbg = vb * jax.nn.sigmoid(gb)
        ext31[_HALO:_HALO + tt, :] = vbg
        _make_shifted(ext31, sh31)
        vbc = cb_ref[...] + _conv_taps_shifted(ext31, sh31, cw_ref, _HALO - (kc - 1), kc, tt)
        vbc_ref[...] = vbc
        ext31[0:_HALO, :] = vbg[tt - _HALO:tt, :]
        inv = 1.0 / (w // _N_HEADS)
        zc = vbc - _seg_sum(vbc, seg_ref[...]) * inv
        n = zc * lax.rsqrt(_seg_sum(zc * zc, seg_ref[...]) * inv + _LN_EPS)
        pre = n * ng_ref[...] + nb_ref[...]
        y_ref[:, w:2 * w] = _mx(pre * jax.nn.sigmoid(pre))

        mix = _dot(y_ref[...], wout_ref[...])
        mix_ref[...] = mix
        n1, _ = _layer_norm_stats(_ALPHA * xt + (1.0 + gt1) * mix)
        x1_ref[...] = n1 * g1_ref[...] + b1_ref[...]

    tok = lambda c: pl.BlockSpec((None, tt, c), lambda b, s: (b, s, 0))
    smalls = [lcw, lcb, wr_bd, wi_bd, b_r, b_i, lam, cw, cb, ng, nb, seg, wout, ln1g, ln1b]
    nx = len(shards)
    return pl.pallas_call(
        _fused_exchange(body, 3 + len(smalls), 8, 4, nx, nx, _gather_plan, (bl, ns)), grid=(bl, ns),
        in_specs=[tok(d), pl.BlockSpec((None, 1, 6 * d), lambda b, s: (b, 0, 0)), _resident(win.shape)]
        + [_resident(t.shape) for t in smalls] + [_HBM] * nx,
        out_specs=[tok(4 * w), tok(w), tok(d), tok(d), tok(d), tok(d), tok(w), tok(5 * w)] + [_HBM] * nx,
        out_shape=[jax.ShapeDtypeStruct((bl, s_len, 4 * w), _F32), jax.ShapeDtypeStruct((bl, s_len, w), _F32),
                   jax.ShapeDtypeStruct((bl, s_len, d), _F32), jax.ShapeDtypeStruct((bl, s_len, d), _F32),
                   jax.ShapeDtypeStruct((bl, s_len, d), _MXU_DT), jax.ShapeDtypeStruct((bl, s_len, d), _MXU_DT),
                   jax.ShapeDtypeStruct((bl, s_len, w), _F32), jax.ShapeDtypeStruct((bl, s_len, 5 * w), _F32)]
        + [jax.ShapeDtypeStruct((4,) + t.shape, t.dtype) for t in shards],
        scratch_shapes=[pltpu.VMEM((tt + 8, w), _F32), pltpu.VMEM((tt + _HALO, w), _F32),
                        pltpu.VMEM((7, tt + _HALO - 8, w), _F32), pltpu.VMEM((8, w), _F32)] + _gather_sems(nx),
        compiler_params=_cparams(("arbitrary", "arbitrary")), name="mix_fwd",
    )(x, mod3, win, *smalls, *shards)


def _ffn_fwd(x1, mod3, wup, fcw, fcb, wdn, ln2g, ln2b, target):
    bl, s_len, d = x1.shape
    nch, _, fc = wup.shape
    nch //= 2
    f = nch * fc
    tt = min(_TT_FFN, s_len)
    ns = s_len // tt

    def body(x1_ref, mod_ref, wup_ref, fcw_ref, fcb_ref, wdn_ref, g2_ref, b2_ref, tgt_ref,
             u2_ref, hh_ref, f_ref, gc_ref, dz2_ref, loss_ref, dln2_ref, dgt2_ref, ext3):
        first_tile = pl.program_id(1) == 0

        @pl.when(first_tile)
        def _():
            ext3[:, 0:8, :] = jnp.zeros((nch, 8, fc), _F32)
            dgt2_ref[...] = jnp.zeros_like(dgt2_ref)

        @pl.when(first_tile & (pl.program_id(0) == 0))
        def _():
            loss_ref[...] = jnp.zeros_like(loss_ref)
            dln2_ref[...] = jnp.zeros_like(dln2_ref)

        x1t = x1_ref[...]
        sh2, sc2, gt2 = mod_ref[:, 3 * d:4 * d], mod_ref[:, 4 * d:5 * d], mod_ref[:, 5 * d:6 * d]
        u2 = _mx(x1t * (1.0 + sc2) + sh2)
        u2_ref[...] = u2
        y2 = jnp.zeros((tt, d), _F32)
        for j in range(nch):
            lanes = slice(j * fc, (j + 1) * fc)
            v = _dot(u2, wup_ref[j])
            g = _dot(u2, wup_ref[nch + j])
            hh_ref[:, lanes] = v.astype(hh_ref.dtype)
            hh_ref[:, f + j * fc:f + (j + 1) * fc] = g.astype(hh_ref.dtype)
            ext = ext3.at[j]
            ext[8:8 + tt, :] = g
            gc = fcb_ref[:, lanes] + sum(fcw_ref[k:k + 1, lanes] * ext[pl.ds(6 + k, tt), :] for k in range(3))
            gc_ref[:, lanes] = gc
            ext[0:8, :] = g[tt - 8:tt, :]
            fj = _mx(gc * jax.nn.sigmoid(gc) * v)
            f_ref[:, lanes] = fj
            y2 = y2 + _dot(fj, wdn_ref[lanes, :])

        n2, rstd = _layer_norm_stats(_ALPHA * x1t + (1.0 + gt2) * y2)
        err = n2 * g2_ref[...] + b2_ref[...] - tgt_ref[...]
        loss_ref[...] += jnp.sum(_rowsum(err * err), axis=1, keepdims=True)
        dout = err * (1.0 / d)
        dln2_ref[0:1, :] += _rowsum(dout * n2)
        dln2_ref[1:2, :] += _rowsum(dout)
        dz2 = _layer_norm_bwd(dout * g2_ref[...], n2, rstd)
        dz2_ref[...] = dz2
        dgt2_ref[...] += _rowsum(dz2 * y2)

    tok = lambda c: pl.BlockSpec((None, tt, c), lambda b, s: (b, s, 0))
    acc = lambda r: pl.BlockSpec((r, d), lambda b, s: (0, 0))
    smalls = [fcw, fcb, wdn, ln2g, ln2b]
    return pl.pallas_call(
        body, grid=(bl, ns),
        in_specs=[tok(d), pl.BlockSpec((None, 1, 6 * d), lambda b, s: (b, 0, 0)), _resident(wup.shape)]
        + [_resident(t.shape) for t in smalls] + [tok(d)],
        out_specs=[tok(d), tok(2 * f), tok(f), tok(f), tok(d), acc(1), acc(2), pl.BlockSpec((None, 1, d), lambda b, s: (b, 0, 0))],
        out_shape=[jax.ShapeDtypeStruct((bl, s_len, d), _MXU_DT), jax.ShapeDtypeStruct((bl, s_len, 2 * f), _F32),
                   jax.ShapeDtypeStruct((bl, s_len, f), _MXU_DT), jax.ShapeDtypeStruct((bl, s_len, f), _F32),
                   jax.ShapeDtypeStruct((bl, s_len, d), _F32), jax.ShapeDtypeStruct((1, d), _F32), jax.ShapeDtypeStruct((2, d), _F32),
                   jax.ShapeDtypeStruct((bl, 1, d), _F32)],
        scratch_shapes=[pltpu.VMEM((nch, tt + 8, fc), _F32)],
        compiler_params=_cparams(("arbitrary", "arbitrary")), name="ffn_fwd",
    )(x1, mod3, wup, *smalls, target)


def _ffn_bwd(dz2, x1, hh, gc_all, mod3, wup, wdn, fcw, fcb):
    bl, s_len, d = x1.shape
    nch, _, fc = wup.shape
    nch //= 2
    f = nch * fc
    tt = min(_TT_FFN, s_len)
    ns = s_len // tt

    def body(dz2_ref, x1_ref, hh_ref, gc_ref, mod_ref, wup_ref, wdn_ref, fcw_ref, fcb_ref,
             dx1_ref, dy2_ref, dh_ref, dfc_ref, dmod_ref, dext, dcar):
        @pl.when(pl.program_id(1) == 0)
        def _():
            dcar[...] = jnp.zeros_like(dcar)
            dmod_ref[...] = jnp.zeros_like(dmod_ref)

        @pl.when((pl.program_id(1) == 0) & (pl.program_id(0) == 0))
        def _():
            dfc_ref[...] = jnp.zeros_like(dfc_ref)

        sc2, gt2 = mod_ref[:, 4 * d:5 * d], mod_ref[:, 5 * d:6 * d]
        dz2t = dz2_ref[...]
        dy2 = _mx((1.0 + gt2) * dz2t)
        dy2_ref[...] = dy2
        du2 = jnp.zeros((tt, d), _F32)
        for j in range(nch):
            lanes = slice(j * fc, (j + 1) * fc)
            glanes = slice(f + j * fc, f + (j + 1) * fc)
            v = hh_ref[:, lanes].astype(_F32)
            g = hh_ref[:, glanes].astype(_F32)
            gc = gc_ref[:, lanes]
            sg = jax.nn.sigmoid(gc)
            df = _dot_nt(dy2, wdn_ref[lanes, :])
            dv = df * (gc * sg)
            dgc = df * v * (sg * (1.0 + gc * (1.0 - sg)))
            dfc_ref[3:4, lanes] += _rowsum(dgc)
            dext[0:tt, :] = dgc
            dext[tt:tt + 8, :] = dcar[j]
            dcar[j] = dgc[0:8, :]
            dg = jnp.zeros((tt, fc), _F32)
            for k in range(3):
                shifted = dext[pl.ds(2 - k, tt), :]
                dg = dg + fcw_ref[k:k + 1, lanes] * shifted
                dfc_ref[k:k + 1, lanes] += _rowsum(shifted * g)
            dvb, dgb = _mx(dv), _mx(dg)
            dh_ref[:, lanes] = dvb
            dh_ref[:, glanes] = dgb
            du2 = du2 + _dot_nt(dvb, wup_ref[j]) + _dot_nt(dgb, wup_ref[nch + j])

        dx1_ref[...] = _ALPHA * dz2t + du2 * (1.0 + sc2)
        dmod_ref[0:1, :] += _rowsum(du2)
        dmod_ref[1:2, :] += _rowsum(du2 * x1_ref[...])

    tok = lambda c: pl.BlockSpec((None, tt, c), lambda b, i: (b, ns - 1 - i, 0))
    return pl.pallas_call(
        body, grid=(bl, ns),
        in_specs=[tok(d), tok(d), tok(2 * f), tok(f), pl.BlockSpec((None, 1, 6 * d), lambda b, i: (b, 0, 0)),
                  _resident(wup.shape), _resident(wdn.shape), _resident(fcw.shape), _resident(fcb.shape)],
        out_specs=[tok(d), tok(d), tok(2 * f), pl.BlockSpec((4, f), lambda b, i: (0, 0)),
                   pl.BlockSpec((None, 2, d), lambda b, i: (b, 0, 0))],
        out_shape=[jax.ShapeDtypeStruct((bl, s_len, d), _F32), jax.ShapeDtypeStruct((bl, s_len, d), _MXU_DT),
                   jax.ShapeDtypeStruct((bl, s_len, 2 * f), _MXU_DT), jax.ShapeDtypeStruct((4, f), _F32),
                   jax.ShapeDtypeStruct((bl, 2, d), _F32)],
        scratch_shapes=[pltpu.VMEM((tt + 8, fc), _F32), pltpu.VMEM((nch, 8, fc), _F32)],
        compiler_params=_cparams(("arbitrary", "arbitrary")), name="ffn_bwd",
    )(dz2, x1, hh, gc_all, mod3, wup, wdn, fcw, fcb)


def _mix_bwd(dx1, x, mix, proj, h, vbc, lru, mod3, win, lcw, lcb, wr_bd, wi_bd, b_r, b_i, lam, cw, cb, ng, nb, seg, wout, ln1g, chip_sums):
    bl, s_len, d = x.shape
    w = d // 2
    tt = min(_TT_MIX, s_len)
    ns = s_len // tt
    kc = cw.shape[0]

    def body(dx1_ref, x_ref, mix_ref, proj_ref, phalo_ref, h_ref, hhalo_ref, vbc_ref, lru_ref, mod_ref, win_ref, lcw_ref, lcb_ref,
             wr_ref, wi_ref, br_ref, bi_ref, lam_ref, cw_ref, cb_ref, ng_ref, nb_ref, seg_ref, wout_ref, g1_ref,
             gx_ref, dproj_ref, dmix_ref, xcg_ref, vecw_ref, dlcw_ref, dcw_ref, dln1_ref, dmod_ref,
             ext4, ext31, dext4, dext31, sh31, dsh31, car4, car31, gcar):
        s = ns - 1 - pl.program_id(1)
        first = s == 0

        @pl.when(pl.program_id(1) == 0)
        def _():
            car4[...] = jnp.zeros_like(car4)
            car31[...] = jnp.zeros_like(car31)
            gcar[...] = jnp.zeros_like(gcar)
            dmod_ref[...] = jnp.zeros_like(dmod_ref)

        @pl.when((pl.program_id(1) == 0) & (pl.program_id(0) == 0))
        def _():
            for ref in (vecw_ref, dlcw_ref, dcw_ref, dln1_ref):
                ref[...] = jnp.zeros_like(ref)

        xt, mixt = x_ref[...], mix_ref[...]
        sh1, sc1, gt1 = mod_ref[:, 0:d], mod_ref[:, d:2 * d], mod_ref[:, 2 * d:3 * d]

        n1, rstd1 = _layer_norm_stats(_ALPHA * xt + (1.0 + gt1) * mixt)
        dx1t = dx1_ref[...]
        dln1_ref[0:1, :] += _rowsum(dx1t * n1)
        dln1_ref[1:2, :] += _rowsum(dx1t)
        dz1 = _layer_norm_bwd(dx1t * g1_ref[...], n1, rstd1)
        dmod_ref[2:3, :] += _rowsum(dz1 * mixt)
        dmix = _mx((1.0 + gt1) * dz1)
        dmix_ref[...] = dmix
        dya = _dot_nt(dmix, wout_ref[0:w, :])
        dyb = _dot_nt(dmix, wout_ref[w:2 * w, :])

        xa, ga = proj_ref[:, 0:w], proj_ref[:, w:2 * w]
        vb, gb = proj_ref[:, 2 * w:3 * w], proj_ref[:, 3 * w:4 * w]

        sgb = jax.nn.sigmoid(gb)
        vbg = vb * sgb
        hv, hg = phalo_ref[:, 2 * w:3 * w], phalo_ref[:, 3 * w:4 * w]
        ext31[0:_HALO, :] = jnp.where(first, 0.0, hv * jax.nn.sigmoid(hg))
        ext31[_HALO:_HALO + tt, :] = vbg
        _make_shifted(ext31, sh31)
        vbc = vbc_ref[...]
        inv = 1.0 / (w // _N_HEADS)
        zc = vbc - _seg_sum(vbc, seg_ref[...]) * inv
        rstd = lax.rsqrt(_seg_sum(zc * zc, seg_ref[...]) * inv + _LN_EPS)
        n = zc * rstd
        pre = n * ng_ref[...] + nb_ref[...]
        sgp = jax.nn.sigmoid(pre)
        dpre = dyb * (sgp * (1.0 + pre * (1.0 - sgp)))
        vecw_ref[5:6, :] += _rowsum(dpre * n)
        vecw_ref[6:7, :] += _rowsum(dpre)
        dn = dpre * ng_ref[...]
        dvbc = rstd * (dn - _seg_sum(dn, seg_ref[...], 2) * inv - n * (_seg_sum(dn * n, seg_ref[...], 2) * inv))
        vecw_ref[4:5, :] += _rowsum(dvbc)
        dext31[0:tt, :] = dvbc
        dext31[tt:tt + _HALO, :] = car31[...]
        car31[...] = dvbc[0:_HALO, :]
        _make_shifted(dext31, dsh31)
        dvbg = jnp.zeros((tt, w), _F32)
        for k in range(kc):
            dvbg = dvbg + cw_ref[k:k + 1, :] * _tap(dext31, dsh31, kc - 1 - k, tt)
            dcw_ref[k:k + 1, :] += _rowsum(dvbc * _tap(ext31, sh31, _HALO - (kc - 1) + k, tt))
        dproj_ref[:, 2 * w:3 * w] = _mx(dvbg * sgb)
        dproj_ref[:, 3 * w:4 * w] = _mx(dvbg * vb * (sgb * (1.0 - sgb)))

        ext4[0:8, :] = jnp.where(first, 0.0, phalo_ref[_HALO - 8:_HALO, 0:w])
        ext4[8:8 + tt, :] = xa
        xc, r, i, a, mult = (lru_ref[:, k * w:(k + 1) * w] for k in range(5))
        xcg_ref[:, 0:w] = _mx(xc)
        sp = _softplus(-lam_ref[...])
        ht = h_ref[...]
        row = lax.broadcasted_iota(jnp.int32, (tt, w), 0)
        h_before = jnp.where(first, 0.0, hhalo_ref[7:8, :])
        hprev = jnp.where(row == 0, h_before, pltpu.roll(ht, 1, 0))
        gelu, dgelu = _gelu_and_grad(ga)
        dproj_ref[:, w:2 * w] = _mx(dya * ht * dgelu)
        dh = dya * gelu
        coef = jnp.where(row == tt - 1, 1.0, pltpu.roll(a, tt - 1, 0))
        big_g = _scan_rev(coef, dh, gcar[0:1, :])
        gcar[0:1, :] = a[0:1, :] * big_g[0:1, :]
        da = big_g * hprev
        ixc = i * xc
        dlog_a = da * a - (big_g * ixc) * (a * a / mult)
        di = big_g * mult * xc
        dxc = big_g * mult * i
        vecw_ref[3:4, :] += _rowsum(dlog_a * r) * (_LRU_C * jax.nn.sigmoid(-lam_ref[...]))
        dgr_f = dlog_a * (-_LRU_C * sp) * (r * (1.0 - r))
        dgi_f = di * (i * (1.0 - i))
        vecw_ref[1:2, :] += _rowsum(dgr_f)
        vecw_ref[2:3, :] += _rowsum(dgi_f)
        dgr, dgi = _mx(dgr_f), _mx(dgi_f)
        xcg_ref[:, w:2 * w] = dgr
        xcg_ref[:, 2 * w:3 * w] = dgi
        dxc = dxc + _dot_nt(dgr, wr_ref[...]) + _dot_nt(dgi, wi_ref[...])
        vecw_ref[0:1, :] += _rowsum(dxc)
        dext4[0:tt, :] = dxc
        dext4[tt:tt + 8, :] = car4[...]
        car4[...] = dxc[0:8, :]
        dxa = jnp.zeros((tt, w), _F32)
        for k in range(4):
            dxa = dxa + lcw_ref[k:k + 1, :] * dext4[pl.ds(3 - k, tt), :]
            dlcw_ref[k:k + 1, :] += _rowsum(dxc * ext4[pl.ds(5 + k, tt), :])
        dproj_ref[:, 0:w] = _mx(dxa)

        du1 = sum(_dot_nt(dproj_ref[:, k * w:(k + 1) * w], win_ref[k]) for k in range(4))
        gx_ref[...] = _ALPHA * dz1 + du1 * (1.0 + sc1)
        dmod_ref[0:1, :] += _rowsum(du1)
        dmod_ref[1:2, :] += _rowsum(du1 * xt)

    tok = lambda c: pl.BlockSpec((None, tt, c), lambda b, i: (b, ns - 1 - i, 0))
    halo = lambda rows, c: pl.BlockSpec(
        (None, rows, c), lambda b, i: (b, jnp.maximum((ns - 1 - i) * (tt // rows) - 1, 0), 0))
    accw = lambda r, c: pl.BlockSpec((r, c), lambda b, i: (0, 0))
    smalls = [lcw, lcb, wr_bd, wi_bd, b_r, b_i, lam, cw, cb, ng, nb, seg, wout, ln1g]
    nx = len(chip_sums)
    return pl.pallas_call(
        _fused_exchange(body, 11 + len(smalls), 9, 9, nx, nx, _chip_reduce_plan, (bl, ns)), grid=(bl, ns),
        in_specs=[tok(d), tok(d), tok(d), tok(4 * w), halo(_HALO, 4 * w), tok(w), halo(8, w), tok(w), tok(5 * w),
                  pl.BlockSpec((None, 1, 6 * d), lambda b, i: (b, 0, 0)), _resident(win.shape)]
        + [_resident(t.shape) for t in smalls] + [_HBM] * nx,
        out_specs=[tok(d), tok(4 * w), tok(d), tok(3 * w), accw(8, w), accw(4, w), accw(kc, w), accw(2, d),
                   pl.BlockSpec((None, 3, d), lambda b, i: (b, 0, 0))] + [_HBM] * nx,
        out_shape=[jax.ShapeDtypeStruct((bl, s_len, d), _F32), jax.ShapeDtypeStruct((bl, s_len, 4 * w), _MXU_DT),
                   jax.ShapeDtypeStruct((bl, s_len, d), _MXU_DT), jax.ShapeDtypeStruct((bl, s_len, 3 * w), _MXU_DT),
                   jax.ShapeDtypeStruct((8, w), _F32), jax.ShapeDtypeStruct((4, w), _F32),
                   jax.ShapeDtypeStruct((kc, w), _F32), jax.ShapeDtypeStruct((2, d), _F32),
                   jax.ShapeDtypeStruct((bl, 3, d), _F32)]
        + [jax.ShapeDtypeStruct((3,) + t.shape[1:], t.dtype) for t in chip_sums],
        scratch_shapes=[pltpu.VMEM((tt + 8, w), _F32), pltpu.VMEM((tt + _HALO, w), _F32),
                        pltpu.VMEM((tt + 8, w), _F32), pltpu.VMEM((tt + _HALO, w), _F32),
                        pltpu.VMEM((7, tt + _HALO - 8, w), _F32), pltpu.VMEM((7, tt + _HALO - 8, w), _F32),
                        pltpu.VMEM((8, w), _F32), pltpu.VMEM((_HALO, w), _F32), pltpu.VMEM((8, w), _F32)]
        + _chip_reduce_sems(nx),
        compiler_params=_cparams(("arbitrary", "arbitrary")), name="mix_bwd",
    )(dx1, x, mix, proj, proj, h, h, vbc, lru, mod3, win, *smalls, *chip_sums)


def _wgrad(a, b, ma, nbw, na, nb, a_off, b_off, name, exchange=None):
    t = a.shape[0]
    tk = min(_TK_WGRAD, t)
    grid = (na * nb, t // tk)

    def body(a_ref, b_ref, o_ref):
        @pl.when(pl.program_id(1) == 0)
        def _():
            o_ref[...] = jnp.zeros_like(o_ref)
        o_ref[...] += _dot_tn(a_ref[...], b_ref[...])

    xin, xshapes, plan, sems = exchange if exchange else ([], [], None, [])
    nx = len(xin)
    res = pl.pallas_call(
        _fused_exchange(body, 2, 1, 0, nx, len(xshapes), plan, grid) if exchange else body, grid=grid,
        in_specs=[pl.BlockSpec((tk, ma), lambda j, k: (k, j // nb + a_off)),
                  pl.BlockSpec((tk, nbw), lambda j, k: (k, j % nb + b_off))] + [_HBM] * nx,
        out_specs=[pl.BlockSpec((None, ma, nbw), lambda j, k: (j, 0, 0))] + [_HBM] * len(xshapes),
        out_shape=[jax.ShapeDtypeStruct((na * nb, ma, nbw), _F32)] + list(xshapes),
        scratch_shapes=list(sems),
        compiler_params=_cparams(("arbitrary", "arbitrary")), name=name,
    )(a, b, *xin)
    return res if exchange else res[0]


_DEV_DELTAS = tuple(dl for dl in itertools.product((0, 1), repeat=3) if any(dl))
_HBM = pl.BlockSpec(memory_space=pltpu.HBM)
_VMEM = pl.BlockSpec(memory_space=pltpu.VMEM)


def _pos():
    return lax.axis_index("x"), lax.axis_index("y"), lax.axis_index("c")


def _flip(v, delta):
    return 1 - v if delta else v


def _remote(src, dst, ssem, rsem, dev):
    return pltpu.make_async_remote_copy(src_ref=src, dst_ref=dst, send_sem=ssem, recv_sem=rsem,
                                        device_id=dev, device_id_type=_MESH)


def _rows(ref, idx, n):
    return ref.at[pl.ds(pl.multiple_of(idx * n, 8), n)]


def _ada_fwd(c8, w_ada_k, b_ada_k, shards):
    rows, d = c8.shape
    nk = w_ada_k.shape[1]
    n = len(shards)

    def body(*refs):
        c_ref, w_ref, b_ref = refs[:3]
        call_ref, mod_ref = refs[3 + n:5 + n]
        modloc, modrcv, s1, r1, s2, r2 = refs[5 + 2 * n:11 + 2 * n]
        gather_start, gather_finish = _gather_plan(refs[3:3 + n], refs[5 + n:5 + 2 * n], *refs[11 + 2 * n:14 + 2 * n],
                                                   fsem=refs[14 + 2 * n], frsem=refs[15 + 2 * n], bounce=refs[16 + 2 * n:])
        gather_start()
        xi, yi, ci = _pos()
        me, kme = 4 * xi + 2 * yi + ci, 2 * xi + yi
        call_ref[pl.ds(pl.multiple_of(me * rows, 8), rows), :] = c_ref[...]
        sends = []
        for p, (dx, dy, dc) in enumerate(_DEV_DELTAS):
            cp = _remote(c_ref, _rows(call_ref, me, rows), s1.at[p], r1.at[p], (_flip(xi, dx), _flip(yi, dy), _flip(ci, dc)))
            cp.start()
            sends.append(cp)
        for p, (dx, dy, dc) in enumerate(_DEV_DELTAS):
            src = 4 * _flip(xi, dx) + 2 * _flip(yi, dy) + _flip(ci, dc)
            _remote(c_ref, _rows(call_ref, src, rows), s1.at[p], r1.at[p], (xi, yi, ci)).wait_recv()
        for cp in sends:
            cp.wait_send()

        ca = call_ref[...]
        modloc[...] = _dot(_mx(ca * jax.nn.sigmoid(ca)), _mx(w_ref[...])) + b_ref[...]
        modrcv[kme] = modloc[pl.ds(pl.multiple_of(me * rows, 8), rows), :]
        sends = []
        for j, (dx, dy) in enumerate(_CHIP_DELTAS):
            tx, ty = _flip(xi, dx), _flip(yi, dy)
            cp = _remote(_rows(modloc, 4 * tx + 2 * ty + ci, rows), modrcv.at[kme], s2.at[j], r2.at[j], (tx, ty, ci))
            cp.start()
            sends.append(cp)
        for j, (dx, dy) in enumerate(_CHIP_DELTAS):
            ksrc = 2 * _flip(xi, dx) + _flip(yi, dy)
            _remote(_rows(modloc, me, rows), modrcv.at[ksrc], s2.at[j], r2.at[j], (xi, yi, ci)).wait_recv()
        for cp in sends:
            cp.wait_send()
        for j in range(4):
            mod_ref[:, j * nk:(j + 1) * nk] = modrcv[j]
        gather_finish()

    return pl.pallas_call(
        body, in_specs=[_VMEM, _VMEM, _VMEM] + [_HBM] * n, out_specs=[_VMEM, _VMEM] + [_HBM] * n,
        out_shape=[jax.ShapeDtypeStruct((8 * rows, d), _F32), jax.ShapeDtypeStruct((rows, 4 * nk), _F32)]
        + [jax.ShapeDtypeStruct((4,) + a.shape, a.dtype) for a in shards],
        scratch_shapes=[pltpu.VMEM((8 * rows, nk), _F32), pltpu.VMEM((4, rows, nk), _F32),
                        pltpu.SemaphoreType.DMA((7,)), pltpu.SemaphoreType.DMA((7,)),
                        pltpu.SemaphoreType.DMA((3,)), pltpu.SemaphoreType.DMA((3,))]
        + _gather_sems(n) + [pltpu.SemaphoreType.DMA((3, n)), pltpu.SemaphoreType.DMA((3, n))]
        + [pltpu.VMEM(a.shape, a.dtype) for a in shards],
        compiler_params=pltpu.CompilerParams(vmem_limit_bytes=_VMEM_LIMIT), name="ada_fwd",
    )(c8, w_ada_k, b_ada_k, *shards)


def _gather_sems(n):
    return [pltpu.SemaphoreType.DMA((3, n)), pltpu.SemaphoreType.DMA((3, n)), pltpu.SemaphoreType.DMA((n,))]


def _gather_plan(ins, outs, ssem, rsem, lsem, bounce=(), fsem=None, frsem=None):
    n = len(ins)
    xi, yi, ci = _pos()
    kme = 2 * xi + yi
    split = [fsem is not None and ins[a].shape[0] % 32 == 0 for a in range(n)]

    def half(ref, a, which):
        r2 = ins[a].shape[0] // 2
        return ref.at[pl.ds(pl.multiple_of(which * r2, 16), r2)]

    staged = [pltpu.make_async_copy(ins[a], bounce[a], lsem.at[a]) for a in range(len(bounce))]
    local = [pltpu.make_async_copy(bounce[a] if bounce else ins[a], outs[a].at[kme], lsem.at[a]) for a in range(n)]
    sends, recvs, forwards, handed = [], [], [], []
    for j, (dx, dy) in enumerate(_CHIP_DELTAS):
        tx, ty = _flip(xi, dx), _flip(yi, dy)
        for a in range(n):
            sems = (ssem.at[j, a], rsem.at[j, a])
            landing = outs[a].at[2 * tx + ty]
            if split[a]:
                sends.append(_remote(half(ins[a], a, ci), half(outs[a].at[kme], a, ci), *sems, (tx, ty, ci)))
                recvs.append(_remote(half(ins[a], a, ci), half(landing, a, ci), *sems, (xi, yi, ci)))
                fsems = (fsem.at[j, a], frsem.at[j, a])
                forwards.append(_remote(half(landing, a, ci), half(landing, a, ci), *fsems, (xi, yi, 1 - ci)))
                handed.append(_remote(half(ins[a], a, 1 - ci), half(landing, a, 1 - ci), *fsems, (xi, yi, ci)))
            else:
                sends.append(_remote(ins[a], outs[a].at[kme], *sems, (tx, ty, ci)))
                recvs.append(_remote(ins[a], landing, *sems, (xi, yi, ci)))
                forwards.append(None)

    def start():
        for cp in sends + staged:
            cp.start()
        for cp in staged:
            cp.wait()
        for cp in local:
            cp.start()

    def finish():
        for arrived, forward in zip(recvs, forwards):
            arrived.wait_recv()
            if forward is not None:
                forward.start()
        for cp in handed:
            cp.wait_recv()
        for cp in sends + [f for f in forwards if f is not None]:
            cp.wait_send()
        for cp in local:
            cp.wait()

    return start, finish


def _dev_gather_sems(n):
    return [pltpu.SemaphoreType.DMA((7, n)), pltpu.SemaphoreType.DMA((7, n)), pltpu.SemaphoreType.DMA((n,))]


def _dev_gather_plan(ins, outs, ssem, rsem, lsem):
    n = len(ins)
    xi, yi, ci = _pos()
    me = 4 * xi + 2 * yi + ci
    local = [pltpu.make_async_copy(ins[a], outs[a].at[me], lsem.at[a]) for a in range(n)]
    sends, recvs = [], []
    for p, (dx, dy, dc) in enumerate(_DEV_DELTAS):
        tx, ty, tc = _flip(xi, dx), _flip(yi, dy), _flip(ci, dc)
        for a in range(n):
            sends.append(_remote(ins[a], outs[a].at[me], ssem.at[p, a], rsem.at[p, a], (tx, ty, tc)))
            recvs.append(_remote(ins[a], outs[a].at[4 * tx + 2 * ty + tc], ssem.at[p, a], rsem.at[p, a], (xi, yi, ci)))

    def start():
        for cp in local + sends:
            cp.start()

    def finish():
        for cp in recvs:
            cp.wait_recv()
        for cp in sends:
            cp.wait_send()
        for cp in local:
            cp.wait()

    return start, finish


def _pair_sems(n):
    return [pltpu.SemaphoreType.DMA((n,)), pltpu.SemaphoreType.DMA((n,))]


def _pair_plan(ins, outs, ssem, rsem):
    xi, yi, ci = _pos()
    sends = []
    for a in range(len(ins)):
        r2 = ins[a].shape[1] // 2
        src = ins[a].at[:, pl.ds(pl.multiple_of((1 - ci) * r2, 8), r2), :]
        sends.append(_remote(src, outs[a], ssem.at[a], rsem.at[a], (xi, yi, 1 - ci)))

    def start():
        for cp in sends:
            cp.start()

    def finish():
        for cp in sends:
            cp.wait_recv()
        for cp in sends:
            cp.wait_send()

    return start, finish


def _chip_reduce_sems(n):
    return [pltpu.SemaphoreType.DMA((3, n)), pltpu.SemaphoreType.DMA((3, n))]


def _chip_reduce_plan(ins, outs, ssem, rsem):
    xi, yi, ci = _pos()
    sends = []
    for j, (dx, dy) in enumerate(_CHIP_DELTAS):
        tx, ty = _flip(xi, dx), _flip(yi, dy)
        sends += [_remote(ins[a].at[2 * tx + ty], outs[a].at[j], ssem.at[j, a], rsem.at[j, a], (tx, ty, ci))
                  for a in range(len(ins))]

    def start():
        for cp in sends:
            cp.start()

    def finish():
        for cp in sends:
            cp.wait_recv()
        for cp in sends:
            cp.wait_send()

    return start, finish


def _pair_exchange(gs, name):
    n = len(gs)

    def body(*refs):
        start, finish = _pair_plan(refs[:n], refs[n:2 * n], *refs[2 * n:])
        start()
        finish()

    return pl.pallas_call(
        body, in_specs=[_HBM] * n, out_specs=[_HBM] * n, out_shape=_pair_out_shapes(gs),
        scratch_shapes=_pair_sems(n), name=name,
    )(*gs)


def _pair_out_shapes(gs):
    return [jax.ShapeDtypeStruct((g.shape[0], g.shape[1] // 2, g.shape[2]), g.dtype) for g in gs]


def _row_tile(r):
    return max(t for t in range(8, min(r, 256) + 1, 8) if r % t == 0)


def _pair_add(g, r, cidx, name, wire_dtype=None, pair_partials=()):
    nk, r2, c = r.shape
    tr = _row_tile(r2)
    nt = r2 // tr
    nx = len(pair_partials)

    def body(c_ref, g_ref, r_ref, *o_refs):
        s = g_ref[...] + r_ref[...]
        for o_ref in o_refs:
            o_ref[...] = s.astype(o_ref.dtype)

    out_spec = pl.BlockSpec((None, tr, c), lambda k, i, cr: (k, i, 0))
    dtypes = [_F32] + ([wire_dtype] if wire_dtype else [])
    res = pl.pallas_call(
        _fused_exchange(body, 3, len(dtypes), 0, nx, nx, _pair_plan, (nk, nt)) if nx else body,
        grid_spec=pltpu.PrefetchScalarGridSpec(
            num_scalar_prefetch=1, grid=(nk, nt),
            in_specs=[pl.BlockSpec((None, tr, c), lambda k, i, cr: (k, cr[0] * nt + i, 0)), out_spec] + [_HBM] * nx,
            out_specs=[out_spec] * len(dtypes) + [_HBM] * nx, scratch_shapes=_pair_sems(nx) if nx else []),
        out_shape=[jax.ShapeDtypeStruct(r.shape, dt) for dt in dtypes] + _pair_out_shapes(pair_partials),
        compiler_params=_cparams(("arbitrary", "arbitrary")), name=name,
    )(cidx, g, r, *pair_partials)
    return res if wire_dtype or nx else res[0]


def _chip_exchange(ss):
    n = len(ss)

    def body(*refs):
        start, finish = _chip_reduce_plan(refs[:n], refs[n:2 * n], *refs[2 * n:])
        start()
        finish()

    return pl.pallas_call(
        body, in_specs=[_HBM] * n, out_specs=[_HBM] * n,
        out_shape=[jax.ShapeDtypeStruct((3,) + s.shape[1:], s.dtype) for s in ss],
        scratch_shapes=_chip_reduce_sems(n), name="grad_chip_exchange",
    )(*ss)


def _chip_add(s, r, kidx, name):
    _, r2, c = r.shape
    tr = _row_tile(r2)

    def body(k_ref, s_ref, r_ref, o_ref):
        o_ref[...] = ((s_ref[...] + r_ref[0].astype(_F32)) + r_ref[1].astype(_F32)) + r_ref[2].astype(_F32)

    return pl.pallas_call(
        body, grid_spec=pltpu.PrefetchScalarGridSpec(
            num_scalar_prefetch=1, grid=(r2 // tr,),
            in_specs=[pl.BlockSpec((None, tr, c), lambda i, kr: (kr[0], i, 0)),
                      pl.BlockSpec((3, tr, c), lambda i, kr: (0, i, 0))],
            out_specs=pl.BlockSpec((tr, c), lambda i, kr: (i, 0))),
        out_shape=jax.ShapeDtypeStruct((r2, c), _F32),
        compiler_params=_cparams(("arbitrary",)), name=name,
    )(kidx, s, r)


def _pair_swap(hs, name):
    n = len(hs)

    def body(*refs):
        ins, outs = refs[:n], refs[n:2 * n]
        ssem, rsem = refs[2 * n:]
        xi, yi, ci = _pos()
        sends = [_remote(ins[a], outs[a], ssem.at[a], rsem.at[a], (xi, yi, 1 - ci)) for a in range(n)]
        for cp in sends:
            cp.start()
        for cp in sends:
            cp.wait_recv()
        for cp in sends:
            cp.wait_send()

    return pl.pallas_call(
        body, in_specs=[_HBM] * n, out_specs=[_HBM] * n,
        out_shape=[jax.ShapeDtypeStruct(h.shape, h.dtype) for h in hs],
        scratch_shapes=[pltpu.SemaphoreType.DMA((n,)), pltpu.SemaphoreType.DMA((n,))], name=name,
    )(*hs)


def _small_sum(every):
    def body(all_ref, sum_ref):
        tot = all_ref[0]
        for dev in range(1, 8):
            tot = tot + all_ref[dev]
        sum_ref[...] = tot

    return pl.pallas_call(
        body, in_specs=[_VMEM], out_specs=_VMEM, out_shape=jax.ShapeDtypeStruct(every.shape[1:], _F32),
        compiler_params=pltpu.CompilerParams(vmem_limit_bytes=_VMEM_LIMIT), name="small_sum",
    )(every)


def _adamw(w, g, m, v):
    m = _ADAM_B1 * m + (1.0 - _ADAM_B1) * g
    v = _ADAM_B2 * v + (1.0 - _ADAM_B2) * (g * g)
    m_hat = m / (1.0 - _ADAM_B1 ** _ADAM_STEP)
    v_hat = v / (1.0 - _ADAM_B2 ** _ADAM_STEP)
    return -_ADAM_LR * (m_hat / (jnp.sqrt(v_hat) + _ADAM_EPS) + _ADAM_WD * w), m, v


def _adamw_big(w, g_mine, g_theirs, m, v, cidx, name):
    r, c = w.shape
    tr = _row_tile(r // 2)
    nt = r // 2 // tr

    def body(c_ref, w_ref, gm_ref, gt_ref, m_ref, v_ref, g_ref, d_ref, mo_ref, vo_ref):
        g = jnp.where(pl.program_id(0) // nt == c_ref[0], gm_ref[...], gt_ref[...])
        g_ref[...] = g
        d_ref[...], mo_ref[...], vo_ref[...] = _adamw(w_ref[...], g, m_ref[...], v_ref[...])

    spec = pl.BlockSpec((tr, c), lambda i, cr: (i, 0))
    half = pl.BlockSpec((tr, c), lambda i, cr: (i % nt, 0))
    return pl.pallas_call(
        body, grid_spec=pltpu.PrefetchScalarGridSpec(
            num_scalar_prefetch=1, grid=(2 * nt,), in_specs=[spec, half, half, spec, spec], out_specs=[spec] * 4),
        out_shape=[jax.ShapeDtypeStruct((r, c), _F32)] * 4,
        compiler_params=_cparams(("arbitrary",)), name=name,
    )(cidx, w, g_mine, g_theirs, m, v)


def _adamw_small(ws, gs, ms, vs):
    n = len(ws)
    summed = [i for i in range(n) if gs[i].shape != ws[i].shape]

    def body(*refs):
        w_r, g_r, m_r, v_r = (refs[i * n:(i + 1) * n] for i in range(4))
        outs = refs[4 * n:]
        for i in range(n):
            g = g_r[i][...]
            if i in summed:
                g = _rowsum(g)
                outs[3 * n + summed.index(i)][...] = g
            outs[i][...], outs[n + i][...], outs[2 * n + i][...] = _adamw(w_r[i][...], g, m_r[i][...], v_r[i][...])

    shapes = [jax.ShapeDtypeStruct(w.shape, _F32) for w in ws]
    res = pl.pallas_call(
        body, in_specs=[_VMEM] * (4 * n), out_specs=[_VMEM] * (3 * n + len(summed)),
        out_shape=shapes * 3 + [shapes[i] for i in summed],
        compiler_params=pltpu.CompilerParams(vmem_limit_bytes=_VMEM_LIMIT), name="adamw_small",
    )(*ws, *gs, *ms, *vs)
    gs = list(gs)
    for pos, i in enumerate(summed):
        gs[i] = res[3 * n + pos]
    return gs, res[:n], res[n:2 * n], res[2 * n:3 * n]


def _ada_bwd(c_all, dmod_k, w, m, v):
    d, nk = w.shape
    tn = 512 if nk % 512 == 0 else nk

    def body(c_ref, dm_ref, w_ref, m_ref, v_ref, g_ref, d_ref, mo_ref, vo_ref):
        ca = c_ref[...]
        g = _dot_tn(_mx(ca * jax.nn.sigmoid(ca)), _mx(dm_ref[...]))
        g_ref[...] = g
        d_ref[...], mo_ref[...], vo_ref[...] = _adamw(w_ref[...], g, m_ref[...], v_ref[...])

    col = pl.BlockSpec((d, tn), lambda j: (0, j))
    return pl.pallas_call(
        body, grid=(nk // tn,),
        in_specs=[pl.BlockSpec(c_all.shape, lambda j: (0, 0)), pl.BlockSpec((c_all.shape[0], tn), lambda j: (0, j)),
                  col, col, col],
        out_specs=[col] * 4, out_shape=[jax.ShapeDtypeStruct((d, nk), _F32)] * 4,
        compiler_params=_cparams(("arbitrary",)), name="ada_bwd",
    )(c_all, dmod_k, w, m, v)


def _block_diag(wh):
    hn, dh, _ = wh.shape
    eye = jnp.eye(hn, dtype=wh.dtype)
    return (eye[:, None, :, None] * wh[:, :, None, :]).reshape(hn * dh, hn * dh)


def _pack(pieces):
    out = []
    for p in pieces:
        flat = p.reshape(-1, 128)
        out.append(jnp.pad(flat, ((0, (-flat.shape[0]) % 8), (0, 0))))
    return jnp.concatenate(out, axis=0)


def _unpack(pack, shapes):
    out, off = [], 0
    for shp in shapes:
        rows = math.prod(shp) // 128
        out.append(pack[..., off:off + rows, :].reshape(pack.shape[:-2] + tuple(shp)))
        off += rows + (-rows) % 8
    return out


_WEIGHTS = ('w_ada', 'b_ada', 'w_in', 'lru_conv_w', 'lru_conv_b', 'lru_w_r', 'lru_b_r', 'lru_w_i', 'lru_b_i', 'lru_lambda',
            'conv_w', 'conv_b', 'conv_norm_g', 'conv_norm_b', 'w_out', 'ln1_g', 'ln1_b', 'ffn_w_up', 'ffn_conv_w',
            'ffn_conv_b', 'ffn_w_down', 'ln2_g', 'ln2_b')
_BIG = ('w_in', 'w_out', 'ffn_w_up', 'ffn_w_down')


def kernel(x, c, w_ada, b_ada, w_in, lru_conv_w, lru_conv_b, lru_w_r, lru_b_r, lru_w_i, lru_b_i, lru_lambda, conv_w, conv_b, conv_norm_g, conv_norm_b, w_out, ln1_g, ln1_b, ffn_w_up, ffn_conv_w, ffn_conv_b, ffn_w_down, ln2_g, ln2_b, loss_target, m_w_ada, m_b_ada, m_w_in, m_lru_conv_w, m_lru_conv_b, m_lru_w_r, m_lru_b_r, m_lru_w_i, m_lru_b_i, m_lru_lambda, m_conv_w, m_conv_b, m_conv_norm_g, m_conv_norm_b, m_w_out, m_ln1_g, m_ln1_b, m_ffn_w_up, m_ffn_conv_w, m_ffn_conv_b, m_ffn_w_down, m_ln2_g, m_ln2_b, v_w_ada, v_b_ada, v_w_in, v_lru_conv_w, v_lru_conv_b, v_lru_w_r, v_lru_b_r, v_lru_w_i, v_lru_b_i, v_lru_lambda, v_conv_w, v_conv_b, v_conv_norm_g, v_conv_norm_b, v_w_out, v_ln1_g, v_ln1_b, v_ffn_w_up, v_ffn_conv_w, v_ffn_conv_b, v_ffn_w_down, v_ln2_g, v_ln2_b):
    given = dict(locals())
    wt = {n: given[n] for n in _WEIGHTS}
    mo = {n: given["m_" + n] for n in _WEIGHTS}
    vo = {n: given["v_" + n] for n in _WEIGHTS}
    bl, s_len, d = x.shape
    wd = d // 2
    tokens = bl * s_len
    xi, yi, ci = _pos()
    kme = 2 * xi + yi
    kidx = jnp.reshape(kme, (1,)).astype(jnp.int32)
    cidx = jnp.reshape(ci, (1,)).astype(jnp.int32)

    nk = w_ada.shape[2]
    c8 = jnp.pad(c, ((0, 8 - bl), (0, 0)))
    c_all, mod8, win, wout_s, lcw_s, cw_s, fcw_s = _ada_fwd(
        c8, w_ada[0], lax.dynamic_slice(b_ada, (0, kme * nk), (1, nk)),
        [_mx(w_in[0]), _mx(w_out[0]), lru_conv_w[0], conv_w[0], ffn_conv_w[0]])
    mod3 = mod8[:bl].reshape(bl, 1, 6 * d)
    wout = wout_s.reshape(d, d)
    f = 4 * ffn_w_down.shape[1]
    unshard = lambda t: jnp.transpose(t, (1, 0, 2)).reshape(t.shape[1], -1)
    lcw, cw, fcw = unshard(lcw_s), unshard(cw_s), unshard(fcw_s)
    wr_bd, wi_bd = _mx(_block_diag(lru_w_r[0])), _mx(_block_diag(lru_w_i[0]))
    seg = _block_diag(jnp.ones((_N_HEADS, wd // _N_HEADS, wd // _N_HEADS), jnp.bfloat16))
    mixer_small = (lcw, lru_conv_b, wr_bd, wi_bd, lru_b_r, lru_b_i, lru_lambda, cw, conv_b, conv_norm_g, conv_norm_b, seg, wout, ln1_g)

    proj, h, mix, x1, u1, y, vbc, lru, wup, wdn_s = _mix_fwd(x, mod3, win, *mixer_small, ln1_b, [_mx(ffn_w_up[0]), _mx(ffn_w_down[0])])
    wdn = wdn_s.reshape(f, d)
    u2, hh, fact, gc_all, dz2, loss_acc, dln2, dgt2 = _ffn_fwd(x1, mod3, wup, fcw, ffn_conv_b, wdn, ln2_g, ln2_b, loss_target)
    dx1, dy2, dh, dfc, dmod2 = _ffn_bwd(dz2, x1, hh, gc_all, mod3, wup, wdn, fcw, ffn_conv_b)

    flat = lambda t: t.reshape(tokens, t.shape[-1])
    fc = wup.shape[2]
    g_up = _wgrad(flat(u2), flat(dh), d, fc, 1, 4, 0, 0, "wgrad_up")
    g_dn, r_up = _wgrad(flat(fact), flat(dy2), fc, d, f // fc, 1, 0, 0, "wgrad_down",
                        exchange=([g_up], _pair_out_shapes([g_up]), _pair_plan, _pair_sems(1)))
    g_dn = g_dn.reshape(4, f // 4, d)
    s_up, r_dn = _pair_add(g_up, r_up, cidx, "grad_pair_add_ffn_w_up", pair_partials=[g_dn])
    ffn_sum = [s_up, _pair_add(g_dn, r_dn, cidx, "grad_pair_add_ffn_w_down")]
    grad_x, dproj, dmix, xcg, vecw, dlcw, dcw, dln1, dmod1, *ffn_recv = _mix_bwd(
        dx1, x, mix, proj, h, vbc, lru, mod3, win, *mixer_small, ffn_sum)
    g_ri = _wgrad(flat(xcg), flat(xcg), wd, wd, 1, 2, 0, 1, "wgrad_gates")
    dh_ = wd // _N_HEADS
    on_diagonal = jnp.eye(_N_HEADS, dtype=_F32)[None, :, None, :, None]
    g_ri = jnp.sum(g_ri.reshape(2, _N_HEADS, dh_, _N_HEADS, dh_) * on_diagonal, axis=3)

    dmod = jnp.concatenate([dmod1.reshape(bl, 3 * d), dmod2.reshape(bl, 2 * d), dgt2.reshape(bl, d)], axis=1)
    pieces = [vecw, dlcw, dcw, jnp.concatenate([dln1, dln2], axis=0), dfc, g_ri, loss_acc[:, 0:128],
              jnp.pad(dmod, ((0, 8 - bl), (0, 0)))]
    shapes = [p.shape for p in pieces]
    pack = _pack(pieces)
    g_in, every = _wgrad(flat(u1), flat(dproj), d, wd, 1, 4, 0, 0, "wgrad_in", exchange=(
        [pack], [jax.ShapeDtypeStruct((8,) + pack.shape, _F32)], _dev_gather_plan, _dev_gather_sems(1)))
    wire_shape = lambda t: [jax.ShapeDtypeStruct((3,) + t.shape[1:], t.dtype)]
    r_in, = _pair_exchange([g_in], "grad_pair_exchange_w_in")
    s_in, wire_in = _pair_add(g_in, r_in, cidx, "grad_pair_add_w_in", jnp.bfloat16)
    g_out, recv_in = _wgrad(flat(y), flat(dmix), d, d, 1, 1, 0, 0, "wgrad_out", exchange=(
        [wire_in], wire_shape(wire_in), _chip_reduce_plan, _chip_reduce_sems(1)))
    g_out = g_out.reshape(4, d // 4, d)
    r_out, = _pair_exchange([g_out], "grad_pair_exchange_w_out")
    s_out, wire_out = _pair_add(g_out, r_out, cidx, "grad_pair_add_w_out", jnp.bfloat16)
    recv_out, = _chip_exchange([wire_out])
    chip_sum, recv = [s_in, s_out] + ffn_sum, [recv_in, recv_out] + list(ffn_recv)
    half = [_chip_add(s, r, kidx, "grad_chip_add_" + n) for s, r, n in zip(chip_sum, recv, _BIG)]
    grads, deltas, new_m, new_v = {}, {}, {}, {}
    for n, mine, theirs in zip(_BIG, half, _pair_swap(half, "grad_pair_swap")):
        g, dl, mm, vv = _adamw_big(wt[n][0], mine, theirs, mo[n][0], vo[n][0], cidx, "adamw_" + n)
        grads[n], deltas[n], new_m[n], new_v[n] = g[None], dl[None], mm[None], vv[None]

    vecw, dlcw, dcw, dln, dfc, g_ri, loss_sum, dmod_sum = _unpack(_small_sum(every), shapes)
    loss = 0.5 * loss_sum[0, 0] / d
    dmod_all = _unpack(every, shapes)[-1].reshape(64, 6 * d)

    g_ada, dl, mm, vv = _ada_bwd(c_all, lax.dynamic_slice(dmod_all, (0, kme * nk), (64, nk)), w_ada[0], m_w_ada[0], v_w_ada[0])
    grads['w_ada'], deltas['w_ada'], new_m['w_ada'], new_v['w_ada'] = g_ada[None], dl[None], mm[None], vv[None]

    shard = lambda t, width: lax.dynamic_slice(t, (0, kme * width), (t.shape[0], width))
    small = {
        'b_ada': dmod_sum, 'lru_conv_w': shard(dlcw, wd // 4), 'lru_conv_b': vecw[0:1], 'lru_w_r': g_ri[0], 'lru_b_r': vecw[1:2],
        'lru_w_i': g_ri[1], 'lru_b_i': vecw[2:3], 'lru_lambda': vecw[3:4], 'conv_w': shard(dcw, wd // 4), 'conv_b': vecw[4:5],
        'conv_norm_g': vecw[5:6], 'conv_norm_b': vecw[6:7], 'ln1_g': dln[0:1], 'ln1_b': dln[1:2],
        'ffn_conv_w': shard(dfc[0:3], f // 4), 'ffn_conv_b': dfc[3:4], 'ln2_g': dln[2:3], 'ln2_b': dln[3:4]}
    names = list(small)
    gs = [small[n] if n == 'b_ada' else small[n].reshape(wt[n].shape) for n in names]
    gs, dls, mms, vvs = _adamw_small([wt[n] for n in names], gs, [mo[n] for n in names], [vo[n] for n in names])
    for n, g, dl, mm, vv in zip(names, gs, dls, mms, vvs):
        grads[n], deltas[n], new_m[n], new_v[n] = g, dl, mm, vv

    return (loss, grad_x, *[grads[n] for n in _WEIGHTS], *[deltas[n] for n in _WEIGHTS],
            *[new_m[n] for n in _WEIGHTS], *[new_v[n] for n in _WEIGHTS])
```

```python
import functools
import itertools
import math

import jax
import jax.numpy as jnp
from jax import lax
from jax.experimental import pallas as pl
from jax.experimental.pallas import tpu as pltpu

_MXU_DT = jnp.bfloat16
_F32 = jnp.float32
_VMEM_LIMIT = 56 * 1024 * 1024
_TT_MIX = 256
_TT_MIX_FWD = 512
_TT_FFN = 256
_TK_WGRAD = 2048
_HALO = 32

_LRU_C = 8.0
_LN_EPS = 1e-5
_N_HEADS = 8
_DEPTH = 1
_ALPHA = (2 * _DEPTH) ** 0.25
_ADAM_LR, _ADAM_B1, _ADAM_B2, _ADAM_EPS, _ADAM_WD, _ADAM_STEP = 0.001, 0.9, 0.999, 1e-08, 0.01, 10

_MESH = pl.DeviceIdType.MESH
_CHIP_DELTAS = ((1, 0), (0, 1), (1, 1))


def _cparams(sem):
    return pltpu.CompilerParams(dimension_semantics=sem, vmem_limit_bytes=_VMEM_LIMIT)


def _resident(shape):
    nd = len(shape)
    return pl.BlockSpec(shape, lambda *_: (0,) * nd, pipeline_mode=pl.Buffered(1))


def _dot(a, b):
    return jnp.dot(a, b, preferred_element_type=_F32)


def _dot_nt(a, b):
    return lax.dot_general(a, b, (((1,), (1,)), ((), ())), preferred_element_type=_F32)


def _dot_tn(a, b):
    return lax.dot_general(a, b, (((0,), (0,)), ((), ())), preferred_element_type=_F32)


def _mx(v):
    return v.astype(_MXU_DT)


def _expm1(v):
    series = v * (1.0 + v * (1.0 / 2 + v * (1.0 / 6 + v * (1.0 / 24 + v * (1.0 / 120)))))
    return jnp.where(jnp.abs(v) < 0.0625, series, jnp.exp(v) - 1.0)


def _softplus(z):
    e = jnp.exp(-jnp.abs(z))
    u = 1.0 + e
    log1p = jnp.where(u == 1.0, e, jnp.log(u) * e / jnp.where(u == 1.0, 1.0, u - 1.0))
    return jnp.maximum(z, 0.0) + log1p


_GELU_C = math.sqrt(2.0 / math.pi)


def _gelu_and_grad(v):
    t = jnp.tanh(_GELU_C * (v + 0.044715 * v * v * v))
    val = 0.5 * v * (1.0 + t)
    grad = 0.5 * (1.0 + t) + 0.5 * v * (1.0 - t * t) * _GELU_C * (1.0 + 3 * 0.044715 * v * v)
    return val, grad


def _seg_sum(v, seg, passes=3):
    hi = v.astype(jnp.bfloat16)
    r1 = v - hi.astype(_F32)
    mid = r1.astype(jnp.bfloat16)
    out = _dot(hi, seg) + _dot(mid, seg)
    if passes == 3:
        out = out + _dot((r1 - mid.astype(_F32)).astype(jnp.bfloat16), seg)
    return out


def _scan_fwd(a, u, h0):
    n = a.shape[0]
    row = lax.broadcasted_iota(jnp.int32, a.shape, 0)
    h, d = u, 1
    while d < n:
        keep = row >= d
        h = a * jnp.where(keep, pltpu.roll(h, d, 0), 0.0) + h
        a = a * jnp.where(keep, pltpu.roll(a, d, 0), 1.0)
        d *= 2
    return h + a * h0


def _scan_rev(c, g, g_end):
    n = c.shape[0]
    row = lax.broadcasted_iota(jnp.int32, c.shape, 0)
    d = 1
    while d < n:
        keep = row < n - d
        g = c * jnp.where(keep, pltpu.roll(g, n - d, 0), 0.0) + g
        c = c * jnp.where(keep, pltpu.roll(c, n - d, 0), 1.0)
        d *= 2
    return g + c * g_end


def _layer_norm_stats(z):
    mu = jnp.mean(z, axis=-1, keepdims=True)
    zc = z - mu
    var = jnp.mean(zc * zc, axis=-1, keepdims=True)
    rstd = lax.rsqrt(var + _LN_EPS)
    return zc * rstd, rstd


def _layer_norm_bwd(dn, n, rstd):
    return rstd * (dn - jnp.mean(dn, axis=-1, keepdims=True) - n * jnp.mean(dn * n, axis=-1, keepdims=True))


def _rowsum(v):
    return jnp.sum(v, axis=0, keepdims=True)


def _fused_exchange(body, n_in, n_out, n_scratch, n_xin, n_xout, plan, grid):
    def wrapped(*refs):
        o0 = n_in + n_xin
        s0 = o0 + n_out + n_xout
        start, finish = plan(refs[n_in:o0], refs[o0 + n_out:s0], *refs[s0 + n_scratch:])
        step = 0
        for axis, size in enumerate(grid):
            step = step * size + pl.program_id(axis)

        @pl.when(step == 0)
        def _():
            start()

        body(*refs[:n_in], *refs[o0:o0 + n_out], *refs[s0:s0 + n_scratch])

        @pl.when(step == math.prod(grid) - 1)
        def _():
            finish()

    return wrapped


def _lru_gates(xc, wr_ref, wi_ref, br_ref, bi_ref, lam_ref):
    xcb = _mx(xc)
    r = jax.nn.sigmoid(_dot(xcb, wr_ref[...]) + br_ref[...])
    i = jax.nn.sigmoid(_dot(xcb, wi_ref[...]) + bi_ref[...])
    sp = _softplus(-lam_ref[...])
    log_a = -_LRU_C * r * sp
    a = jnp.exp(log_a)
    mult = jnp.sqrt(-_expm1(2.0 * log_a))
    return r, i, sp, a, mult


def _conv_taps(ext_ref, w_ref, first, n_taps, tt):
    acc = w_ref[0:1, :] * ext_ref[pl.ds(first, tt), :]
    for k in range(1, n_taps):
        acc = acc + w_ref[k:k + 1, :] * ext_ref[pl.ds(first + k, tt), :]
    return acc


def _make_shifted(ext_ref, sh_ref):
    n = sh_ref.shape[1]
    for r in range(1, 8):
        sh_ref[r - 1] = ext_ref[pl.ds(r, n), :]


def _tap(ext_ref, sh_ref, off, tt):
    base = (off // 8) * 8
    if off % 8 == 0:
        return ext_ref[pl.ds(base, tt), :]
    return sh_ref[off % 8 - 1, pl.ds(base, tt), :]


def _conv_taps_shifted(ext_ref, sh_ref, w_ref, first, n_taps, tt):
    acc = w_ref[0:1, :] * _tap(ext_ref, sh_ref, first, tt)
    for k in range(1, n_taps):
        acc = acc + w_ref[k:k + 1, :] * _tap(ext_ref, sh_ref, first + k, tt)
    return acc


def _mix_fwd(x, mod3, win, lcw, lcb, wr_bd, wi_bd, b_r, b_i, lam, cw, cb, ng, nb, seg, wout, ln1g, ln1b, shards):
    bl, s_len, d = x.shape
    w = d // 2
    tt = min(_TT_MIX_FWD, s_len)
    ns = s_len // tt
    kc = cw.shape[0]

    def body(x_ref, mod_ref, win_ref, lcw_ref, lcb_ref, wr_ref, wi_ref, br_ref, bi_ref, lam_ref, cw_ref, cb_ref,
             ng_ref, nb_ref, seg_ref, wout_ref, g1_ref, b1_ref,
             proj_ref, h_ref, mix_ref, x1_ref, u1_ref, y_ref, vbc_ref, lru_ref, ext4, ext31, sh31, hcar):
        @pl.when(pl.program_id(1) == 0)
        def _():
            ext4[0:8, :] = jnp.zeros((8, w), _F32)
            ext31[0:_HALO, :] = jnp.zeros((_HALO, w), _F32)
            hcar[...] = jnp.zeros_like(hcar)

        xt = x_ref[...]
        sh1, sc1, gt1 = mod_ref[:, 0:d], mod_ref[:, d:2 * d], mod_ref[:, 2 * d:3 * d]
        u1 = _mx(xt * (1.0 + sc1) + sh1)
        u1_ref[...] = u1
        xa, ga, vb, gb = (_dot(u1, win_ref[k]) for k in range(4))
        proj_ref[:, 0:w] = xa
        proj_ref[:, w:2 * w] = ga
        proj_ref[:, 2 * w:3 * w] = vb
        proj_ref[:, 3 * w:4 * w] = gb

        ext4[8:8 + tt, :] = xa
        xc = lcb_ref[...] + _conv_taps(ext4, lcw_ref, 5, 4, tt)
        ext4[0:8, :] = xa[tt - 8:tt, :]
        r, i, sp, a, mult = _lru_gates(xc, wr_ref, wi_ref, br_ref, bi_ref, lam_ref)
        for k, val in enumerate((xc, r, i, a, mult)):
            lru_ref[:, k * w:(k + 1) * w] = val
        h = _scan_fwd(a, mult * (i * xc), hcar[0:1, :])
        hcar[0:1, :] = h[tt - 1:tt, :]
        h_ref[...] = h
        gelu, _ = _gelu_and_grad(ga)
        y_ref[:, 0:w] = _mx(gelu * h)

        vbg = vb * jax.nn.sigmoid(gb)
        ext31[_HALO:_HALO + tt, :] = vbg
        _make_shifted(ext31, sh31)
        vbc = cb_ref[...] + _conv_taps_shifted(ext31, sh31, cw_ref, _HALO - (kc - 1), kc, tt)
        vbc_ref[...] = vbc
        ext31[0:_HALO, :] = vbg[tt - _HALO:tt, :]
        inv = 1.0 / (w // _N_HEADS)
        zc = vbc - _seg_sum(vbc, seg_ref[...]) * inv
        n = zc * lax.rsqrt(_seg_sum(zc * zc, seg_ref[...]) * inv + _LN_EPS)
        pre = n * ng_ref[...] + nb_ref[...]
        y_ref[:, w:2 * w] = _mx(pre * jax.nn.sigmoid(pre))

        mix = _dot(y_ref[...], wout_ref[...])
        mix_ref[...] = mix
        n1, _ = _layer_norm_stats(_ALPHA * xt + (1.0 + gt1) * mix)
        x1_ref[...] = n1 * g1_ref[...] + b1_ref[...]

    tok = lambda c: pl.BlockSpec((None, tt, c), lambda b, s: (b, s, 0))
    smalls = [lcw, lcb, wr_bd, wi_bd, b_r, b_i, lam, cw, cb, ng, nb, seg, wout, ln1g, ln1b]
    nx = len(shards)
    return pl.pallas_call(
        _fused_exchange(body, 3 + len(smalls), 8, 4, nx, nx, _gather_plan, (bl, ns)), grid=(bl, ns),
        in_specs=[tok(d), pl.BlockSpec((None, 1, 6 * d), lambda b, s: (b, 0, 0)), _resident(win.shape)]
        + [_resident(t.shape) for t in smalls] + [_HBM] * nx,
        out_specs=[tok(4 * w), tok(w), tok(d), tok(d), tok(d), tok(d), tok(w), tok(5 * w)] + [_HBM] * nx,
        out_shape=[jax.ShapeDtypeStruct((bl, s_len, 4 * w), _F32), jax.ShapeDtypeStruct((bl, s_len, w), _F32),
                   jax.ShapeDtypeStruct((bl, s_len, d), _F32), jax.ShapeDtypeStruct((bl, s_len, d), _F32),
                   jax.ShapeDtypeStruct((bl, s_len, d), _MXU_DT), jax.ShapeDtypeStruct((bl, s_len, d), _MXU_DT),
                   jax.ShapeDtypeStruct((bl, s_len, w), _F32), jax.ShapeDtypeStruct((bl, s_len, 5 * w), _F32)]
        + [jax.ShapeDtypeStruct((4,) + t.shape, t.dtype) for t in shards],
        scratch_shapes=[pltpu.VMEM((tt + 8, w), _F32), pltpu.VMEM((tt + _HALO, w), _F32),
                        pltpu.VMEM((7, tt + _HALO - 8, w), _F32), pltpu.VMEM((8, w), _F32)] + _gather_sems(nx),
        compiler_params=_cparams(("arbitrary", "arbitrary")), name="mix_fwd",
    )(x, mod3, win, *smalls, *shards)


def _ffn_fwd(x1, mod3, wup, fcw, fcb, wdn, ln2g, ln2b, target):
    bl, s_len, d = x1.shape
    nch, _, fc = wup.shape
    nch //= 2
    f = nch * fc
    tt = min(_TT_FFN, s_len)
    ns = s_len // tt

    def body(x1_ref, mod_ref, wup_ref, fcw_ref, fcb_ref, wdn_ref, g2_ref, b2_ref, tgt_ref,
             u2_ref, hh_ref, f_ref, gc_ref, dz2_ref, loss_ref, dln2_ref, dgt2_ref, ext3):
        first_tile = pl.program_id(1) == 0

        @pl.when(first_tile)
        def _():
            ext3[:, 0:8, :] = jnp.zeros((nch, 8, fc), _F32)
            dgt2_ref[...] = jnp.zeros_like(dgt2_ref)

        @pl.when(first_tile & (pl.program_id(0) == 0))
        def _():
            loss_ref[...] = jnp.zeros_like(loss_ref)
            dln2_ref[...] = jnp.zeros_like(dln2_ref)

        x1t = x1_ref[...]
        sh2, sc2, gt2 = mod_ref[:, 3 * d:4 * d], mod_ref[:, 4 * d:5 * d], mod_ref[:, 5 * d:6 * d]
        u2 = _mx(x1t * (1.0 + sc2) + sh2)
        u2_ref[...] = u2
        y2 = jnp.zeros((tt, d), _F32)
        for j in range(nch):
            lanes = slice(j * fc, (j + 1) * fc)
            v = _dot(u2, wup_ref[j])
            g = _dot(u2, wup_ref[nch + j])
            hh_ref[:, lanes] = v.astype(hh_ref.dtype)
            hh_ref[:, f + j * fc:f + (j + 1) * fc] = g.astype(hh_ref.dtype)
            ext = ext3.at[j]
            ext[8:8 + tt, :] = g
            gc = fcb_ref[:, lanes] + sum(fcw_ref[k:k + 1, lanes] * ext[pl.ds(6 + k, tt), :] for k in range(3))
            gc_ref[:, lanes] = gc
            ext[0:8, :] = g[tt - 8:tt, :]
            fj = _mx(gc * jax.nn.sigmoid(gc) * v)
            f_ref[:, lanes] = fj
            y2 = y2 + _dot(fj, wdn_ref[lanes, :])

        n2, rstd = _layer_norm_stats(_ALPHA * x1t + (1.0 + gt2) * y2)
        err = n2 * g2_ref[...] + b2_ref[...] - tgt_ref[...]
        loss_ref[...] += jnp.sum(_rowsum(err * err), axis=1, keepdims=True)
        dout = err * (1.0 / d)
        dln2_ref[0:1, :] += _rowsum(dout * n2)
        dln2_ref[1:2, :] += _rowsum(dout)
        dz2 = _layer_norm_bwd(dout * g2_ref[...], n2, rstd)
        dz2_ref[...] = dz2
        dgt2_ref[...] += _rowsum(dz2 * y2)

    tok = lambda c: pl.BlockSpec((None, tt, c), lambda b, s: (b, s, 0))
    acc = lambda r: pl.BlockSpec((r, d), lambda b, s: (0, 0))
    smalls = [fcw, fcb, wdn, ln2g, ln2b]
    return pl.pallas_call(
        body, grid=(bl, ns),
        in_specs=[tok(d), pl.BlockSpec((None, 1, 6 * d), lambda b, s: (b, 0, 0)), _resident(wup.shape)]
        + [_resident(t.shape) for t in smalls] + [tok(d)],
        out_specs=[tok(d), tok(2 * f), tok(f), tok(f), tok(d), acc(1), acc(2), pl.BlockSpec((None, 1, d), lambda b, s: (b, 0, 0))],
        out_shape=[jax.ShapeDtypeStruct((bl, s_len, d), _MXU_DT), jax.ShapeDtypeStruct((bl, s_len, 2 * f), _F32),
                   jax.ShapeDtypeStruct((bl, s_len, f), _MXU_DT), jax.ShapeDtypeStruct((bl, s_len, f), _F32),
                   jax.ShapeDtypeStruct((bl, s_len, d), _F32), jax.ShapeDtypeStruct((1, d), _F32), jax.ShapeDtypeStruct((2, d), _F32),
                   jax.ShapeDtypeStruct((bl, 1, d), _F32)],
        scratch_shapes=[pltpu.VMEM((nch, tt + 8, fc), _F32)],
        compiler_params=_cparams(("arbitrary", "arbitrary")), name="ffn_fwd",
    )(x1, mod3, wup, *smalls, target)


def _ffn_bwd(dz2, x1, hh, gc_all, mod3, wup, wdn, fcw, fcb):
    bl, s_len, d = x1.shape
    nch, _, fc = wup.shape
    nch //= 2
    f = nch * fc
    tt = min(_TT_FFN, s_len)
    ns = s_len // tt

    def body(dz2_ref, x1_ref, hh_ref, gc_ref, mod_ref, wup_ref, wdn_ref, fcw_ref, fcb_ref,
             dx1_ref, dy2_ref, dh_ref, dfc_ref, dmod_ref, dext, dcar):
        @pl.when(pl.program_id(1) == 0)
        def _():
            dcar[...] = jnp.zeros_like(dcar)
            dmod_ref[...] = jnp.zeros_like(dmod_ref)

        @pl.when((pl.program_id(1) == 0) & (pl.program_id(0) == 0))
        def _():
            dfc_ref[...] = jnp.zeros_like(dfc_ref)

        sc2, gt2 = mod_ref[:, 4 * d:5 * d], mod_ref[:, 5 * d:6 * d]
        dz2t = dz2_ref[...]
        dy2 = _mx((1.0 + gt2) * dz2t)
        dy2_ref[...] = dy2
        du2 = jnp.zeros((tt, d), _F32)
        for j in range(nch):
            lanes = slice(j * fc, (j + 1) * fc)
            glanes = slice(f + j * fc, f + (j + 1) * fc)
            v = hh_ref[:, lanes].astype(_F32)
            g = hh_ref[:, glanes].astype(_F32)
            gc = gc_ref[:, lanes]
            sg = jax.nn.sigmoid(gc)
            df = _dot_nt(dy2, wdn_ref[lanes, :])
            dv = df * (gc * sg)
            dgc = df * v * (sg * (1.0 + gc * (1.0 - sg)))
            dfc_ref[3:4, lanes] += _rowsum(dgc)
            dext[0:tt, :] = dgc
            dext[tt:tt + 8, :] = dcar[j]
            dcar[j] = dgc[0:8, :]
            dg = jnp.zeros((tt, fc), _F32)
            for k in range(3):
                shifted = dext[pl.ds(2 - k, tt), :]
                dg = dg + fcw_ref[k:k + 1, lanes] * shifted
                dfc_ref[k:k + 1, lanes] += _rowsum(shifted * g)
            dvb, dgb = _mx(dv), _mx(dg)
            dh_ref[:, lanes] = dvb
            dh_ref[:, glanes] = dgb
            du2 = du2 + _dot_nt(dvb, wup_ref[j]) + _dot_nt(dgb, wup_ref[nch + j])

        dx1_ref[...] = _ALPHA * dz2t + du2 * (1.0 + sc2)
        dmod_ref[0:1, :] += _rowsum(du2)
        dmod_ref[1:2, :] += _rowsum(du2 * x1_ref[...])

    tok = lambda c: pl.BlockSpec((None, tt, c), lambda b, i: (b, ns - 1 - i, 0))
    return pl.pallas_call(
        body, grid=(bl, ns),
        in_specs=[tok(d), tok(d), tok(2 * f), tok(f), pl.BlockSpec((None, 1, 6 * d), lambda b, i: (b, 0, 0)),
                  _resident(wup.shape), _resident(wdn.shape), _resident(fcw.shape), _resident(fcb.shape)],
        out_specs=[tok(d), tok(d), tok(2 * f), pl.BlockSpec((4, f), lambda b, i: (0, 0)),
                   pl.BlockSpec((None, 2, d), lambda b, i: (b, 0, 0))],
        out_shape=[jax.ShapeDtypeStruct((bl, s_len, d), _F32), jax.ShapeDtypeStruct((bl, s_len, d), _MXU_DT),
                   jax.ShapeDtypeStruct((bl, s_len, 2 * f), _MXU_DT), jax.ShapeDtypeStruct((4, f), _F32),
                   jax.ShapeDtypeStruct((bl, 2, d), _F32)],
        scratch_shapes=[pltpu.VMEM((tt + 8, fc), _F32), pltpu.VMEM((nch, 8, fc), _F32)],
        compiler_params=_cparams(("arbitrary", "arbitrary")), name="ffn_bwd",
    )(dz2, x1, hh, gc_all, mod3, wup, wdn, fcw, fcb)


def _mix_bwd(dx1, x, mix, proj, h, vbc, lru, mod3, win, lcw, lcb, wr_bd, wi_bd, b_r, b_i, lam, cw, cb, ng, nb, seg, wout, ln1g, chip_sums):
    bl, s_len, d = x.shape
    w = d // 2
    tt = min(_TT_MIX, s_len)
    ns = s_len // tt
    kc = cw.shape[0]

    def body(dx1_ref, x_ref, mix_ref, proj_ref, phalo_ref, h_ref, hhalo_ref, vbc_ref, lru_ref, mod_ref, win_ref, lcw_ref, lcb_ref,
             wr_ref, wi_ref, br_ref, bi_ref, lam_ref, cw_ref, cb_ref, ng_ref, nb_ref, seg_ref, wout_ref, g1_ref,
             gx_ref, dproj_ref, dmix_ref, xcg_ref, vecw_ref, dlcw_ref, dcw_ref, dln1_ref, dmod_ref,
             ext4, ext31, dext4, dext31, sh31, dsh31, car4, car31, gcar):
        s = ns - 1 - pl.program_id(1)
        first = s == 0

        @pl.when(pl.program_id(1) == 0)
        def _():
            car4[...] = jnp.zeros_like(car4)
            car31[...] = jnp.zeros_like(car31)
            gcar[...] = jnp.zeros_like(gcar)
            dmod_ref[...] = jnp.zeros_like(dmod_ref)

        @pl.when((pl.program_id(1) == 0) & (pl.program_id(0) == 0))
        def _():
            for ref in (vecw_ref, dlcw_ref, dcw_ref, dln1_ref):
                ref[...] = jnp.zeros_like(ref)

        xt, mixt = x_ref[...], mix_ref[...]
        sh1, sc1, gt1 = mod_ref[:, 0:d], mod_ref[:, d:2 * d], mod_ref[:, 2 * d:3 * d]

        n1, rstd1 = _layer_norm_stats(_ALPHA * xt + (1.0 + gt1) * mixt)
        dx1t = dx1_ref[...]
        dln1_ref[0:1, :] += _rowsum(dx1t * n1)
        dln1_ref[1:2, :] += _rowsum(dx1t)
        dz1 = _layer_norm_bwd(dx1t * g1_ref[...], n1, rstd1)
        dmod_ref[2:3, :] += _rowsum(dz1 * mixt)
        dmix = _mx((1.0 + gt1) * dz1)
        dmix_ref[...] = dmix
        dya = _dot_nt(dmix, wout_ref[0:w, :])
        dyb = _dot_nt(dmix, wout_ref[w:2 * w, :])

        xa, ga = proj_ref[:, 0:w], proj_ref[:, w:2 * w]
        vb, gb = proj_ref[:, 2 * w:3 * w], proj_ref[:, 3 * w:4 * w]

        sgb = jax.nn.sigmoid(gb)
        vbg = vb * sgb
        hv, hg = phalo_ref[:, 2 * w:3 * w], phalo_ref[:, 3 * w:4 * w]
        ext31[0:_HALO, :] = jnp.where(first, 0.0, hv * jax.nn.sigmoid(hg))
        ext31[_HALO:_HALO + tt, :] = vbg
        _make_shifted(ext31, sh31)
        vbc = vbc_ref[...]
        inv = 1.0 / (w // _N_HEADS)
        zc = vbc - _seg_sum(vbc, seg_ref[...]) * inv
        rstd = lax.rsqrt(_seg_sum(zc * zc, seg_ref[...]) * inv + _LN_EPS)
        n = zc * rstd
        pre = n * ng_ref[...] + nb_ref[...]
        sgp = jax.nn.sigmoid(pre)
        dpre = dyb * (sgp * (1.0 + pre * (1.0 - sgp)))
        vecw_ref[5:6, :] += _rowsum(dpre * n)
        vecw_ref[6:7, :] += _rowsum(dpre)
        dn = dpre * ng_ref[...]
        dvbc = rstd * (dn - _seg_sum(dn, seg_ref[...], 2) * inv - n * (_seg_sum(dn * n, seg_ref[...], 2) * inv))
        vecw_ref[4:5, :] += _rowsum(dvbc)
        dext31[0:tt, :] = dvbc
        dext31[tt:tt + _HALO, :] = car31[...]
        car31[...] = dvbc[0:_HALO, :]
        _make_shifted(dext31, dsh31)
        dvbg = jnp.zeros((tt, w), _F32)
        for k in range(kc):
            dvbg = dvbg + cw_ref[k:k + 1, :] * _tap(dext31, dsh31, kc - 1 - k, tt)
            dcw_ref[k:k + 1, :] += _rowsum(dvbc * _tap(ext31, sh31, _HALO - (kc - 1) + k, tt))
        dproj_ref[:, 2 * w:3 * w] = _mx(dvbg * sgb)
        dproj_ref[:, 3 * w:4 * w] = _mx(dvbg * vb * (sgb * (1.0 - sgb)))

        ext4[0:8, :] = jnp.where(first, 0.0, phalo_ref[_HALO - 8:_HALO, 0:w])
        ext4[8:8 + tt, :] = xa
        xc, r, i, a, mult = (lru_ref[:, k * w:(k + 1) * w] for k in range(5))
        xcg_ref[:, 0:w] = _mx(xc)
        sp = _softplus(-lam_ref[...])
        ht = h_ref[...]
        row = lax.broadcasted_iota(jnp.int32, (tt, w), 0)
        h_before = jnp.where(first, 0.0, hhalo_ref[7:8, :])
        hprev = jnp.where(row == 0, h_before, pltpu.roll(ht, 1, 0))
        gelu, dgelu = _gelu_and_grad(ga)
        dproj_ref[:, w:2 * w] = _mx(dya * ht * dgelu)
        dh = dya * gelu
        coef = jnp.where(row == tt - 1, 1.0, pltpu.roll(a, tt - 1, 0))
        big_g = _scan_rev(coef, dh, gcar[0:1, :])
        gcar[0:1, :] = a[0:1, :] * big_g[0:1, :]
        da = big_g * hprev
        ixc = i * xc
        dlog_a = da * a - (big_g * ixc) * (a * a / mult)
        di = big_g * mult * xc
        dxc = big_g * mult * i
        vecw_ref[3:4, :] += _rowsum(dlog_a * r) * (_LRU_C * jax.nn.sigmoid(-lam_ref[...]))
        dgr_f = dlog_a * (-_LRU_C * sp) * (r * (1.0 - r))
        dgi_f = di * (i * (1.0 - i))
        vecw_ref[1:2, :] += _rowsum(dgr_f)
        vecw_ref[2:3, :] += _rowsum(dgi_f)
        dgr, dgi = _mx(dgr_f), _mx(dgi_f)
        xcg_ref[:, w:2 * w] = dgr
        xcg_ref[:, 2 * w:3 * w] = dgi
        dxc = dxc + _dot_nt(dgr, wr_ref[...]) + _dot_nt(dgi, wi_ref[...])
        vecw_ref[0:1, :] += _rowsum(dxc)
        dext4[0:tt, :] = dxc
        dext4[tt:tt + 8, :] = car4[...]
        car4[...] = dxc[0:8, :]
        dxa = jnp.zeros((tt, w), _F32)
        for k in range(4):
            dxa = dxa + lcw_ref[k:k + 1, :] * dext4[pl.ds(3 - k, tt), :]
            dlcw_ref[k:k + 1, :] += _rowsum(dxc * ext4[pl.ds(5 + k, tt), :])
        dproj_ref[:, 0:w] = _mx(dxa)

        du1 = sum(_dot_nt(dproj_ref[:, k * w:(k + 1) * w], win_ref[k]) for k in range(4))
        gx_ref[...] = _ALPHA * dz1 + du1 * (1.0 + sc1)
        dmod_ref[0:1, :] += _rowsum(du1)
        dmod_ref[1:2, :] += _rowsum(du1 * xt)

    tok = lambda c: pl.BlockSpec((None, tt, c), lambda b, i: (b, ns - 1 - i, 0))
    halo = lambda rows, c: pl.BlockSpec(
        (None, rows, c), lambda b, i: (b, jnp.maximum((ns - 1 - i) * (tt // rows) - 1, 0), 0))
    accw = lambda r, c: pl.BlockSpec((r, c), lambda b, i: (0, 0))
    smalls = [lcw, lcb, wr_bd, wi_bd, b_r, b_i, lam, cw, cb, ng, nb, seg, wout, ln1g]
    nx = len(chip_sums)
    return pl.pallas_call(
        _fused_exchange(body, 11 + len(smalls), 9, 9, nx, nx, _chip_reduce_plan, (bl, ns)), grid=(bl, ns),
        in_specs=[tok(d), tok(d), tok(d), tok(4 * w), halo(_HALO, 4 * w), tok(w), halo(8, w), tok(w), tok(5 * w),
                  pl.BlockSpec((None, 1, 6 * d), lambda b, i: (b, 0, 0)), _resident(win.shape)]
        + [_resident(t.shape) for t in smalls] + [_HBM] * nx,
        out_specs=[tok(d), tok(4 * w), tok(d), tok(3 * w), accw(8, w), accw(4, w), accw(kc, w), accw(2, d),
                   pl.BlockSpec((None, 3, d), lambda b, i: (b, 0, 0))] + [_HBM] * nx,
        out_shape=[jax.ShapeDtypeStruct((bl, s_len, d), _F32), jax.ShapeDtypeStruct((bl, s_len, 4 * w), _MXU_DT),
                   jax.ShapeDtypeStruct((bl, s_len, d), _MXU_DT), jax.ShapeDtypeStruct((bl, s_len, 3 * w), _MXU_DT),
                   jax.ShapeDtypeStruct((8, w), _F32), jax.ShapeDtypeStruct((4, w), _F32),
                   jax.ShapeDtypeStruct((kc, w), _F32), jax.ShapeDtypeStruct((2, d), _F32),
                   jax.ShapeDtypeStruct((bl, 3, d), _F32)]
        + [jax.ShapeDtypeStruct((3,) + t.shape[1:], t.dtype) for t in chip_sums],
        scratch_shapes=[pltpu.VMEM((tt + 8, w), _F32), pltpu.VMEM((tt + _HALO, w), _F32),
                        pltpu.VMEM((tt + 8, w), _F32), pltpu.VMEM((tt + _HALO, w), _F32),
                        pltpu.VMEM((7, tt + _HALO - 8, w), _F32), pltpu.VMEM((7, tt + _HALO - 8, w), _F32),
                        pltpu.VMEM((8, w), _F32), pltpu.VMEM((_HALO, w), _F32), pltpu.VMEM((8, w), _F32)]
        + _chip_reduce_sems(nx),
        compiler_params=_cparams(("arbitrary", "arbitrary")), name="mix_bwd",
    )(dx1, x, mix, proj, proj, h, h, vbc, lru, mod3, win, *smalls, *chip_sums)


def _wgrad(a, b, ma, nbw, na, nb, a_off, b_off, name, exchange=None):
    t = a.shape[0]
    tk = min(_TK_WGRAD, t)
    grid = (na * nb, t // tk)

    def body(a_ref, b_ref, o_ref):
        @pl.when(pl.program_id(1) == 0)
        def _():
            o_ref[...] = jnp.zeros_like(o_ref)
        o_ref[...] += _dot_tn(a_ref[...], b_ref[...])

    xin, xshapes, plan, sems = exchange if exchange else ([], [], None, [])
    nx = len(xin)
    res = pl.pallas_call(
        _fused_exchange(body, 2, 1, 0, nx, len(xshapes), plan, grid) if exchange else body, grid=grid,
        in_specs=[pl.BlockSpec((tk, ma), lambda j, k: (k, j // nb + a_off)),
                  pl.BlockSpec((tk, nbw), lambda j, k: (k, j % nb + b_off))] + [_HBM] * nx,
        out_specs=[pl.BlockSpec((None, ma, nbw), lambda j, k: (j, 0, 0))] + [_HBM] * len(xshapes),
        out_shape=[jax.ShapeDtypeStruct((na * nb, ma, nbw), _F32)] + list(xshapes),
        scratch_shapes=list(sems),
        compiler_params=_cparams(("arbitrary", "arbitrary")), name=name,
    )(a, b, *xin)
    return res if exchange else res[0]


_DEV_DELTAS = tuple(dl for dl in itertools.product((0, 1), repeat=3) if any(dl))
_HBM = pl.BlockSpec(memory_space=pltpu.HBM)
_VMEM = pl.BlockSpec(memory_space=pltpu.VMEM)


def _pos():
    return lax.axis_index("x"), lax.axis_index("y"), lax.axis_index("c")


def _flip(v, delta):
    return 1 - v if delta else v


def _remote(src, dst, ssem, rsem, dev):
    return pltpu.make_async_remote_copy(src_ref=src, dst_ref=dst, send_sem=ssem, recv_sem=rsem,
                                        device_id=dev, device_id_type=_MESH)


def _rows(ref, idx, n):
    return ref.at[pl.ds(pl.multiple_of(idx * n, 8), n)]


def _ada_fwd(c8, w_ada_k, b_ada_k, shards):
    rows, d = c8.shape
    nk = w_ada_k.shape[1]
    n = len(shards)

    def body(*refs):
        c_ref, w_ref, b_ref = refs[:3]
        call_ref, mod_ref = refs[3 + n:5 + n]
        modloc, modrcv, s1, r1, s2, r2 = refs[5 + 2 * n:11 + 2 * n]
        gather_start, gather_finish = _gather_plan(refs[3:3 + n], refs[5 + n:5 + 2 * n], *refs[11 + 2 * n:14 + 2 * n],
                                                   fsem=refs[14 + 2 * n], frsem=refs[15 + 2 * n], bounce=refs[16 + 2 * n:])
        gather_start()
        xi, yi, ci = _pos()
        me, kme = 4 * xi + 2 * yi + ci, 2 * xi + yi
        call_ref[pl.ds(pl.multiple_of(me * rows, 8), rows), :] = c_ref[...]
        sends = []
        for p, (dx, dy, dc) in enumerate(_DEV_DELTAS):
            cp = _remote(c_ref, _rows(call_ref, me, rows), s1.at[p], r1.at[p], (_flip(xi, dx), _flip(yi, dy), _flip(ci, dc)))
            cp.start()
            sends.append(cp)
        for p, (dx, dy, dc) in enumerate(_DEV_DELTAS):
            src = 4 * _flip(xi, dx) + 2 * _flip(yi, dy) + _flip(ci, dc)
            _remote(c_ref, _rows(call_ref, src, rows), s1.at[p], r1.at[p], (xi, yi, ci)).wait_recv()
        for cp in sends:
            cp.wait_send()

        ca = call_ref[...]
        modloc[...] = _dot(_mx(ca * jax.nn.sigmoid(ca)), _mx(w_ref[...])) + b_ref[...]
        modrcv[kme] = modloc[pl.ds(pl.multiple_of(me * rows, 8), rows), :]
        sends = []
        for j, (dx, dy) in enumerate(_CHIP_DELTAS):
            tx, ty = _flip(xi, dx), _flip(yi, dy)
            cp = _remote(_rows(modloc, 4 * tx + 2 * ty + ci, rows), modrcv.at[kme], s2.at[j], r2.at[j], (tx, ty, ci))
            cp.start()
            sends.append(cp)
        for j, (dx, dy) in enumerate(_CHIP_DELTAS):
            ksrc = 2 * _flip(xi, dx) + _flip(yi, dy)
            _remote(_rows(modloc, me, rows), modrcv.at[ksrc], s2.at[j], r2.at[j], (xi, yi, ci)).wait_recv()
        for cp in sends:
            cp.wait_send()
        for j in range(4):
            mod_ref[:, j * nk:(j + 1) * nk] = modrcv[j]
        gather_finish()

    return pl.pallas_call(
        body, in_specs=[_VMEM, _VMEM, _VMEM] + [_HBM] * n, out_specs=[_VMEM, _VMEM] + [_HBM] * n,
        out_shape=[jax.ShapeDtypeStruct((8 * rows, d), _F32), jax.ShapeDtypeStruct((rows, 4 * nk), _F32)]
        + [jax.ShapeDtypeStruct((4,) + a.shape, a.dtype) for a in shards],
        scratch_shapes=[pltpu.VMEM((8 * rows, nk), _F32), pltpu.VMEM((4, rows, nk), _F32),
                        pltpu.SemaphoreType.DMA((7,)), pltpu.SemaphoreType.DMA((7,)),
                        pltpu.SemaphoreType.DMA((3,)), pltpu.SemaphoreType.DMA((3,))]
        + _gather_sems(n) + [pltpu.SemaphoreType.DMA((3, n)), pltpu.SemaphoreType.DMA((3, n))]
        + [pltpu.VMEM(a.shape, a.dtype) for a in shards],
        compiler_params=pltpu.CompilerParams(vmem_limit_bytes=_VMEM_LIMIT), name="ada_fwd",
    )(c8, w_ada_k, b_ada_k, *shards)


def _gather_sems(n):
    return [pltpu.SemaphoreType.DMA((3, n)), pltpu.SemaphoreType.DMA((3, n)), pltpu.SemaphoreType.DMA((n,))]


def _gather_plan(ins, outs, ssem, rsem, lsem, bounce=(), fsem=None, frsem=None):
    n = len(ins)
    xi, yi, ci = _pos()
    kme = 2 * xi + yi
    split = [fsem is not None and ins[a].shape[0] % 32 == 0 for a in range(n)]

    def half(ref, a, which):
        r2 = ins[a].shape[0] // 2
        return ref.at[pl.ds(pl.multiple_of(which * r2, 16), r2)]

    staged = [pltpu.make_async_copy(ins[a], bounce[a], lsem.at[a]) for a in range(len(bounce))]
    local = [pltpu.make_async_copy(bounce[a] if bounce else ins[a], outs[a].at[kme], lsem.at[a]) for a in range(n)]
    sends, recvs, forwards, handed = [], [], [], []
    for j, (dx, dy) in enumerate(_CHIP_DELTAS):
        tx, ty = _flip(xi, dx), _flip(yi, dy)
        for a in range(n):
            sems = (ssem.at[j, a], rsem.at[j, a])
            landing = outs[a].at[2 * tx + ty]
            if split[a]:
                sends.append(_remote(half(ins[a], a, ci), half(outs[a].at[kme], a, ci), *sems, (tx, ty, ci)))
                recvs.append(_remote(half(ins[a], a, ci), half(landing, a, ci), *sems, (xi, yi, ci)))
                fsems = (fsem.at[j, a], frsem.at[j, a])
                forwards.append(_remote(half(landing, a, ci), half(landing, a, ci), *fsems, (xi, yi, 1 - ci)))
                handed.append(_remote(half(ins[a], a, 1 - ci), half(landing, a, 1 - ci), *fsems, (xi, yi, ci)))
            else:
                sends.append(_remote(ins[a], outs[a].at[kme], *sems, (tx, ty, ci)))
                recvs.append(_remote(ins[a], landing, *sems, (xi, yi, ci)))
                forwards.append(None)

    def start():
        for cp in sends + staged:
            cp.start()
        for cp in staged:
            cp.wait()
        for cp in local:
            cp.start()

    def finish():
        for arrived, forward in zip(recvs, forwards):
            arrived.wait_recv()
            if forward is not None:
                forward.start()
        for cp in handed:
            cp.wait_recv()
        for cp in sends + [f for f in forwards if f is not None]:
            cp.wait_send()
        for cp in local:
            cp.wait()

    return start, finish


def _dev_gather_sems(n):
    return [pltpu.SemaphoreType.DMA((7, n)), pltpu.SemaphoreType.DMA((7, n)), pltpu.SemaphoreType.DMA((n,))]


def _dev_gather_plan(ins, outs, ssem, rsem, lsem):
    n = len(ins)
    xi, yi, ci = _pos()
    me = 4 * xi + 2 * yi + ci
    local = [pltpu.make_async_copy(ins[a], outs[a].at[me], lsem.at[a]) for a in range(n)]
    sends, recvs = [], []
    for p, (dx, dy, dc) in enumerate(_DEV_DELTAS):
        tx, ty, tc = _flip(xi, dx), _flip(yi, dy), _flip(ci, dc)
        for a in range(n):
            sends.append(_remote(ins[a], outs[a].at[me], ssem.at[p, a], rsem.at[p, a], (tx, ty, tc)))
            recvs.append(_remote(ins[a], outs[a].at[4 * tx + 2 * ty + tc], ssem.at[p, a], rsem.at[p, a], (xi, yi, ci)))

    def start():
        for cp in local + sends:
            cp.start()

    def finish():
        for cp in recvs:
            cp.wait_recv()
        for cp in sends:
            cp.wait_send()
        for cp in local:
            cp.wait()

    return start, finish


def _pair_sems(n):
    return [pltpu.SemaphoreType.DMA((n,)), pltpu.SemaphoreType.DMA((n,))]


def _pair_plan(ins, outs, ssem, rsem):
    xi, yi, ci = _pos()
    sends = []
    for a in range(len(ins)):
        r2 = ins[a].shape[1] // 2
        src = ins[a].at[:, pl.ds(pl.multiple_of((1 - ci) * r2, 8), r2), :]
        sends.append(_remote(src, outs[a], ssem.at[a], rsem.at[a], (xi, yi, 1 - ci)))

    def start():
        for cp in sends:
            cp.start()

    def finish():
        for cp in sends:
            cp.wait_recv()
        for cp in sends:
            cp.wait_send()

    return start, finish


def _chip_reduce_sems(n):
    return [pltpu.SemaphoreType.DMA((3, n)), pltpu.SemaphoreType.DMA((3, n))]


def _chip_reduce_plan(ins, outs, ssem, rsem):
    xi, yi, ci = _pos()
    sends = []
    for j, (dx, dy) in enumerate(_CHIP_DELTAS):
        tx, ty = _flip(xi, dx), _flip(yi, dy)
        sends += [_remote(ins[a].at[2 * tx + ty], outs[a].at[j], ssem.at[j, a], rsem.at[j, a], (tx, ty, ci))
                  for a in range(len(ins))]

    def start():
        for cp in sends:
            cp.start()

    def finish():
        for cp in sends:
            cp.wait_recv()
        for cp in sends:
            cp.wait_send()

    return start, finish


def _pair_exchange(gs, name):
    n = len(gs)

    def body(*refs):
        start, finish = _pair_plan(refs[:n], refs[n:2 * n], *refs[2 * n:])
        start()
        finish()

    return pl.pallas_call(
        body, in_specs=[_HBM] * n, out_specs=[_HBM] * n, out_shape=_pair_out_shapes(gs),
        scratch_shapes=_pair_sems(n), name=name,
    )(*gs)


def _pair_out_shapes(gs):
    return [jax.ShapeDtypeStruct((g.shape[0], g.shape[1] // 2, g.shape[2]), g.dtype) for g in gs]


def _row_tile(r):
    return max(t for t in range(8, min(r, 256) + 1, 8) if r % t == 0)


def _pair_add(g, r, cidx, name, wire_dtype=None, exchange=None):
    nk, r2, c = r.shape
    tr = _row_tile(r2)
    nt = r2 // tr
    xin, xshapes, plan, sems = exchange if exchange else ([], [], None, [])
    nx = len(xin)

    def body(c_ref, g_ref, r_ref, *o_refs):
        s = g_ref[...] + r_ref[...]
        for o_ref in o_refs:
            o_ref[...] = s.astype(o_ref.dtype)

    out_spec = pl.BlockSpec((None, tr, c), lambda k, i, cr: (k, i, 0))
    dtypes = [_F32] + ([wire_dtype] if wire_dtype else [])
    res = pl.pallas_call(
        _fused_exchange(body, 3, len(dtypes), 0, nx, len(xshapes), plan, (nk, nt)) if exchange else body,
        grid_spec=pltpu.PrefetchScalarGridSpec(
            num_scalar_prefetch=1, grid=(nk, nt),
            in_specs=[pl.BlockSpec((None, tr, c), lambda k, i, cr: (k, cr[0] * nt + i, 0)), out_spec] + [_HBM] * nx,
            out_specs=[out_spec] * len(dtypes) + [_HBM] * len(xshapes), scratch_shapes=list(sems)),
        out_shape=[jax.ShapeDtypeStruct(r.shape, dt) for dt in dtypes] + list(xshapes),
        compiler_params=_cparams(("arbitrary", "arbitrary")), name=name,
    )(cidx, g, r, *xin)
    return res if wire_dtype or exchange else res[0]


def _chip_exchange(ss):
    n = len(ss)

    def body(*refs):
        start, finish = _chip_reduce_plan(refs[:n], refs[n:2 * n], *refs[2 * n:])
        start()
        finish()

    return pl.pallas_call(
        body, in_specs=[_HBM] * n, out_specs=[_HBM] * n,
        out_shape=[jax.ShapeDtypeStruct((3,) + s.shape[1:], s.dtype) for s in ss],
        scratch_shapes=_chip_reduce_sems(n), name="grad_chip_exchange",
    )(*ss)


def _chip_add(s, r, kidx, name):
    _, r2, c = r.shape
    tr = _row_tile(r2)

    def body(k_ref, s_ref, r_ref, o_ref):
        o_ref[...] = ((s_ref[...] + r_ref[0].astype(_F32)) + r_ref[1].astype(_F32)) + r_ref[2].astype(_F32)

    return pl.pallas_call(
        body, grid_spec=pltpu.PrefetchScalarGridSpec(
            num_scalar_prefetch=1, grid=(r2 // tr,),
            in_specs=[pl.BlockSpec((None, tr, c), lambda i, kr: (kr[0], i, 0)),
                      pl.BlockSpec((3, tr, c), lambda i, kr: (0, i, 0))],
            out_specs=pl.BlockSpec((tr, c), lambda i, kr: (i, 0))),
        out_shape=jax.ShapeDtypeStruct((r2, c), _F32),
        compiler_params=_cparams(("arbitrary",)), name=name,
    )(kidx, s, r)


def _pair_swap_plan(ins, outs, ssem, rsem):
    xi, yi, ci = _pos()
    sends = [_remote(ins[a], outs[a], ssem.at[a], rsem.at[a], (xi, yi, 1 - ci)) for a in range(len(ins))]

    def start():
        for cp in sends:
            cp.start()

    def finish():
        for cp in sends:
            cp.wait_recv()
        for cp in sends:
            cp.wait_send()

    return start, finish


def _pair_swap(hs, name):
    n = len(hs)

    def body(*refs):
        start, finish = _pair_swap_plan(refs[:n], refs[n:2 * n], *refs[2 * n:])
        start()
        finish()

    return pl.pallas_call(
        body, in_specs=[_HBM] * n, out_specs=[_HBM] * n,
        out_shape=[jax.ShapeDtypeStruct(h.shape, h.dtype) for h in hs],
        scratch_shapes=[pltpu.SemaphoreType.DMA((n,)), pltpu.SemaphoreType.DMA((n,))], name=name,
    )(*hs)


def _small_sum(every):
    def body(all_ref, sum_ref):
        tot = all_ref[0]
        for dev in range(1, 8):
            tot = tot + all_ref[dev]
        sum_ref[...] = tot

    return pl.pallas_call(
        body, in_specs=[_VMEM], out_specs=_VMEM, out_shape=jax.ShapeDtypeStruct(every.shape[1:], _F32),
        compiler_params=pltpu.CompilerParams(vmem_limit_bytes=_VMEM_LIMIT), name="small_sum",
    )(every)


def _adamw(w, g, m, v):
    m = _ADAM_B1 * m + (1.0 - _ADAM_B1) * g
    v = _ADAM_B2 * v + (1.0 - _ADAM_B2) * (g * g)
    m_hat = m / (1.0 - _ADAM_B1 ** _ADAM_STEP)
    v_hat = v / (1.0 - _ADAM_B2 ** _ADAM_STEP)
    return -_ADAM_LR * (m_hat / (jnp.sqrt(v_hat) + _ADAM_EPS) + _ADAM_WD * w), m, v


def _adamw_big(w, g_mine, g_theirs, m, v, cidx, name):
    r, c = w.shape
    tr = _row_tile(r // 2)
    nt = r // 2 // tr

    def body(c_ref, w_ref, gm_ref, gt_ref, m_ref, v_ref, g_ref, d_ref, mo_ref, vo_ref):
        g = jnp.where(pl.program_id(0) // nt == c_ref[0], gm_ref[...], gt_ref[...])
        g_ref[...] = g
        d_ref[...], mo_ref[...], vo_ref[...] = _adamw(w_ref[...], g, m_ref[...], v_ref[...])

    spec = pl.BlockSpec((tr, c), lambda i, cr: (i, 0))
    half = pl.BlockSpec((tr, c), lambda i, cr: (i % nt, 0))
    return pl.pallas_call(
        body, grid_spec=pltpu.PrefetchScalarGridSpec(
            num_scalar_prefetch=1, grid=(2 * nt,), in_specs=[spec, half, half, spec, spec], out_specs=[spec] * 4),
        out_shape=[jax.ShapeDtypeStruct((r, c), _F32)] * 4,
        compiler_params=_cparams(("arbitrary",)), name=name,
    )(cidx, w, g_mine, g_theirs, m, v)


def _adamw_small(ws, gs, ms, vs):
    n = len(ws)
    summed = [i for i in range(n) if gs[i].shape != ws[i].shape]

    def body(*refs):
        w_r, g_r, m_r, v_r = (refs[i * n:(i + 1) * n] for i in range(4))
        outs = refs[4 * n:]
        for i in range(n):
            g = g_r[i][...]
            if i in summed:
                g = _rowsum(g)
                outs[3 * n + summed.index(i)][...] = g
            outs[i][...], outs[n + i][...], outs[2 * n + i][...] = _adamw(w_r[i][...], g, m_r[i][...], v_r[i][...])

    shapes = [jax.ShapeDtypeStruct(w.shape, _F32) for w in ws]
    res = pl.pallas_call(
        body, in_specs=[_VMEM] * (4 * n), out_specs=[_VMEM] * (3 * n + len(summed)),
        out_shape=shapes * 3 + [shapes[i] for i in summed],
        compiler_params=pltpu.CompilerParams(vmem_limit_bytes=_VMEM_LIMIT), name="adamw_small",
    )(*ws, *gs, *ms, *vs)
    gs = list(gs)
    for pos, i in enumerate(summed):
        gs[i] = res[3 * n + pos]
    return gs, res[:n], res[n:2 * n], res[2 * n:3 * n]


def _ada_bwd(c_all, dmod_k, w, m, v):
    d, nk = w.shape
    tn = 512 if nk % 512 == 0 else nk

    def body(c_ref, dm_ref, w_ref, m_ref, v_ref, g_ref, d_ref, mo_ref, vo_ref):
        ca = c_ref[...]
        g = _dot_tn(_mx(ca * jax.nn.sigmoid(ca)), _mx(dm_ref[...]))
        g_ref[...] = g
        d_ref[...], mo_ref[...], vo_ref[...] = _adamw(w_ref[...], g, m_ref[...], v_ref[...])

    col = pl.BlockSpec((d, tn), lambda j: (0, j))
    return pl.pallas_call(
        body, grid=(nk // tn,),
        in_specs=[pl.BlockSpec(c_all.shape, lambda j: (0, 0)), pl.BlockSpec((c_all.shape[0], tn), lambda j: (0, j)),
                  col, col, col],
        out_specs=[col] * 4, out_shape=[jax.ShapeDtypeStruct((d, nk), _F32)] * 4,
        compiler_params=_cparams(("arbitrary",)), name="ada_bwd",
    )(c_all, dmod_k, w, m, v)


def _block_diag(wh):
    hn, dh, _ = wh.shape
    eye = jnp.eye(hn, dtype=wh.dtype)
    return (eye[:, None, :, None] * wh[:, :, None, :]).reshape(hn * dh, hn * dh)


def _pack(pieces):
    out = []
    for p in pieces:
        flat = p.reshape(-1, 128)
        out.append(jnp.pad(flat, ((0, (-flat.shape[0]) % 8), (0, 0))))
    return jnp.concatenate(out, axis=0)


def _unpack(pack, shapes):
    out, off = [], 0
    for shp in shapes:
        rows = math.prod(shp) // 128
        out.append(pack[..., off:off + rows, :].reshape(pack.shape[:-2] + tuple(shp)))
        off += rows + (-rows) % 8
    return out


_WEIGHTS = ('w_ada', 'b_ada', 'w_in', 'lru_conv_w', 'lru_conv_b', 'lru_w_r', 'lru_b_r', 'lru_w_i', 'lru_b_i', 'lru_lambda',
            'conv_w', 'conv_b', 'conv_norm_g', 'conv_norm_b', 'w_out', 'ln1_g', 'ln1_b', 'ffn_w_up', 'ffn_conv_w',
            'ffn_conv_b', 'ffn_w_down', 'ln2_g', 'ln2_b')
_BIG = ('w_in', 'w_out', 'ffn_w_up', 'ffn_w_down')


def kernel(x, c, w_ada, b_ada, w_in, lru_conv_w, lru_conv_b, lru_w_r, lru_b_r, lru_w_i, lru_b_i, lru_lambda, conv_w, conv_b, conv_norm_g, conv_norm_b, w_out, ln1_g, ln1_b, ffn_w_up, ffn_conv_w, ffn_conv_b, ffn_w_down, ln2_g, ln2_b, loss_target, m_w_ada, m_b_ada, m_w_in, m_lru_conv_w, m_lru_conv_b, m_lru_w_r, m_lru_b_r, m_lru_w_i, m_lru_b_i, m_lru_lambda, m_conv_w, m_conv_b, m_conv_norm_g, m_conv_norm_b, m_w_out, m_ln1_g, m_ln1_b, m_ffn_w_up, m_ffn_conv_w, m_ffn_conv_b, m_ffn_w_down, m_ln2_g, m_ln2_b, v_w_ada, v_b_ada, v_w_in, v_lru_conv_w, v_lru_conv_b, v_lru_w_r, v_lru_b_r, v_lru_w_i, v_lru_b_i, v_lru_lambda, v_conv_w, v_conv_b, v_conv_norm_g, v_conv_norm_b, v_w_out, v_ln1_g, v_ln1_b, v_ffn_w_up, v_ffn_conv_w, v_ffn_conv_b, v_ffn_w_down, v_ln2_g, v_ln2_b):
    given = dict(locals())
    wt = {n: given[n] for n in _WEIGHTS}
    mo = {n: given["m_" + n] for n in _WEIGHTS}
    vo = {n: given["v_" + n] for n in _WEIGHTS}
    bl, s_len, d = x.shape
    wd = d // 2
    tokens = bl * s_len
    xi, yi, ci = _pos()
    kme = 2 * xi + yi
    kidx = jnp.reshape(kme, (1,)).astype(jnp.int32)
    cidx = jnp.reshape(ci, (1,)).astype(jnp.int32)

    nk = w_ada.shape[2]
    c8 = jnp.pad(c, ((0, 8 - bl), (0, 0)))
    c_all, mod8, win, wout_s, lcw_s, cw_s, fcw_s = _ada_fwd(
        c8, w_ada[0], lax.dynamic_slice(b_ada, (0, kme * nk), (1, nk)),
        [_mx(w_in[0]), _mx(w_out[0]), lru_conv_w[0], conv_w[0], ffn_conv_w[0]])
    mod3 = mod8[:bl].reshape(bl, 1, 6 * d)
    wout = wout_s.reshape(d, d)
    f = 4 * ffn_w_down.shape[1]
    unshard = lambda t: jnp.transpose(t, (1, 0, 2)).reshape(t.shape[1], -1)
    lcw, cw, fcw = unshard(lcw_s), unshard(cw_s), unshard(fcw_s)
    wr_bd, wi_bd = _mx(_block_diag(lru_w_r[0])), _mx(_block_diag(lru_w_i[0]))
    seg = _block_diag(jnp.ones((_N_HEADS, wd // _N_HEADS, wd // _N_HEADS), jnp.bfloat16))
    mixer_small = (lcw, lru_conv_b, wr_bd, wi_bd, lru_b_r, lru_b_i, lru_lambda, cw, conv_b, conv_norm_g, conv_norm_b, seg, wout, ln1_g)

    proj, h, mix, x1, u1, y, vbc, lru, wup, wdn_s = _mix_fwd(x, mod3, win, *mixer_small, ln1_b, [_mx(ffn_w_up[0]), _mx(ffn_w_down[0])])
    wdn = wdn_s.reshape(f, d)
    u2, hh, fact, gc_all, dz2, loss_acc, dln2, dgt2 = _ffn_fwd(x1, mod3, wup, fcw, ffn_conv_b, wdn, ln2_g, ln2_b, loss_target)
    dx1, dy2, dh, dfc, dmod2 = _ffn_bwd(dz2, x1, hh, gc_all, mod3, wup, wdn, fcw, ffn_conv_b)

    flat = lambda t: t.reshape(tokens, t.shape[-1])
    fc = wup.shape[2]
    g_up = _wgrad(flat(u2), flat(dh), d, fc, 1, 4, 0, 0, "wgrad_up")
    g_dn, r_up = _wgrad(flat(fact), flat(dy2), fc, d, f // fc, 1, 0, 0, "wgrad_down",
                        exchange=([g_up], _pair_out_shapes([g_up]), _pair_plan, _pair_sems(1)))
    g_dn = g_dn.reshape(4, f // 4, d)
    s_up, r_dn = _pair_add(g_up, r_up, cidx, "grad_pair_add_ffn_w_up",
                           exchange=([g_dn], _pair_out_shapes([g_dn]), _pair_plan, _pair_sems(1)))
    ffn_sum = [s_up, _pair_add(g_dn, r_dn, cidx, "grad_pair_add_ffn_w_down")]
    grad_x, dproj, dmix, xcg, vecw, dlcw, dcw, dln1, dmod1, *ffn_recv = _mix_bwd(
        dx1, x, mix, proj, h, vbc, lru, mod3, win, *mixer_small, ffn_sum)
    g_ri = _wgrad(flat(xcg), flat(xcg), wd, wd, 1, 2, 0, 1, "wgrad_gates")
    dh_ = wd // _N_HEADS
    on_diagonal = jnp.eye(_N_HEADS, dtype=_F32)[None, :, None, :, None]
    g_ri = jnp.sum(g_ri.reshape(2, _N_HEADS, dh_, _N_HEADS, dh_) * on_diagonal, axis=3)

    dmod = jnp.concatenate([dmod1.reshape(bl, 3 * d), dmod2.reshape(bl, 2 * d), dgt2.reshape(bl, d)], axis=1)
    pieces = [vecw, dlcw, dcw, jnp.concatenate([dln1, dln2], axis=0), dfc, g_ri, loss_acc[:, 0:128],
              jnp.pad(dmod, ((0, 8 - bl), (0, 0)))]
    shapes = [p.shape for p in pieces]
    pack = _pack(pieces)
    g_in, every = _wgrad(flat(u1), flat(dproj), d, wd, 1, 4, 0, 0, "wgrad_in", exchange=(
        [pack], [jax.ShapeDtypeStruct((8,) + pack.shape, _F32)], _dev_gather_plan, _dev_gather_sems(1)))
    wire_shape = lambda t: [jax.ShapeDtypeStruct((3,) + t.shape[1:], t.dtype)]
    ffn_half = [_chip_add(s, r, kidx, "grad_chip_add_" + n) for s, r, n in zip(ffn_sum, ffn_recv, _BIG[2:])]
    same = lambda ts: [jax.ShapeDtypeStruct(t.shape, t.dtype) for t in ts]
    r_in, = _pair_exchange([g_in], "grad_pair_exchange_w_in")
    s_in, wire_in, *ffn_theirs = _pair_add(g_in, r_in, cidx, "grad_pair_add_w_in", jnp.bfloat16,
                                           exchange=(ffn_half, same(ffn_half), _pair_swap_plan, _pair_sems(2)))
    g_out, recv_in = _wgrad(flat(y), flat(dmix), d, d, 1, 1, 0, 0, "wgrad_out", exchange=(
        [wire_in], wire_shape(wire_in), _chip_reduce_plan, _chip_reduce_sems(1)))
    g_out = g_out.reshape(4, d // 4, d)
    r_out, = _pair_exchange([g_out], "grad_pair_exchange_w_out")
    s_out, wire_out = _pair_add(g_out, r_out, cidx, "grad_pair_add_w_out", jnp.bfloat16)
    recv_out, = _chip_exchange([wire_out])
    mix_half = [_chip_add(s, r, kidx, "grad_chip_add_" + n) for s, r, n in zip([s_in, s_out], [recv_in, recv_out], _BIG)]
    half, other = mix_half + ffn_half, list(_pair_swap(mix_half, "grad_pair_swap")) + ffn_theirs
    grads, deltas, new_m, new_v = {}, {}, {}, {}
    for n, mine, theirs in zip(_BIG, half, other):
        g, dl, mm, vv = _adamw_big(wt[n][0], mine, theirs, mo[n][0], vo[n][0], cidx, "adamw_" + n)
        grads[n], deltas[n], new_m[n], new_v[n] = g[None], dl[None], mm[None], vv[None]

    vecw, dlcw, dcw, dln, dfc, g_ri, loss_sum, dmod_sum = _unpack(_small_sum(every), shapes)
    loss = 0.5 * loss_sum[0, 0] / d
    dmod_all = _unpack(every, shapes)[-1].reshape(64, 6 * d)

    g_ada, dl, mm, vv = _ada_bwd(c_all, lax.dynamic_slice(dmod_all, (0, kme * nk), (64, nk)), w_ada[0], m_w_ada[0], v_w_ada[0])
    grads['w_ada'], deltas['w_ada'], new_m['w_ada'], new_v['w_ada'] = g_ada[None], dl[None], mm[None], vv[None]

    shard = lambda t, width: lax.dynamic_slice(t, (0, kme * width), (t.shape[0], width))
    small = {
        'b_ada': dmod_sum, 'lru_conv_w': shard(dlcw, wd // 4), 'lru_conv_b': vecw[0:1], 'lru_w_r': g_ri[0], 'lru_b_r': vecw[1:2],
        'lru_w_i': g_ri[1], 'lru_b_i': vecw[2:3], 'lru_lambda': vecw[3:4], 'conv_w': shard(dcw, wd // 4), 'conv_b': vecw[4:5],
        'conv_norm_g': vecw[5:6], 'conv_norm_b': vecw[6:7], 'ln1_g': dln[0:1], 'ln1_b': dln[1:2],
        'ffn_conv_w': shard(dfc[0:3], f // 4), 'ffn_conv_b': dfc[3:4], 'ln2_g': dln[2:3], 'ln2_b': dln[3:4]}
    names = list(small)
    gs = [small[n] if n == 'b_ada' else small[n].reshape(wt[n].shape) for n in names]
    gs, dls, mms, vvs = _adamw_small([wt[n] for n in names], gs, [mo[n] for n in names], [vo[n] for n in names])
    for n, g, dl, mm, vv in zip(names, gs, dls, mms, vvs):
        grads[n], deltas[n], new_m[n], new_v[n] = g, dl, mm, vv

    return (loss, grad_x, *[grads[n] for n in _WEIGHTS], *[deltas[n] for n in _WEIGHTS],
            *[new_m[n] for n in _WEIGHTS], *[new_v[n] for n in _WEIGHTS])
```

```python
import functools
import itertools
import math

import jax
import jax.numpy as jnp
from jax import lax
from jax.experimental import pallas as pl
from jax.experimental.pallas import tpu as pltpu

_MXU_DT = jnp.bfloat16
_F32 = jnp.float32
_VMEM_LIMIT = 56 * 1024 * 1024
_TT_MIX = 256
_TT_MIX_FWD = 512
_TT_FFN = 256
_TK_WGRAD = 2048
_HALO = 32

_LRU_C = 8.0
_LN_EPS = 1e-5
_N_HEADS = 8
_DEPTH = 1
_ALPHA = (2 * _DEPTH) ** 0.25
_ADAM_LR, _ADAM_B1, _ADAM_B2, _ADAM_EPS, _ADAM_WD, _ADAM_STEP = 0.001, 0.9, 0.999, 1e-08, 0.01, 10

_MESH = pl.DeviceIdType.MESH
_CHIP_DELTAS = ((1, 0), (0, 1), (1, 1))


def _cparams(sem):
    return pltpu.CompilerParams(dimension_semantics=sem, vmem_limit_bytes=_VMEM_LIMIT)


def _resident(shape):
    nd = len(shape)
    return pl.BlockSpec(shape, lambda *_: (0,) * nd, pipeline_mode=pl.Buffered(1))


def _dot(a, b):
    return jnp.dot(a, b, preferred_element_type=_F32)


def _dot_nt(a, b):
    return lax.dot_general(a, b, (((1,), (1,)), ((), ())), preferred_element_type=_F32)


def _dot_tn(a, b):
    return lax.dot_general(a, b, (((0,), (0,)), ((), ())), preferred_element_type=_F32)


def _mx(v):
    return v.astype(_MXU_DT)


def _expm1(v):
    series = v * (1.0 + v * (1.0 / 2 + v * (1.0 / 6 + v * (1.0 / 24 + v * (1.0 / 120)))))
    return jnp.where(jnp.abs(v) < 0.0625, series, jnp.exp(v) - 1.0)


def _softplus(z):
    e = jnp.exp(-jnp.abs(z))
    u = 1.0 + e
    log1p = jnp.where(u == 1.0, e, jnp.log(u) * e / jnp.where(u == 1.0, 1.0, u - 1.0))
    return jnp.maximum(z, 0.0) + log1p


_GELU_C = math.sqrt(2.0 / math.pi)


def _gelu_and_grad(v):
    t = jnp.tanh(_GELU_C * (v + 0.044715 * v * v * v))
    val = 0.5 * v * (1.0 + t)
    grad = 0.5 * (1.0 + t) + 0.5 * v * (1.0 - t * t) * _GELU_C * (1.0 + 3 * 0.044715 * v * v)
    return val, grad


def _seg_sum(v, seg, passes=3):
    hi = v.astype(jnp.bfloat16)
    r1 = v - hi.astype(_F32)
    mid = r1.astype(jnp.bfloat16)
    out = _dot(hi, seg) + _dot(mid, seg)
    if passes == 3:
        out = out + _dot((r1 - mid.astype(_F32)).astype(jnp.bfloat16), seg)
    return out


def _scan_fwd(a, u, h0):
    n = a.shape[0]
    row = lax.broadcasted_iota(jnp.int32, a.shape, 0)
    h, d = u, 1
    while d < n:
        keep = row >= d
        h = a * jnp.where(keep, pltpu.roll(h, d, 0), 0.0) + h
        a = a * jnp.where(keep, pltpu.roll(a, d, 0), 1.0)
        d *= 2
    return h + a * h0


def _scan_rev(c, g, g_end):
    n = c.shape[0]
    row = lax.broadcasted_iota(jnp.int32, c.shape, 0)
    d = 1
    while d < n:
        keep = row < n - d
        g = c * jnp.where(keep, pltpu.roll(g, n - d, 0), 0.0) + g
        c = c * jnp.where(keep, pltpu.roll(c, n - d, 0), 1.0)
        d *= 2
    return g + c * g_end


def _layer_norm_stats(z):
    mu = jnp.mean(z, axis=-1, keepdims=True)
    zc = z - mu
    var = jnp.mean(zc * zc, axis=-1, keepdims=True)
    rstd = lax.rsqrt(var + _LN_EPS)
    return zc * rstd, rstd


def _layer_norm_bwd(dn, n, rstd):
    return rstd * (dn - jnp.mean(dn, axis=-1, keepdims=True) - n * jnp.mean(dn * n, axis=-1, keepdims=True))


def _rowsum(v):
    return jnp.sum(v, axis=0, keepdims=True)


def _fused_exchange(body, n_in, n_out, n_scratch, n_xin, n_xout, plan, grid):
    def wrapped(*refs):
        o0 = n_in + n_xin
        s0 = o0 + n_out + n_xout
        start, finish = plan(refs[n_in:o0], refs[o0 + n_out:s0], *refs[s0 + n_scratch:])
        step = 0
        for axis, size in enumerate(grid):
            step = step * size + pl.program_id(axis)

        @pl.when(step == 0)
        def _():
            start()

        body(*refs[:n_in], *refs[o0:o0 + n_out], *refs[s0:s0 + n_scratch])

        @pl.when(step == math.prod(grid) - 1)
        def _():
            finish()

    return wrapped


def _lru_gates(xc, wr_ref, wi_ref, br_ref, bi_ref, lam_ref):
    xcb = _mx(xc)
    r = jax.nn.sigmoid(_dot(xcb, wr_ref[...]) + br_ref[...])
    i = jax.nn.sigmoid(_dot(xcb, wi_ref[...]) + bi_ref[...])
    sp = _softplus(-lam_ref[...])
    log_a = -_LRU_C * r * sp
    a = jnp.exp(log_a)
    mult = jnp.sqrt(-_expm1(2.0 * log_a))
    return r, i, sp, a, mult


def _conv_taps(ext_ref, w_ref, first, n_taps, tt):
    acc = w_ref[0:1, :] * ext_ref[pl.ds(first, tt), :]
    for k in range(1, n_taps):
        acc = acc + w_ref[k:k + 1, :] * ext_ref[pl.ds(first + k, tt), :]
    return acc


def _make_shifted(ext_ref, sh_ref):
    n = sh_ref.shape[1]
    for r in range(1, 8):
        sh_ref[r - 1] = ext_ref[pl.ds(r, n), :]


def _tap(ext_ref, sh_ref, off, tt):
    base = (off // 8) * 8
    if off % 8 == 0:
        return ext_ref[pl.ds(base, tt), :]
    return sh_ref[off % 8 - 1, pl.ds(base, tt), :]


def _conv_taps_shifted(ext_ref, sh_ref, w_ref, first, n_taps, tt):
    acc = w_ref[0:1, :] * _tap(ext_ref, sh_ref, first, tt)
    for k in range(1, n_taps):
        acc = acc + w_ref[k:k + 1, :] * _tap(ext_ref, sh_ref, first + k, tt)
    return acc


def _mix_fwd(x, mod3, win, lcw, lcb, wr_bd, wi_bd, b_r, b_i, lam, cw, cb, ng, nb, seg, wout, ln1g, ln1b, shards):
    bl, s_len, d = x.shape
    w = d // 2
    tt = min(_TT_MIX_FWD, s_len)
    ns = s_len // tt
    kc = cw.shape[0]

    def body(x_ref, mod_ref, win_ref, lcw_ref, lcb_ref, wr_ref, wi_ref, br_ref, bi_ref, lam_ref, cw_ref, cb_ref,
             ng_ref, nb_ref, seg_ref, wout_ref, g1_ref, b1_ref,
             proj_ref, h_ref, mix_ref, x1_ref, u1_ref, y_ref, vbc_ref, lru_ref, ext4, ext31, sh31, hcar):
        @pl.when(pl.program_id(1) == 0)
        def _():
            ext4[0:8, :] = jnp.zeros((8, w), _F32)
            ext31[0:_HALO, :] = jnp.zeros((_HALO, w), _F32)
            hcar[...] = jnp.zeros_like(hcar)

        xt = x_ref[...]
        sh1, sc1, gt1 = mod_ref[:, 0:d], mod_ref[:, d:2 * d], mod_ref[:, 2 * d:3 * d]
        u1 = _mx(xt * (1.0 + sc1) + sh1)
        u1_ref[...] = u1
        xa, ga, vb, gb = (_dot(u1, win_ref[k]) for k in range(4))
        proj_ref[:, 0:w] = xa
        proj_ref[:, w:2 * w] = ga
        proj_ref[:, 2 * w:3 * w] = vb
        proj_ref[:, 3 * w:4 * w] = gb

        ext4[8:8 + tt, :] = xa
        xc = lcb_ref[...] + _conv_taps(ext4, lcw_ref, 5, 4, tt)
        ext4[0:8, :] = xa[tt - 8:tt, :]
        r, i, sp, a, mult = _lru_gates(xc, wr_ref, wi_ref, br_ref, bi_ref, lam_ref)
        for k, val in enumerate((xc, r, i, a, mult)):
            lru_ref[:, k * w:(k + 1) * w] = val
        h = _scan_fwd(a, mult * (i * xc), hcar[0:1, :])
        hcar[0:1, :] = h[tt - 1:tt, :]
        h_ref[...] = h
        gelu, _ = _gelu_and_grad(ga)
        y_ref[:, 0:w] = _mx(gelu * h)

        vbg = vb * jax.nn.sigmoid(gb)
        ext31[_HALO:_HALO + tt, :] = vbg
        _make_shifted(ext31, sh31)
        vbc = cb_ref[...] + _conv_taps_shifted(ext31, sh31, cw_ref, _HALO - (kc - 1), kc, tt)
        vbc_ref[...] = vbc
        ext31[0:_HALO, :] = vbg[tt - _HALO:tt, :]
        inv = 1.0 / (w // _N_HEADS)
        zc = vbc - _seg_sum(vbc, seg_ref[...]) * inv
        n = zc * lax.rsqrt(_seg_sum(zc * zc, seg_ref[...]) * inv + _LN_EPS)
        pre = n * ng_ref[...] + nb_ref[...]
        y_ref[:, w:2 * w] = _mx(pre * jax.nn.sigmoid(pre))

        mix = _dot(y_ref[...], wout_ref[...])
        mix_ref[...] = mix
        n1, _ = _layer_norm_stats(_ALPHA * xt + (1.0 + gt1) * mix)
        x1_ref[...] = n1 * g1_ref[...] + b1_ref[...]

    tok = lambda c: pl.BlockSpec((None, tt, c), lambda b, s: (b, s, 0))
    smalls = [lcw, lcb, wr_bd, wi_bd, b_r, b_i, lam, cw, cb, ng, nb, seg, wout, ln1g, ln1b]
    nx = len(shards)
    return pl.pallas_call(
        _fused_exchange(body, 3 + len(smalls), 8, 4, nx, nx, _gather_plan, (bl, ns)), grid=(bl, ns),
        in_specs=[tok(d), pl.BlockSpec((None, 1, 6 * d), lambda b, s: (b, 0, 0)), _resident(win.shape)]
        + [_resident(t.shape) for t in smalls] + [_HBM] * nx,
        out_specs=[tok(4 * w), tok(w), tok(d), tok(d), tok(d), tok(d), tok(w), tok(5 * w)] + [_HBM] * nx,
        out_shape=[jax.ShapeDtypeStruct((bl, s_len, 4 * w), _F32), jax.ShapeDtypeStruct((bl, s_len, w), _F32),
                   jax.ShapeDtypeStruct((bl, s_len, d), _F32), jax.ShapeDtypeStruct((bl, s_len, d), _F32),
                   jax.ShapeDtypeStruct((bl, s_len, d), _MXU_DT), jax.ShapeDtypeStruct((bl, s_len, d), _MXU_DT),
                   jax.ShapeDtypeStruct((bl, s_len, w), _F32), jax.ShapeDtypeStruct((bl, s_len, 5 * w), _F32)]
        + [jax.ShapeDtypeStruct((4,) + t.shape, t.dtype) for t in shards],
        scratch_shapes=[pltpu.VMEM((tt + 8, w), _F32), pltpu.VMEM((tt + _HALO, w), _F32),
                        pltpu.VMEM((7, tt + _HALO - 8, w), _F32), pltpu.VMEM((8, w), _F32)] + _gather_sems(nx),
        compiler_params=_cparams(("arbitrary", "arbitrary")), name="mix_fwd",
    )(x, mod3, win, *smalls, *shards)


def _ffn_fwd(x1, mod3, wup, fcw, fcb, wdn, ln2g, ln2b, target):
    bl, s_len, d = x1.shape
    nch, _, fc = wup.shape
    nch //= 2
    f = nch * fc
    tt = min(_TT_FFN, s_len)
    ns = s_len // tt

    def body(x1_ref, mod_ref, wup_ref, fcw_ref, fcb_ref, wdn_ref, g2_ref, b2_ref, tgt_ref,
             u2_ref, hh_ref, f_ref, gc_ref, dz2_ref, loss_ref, dln2_ref, dgt2_ref, ext3):
        first_tile = pl.program_id(1) == 0

        @pl.when(first_tile)
        def _():
            ext3[:, 0:8, :] = jnp.zeros((nch, 8, fc), _F32)
            dgt2_ref[...] = jnp.zeros_like(dgt2_ref)

        @pl.when(first_tile & (pl.program_id(0) == 0))
        def _():
            loss_ref[...] = jnp.zeros_like(loss_ref)
            dln2_ref[...] = jnp.zeros_like(dln2_ref)

        x1t = x1_ref[...]
        sh2, sc2, gt2 = mod_ref[:, 3 * d:4 * d], mod_ref[:, 4 * d:5 * d], mod_ref[:, 5 * d:6 * d]
        u2 = _mx(x1t * (1.0 + sc2) + sh2)
        u2_ref[...] = u2
        y2 = jnp.zeros((tt, d), _F32)
        for j in range(nch):
            lanes = slice(j * fc, (j + 1) * fc)
            v = _dot(u2, wup_ref[j])
            g = _dot(u2, wup_ref[nch + j])
            hh_ref[:, lanes] = v.astype(hh_ref.dtype)
            hh_ref[:, f + j * fc:f + (j + 1) * fc] = g.astype(hh_ref.dtype)
            ext = ext3.at[j]
            ext[8:8 + tt, :] = g
            gc = fcb_ref[:, lanes] + sum(fcw_ref[k:k + 1, lanes] * ext[pl.ds(6 + k, tt), :] for k in range(3))
            gc_ref[:, lanes] = gc
            ext[0:8, :] = g[tt - 8:tt, :]
            fj = _mx(gc * jax.nn.sigmoid(gc) * v)
            f_ref[:, lanes] = fj
            y2 = y2 + _dot(fj, wdn_ref[lanes, :])

        n2, rstd = _layer_norm_stats(_ALPHA * x1t + (1.0 + gt2) * y2)
        err = n2 * g2_ref[...] + b2_ref[...] - tgt_ref[...]
        loss_ref[...] += jnp.sum(_rowsum(err * err), axis=1, keepdims=True)
        dout = err * (1.0 / d)
        dln2_ref[0:1, :] += _rowsum(dout * n2)
        dln2_ref[1:2, :] += _rowsum(dout)
        dz2 = _layer_norm_bwd(dout * g2_ref[...], n2, rstd)
        dz2_ref[...] = dz2
        dgt2_ref[...] += _rowsum(dz2 * y2)

    tok = lambda c: pl.BlockSpec((None, tt, c), lambda b, s: (b, s, 0))
    acc = lambda r: pl.BlockSpec((r, d), lambda b, s: (0, 0))
    smalls = [fcw, fcb, wdn, ln2g, ln2b]
    return pl.pallas_call(
        body, grid=(bl, ns),
        in_specs=[tok(d), pl.BlockSpec((None, 1, 6 * d), lambda b, s: (b, 0, 0)), _resident(wup.shape)]
        + [_resident(t.shape) for t in smalls] + [tok(d)],
        out_specs=[tok(d), tok(2 * f), tok(f), tok(f), tok(d), acc(1), acc(2), pl.BlockSpec((None, 1, d), lambda b, s: (b, 0, 0))],
        out_shape=[jax.ShapeDtypeStruct((bl, s_len, d), _MXU_DT), jax.ShapeDtypeStruct((bl, s_len, 2 * f), _F32),
                   jax.ShapeDtypeStruct((bl, s_len, f), _MXU_DT), jax.ShapeDtypeStruct((bl, s_len, f), _F32),
                   jax.ShapeDtypeStruct((bl, s_len, d), _F32), jax.ShapeDtypeStruct((1, d), _F32), jax.ShapeDtypeStruct((2, d), _F32),
                   jax.ShapeDtypeStruct((bl, 1, d), _F32)],
        scratch_shapes=[pltpu.VMEM((nch, tt + 8, fc), _F32)],
        compiler_params=_cparams(("arbitrary", "arbitrary")), name="ffn_fwd",
    )(x1, mod3, wup, *smalls, target)


def _ffn_bwd(dz2, x1, hh, gc_all, mod3, wup, wdn, fcw, fcb):
    bl, s_len, d = x1.shape
    nch, _, fc = wup.shape
    nch //= 2
    f = nch * fc
    tt = min(_TT_FFN, s_len)
    ns = s_len // tt

    def body(dz2_ref, x1_ref, hh_ref, gc_ref, mod_ref, wup_ref, wdn_ref, fcw_ref, fcb_ref,
             dx1_ref, dy2_ref, dh_ref, dfc_ref, dmod_ref, dext, dcar):
        @pl.when(pl.program_id(1) == 0)
        def _():
            dcar[...] = jnp.zeros_like(dcar)
            dmod_ref[...] = jnp.zeros_like(dmod_ref)

        @pl.when((pl.program_id(1) == 0) & (pl.program_id(0) == 0))
        def _():
            dfc_ref[...] = jnp.zeros_like(dfc_ref)

        sc2, gt2 = mod_ref[:, 4 * d:5 * d], mod_ref[:, 5 * d:6 * d]
        dz2t = dz2_ref[...]
        dy2 = _mx((1.0 + gt2) * dz2t)
        dy2_ref[...] = dy2
        du2 = jnp.zeros((tt, d), _F32)
        for j in range(nch):
            lanes = slice(j * fc, (j + 1) * fc)
            glanes = slice(f + j * fc, f + (j + 1) * fc)
            v = hh_ref[:, lanes].astype(_F32)
            g = hh_ref[:, glanes].astype(_F32)
            gc = gc_ref[:, lanes]
            sg = jax.nn.sigmoid(gc)
            df = _dot_nt(dy2, wdn_ref[lanes, :])
            dv = df * (gc * sg)
            dgc = df * v * (sg * (1.0 + gc * (1.0 - sg)))
            dfc_ref[3:4, lanes] += _rowsum(dgc)
            dext[0:tt, :] = dgc
            dext[tt:tt + 8, :] = dcar[j]
            dcar[j] = dgc[0:8, :]
            dg = jnp.zeros((tt, fc), _F32)
            for k in range(3):
                shifted = dext[pl.ds(2 - k, tt), :]
                dg = dg + fcw_ref[k:k + 1, lanes] * shifted
                dfc_ref[k:k + 1, lanes] += _rowsum(shifted * g)
            dvb, dgb = _mx(dv), _mx(dg)
            dh_ref[:, lanes] = dvb
            dh_ref[:, glanes] = dgb
            du2 = du2 + _dot_nt(dvb, wup_ref[j]) + _dot_nt(dgb, wup_ref[nch + j])

        dx1_ref[...] = _ALPHA * dz2t + du2 * (1.0 + sc2)
        dmod_ref[0:1, :] += _rowsum(du2)
        dmod_ref[1:2, :] += _rowsum(du2 * x1_ref[...])

    tok = lambda c: pl.BlockSpec((None, tt, c), lambda b, i: (b, ns - 1 - i, 0))
    return pl.pallas_call(
        body, grid=(bl, ns),
        in_specs=[tok(d), tok(d), tok(2 * f), tok(f), pl.BlockSpec((None, 1, 6 * d), lambda b, i: (b, 0, 0)),
                  _resident(wup.shape), _resident(wdn.shape), _resident(fcw.shape), _resident(fcb.shape)],
        out_specs=[tok(d), tok(d), tok(2 * f), pl.BlockSpec((4, f), lambda b, i: (0, 0)),
                   pl.BlockSpec((None, 2, d), lambda b, i: (b, 0, 0))],
        out_shape=[jax.ShapeDtypeStruct((bl, s_len, d), _F32), jax.ShapeDtypeStruct((bl, s_len, d), _MXU_DT),
                   jax.ShapeDtypeStruct((bl, s_len, 2 * f), _MXU_DT), jax.ShapeDtypeStruct((4, f), _F32),
                   jax.ShapeDtypeStruct((bl, 2, d), _F32)],
        scratch_shapes=[pltpu.VMEM((tt + 8, fc), _F32), pltpu.VMEM((nch, 8, fc), _F32)],
        compiler_params=_cparams(("arbitrary", "arbitrary")), name="ffn_bwd",
    )(dz2, x1, hh, gc_all, mod3, wup, wdn, fcw, fcb)


def _mix_bwd(dx1, x, mix, proj, h, vbc, lru, mod3, win, lcw, lcb, wr_bd, wi_bd, b_r, b_i, lam, cw, cb, ng, nb, seg, wout, ln1g, chip_sums):
    bl, s_len, d = x.shape
    w = d // 2
    tt = min(_TT_MIX, s_len)
    ns = s_len // tt
    kc = cw.shape[0]

    def body(dx1_ref, x_ref, mix_ref, proj_ref, phalo_ref, h_ref, hhalo_ref, vbc_ref, lru_ref, mod_ref, win_ref, lcw_ref, lcb_ref,
             wr_ref, wi_ref, br_ref, bi_ref, lam_ref, cw_ref, cb_ref, ng_ref, nb_ref, seg_ref, wout_ref, g1_ref,
             gx_ref, dproj_ref, dmix_ref, xcg_ref, vecw_ref, dlcw_ref, dcw_ref, dln1_ref, dmod_ref,
             ext4, ext31, dext4, dext31, sh31, dsh31, car4, car31, gcar):
        s = ns - 1 - pl.program_id(1)
        first = s == 0

        @pl.when(pl.program_id(1) == 0)
        def _():
            car4[...] = jnp.zeros_like(car4)
            car31[...] = jnp.zeros_like(car31)
            gcar[...] = jnp.zeros_like(gcar)
            dmod_ref[...] = jnp.zeros_like(dmod_ref)

        @pl.when((pl.program_id(1) == 0) & (pl.program_id(0) == 0))
        def _():
            for ref in (vecw_ref, dlcw_ref, dcw_ref, dln1_ref):
                ref[...] = jnp.zeros_like(ref)

        xt, mixt = x_ref[...], mix_ref[...]
        sh1, sc1, gt1 = mod_ref[:, 0:d], mod_ref[:, d:2 * d], mod_ref[:, 2 * d:3 * d]

        n1, rstd1 = _layer_norm_stats(_ALPHA * xt + (1.0 + gt1) * mixt)
        dx1t = dx1_ref[...]
        dln1_ref[0:1, :] += _rowsum(dx1t * n1)
        dln1_ref[1:2, :] += _rowsum(dx1t)
        dz1 = _layer_norm_bwd(dx1t * g1_ref[...], n1, rstd1)
        dmod_ref[2:3, :] += _rowsum(dz1 * mixt)
        dmix = _mx((1.0 + gt1) * dz1)
        dmix_ref[...] = dmix
        dya = _dot_nt(dmix, wout_ref[0:w, :])
        dyb = _dot_nt(dmix, wout_ref[w:2 * w, :])

        xa, ga = proj_ref[:, 0:w], proj_ref[:, w:2 * w]
        vb, gb = proj_ref[:, 2 * w:3 * w], proj_ref[:, 3 * w:4 * w]

        sgb = jax.nn.sigmoid(gb)
        vbg = vb * sgb
        hv, hg = phalo_ref[:, 2 * w:3 * w], phalo_ref[:, 3 * w:4 * w]
        ext31[0:_HALO, :] = jnp.where(first, 0.0, hv * jax.nn.sigmoid(hg))
        ext31[_HALO:_HALO + tt, :] = vbg
        _make_shifted(ext31, sh31)
        vbc = vbc_ref[...]
        inv = 1.0 / (w // _N_HEADS)
        zc = vbc - _seg_sum(vbc, seg_ref[...]) * inv
        rstd = lax.rsqrt(_seg_sum(zc * zc, seg_ref[...]) * inv + _LN_EPS)
        n = zc * rstd
        pre = n * ng_ref[...] + nb_ref[...]
        sgp = jax.nn.sigmoid(pre)
        dpre = dyb * (sgp * (1.0 + pre * (1.0 - sgp)))
        vecw_ref[5:6, :] += _rowsum(dpre * n)
        vecw_ref[6:7, :] += _rowsum(dpre)
        dn = dpre * ng_ref[...]
        dvbc = rstd * (dn - _seg_sum(dn, seg_ref[...], 2) * inv - n * (_seg_sum(dn * n, seg_ref[...], 2) * inv))
        vecw_ref[4:5, :] += _rowsum(dvbc)
        dext31[0:tt, :] = dvbc
        dext31[tt:tt + _HALO, :] = car31[...]
        car31[...] = dvbc[0:_HALO, :]
        _make_shifted(dext31, dsh31)
        dvbg = jnp.zeros((tt, w), _F32)
        for k in range(kc):
            dvbg = dvbg + cw_ref[k:k + 1, :] * _tap(dext31, dsh31, kc - 1 - k, tt)
            dcw_ref[k:k + 1, :] += _rowsum(dvbc * _tap(ext31, sh31, _HALO - (kc - 1) + k, tt))
        dproj_ref[:, 2 * w:3 * w] = _mx(dvbg * sgb)
        dproj_ref[:, 3 * w:4 * w] = _mx(dvbg * vb * (sgb * (1.0 - sgb)))

        ext4[0:8, :] = jnp.where(first, 0.0, phalo_ref[_HALO - 8:_HALO, 0:w])
        ext4[8:8 + tt, :] = xa
        xc, r, i, a, mult = (lru_ref[:, k * w:(k + 1) * w] for k in range(5))
        xcg_ref[:, 0:w] = _mx(xc)
        sp = _softplus(-lam_ref[...])
        ht = h_ref[...]
        row = lax.broadcasted_iota(jnp.int32, (tt, w), 0)
        h_before = jnp.where(first, 0.0, hhalo_ref[7:8, :])
        hprev = jnp.where(row == 0, h_before, pltpu.roll(ht, 1, 0))
        gelu, dgelu = _gelu_and_grad(ga)
        dproj_ref[:, w:2 * w] = _mx(dya * ht * dgelu)
        dh = dya * gelu
        coef = jnp.where(row == tt - 1, 1.0, pltpu.roll(a, tt - 1, 0))
        big_g = _scan_rev(coef, dh, gcar[0:1, :])
        gcar[0:1, :] = a[0:1, :] * big_g[0:1, :]
        da = big_g * hprev
        ixc = i * xc
        dlog_a = da * a - (big_g * ixc) * (a * a / mult)
        di = big_g * mult * xc
        dxc = big_g * mult * i
        vecw_ref[3:4, :] += _rowsum(dlog_a * r) * (_LRU_C * jax.nn.sigmoid(-lam_ref[...]))
        dgr_f = dlog_a * (-_LRU_C * sp) * (r * (1.0 - r))
        dgi_f = di * (i * (1.0 - i))
        vecw_ref[1:2, :] += _rowsum(dgr_f)
        vecw_ref[2:3, :] += _rowsum(dgi_f)
        dgr, dgi = _mx(dgr_f), _mx(dgi_f)
        xcg_ref[:, w:2 * w] = dgr
        xcg_ref[:, 2 * w:3 * w] = dgi
        dxc = dxc + _dot_nt(dgr, wr_ref[...]) + _dot_nt(dgi, wi_ref[...])
        vecw_ref[0:1, :] += _rowsum(dxc)
        dext4[0:tt, :] = dxc
        dext4[tt:tt + 8, :] = car4[...]
        car4[...] = dxc[0:8, :]
        dxa = jnp.zeros((tt, w), _F32)
        for k in range(4):
            dxa = dxa + lcw_ref[k:k + 1, :] * dext4[pl.ds(3 - k, tt), :]
            dlcw_ref[k:k + 1, :] += _rowsum(dxc * ext4[pl.ds(5 + k, tt), :])
        dproj_ref[:, 0:w] = _mx(dxa)

        du1 = sum(_dot_nt(dproj_ref[:, k * w:(k + 1) * w], win_ref[k]) for k in range(4))
        gx_ref[...] = _ALPHA * dz1 + du1 * (1.0 + sc1)
        dmod_ref[0:1, :] += _rowsum(du1)
        dmod_ref[1:2, :] += _rowsum(du1 * xt)

    tok = lambda c: pl.BlockSpec((None, tt, c), lambda b, i: (b, ns - 1 - i, 0))
    halo = lambda rows, c: pl.BlockSpec(
        (None, rows, c), lambda b, i: (b, jnp.maximum((ns - 1 - i) * (tt // rows) - 1, 0), 0))
    accw = lambda r, c: pl.BlockSpec((r, c), lambda b, i: (0, 0))
    smalls = [lcw, lcb, wr_bd, wi_bd, b_r, b_i, lam, cw, cb, ng, nb, seg, wout, ln1g]
    nx = len(chip_sums)
    return pl.pallas_call(
        _fused_exchange(body, 11 + len(smalls), 9, 9, nx, nx, _chip_reduce_plan, (bl, ns)), grid=(bl, ns),
        in_specs=[tok(d), tok(d), tok(d), tok(4 * w), halo(_HALO, 4 * w), tok(w), halo(8, w), tok(w), tok(5 * w),
                  pl.BlockSpec((None, 1, 6 * d), lambda b, i: (b, 0, 0)), _resident(win.shape)]
        + [_resident(t.shape) for t in smalls] + [_HBM] * nx,
        out_specs=[tok(d), tok(4 * w), tok(d), tok(3 * w), accw(8, w), accw(4, w), accw(kc, w), accw(2, d),
                   pl.BlockSpec((None, 3, d), lambda b, i: (b, 0, 0))] + [_HBM] * nx,
        out_shape=[jax.ShapeDtypeStruct((bl, s_len, d), _F32), jax.ShapeDtypeStruct((bl, s_len, 4 * w), _MXU_DT),
                   jax.ShapeDtypeStruct((bl, s_len, d), _MXU_DT), jax.ShapeDtypeStruct((bl, s_len, 3 * w), _MXU_DT),
                   jax.ShapeDtypeStruct((8, w), _F32), jax.ShapeDtypeStruct((4, w), _F32),
                   jax.ShapeDtypeStruct((kc, w), _F32), jax.ShapeDtypeStruct((2, d), _F32),
                   jax.ShapeDtypeStruct((bl, 3, d), _F32)]
        + [jax.ShapeDtypeStruct((3,) + t.shape[1:], t.dtype) for t in chip_sums],
        scratch_shapes=[pltpu.VMEM((tt + 8, w), _F32), pltpu.VMEM((tt + _HALO, w), _F32),
                        pltpu.VMEM((tt + 8, w), _F32), pltpu.VMEM((tt + _HALO, w), _F32),
                        pltpu.VMEM((7, tt + _HALO - 8, w), _F32), pltpu.VMEM((7, tt + _HALO - 8, w), _F32),
                        pltpu.VMEM((8, w), _F32), pltpu.VMEM((_HALO, w), _F32), pltpu.VMEM((8, w), _F32)]
        + _chip_reduce_sems(nx),
        compiler_params=_cparams(("arbitrary", "arbitrary")), name="mix_bwd",
    )(dx1, x, mix, proj, proj, h, h, vbc, lru, mod3, win, *smalls, *chip_sums)


def _wgrad(a, b, ma, nbw, na, nb, a_off, b_off, name, exchange=None, zipped=False):
    t = a.shape[0]
    tk = min(_TK_WGRAD, t)
    nout = nb if zipped else na * nb
    grid = (nout, t // tk)
    a_block = (lambda j: j % na) if zipped else (lambda j: j // nb)
    b_block = (lambda j: j) if zipped else (lambda j: j % nb)

    def body(a_ref, b_ref, o_ref):
        @pl.when(pl.program_id(1) == 0)
        def _():
            o_ref[...] = jnp.zeros_like(o_ref)
        o_ref[...] += _dot_tn(a_ref[...], b_ref[...])

    xin, xshapes, plan, sems = exchange if exchange else ([], [], None, [])
    nx = len(xin)
    res = pl.pallas_call(
        _fused_exchange(body, 2, 1, 0, nx, len(xshapes), plan, grid) if exchange else body, grid=grid,
        in_specs=[pl.BlockSpec((tk, ma), lambda j, k: (k, a_block(j) + a_off)),
                  pl.BlockSpec((tk, nbw), lambda j, k: (k, b_block(j) + b_off))] + [_HBM] * nx,
        out_specs=[pl.BlockSpec((None, ma, nbw), lambda j, k: (j, 0, 0))] + [_HBM] * len(xshapes),
        out_shape=[jax.ShapeDtypeStruct((nout, ma, nbw), _F32)] + list(xshapes),
        scratch_shapes=list(sems),
        compiler_params=_cparams(("arbitrary", "arbitrary")), name=name,
    )(a, b, *xin)
    return res if exchange else res[0]


_DEV_DELTAS = tuple(dl for dl in itertools.product((0, 1), repeat=3) if any(dl))
_HBM = pl.BlockSpec(memory_space=pltpu.HBM)
_VMEM = pl.BlockSpec(memory_space=pltpu.VMEM)


def _pos():
    return lax.axis_index("x"), lax.axis_index("y"), lax.axis_index("c")


def _flip(v, delta):
    return 1 - v if delta else v


def _remote(src, dst, ssem, rsem, dev):
    return pltpu.make_async_remote_copy(src_ref=src, dst_ref=dst, send_sem=ssem, recv_sem=rsem,
                                        device_id=dev, device_id_type=_MESH)


def _rows(ref, idx, n):
    return ref.at[pl.ds(pl.multiple_of(idx * n, 8), n)]


def _ada_fwd(c8, w_ada_k, b_ada_k, shards):
    rows, d = c8.shape
    nk = w_ada_k.shape[1]
    n = len(shards)

    def body(*refs):
        c_ref, w_ref, b_ref = refs[:3]
        call_ref, mod_ref = refs[3 + n:5 + n]
        modloc, modrcv, s1, r1, s2, r2 = refs[5 + 2 * n:11 + 2 * n]
        gather_start, gather_finish = _gather_plan(refs[3:3 + n], refs[5 + n:5 + 2 * n], *refs[11 + 2 * n:14 + 2 * n],
                                                   fsem=refs[14 + 2 * n], frsem=refs[15 + 2 * n], bounce=refs[16 + 2 * n:])
        gather_start()
        xi, yi, ci = _pos()
        me, kme = 4 * xi + 2 * yi + ci, 2 * xi + yi
        call_ref[pl.ds(pl.multiple_of(me * rows, 8), rows), :] = c_ref[...]
        sends = []
        for p, (dx, dy, dc) in enumerate(_DEV_DELTAS):
            cp = _remote(c_ref, _rows(call_ref, me, rows), s1.at[p], r1.at[p], (_flip(xi, dx), _flip(yi, dy), _flip(ci, dc)))
            cp.start()
            sends.append(cp)
        for p, (dx, dy, dc) in enumerate(_DEV_DELTAS):
            src = 4 * _flip(xi, dx) + 2 * _flip(yi, dy) + _flip(ci, dc)
            _remote(c_ref, _rows(call_ref, src, rows), s1.at[p], r1.at[p], (xi, yi, ci)).wait_recv()
        for cp in sends:
            cp.wait_send()

        ca = call_ref[...]
        modloc[...] = _dot(_mx(ca * jax.nn.sigmoid(ca)), _mx(w_ref[...])) + b_ref[...]
        modrcv[kme] = modloc[pl.ds(pl.multiple_of(me * rows, 8), rows), :]
        sends = []
        for j, (dx, dy) in enumerate(_CHIP_DELTAS):
            tx, ty = _flip(xi, dx), _flip(yi, dy)
            cp = _remote(_rows(modloc, 4 * tx + 2 * ty + ci, rows), modrcv.at[kme], s2.at[j], r2.at[j], (tx, ty, ci))
            cp.start()
            sends.append(cp)
        for j, (dx, dy) in enumerate(_CHIP_DELTAS):
            ksrc = 2 * _flip(xi, dx) + _flip(yi, dy)
            _remote(_rows(modloc, me, rows), modrcv.at[ksrc], s2.at[j], r2.at[j], (xi, yi, ci)).wait_recv()
        for cp in sends:
            cp.wait_send()
        for j in range(4):
            mod_ref[:, j * nk:(j + 1) * nk] = modrcv[j]
        gather_finish()

    return pl.pallas_call(
        body, in_specs=[_VMEM, _VMEM, _VMEM] + [_HBM] * n, out_specs=[_VMEM, _VMEM] + [_HBM] * n,
        out_shape=[jax.ShapeDtypeStruct((8 * rows, d), _F32), jax.ShapeDtypeStruct((rows, 4 * nk), _F32)]
        + [jax.ShapeDtypeStruct((4,) + a.shape, a.dtype) for a in shards],
        scratch_shapes=[pltpu.VMEM((8 * rows, nk), _F32), pltpu.VMEM((4, rows, nk), _F32),
                        pltpu.SemaphoreType.DMA((7,)), pltpu.SemaphoreType.DMA((7,)),
                        pltpu.SemaphoreType.DMA((3,)), pltpu.SemaphoreType.DMA((3,))]
        + _gather_sems(n) + [pltpu.SemaphoreType.DMA((3, n)), pltpu.SemaphoreType.DMA((3, n))]
        + [pltpu.VMEM(a.shape, a.dtype) for a in shards],
        compiler_params=pltpu.CompilerParams(vmem_limit_bytes=_VMEM_LIMIT), name="ada_fwd",
    )(c8, w_ada_k, b_ada_k, *shards)


def _gather_sems(n):
    return [pltpu.SemaphoreType.DMA((3, n)), pltpu.SemaphoreType.DMA((3, n)), pltpu.SemaphoreType.DMA((n,))]


def _gather_plan(ins, outs, ssem, rsem, lsem, bounce=(), fsem=None, frsem=None):
    n = len(ins)
    xi, yi, ci = _pos()
    kme = 2 * xi + yi
    split = [fsem is not None and ins[a].shape[0] % 32 == 0 for a in range(n)]

    def half(ref, a, which):
        r2 = ins[a].shape[0] // 2
        return ref.at[pl.ds(pl.multiple_of(which * r2, 16), r2)]

    staged = [pltpu.make_async_copy(ins[a], bounce[a], lsem.at[a]) for a in range(len(bounce))]
    local = [pltpu.make_async_copy(bounce[a] if bounce else ins[a], outs[a].at[kme], lsem.at[a]) for a in range(n)]
    sends, recvs, forwards, handed = [], [], [], []
    for j, (dx, dy) in enumerate(_CHIP_DELTAS):
        tx, ty = _flip(xi, dx), _flip(yi, dy)
        for a in range(n):
            sems = (ssem.at[j, a], rsem.at[j, a])
            landing = outs[a].at[2 * tx + ty]
            if split[a]:
                sends.append(_remote(half(ins[a], a, ci), half(outs[a].at[kme], a, ci), *sems, (tx, ty, ci)))
                recvs.append(_remote(half(ins[a], a, ci), half(landing, a, ci), *sems, (xi, yi, ci)))
                fsems = (fsem.at[j, a], frsem.at[j, a])
                forwards.append(_remote(half(landing, a, ci), half(landing, a, ci), *fsems, (xi, yi, 1 - ci)))
                handed.append(_remote(half(ins[a], a, 1 - ci), half(landing, a, 1 - ci), *fsems, (xi, yi, ci)))
            else:
                sends.append(_remote(ins[a], outs[a].at[kme], *sems, (tx, ty, ci)))
                recvs.append(_remote(ins[a], landing, *sems, (xi, yi, ci)))
                forwards.append(None)

    def start():
        for cp in sends + staged:
            cp.start()
        for cp in staged:
            cp.wait()
        for cp in local:
            cp.start()

    def finish():
        for arrived, forward in zip(recvs, forwards):
            arrived.wait_recv()
            if forward is not None:
                forward.start()
        for cp in handed:
            cp.wait_recv()
        for cp in sends + [f for f in forwards if f is not None]:
            cp.wait_send()
        for cp in local:
            cp.wait()

    return start, finish


def _dev_gather_sems(n):
    return [pltpu.SemaphoreType.DMA((7, n)), pltpu.SemaphoreType.DMA((7, n)), pltpu.SemaphoreType.DMA((n,))]


def _dev_gather_plan(ins, outs, ssem, rsem, lsem):
    n = len(ins)
    xi, yi, ci = _pos()
    me = 4 * xi + 2 * yi + ci
    local = [pltpu.make_async_copy(ins[a], outs[a].at[me], lsem.at[a]) for a in range(n)]
    sends, recvs = [], []
    for p, (dx, dy, dc) in enumerate(_DEV_DELTAS):
        tx, ty, tc = _flip(xi, dx), _flip(yi, dy), _flip(ci, dc)
        for a in range(n):
            sends.append(_remote(ins[a], outs[a].at[me], ssem.at[p, a], rsem.at[p, a], (tx, ty, tc)))
            recvs.append(_remote(ins[a], outs[a].at[4 * tx + 2 * ty + tc], ssem.at[p, a], rsem.at[p, a], (xi, yi, ci)))

    def start():
        for cp in local + sends:
            cp.start()

    def finish():
        for cp in recvs:
            cp.wait_recv()
        for cp in sends:
            cp.wait_send()
        for cp in local:
            cp.wait()

    return start, finish


def _pair_sems(n):
    return [pltpu.SemaphoreType.DMA((n,)), pltpu.SemaphoreType.DMA((n,))]


def _pair_plan(ins, outs, ssem, rsem):
    xi, yi, ci = _pos()
    sends = []
    for a in range(len(ins)):
        r2 = ins[a].shape[1] // 2
        src = ins[a].at[:, pl.ds(pl.multiple_of((1 - ci) * r2, 8), r2), :]
        sends.append(_remote(src, outs[a], ssem.at[a], rsem.at[a], (xi, yi, 1 - ci)))

    def start():
        for cp in sends:
            cp.start()

    def finish():
        for cp in sends:
            cp.wait_recv()
        for cp in sends:
            cp.wait_send()

    return start, finish


def _chip_reduce_sems(n):
    return [pltpu.SemaphoreType.DMA((3, n)), pltpu.SemaphoreType.DMA((3, n))]


def _chip_reduce_plan(ins, outs, ssem, rsem):
    xi, yi, ci = _pos()
    sends = []
    for j, (dx, dy) in enumerate(_CHIP_DELTAS):
        tx, ty = _flip(xi, dx), _flip(yi, dy)
        sends += [_remote(ins[a].at[2 * tx + ty], outs[a].at[j], ssem.at[j, a], rsem.at[j, a], (tx, ty, ci))
                  for a in range(len(ins))]

    def start():
        for cp in sends:
            cp.start()

    def finish():
        for cp in sends:
            cp.wait_recv()
        for cp in sends:
            cp.wait_send()

    return start, finish


def _pair_exchange(gs, name):
    n = len(gs)

    def body(*refs):
        start, finish = _pair_plan(refs[:n], refs[n:2 * n], *refs[2 * n:])
        start()
        finish()

    return pl.pallas_call(
        body, in_specs=[_HBM] * n, out_specs=[_HBM] * n, out_shape=_pair_out_shapes(gs),
        scratch_shapes=_pair_sems(n), name=name,
    )(*gs)


def _pair_out_shapes(gs):
    return [jax.ShapeDtypeStruct((g.shape[0], g.shape[1] // 2, g.shape[2]), g.dtype) for g in gs]


def _row_tile(r):
    return max(t for t in range(8, min(r, 256) + 1, 8) if r % t == 0)


def _pair_add(g, r, cidx, name, wire_dtype=None, exchange=None):
    nk, r2, c = r.shape
    tr = _row_tile(r2)
    nt = r2 // tr
    xin, xshapes, plan, sems = exchange if exchange else ([], [], None, [])
    nx = len(xin)

    def body(c_ref, g_ref, r_ref, *o_refs):
        s = g_ref[...] + r_ref[...]
        for o_ref in o_refs:
            o_ref[...] = s.astype(o_ref.dtype)

    out_spec = pl.BlockSpec((None, tr, c), lambda k, i, cr: (k, i, 0))
    dtypes = [_F32] + ([wire_dtype] if wire_dtype else [])
    res = pl.pallas_call(
        _fused_exchange(body, 3, len(dtypes), 0, nx, len(xshapes), plan, (nk, nt)) if exchange else body,
        grid_spec=pltpu.PrefetchScalarGridSpec(
            num_scalar_prefetch=1, grid=(nk, nt),
            in_specs=[pl.BlockSpec((None, tr, c), lambda k, i, cr: (k, cr[0] * nt + i, 0)), out_spec] + [_HBM] * nx,
            out_specs=[out_spec] * len(dtypes) + [_HBM] * len(xshapes), scratch_shapes=list(sems)),
        out_shape=[jax.ShapeDtypeStruct(r.shape, dt) for dt in dtypes] + list(xshapes),
        compiler_params=_cparams(("arbitrary", "arbitrary")), name=name,
    )(cidx, g, r, *xin)
    return res if wire_dtype or exchange else res[0]


def _chip_exchange(ss):
    n = len(ss)

    def body(*refs):
        start, finish = _chip_reduce_plan(refs[:n], refs[n:2 * n], *refs[2 * n:])
        start()
        finish()

    return pl.pallas_call(
        body, in_specs=[_HBM] * n, out_specs=[_HBM] * n,
        out_shape=[jax.ShapeDtypeStruct((3,) + s.shape[1:], s.dtype) for s in ss],
        scratch_shapes=_chip_reduce_sems(n), name="grad_chip_exchange",
    )(*ss)


def _chip_add(s, r, kidx, name):
    _, r2, c = r.shape
    tr = _row_tile(r2)

    def body(k_ref, s_ref, r_ref, o_ref):
        o_ref[...] = ((s_ref[...] + r_ref[0].astype(_F32)) + r_ref[1].astype(_F32)) + r_ref[2].astype(_F32)

    return pl.pallas_call(
        body, grid_spec=pltpu.PrefetchScalarGridSpec(
            num_scalar_prefetch=1, grid=(r2 // tr,),
            in_specs=[pl.BlockSpec((None, tr, c), lambda i, kr: (kr[0], i, 0)),
                      pl.BlockSpec((3, tr, c), lambda i, kr: (0, i, 0))],
            out_specs=pl.BlockSpec((tr, c), lambda i, kr: (i, 0))),
        out_shape=jax.ShapeDtypeStruct((r2, c), _F32),
        compiler_params=_cparams(("arbitrary",)), name=name,
    )(kidx, s, r)


def _pair_swap_plan(ins, outs, ssem, rsem):
    xi, yi, ci = _pos()
    sends = [_remote(ins[a], outs[a], ssem.at[a], rsem.at[a], (xi, yi, 1 - ci)) for a in range(len(ins))]

    def start():
        for cp in sends:
            cp.start()

    def finish():
        for cp in sends:
            cp.wait_recv()
        for cp in sends:
            cp.wait_send()

    return start, finish


def _pair_swap(hs, name):
    n = len(hs)

    def body(*refs):
        start, finish = _pair_swap_plan(refs[:n], refs[n:2 * n], *refs[2 * n:])
        start()
        finish()

    return pl.pallas_call(
        body, in_specs=[_HBM] * n, out_specs=[_HBM] * n,
        out_shape=[jax.ShapeDtypeStruct(h.shape, h.dtype) for h in hs],
        scratch_shapes=[pltpu.SemaphoreType.DMA((n,)), pltpu.SemaphoreType.DMA((n,))], name=name,
    )(*hs)


def _small_sum(every):
    def body(all_ref, sum_ref):
        tot = all_ref[0]
        for dev in range(1, 8):
            tot = tot + all_ref[dev]
        sum_ref[...] = tot

    return pl.pallas_call(
        body, in_specs=[_VMEM], out_specs=_VMEM, out_shape=jax.ShapeDtypeStruct(every.shape[1:], _F32),
        compiler_params=pltpu.CompilerParams(vmem_limit_bytes=_VMEM_LIMIT), name="small_sum",
    )(every)


def _adamw(w, g, m, v):
    m = _ADAM_B1 * m + (1.0 - _ADAM_B1) * g
    v = _ADAM_B2 * v + (1.0 - _ADAM_B2) * (g * g)
    m_hat = m / (1.0 - _ADAM_B1 ** _ADAM_STEP)
    v_hat = v / (1.0 - _ADAM_B2 ** _ADAM_STEP)
    return -_ADAM_LR * (m_hat / (jnp.sqrt(v_hat) + _ADAM_EPS) + _ADAM_WD * w), m, v


def _adamw_big(w, g_mine, g_theirs, m, v, cidx, name):
    r, c = w.shape
    tr = _row_tile(r // 2)
    nt = r // 2 // tr

    def body(c_ref, w_ref, gm_ref, gt_ref, m_ref, v_ref, g_ref, d_ref, mo_ref, vo_ref):
        g = jnp.where(pl.program_id(0) // nt == c_ref[0], gm_ref[...], gt_ref[...])
        g_ref[...] = g
        d_ref[...], mo_ref[...], vo_ref[...] = _adamw(w_ref[...], g, m_ref[...], v_ref[...])

    spec = pl.BlockSpec((tr, c), lambda i, cr: (i, 0))
    half = pl.BlockSpec((tr, c), lambda i, cr: (i % nt, 0))
    return pl.pallas_call(
        body, grid_spec=pltpu.PrefetchScalarGridSpec(
            num_scalar_prefetch=1, grid=(2 * nt,), in_specs=[spec, half, half, spec, spec], out_specs=[spec] * 4),
        out_shape=[jax.ShapeDtypeStruct((r, c), _F32)] * 4,
        compiler_params=_cparams(("arbitrary",)), name=name,
    )(cidx, w, g_mine, g_theirs, m, v)


def _adamw_small(ws, gs, ms, vs):
    n = len(ws)
    summed = [i for i in range(n) if gs[i].shape != ws[i].shape]

    def body(*refs):
        w_r, g_r, m_r, v_r = (refs[i * n:(i + 1) * n] for i in range(4))
        outs = refs[4 * n:]
        for i in range(n):
            g = g_r[i][...]
            if i in summed:
                g = _rowsum(g)
                outs[3 * n + summed.index(i)][...] = g
            outs[i][...], outs[n + i][...], outs[2 * n + i][...] = _adamw(w_r[i][...], g, m_r[i][...], v_r[i][...])

    shapes = [jax.ShapeDtypeStruct(w.shape, _F32) for w in ws]
    res = pl.pallas_call(
        body, in_specs=[_VMEM] * (4 * n), out_specs=[_VMEM] * (3 * n + len(summed)),
        out_shape=shapes * 3 + [shapes[i] for i in summed],
        compiler_params=pltpu.CompilerParams(vmem_limit_bytes=_VMEM_LIMIT), name="adamw_small",
    )(*ws, *gs, *ms, *vs)
    gs = list(gs)
    for pos, i in enumerate(summed):
        gs[i] = res[3 * n + pos]
    return gs, res[:n], res[n:2 * n], res[2 * n:3 * n]


def _ada_bwd(c_all, dmod_k, w, m, v):
    d, nk = w.shape
    tn = 512 if nk % 512 == 0 else nk

    def body(c_ref, dm_ref, w_ref, m_ref, v_ref, g_ref, d_ref, mo_ref, vo_ref):
        ca = c_ref[...]
        g = _dot_tn(_mx(ca * jax.nn.sigmoid(ca)), _mx(dm_ref[...]))
        g_ref[...] = g
        d_ref[...], mo_ref[...], vo_ref[...] = _adamw(w_ref[...], g, m_ref[...], v_ref[...])

    col = pl.BlockSpec((d, tn), lambda j: (0, j))
    return pl.pallas_call(
        body, grid=(nk // tn,),
        in_specs=[pl.BlockSpec(c_all.shape, lambda j: (0, 0)), pl.BlockSpec((c_all.shape[0], tn), lambda j: (0, j)),
                  col, col, col],
        out_specs=[col] * 4, out_shape=[jax.ShapeDtypeStruct((d, nk), _F32)] * 4,
        compiler_params=_cparams(("arbitrary",)), name="ada_bwd",
    )(c_all, dmod_k, w, m, v)


def _block_diag(wh):
    hn, dh, _ = wh.shape
    eye = jnp.eye(hn, dtype=wh.dtype)
    return (eye[:, None, :, None] * wh[:, :, None, :]).reshape(hn * dh, hn * dh)


def _pack(pieces):
    out = []
    for p in pieces:
        flat = p.reshape(-1, 128)
        out.append(jnp.pad(flat, ((0, (-flat.shape[0]) % 8), (0, 0))))
    return jnp.concatenate(out, axis=0)


def _unpack(pack, shapes):
    out, off = [], 0
    for shp in shapes:
        rows = math.prod(shp) // 128
        out.append(pack[..., off:off + rows, :].reshape(pack.shape[:-2] + tuple(shp)))
        off += rows + (-rows) % 8
    return out


_WEIGHTS = ('w_ada', 'b_ada', 'w_in', 'lru_conv_w', 'lru_conv_b', 'lru_w_r', 'lru_b_r', 'lru_w_i', 'lru_b_i', 'lru_lambda',
            'conv_w', 'conv_b', 'conv_norm_g', 'conv_norm_b', 'w_out', 'ln1_g', 'ln1_b', 'ffn_w_up', 'ffn_conv_w',
            'ffn_conv_b', 'ffn_w_down', 'ln2_g', 'ln2_b')
_BIG = ('w_in', 'w_out', 'ffn_w_up', 'ffn_w_down')


def kernel(x, c, w_ada, b_ada, w_in, lru_conv_w, lru_conv_b, lru_w_r, lru_b_r, lru_w_i, lru_b_i, lru_lambda, conv_w, conv_b, conv_norm_g, conv_norm_b, w_out, ln1_g, ln1_b, ffn_w_up, ffn_conv_w, ffn_conv_b, ffn_w_down, ln2_g, ln2_b, loss_target, m_w_ada, m_b_ada, m_w_in, m_lru_conv_w, m_lru_conv_b, m_lru_w_r, m_lru_b_r, m_lru_w_i, m_lru_b_i, m_lru_lambda, m_conv_w, m_conv_b, m_conv_norm_g, m_conv_norm_b, m_w_out, m_ln1_g, m_ln1_b, m_ffn_w_up, m_ffn_conv_w, m_ffn_conv_b, m_ffn_w_down, m_ln2_g, m_ln2_b, v_w_ada, v_b_ada, v_w_in, v_lru_conv_w, v_lru_conv_b, v_lru_w_r, v_lru_b_r, v_lru_w_i, v_lru_b_i, v_lru_lambda, v_conv_w, v_conv_b, v_conv_norm_g, v_conv_norm_b, v_w_out, v_ln1_g, v_ln1_b, v_ffn_w_up, v_ffn_conv_w, v_ffn_conv_b, v_ffn_w_down, v_ln2_g, v_ln2_b):
    given = dict(locals())
    wt = {n: given[n] for n in _WEIGHTS}
    mo = {n: given["m_" + n] for n in _WEIGHTS}
    vo = {n: given["v_" + n] for n in _WEIGHTS}
    bl, s_len, d = x.shape
    wd = d // 2
    tokens = bl * s_len
    xi, yi, ci = _pos()
    kme = 2 * xi + yi
    kidx = jnp.reshape(kme, (1,)).astype(jnp.int32)
    cidx = jnp.reshape(ci, (1,)).astype(jnp.int32)

    nk = w_ada.shape[2]
    c8 = jnp.pad(c, ((0, 8 - bl), (0, 0)))
    c_all, mod8, win, wout_s, lcw_s, cw_s, fcw_s = _ada_fwd(
        c8, w_ada[0], lax.dynamic_slice(b_ada, (0, kme * nk), (1, nk)),
        [_mx(w_in[0]), _mx(w_out[0]), lru_conv_w[0], conv_w[0], ffn_conv_w[0]])
    mod3 = mod8[:bl].reshape(bl, 1, 6 * d)
    wout = wout_s.reshape(d, d)
    f = 4 * ffn_w_down.shape[1]
    unshard = lambda t: jnp.transpose(t, (1, 0, 2)).reshape(t.shape[1], -1)
    lcw, cw, fcw = unshard(lcw_s), unshard(cw_s), unshard(fcw_s)
    wr_bd, wi_bd = _mx(_block_diag(lru_w_r[0])), _mx(_block_diag(lru_w_i[0]))
    seg = _block_diag(jnp.ones((_N_HEADS, wd // _N_HEADS, wd // _N_HEADS), jnp.bfloat16))
    mixer_small = (lcw, lru_conv_b, wr_bd, wi_bd, lru_b_r, lru_b_i, lru_lambda, cw, conv_b, conv_norm_g, conv_norm_b, seg, wout, ln1_g)

    proj, h, mix, x1, u1, y, vbc, lru, wup, wdn_s = _mix_fwd(x, mod3, win, *mixer_small, ln1_b, [_mx(ffn_w_up[0]), _mx(ffn_w_down[0])])
    wdn = wdn_s.reshape(f, d)
    u2, hh, fact, gc_all, dz2, loss_acc, dln2, dgt2 = _ffn_fwd(x1, mod3, wup, fcw, ffn_conv_b, wdn, ln2_g, ln2_b, loss_target)
    dx1, dy2, dh, dfc, dmod2 = _ffn_bwd(dz2, x1, hh, gc_all, mod3, wup, wdn, fcw, ffn_conv_b)

    flat = lambda t: t.reshape(tokens, t.shape[-1])
    fc = wup.shape[2]
    g_up = _wgrad(flat(u2), flat(dh), d, fc, 1, 4, 0, 0, "wgrad_up")
    g_dn, r_up = _wgrad(flat(fact), flat(dy2), fc, d, f // fc, 1, 0, 0, "wgrad_down",
                        exchange=([g_up], _pair_out_shapes([g_up]), _pair_plan, _pair_sems(1)))
    g_dn = g_dn.reshape(4, f // 4, d)
    s_up, r_dn = _pair_add(g_up, r_up, cidx, "grad_pair_add_ffn_w_up",
                           exchange=([g_dn], _pair_out_shapes([g_dn]), _pair_plan, _pair_sems(1)))
    ffn_sum = [s_up, _pair_add(g_dn, r_dn, cidx, "grad_pair_add_ffn_w_down")]
    grad_x, dproj, dmix, xcg, vecw, dlcw, dcw, dln1, dmod1, *ffn_recv = _mix_bwd(
        dx1, x, mix, proj, h, vbc, lru, mod3, win, *mixer_small, ffn_sum)
    dh_ = wd // _N_HEADS
    per = 128 // dh_
    g_ri = _wgrad(flat(xcg), flat(xcg), 128, 128, wd // 128, 2 * wd // 128, 0, wd // 128, "wgrad_gates", zipped=True)
    on_diagonal = jnp.eye(per, dtype=_F32)[None, None, :, None, :, None]
    g_ri = jnp.sum(g_ri.reshape(2, wd // 128, per, dh_, per, dh_) * on_diagonal, axis=4).reshape(2, _N_HEADS, dh_, dh_)

    dmod = jnp.concatenate([dmod1.reshape(bl, 3 * d), dmod2.reshape(bl, 2 * d), dgt2.reshape(bl, d)], axis=1)
    pieces = [vecw, dlcw, dcw, jnp.concatenate([dln1, dln2], axis=0), dfc, g_ri, loss_acc[:, 0:128],
              jnp.pad(dmod, ((0, 8 - bl), (0, 0)))]
    shapes = [p.shape for p in pieces]
    pack = _pack(pieces)
    g_in, every = _wgrad(flat(u1), flat(dproj), d, wd, 1, 4, 0, 0, "wgrad_in", exchange=(
        [pack], [jax.ShapeDtypeStruct((8,) + pack.shape, _F32)], _dev_gather_plan, _dev_gather_sems(1)))
    wire_shape = lambda t: [jax.ShapeDtypeStruct((3,) + t.shape[1:], t.dtype)]
    ffn_half = [_chip_add(s, r, kidx, "grad_chip_add_" + n) for s, r, n in zip(ffn_sum, ffn_recv, _BIG[2:])]
    same = lambda ts: [jax.ShapeDtypeStruct(t.shape, t.dtype) for t in ts]
    r_in, = _pair_exchange([g_in], "grad_pair_exchange_w_in")
    s_in, wire_in, *ffn_theirs = _pair_add(g_in, r_in, cidx, "grad_pair_add_w_in", jnp.bfloat16,
                                           exchange=(ffn_half, same(ffn_half), _pair_swap_plan, _pair_sems(2)))
    g_out, recv_in = _wgrad(flat(y), flat(dmix), d, d, 1, 1, 0, 0, "wgrad_out", exchange=(
        [wire_in], wire_shape(wire_in), _chip_reduce_plan, _chip_reduce_sems(1)))
    g_out = g_out.reshape(4, d // 4, d)
    r_out, = _pair_exchange([g_out], "grad_pair_exchange_w_out")
    s_out, wire_out = _pair_add(g_out, r_out, cidx, "grad_pair_add_w_out", jnp.bfloat16)
    recv_out, = _chip_exchange([wire_out])
    mix_half = [_chip_add(s, r, kidx, "grad_chip_add_" + n) for s, r, n in zip([s_in, s_out], [recv_in, recv_out], _BIG)]
    half, other = mix_half + ffn_half, list(_pair_swap(mix_half, "grad_pair_swap")) + ffn_theirs
    grads, deltas, new_m, new_v = {}, {}, {}, {}
    for n, mine, theirs in zip(_BIG, half, other):
        g, dl, mm, vv = _adamw_big(wt[n][0], mine, theirs, mo[n][0], vo[n][0], cidx, "adamw_" + n)
        grads[n], deltas[n], new_m[n], new_v[n] = g[None], dl[None], mm[None], vv[None]

    vecw, dlcw, dcw, dln, dfc, g_ri, loss_sum, dmod_sum = _unpack(_small_sum(every), shapes)
    loss = 0.5 * loss_sum[0, 0] / d
    dmod_all = _unpack(every, shapes)[-1].reshape(64, 6 * d)

    g_ada, dl, mm, vv = _ada_bwd(c_all, lax.dynamic_slice(dmod_all, (0, kme * nk), (64, nk)), w_ada[0], m_w_ada[0], v_w_ada[0])
    grads['w_ada'], deltas['w_ada'], new_m['w_ada'], new_v['w_ada'] = g_ada[None], dl[None], mm[None], vv[None]

    shard = lambda t, width: lax.dynamic_slice(t, (0, kme * width), (t.shape[0], width))
    small = {
        'b_ada': dmod_sum, 'lru_conv_w': shard(dlcw, wd // 4), 'lru_conv_b': vecw[0:1], 'lru_w_r': g_ri[0], 'lru_b_r': vecw[1:2],
        'lru_w_i': g_ri[1], 'lru_b_i': vecw[2:3], 'lru_lambda': vecw[3:4], 'conv_w': shard(dcw, wd // 4), 'conv_b': vecw[4:5],
        'conv_norm_g': vecw[5:6], 'conv_norm_b': vecw[6:7], 'ln1_g': dln[0:1], 'ln1_b': dln[1:2],
        'ffn_conv_w': shard(dfc[0:3], f // 4), 'ffn_conv_b': dfc[3:4], 'ln2_g': dln[2:3], 'ln2_b': dln[3:4]}
    names = list(small)
    gs = [small[n] if n == 'b_ada' else small[n].reshape(wt[n].shape) for n in names]
    gs, dls, mms, vvs = _adamw_small([wt[n] for n in names], gs, [mo[n] for n in names], [vo[n] for n in names])
    for n, g, dl, mm, vv in zip(names, gs, dls, mms, vvs):
        grads[n], deltas[n], new_m[n], new_v[n] = g, dl, mm, vv

    return (loss, grad_x, *[grads[n] for n in _WEIGHTS], *[deltas[n] for n in _WEIGHTS],
            *[new_m[n] for n in _WEIGHTS], *[new_v[n] for n in _WEIGHTS])
```

```python
import functools
import itertools
import math

import jax
import jax.numpy as jnp
from jax import lax
from jax.experimental import pallas as pl
from jax.experimental.pallas import tpu as pltpu

_MXU_DT = jnp.bfloat16
_F32 = jnp.float32
_VMEM_LIMIT = 56 * 1024 * 1024
_TT_MIX = 256
_TT_MIX_FWD = 512
_TT_FFN = 256
_TK_WGRAD = 2048
_HALO = 32

_LRU_C = 8.0
_LN_EPS = 1e-5
_N_HEADS = 8
_DEPTH = 1
_ALPHA = (2 * _DEPTH) ** 0.25
_ADAM_LR, _ADAM_B1, _ADAM_B2, _ADAM_EPS, _ADAM_WD, _ADAM_STEP = 0.001, 0.9, 0.999, 1e-08, 0.01, 10

_MESH = pl.DeviceIdType.MESH
_CHIP_DELTAS = ((1, 0), (0, 1), (1, 1))


def _cparams(sem):
    return pltpu.CompilerParams(dimension_semantics=sem, vmem_limit_bytes=_VMEM_LIMIT)


def _resident(shape):
    nd = len(shape)
    return pl.BlockSpec(shape, lambda *_: (0,) * nd, pipeline_mode=pl.Buffered(1))


def _dot(a, b):
    return jnp.dot(a, b, preferred_element_type=_F32)


def _dot_nt(a, b):
    return lax.dot_general(a, b, (((1,), (1,)), ((), ())), preferred_element_type=_F32)


def _dot_tn(a, b):
    return lax.dot_general(a, b, (((0,), (0,)), ((), ())), preferred_element_type=_F32)


def _mx(v):
    return v.astype(_MXU_DT)


def _expm1(v):
    series = v * (1.0 + v * (1.0 / 2 + v * (1.0 / 6 + v * (1.0 / 24 + v * (1.0 / 120)))))
    return jnp.where(jnp.abs(v) < 0.0625, series, jnp.exp(v) - 1.0)


def _softplus(z):
    e = jnp.exp(-jnp.abs(z))
    u = 1.0 + e
    log1p = jnp.where(u == 1.0, e, jnp.log(u) * e / jnp.where(u == 1.0, 1.0, u - 1.0))
    return jnp.maximum(z, 0.0) + log1p


_GELU_C = math.sqrt(2.0 / math.pi)


def _gelu_and_grad(v):
    t = jnp.tanh(_GELU_C * (v + 0.044715 * v * v * v))
    val = 0.5 * v * (1.0 + t)
    grad = 0.5 * (1.0 + t) + 0.5 * v * (1.0 - t * t) * _GELU_C * (1.0 + 3 * 0.044715 * v * v)
    return val, grad


def _seg_sum(v, seg, passes=3):
    hi = v.astype(jnp.bfloat16)
    r1 = v - hi.astype(_F32)
    mid = r1.astype(jnp.bfloat16)
    out = _dot(hi, seg) + _dot(mid, seg)
    if passes == 3:
        out = out + _dot((r1 - mid.astype(_F32)).astype(jnp.bfloat16), seg)
    return out


_SCAN_BLOCK = 32


def _scan_fwd(a, u, h0):
    n = a.shape[0]
    blk = min(_SCAN_BLOCK, n)
    sub = lax.broadcasted_iota(jnp.int32, a.shape, 0) % blk
    h, d = u, 1
    while d < blk:
        keep = sub >= d
        h = a * jnp.where(keep, pltpu.roll(h, d, 0), 0.0) + h
        a = a * jnp.where(keep, pltpu.roll(a, d, 0), 1.0)
        d *= 2
    out, carry = [], h0
    for b in range(n // blk):
        rows = slice(b * blk, (b + 1) * blk)
        out.append(h[rows] + a[rows] * carry)
        carry = out[-1][blk - 1:blk, :]
    return jnp.concatenate(out, axis=0)


def _scan_rev(c, g, g_end):
    n = c.shape[0]
    blk = min(_SCAN_BLOCK, n)
    sub = lax.broadcasted_iota(jnp.int32, c.shape, 0) % blk
    d = 1
    while d < blk:
        keep = sub < blk - d
        g = c * jnp.where(keep, pltpu.roll(g, n - d, 0), 0.0) + g
        c = c * jnp.where(keep, pltpu.roll(c, n - d, 0), 1.0)
        d *= 2
    out, carry = [None] * (n // blk), g_end
    for b in reversed(range(n // blk)):
        rows = slice(b * blk, (b + 1) * blk)
        out[b] = g[rows] + c[rows] * carry
        carry = out[b][0:1, :]
    return jnp.concatenate(out, axis=0)


def _layer_norm_stats(z):
    mu = jnp.mean(z, axis=-1, keepdims=True)
    zc = z - mu
    var = jnp.mean(zc * zc, axis=-1, keepdims=True)
    rstd = lax.rsqrt(var + _LN_EPS)
    return zc * rstd, rstd


def _layer_norm_bwd(dn, n, rstd):
    return rstd * (dn - jnp.mean(dn, axis=-1, keepdims=True) - n * jnp.mean(dn * n, axis=-1, keepdims=True))


def _rowsum(v):
    return jnp.sum(v, axis=0, keepdims=True)


def _fused_exchange(body, n_in, n_out, n_scratch, n_xin, n_xout, plan, grid):
    def wrapped(*refs):
        o0 = n_in + n_xin
        s0 = o0 + n_out + n_xout
        start, finish = plan(refs[n_in:o0], refs[o0 + n_out:s0], *refs[s0 + n_scratch:])
        step = 0
        for axis, size in enumerate(grid):
            step = step * size + pl.program_id(axis)

        @pl.when(step == 0)
        def _():
            start()

        body(*refs[:n_in], *refs[o0:o0 + n_out], *refs[s0:s0 + n_scratch])

        @pl.when(step == math.prod(grid) - 1)
        def _():
            finish()

    return wrapped


def _lru_gates(xc, wr_ref, wi_ref, br_ref, bi_ref, lam_ref):
    xcb = _mx(xc)
    r = jax.nn.sigmoid(_dot(xcb, wr_ref[...]) + br_ref[...])
    i = jax.nn.sigmoid(_dot(xcb, wi_ref[...]) + bi_ref[...])
    sp = _softplus(-lam_ref[...])
    log_a = -_LRU_C * r * sp
    a = jnp.exp(log_a)
    mult = jnp.sqrt(-_expm1(2.0 * log_a))
    return r, i, sp, a, mult


def _conv_taps(ext_ref, w_ref, first, n_taps, tt):
    acc = w_ref[0:1, :] * ext_ref[pl.ds(first, tt), :]
    for k in range(1, n_taps):
        acc = acc + w_ref[k:k + 1, :] * ext_ref[pl.ds(first + k, tt), :]
    return acc


def _make_shifted(ext_ref, sh_ref):
    n = sh_ref.shape[1]
    for r in range(1, 8):
        sh_ref[r - 1] = ext_ref[pl.ds(r, n), :]


def _tap(ext_ref, sh_ref, off, tt):
    base = (off // 8) * 8
    if off % 8 == 0:
        return ext_ref[pl.ds(base, tt), :]
    return sh_ref[off % 8 - 1, pl.ds(base, tt), :]


def _conv_taps_shifted(ext_ref, sh_ref, w_ref, first, n_taps, tt):
    acc = w_ref[0:1, :] * _tap(ext_ref, sh_ref, first, tt)
    for k in range(1, n_taps):
        acc = acc + w_ref[k:k + 1, :] * _tap(ext_ref, sh_ref, first + k, tt)
    return acc


def _mix_fwd(x, mod3, win, lcw, lcb, wr_bd, wi_bd, b_r, b_i, lam, cw, cb, ng, nb, seg, wout, ln1g, ln1b, shards):
    bl, s_len, d = x.shape
    w = d // 2
    tt = min(_TT_MIX_FWD, s_len)
    ns = s_len // tt
    kc = cw.shape[0]

    def body(x_ref, mod_ref, win_ref, lcw_ref, lcb_ref, wr_ref, wi_ref, br_ref, bi_ref, lam_ref, cw_ref, cb_ref,
             ng_ref, nb_ref, seg_ref, wout_ref, g1_ref, b1_ref,
             proj_ref, h_ref, mix_ref, x1_ref, u1_ref, y_ref, vbc_ref, lru_ref, ext4, ext31, sh31, hcar):
        @pl.when(pl.program_id(1) == 0)
        def _():
            ext4[0:8, :] = jnp.zeros((8, w), _F32)
            ext31[0:_HALO, :] = jnp.zeros((_HALO, w), _F32)
            hcar[...] = jnp.zeros_like(hcar)

        xt = x_ref[...]
        sh1, sc1, gt1 = mod_ref[:, 0:d], mod_ref[:, d:2 * d], mod_ref[:, 2 * d:3 * d]
        u1 = _mx(xt * (1.0 + sc1) + sh1)
        u1_ref[...] = u1
        xa, ga, vb, gb = (_dot(u1, win_ref[k]) for k in range(4))
        proj_ref[:, 0:w] = xa
        proj_ref[:, w:2 * w] = ga
        proj_ref[:, 2 * w:3 * w] = vb
        proj_ref[:, 3 * w:4 * w] = gb

        ext4[8:8 + tt, :] = xa
        xc = lcb_ref[...] + _conv_taps(ext4, lcw_ref, 5, 4, tt)
        ext4[0:8, :] = xa[tt - 8:tt, :]
        r, i, sp, a, mult = _lru_gates(xc, wr_ref, wi_ref, br_ref, bi_ref, lam_ref)
        for k, val in enumerate((xc, r, i, a, mult)):
            lru_ref[:, k * w:(k + 1) * w] = val
        h = _scan_fwd(a, mult * (i * xc), hcar[0:1, :])
        hcar[0:1, :] = h[tt - 1:tt, :]
        h_ref[...] = h
        gelu, _ = _gelu_and_grad(ga)
        y_ref[:, 0:w] = _mx(gelu * h)

        vbg = vb * jax.nn.sigmoid(gb)
        ext31[_HALO:_HALO + tt, :] = vbg
        _make_shifted(ext31, sh31)
        vbc = cb_ref[...] + _conv_taps_shifted(ext31, sh31, cw_ref, _HALO - (kc - 1), kc, tt)
        vbc_ref[...] = vbc
        ext31[0:_HALO, :] = vbg[tt - _HALO:tt, :]
        inv = 1.0 / (w // _N_HEADS)
        zc = vbc - _seg_sum(vbc, seg_ref[...]) * inv
        n = zc * lax.rsqrt(_seg_sum(zc * zc, seg_ref[...]) * inv + _LN_EPS)
        pre = n * ng_ref[...] + nb_ref[...]
        y_ref[:, w:2 * w] = _mx(pre * jax.nn.sigmoid(pre))

        mix = _dot(y_ref[...], wout_ref[...])
        mix_ref[...] = mix
        n1, _ = _layer_norm_stats(_ALPHA * xt + (1.0 + gt1) * mix)
        x1_ref[...] = n1 * g1_ref[...] + b1_ref[...]

    tok = lambda c: pl.BlockSpec((None, tt, c), lambda b, s: (b, s, 0))
    smalls = [lcw, lcb, wr_bd, wi_bd, b_r, b_i, lam, cw, cb, ng, nb, seg, wout, ln1g, ln1b]
    nx = len(shards)
    return pl.pallas_call(
        _fused_exchange(body, 3 + len(smalls), 8, 4, nx, nx, _gather_plan, (bl, ns)), grid=(bl, ns),
        in_specs=[tok(d), pl.BlockSpec((None, 1, 6 * d), lambda b, s: (b, 0, 0)), _resident(win.shape)]
        + [_resident(t.shape) for t in smalls] + [_HBM] * nx,
        out_specs=[tok(4 * w), tok(w), tok(d), tok(d), tok(d), tok(d), tok(w), tok(5 * w)] + [_HBM] * nx,
        out_shape=[jax.ShapeDtypeStruct((bl, s_len, 4 * w), _F32), jax.ShapeDtypeStruct((bl, s_len, w), _F32),
                   jax.ShapeDtypeStruct((bl, s_len, d), _F32), jax.ShapeDtypeStruct((bl, s_len, d), _F32),
                   jax.ShapeDtypeStruct((bl, s_len, d), _MXU_DT), jax.ShapeDtypeStruct((bl, s_len, d), _MXU_DT),
                   jax.ShapeDtypeStruct((bl, s_len, w), _F32), jax.ShapeDtypeStruct((bl, s_len, 5 * w), _F32)]
        + [jax.ShapeDtypeStruct((4,) + t.shape, t.dtype) for t in shards],
        scratch_shapes=[pltpu.VMEM((tt + 8, w), _F32), pltpu.VMEM((tt + _HALO, w), _F32),
                        pltpu.VMEM((7, tt + _HALO - 8, w), _F32), pltpu.VMEM((8, w), _F32)] + _gather_sems(nx),
        compiler_params=_cparams(("arbitrary", "arbitrary")), name="mix_fwd",
    )(x, mod3, win, *smalls, *shards)


def _ffn_fwd(x1, mod3, wup, fcw, fcb, wdn, ln2g, ln2b, target):
    bl, s_len, d = x1.shape
    nch, _, fc = wup.shape
    nch //= 2
    f = nch * fc
    tt = min(_TT_FFN, s_len)
    ns = s_len // tt

    def body(x1_ref, mod_ref, wup_ref, fcw_ref, fcb_ref, wdn_ref, g2_ref, b2_ref, tgt_ref,
             u2_ref, hh_ref, f_ref, gc_ref, dz2_ref, loss_ref, dln2_ref, dgt2_ref, ext3):
        first_tile = pl.program_id(1) == 0

        @pl.when(first_tile)
        def _():
            ext3[:, 0:8, :] = jnp.zeros((nch, 8, fc), _F32)
            dgt2_ref[...] = jnp.zeros_like(dgt2_ref)

        @pl.when(first_tile & (pl.program_id(0) == 0))
        def _():
            loss_ref[...] = jnp.zeros_like(loss_ref)
            dln2_ref[...] = jnp.zeros_like(dln2_ref)

        x1t = x1_ref[...]
        sh2, sc2, gt2 = mod_ref[:, 3 * d:4 * d], mod_ref[:, 4 * d:5 * d], mod_ref[:, 5 * d:6 * d]
        u2 = _mx(x1t * (1.0 + sc2) + sh2)
        u2_ref[...] = u2
        y2 = jnp.zeros((tt, d), _F32)
        for j in range(nch):
            lanes = slice(j * fc, (j + 1) * fc)
            v = _dot(u2, wup_ref[j])
            g = _dot(u2, wup_ref[nch + j])
            hh_ref[:, lanes] = v.astype(hh_ref.dtype)
            hh_ref[:, f + j * fc:f + (j + 1) * fc] = g.astype(hh_ref.dtype)
            ext = ext3.at[j]
            ext[8:8 + tt, :] = g
            gc = fcb_ref[:, lanes] + sum(fcw_ref[k:k + 1, lanes] * ext[pl.ds(6 + k, tt), :] for k in range(3))
            gc_ref[:, lanes] = gc
            ext[0:8, :] = g[tt - 8:tt, :]
            fj = _mx(gc * jax.nn.sigmoid(gc) * v)
            f_ref[:, lanes] = fj
            y2 = y2 + _dot(fj, wdn_ref[lanes, :])

        n2, rstd = _layer_norm_stats(_ALPHA * x1t + (1.0 + gt2) * y2)
        err = n2 * g2_ref[...] + b2_ref[...] - tgt_ref[...]
        loss_ref[...] += jnp.sum(_rowsum(err * err), axis=1, keepdims=True)
        dout = err * (1.0 / d)
        dln2_ref[0:1, :] += _rowsum(dout * n2)
        dln2_ref[1:2, :] += _rowsum(dout)
        dz2 = _layer_norm_bwd(dout * g2_ref[...], n2, rstd)
        dz2_ref[...] = dz2
        dgt2_ref[...] += _rowsum(dz2 * y2)

    tok = lambda c: pl.BlockSpec((None, tt, c), lambda b, s: (b, s, 0))
    acc = lambda r: pl.BlockSpec((r, d), lambda b, s: (0, 0))
    smalls = [fcw, fcb, wdn, ln2g, ln2b]
    return pl.pallas_call(
        body, grid=(bl, ns),
        in_specs=[tok(d), pl.BlockSpec((None, 1, 6 * d), lambda b, s: (b, 0, 0)), _resident(wup.shape)]
        + [_resident(t.shape) for t in smalls] + [tok(d)],
        out_specs=[tok(d), tok(2 * f), tok(f), tok(f), tok(d), acc(1), acc(2), pl.BlockSpec((None, 1, d), lambda b, s: (b, 0, 0))],
        out_shape=[jax.ShapeDtypeStruct((bl, s_len, d), _MXU_DT), jax.ShapeDtypeStruct((bl, s_len, 2 * f), _F32),
                   jax.ShapeDtypeStruct((bl, s_len, f), _MXU_DT), jax.ShapeDtypeStruct((bl, s_len, f), _F32),
                   jax.ShapeDtypeStruct((bl, s_len, d), _F32), jax.ShapeDtypeStruct((1, d), _F32), jax.ShapeDtypeStruct((2, d), _F32),
                   jax.ShapeDtypeStruct((bl, 1, d), _F32)],
        scratch_shapes=[pltpu.VMEM((nch, tt + 8, fc), _F32)],
        compiler_params=_cparams(("arbitrary", "arbitrary")), name="ffn_fwd",
    )(x1, mod3, wup, *smalls, target)


def _ffn_bwd(dz2, x1, hh, gc_all, mod3, wup, wdn, fcw, fcb):
    bl, s_len, d = x1.shape
    nch, _, fc = wup.shape
    nch //= 2
    f = nch * fc
    tt = min(_TT_FFN, s_len)
    ns = s_len // tt

    def body(dz2_ref, x1_ref, hh_ref, gc_ref, mod_ref, wup_ref, wdn_ref, fcw_ref, fcb_ref,
             dx1_ref, dy2_ref, dh_ref, dfc_ref, dmod_ref, dext, dcar):
        @pl.when(pl.program_id(1) == 0)
        def _():
            dcar[...] = jnp.zeros_like(dcar)
            dmod_ref[...] = jnp.zeros_like(dmod_ref)

        @pl.when((pl.program_id(1) == 0) & (pl.program_id(0) == 0))
        def _():
            dfc_ref[...] = jnp.zeros_like(dfc_ref)

        sc2, gt2 = mod_ref[:, 4 * d:5 * d], mod_ref[:, 5 * d:6 * d]
        dz2t = dz2_ref[...]
        dy2 = _mx((1.0 + gt2) * dz2t)
        dy2_ref[...] = dy2
        du2 = jnp.zeros((tt, d), _F32)
        for j in range(nch):
            lanes = slice(j * fc, (j + 1) * fc)
            glanes = slice(f + j * fc, f + (j + 1) * fc)
            v = hh_ref[:, lanes].astype(_F32)
            g = hh_ref[:, glanes].astype(_F32)
            gc = gc_ref[:, lanes]
            sg = jax.nn.sigmoid(gc)
            df = _dot_nt(dy2, wdn_ref[lanes, :])
            dv = df * (gc * sg)
            dgc = df * v * (sg * (1.0 + gc * (1.0 - sg)))
            dfc_ref[3:4, lanes] += _rowsum(dgc)
            dext[0:tt, :] = dgc
            dext[tt:tt + 8, :] = dcar[j]
            dcar[j] = dgc[0:8, :]
            dg = jnp.zeros((tt, fc), _F32)
            for k in range(3):
                shifted = dext[pl.ds(2 - k, tt), :]
                dg = dg + fcw_ref[k:k + 1, lanes] * shifted
                dfc_ref[k:k + 1, lanes] += _rowsum(shifted * g)
            dvb, dgb = _mx(dv), _mx(dg)
            dh_ref[:, lanes] = dvb
            dh_ref[:, glanes] = dgb
            du2 = du2 + _dot_nt(dvb, wup_ref[j]) + _dot_nt(dgb, wup_ref[nch + j])

        dx1_ref[...] = _ALPHA * dz2t + du2 * (1.0 + sc2)
        dmod_ref[0:1, :] += _rowsum(du2)
        dmod_ref[1:2, :] += _rowsum(du2 * x1_ref[...])

    tok = lambda c: pl.BlockSpec((None, tt, c), lambda b, i: (b, ns - 1 - i, 0))
    return pl.pallas_call(
        body, grid=(bl, ns),
        in_specs=[tok(d), tok(d), tok(2 * f), tok(f), pl.BlockSpec((None, 1, 6 * d), lambda b, i: (b, 0, 0)),
                  _resident(wup.shape), _resident(wdn.shape), _resident(fcw.shape), _resident(fcb.shape)],
        out_specs=[tok(d), tok(d), tok(2 * f), pl.BlockSpec((4, f), lambda b, i: (0, 0)),
                   pl.BlockSpec((None, 2, d), lambda b, i: (b, 0, 0))],
        out_shape=[jax.ShapeDtypeStruct((bl, s_len, d), _F32), jax.ShapeDtypeStruct((bl, s_len, d), _MXU_DT),
                   jax.ShapeDtypeStruct((bl, s_len, 2 * f), _MXU_DT), jax.ShapeDtypeStruct((4, f), _F32),
                   jax.ShapeDtypeStruct((bl, 2, d), _F32)],
        scratch_shapes=[pltpu.VMEM((tt + 8, fc), _F32), pltpu.VMEM((nch, 8, fc), _F32)],
        compiler_params=_cparams(("arbitrary", "arbitrary")), name="ffn_bwd",
    )(dz2, x1, hh, gc_all, mod3, wup, wdn, fcw, fcb)


def _mix_bwd(dx1, x, mix, proj, h, vbc, lru, mod3, win, lcw, lcb, wr_bd, wi_bd, b_r, b_i, lam, cw, cb, ng, nb, seg, wout, ln1g, chip_sums):
    bl, s_len, d = x.shape
    w = d // 2
    tt = min(_TT_MIX, s_len)
    ns = s_len // tt
    kc = cw.shape[0]

    def body(dx1_ref, x_ref, mix_ref, proj_ref, phalo_ref, h_ref, hhalo_ref, vbc_ref, lru_ref, mod_ref, win_ref, lcw_ref, lcb_ref,
             wr_ref, wi_ref, br_ref, bi_ref, lam_ref, cw_ref, cb_ref, ng_ref, nb_ref, seg_ref, wout_ref, g1_ref,
             gx_ref, dproj_ref, dmix_ref, xcg_ref, vecw_ref, dlcw_ref, dcw_ref, dln1_ref, dmod_ref,
             ext4, ext31, dext4, dext31, sh31, dsh31, car4, car31, gcar):
        s = ns - 1 - pl.program_id(1)
        first = s == 0

        @pl.when(pl.program_id(1) == 0)
        def _():
            car4[...] = jnp.zeros_like(car4)
            car31[...] = jnp.zeros_like(car31)
            gcar[...] = jnp.zeros_like(gcar)
            dmod_ref[...] = jnp.zeros_like(dmod_ref)

        @pl.when((pl.program_id(1) == 0) & (pl.program_id(0) == 0))
        def _():
            for ref in (vecw_ref, dlcw_ref, dcw_ref, dln1_ref):
                ref[...] = jnp.zeros_like(ref)

        xt, mixt = x_ref[...], mix_ref[...]
        sh1, sc1, gt1 = mod_ref[:, 0:d], mod_ref[:, d:2 * d], mod_ref[:, 2 * d:3 * d]

        n1, rstd1 = _layer_norm_stats(_ALPHA * xt + (1.0 + gt1) * mixt)
        dx1t = dx1_ref[...]
        dln1_ref[0:1, :] += _rowsum(dx1t * n1)
        dln1_ref[1:2, :] += _rowsum(dx1t)
        dz1 = _layer_norm_bwd(dx1t * g1_ref[...], n1, rstd1)
        dmod_ref[2:3, :] += _rowsum(dz1 * mixt)
        dmix = _mx((1.0 + gt1) * dz1)
        dmix_ref[...] = dmix
        dya = _dot_nt(dmix, wout_ref[0:w, :])
        dyb = _dot_nt(dmix, wout_ref[w:2 * w, :])

        xa, ga = proj_ref[:, 0:w], proj_ref[:, w:2 * w]
        vb, gb = proj_ref[:, 2 * w:3 * w], proj_ref[:, 3 * w:4 * w]

        sgb = jax.nn.sigmoid(gb)
        vbg = vb * sgb
        hv, hg = phalo_ref[:, 2 * w:3 * w], phalo_ref[:, 3 * w:4 * w]
        ext31[0:_HALO, :] = jnp.where(first, 0.0, hv * jax.nn.sigmoid(hg))
        ext31[_HALO:_HALO + tt, :] = vbg
        _make_shifted(ext31, sh31)
        vbc = vbc_ref[...]
        inv = 1.0 / (w // _N_HEADS)
        zc = vbc - _seg_sum(vbc, seg_ref[...]) * inv
        rstd = lax.rsqrt(_seg_sum(zc * zc, seg_ref[...]) * inv + _LN_EPS)
        n = zc * rstd
        pre = n * ng_ref[...] + nb_ref[...]
        sgp = jax.nn.sigmoid(pre)
        dpre = dyb * (sgp * (1.0 + pre * (1.0 - sgp)))
        vecw_ref[5:6, :] += _rowsum(dpre * n)
        vecw_ref[6:7, :] += _rowsum(dpre)
        dn = dpre * ng_ref[...]
        dvbc = rstd * (dn - _seg_sum(dn, seg_ref[...], 2) * inv - n * (_seg_sum(dn * n, seg_ref[...], 2) * inv))
        vecw_ref[4:5, :] += _rowsum(dvbc)
        dext31[0:tt, :] = dvbc
        dext31[tt:tt + _HALO, :] = car31[...]
        car31[...] = dvbc[0:_HALO, :]
        _make_shifted(dext31, dsh31)
        dvbg = jnp.zeros((tt, w), _F32)
        for k in range(kc):
            dvbg = dvbg + cw_ref[k:k + 1, :] * _tap(dext31, dsh31, kc - 1 - k, tt)
            dcw_ref[k:k + 1, :] += _rowsum(dvbc * _tap(ext31, sh31, _HALO - (kc - 1) + k, tt))
        dproj_ref[:, 2 * w:3 * w] = _mx(dvbg * sgb)
        dproj_ref[:, 3 * w:4 * w] = _mx(dvbg * vb * (sgb * (1.0 - sgb)))

        ext4[0:8, :] = jnp.where(first, 0.0, phalo_ref[_HALO - 8:_HALO, 0:w])
        ext4[8:8 + tt, :] = xa
        xc, r, i, a, mult = (lru_ref[:, k * w:(k + 1) * w] for k in range(5))
        xcg_ref[:, 0:w] = _mx(xc)
        sp = _softplus(-lam_ref[...])
        ht = h_ref[...]
        row = lax.broadcasted_iota(jnp.int32, (tt, w), 0)
        h_before = jnp.where(first, 0.0, hhalo_ref[7:8, :])
        hprev = jnp.where(row == 0, h_before, pltpu.roll(ht, 1, 0))
        gelu, dgelu = _gelu_and_grad(ga)
        dproj_ref[:, w:2 * w] = _mx(dya * ht * dgelu)
        dh = dya * gelu
        coef = jnp.where(row == tt - 1, 1.0, pltpu.roll(a, tt - 1, 0))
        big_g = _scan_rev(coef, dh, gcar[0:1, :])
        gcar[0:1, :] = a[0:1, :] * big_g[0:1, :]
        da = big_g * hprev
        ixc = i * xc
        dlog_a = da * a - (big_g * ixc) * (a * a / mult)
        di = big_g * mult * xc
        dxc = big_g * mult * i
        vecw_ref[3:4, :] += _rowsum(dlog_a * r) * (_LRU_C * jax.nn.sigmoid(-lam_ref[...]))
        dgr_f = dlog_a * (-_LRU_C * sp) * (r * (1.0 - r))
        dgi_f = di * (i * (1.0 - i))
        vecw_ref[1:2, :] += _rowsum(dgr_f)
        vecw_ref[2:3, :] += _rowsum(dgi_f)
        dgr, dgi = _mx(dgr_f), _mx(dgi_f)
        xcg_ref[:, w:2 * w] = dgr
        xcg_ref[:, 2 * w:3 * w] = dgi
        dxc = dxc + _dot_nt(dgr, wr_ref[...]) + _dot_nt(dgi, wi_ref[...])
        vecw_ref[0:1, :] += _rowsum(dxc)
        dext4[0:tt, :] = dxc
        dext4[tt:tt + 8, :] = car4[...]
        car4[...] = dxc[0:8, :]
        dxa = jnp.zeros((tt, w), _F32)
        for k in range(4):
            dxa = dxa + lcw_ref[k:k + 1, :] * dext4[pl.ds(3 - k, tt), :]
            dlcw_ref[k:k + 1, :] += _rowsum(dxc * ext4[pl.ds(5 + k, tt), :])
        dproj_ref[:, 0:w] = _mx(dxa)

        du1 = sum(_dot_nt(dproj_ref[:, k * w:(k + 1) * w], win_ref[k]) for k in range(4))
        gx_ref[...] = _ALPHA * dz1 + du1 * (1.0 + sc1)
        dmod_ref[0:1, :] += _rowsum(du1)
        dmod_ref[1:2, :] += _rowsum(du1 * xt)

    tok = lambda c: pl.BlockSpec((None, tt, c), lambda b, i: (b, ns - 1 - i, 0))
    halo = lambda rows, c: pl.BlockSpec(
        (None, rows, c), lambda b, i: (b, jnp.maximum((ns - 1 - i) * (tt // rows) - 1, 0), 0))
    accw = lambda r, c: pl.BlockSpec((r, c), lambda b, i: (0, 0))
    smalls = [lcw, lcb, wr_bd, wi_bd, b_r, b_i, lam, cw, cb, ng, nb, seg, wout, ln1g]
    nx = len(chip_sums)
    return pl.pallas_call(
        _fused_exchange(body, 11 + len(smalls), 9, 9, nx, nx, _chip_reduce_plan, (bl, ns)), grid=(bl, ns),
        in_specs=[tok(d), tok(d), tok(d), tok(4 * w), halo(_HALO, 4 * w), tok(w), halo(8, w), tok(w), tok(5 * w),
                  pl.BlockSpec((None, 1, 6 * d), lambda b, i: (b, 0, 0)), _resident(win.shape)]
        + [_resident(t.shape) for t in smalls] + [_HBM] * nx,
        out_specs=[tok(d), tok(4 * w), tok(d), tok(3 * w), accw(8, w), accw(4, w), accw(kc, w), accw(2, d),
                   pl.BlockSpec((None, 3, d), lambda b, i: (b, 0, 0))] + [_HBM] * nx,
        out_shape=[jax.ShapeDtypeStruct((bl, s_len, d), _F32), jax.ShapeDtypeStruct((bl, s_len, 4 * w), _MXU_DT),
                   jax.ShapeDtypeStruct((bl, s_len, d), _MXU_DT), jax.ShapeDtypeStruct((bl, s_len, 3 * w), _MXU_DT),
                   jax.ShapeDtypeStruct((8, w), _F32), jax.ShapeDtypeStruct((4, w), _F32),
                   jax.ShapeDtypeStruct((kc, w), _F32), jax.ShapeDtypeStruct((2, d), _F32),
                   jax.ShapeDtypeStruct((bl, 3, d), _F32)]
        + [jax.ShapeDtypeStruct((3,) + t.shape[1:], t.dtype) for t in chip_sums],
        scratch_shapes=[pltpu.VMEM((tt + 8, w), _F32), pltpu.VMEM((tt + _HALO, w), _F32),
                        pltpu.VMEM((tt + 8, w), _F32), pltpu.VMEM((tt + _HALO, w), _F32),
                        pltpu.VMEM((7, tt + _HALO - 8, w), _F32), pltpu.VMEM((7, tt + _HALO - 8, w), _F32),
                        pltpu.VMEM((8, w), _F32), pltpu.VMEM((_HALO, w), _F32), pltpu.VMEM((8, w), _F32)]
        + _chip_reduce_sems(nx),
        compiler_params=_cparams(("arbitrary", "arbitrary")), name="mix_bwd",
    )(dx1, x, mix, proj, proj, h, h, vbc, lru, mod3, win, *smalls, *chip_sums)


def _wgrad(a, b, ma, nbw, na, nb, a_off, b_off, name, exchange=None):
    t = a.shape[0]
    tk = min(_TK_WGRAD, t)
    grid = (na * nb, t // tk)

    def body(a_ref, b_ref, o_ref):
        @pl.when(pl.program_id(1) == 0)
        def _():
            o_ref[...] = jnp.zeros_like(o_ref)
        o_ref[...] += _dot_tn(a_ref[...], b_ref[...])

    xin, xshapes, plan, sems = exchange if exchange else ([], [], None, [])
    nx = len(xin)
    res = pl.pallas_call(
        _fused_exchange(body, 2, 1, 0, nx, len(xshapes), plan, grid) if exchange else body, grid=grid,
        in_specs=[pl.BlockSpec((tk, ma), lambda j, k: (k, j // nb + a_off)),
                  pl.BlockSpec((tk, nbw), lambda j, k: (k, j % nb + b_off))] + [_HBM] * nx,
        out_specs=[pl.BlockSpec((None, ma, nbw), lambda j, k: (j, 0, 0))] + [_HBM] * len(xshapes),
        out_shape=[jax.ShapeDtypeStruct((na * nb, ma, nbw), _F32)] + list(xshapes),
        scratch_shapes=list(sems),
        compiler_params=_cparams(("arbitrary", "arbitrary")), name=name,
    )(a, b, *xin)
    return res if exchange else res[0]


_DEV_DELTAS = tuple(dl for dl in itertools.product((0, 1), repeat=3) if any(dl))
_HBM = pl.BlockSpec(memory_space=pltpu.HBM)
_VMEM = pl.BlockSpec(memory_space=pltpu.VMEM)


def _pos():
    return lax.axis_index("x"), lax.axis_index("y"), lax.axis_index("c")


def _flip(v, delta):
    return 1 - v if delta else v


def _remote(src, dst, ssem, rsem, dev):
    return pltpu.make_async_remote_copy(src_ref=src, dst_ref=dst, send_sem=ssem, recv_sem=rsem,
                                        device_id=dev, device_id_type=_MESH)


def _rows(ref, idx, n):
    return ref.at[pl.ds(pl.multiple_of(idx * n, 8), n)]


def _ada_fwd(c8, w_ada_k, b_ada_k, shards):
    rows, d = c8.shape
    nk = w_ada_k.shape[1]
    n = len(shards)

    def body(*refs):
        c_ref, w_ref, b_ref = refs[:3]
        call_ref, mod_ref = refs[3 + n:5 + n]
        modloc, modrcv, s1, r1, s2, r2 = refs[5 + 2 * n:11 + 2 * n]
        gather_start, gather_finish = _gather_plan(refs[3:3 + n], refs[5 + n:5 + 2 * n], *refs[11 + 2 * n:14 + 2 * n],
                                                   fsem=refs[14 + 2 * n], frsem=refs[15 + 2 * n], bounce=refs[16 + 2 * n:])
        gather_start()
        xi, yi, ci = _pos()
        me, kme = 4 * xi + 2 * yi + ci, 2 * xi + yi
        call_ref[pl.ds(pl.multiple_of(me * rows, 8), rows), :] = c_ref[...]
        sends = []
        for p, (dx, dy, dc) in enumerate(_DEV_DELTAS):
            cp = _remote(c_ref, _rows(call_ref, me, rows), s1.at[p], r1.at[p], (_flip(xi, dx), _flip(yi, dy), _flip(ci, dc)))
            cp.start()
            sends.append(cp)
        for p, (dx, dy, dc) in enumerate(_DEV_DELTAS):
            src = 4 * _flip(xi, dx) + 2 * _flip(yi, dy) + _flip(ci, dc)
            _remote(c_ref, _rows(call_ref, src, rows), s1.at[p], r1.at[p], (xi, yi, ci)).wait_recv()
        for cp in sends:
            cp.wait_send()

        ca = call_ref[...]
        modloc[...] = _dot(_mx(ca * jax.nn.sigmoid(ca)), _mx(w_ref[...])) + b_ref[...]
        modrcv[kme] = modloc[pl.ds(pl.multiple_of(me * rows, 8), rows), :]
        sends = []
        for j, (dx, dy) in enumerate(_CHIP_DELTAS):
            tx, ty = _flip(xi, dx), _flip(yi, dy)
            cp = _remote(_rows(modloc, 4 * tx + 2 * ty + ci, rows), modrcv.at[kme], s2.at[j], r2.at[j], (tx, ty, ci))
            cp.start()
            sends.append(cp)
        for j, (dx, dy) in enumerate(_CHIP_DELTAS):
            ksrc = 2 * _flip(xi, dx) + _flip(yi, dy)
            _remote(_rows(modloc, me, rows), modrcv.at[ksrc], s2.at[j], r2.at[j], (xi, yi, ci)).wait_recv()
        for cp in sends:
            cp.wait_send()
        for j in range(4):
            mod_ref[:, j * nk:(j + 1) * nk] = modrcv[j]
        gather_finish()

    return pl.pallas_call(
        body, in_specs=[_VMEM, _VMEM, _VMEM] + [_HBM] * n, out_specs=[_VMEM, _VMEM] + [_HBM] * n,
        out_shape=[jax.ShapeDtypeStruct((8 * rows, d), _F32), jax.ShapeDtypeStruct((rows, 4 * nk), _F32)]
        + [jax.ShapeDtypeStruct((4,) + a.shape, a.dtype) for a in shards],
        scratch_shapes=[pltpu.VMEM((8 * rows, nk), _F32), pltpu.VMEM((4, rows, nk), _F32),
                        pltpu.SemaphoreType.DMA((7,)), pltpu.SemaphoreType.DMA((7,)),
                        pltpu.SemaphoreType.DMA((3,)), pltpu.SemaphoreType.DMA((3,))]
        + _gather_sems(n) + [pltpu.SemaphoreType.DMA((3, n)), pltpu.SemaphoreType.DMA((3, n))]
        + [pltpu.VMEM(a.shape, a.dtype) for a in shards],
        compiler_params=pltpu.CompilerParams(vmem_limit_bytes=_VMEM_LIMIT), name="ada_fwd",
    )(c8, w_ada_k, b_ada_k, *shards)


def _gather_sems(n):
    return [pltpu.SemaphoreType.DMA((3, n)), pltpu.SemaphoreType.DMA((3, n)), pltpu.SemaphoreType.DMA((n,))]


def _gather_plan(ins, outs, ssem, rsem, lsem, bounce=(), fsem=None, frsem=None):
    n = len(ins)
    xi, yi, ci = _pos()
    kme = 2 * xi + yi
    split = [fsem is not None and ins[a].shape[0] % 32 == 0 for a in range(n)]

    def half(ref, a, which):
        r2 = ins[a].shape[0] // 2
        return ref.at[pl.ds(pl.multiple_of(which * r2, 16), r2)]

    staged = [pltpu.make_async_copy(ins[a], bounce[a], lsem.at[a]) for a in range(len(bounce))]
    local = [pltpu.make_async_copy(bounce[a] if bounce else ins[a], outs[a].at[kme], lsem.at[a]) for a in range(n)]
    sends, recvs, forwards, handed = [], [], [], []
    for j, (dx, dy) in enumerate(_CHIP_DELTAS):
        tx, ty = _flip(xi, dx), _flip(yi, dy)
        for a in range(n):
            sems = (ssem.at[j, a], rsem.at[j, a])
            landing = outs[a].at[2 * tx + ty]
            if split[a]:
                sends.append(_remote(half(ins[a], a, ci), half(outs[a].at[kme], a, ci), *sems, (tx, ty, ci)))
                recvs.append(_remote(half(ins[a], a, ci), half(landing, a, ci), *sems, (xi, yi, ci)))
                fsems = (fsem.at[j, a], frsem.at[j, a])
                forwards.append(_remote(half(landing, a, ci), half(landing, a, ci), *fsems, (xi, yi, 1 - ci)))
                handed.append(_remote(half(ins[a], a, 1 - ci), half(landing, a, 1 - ci), *fsems, (xi, yi, ci)))
            else:
                sends.append(_remote(ins[a], outs[a].at[kme], *sems, (tx, ty, ci)))
                recvs.append(_remote(ins[a], landing, *sems, (xi, yi, ci)))
                forwards.append(None)

    def start():
        for cp in sends + staged:
            cp.start()
        for cp in staged:
            cp.wait()
        for cp in local:
            cp.start()

    def finish():
        for arrived, forward in zip(recvs, forwards):
            arrived.wait_recv()
            if forward is not None:
                forward.start()
        for cp in handed:
            cp.wait_recv()
        for cp in sends + [f for f in forwards if f is not None]:
            cp.wait_send()
        for cp in local:
            cp.wait()

    return start, finish


def _dev_gather_sems(n):
    return [pltpu.SemaphoreType.DMA((7, n)), pltpu.SemaphoreType.DMA((7, n)), pltpu.SemaphoreType.DMA((n,))]


def _dev_gather_plan(ins, outs, ssem, rsem, lsem):
    n = len(ins)
    xi, yi, ci = _pos()
    me = 4 * xi + 2 * yi + ci
    local = [pltpu.make_async_copy(ins[a], outs[a].at[me], lsem.at[a]) for a in range(n)]
    sends, recvs = [], []
    for p, (dx, dy, dc) in enumerate(_DEV_DELTAS):
        tx, ty, tc = _flip(xi, dx), _flip(yi, dy), _flip(ci, dc)
        for a in range(n):
            sends.append(_remote(ins[a], outs[a].at[me], ssem.at[p, a], rsem.at[p, a], (tx, ty, tc)))
            recvs.append(_remote(ins[a], outs[a].at[4 * tx + 2 * ty + tc], ssem.at[p, a], rsem.at[p, a], (xi, yi, ci)))

    def start():
        for cp in local + sends:
            cp.start()

    def finish():
        for cp in recvs:
            cp.wait_recv()
        for cp in sends:
            cp.wait_send()
        for cp in local:
            cp.wait()

    return start, finish


def _pair_sems(n):
    return [pltpu.SemaphoreType.DMA((n,)), pltpu.SemaphoreType.DMA((n,))]


def _pair_plan(ins, outs, ssem, rsem):
    xi, yi, ci = _pos()
    sends = []
    for a in range(len(ins)):
        r2 = ins[a].shape[1] // 2
        src = ins[a].at[:, pl.ds(pl.multiple_of((1 - ci) * r2, 8), r2), :]
        sends.append(_remote(src, outs[a], ssem.at[a], rsem.at[a], (xi, yi, 1 - ci)))

    def start():
        for cp in sends:
            cp.start()

    def finish():
        for cp in sends:
            cp.wait_recv()
        for cp in sends:
            cp.wait_send()

    return start, finish


def _chip_reduce_sems(n):
    return [pltpu.SemaphoreType.DMA((3, n)), pltpu.SemaphoreType.DMA((3, n))]


def _chip_reduce_plan(ins, outs, ssem, rsem):
    xi, yi, ci = _pos()
    sends = []
    for j, (dx, dy) in enumerate(_CHIP_DELTAS):
        tx, ty = _flip(xi, dx), _flip(yi, dy)
        sends += [_remote(ins[a].at[2 * tx + ty], outs[a].at[j], ssem.at[j, a], rsem.at[j, a], (tx, ty, ci))
                  for a in range(len(ins))]

    def start():
        for cp in sends:
            cp.start()

    def finish():
        for cp in sends:
            cp.wait_recv()
        for cp in sends:
            cp.wait_send()

    return start, finish


def _pair_exchange(gs, name):
    n = len(gs)

    def body(*refs):
        start, finish = _pair_plan(refs[:n], refs[n:2 * n], *refs[2 * n:])
        start()
        finish()

    return pl.pallas_call(
        body, in_specs=[_HBM] * n, out_specs=[_HBM] * n, out_shape=_pair_out_shapes(gs),
        scratch_shapes=_pair_sems(n), name=name,
    )(*gs)


def _pair_out_shapes(gs):
    return [jax.ShapeDtypeStruct((g.shape[0], g.shape[1] // 2, g.shape[2]), g.dtype) for g in gs]


def _row_tile(r):
    return max(t for t in range(8, min(r, 256) + 1, 8) if r % t == 0)


def _pair_add(g, r, cidx, name, wire_dtype=None, exchange=None):
    nk, r2, c = r.shape
    tr = _row_tile(r2)
    nt = r2 // tr
    xin, xshapes, plan, sems = exchange if exchange else ([], [], None, [])
    nx = len(xin)

    def body(c_ref, g_ref, r_ref, *o_refs):
        s = g_ref[...] + r_ref[...]
        for o_ref in o_refs:
            o_ref[...] = s.astype(o_ref.dtype)

    out_spec = pl.BlockSpec((None, tr, c), lambda k, i, cr: (k, i, 0))
    dtypes = [_F32] + ([wire_dtype] if wire_dtype else [])
    res = pl.pallas_call(
        _fused_exchange(body, 3, len(dtypes), 0, nx, len(xshapes), plan, (nk, nt)) if exchange else body,
        grid_spec=pltpu.PrefetchScalarGridSpec(
            num_scalar_prefetch=1, grid=(nk, nt),
            in_specs=[pl.BlockSpec((None, tr, c), lambda k, i, cr: (k, cr[0] * nt + i, 0)), out_spec] + [_HBM] * nx,
            out_specs=[out_spec] * len(dtypes) + [_HBM] * len(xshapes), scratch_shapes=list(sems)),
        out_shape=[jax.ShapeDtypeStruct(r.shape, dt) for dt in dtypes] + list(xshapes),
        compiler_params=_cparams(("arbitrary", "arbitrary")), name=name,
    )(cidx, g, r, *xin)
    return res if wire_dtype or exchange else res[0]


def _chip_exchange(ss):
    n = len(ss)

    def body(*refs):
        start, finish = _chip_reduce_plan(refs[:n], refs[n:2 * n], *refs[2 * n:])
        start()
        finish()

    return pl.pallas_call(
        body, in_specs=[_HBM] * n, out_specs=[_HBM] * n,
        out_shape=[jax.ShapeDtypeStruct((3,) + s.shape[1:], s.dtype) for s in ss],
        scratch_shapes=_chip_reduce_sems(n), name="grad_chip_exchange",
    )(*ss)


def _chip_add(s, r, kidx, name):
    _, r2, c = r.shape
    tr = _row_tile(r2)

    def body(k_ref, s_ref, r_ref, o_ref):
        o_ref[...] = ((s_ref[...] + r_ref[0].astype(_F32)) + r_ref[1].astype(_F32)) + r_ref[2].astype(_F32)

    return pl.pallas_call(
        body, grid_spec=pltpu.PrefetchScalarGridSpec(
            num_scalar_prefetch=1, grid=(r2 // tr,),
            in_specs=[pl.BlockSpec((None, tr, c), lambda i, kr: (kr[0], i, 0)),
                      pl.BlockSpec((3, tr, c), lambda i, kr: (0, i, 0))],
            out_specs=pl.BlockSpec((tr, c), lambda i, kr: (i, 0))),
        out_shape=jax.ShapeDtypeStruct((r2, c), _F32),
        compiler_params=_cparams(("arbitrary",)), name=name,
    )(kidx, s, r)


def _pair_swap_plan(ins, outs, ssem, rsem):
    xi, yi, ci = _pos()
    sends = [_remote(ins[a], outs[a], ssem.at[a], rsem.at[a], (xi, yi, 1 - ci)) for a in range(len(ins))]

    def start():
        for cp in sends:
            cp.start()

    def finish():
        for cp in sends:
            cp.wait_recv()
        for cp in sends:
            cp.wait_send()

    return start, finish


def _pair_swap(hs, name):
    n = len(hs)

    def body(*refs):
        start, finish = _pair_swap_plan(refs[:n], refs[n:2 * n], *refs[2 * n:])
        start()
        finish()

    return pl.pallas_call(
        body, in_specs=[_HBM] * n, out_specs=[_HBM] * n,
        out_shape=[jax.ShapeDtypeStruct(h.shape, h.dtype) for h in hs],
        scratch_shapes=[pltpu.SemaphoreType.DMA((n,)), pltpu.SemaphoreType.DMA((n,))], name=name,
    )(*hs)


def _small_sum(every):
    def body(all_ref, sum_ref):
        tot = all_ref[0]
        for dev in range(1, 8):
            tot = tot + all_ref[dev]
        sum_ref[...] = tot

    return pl.pallas_call(
        body, in_specs=[_VMEM], out_specs=_VMEM, out_shape=jax.ShapeDtypeStruct(every.shape[1:], _F32),
        compiler_params=pltpu.CompilerParams(vmem_limit_bytes=_VMEM_LIMIT), name="small_sum",
    )(every)


def _adamw(w, g, m, v):
    m = _ADAM_B1 * m + (1.0 - _ADAM_B1) * g
    v = _ADAM_B2 * v + (1.0 - _ADAM_B2) * (g * g)
    m_hat = m / (1.0 - _ADAM_B1 ** _ADAM_STEP)
    v_hat = v / (1.0 - _ADAM_B2 ** _ADAM_STEP)
    return -_ADAM_LR * (m_hat / (jnp.sqrt(v_hat) + _ADAM_EPS) + _ADAM_WD * w), m, v


def _adamw_big(w, g_mine, g_theirs, m, v, cidx, name):
    r, c = w.shape
    tr = _row_tile(r // 2)
    nt = r // 2 // tr

    def body(c_ref, w_ref, gm_ref, gt_ref, m_ref, v_ref, g_ref, d_ref, mo_ref, vo_ref):
        g = jnp.where(pl.program_id(0) // nt == c_ref[0], gm_ref[...], gt_ref[...])
        g_ref[...] = g
        d_ref[...], mo_ref[...], vo_ref[...] = _adamw(w_ref[...], g, m_ref[...], v_ref[...])

    spec = pl.BlockSpec((tr, c), lambda i, cr: (i, 0))
    half = pl.BlockSpec((tr, c), lambda i, cr: (i % nt, 0))
    return pl.pallas_call(
        body, grid_spec=pltpu.PrefetchScalarGridSpec(
            num_scalar_prefetch=1, grid=(2 * nt,), in_specs=[spec, half, half, spec, spec], out_specs=[spec] * 4),
        out_shape=[jax.ShapeDtypeStruct((r, c), _F32)] * 4,
        compiler_params=_cparams(("arbitrary",)), name=name,
    )(cidx, w, g_mine, g_theirs, m, v)


def _adamw_small(ws, gs, ms, vs):
    n = len(ws)
    summed = [i for i in range(n) if gs[i].shape != ws[i].shape]

    def body(*refs):
        w_r, g_r, m_r, v_r = (refs[i * n:(i + 1) * n] for i in range(4))
        outs = refs[4 * n:]
        for i in range(n):
            g = g_r[i][...]
            if i in summed:
                g = _rowsum(g)
                outs[3 * n + summed.index(i)][...] = g
            outs[i][...], outs[n + i][...], outs[2 * n + i][...] = _adamw(w_r[i][...], g, m_r[i][...], v_r[i][...])

    shapes = [jax.ShapeDtypeStruct(w.shape, _F32) for w in ws]
    res = pl.pallas_call(
        body, in_specs=[_VMEM] * (4 * n), out_specs=[_VMEM] * (3 * n + len(summed)),
        out_shape=shapes * 3 + [shapes[i] for i in summed],
        compiler_params=pltpu.CompilerParams(vmem_limit_bytes=_VMEM_LIMIT), name="adamw_small",
    )(*ws, *gs, *ms, *vs)
    gs = list(gs)
    for pos, i in enumerate(summed):
        gs[i] = res[3 * n + pos]
    return gs, res[:n], res[n:2 * n], res[2 * n:3 * n]


def _ada_bwd(c_all, dmod_k, w, m, v):
    d, nk = w.shape
    tn = 512 if nk % 512 == 0 else nk

    def body(c_ref, dm_ref, w_ref, m_ref, v_ref, g_ref, d_ref, mo_ref, vo_ref):
        ca = c_ref[...]
        g = _dot_tn(_mx(ca * jax.nn.sigmoid(ca)), _mx(dm_ref[...]))
        g_ref[...] = g
        d_ref[...], mo_ref[...], vo_ref[...] = _adamw(w_ref[...], g, m_ref[...], v_ref[...])

    col = pl.BlockSpec((d, tn), lambda j: (0, j))
    return pl.pallas_call(
        body, grid=(nk // tn,),
        in_specs=[pl.BlockSpec(c_all.shape, lambda j: (0, 0)), pl.BlockSpec((c_all.shape[0], tn), lambda j: (0, j)),
                  col, col, col],
        out_specs=[col] * 4, out_shape=[jax.ShapeDtypeStruct((d, nk), _F32)] * 4,
        compiler_params=_cparams(("arbitrary",)), name="ada_bwd",
    )(c_all, dmod_k, w, m, v)


def _block_diag(wh):
    hn, dh, _ = wh.shape
    eye = jnp.eye(hn, dtype=wh.dtype)
    return (eye[:, None, :, None] * wh[:, :, None, :]).reshape(hn * dh, hn * dh)


def _pack(pieces):
    out = []
    for p in pieces:
        flat = p.reshape(-1, 128)
        out.append(jnp.pad(flat, ((0, (-flat.shape[0]) % 8), (0, 0))))
    return jnp.concatenate(out, axis=0)


def _unpack(pack, shapes):
    out, off = [], 0
    for shp in shapes:
        rows = math.prod(shp) // 128
        out.append(pack[..., off:off + rows, :].reshape(pack.shape[:-2] + tuple(shp)))
        off += rows + (-rows) % 8
    return out


_WEIGHTS = ('w_ada', 'b_ada', 'w_in', 'lru_conv_w', 'lru_conv_b', 'lru_w_r', 'lru_b_r', 'lru_w_i', 'lru_b_i', 'lru_lambda',
            'conv_w', 'conv_b', 'conv_norm_g', 'conv_norm_b', 'w_out', 'ln1_g', 'ln1_b', 'ffn_w_up', 'ffn_conv_w',
            'ffn_conv_b', 'ffn_w_down', 'ln2_g', 'ln2_b')
_BIG = ('w_in', 'w_out', 'ffn_w_up', 'ffn_w_down')


def kernel(x, c, w_ada, b_ada, w_in, lru_conv_w, lru_conv_b, lru_w_r, lru_b_r, lru_w_i, lru_b_i, lru_lambda, conv_w, conv_b, conv_norm_g, conv_norm_b, w_out, ln1_g, ln1_b, ffn_w_up, ffn_conv_w, ffn_conv_b, ffn_w_down, ln2_g, ln2_b, loss_target, m_w_ada, m_b_ada, m_w_in, m_lru_conv_w, m_lru_conv_b, m_lru_w_r, m_lru_b_r, m_lru_w_i, m_lru_b_i, m_lru_lambda, m_conv_w, m_conv_b, m_conv_norm_g, m_conv_norm_b, m_w_out, m_ln1_g, m_ln1_b, m_ffn_w_up, m_ffn_conv_w, m_ffn_conv_b, m_ffn_w_down, m_ln2_g, m_ln2_b, v_w_ada, v_b_ada, v_w_in, v_lru_conv_w, v_lru_conv_b, v_lru_w_r, v_lru_b_r, v_lru_w_i, v_lru_b_i, v_lru_lambda, v_conv_w, v_conv_b, v_conv_norm_g, v_conv_norm_b, v_w_out, v_ln1_g, v_ln1_b, v_ffn_w_up, v_ffn_conv_w, v_ffn_conv_b, v_ffn_w_down, v_ln2_g, v_ln2_b):
    given = dict(locals())
    wt = {n: given[n] for n in _WEIGHTS}
    mo = {n: given["m_" + n] for n in _WEIGHTS}
    vo = {n: given["v_" + n] for n in _WEIGHTS}
    bl, s_len, d = x.shape
    wd = d // 2
    tokens = bl * s_len
    xi, yi, ci = _pos()
    kme = 2 * xi + yi
    kidx = jnp.reshape(kme, (1,)).astype(jnp.int32)
    cidx = jnp.reshape(ci, (1,)).astype(jnp.int32)

    nk = w_ada.shape[2]
    c8 = jnp.pad(c, ((0, 8 - bl), (0, 0)))
    c_all, mod8, win, wout_s, lcw_s, cw_s, fcw_s = _ada_fwd(
        c8, w_ada[0], lax.dynamic_slice(b_ada, (0, kme * nk), (1, nk)),
        [_mx(w_in[0]), _mx(w_out[0]), lru_conv_w[0], conv_w[0], ffn_conv_w[0]])
    mod3 = mod8[:bl].reshape(bl, 1, 6 * d)
    wout = wout_s.reshape(d, d)
    f = 4 * ffn_w_down.shape[1]
    unshard = lambda t: jnp.transpose(t, (1, 0, 2)).reshape(t.shape[1], -1)
    lcw, cw, fcw = unshard(lcw_s), unshard(cw_s), unshard(fcw_s)
    wr_bd, wi_bd = _mx(_block_diag(lru_w_r[0])), _mx(_block_diag(lru_w_i[0]))
    seg = _block_diag(jnp.ones((_N_HEADS, wd // _N_HEADS, wd // _N_HEADS), jnp.bfloat16))
    mixer_small = (lcw, lru_conv_b, wr_bd, wi_bd, lru_b_r, lru_b_i, lru_lambda, cw, conv_b, conv_norm_g, conv_norm_b, seg, wout, ln1_g)

    proj, h, mix, x1, u1, y, vbc, lru, wup, wdn_s = _mix_fwd(x, mod3, win, *mixer_small, ln1_b, [_mx(ffn_w_up[0]), _mx(ffn_w_down[0])])
    wdn = wdn_s.reshape(f, d)
    u2, hh, fact, gc_all, dz2, loss_acc, dln2, dgt2 = _ffn_fwd(x1, mod3, wup, fcw, ffn_conv_b, wdn, ln2_g, ln2_b, loss_target)
    dx1, dy2, dh, dfc, dmod2 = _ffn_bwd(dz2, x1, hh, gc_all, mod3, wup, wdn, fcw, ffn_conv_b)

    flat = lambda t: t.reshape(tokens, t.shape[-1])
    fc = wup.shape[2]
    g_up = _wgrad(flat(u2), flat(dh), d, fc, 1, 4, 0, 0, "wgrad_up")
    g_dn, r_up = _wgrad(flat(fact), flat(dy2), fc, d, f // fc, 1, 0, 0, "wgrad_down",
                        exchange=([g_up], _pair_out_shapes([g_up]), _pair_plan, _pair_sems(1)))
    g_dn = g_dn.reshape(4, f // 4, d)
    s_up, r_dn = _pair_add(g_up, r_up, cidx, "grad_pair_add_ffn_w_up",
                           exchange=([g_dn], _pair_out_shapes([g_dn]), _pair_plan, _pair_sems(1)))
    ffn_sum = [s_up, _pair_add(g_dn, r_dn, cidx, "grad_pair_add_ffn_w_down")]
    grad_x, dproj, dmix, xcg, vecw, dlcw, dcw, dln1, dmod1, *ffn_recv = _mix_bwd(
        dx1, x, mix, proj, h, vbc, lru, mod3, win, *mixer_small, ffn_sum)
    g_ri = _wgrad(flat(xcg), flat(xcg), wd, wd, 1, 2, 0, 1, "wgrad_gates")
    dh_ = wd // _N_HEADS
    on_diagonal = jnp.eye(_N_HEADS, dtype=_F32)[None, :, None, :, None]
    g_ri = jnp.sum(g_ri.reshape(2, _N_HEADS, dh_, _N_HEADS, dh_) * on_diagonal, axis=3)

    dmod = jnp.concatenate([dmod1.reshape(bl, 3 * d), dmod2.reshape(bl, 2 * d), dgt2.reshape(bl, d)], axis=1)
    pieces = [vecw, dlcw, dcw, jnp.concatenate([dln1, dln2], axis=0), dfc, g_ri, loss_acc[:, 0:128],
              jnp.pad(dmod, ((0, 8 - bl), (0, 0)))]
    shapes = [p.shape for p in pieces]
    pack = _pack(pieces)
    g_in, every = _wgrad(flat(u1), flat(dproj), d, wd, 1, 4, 0, 0, "wgrad_in", exchange=(
        [pack], [jax.ShapeDtypeStruct((8,) + pack.shape, _F32)], _dev_gather_plan, _dev_gather_sems(1)))
    wire_shape = lambda t: [jax.ShapeDtypeStruct((3,) + t.shape[1:], t.dtype)]
    ffn_half = [_chip_add(s, r, kidx, "grad_chip_add_" + n) for s, r, n in zip(ffn_sum, ffn_recv, _BIG[2:])]
    same = lambda ts: [jax.ShapeDtypeStruct(t.shape, t.dtype) for t in ts]
    r_in, = _pair_exchange([g_in], "grad_pair_exchange_w_in")
    s_in, wire_in, *ffn_theirs = _pair_add(g_in, r_in, cidx, "grad_pair_add_w_in", jnp.bfloat16,
                                           exchange=(ffn_half, same(ffn_half), _pair_swap_plan, _pair_sems(2)))
    g_out, recv_in = _wgrad(flat(y), flat(dmix), d, d, 1, 1, 0, 0, "wgrad_out", exchange=(
        [wire_in], wire_shape(wire_in), _chip_reduce_plan, _chip_reduce_sems(1)))
    g_out = g_out.reshape(4, d // 4, d)
    r_out, = _pair_exchange([g_out], "grad_pair_exchange_w_out")
    s_out, wire_out = _pair_add(g_out, r_out, cidx, "grad_pair_add_w_out", jnp.bfloat16)
    recv_out, = _chip_exchange([wire_out])
    mix_half = [_chip_add(s, r, kidx, "grad_chip_add_" + n) for s, r, n in zip([s_in, s_out], [recv_in, recv_out], _BIG)]
    half, other = mix_half + ffn_half, list(_pair_swap(mix_half, "grad_pair_swap")) + ffn_theirs
    grads, deltas, new_m, new_v = {}, {}, {}, {}
    for n, mine, theirs in zip(_BIG, half, other):
        g, dl, mm, vv = _adamw_big(wt[n][0], mine, theirs, mo[n][0], vo[n][0], cidx, "adamw_" + n)
        grads[n], deltas[n], new_m[n], new_v[n] = g[None], dl[None], mm[None], vv[None]

    vecw, dlcw, dcw, dln, dfc, g_ri, loss_sum, dmod_sum = _unpack(_small_sum(every), shapes)
    loss = 0.5 * loss_sum[0, 0] / d
    dmod_all = _unpack(every, shapes)[-1].reshape(64, 6 * d)

    g_ada, dl, mm, vv = _ada_bwd(c_all, lax.dynamic_slice(dmod_all, (0, kme * nk), (64, nk)), w_ada[0], m_w_ada[0], v_w_ada[0])
    grads['w_ada'], deltas['w_ada'], new_m['w_ada'], new_v['w_ada'] = g_ada[None], dl[None], mm[None], vv[None]

    shard = lambda t, width: lax.dynamic_slice(t, (0, kme * width), (t.shape[0], width))
    small = {
        'b_ada': dmod_sum, 'lru_conv_w': shard(dlcw, wd // 4), 'lru_conv_b': vecw[0:1], 'lru_w_r': g_ri[0], 'lru_b_r': vecw[1:2],
        'lru_w_i': g_ri[1], 'lru_b_i': vecw[2:3], 'lru_lambda': vecw[3:4], 'conv_w': shard(dcw, wd // 4), 'conv_b': vecw[4:5],
        'conv_norm_g': vecw[5:6], 'conv_norm_b': vecw[6:7], 'ln1_g': dln[0:1], 'ln1_b': dln[1:2],
        'ffn_conv_w': shard(dfc[0:3], f // 4), 'ffn_conv_b': dfc[3:4], 'ln2_g': dln[2:3], 'ln2_b': dln[3:4]}
    names = list(small)
    gs = [small[n] if n == 'b_ada' else small[n].reshape(wt[n].shape) for n in names]
    gs, dls, mms, vvs = _adamw_small([wt[n] for n in names], gs, [mo[n] for n in names], [vo[n] for n in names])
    for n, g, dl, mm, vv in zip(names, gs, dls, mms, vvs):
        grads[n], deltas[n], new_m[n], new_v[n] = g, dl, mm, vv

    return (loss, grad_x, *[grads[n] for n in _WEIGHTS], *[deltas[n] for n in _WEIGHTS],
            *[new_m[n] for n in _WEIGHTS], *[new_v[n] for n in _WEIGHTS])
```

```python
import functools
import itertools
import math

import jax
import jax.numpy as jnp
from jax import lax
from jax.experimental import pallas as pl
from jax.experimental.pallas import tpu as pltpu

_MXU_DT = jnp.bfloat16
_F32 = jnp.float32
_VMEM_LIMIT = 56 * 1024 * 1024
_TT_MIX = 256
_TT_MIX_FWD = 512
_TT_FFN = 256
_TK_WGRAD = 2048
_HALO = 32

_LRU_C = 8.0
_LN_EPS = 1e-5
_N_HEADS = 8
_DEPTH = 1
_ALPHA = (2 * _DEPTH) ** 0.25
_ADAM_LR, _ADAM_B1, _ADAM_B2, _ADAM_EPS, _ADAM_WD, _ADAM_STEP = 0.001, 0.9, 0.999, 1e-08, 0.01, 10

_MESH = pl.DeviceIdType.MESH
_CHIP_DELTAS = ((1, 0), (0, 1), (1, 1))


def _cparams(sem):
    return pltpu.CompilerParams(dimension_semantics=sem, vmem_limit_bytes=_VMEM_LIMIT)


def _resident(shape):
    nd = len(shape)
    return pl.BlockSpec(shape, lambda *_: (0,) * nd, pipeline_mode=pl.Buffered(1))


def _dot(a, b):
    return jnp.dot(a, b, preferred_element_type=_F32)


def _dot_nt(a, b):
    return lax.dot_general(a, b, (((1,), (1,)), ((), ())), preferred_element_type=_F32)


def _dot_tn(a, b):
    return lax.dot_general(a, b, (((0,), (0,)), ((), ())), preferred_element_type=_F32)


def _mx(v):
    return v.astype(_MXU_DT)


def _expm1(v):
    series = v * (1.0 + v * (1.0 / 2 + v * (1.0 / 6 + v * (1.0 / 24 + v * (1.0 / 120)))))
    return jnp.where(jnp.abs(v) < 0.0625, series, jnp.exp(v) - 1.0)


def _softplus(z):
    e = jnp.exp(-jnp.abs(z))
    u = 1.0 + e
    log1p = jnp.where(u == 1.0, e, jnp.log(u) * e / jnp.where(u == 1.0, 1.0, u - 1.0))
    return jnp.maximum(z, 0.0) + log1p


_GELU_C = math.sqrt(2.0 / math.pi)


def _gelu_and_grad(v):
    t = jnp.tanh(_GELU_C * (v + 0.044715 * v * v * v))
    val = 0.5 * v * (1.0 + t)
    grad = 0.5 * (1.0 + t) + 0.5 * v * (1.0 - t * t) * _GELU_C * (1.0 + 3 * 0.044715 * v * v)
    return val, grad


def _seg_sum(v, seg, passes=3):
    hi = v.astype(jnp.bfloat16)
    r1 = v - hi.astype(_F32)
    mid = r1.astype(jnp.bfloat16)
    out = _dot(hi, seg) + _dot(mid, seg)
    if passes == 3:
        out = out + _dot((r1 - mid.astype(_F32)).astype(jnp.bfloat16), seg)
    return out


_SCAN_BLOCK = 32


def _scan_fwd(a, u, h0):
    n = a.shape[0]
    blk = min(_SCAN_BLOCK, n)
    sub = lax.broadcasted_iota(jnp.int32, a.shape, 0) % blk
    h, d = u, 1
    while d < blk:
        keep = sub >= d
        h = a * jnp.where(keep, pltpu.roll(h, d, 0), 0.0) + h
        a = a * jnp.where(keep, pltpu.roll(a, d, 0), 1.0)
        d *= 2
    out, carry = [], h0
    for b in range(n // blk):
        rows = slice(b * blk, (b + 1) * blk)
        out.append(h[rows] + a[rows] * carry)
        carry = out[-1][blk - 1:blk, :]
    return jnp.concatenate(out, axis=0)


def _scan_rev(c, g, g_end):
    n = c.shape[0]
    blk = min(_SCAN_BLOCK, n)
    sub = lax.broadcasted_iota(jnp.int32, c.shape, 0) % blk
    d = 1
    while d < blk:
        keep = sub < blk - d
        g = c * jnp.where(keep, pltpu.roll(g, n - d, 0), 0.0) + g
        c = c * jnp.where(keep, pltpu.roll(c, n - d, 0), 1.0)
        d *= 2
    out, carry = [None] * (n // blk), g_end
    for b in reversed(range(n // blk)):
        rows = slice(b * blk, (b + 1) * blk)
        out[b] = g[rows] + c[rows] * carry
        carry = out[b][0:1, :]
    return jnp.concatenate(out, axis=0)


def _layer_norm_stats(z):
    mu = jnp.mean(z, axis=-1, keepdims=True)
    zc = z - mu
    var = jnp.mean(zc * zc, axis=-1, keepdims=True)
    rstd = lax.rsqrt(var + _LN_EPS)
    return zc * rstd, rstd


def _layer_norm_bwd(dn, n, rstd):
    return rstd * (dn - jnp.mean(dn, axis=-1, keepdims=True) - n * jnp.mean(dn * n, axis=-1, keepdims=True))


def _rowsum(v):
    return jnp.sum(v, axis=0, keepdims=True)


def _fused_exchange(body, n_in, n_out, n_scratch, n_xin, n_xout, plan, grid):
    def wrapped(*refs):
        o0 = n_in + n_xin
        s0 = o0 + n_out + n_xout
        start, finish = plan(refs[n_in:o0], refs[o0 + n_out:s0], *refs[s0 + n_scratch:])
        step = 0
        for axis, size in enumerate(grid):
            step = step * size + pl.program_id(axis)

        @pl.when(step == 0)
        def _():
            start()

        body(*refs[:n_in], *refs[o0:o0 + n_out], *refs[s0:s0 + n_scratch])

        @pl.when(step == math.prod(grid) - 1)
        def _():
            finish()

    return wrapped


def _lru_gates(xc, wr_ref, wi_ref, br_ref, bi_ref, lam_ref):
    xcb = _mx(xc)
    r = jax.nn.sigmoid(_dot(xcb, wr_ref[...]) + br_ref[...])
    i = jax.nn.sigmoid(_dot(xcb, wi_ref[...]) + bi_ref[...])
    sp = _softplus(-lam_ref[...])
    log_a = -_LRU_C * r * sp
    a = jnp.exp(log_a)
    mult = jnp.sqrt(-_expm1(2.0 * log_a))
    return r, i, sp, a, mult


def _conv_taps(ext_ref, w_ref, first, n_taps, tt):
    acc = w_ref[0:1, :] * ext_ref[pl.ds(first, tt), :]
    for k in range(1, n_taps):
        acc = acc + w_ref[k:k + 1, :] * ext_ref[pl.ds(first + k, tt), :]
    return acc


def _make_shifted(ext_ref, sh_ref):
    n = sh_ref.shape[1]
    for r in range(1, 8):
        sh_ref[r - 1] = ext_ref[pl.ds(r, n), :]


def _tap(ext_ref, sh_ref, off, tt):
    base = (off // 8) * 8
    if off % 8 == 0:
        return ext_ref[pl.ds(base, tt), :]
    return sh_ref[off % 8 - 1, pl.ds(base, tt), :]


def _conv_taps_shifted(ext_ref, sh_ref, w_ref, first, n_taps, tt):
    acc = w_ref[0:1, :] * _tap(ext_ref, sh_ref, first, tt)
    for k in range(1, n_taps):
        acc = acc + w_ref[k:k + 1, :] * _tap(ext_ref, sh_ref, first + k, tt)
    return acc


def _mix_fwd(x, mod3, win, lcw, lcb, wr_bd, wi_bd, b_r, b_i, lam, cw, cb, ng, nb, seg, wout, ln1g, ln1b, shards):
    bl, s_len, d = x.shape
    w = d // 2
    tt = min(_TT_MIX_FWD, s_len)
    ns = s_len // tt
    kc = cw.shape[0]

    def body(x_ref, mod_ref, win_ref, lcw_ref, lcb_ref, wr_ref, wi_ref, br_ref, bi_ref, lam_ref, cw_ref, cb_ref,
             ng_ref, nb_ref, seg_ref, wout_ref, g1_ref, b1_ref,
             proj_ref, h_ref, mix_ref, x1_ref, u1_ref, y_ref, vbc_ref, lru_ref, ext4, ext31, sh31, hcar):
        @pl.when(pl.program_id(1) == 0)
        def _():
            ext4[0:8, :] = jnp.zeros((8, w), _F32)
            ext31[0:_HALO, :] = jnp.zeros((_HALO, w), _F32)
            hcar[...] = jnp.zeros_like(hcar)

        xt = x_ref[...]
        sh1, sc1, gt1 = mod_ref[:, 0:d], mod_ref[:, d:2 * d], mod_ref[:, 2 * d:3 * d]
        u1 = _mx(xt * (1.0 + sc1) + sh1)
        u1_ref[...] = u1
        xa, ga, vb, gb = (_dot(u1, win_ref[k]) for k in range(4))
        proj_ref[:, 0:w] = xa
        proj_ref[:, w:2 * w] = ga
        proj_ref[:, 2 * w:3 * w] = vb
        proj_ref[:, 3 * w:4 * w] = gb

        ext4[8:8 + tt, :] = xa
        xc = lcb_ref[...] + _conv_taps(ext4, lcw_ref, 5, 4, tt)
        ext4[0:8, :] = xa[tt - 8:tt, :]
        r, i, sp, a, mult = _lru_gates(xc, wr_ref, wi_ref, br_ref, bi_ref, lam_ref)
        for k, val in enumerate((xc, r, i, a, mult)):
            lru_ref[:, k * w:(k + 1) * w] = val
        h = _scan_fwd(a, mult * (i * xc), hcar[0:1, :])
        hcar[0:1, :] = h[tt - 1:tt, :]
        h_ref[...] = h
        gelu, _ = _gelu_and_grad(ga)
        y_ref[:, 0:w] = _mx(gelu * h)

        vbg = vb * jax.nn.sigmoid(gb)
        ext31[_HALO:_HALO + tt, :] = vbg
        _make_shifted(ext31, sh31)
        vbc = cb_ref[...] + _conv_taps_shifted(ext31, sh31, cw_ref, _HALO - (kc - 1), kc, tt)
        vbc_ref[...] = vbc
        ext31[0:_HALO, :] = vbg[tt - _HALO:tt, :]
        inv = 1.0 / (w // _N_HEADS)
        zc = vbc - _seg_sum(vbc, seg_ref[...]) * inv
        n = zc * lax.rsqrt(_seg_sum(zc * zc, seg_ref[...], 2) * inv + _LN_EPS)
        pre = n * ng_ref[...] + nb_ref[...]
        y_ref[:, w:2 * w] = _mx(pre * jax.nn.sigmoid(pre))

        mix = _dot(y_ref[...], wout_ref[...])
        mix_ref[...] = mix
        n1, _ = _layer_norm_stats(_ALPHA * xt + (1.0 + gt1) * mix)
        x1_ref[...] = n1 * g1_ref[...] + b1_ref[...]

    tok = lambda c: pl.BlockSpec((None, tt, c), lambda b, s: (b, s, 0))
    smalls = [lcw, lcb, wr_bd, wi_bd, b_r, b_i, lam, cw, cb, ng, nb, seg, wout, ln1g, ln1b]
    nx = len(shards)
    return pl.pallas_call(
        _fused_exchange(body, 3 + len(smalls), 8, 4, nx, nx, _gather_plan, (bl, ns)), grid=(bl, ns),
        in_specs=[tok(d), pl.BlockSpec((None, 1, 6 * d), lambda b, s: (b, 0, 0)), _resident(win.shape)]
        + [_resident(t.shape) for t in smalls] + [_HBM] * nx,
        out_specs=[tok(4 * w), tok(w), tok(d), tok(d), tok(d), tok(d), tok(w), tok(5 * w)] + [_HBM] * nx,
        out_shape=[jax.ShapeDtypeStruct((bl, s_len, 4 * w), _F32), jax.ShapeDtypeStruct((bl, s_len, w), _F32),
                   jax.ShapeDtypeStruct((bl, s_len, d), _F32), jax.ShapeDtypeStruct((bl, s_len, d), _F32),
                   jax.ShapeDtypeStruct((bl, s_len, d), _MXU_DT), jax.ShapeDtypeStruct((bl, s_len, d), _MXU_DT),
                   jax.ShapeDtypeStruct((bl, s_len, w), _F32), jax.ShapeDtypeStruct((bl, s_len, 5 * w), _F32)]
        + [jax.ShapeDtypeStruct((4,) + t.shape, t.dtype) for t in shards],
        scratch_shapes=[pltpu.VMEM((tt + 8, w), _F32), pltpu.VMEM((tt + _HALO, w), _F32),
                        pltpu.VMEM((7, tt + _HALO - 8, w), _F32), pltpu.VMEM((8, w), _F32)] + _gather_sems(nx),
        compiler_params=_cparams(("arbitrary", "arbitrary")), name="mix_fwd",
    )(x, mod3, win, *smalls, *shards)


def _ffn_fwd(x1, mod3, wup, fcw, fcb, wdn, ln2g, ln2b, target):
    bl, s_len, d = x1.shape
    nch, _, fc = wup.shape
    nch //= 2
    f = nch * fc
    tt = min(_TT_FFN, s_len)
    ns = s_len // tt

    def body(x1_ref, mod_ref, wup_ref, fcw_ref, fcb_ref, wdn_ref, g2_ref, b2_ref, tgt_ref,
             u2_ref, hh_ref, f_ref, gc_ref, dz2_ref, loss_ref, dln2_ref, dgt2_ref, ext3):
        first_tile = pl.program_id(1) == 0

        @pl.when(first_tile)
        def _():
            ext3[:, 0:8, :] = jnp.zeros((nch, 8, fc), _F32)
            dgt2_ref[...] = jnp.zeros_like(dgt2_ref)

        @pl.when(first_tile & (pl.program_id(0) == 0))
        def _():
            loss_ref[...] = jnp.zeros_like(loss_ref)
            dln2_ref[...] = jnp.zeros_like(dln2_ref)

        x1t = x1_ref[...]
        sh2, sc2, gt2 = mod_ref[:, 3 * d:4 * d], mod_ref[:, 4 * d:5 * d], mod_ref[:, 5 * d:6 * d]
        u2 = _mx(x1t * (1.0 + sc2) + sh2)
        u2_ref[...] = u2
        y2 = jnp.zeros((tt, d), _F32)
        for j in range(nch):
            lanes = slice(j * fc, (j + 1) * fc)
            v = _dot(u2, wup_ref[j])
            g = _dot(u2, wup_ref[nch + j])
            hh_ref[:, lanes] = v.astype(hh_ref.dtype)
            hh_ref[:, f + j * fc:f + (j + 1) * fc] = g.astype(hh_ref.dtype)
            ext = ext3.at[j]
            ext[8:8 + tt, :] = g
            gc = fcb_ref[:, lanes] + sum(fcw_ref[k:k + 1, lanes] * ext[pl.ds(6 + k, tt), :] for k in range(3))
            gc_ref[:, lanes] = gc
            ext[0:8, :] = g[tt - 8:tt, :]
            fj = _mx(gc * jax.nn.sigmoid(gc) * v)
            f_ref[:, lanes] = fj
            y2 = y2 + _dot(fj, wdn_ref[lanes, :])

        n2, rstd = _layer_norm_stats(_ALPHA * x1t + (1.0 + gt2) * y2)
        err = n2 * g2_ref[...] + b2_ref[...] - tgt_ref[...]
        loss_ref[...] += jnp.sum(_rowsum(err * err), axis=1, keepdims=True)
        dout = err * (1.0 / d)
        dln2_ref[0:1, :] += _rowsum(dout * n2)
        dln2_ref[1:2, :] += _rowsum(dout)
        dz2 = _layer_norm_bwd(dout * g2_ref[...], n2, rstd)
        dz2_ref[...] = dz2
        dgt2_ref[...] += _rowsum(dz2 * y2)

    tok = lambda c: pl.BlockSpec((None, tt, c), lambda b, s: (b, s, 0))
    acc = lambda r: pl.BlockSpec((r, d), lambda b, s: (0, 0))
    smalls = [fcw, fcb, wdn, ln2g, ln2b]
    return pl.pallas_call(
        body, grid=(bl, ns),
        in_specs=[tok(d), pl.BlockSpec((None, 1, 6 * d), lambda b, s: (b, 0, 0)), _resident(wup.shape)]
        + [_resident(t.shape) for t in smalls] + [tok(d)],
        out_specs=[tok(d), tok(2 * f), tok(f), tok(f), tok(d), acc(1), acc(2), pl.BlockSpec((None, 1, d), lambda b, s: (b, 0, 0))],
        out_shape=[jax.ShapeDtypeStruct((bl, s_len, d), _MXU_DT), jax.ShapeDtypeStruct((bl, s_len, 2 * f), _F32),
                   jax.ShapeDtypeStruct((bl, s_len, f), _MXU_DT), jax.ShapeDtypeStruct((bl, s_len, f), _F32),
                   jax.ShapeDtypeStruct((bl, s_len, d), _F32), jax.ShapeDtypeStruct((1, d), _F32), jax.ShapeDtypeStruct((2, d), _F32),
                   jax.ShapeDtypeStruct((bl, 1, d), _F32)],
        scratch_shapes=[pltpu.VMEM((nch, tt + 8, fc), _F32)],
        compiler_params=_cparams(("arbitrary", "arbitrary")), name="ffn_fwd",
    )(x1, mod3, wup, *smalls, target)


def _ffn_bwd(dz2, x1, hh, gc_all, mod3, wup, wdn, fcw, fcb):
    bl, s_len, d = x1.shape
    nch, _, fc = wup.shape
    nch //= 2
    f = nch * fc
    tt = min(_TT_FFN, s_len)
    ns = s_len // tt

    def body(dz2_ref, x1_ref, hh_ref, gc_ref, mod_ref, wup_ref, wdn_ref, fcw_ref, fcb_ref,
             dx1_ref, dy2_ref, dh_ref, dfc_ref, dmod_ref, dext, dcar):
        @pl.when(pl.program_id(1) == 0)
        def _():
            dcar[...] = jnp.zeros_like(dcar)
            dmod_ref[...] = jnp.zeros_like(dmod_ref)

        @pl.when((pl.program_id(1) == 0) & (pl.program_id(0) == 0))
        def _():
            dfc_ref[...] = jnp.zeros_like(dfc_ref)

        sc2, gt2 = mod_ref[:, 4 * d:5 * d], mod_ref[:, 5 * d:6 * d]
        dz2t = dz2_ref[...]
        dy2 = _mx((1.0 + gt2) * dz2t)
        dy2_ref[...] = dy2
        du2 = jnp.zeros((tt, d), _F32)
        for j in range(nch):
            lanes = slice(j * fc, (j + 1) * fc)
            glanes = slice(f + j * fc, f + (j + 1) * fc)
            v = hh_ref[:, lanes].astype(_F32)
            g = hh_ref[:, glanes].astype(_F32)
            gc = gc_ref[:, lanes]
            sg = jax.nn.sigmoid(gc)
            df = _dot_nt(dy2, wdn_ref[lanes, :])
            dv = df * (gc * sg)
            dgc = df * v * (sg * (1.0 + gc * (1.0 - sg)))
            dfc_ref[3:4, lanes] += _rowsum(dgc)
            dext[0:tt, :] = dgc
            dext[tt:tt + 8, :] = dcar[j]
            dcar[j] = dgc[0:8, :]
            dg = jnp.zeros((tt, fc), _F32)
            for k in range(3):
                shifted = dext[pl.ds(2 - k, tt), :]
                dg = dg + fcw_ref[k:k + 1, lanes] * shifted
                dfc_ref[k:k + 1, lanes] += _rowsum(shifted * g)
            dvb, dgb = _mx(dv), _mx(dg)
            dh_ref[:, lanes] = dvb
            dh_ref[:, glanes] = dgb
            du2 = du2 + _dot_nt(dvb, wup_ref[j]) + _dot_nt(dgb, wup_ref[nch + j])

        dx1_ref[...] = _ALPHA * dz2t + du2 * (1.0 + sc2)
        dmod_ref[0:1, :] += _rowsum(du2)
        dmod_ref[1:2, :] += _rowsum(du2 * x1_ref[...])

    tok = lambda c: pl.BlockSpec((None, tt, c), lambda b, i: (b, ns - 1 - i, 0))
    return pl.pallas_call(
        body, grid=(bl, ns),
        in_specs=[tok(d), tok(d), tok(2 * f), tok(f), pl.BlockSpec((None, 1, 6 * d), lambda b, i: (b, 0, 0)),
                  _resident(wup.shape), _resident(wdn.shape), _resident(fcw.shape), _resident(fcb.shape)],
        out_specs=[tok(d), tok(d), tok(2 * f), pl.BlockSpec((4, f), lambda b, i: (0, 0)),
                   pl.BlockSpec((None, 2, d), lambda b, i: (b, 0, 0))],
        out_shape=[jax.ShapeDtypeStruct((bl, s_len, d), _F32), jax.ShapeDtypeStruct((bl, s_len, d), _MXU_DT),
                   jax.ShapeDtypeStruct((bl, s_len, 2 * f), _MXU_DT), jax.ShapeDtypeStruct((4, f), _F32),
                   jax.ShapeDtypeStruct((bl, 2, d), _F32)],
        scratch_shapes=[pltpu.VMEM((tt + 8, fc), _F32), pltpu.VMEM((nch, 8, fc), _F32)],
        compiler_params=_cparams(("arbitrary", "arbitrary")), name="ffn_bwd",
    )(dz2, x1, hh, gc_all, mod3, wup, wdn, fcw, fcb)


def _mix_bwd(dx1, x, mix, proj, h, vbc, lru, mod3, win, lcw, lcb, wr_bd, wi_bd, b_r, b_i, lam, cw, cb, ng, nb, seg, wout, ln1g, chip_sums):
    bl, s_len, d = x.shape
    w = d // 2
    tt = min(_TT_MIX, s_len)
    ns = s_len // tt
    kc = cw.shape[0]

    def body(dx1_ref, x_ref, mix_ref, proj_ref, phalo_ref, h_ref, hhalo_ref, vbc_ref, lru_ref, mod_ref, win_ref, lcw_ref, lcb_ref,
             wr_ref, wi_ref, br_ref, bi_ref, lam_ref, cw_ref, cb_ref, ng_ref, nb_ref, seg_ref, wout_ref, g1_ref,
             gx_ref, dproj_ref, dmix_ref, xcg_ref, vecw_ref, dlcw_ref, dcw_ref, dln1_ref, dmod_ref,
             ext4, ext31, dext4, dext31, sh31, dsh31, car4, car31, gcar):
        s = ns - 1 - pl.program_id(1)
        first = s == 0

        @pl.when(pl.program_id(1) == 0)
        def _():
            car4[...] = jnp.zeros_like(car4)
            car31[...] = jnp.zeros_like(car31)
            gcar[...] = jnp.zeros_like(gcar)
            dmod_ref[...] = jnp.zeros_like(dmod_ref)

        @pl.when((pl.program_id(1) == 0) & (pl.program_id(0) == 0))
        def _():
            for ref in (vecw_ref, dlcw_ref, dcw_ref, dln1_ref):
                ref[...] = jnp.zeros_like(ref)

        xt, mixt = x_ref[...], mix_ref[...]
        sh1, sc1, gt1 = mod_ref[:, 0:d], mod_ref[:, d:2 * d], mod_ref[:, 2 * d:3 * d]

        n1, rstd1 = _layer_norm_stats(_ALPHA * xt + (1.0 + gt1) * mixt)
        dx1t = dx1_ref[...]
        dln1_ref[0:1, :] += _rowsum(dx1t * n1)
        dln1_ref[1:2, :] += _rowsum(dx1t)
        dz1 = _layer_norm_bwd(dx1t * g1_ref[...], n1, rstd1)
        dmod_ref[2:3, :] += _rowsum(dz1 * mixt)
        dmix = _mx((1.0 + gt1) * dz1)
        dmix_ref[...] = dmix
        dya = _dot_nt(dmix, wout_ref[0:w, :])
        dyb = _dot_nt(dmix, wout_ref[w:2 * w, :])

        xa, ga = proj_ref[:, 0:w], proj_ref[:, w:2 * w]
        vb, gb = proj_ref[:, 2 * w:3 * w], proj_ref[:, 3 * w:4 * w]

        sgb = jax.nn.sigmoid(gb)
        vbg = vb * sgb
        hv, hg = phalo_ref[:, 2 * w:3 * w], phalo_ref[:, 3 * w:4 * w]
        ext31[0:_HALO, :] = jnp.where(first, 0.0, hv * jax.nn.sigmoid(hg))
        ext31[_HALO:_HALO + tt, :] = vbg
        _make_shifted(ext31, sh31)
        vbc = vbc_ref[...]
        inv = 1.0 / (w // _N_HEADS)
        zc = vbc - _seg_sum(vbc, seg_ref[...]) * inv
        rstd = lax.rsqrt(_seg_sum(zc * zc, seg_ref[...], 2) * inv + _LN_EPS)
        n = zc * rstd
        pre = n * ng_ref[...] + nb_ref[...]
        sgp = jax.nn.sigmoid(pre)
        dpre = dyb * (sgp * (1.0 + pre * (1.0 - sgp)))
        vecw_ref[5:6, :] += _rowsum(dpre * n)
        vecw_ref[6:7, :] += _rowsum(dpre)
        dn = dpre * ng_ref[...]
        dvbc = rstd * (dn - _seg_sum(dn, seg_ref[...], 2) * inv - n * (_seg_sum(dn * n, seg_ref[...], 2) * inv))
        vecw_ref[4:5, :] += _rowsum(dvbc)
        dext31[0:tt, :] = dvbc
        dext31[tt:tt + _HALO, :] = car31[...]
        car31[...] = dvbc[0:_HALO, :]
        _make_shifted(dext31, dsh31)
        dvbg = jnp.zeros((tt, w), _F32)
        for k in range(kc):
            dvbg = dvbg + cw_ref[k:k + 1, :] * _tap(dext31, dsh31, kc - 1 - k, tt)
            dcw_ref[k:k + 1, :] += _rowsum(dvbc * _tap(ext31, sh31, _HALO - (kc - 1) + k, tt))
        dproj_ref[:, 2 * w:3 * w] = _mx(dvbg * sgb)
        dproj_ref[:, 3 * w:4 * w] = _mx(dvbg * vb * (sgb * (1.0 - sgb)))

        ext4[0:8, :] = jnp.where(first, 0.0, phalo_ref[_HALO - 8:_HALO, 0:w])
        ext4[8:8 + tt, :] = xa
        xc, r, i, a, mult = (lru_ref[:, k * w:(k + 1) * w] for k in range(5))
        xcg_ref[:, 0:w] = _mx(xc)
        sp = _softplus(-lam_ref[...])
        ht = h_ref[...]
        row = lax.broadcasted_iota(jnp.int32, (tt, w), 0)
        h_before = jnp.where(first, 0.0, hhalo_ref[7:8, :])
        hprev = jnp.where(row == 0, h_before, pltpu.roll(ht, 1, 0))
        gelu, dgelu = _gelu_and_grad(ga)
        dproj_ref[:, w:2 * w] = _mx(dya * ht * dgelu)
        dh = dya * gelu
        coef = jnp.where(row == tt - 1, 1.0, pltpu.roll(a, tt - 1, 0))
        big_g = _scan_rev(coef, dh, gcar[0:1, :])
        gcar[0:1, :] = a[0:1, :] * big_g[0:1, :]
        da = big_g * hprev
        ixc = i * xc
        dlog_a = da * a - (big_g * ixc) * (a * a / mult)
        di = big_g * mult * xc
        dxc = big_g * mult * i
        vecw_ref[3:4, :] += _rowsum(dlog_a * r) * (_LRU_C * jax.nn.sigmoid(-lam_ref[...]))
        dgr_f = dlog_a * (-_LRU_C * sp) * (r * (1.0 - r))
        dgi_f = di * (i * (1.0 - i))
        vecw_ref[1:2, :] += _rowsum(dgr_f)
        vecw_ref[2:3, :] += _rowsum(dgi_f)
        dgr, dgi = _mx(dgr_f), _mx(dgi_f)
        xcg_ref[:, w:2 * w] = dgr
        xcg_ref[:, 2 * w:3 * w] = dgi
        dxc = dxc + _dot_nt(dgr, wr_ref[...]) + _dot_nt(dgi, wi_ref[...])
        vecw_ref[0:1, :] += _rowsum(dxc)
        dext4[0:tt, :] = dxc
        dext4[tt:tt + 8, :] = car4[...]
        car4[...] = dxc[0:8, :]
        dxa = jnp.zeros((tt, w), _F32)
        for k in range(4):
            dxa = dxa + lcw_ref[k:k + 1, :] * dext4[pl.ds(3 - k, tt), :]
            dlcw_ref[k:k + 1, :] += _rowsum(dxc * ext4[pl.ds(5 + k, tt), :])
        dproj_ref[:, 0:w] = _mx(dxa)

        du1 = sum(_dot_nt(dproj_ref[:, k * w:(k + 1) * w], win_ref[k]) for k in range(4))
        gx_ref[...] = _ALPHA * dz1 + du1 * (1.0 + sc1)
        dmod_ref[0:1, :] += _rowsum(du1)
        dmod_ref[1:2, :] += _rowsum(du1 * xt)

    tok = lambda c: pl.BlockSpec((None, tt, c), lambda b, i: (b, ns - 1 - i, 0))
    halo = lambda rows, c: pl.BlockSpec(
        (None, rows, c), lambda b, i: (b, jnp.maximum((ns - 1 - i) * (tt // rows) - 1, 0), 0))
    accw = lambda r, c: pl.BlockSpec((r, c), lambda b, i: (0, 0))
    smalls = [lcw, lcb, wr_bd, wi_bd, b_r, b_i, lam, cw, cb, ng, nb, seg, wout, ln1g]
    nx = len(chip_sums)
    return pl.pallas_call(
        _fused_exchange(body, 11 + len(smalls), 9, 9, nx, nx, _chip_reduce_plan, (bl, ns)), grid=(bl, ns),
        in_specs=[tok(d), tok(d), tok(d), tok(4 * w), halo(_HALO, 4 * w), tok(w), halo(8, w), tok(w), tok(5 * w),
                  pl.BlockSpec((None, 1, 6 * d), lambda b, i: (b, 0, 0)), _resident(win.shape)]
        + [_resident(t.shape) for t in smalls] + [_HBM] * nx,
        out_specs=[tok(d), tok(4 * w), tok(d), tok(3 * w), accw(8, w), accw(4, w), accw(kc, w), accw(2, d),
                   pl.BlockSpec((None, 3, d), lambda b, i: (b, 0, 0))] + [_HBM] * nx,
        out_shape=[jax.ShapeDtypeStruct((bl, s_len, d), _F32), jax.ShapeDtypeStruct((bl, s_len, 4 * w), _MXU_DT),
                   jax.ShapeDtypeStruct((bl, s_len, d), _MXU_DT), jax.ShapeDtypeStruct((bl, s_len, 3 * w), _MXU_DT),
                   jax.ShapeDtypeStruct((8, w), _F32), jax.ShapeDtypeStruct((4, w), _F32),
                   jax.ShapeDtypeStruct((kc, w), _F32), jax.ShapeDtypeStruct((2, d), _F32),
                   jax.ShapeDtypeStruct((bl, 3, d), _F32)]
        + [jax.ShapeDtypeStruct((3,) + t.shape[1:], t.dtype) for t in chip_sums],
        scratch_shapes=[pltpu.VMEM((tt + 8, w), _F32), pltpu.VMEM((tt + _HALO, w), _F32),
                        pltpu.VMEM((tt + 8, w), _F32), pltpu.VMEM((tt + _HALO, w), _F32),
                        pltpu.VMEM((7, tt + _HALO - 8, w), _F32), pltpu.VMEM((7, tt + _HALO - 8, w), _F32),
                        pltpu.VMEM((8, w), _F32), pltpu.VMEM((_HALO, w), _F32), pltpu.VMEM((8, w), _F32)]
        + _chip_reduce_sems(nx),
        compiler_params=_cparams(("arbitrary", "arbitrary")), name="mix_bwd",
    )(dx1, x, mix, proj, proj, h, h, vbc, lru, mod3, win, *smalls, *chip_sums)


def _wgrad(a, b, ma, nbw, na, nb, a_off, b_off, name, exchange=None):
    t = a.shape[0]
    tk = min(_TK_WGRAD, t)
    grid = (na * nb, t // tk)

    def body(a_ref, b_ref, o_ref):
        @pl.when(pl.program_id(1) == 0)
        def _():
            o_ref[...] = jnp.zeros_like(o_ref)
        o_ref[...] += _dot_tn(a_ref[...], b_ref[...])

    xin, xshapes, plan, sems = exchange if exchange else ([], [], None, [])
    nx = len(xin)
    res = pl.pallas_call(
        _fused_exchange(body, 2, 1, 0, nx, len(xshapes), plan, grid) if exchange else body, grid=grid,
        in_specs=[pl.BlockSpec((tk, ma), lambda j, k: (k, j // nb + a_off)),
                  pl.BlockSpec((tk, nbw), lambda j, k: (k, j % nb + b_off))] + [_HBM] * nx,
        out_specs=[pl.BlockSpec((None, ma, nbw), lambda j, k: (j, 0, 0))] + [_HBM] * len(xshapes),
        out_shape=[jax.ShapeDtypeStruct((na * nb, ma, nbw), _F32)] + list(xshapes),
        scratch_shapes=list(sems),
        compiler_params=_cparams(("arbitrary", "arbitrary")), name=name,
    )(a, b, *xin)
    return res if exchange else res[0]


_DEV_DELTAS = tuple(dl for dl in itertools.product((0, 1), repeat=3) if any(dl))
_HBM = pl.BlockSpec(memory_space=pltpu.HBM)
_VMEM = pl.BlockSpec(memory_space=pltpu.VMEM)


def _pos():
    return lax.axis_index("x"), lax.axis_index("y"), lax.axis_index("c")


def _flip(v, delta):
    return 1 - v if delta else v


def _remote(src, dst, ssem, rsem, dev):
    return pltpu.make_async_remote_copy(src_ref=src, dst_ref=dst, send_sem=ssem, recv_sem=rsem,
                                        device_id=dev, device_id_type=_MESH)


def _rows(ref, idx, n):
    return ref.at[pl.ds(pl.multiple_of(idx * n, 8), n)]


def _ada_fwd(c8, w_ada_k, b_ada_k, shards):
    rows, d = c8.shape
    nk = w_ada_k.shape[1]
    n = len(shards)

    def body(*refs):
        c_ref, w_ref, b_ref = refs[:3]
        call_ref, mod_ref = refs[3 + n:5 + n]
        modloc, modrcv, s1, r1, s2, r2 = refs[5 + 2 * n:11 + 2 * n]
        gather_start, gather_finish = _gather_plan(refs[3:3 + n], refs[5 + n:5 + 2 * n], *refs[11 + 2 * n:14 + 2 * n],
                                                   fsem=refs[14 + 2 * n], frsem=refs[15 + 2 * n], bounce=refs[16 + 2 * n:])
        gather_start()
        xi, yi, ci = _pos()
        me, kme = 4 * xi + 2 * yi + ci, 2 * xi + yi
        call_ref[pl.ds(pl.multiple_of(me * rows, 8), rows), :] = c_ref[...]
        sends = []
        for p, (dx, dy, dc) in enumerate(_DEV_DELTAS):
            cp = _remote(c_ref, _rows(call_ref, me, rows), s1.at[p], r1.at[p], (_flip(xi, dx), _flip(yi, dy), _flip(ci, dc)))
            cp.start()
            sends.append(cp)
        for p, (dx, dy, dc) in enumerate(_DEV_DELTAS):
            src = 4 * _flip(xi, dx) + 2 * _flip(yi, dy) + _flip(ci, dc)
            _remote(c_ref, _rows(call_ref, src, rows), s1.at[p], r1.at[p], (xi, yi, ci)).wait_recv()
        for cp in sends:
            cp.wait_send()

        ca = call_ref[...]
        modloc[...] = _dot(_mx(ca * jax.nn.sigmoid(ca)), _mx(w_ref[...])) + b_ref[...]
        modrcv[kme] = modloc[pl.ds(pl.multiple_of(me * rows, 8), rows), :]
        sends = []
        for j, (dx, dy) in enumerate(_CHIP_DELTAS):
            tx, ty = _flip(xi, dx), _flip(yi, dy)
            cp = _remote(_rows(modloc, 4 * tx + 2 * ty + ci, rows), modrcv.at[kme], s2.at[j], r2.at[j], (tx, ty, ci))
            cp.start()
            sends.append(cp)
        for j, (dx, dy) in enumerate(_CHIP_DELTAS):
            ksrc = 2 * _flip(xi, dx) + _flip(yi, dy)
            _remote(_rows(modloc, me, rows), modrcv.at[ksrc], s2.at[j], r2.at[j], (xi, yi, ci)).wait_recv()
        for cp in sends:
            cp.wait_send()
        for j in range(4):
            mod_ref[:, j * nk:(j + 1) * nk] = modrcv[j]
        gather_finish()

    return pl.pallas_call(
        body, in_specs=[_VMEM, _VMEM, _VMEM] + [_HBM] * n, out_specs=[_VMEM, _VMEM] + [_HBM] * n,
        out_shape=[jax.ShapeDtypeStruct((8 * rows, d), _F32), jax.ShapeDtypeStruct((rows, 4 * nk), _F32)]
        + [jax.ShapeDtypeStruct((4,) + a.shape, a.dtype) for a in shards],
        scratch_shapes=[pltpu.VMEM((8 * rows, nk), _F32), pltpu.VMEM((4, rows, nk), _F32),
                        pltpu.SemaphoreType.DMA((7,)), pltpu.SemaphoreType.DMA((7,)),
                        pltpu.SemaphoreType.DMA((3,)), pltpu.SemaphoreType.DMA((3,))]
        + _gather_sems(n) + [pltpu.SemaphoreType.DMA((3, n)), pltpu.SemaphoreType.DMA((3, n))]
        + [pltpu.VMEM(a.shape, a.dtype) for a in shards],
        compiler_params=pltpu.CompilerParams(vmem_limit_bytes=_VMEM_LIMIT), name="ada_fwd",
    )(c8, w_ada_k, b_ada_k, *shards)


def _gather_sems(n):
    return [pltpu.SemaphoreType.DMA((3, n)), pltpu.SemaphoreType.DMA((3, n)), pltpu.SemaphoreType.DMA((n,))]


def _gather_plan(ins, outs, ssem, rsem, lsem, bounce=(), fsem=None, frsem=None):
    n = len(ins)
    xi, yi, ci = _pos()
    kme = 2 * xi + yi
    split = [fsem is not None and ins[a].shape[0] % 32 == 0 for a in range(n)]

    def half(ref, a, which):
        r2 = ins[a].shape[0] // 2
        return ref.at[pl.ds(pl.multiple_of(which * r2, 16), r2)]

    staged = [pltpu.make_async_copy(ins[a], bounce[a], lsem.at[a]) for a in range(len(bounce))]
    local = [pltpu.make_async_copy(bounce[a] if bounce else ins[a], outs[a].at[kme], lsem.at[a]) for a in range(n)]
    sends, recvs, forwards, handed = [], [], [], []
    for j, (dx, dy) in enumerate(_CHIP_DELTAS):
        tx, ty = _flip(xi, dx), _flip(yi, dy)
        for a in range(n):
            sems = (ssem.at[j, a], rsem.at[j, a])
            landing = outs[a].at[2 * tx + ty]
            if split[a]:
                sends.append(_remote(half(ins[a], a, ci), half(outs[a].at[kme], a, ci), *sems, (tx, ty, ci)))
                recvs.append(_remote(half(ins[a], a, ci), half(landing, a, ci), *sems, (xi, yi, ci)))
                fsems = (fsem.at[j, a], frsem.at[j, a])
                forwards.append(_remote(half(landing, a, ci), half(landing, a, ci), *fsems, (xi, yi, 1 - ci)))
                handed.append(_remote(half(ins[a], a, 1 - ci), half(landing, a, 1 - ci), *fsems, (xi, yi, ci)))
            else:
                sends.append(_remote(ins[a], outs[a].at[kme], *sems, (tx, ty, ci)))
                recvs.append(_remote(ins[a], landing, *sems, (xi, yi, ci)))
                forwards.append(None)

    def start():
        for cp in sends + staged:
            cp.start()
        for cp in staged:
            cp.wait()
        for cp in local:
            cp.start()

    def finish():
        for arrived, forward in zip(recvs, forwards):
            arrived.wait_recv()
            if forward is not None:
                forward.start()
        for cp in handed:
            cp.wait_recv()
        for cp in sends + [f for f in forwards if f is not None]:
            cp.wait_send()
        for cp in local:
            cp.wait()

    return start, finish


def _dev_gather_sems(n):
    return [pltpu.SemaphoreType.DMA((7, n)), pltpu.SemaphoreType.DMA((7, n)), pltpu.SemaphoreType.DMA((n,))]


def _dev_gather_plan(ins, outs, ssem, rsem, lsem):
    n = len(ins)
    xi, yi, ci = _pos()
    me = 4 * xi + 2 * yi + ci
    local = [pltpu.make_async_copy(ins[a], outs[a].at[me], lsem.at[a]) for a in range(n)]
    sends, recvs = [], []
    for p, (dx, dy, dc) in enumerate(_DEV_DELTAS):
        tx, ty, tc = _flip(xi, dx), _flip(yi, dy), _flip(ci, dc)
        for a in range(n):
            sends.append(_remote(ins[a], outs[a].at[me], ssem.at[p, a], rsem.at[p, a], (tx, ty, tc)))
            recvs.append(_remote(ins[a], outs[a].at[4 * tx + 2 * ty + tc], ssem.at[p, a], rsem.at[p, a], (xi, yi, ci)))

    def start():
        for cp in local + sends:
            cp.start()

    def finish():
        for cp in recvs:
            cp.wait_recv()
        for cp in sends:
            cp.wait_send()
        for cp in local:
            cp.wait()

    return start, finish


def _pair_sems(n):
    return [pltpu.SemaphoreType.DMA((n,)), pltpu.SemaphoreType.DMA((n,))]


def _pair_plan(ins, outs, ssem, rsem):
    xi, yi, ci = _pos()
    sends = []
    for a in range(len(ins)):
        r2 = ins[a].shape[1] // 2
        src = ins[a].at[:, pl.ds(pl.multiple_of((1 - ci) * r2, 8), r2), :]
        sends.append(_remote(src, outs[a], ssem.at[a], rsem.at[a], (xi, yi, 1 - ci)))

    def start():
        for cp in sends:
            cp.start()

    def finish():
        for cp in sends:
            cp.wait_recv()
        for cp in sends:
            cp.wait_send()

    return start, finish


def _chip_reduce_sems(n):
    return [pltpu.SemaphoreType.DMA((3, n)), pltpu.SemaphoreType.DMA((3, n))]


def _chip_reduce_plan(ins, outs, ssem, rsem):
    xi, yi, ci = _pos()
    sends = []
    for j, (dx, dy) in enumerate(_CHIP_DELTAS):
        tx, ty = _flip(xi, dx), _flip(yi, dy)
        sends += [_remote(ins[a].at[2 * tx + ty], outs[a].at[j], ssem.at[j, a], rsem.at[j, a], (tx, ty, ci))
                  for a in range(len(ins))]

    def start():
        for cp in sends:
            cp.start()

    def finish():
        for cp in sends:
            cp.wait_recv()
        for cp in sends:
            cp.wait_send()

    return start, finish


def _pair_exchange(gs, name):
    n = len(gs)

    def body(*refs):
        start, finish = _pair_plan(refs[:n], refs[n:2 * n], *refs[2 * n:])
        start()
        finish()

    return pl.pallas_call(
        body, in_specs=[_HBM] * n, out_specs=[_HBM] * n, out_shape=_pair_out_shapes(gs),
        scratch_shapes=_pair_sems(n), name=name,
    )(*gs)


def _pair_out_shapes(gs):
    return [jax.ShapeDtypeStruct((g.shape[0], g.shape[1] // 2, g.shape[2]), g.dtype) for g in gs]


def _row_tile(r):
    return max(t for t in range(8, min(r, 256) + 1, 8) if r % t == 0)


def _pair_add(g, r, cidx, name, wire_dtype=None, exchange=None):
    nk, r2, c = r.shape
    tr = _row_tile(r2)
    nt = r2 // tr
    xin, xshapes, plan, sems = exchange if exchange else ([], [], None, [])
    nx = len(xin)

    def body(c_ref, g_ref, r_ref, *o_refs):
        s = g_ref[...] + r_ref[...]
        for o_ref in o_refs:
            o_ref[...] = s.astype(o_ref.dtype)

    out_spec = pl.BlockSpec((None, tr, c), lambda k, i, cr: (k, i, 0))
    dtypes = [_F32] + ([wire_dtype] if wire_dtype else [])
    res = pl.pallas_call(
        _fused_exchange(body, 3, len(dtypes), 0, nx, len(xshapes), plan, (nk, nt)) if exchange else body,
        grid_spec=pltpu.PrefetchScalarGridSpec(
            num_scalar_prefetch=1, grid=(nk, nt),
            in_specs=[pl.BlockSpec((None, tr, c), lambda k, i, cr: (k, cr[0] * nt + i, 0)), out_spec] + [_HBM] * nx,
            out_specs=[out_spec] * len(dtypes) + [_HBM] * len(xshapes), scratch_shapes=list(sems)),
        out_shape=[jax.ShapeDtypeStruct(r.shape, dt) for dt in dtypes] + list(xshapes),
        compiler_params=_cparams(("arbitrary", "arbitrary")), name=name,
    )(cidx, g, r, *xin)
    return res if wire_dtype or exchange else res[0]


def _chip_exchange(ss):
    n = len(ss)

    def body(*refs):
        start, finish = _chip_reduce_plan(refs[:n], refs[n:2 * n], *refs[2 * n:])
        start()
        finish()

    return pl.pallas_call(
        body, in_specs=[_HBM] * n, out_specs=[_HBM] * n,
        out_shape=[jax.ShapeDtypeStruct((3,) + s.shape[1:], s.dtype) for s in ss],
        scratch_shapes=_chip_reduce_sems(n), name="grad_chip_exchange",
    )(*ss)


def _chip_add(s, r, kidx, name):
    _, r2, c = r.shape
    tr = _row_tile(r2)

    def body(k_ref, s_ref, r_ref, o_ref):
        o_ref[...] = ((s_ref[...] + r_ref[0].astype(_F32)) + r_ref[1].astype(_F32)) + r_ref[2].astype(_F32)

    return pl.pallas_call(
        body, grid_spec=pltpu.PrefetchScalarGridSpec(
            num_scalar_prefetch=1, grid=(r2 // tr,),
            in_specs=[pl.BlockSpec((None, tr, c), lambda i, kr: (kr[0], i, 0)),
                      pl.BlockSpec((3, tr, c), lambda i, kr: (0, i, 0))],
            out_specs=pl.BlockSpec((tr, c), lambda i, kr: (i, 0))),
        out_shape=jax.ShapeDtypeStruct((r2, c), _F32),
        compiler_params=_cparams(("arbitrary",)), name=name,
    )(kidx, s, r)


def _pair_swap_plan(ins, outs, ssem, rsem):
    xi, yi, ci = _pos()
    sends = [_remote(ins[a], outs[a], ssem.at[a], rsem.at[a], (xi, yi, 1 - ci)) for a in range(len(ins))]

    def start():
        for cp in sends:
            cp.start()

    def finish():
        for cp in sends:
            cp.wait_recv()
        for cp in sends:
            cp.wait_send()

    return start, finish


def _pair_swap(hs, name):
    n = len(hs)

    def body(*refs):
        start, finish = _pair_swap_plan(refs[:n], refs[n:2 * n], *refs[2 * n:])
        start()
        finish()

    return pl.pallas_call(
        body, in_specs=[_HBM] * n, out_specs=[_HBM] * n,
        out_shape=[jax.ShapeDtypeStruct(h.shape, h.dtype) for h in hs],
        scratch_shapes=[pltpu.SemaphoreType.DMA((n,)), pltpu.SemaphoreType.DMA((n,))], name=name,
    )(*hs)


def _small_sum(every):
    def body(all_ref, sum_ref):
        tot = all_ref[0]
        for dev in range(1, 8):
            tot = tot + all_ref[dev]
        sum_ref[...] = tot

    return pl.pallas_call(
        body, in_specs=[_VMEM], out_specs=_VMEM, out_shape=jax.ShapeDtypeStruct(every.shape[1:], _F32),
        compiler_params=pltpu.CompilerParams(vmem_limit_bytes=_VMEM_LIMIT), name="small_sum",
    )(every)


def _adamw(w, g, m, v):
    m = _ADAM_B1 * m + (1.0 - _ADAM_B1) * g
    v = _ADAM_B2 * v + (1.0 - _ADAM_B2) * (g * g)
    m_hat = m / (1.0 - _ADAM_B1 ** _ADAM_STEP)
    v_hat = v / (1.0 - _ADAM_B2 ** _ADAM_STEP)
    return -_ADAM_LR * (m_hat / (jnp.sqrt(v_hat) + _ADAM_EPS) + _ADAM_WD * w), m, v


def _adamw_big(w, g_mine, g_theirs, m, v, cidx, name):
    r, c = w.shape
    tr = _row_tile(r // 2)
    nt = r // 2 // tr

    def body(c_ref, w_ref, gm_ref, gt_ref, m_ref, v_ref, g_ref, d_ref, mo_ref, vo_ref):
        g = jnp.where(pl.program_id(0) // nt == c_ref[0], gm_ref[...], gt_ref[...])
        g_ref[...] = g
        d_ref[...], mo_ref[...], vo_ref[...] = _adamw(w_ref[...], g, m_ref[...], v_ref[...])

    spec = pl.BlockSpec((tr, c), lambda i, cr: (i, 0))
    half = pl.BlockSpec((tr, c), lambda i, cr: (i % nt, 0))
    return pl.pallas_call(
        body, grid_spec=pltpu.PrefetchScalarGridSpec(
            num_scalar_prefetch=1, grid=(2 * nt,), in_specs=[spec, half, half, spec, spec], out_specs=[spec] * 4),
        out_shape=[jax.ShapeDtypeStruct((r, c), _F32)] * 4,
        compiler_params=_cparams(("arbitrary",)), name=name,
    )(cidx, w, g_mine, g_theirs, m, v)


def _adamw_small(ws, gs, ms, vs):
    n = len(ws)
    summed = [i for i in range(n) if gs[i].shape != ws[i].shape]

    def body(*refs):
        w_r, g_r, m_r, v_r = (refs[i * n:(i + 1) * n] for i in range(4))
        outs = refs[4 * n:]
        for i in range(n):
            g = g_r[i][...]
            if i in summed:
                g = _rowsum(g)
                outs[3 * n + summed.index(i)][...] = g
            outs[i][...], outs[n + i][...], outs[2 * n + i][...] = _adamw(w_r[i][...], g, m_r[i][...], v_r[i][...])

    shapes = [jax.ShapeDtypeStruct(w.shape, _F32) for w in ws]
    res = pl.pallas_call(
        body, in_specs=[_VMEM] * (4 * n), out_specs=[_VMEM] * (3 * n + len(summed)),
        out_shape=shapes * 3 + [shapes[i] for i in summed],
        compiler_params=pltpu.CompilerParams(vmem_limit_bytes=_VMEM_LIMIT), name="adamw_small",
    )(*ws, *gs, *ms, *vs)
    gs = list(gs)
    for pos, i in enumerate(summed):
        gs[i] = res[3 * n + pos]
    return gs, res[:n], res[n:2 * n], res[2 * n:3 * n]


def _ada_bwd(c_all, dmod_k, w, m, v):
    d, nk = w.shape
    tn = 512 if nk % 512 == 0 else nk

    def body(c_ref, dm_ref, w_ref, m_ref, v_ref, g_ref, d_ref, mo_ref, vo_ref):
        ca = c_ref[...]
        g = _dot_tn(_mx(ca * jax.nn.sigmoid(ca)), _mx(dm_ref[...]))
        g_ref[...] = g
        d_ref[...], mo_ref[...], vo_ref[...] = _adamw(w_ref[...], g, m_ref[...], v_ref[...])

    col = pl.BlockSpec((d, tn), lambda j: (0, j))
    return pl.pallas_call(
        body, grid=(nk // tn,),
        in_specs=[pl.BlockSpec(c_all.shape, lambda j: (0, 0)), pl.BlockSpec((c_all.shape[0], tn), lambda j: (0, j)),
                  col, col, col],
        out_specs=[col] * 4, out_shape=[jax.ShapeDtypeStruct((d, nk), _F32)] * 4,
        compiler_params=_cparams(("arbitrary",)), name="ada_bwd",
    )(c_all, dmod_k, w, m, v)


def _block_diag(wh):
    hn, dh, _ = wh.shape
    eye = jnp.eye(hn, dtype=wh.dtype)
    return (eye[:, None, :, None] * wh[:, :, None, :]).reshape(hn * dh, hn * dh)


def _pack(pieces):
    out = []
    for p in pieces:
        flat = p.reshape(-1, 128)
        out.append(jnp.pad(flat, ((0, (-flat.shape[0]) % 8), (0, 0))))
    return jnp.concatenate(out, axis=0)


def _unpack(pack, shapes):
    out, off = [], 0
    for shp in shapes:
        rows = math.prod(shp) // 128
        out.append(pack[..., off:off + rows, :].reshape(pack.shape[:-2] + tuple(shp)))
        off += rows + (-rows) % 8
    return out


_WEIGHTS = ('w_ada', 'b_ada', 'w_in', 'lru_conv_w', 'lru_conv_b', 'lru_w_r', 'lru_b_r', 'lru_w_i', 'lru_b_i', 'lru_lambda',
            'conv_w', 'conv_b', 'conv_norm_g', 'conv_norm_b', 'w_out', 'ln1_g', 'ln1_b', 'ffn_w_up', 'ffn_conv_w',
            'ffn_conv_b', 'ffn_w_down', 'ln2_g', 'ln2_b')
_BIG = ('w_in', 'w_out', 'ffn_w_up', 'ffn_w_down')


def kernel(x, c, w_ada, b_ada, w_in, lru_conv_w, lru_conv_b, lru_w_r, lru_b_r, lru_w_i, lru_b_i, lru_lambda, conv_w, conv_b, conv_norm_g, conv_norm_b, w_out, ln1_g, ln1_b, ffn_w_up, ffn_conv_w, ffn_conv_b, ffn_w_down, ln2_g, ln2_b, loss_target, m_w_ada, m_b_ada, m_w_in, m_lru_conv_w, m_lru_conv_b, m_lru_w_r, m_lru_b_r, m_lru_w_i, m_lru_b_i, m_lru_lambda, m_conv_w, m_conv_b, m_conv_norm_g, m_conv_norm_b, m_w_out, m_ln1_g, m_ln1_b, m_ffn_w_up, m_ffn_conv_w, m_ffn_conv_b, m_ffn_w_down, m_ln2_g, m_ln2_b, v_w_ada, v_b_ada, v_w_in, v_lru_conv_w, v_lru_conv_b, v_lru_w_r, v_lru_b_r, v_lru_w_i, v_lru_b_i, v_lru_lambda, v_conv_w, v_conv_b, v_conv_norm_g, v_conv_norm_b, v_w_out, v_ln1_g, v_ln1_b, v_ffn_w_up, v_ffn_conv_w, v_ffn_conv_b, v_ffn_w_down, v_ln2_g, v_ln2_b):
    given = dict(locals())
    wt = {n: given[n] for n in _WEIGHTS}
    mo = {n: given["m_" + n] for n in _WEIGHTS}
    vo = {n: given["v_" + n] for n in _WEIGHTS}
    bl, s_len, d = x.shape
    wd = d // 2
    tokens = bl * s_len
    xi, yi, ci = _pos()
    kme = 2 * xi + yi
    kidx = jnp.reshape(kme, (1,)).astype(jnp.int32)
    cidx = jnp.reshape(ci, (1,)).astype(jnp.int32)

    nk = w_ada.shape[2]
    c8 = jnp.pad(c, ((0, 8 - bl), (0, 0)))
    c_all, mod8, win, wout_s, lcw_s, cw_s, fcw_s = _ada_fwd(
        c8, w_ada[0], lax.dynamic_slice(b_ada, (0, kme * nk), (1, nk)),
        [_mx(w_in[0]), _mx(w_out[0]), lru_conv_w[0], conv_w[0], ffn_conv_w[0]])
    mod3 = mod8[:bl].reshape(bl, 1, 6 * d)
    wout = wout_s.reshape(d, d)
    f = 4 * ffn_w_down.shape[1]
    unshard = lambda t: jnp.transpose(t, (1, 0, 2)).reshape(t.shape[1], -1)
    lcw, cw, fcw = unshard(lcw_s), unshard(cw_s), unshard(fcw_s)
    wr_bd, wi_bd = _mx(_block_diag(lru_w_r[0])), _mx(_block_diag(lru_w_i[0]))
    seg = _block_diag(jnp.ones((_N_HEADS, wd // _N_HEADS, wd // _N_HEADS), jnp.bfloat16))
    mixer_small = (lcw, lru_conv_b, wr_bd, wi_bd, lru_b_r, lru_b_i, lru_lambda, cw, conv_b, conv_norm_g, conv_norm_b, seg, wout, ln1_g)

    proj, h, mix, x1, u1, y, vbc, lru, wup, wdn_s = _mix_fwd(x, mod3, win, *mixer_small, ln1_b, [_mx(ffn_w_up[0]), _mx(ffn_w_down[0])])
    wdn = wdn_s.reshape(f, d)
    u2, hh, fact, gc_all, dz2, loss_acc, dln2, dgt2 = _ffn_fwd(x1, mod3, wup, fcw, ffn_conv_b, wdn, ln2_g, ln2_b, loss_target)
    dx1, dy2, dh, dfc, dmod2 = _ffn_bwd(dz2, x1, hh, gc_all, mod3, wup, wdn, fcw, ffn_conv_b)

    flat = lambda t: t.reshape(tokens, t.shape[-1])
    fc = wup.shape[2]
    g_up = _wgrad(flat(u2), flat(dh), d, fc, 1, 4, 0, 0, "wgrad_up")
    g_dn, r_up = _wgrad(flat(fact), flat(dy2), fc, d, f // fc, 1, 0, 0, "wgrad_down",
                        exchange=([g_up], _pair_out_shapes([g_up]), _pair_plan, _pair_sems(1)))
    g_dn = g_dn.reshape(4, f // 4, d)
    s_up, r_dn = _pair_add(g_up, r_up, cidx, "grad_pair_add_ffn_w_up",
                           exchange=([g_dn], _pair_out_shapes([g_dn]), _pair_plan, _pair_sems(1)))
    ffn_sum = [s_up, _pair_add(g_dn, r_dn, cidx, "grad_pair_add_ffn_w_down")]
    grad_x, dproj, dmix, xcg, vecw, dlcw, dcw, dln1, dmod1, *ffn_recv = _mix_bwd(
        dx1, x, mix, proj, h, vbc, lru, mod3, win, *mixer_small, ffn_sum)
    g_ri = _wgrad(flat(xcg), flat(xcg), wd, wd, 1, 2, 0, 1, "wgrad_gates")
    dh_ = wd // _N_HEADS
    on_diagonal = jnp.eye(_N_HEADS, dtype=_F32)[None, :, None, :, None]
    g_ri = jnp.sum(g_ri.reshape(2, _N_HEADS, dh_, _N_HEADS, dh_) * on_diagonal, axis=3)

    dmod = jnp.concatenate([dmod1.reshape(bl, 3 * d), dmod2.reshape(bl, 2 * d), dgt2.reshape(bl, d)], axis=1)
    pieces = [vecw, dlcw, dcw, jnp.concatenate([dln1, dln2], axis=0), dfc, g_ri, loss_acc[:, 0:128],
              jnp.pad(dmod, ((0, 8 - bl), (0, 0)))]
    shapes = [p.shape for p in pieces]
    pack = _pack(pieces)
    g_in, every = _wgrad(flat(u1), flat(dproj), d, wd, 1, 4, 0, 0, "wgrad_in", exchange=(
        [pack], [jax.ShapeDtypeStruct((8,) + pack.shape, _F32)], _dev_gather_plan, _dev_gather_sems(1)))
    wire_shape = lambda t: [jax.ShapeDtypeStruct((3,) + t.shape[1:], t.dtype)]
    ffn_half = [_chip_add(s, r, kidx, "grad_chip_add_" + n) for s, r, n in zip(ffn_sum, ffn_recv, _BIG[2:])]
    same = lambda ts: [jax.ShapeDtypeStruct(t.shape, t.dtype) for t in ts]
    r_in, = _pair_exchange([g_in], "grad_pair_exchange_w_in")
    s_in, wire_in, *ffn_theirs = _pair_add(g_in, r_in, cidx, "grad_pair_add_w_in", jnp.bfloat16,
                                           exchange=(ffn_half, same(ffn_half), _pair_swap_plan, _pair_sems(2)))
    g_out, recv_in = _wgrad(flat(y), flat(dmix), d, d, 1, 1, 0, 0, "wgrad_out", exchange=(
        [wire_in], wire_shape(wire_in), _chip_reduce_plan, _chip_reduce_sems(1)))
    g_out = g_out.reshape(4, d // 4, d)
    r_out, = _pair_exchange([g_out], "grad_pair_exchange_w_out")
    s_out, wire_out = _pair_add(g_out, r_out, cidx, "grad_pair_add_w_out", jnp.bfloat16)
    recv_out, = _chip_exchange([wire_out])
    mix_half = [_chip_add(s, r, kidx, "grad_chip_add_" + n) for s, r, n in zip([s_in, s_out], [recv_in, recv_out], _BIG)]
    half, other = mix_half + ffn_half, list(_pair_swap(mix_half, "grad_pair_swap")) + ffn_theirs
    grads, deltas, new_m, new_v = {}, {}, {}, {}
    for n, mine, theirs in zip(_BIG, half, other):
        g, dl, mm, vv = _adamw_big(wt[n][0], mine, theirs, mo[n][0], vo[n][0], cidx, "adamw_" + n)
        grads[n], deltas[n], new_m[n], new_v[n] = g[None], dl[None], mm[None], vv[None]

    vecw, dlcw, dcw, dln, dfc, g_ri, loss_sum, dmod_sum = _unpack(_small_sum(every), shapes)
    loss = 0.5 * loss_sum[0, 0] / d
    dmod_all = _unpack(every, shapes)[-1].reshape(64, 6 * d)

    g_ada, dl, mm, vv = _ada_bwd(c_all, lax.dynamic_slice(dmod_all, (0, kme * nk), (64, nk)), w_ada[0], m_w_ada[0], v_w_ada[0])
    grads['w_ada'], deltas['w_ada'], new_m['w_ada'], new_v['w_ada'] = g_ada[None], dl[None], mm[None], vv[None]

    shard = lambda t, width: lax.dynamic_slice(t, (0, kme * width), (t.shape[0], width))
    small = {
        'b_ada': dmod_sum, 'lru_conv_w': shard(dlcw, wd // 4), 'lru_conv_b': vecw[0:1], 'lru_w_r': g_ri[0], 'lru_b_r': vecw[1:2],
        'lru_w_i': g_ri[1], 'lru_b_i': vecw[2:3], 'lru_lambda': vecw[3:4], 'conv_w': shard(dcw, wd // 4), 'conv_b': vecw[4:5],
        'conv_norm_g': vecw[5:6], 'conv_norm_b': vecw[6:7], 'ln1_g': dln[0:1], 'ln1_b': dln[1:2],
        'ffn_conv_w': shard(dfc[0:3], f // 4), 'ffn_conv_b': dfc[3:4], 'ln2_g': dln[2:3], 'ln2_b': dln[3:4]}
    names = list(small)
    gs = [small[n] if n == 'b_ada' else small[n].reshape(wt[n].shape) for n in names]
    gs, dls, mms, vvs = _adamw_small([wt[n] for n in names], gs, [mo[n] for n in names], [vo[n] for n in names])
    for n, g, dl, mm, vv in zip(names, gs, dls, mms, vvs):
        grads[n], deltas[n], new_m[n], new_v[n] = g, dl, mm, vv

    return (loss, grad_x, *[grads[n] for n in _WEIGHTS], *[deltas[n] for n in _WEIGHTS],
            *[new_m[n] for n in _WEIGHTS], *[new_v[n] for n in _WEIGHTS])
```

```python
import functools
import itertools
import math

import jax
import jax.numpy as jnp
from jax import lax
from jax.experimental import pallas as pl
from jax.experimental.pallas import tpu as pltpu

_MXU_DT = jnp.bfloat16
_F32 = jnp.float32
_VMEM_LIMIT = 56 * 1024 * 1024
_TT_MIX = 256
_TT_MIX_FWD = 512
_TT_FFN = 256
_TK_WGRAD = 2048
_HALO = 32

_LRU_C = 8.0
_LN_EPS = 1e-5
_N_HEADS = 8
_DEPTH = 1
_ALPHA = (2 * _DEPTH) ** 0.25
_ADAM_LR, _ADAM_B1, _ADAM_B2, _ADAM_EPS, _ADAM_WD, _ADAM_STEP = 0.001, 0.9, 0.999, 1e-08, 0.01, 10

_MESH = pl.DeviceIdType.MESH
_CHIP_DELTAS = ((1, 0), (0, 1), (1, 1))


def _cparams(sem):
    return pltpu.CompilerParams(dimension_semantics=sem, vmem_limit_bytes=_VMEM_LIMIT)


def _resident(shape):
    nd = len(shape)
    return pl.BlockSpec(shape, lambda *_: (0,) * nd, pipeline_mode=pl.Buffered(1))


def _dot(a, b):
    return jnp.dot(a, b, preferred_element_type=_F32)


def _dot_nt(a, b):
    return lax.dot_general(a, b, (((1,), (1,)), ((), ())), preferred_element_type=_F32)


def _dot_tn(a, b):
    return lax.dot_general(a, b, (((0,), (0,)), ((), ())), preferred_element_type=_F32)


def _mx(v):
    return v.astype(_MXU_DT)


def _expm1(v):
    series = v * (1.0 + v * (1.0 / 2 + v * (1.0 / 6 + v * (1.0 / 24 + v * (1.0 / 120)))))
    return jnp.where(jnp.abs(v) < 0.0625, series, jnp.exp(v) - 1.0)


def _softplus(z):
    e = jnp.exp(-jnp.abs(z))
    u = 1.0 + e
    log1p = jnp.where(u == 1.0, e, jnp.log(u) * e / jnp.where(u == 1.0, 1.0, u - 1.0))
    return jnp.maximum(z, 0.0) + log1p


_GELU_C = math.sqrt(2.0 / math.pi)


def _gelu_and_grad(v):
    t = jnp.tanh(_GELU_C * (v + 0.044715 * v * v * v))
    val = 0.5 * v * (1.0 + t)
    grad = 0.5 * (1.0 + t) + 0.5 * v * (1.0 - t * t) * _GELU_C * (1.0 + 3 * 0.044715 * v * v)
    return val, grad


def _seg_sum(v, seg, passes=3):
    hi = v.astype(jnp.bfloat16)
    r1 = v - hi.astype(_F32)
    mid = r1.astype(jnp.bfloat16)
    out = _dot(hi, seg) + _dot(mid, seg)
    if passes == 3:
        out = out + _dot((r1 - mid.astype(_F32)).astype(jnp.bfloat16), seg)
    return out


_SCAN_BLOCK = 32


def _scan_fwd(a, u, h0):
    n = a.shape[0]
    blk = min(_SCAN_BLOCK, n)
    sub = lax.broadcasted_iota(jnp.int32, a.shape, 0) % blk
    h, d = u, 1
    while d < blk:
        keep = sub >= d
        h = a * jnp.where(keep, pltpu.roll(h, d, 0), 0.0) + h
        a = a * jnp.where(keep, pltpu.roll(a, d, 0), 1.0)
        d *= 2
    out, carry = [], h0
    for b in range(n // blk):
        rows = slice(b * blk, (b + 1) * blk)
        out.append(h[rows] + a[rows] * carry)
        carry = out[-1][blk - 1:blk, :]
    return jnp.concatenate(out, axis=0)


def _scan_rev(c, g, g_end):
    n = c.shape[0]
    blk = min(_SCAN_BLOCK, n)
    sub = lax.broadcasted_iota(jnp.int32, c.shape, 0) % blk
    d = 1
    while d < blk:
        keep = sub < blk - d
        g = c * jnp.where(keep, pltpu.roll(g, n - d, 0), 0.0) + g
        c = c * jnp.where(keep, pltpu.roll(c, n - d, 0), 1.0)
        d *= 2
    out, carry = [None] * (n // blk), g_end
    for b in reversed(range(n // blk)):
        rows = slice(b * blk, (b + 1) * blk)
        out[b] = g[rows] + c[rows] * carry
        carry = out[b][0:1, :]
    return jnp.concatenate(out, axis=0)


def _layer_norm_stats(z):
    mu = jnp.mean(z, axis=-1, keepdims=True)
    zc = z - mu
    var = jnp.mean(zc * zc, axis=-1, keepdims=True)
    rstd = lax.rsqrt(var + _LN_EPS)
    return zc * rstd, rstd


def _layer_norm_bwd(dn, n, rstd):
    return rstd * (dn - jnp.mean(dn, axis=-1, keepdims=True) - n * jnp.mean(dn * n, axis=-1, keepdims=True))


def _rowsum(v):
    return jnp.sum(v, axis=0, keepdims=True)


def _rowsum_halving(v):
    n = v.shape[0]
    while n > 8 and n % 16 == 0:
        v = v[:n // 2] + v[n // 2:]
        n //= 2
    return jnp.sum(v, axis=0, keepdims=True)


def _fused_exchange(body, n_in, n_out, n_scratch, n_xin, n_xout, plan, grid):
    def wrapped(*refs):
        o0 = n_in + n_xin
        s0 = o0 + n_out + n_xout
        start, finish = plan(refs[n_in:o0], refs[o0 + n_out:s0], *refs[s0 + n_scratch:])
        step = 0
        for axis, size in enumerate(grid):
            step = step * size + pl.program_id(axis)

        @pl.when(step == 0)
        def _():
            start()

        body(*refs[:n_in], *refs[o0:o0 + n_out], *refs[s0:s0 + n_scratch])

        @pl.when(step == math.prod(grid) - 1)
        def _():
            finish()

    return wrapped


def _lru_gates(xc, wr_ref, wi_ref, br_ref, bi_ref, lam_ref):
    xcb = _mx(xc)
    r = jax.nn.sigmoid(_dot(xcb, wr_ref[...]) + br_ref[...])
    i = jax.nn.sigmoid(_dot(xcb, wi_ref[...]) + bi_ref[...])
    sp = _softplus(-lam_ref[...])
    log_a = -_LRU_C * r * sp
    a = jnp.exp(log_a)
    mult = jnp.sqrt(-_expm1(2.0 * log_a))
    return r, i, sp, a, mult


def _conv_taps(ext_ref, w_ref, first, n_taps, tt):
    acc = w_ref[0:1, :] * ext_ref[pl.ds(first, tt), :]
    for k in range(1, n_taps):
        acc = acc + w_ref[k:k + 1, :] * ext_ref[pl.ds(first + k, tt), :]
    return acc


def _make_shifted(ext_ref, sh_ref):
    n = sh_ref.shape[1]
    for r in range(1, 8):
        sh_ref[r - 1] = ext_ref[pl.ds(r, n), :]


def _tap(ext_ref, sh_ref, off, tt):
    base = (off // 8) * 8
    if off % 8 == 0:
        return ext_ref[pl.ds(base, tt), :]
    return sh_ref[off % 8 - 1, pl.ds(base, tt), :]


def _conv_taps_shifted(ext_ref, sh_ref, w_ref, first, n_taps, tt):
    acc = w_ref[0:1, :] * _tap(ext_ref, sh_ref, first, tt)
    for k in range(1, n_taps):
        acc = acc + w_ref[k:k + 1, :] * _tap(ext_ref, sh_ref, first + k, tt)
    return acc


def _mix_fwd(x, mod3, win, lcw, lcb, wr_bd, wi_bd, b_r, b_i, lam, cw, cb, ng, nb, seg, wout, ln1g, ln1b, shards):
    bl, s_len, d = x.shape
    w = d // 2
    tt = min(_TT_MIX_FWD, s_len)
    ns = s_len // tt
    kc = cw.shape[0]

    def body(x_ref, mod_ref, win_ref, lcw_ref, lcb_ref, wr_ref, wi_ref, br_ref, bi_ref, lam_ref, cw_ref, cb_ref,
             ng_ref, nb_ref, seg_ref, wout_ref, g1_ref, b1_ref,
             proj_ref, h_ref, mix_ref, x1_ref, u1_ref, y_ref, vbc_ref, lru_ref, ext4, ext31, sh31, hcar):
        @pl.when(pl.program_id(1) == 0)
        def _():
            ext4[0:8, :] = jnp.zeros((8, w), _F32)
            ext31[0:_HALO, :] = jnp.zeros((_HALO, w), _F32)
            hcar[...] = jnp.zeros_like(hcar)

        xt = x_ref[...]
        sh1, sc1, gt1 = mod_ref[:, 0:d], mod_ref[:, d:2 * d], mod_ref[:, 2 * d:3 * d]
        u1 = _mx(xt * (1.0 + sc1) + sh1)
        u1_ref[...] = u1
        xa, ga, vb, gb = (_dot(u1, win_ref[k]) for k in range(4))
        proj_ref[:, 0:w] = xa
        proj_ref[:, w:2 * w] = ga
        proj_ref[:, 2 * w:3 * w] = vb
        proj_ref[:, 3 * w:4 * w] = gb

        ext4[8:8 + tt, :] = xa
        xc = lcb_ref[...] + _conv_taps(ext4, lcw_ref, 5, 4, tt)
        ext4[0:8, :] = xa[tt - 8:tt, :]
        r, i, sp, a, mult = _lru_gates(xc, wr_ref, wi_ref, br_ref, bi_ref, lam_ref)
        for k, val in enumerate((xc, r, i, a, mult)):
            lru_ref[:, k * w:(k + 1) * w] = val
        h = _scan_fwd(a, mult * (i * xc), hcar[0:1, :])
        hcar[0:1, :] = h[tt - 1:tt, :]
        h_ref[...] = h
        gelu, _ = _gelu_and_grad(ga)
        y_ref[:, 0:w] = _mx(gelu * h)

        vbg = vb * jax.nn.sigmoid(gb)
        ext31[_HALO:_HALO + tt, :] = vbg
        _make_shifted(ext31, sh31)
        vbc = cb_ref[...] + _conv_taps_shifted(ext31, sh31, cw_ref, _HALO - (kc - 1), kc, tt)
        vbc_ref[...] = vbc
        ext31[0:_HALO, :] = vbg[tt - _HALO:tt, :]
        inv = 1.0 / (w // _N_HEADS)
        zc = vbc - _seg_sum(vbc, seg_ref[...]) * inv
        n = zc * lax.rsqrt(_seg_sum(zc * zc, seg_ref[...], 2) * inv + _LN_EPS)
        pre = n * ng_ref[...] + nb_ref[...]
        y_ref[:, w:2 * w] = _mx(pre * jax.nn.sigmoid(pre))

        mix = _dot(y_ref[...], wout_ref[...])
        mix_ref[...] = mix
        n1, _ = _layer_norm_stats(_ALPHA * xt + (1.0 + gt1) * mix)
        x1_ref[...] = n1 * g1_ref[...] + b1_ref[...]

    tok = lambda c: pl.BlockSpec((None, tt, c), lambda b, s: (b, s, 0))
    smalls = [lcw, lcb, wr_bd, wi_bd, b_r, b_i, lam, cw, cb, ng, nb, seg, wout, ln1g, ln1b]
    nx = len(shards)
    return pl.pallas_call(
        _fused_exchange(body, 3 + len(smalls), 8, 4, nx, nx, _gather_plan, (bl, ns)), grid=(bl, ns),
        in_specs=[tok(d), pl.BlockSpec((None, 1, 6 * d), lambda b, s: (b, 0, 0)), _resident(win.shape)]
        + [_resident(t.shape) for t in smalls] + [_HBM] * nx,
        out_specs=[tok(4 * w), tok(w), tok(d), tok(d), tok(d), tok(d), tok(w), tok(5 * w)] + [_HBM] * nx,
        out_shape=[jax.ShapeDtypeStruct((bl, s_len, 4 * w), _F32), jax.ShapeDtypeStruct((bl, s_len, w), _F32),
                   jax.ShapeDtypeStruct((bl, s_len, d), _F32), jax.ShapeDtypeStruct((bl, s_len, d), _F32),
                   jax.ShapeDtypeStruct((bl, s_len, d), _MXU_DT), jax.ShapeDtypeStruct((bl, s_len, d), _MXU_DT),
                   jax.ShapeDtypeStruct((bl, s_len, w), _F32), jax.ShapeDtypeStruct((bl, s_len, 5 * w), _F32)]
        + [jax.ShapeDtypeStruct((4,) + t.shape, t.dtype) for t in shards],
        scratch_shapes=[pltpu.VMEM((tt + 8, w), _F32), pltpu.VMEM((tt + _HALO, w), _F32),
                        pltpu.VMEM((7, tt + _HALO - 8, w), _F32), pltpu.VMEM((8, w), _F32)] + _gather_sems(nx),
        compiler_params=_cparams(("arbitrary", "arbitrary")), name="mix_fwd",
    )(x, mod3, win, *smalls, *shards)


def _ffn_fwd(x1, mod3, wup, fcw, fcb, wdn, ln2g, ln2b, target):
    bl, s_len, d = x1.shape
    nch, _, fc = wup.shape
    nch //= 2
    f = nch * fc
    tt = min(_TT_FFN, s_len)
    ns = s_len // tt

    def body(x1_ref, mod_ref, wup_ref, fcw_ref, fcb_ref, wdn_ref, g2_ref, b2_ref, tgt_ref,
             u2_ref, hh_ref, f_ref, gc_ref, dz2_ref, loss_ref, dln2_ref, dgt2_ref, ext3):
        first_tile = pl.program_id(1) == 0

        @pl.when(first_tile)
        def _():
            ext3[:, 0:8, :] = jnp.zeros((nch, 8, fc), _F32)
            dgt2_ref[...] = jnp.zeros_like(dgt2_ref)

        @pl.when(first_tile & (pl.program_id(0) == 0))
        def _():
            loss_ref[...] = jnp.zeros_like(loss_ref)
            dln2_ref[...] = jnp.zeros_like(dln2_ref)

        x1t = x1_ref[...]
        sh2, sc2, gt2 = mod_ref[:, 3 * d:4 * d], mod_ref[:, 4 * d:5 * d], mod_ref[:, 5 * d:6 * d]
        u2 = _mx(x1t * (1.0 + sc2) + sh2)
        u2_ref[...] = u2
        y2 = jnp.zeros((tt, d), _F32)
        for j in range(nch):
            lanes = slice(j * fc, (j + 1) * fc)
            v = _dot(u2, wup_ref[j])
            g = _dot(u2, wup_ref[nch + j])
            hh_ref[:, lanes] = v.astype(hh_ref.dtype)
            hh_ref[:, f + j * fc:f + (j + 1) * fc] = g.astype(hh_ref.dtype)
            ext = ext3.at[j]
            ext[8:8 + tt, :] = g
            gc = fcb_ref[:, lanes] + sum(fcw_ref[k:k + 1, lanes] * ext[pl.ds(6 + k, tt), :] for k in range(3))
            gc_ref[:, lanes] = gc
            ext[0:8, :] = g[tt - 8:tt, :]
            fj = _mx(gc * jax.nn.sigmoid(gc) * v)
            f_ref[:, lanes] = fj
            y2 = y2 + _dot(fj, wdn_ref[lanes, :])

        n2, rstd = _layer_norm_stats(_ALPHA * x1t + (1.0 + gt2) * y2)
        err = n2 * g2_ref[...] + b2_ref[...] - tgt_ref[...]
        loss_ref[...] += jnp.sum(_rowsum(err * err), axis=1, keepdims=True)
        dout = err * (1.0 / d)
        dln2_ref[0:1, :] += _rowsum(dout * n2)
        dln2_ref[1:2, :] += _rowsum(dout)
        dz2 = _layer_norm_bwd(dout * g2_ref[...], n2, rstd)
        dz2_ref[...] = dz2
        dgt2_ref[...] += _rowsum(dz2 * y2)

    tok = lambda c: pl.BlockSpec((None, tt, c), lambda b, s: (b, s, 0))
    acc = lambda r: pl.BlockSpec((r, d), lambda b, s: (0, 0))
    smalls = [fcw, fcb, wdn, ln2g, ln2b]
    return pl.pallas_call(
        body, grid=(bl, ns),
        in_specs=[tok(d), pl.BlockSpec((None, 1, 6 * d), lambda b, s: (b, 0, 0)), _resident(wup.shape)]
        + [_resident(t.shape) for t in smalls] + [tok(d)],
        out_specs=[tok(d), tok(2 * f), tok(f), tok(f), tok(d), acc(1), acc(2), pl.BlockSpec((None, 1, d), lambda b, s: (b, 0, 0))],
        out_shape=[jax.ShapeDtypeStruct((bl, s_len, d), _MXU_DT), jax.ShapeDtypeStruct((bl, s_len, 2 * f), _F32),
                   jax.ShapeDtypeStruct((bl, s_len, f), _MXU_DT), jax.ShapeDtypeStruct((bl, s_len, f), _F32),
                   jax.ShapeDtypeStruct((bl, s_len, d), _F32), jax.ShapeDtypeStruct((1, d), _F32), jax.ShapeDtypeStruct((2, d), _F32),
                   jax.ShapeDtypeStruct((bl, 1, d), _F32)],
        scratch_shapes=[pltpu.VMEM((nch, tt + 8, fc), _F32)],
        compiler_params=_cparams(("arbitrary", "arbitrary")), name="ffn_fwd",
    )(x1, mod3, wup, *smalls, target)


def _ffn_bwd(dz2, x1, hh, gc_all, mod3, wup, wdn, fcw, fcb):
    bl, s_len, d = x1.shape
    nch, _, fc = wup.shape
    nch //= 2
    f = nch * fc
    tt = min(_TT_FFN, s_len)
    ns = s_len // tt

    def body(dz2_ref, x1_ref, hh_ref, gc_ref, mod_ref, wup_ref, wdn_ref, fcw_ref, fcb_ref,
             dx1_ref, dy2_ref, dh_ref, dfc_ref, dmod_ref, dext, dcar):
        @pl.when(pl.program_id(1) == 0)
        def _():
            dcar[...] = jnp.zeros_like(dcar)
            dmod_ref[...] = jnp.zeros_like(dmod_ref)

        @pl.when((pl.program_id(1) == 0) & (pl.program_id(0) == 0))
        def _():
            dfc_ref[...] = jnp.zeros_like(dfc_ref)

        sc2, gt2 = mod_ref[:, 4 * d:5 * d], mod_ref[:, 5 * d:6 * d]
        dz2t = dz2_ref[...]
        dy2 = _mx((1.0 + gt2) * dz2t)
        dy2_ref[...] = dy2
        du2 = jnp.zeros((tt, d), _F32)
        for j in range(nch):
            lanes = slice(j * fc, (j + 1) * fc)
            glanes = slice(f + j * fc, f + (j + 1) * fc)
            v = hh_ref[:, lanes].astype(_F32)
            g = hh_ref[:, glanes].astype(_F32)
            gc = gc_ref[:, lanes]
            sg = jax.nn.sigmoid(gc)
            df = _dot_nt(dy2, wdn_ref[lanes, :])
            dv = df * (gc * sg)
            dgc = df * v * (sg * (1.0 + gc * (1.0 - sg)))
            dfc_ref[3:4, lanes] += _rowsum_halving(dgc)
            dext[0:tt, :] = dgc
            dext[tt:tt + 8, :] = dcar[j]
            dcar[j] = dgc[0:8, :]
            dg = jnp.zeros((tt, fc), _F32)
            for k in range(3):
                shifted = dext[pl.ds(2 - k, tt), :]
                dg = dg + fcw_ref[k:k + 1, lanes] * shifted
                dfc_ref[k:k + 1, lanes] += _rowsum_halving(shifted * g)
            dvb, dgb = _mx(dv), _mx(dg)
            dh_ref[:, lanes] = dvb
            dh_ref[:, glanes] = dgb
            du2 = du2 + _dot_nt(dvb, wup_ref[j]) + _dot_nt(dgb, wup_ref[nch + j])

        dx1_ref[...] = _ALPHA * dz2t + du2 * (1.0 + sc2)
        dmod_ref[0:1, :] += _rowsum_halving(du2)
        dmod_ref[1:2, :] += _rowsum_halving(du2 * x1_ref[...])

    tok = lambda c: pl.BlockSpec((None, tt, c), lambda b, i: (b, ns - 1 - i, 0))
    return pl.pallas_call(
        body, grid=(bl, ns),
        in_specs=[tok(d), tok(d), tok(2 * f), tok(f), pl.BlockSpec((None, 1, 6 * d), lambda b, i: (b, 0, 0)),
                  _resident(wup.shape), _resident(wdn.shape), _resident(fcw.shape), _resident(fcb.shape)],
        out_specs=[tok(d), tok(d), tok(2 * f), pl.BlockSpec((4, f), lambda b, i: (0, 0)),
                   pl.BlockSpec((None, 2, d), lambda b, i: (b, 0, 0))],
        out_shape=[jax.ShapeDtypeStruct((bl, s_len, d), _F32), jax.ShapeDtypeStruct((bl, s_len, d), _MXU_DT),
                   jax.ShapeDtypeStruct((bl, s_len, 2 * f), _MXU_DT), jax.ShapeDtypeStruct((4, f), _F32),
                   jax.ShapeDtypeStruct((bl, 2, d), _F32)],
        scratch_shapes=[pltpu.VMEM((tt + 8, fc), _F32), pltpu.VMEM((nch, 8, fc), _F32)],
        compiler_params=_cparams(("arbitrary", "arbitrary")), name="ffn_bwd",
    )(dz2, x1, hh, gc_all, mod3, wup, wdn, fcw, fcb)


def _mix_bwd(dx1, x, mix, proj, h, vbc, lru, mod3, win, lcw, lcb, wr_bd, wi_bd, b_r, b_i, lam, cw, cb, ng, nb, seg, wout, ln1g, chip_sums):
    bl, s_len, d = x.shape
    w = d // 2
    tt = min(_TT_MIX, s_len)
    ns = s_len // tt
    kc = cw.shape[0]

    def body(dx1_ref, x_ref, mix_ref, proj_ref, phalo_ref, h_ref, hhalo_ref, vbc_ref, lru_ref, mod_ref, win_ref, lcw_ref, lcb_ref,
             wr_ref, wi_ref, br_ref, bi_ref, lam_ref, cw_ref, cb_ref, ng_ref, nb_ref, seg_ref, wout_ref, g1_ref,
             gx_ref, dproj_ref, dmix_ref, xcg_ref, vecw_ref, dlcw_ref, dcw_ref, dln1_ref, dmod_ref,
             ext4, ext31, dext4, dext31, sh31, dsh31, car4, car31, gcar):
        s = ns - 1 - pl.program_id(1)
        first = s == 0

        @pl.when(pl.program_id(1) == 0)
        def _():
            car4[...] = jnp.zeros_like(car4)
            car31[...] = jnp.zeros_like(car31)
            gcar[...] = jnp.zeros_like(gcar)
            dmod_ref[...] = jnp.zeros_like(dmod_ref)

        @pl.when((pl.program_id(1) == 0) & (pl.program_id(0) == 0))
        def _():
            for ref in (vecw_ref, dlcw_ref, dcw_ref, dln1_ref):
                ref[...] = jnp.zeros_like(ref)

        xt, mixt = x_ref[...], mix_ref[...]
        sh1, sc1, gt1 = mod_ref[:, 0:d], mod_ref[:, d:2 * d], mod_ref[:, 2 * d:3 * d]

        n1, rstd1 = _layer_norm_stats(_ALPHA * xt + (1.0 + gt1) * mixt)
        dx1t = dx1_ref[...]
        dln1_ref[0:1, :] += _rowsum(dx1t * n1)
        dln1_ref[1:2, :] += _rowsum(dx1t)
        dz1 = _layer_norm_bwd(dx1t * g1_ref[...], n1, rstd1)
        dmod_ref[2:3, :] += _rowsum(dz1 * mixt)
        dmix = _mx((1.0 + gt1) * dz1)
        dmix_ref[...] = dmix
        dya = _dot_nt(dmix, wout_ref[0:w, :])
        dyb = _dot_nt(dmix, wout_ref[w:2 * w, :])

        xa, ga = proj_ref[:, 0:w], proj_ref[:, w:2 * w]
        vb, gb = proj_ref[:, 2 * w:3 * w], proj_ref[:, 3 * w:4 * w]

        sgb = jax.nn.sigmoid(gb)
        vbg = vb * sgb
        hv, hg = phalo_ref[:, 2 * w:3 * w], phalo_ref[:, 3 * w:4 * w]
        ext31[0:_HALO, :] = jnp.where(first, 0.0, hv * jax.nn.sigmoid(hg))
        ext31[_HALO:_HALO + tt, :] = vbg
        _make_shifted(ext31, sh31)
        vbc = vbc_ref[...]
        inv = 1.0 / (w // _N_HEADS)
        zc = vbc - _seg_sum(vbc, seg_ref[...]) * inv
        rstd = lax.rsqrt(_seg_sum(zc * zc, seg_ref[...], 2) * inv + _LN_EPS)
        n = zc * rstd
        pre = n * ng_ref[...] + nb_ref[...]
        sgp = jax.nn.sigmoid(pre)
        dpre = dyb * (sgp * (1.0 + pre * (1.0 - sgp)))
        vecw_ref[5:6, :] += _rowsum(dpre * n)
        vecw_ref[6:7, :] += _rowsum(dpre)
        dn = dpre * ng_ref[...]
        dvbc = rstd * (dn - _seg_sum(dn, seg_ref[...], 2) * inv - n * (_seg_sum(dn * n, seg_ref[...], 2) * inv))
        vecw_ref[4:5, :] += _rowsum(dvbc)
        dext31[0:tt, :] = dvbc
        dext31[tt:tt + _HALO, :] = car31[...]
        car31[...] = dvbc[0:_HALO, :]
        _make_shifted(dext31, dsh31)
        dvbg = jnp.zeros((tt, w), _F32)
        for k in range(kc):
            dvbg = dvbg + cw_ref[k:k + 1, :] * _tap(dext31, dsh31, kc - 1 - k, tt)
            dcw_ref[k:k + 1, :] += _rowsum(dvbc * _tap(ext31, sh31, _HALO - (kc - 1) + k, tt))
        dproj_ref[:, 2 * w:3 * w] = _mx(dvbg * sgb)
        dproj_ref[:, 3 * w:4 * w] = _mx(dvbg * vb * (sgb * (1.0 - sgb)))

        ext4[0:8, :] = jnp.where(first, 0.0, phalo_ref[_HALO - 8:_HALO, 0:w])
        ext4[8:8 + tt, :] = xa
        xc, r, i, a, mult = (lru_ref[:, k * w:(k + 1) * w] for k in range(5))
        xcg_ref[:, 0:w] = _mx(xc)
        sp = _softplus(-lam_ref[...])
        ht = h_ref[...]
        row = lax.broadcasted_iota(jnp.int32, (tt, w), 0)
        h_before = jnp.where(first, 0.0, hhalo_ref[7:8, :])
        hprev = jnp.where(row == 0, h_before, pltpu.roll(ht, 1, 0))
        gelu, dgelu = _gelu_and_grad(ga)
        dproj_ref[:, w:2 * w] = _mx(dya * ht * dgelu)
        dh = dya * gelu
        coef = jnp.where(row == tt - 1, 1.0, pltpu.roll(a, tt - 1, 0))
        big_g = _scan_rev(coef, dh, gcar[0:1, :])
        gcar[0:1, :] = a[0:1, :] * big_g[0:1, :]
        da = big_g * hprev
        ixc = i * xc
        dlog_a = da * a - (big_g * ixc) * (a * a / mult)
        di = big_g * mult * xc
        dxc = big_g * mult * i
        vecw_ref[3:4, :] += _rowsum(dlog_a * r) * (_LRU_C * jax.nn.sigmoid(-lam_ref[...]))
        dgr_f = dlog_a * (-_LRU_C * sp) * (r * (1.0 - r))
        dgi_f = di * (i * (1.0 - i))
        vecw_ref[1:2, :] += _rowsum(dgr_f)
        vecw_ref[2:3, :] += _rowsum(dgi_f)
        dgr, dgi = _mx(dgr_f), _mx(dgi_f)
        xcg_ref[:, w:2 * w] = dgr
        xcg_ref[:, 2 * w:3 * w] = dgi
        dxc = dxc + _dot_nt(dgr, wr_ref[...]) + _dot_nt(dgi, wi_ref[...])
        vecw_ref[0:1, :] += _rowsum(dxc)
        dext4[0:tt, :] = dxc
        dext4[tt:tt + 8, :] = car4[...]
        car4[...] = dxc[0:8, :]
        dxa = jnp.zeros((tt, w), _F32)
        for k in range(4):
            dxa = dxa + lcw_ref[k:k + 1, :] * dext4[pl.ds(3 - k, tt), :]
            dlcw_ref[k:k + 1, :] += _rowsum(dxc * ext4[pl.ds(5 + k, tt), :])
        dproj_ref[:, 0:w] = _mx(dxa)

        du1 = sum(_dot_nt(dproj_ref[:, k * w:(k + 1) * w], win_ref[k]) for k in range(4))
        gx_ref[...] = _ALPHA * dz1 + du1 * (1.0 + sc1)
        dmod_ref[0:1, :] += _rowsum(du1)
        dmod_ref[1:2, :] += _rowsum(du1 * xt)

    tok = lambda c: pl.BlockSpec((None, tt, c), lambda b, i: (b, ns - 1 - i, 0))
    halo = lambda rows, c: pl.BlockSpec(
        (None, rows, c), lambda b, i: (b, jnp.maximum((ns - 1 - i) * (tt // rows) - 1, 0), 0))
    accw = lambda r, c: pl.BlockSpec((r, c), lambda b, i: (0, 0))
    smalls = [lcw, lcb, wr_bd, wi_bd, b_r, b_i, lam, cw, cb, ng, nb, seg, wout, ln1g]
    nx = len(chip_sums)
    return pl.pallas_call(
        _fused_exchange(body, 11 + len(smalls), 9, 9, nx, nx, _chip_reduce_plan, (bl, ns)), grid=(bl, ns),
        in_specs=[tok(d), tok(d), tok(d), tok(4 * w), halo(_HALO, 4 * w), tok(w), halo(8, w), tok(w), tok(5 * w),
                  pl.BlockSpec((None, 1, 6 * d), lambda b, i: (b, 0, 0)), _resident(win.shape)]
        + [_resident(t.shape) for t in smalls] + [_HBM] * nx,
        out_specs=[tok(d), tok(4 * w), tok(d), tok(3 * w), accw(8, w), accw(4, w), accw(kc, w), accw(2, d),
                   pl.BlockSpec((None, 3, d), lambda b, i: (b, 0, 0))] + [_HBM] * nx,
        out_shape=[jax.ShapeDtypeStruct((bl, s_len, d), _F32), jax.ShapeDtypeStruct((bl, s_len, 4 * w), _MXU_DT),
                   jax.ShapeDtypeStruct((bl, s_len, d), _MXU_DT), jax.ShapeDtypeStruct((bl, s_len, 3 * w), _MXU_DT),
                   jax.ShapeDtypeStruct((8, w), _F32), jax.ShapeDtypeStruct((4, w), _F32),
                   jax.ShapeDtypeStruct((kc, w), _F32), jax.ShapeDtypeStruct((2, d), _F32),
                   jax.ShapeDtypeStruct((bl, 3, d), _F32)]
        + [jax.ShapeDtypeStruct((3,) + t.shape[1:], t.dtype) for t in chip_sums],
        scratch_shapes=[pltpu.VMEM((tt + 8, w), _F32), pltpu.VMEM((tt + _HALO, w), _F32),
                        pltpu.VMEM((tt + 8, w), _F32), pltpu.VMEM((tt + _HALO, w), _F32),
                        pltpu.VMEM((7, tt + _HALO - 8, w), _F32), pltpu.VMEM((7, tt + _HALO - 8, w), _F32),
                        pltpu.VMEM((8, w), _F32), pltpu.VMEM((_HALO, w), _F32), pltpu.VMEM((8, w), _F32)]
        + _chip_reduce_sems(nx),
        compiler_params=_cparams(("arbitrary", "arbitrary")), name="mix_bwd",
    )(dx1, x, mix, proj, proj, h, h, vbc, lru, mod3, win, *smalls, *chip_sums)


def _wgrad(a, b, ma, nbw, na, nb, a_off, b_off, name, exchange=None):
    t = a.shape[0]
    tk = min(_TK_WGRAD, t)
    grid = (na * nb, t // tk)

    def body(a_ref, b_ref, o_ref):
        @pl.when(pl.program_id(1) == 0)
        def _():
            o_ref[...] = jnp.zeros_like(o_ref)
        o_ref[...] += _dot_tn(a_ref[...], b_ref[...])

    xin, xshapes, plan, sems = exchange if exchange else ([], [], None, [])
    nx = len(xin)
    res = pl.pallas_call(
        _fused_exchange(body, 2, 1, 0, nx, len(xshapes), plan, grid) if exchange else body, grid=grid,
        in_specs=[pl.BlockSpec((tk, ma), lambda j, k: (k, j // nb + a_off)),
                  pl.BlockSpec((tk, nbw), lambda j, k: (k, j % nb + b_off))] + [_HBM] * nx,
        out_specs=[pl.BlockSpec((None, ma, nbw), lambda j, k: (j, 0, 0))] + [_HBM] * len(xshapes),
        out_shape=[jax.ShapeDtypeStruct((na * nb, ma, nbw), _F32)] + list(xshapes),
        scratch_shapes=list(sems),
        compiler_params=_cparams(("arbitrary", "arbitrary")), name=name,
    )(a, b, *xin)
    return res if exchange else res[0]


_DEV_DELTAS = tuple(dl for dl in itertools.product((0, 1), repeat=3) if any(dl))
_HBM = pl.BlockSpec(memory_space=pltpu.HBM)
_VMEM = pl.BlockSpec(memory_space=pltpu.VMEM)


def _pos():
    return lax.axis_index("x"), lax.axis_index("y"), lax.axis_index("c")


def _flip(v, delta):
    return 1 - v if delta else v


def _remote(src, dst, ssem, rsem, dev):
    return pltpu.make_async_remote_copy(src_ref=src, dst_ref=dst, send_sem=ssem, recv_sem=rsem,
                                        device_id=dev, device_id_type=_MESH)


def _rows(ref, idx, n):
    return ref.at[pl.ds(pl.multiple_of(idx * n, 8), n)]


def _ada_fwd(c8, w_ada_k, b_ada_k, shards):
    rows, d = c8.shape
    nk = w_ada_k.shape[1]
    n = len(shards)

    def body(*refs):
        c_ref, w_ref, b_ref = refs[:3]
        call_ref, mod_ref = refs[3 + n:5 + n]
        modloc, modrcv, s1, r1, s2, r2 = refs[5 + 2 * n:11 + 2 * n]
        gather_start, gather_finish = _gather_plan(refs[3:3 + n], refs[5 + n:5 + 2 * n], *refs[11 + 2 * n:14 + 2 * n],
                                                   fsem=refs[14 + 2 * n], frsem=refs[15 + 2 * n], bounce=refs[16 + 2 * n:])
        gather_start()
        xi, yi, ci = _pos()
        me, kme = 4 * xi + 2 * yi + ci, 2 * xi + yi
        call_ref[pl.ds(pl.multiple_of(me * rows, 8), rows), :] = c_ref[...]
        sends = []
        for p, (dx, dy, dc) in enumerate(_DEV_DELTAS):
            cp = _remote(c_ref, _rows(call_ref, me, rows), s1.at[p], r1.at[p], (_flip(xi, dx), _flip(yi, dy), _flip(ci, dc)))
            cp.start()
            sends.append(cp)
        for p, (dx, dy, dc) in enumerate(_DEV_DELTAS):
            src = 4 * _flip(xi, dx) + 2 * _flip(yi, dy) + _flip(ci, dc)
            _remote(c_ref, _rows(call_ref, src, rows), s1.at[p], r1.at[p], (xi, yi, ci)).wait_recv()
        for cp in sends:
            cp.wait_send()

        ca = call_ref[...]
        modloc[...] = _dot(_mx(ca * jax.nn.sigmoid(ca)), _mx(w_ref[...])) + b_ref[...]
        modrcv[kme] = modloc[pl.ds(pl.multiple_of(me * rows, 8), rows), :]
        sends = []
        for j, (dx, dy) in enumerate(_CHIP_DELTAS):
            tx, ty = _flip(xi, dx), _flip(yi, dy)
            cp = _remote(_rows(modloc, 4 * tx + 2 * ty + ci, rows), modrcv.at[kme], s2.at[j], r2.at[j], (tx, ty, ci))
            cp.start()
            sends.append(cp)
        for j, (dx, dy) in enumerate(_CHIP_DELTAS):
            ksrc = 2 * _flip(xi, dx) + _flip(yi, dy)
            _remote(_rows(modloc, me, rows), modrcv.at[ksrc], s2.at[j], r2.at[j], (xi, yi, ci)).wait_recv()
        for cp in sends:
            cp.wait_send()
        for j in range(4):
            mod_ref[:, j * nk:(j + 1) * nk] = modrcv[j]
        gather_finish()

    return pl.pallas_call(
        body, in_specs=[_VMEM, _VMEM, _VMEM] + [_HBM] * n, out_specs=[_VMEM, _VMEM] + [_HBM] * n,
        out_shape=[jax.ShapeDtypeStruct((8 * rows, d), _F32), jax.ShapeDtypeStruct((rows, 4 * nk), _F32)]
        + [jax.ShapeDtypeStruct((4,) + a.shape, a.dtype) for a in shards],
        scratch_shapes=[pltpu.VMEM((8 * rows, nk), _F32), pltpu.VMEM((4, rows, nk), _F32),
                        pltpu.SemaphoreType.DMA((7,)), pltpu.SemaphoreType.DMA((7,)),
                        pltpu.SemaphoreType.DMA((3,)), pltpu.SemaphoreType.DMA((3,))]
        + _gather_sems(n) + [pltpu.SemaphoreType.DMA((3, n)), pltpu.SemaphoreType.DMA((3, n))]
        + [pltpu.VMEM(a.shape, a.dtype) for a in shards],
        compiler_params=pltpu.CompilerParams(vmem_limit_bytes=_VMEM_LIMIT), name="ada_fwd",
    )(c8, w_ada_k, b_ada_k, *shards)


def _gather_sems(n):
    return [pltpu.SemaphoreType.DMA((3, n)), pltpu.SemaphoreType.DMA((3, n)), pltpu.SemaphoreType.DMA((n,))]


def _gather_plan(ins, outs, ssem, rsem, lsem, bounce=(), fsem=None, frsem=None):
    n = len(ins)
    xi, yi, ci = _pos()
    kme = 2 * xi + yi
    split = [fsem is not None and ins[a].shape[0] % 32 == 0 for a in range(n)]

    def half(ref, a, which):
        r2 = ins[a].shape[0] // 2
        return ref.at[pl.ds(pl.multiple_of(which * r2, 16), r2)]

    staged = [pltpu.make_async_copy(ins[a], bounce[a], lsem.at[a]) for a in range(len(bounce))]
    local = [pltpu.make_async_copy(bounce[a] if bounce else ins[a], outs[a].at[kme], lsem.at[a]) for a in range(n)]
    sends, recvs, forwards, handed = [], [], [], []
    for j, (dx, dy) in enumerate(_CHIP_DELTAS):
        tx, ty = _flip(xi, dx), _flip(yi, dy)
        for a in range(n):
            sems = (ssem.at[j, a], rsem.at[j, a])
            landing = outs[a].at[2 * tx + ty]
            if split[a]:
                sends.append(_remote(half(ins[a], a, ci), half(outs[a].at[kme], a, ci), *sems, (tx, ty, ci)))
                recvs.append(_remote(half(ins[a], a, ci), half(landing, a, ci), *sems, (xi, yi, ci)))
                fsems = (fsem.at[j, a], frsem.at[j, a])
                forwards.append(_remote(half(landing, a, ci), half(landing, a, ci), *fsems, (xi, yi, 1 - ci)))
                handed.append(_remote(half(ins[a], a, 1 - ci), half(landing, a, 1 - ci), *fsems, (xi, yi, ci)))
            else:
                sends.append(_remote(ins[a], outs[a].at[kme], *sems, (tx, ty, ci)))
                recvs.append(_remote(ins[a], landing, *sems, (xi, yi, ci)))
                forwards.append(None)

    def start():
        for cp in sends + staged:
            cp.start()
        for cp in staged:
            cp.wait()
        for cp in local:
            cp.start()

    def finish():
        for arrived, forward in zip(recvs, forwards):
            arrived.wait_recv()
            if forward is not None:
                forward.start()
        for cp in handed:
            cp.wait_recv()
        for cp in sends + [f for f in forwards if f is not None]:
            cp.wait_send()
        for cp in local:
            cp.wait()

    return start, finish


def _dev_gather_sems(n):
    return [pltpu.SemaphoreType.DMA((7, n)), pltpu.SemaphoreType.DMA((7, n)), pltpu.SemaphoreType.DMA((n,))]


def _dev_gather_plan(ins, outs, ssem, rsem, lsem):
    n = len(ins)
    xi, yi, ci = _pos()
    me = 4 * xi + 2 * yi + ci
    local = [pltpu.make_async_copy(ins[a], outs[a].at[me], lsem.at[a]) for a in range(n)]
    sends, recvs = [], []
    for p, (dx, dy, dc) in enumerate(_DEV_DELTAS):
        tx, ty, tc = _flip(xi, dx), _flip(yi, dy), _flip(ci, dc)
        for a in range(n):
            sends.append(_remote(ins[a], outs[a].at[me], ssem.at[p, a], rsem.at[p, a], (tx, ty, tc)))
            recvs.append(_remote(ins[a], outs[a].at[4 * tx + 2 * ty + tc], ssem.at[p, a], rsem.at[p, a], (xi, yi, ci)))

    def start():
        for cp in local + sends:
            cp.start()

    def finish():
        for cp in recvs:
            cp.wait_recv()
        for cp in sends:
            cp.wait_send()
        for cp in local:
            cp.wait()

    return start, finish


def _pair_sems(n):
    return [pltpu.SemaphoreType.DMA((n,)), pltpu.SemaphoreType.DMA((n,))]


def _pair_plan(ins, outs, ssem, rsem):
    xi, yi, ci = _pos()
    sends = []
    for a in range(len(ins)):
        r2 = ins[a].shape[1] // 2
        src = ins[a].at[:, pl.ds(pl.multiple_of((1 - ci) * r2, 8), r2), :]
        sends.append(_remote(src, outs[a], ssem.at[a], rsem.at[a], (xi, yi, 1 - ci)))

    def start():
        for cp in sends:
            cp.start()

    def finish():
        for cp in sends:
            cp.wait_recv()
        for cp in sends:
            cp.wait_send()

    return start, finish


def _chip_reduce_sems(n):
    return [pltpu.SemaphoreType.DMA((3, n)), pltpu.SemaphoreType.DMA((3, n))]


def _chip_reduce_plan(ins, outs, ssem, rsem):
    xi, yi, ci = _pos()
    sends = []
    for j, (dx, dy) in enumerate(_CHIP_DELTAS):
        tx, ty = _flip(xi, dx), _flip(yi, dy)
        sends += [_remote(ins[a].at[2 * tx + ty], outs[a].at[j], ssem.at[j, a], rsem.at[j, a], (tx, ty, ci))
                  for a in range(len(ins))]

    def start():
        for cp in sends:
            cp.start()

    def finish():
        for cp in sends:
            cp.wait_recv()
        for cp in sends:
            cp.wait_send()

    return start, finish


def _pair_exchange(gs, name):
    n = len(gs)

    def body(*refs):
        start, finish = _pair_plan(refs[:n], refs[n:2 * n], *refs[2 * n:])
        start()
        finish()

    return pl.pallas_call(
        body, in_specs=[_HBM] * n, out_specs=[_HBM] * n, out_shape=_pair_out_shapes(gs),
        scratch_shapes=_pair_sems(n), name=name,
    )(*gs)


def _pair_out_shapes(gs):
    return [jax.ShapeDtypeStruct((g.shape[0], g.shape[1] // 2, g.shape[2]), g.dtype) for g in gs]


def _row_tile(r):
    return max(t for t in range(8, min(r, 256) + 1, 8) if r % t == 0)


def _pair_add(g, r, cidx, name, wire_dtype=None, exchange=None):
    nk, r2, c = r.shape
    tr = _row_tile(r2)
    nt = r2 // tr
    xin, xshapes, plan, sems = exchange if exchange else ([], [], None, [])
    nx = len(xin)

    def body(c_ref, g_ref, r_ref, *o_refs):
        s = g_ref[...] + r_ref[...]
        for o_ref in o_refs:
            o_ref[...] = s.astype(o_ref.dtype)

    out_spec = pl.BlockSpec((None, tr, c), lambda k, i, cr: (k, i, 0))
    dtypes = [_F32] + ([wire_dtype] if wire_dtype else [])
    res = pl.pallas_call(
        _fused_exchange(body, 3, len(dtypes), 0, nx, len(xshapes), plan, (nk, nt)) if exchange else body,
        grid_spec=pltpu.PrefetchScalarGridSpec(
            num_scalar_prefetch=1, grid=(nk, nt),
            in_specs=[pl.BlockSpec((None, tr, c), lambda k, i, cr: (k, cr[0] * nt + i, 0)), out_spec] + [_HBM] * nx,
            out_specs=[out_spec] * len(dtypes) + [_HBM] * len(xshapes), scratch_shapes=list(sems)),
        out_shape=[jax.ShapeDtypeStruct(r.shape, dt) for dt in dtypes] + list(xshapes),
        compiler_params=_cparams(("arbitrary", "arbitrary")), name=name,
    )(cidx, g, r, *xin)
    return res if wire_dtype or exchange else res[0]


def _chip_exchange(ss):
    n = len(ss)

    def body(*refs):
        start, finish = _chip_reduce_plan(refs[:n], refs[n:2 * n], *refs[2 * n:])
        start()
        finish()

    return pl.pallas_call(
        body, in_specs=[_HBM] * n, out_specs=[_HBM] * n,
        out_shape=[jax.ShapeDtypeStruct((3,) + s.shape[1:], s.dtype) for s in ss],
        scratch_shapes=_chip_reduce_sems(n), name="grad_chip_exchange",
    )(*ss)


def _chip_add(s, r, kidx, name):
    _, r2, c = r.shape
    tr = _row_tile(r2)

    def body(k_ref, s_ref, r_ref, o_ref):
        o_ref[...] = ((s_ref[...] + r_ref[0].astype(_F32)) + r_ref[1].astype(_F32)) + r_ref[2].astype(_F32)

    return pl.pallas_call(
        body, grid_spec=pltpu.PrefetchScalarGridSpec(
            num_scalar_prefetch=1, grid=(r2 // tr,),
            in_specs=[pl.BlockSpec((None, tr, c), lambda i, kr: (kr[0], i, 0)),
                      pl.BlockSpec((3, tr, c), lambda i, kr: (0, i, 0))],
            out_specs=pl.BlockSpec((tr, c), lambda i, kr: (i, 0))),
        out_shape=jax.ShapeDtypeStruct((r2, c), _F32),
        compiler_params=_cparams(("arbitrary",)), name=name,
    )(kidx, s, r)


def _pair_swap_plan(ins, outs, ssem, rsem):
    xi, yi, ci = _pos()
    sends = [_remote(ins[a], outs[a], ssem.at[a], rsem.at[a], (xi, yi, 1 - ci)) for a in range(len(ins))]

    def start():
        for cp in sends:
            cp.start()

    def finish():
        for cp in sends:
            cp.wait_recv()
        for cp in sends:
            cp.wait_send()

    return start, finish


def _pair_swap(hs, name):
    n = len(hs)

    def body(*refs):
        start, finish = _pair_swap_plan(refs[:n], refs[n:2 * n], *refs[2 * n:])
        start()
        finish()

    return pl.pallas_call(
        body, in_specs=[_HBM] * n, out_specs=[_HBM] * n,
        out_shape=[jax.ShapeDtypeStruct(h.shape, h.dtype) for h in hs],
        scratch_shapes=[pltpu.SemaphoreType.DMA((n,)), pltpu.SemaphoreType.DMA((n,))], name=name,
    )(*hs)


def _small_sum(every):
    def body(all_ref, sum_ref):
        tot = all_ref[0]
        for dev in range(1, 8):
            tot = tot + all_ref[dev]
        sum_ref[...] = tot

    return pl.pallas_call(
        body, in_specs=[_VMEM], out_specs=_VMEM, out_shape=jax.ShapeDtypeStruct(every.shape[1:], _F32),
        compiler_params=pltpu.CompilerParams(vmem_limit_bytes=_VMEM_LIMIT), name="small_sum",
    )(every)


def _adamw(w, g, m, v):
    m = _ADAM_B1 * m + (1.0 - _ADAM_B1) * g
    v = _ADAM_B2 * v + (1.0 - _ADAM_B2) * (g * g)
    m_hat = m / (1.0 - _ADAM_B1 ** _ADAM_STEP)
    v_hat = v / (1.0 - _ADAM_B2 ** _ADAM_STEP)
    return -_ADAM_LR * (m_hat / (jnp.sqrt(v_hat) + _ADAM_EPS) + _ADAM_WD * w), m, v


def _adamw_big(w, g_mine, g_theirs, m, v, cidx, name):
    r, c = w.shape
    tr = _row_tile(r // 2)
    nt = r // 2 // tr

    def body(c_ref, w_ref, gm_ref, gt_ref, m_ref, v_ref, g_ref, d_ref, mo_ref, vo_ref):
        g = jnp.where(pl.program_id(0) // nt == c_ref[0], gm_ref[...], gt_ref[...])
        g_ref[...] = g
        d_ref[...], mo_ref[...], vo_ref[...] = _adamw(w_ref[...], g, m_ref[...], v_ref[...])

    spec = pl.BlockSpec((tr, c), lambda i, cr: (i, 0))
    half = pl.BlockSpec((tr, c), lambda i, cr: (i % nt, 0))
    return pl.pallas_call(
        body, grid_spec=pltpu.PrefetchScalarGridSpec(
            num_scalar_prefetch=1, grid=(2 * nt,), in_specs=[spec, half, half, spec, spec], out_specs=[spec] * 4),
        out_shape=[jax.ShapeDtypeStruct((r, c), _F32)] * 4,
        compiler_params=_cparams(("arbitrary",)), name=name,
    )(cidx, w, g_mine, g_theirs, m, v)


def _adamw_small(ws, gs, ms, vs):
    n = len(ws)
    summed = [i for i in range(n) if gs[i].shape != ws[i].shape]

    def body(*refs):
        w_r, g_r, m_r, v_r = (refs[i * n:(i + 1) * n] for i in range(4))
        outs = refs[4 * n:]
        for i in range(n):
            g = g_r[i][...]
            if i in summed:
                g = _rowsum(g)
                outs[3 * n + summed.index(i)][...] = g
            outs[i][...], outs[n + i][...], outs[2 * n + i][...] = _adamw(w_r[i][...], g, m_r[i][...], v_r[i][...])

    shapes = [jax.ShapeDtypeStruct(w.shape, _F32) for w in ws]
    res = pl.pallas_call(
        body, in_specs=[_VMEM] * (4 * n), out_specs=[_VMEM] * (3 * n + len(summed)),
        out_shape=shapes * 3 + [shapes[i] for i in summed],
        compiler_params=pltpu.CompilerParams(vmem_limit_bytes=_VMEM_LIMIT), name="adamw_small",
    )(*ws, *gs, *ms, *vs)
    gs = list(gs)
    for pos, i in enumerate(summed):
        gs[i] = res[3 * n + pos]
    return gs, res[:n], res[n:2 * n], res[2 * n:3 * n]


def _ada_bwd(c_all, dmod_k, w, m, v):
    d, nk = w.shape
    tn = 512 if nk % 512 == 0 else nk

    def body(c_ref, dm_ref, w_ref, m_ref, v_ref, g_ref, d_ref, mo_ref, vo_ref):
        ca = c_ref[...]
        g = _dot_tn(_mx(ca * jax.nn.sigmoid(ca)), _mx(dm_ref[...]))
        g_ref[...] = g
        d_ref[...], mo_ref[...], vo_ref[...] = _adamw(w_ref[...], g, m_ref[...], v_ref[...])

    col = pl.BlockSpec((d, tn), lambda j: (0, j))
    return pl.pallas_call(
        body, grid=(nk // tn,),
        in_specs=[pl.BlockSpec(c_all.shape, lambda j: (0, 0)), pl.BlockSpec((c_all.shape[0], tn), lambda j: (0, j)),
                  col, col, col],
        out_specs=[col] * 4, out_shape=[jax.ShapeDtypeStruct((d, nk), _F32)] * 4,
        compiler_params=_cparams(("arbitrary",)), name="ada_bwd",
    )(c_all, dmod_k, w, m, v)


def _block_diag(wh):
    hn, dh, _ = wh.shape
    eye = jnp.eye(hn, dtype=wh.dtype)
    return (eye[:, None, :, None] * wh[:, :, None, :]).reshape(hn * dh, hn * dh)


def _pack(pieces):
    out = []
    for p in pieces:
        flat = p.reshape(-1, 128)
        out.append(jnp.pad(flat, ((0, (-flat.shape[0]) % 8), (0, 0))))
    return jnp.concatenate(out, axis=0)


def _unpack(pack, shapes):
    out, off = [], 0
    for shp in shapes:
        rows = math.prod(shp) // 128
        out.append(pack[..., off:off + rows, :].reshape(pack.shape[:-2] + tuple(shp)))
        off += rows + (-rows) % 8
    return out


_WEIGHTS = ('w_ada', 'b_ada', 'w_in', 'lru_conv_w', 'lru_conv_b', 'lru_w_r', 'lru_b_r', 'lru_w_i', 'lru_b_i', 'lru_lambda',
            'conv_w', 'conv_b', 'conv_norm_g', 'conv_norm_b', 'w_out', 'ln1_g', 'ln1_b', 'ffn_w_up', 'ffn_conv_w',
            'ffn_conv_b', 'ffn_w_down', 'ln2_g', 'ln2_b')
_BIG = ('w_in', 'w_out', 'ffn_w_up', 'ffn_w_down')


def kernel(x, c, w_ada, b_ada, w_in, lru_conv_w, lru_conv_b, lru_w_r, lru_b_r, lru_w_i, lru_b_i, lru_lambda, conv_w, conv_b, conv_norm_g, conv_norm_b, w_out, ln1_g, ln1_b, ffn_w_up, ffn_conv_w, ffn_conv_b, ffn_w_down, ln2_g, ln2_b, loss_target, m_w_ada, m_b_ada, m_w_in, m_lru_conv_w, m_lru_conv_b, m_lru_w_r, m_lru_b_r, m_lru_w_i, m_lru_b_i, m_lru_lambda, m_conv_w, m_conv_b, m_conv_norm_g, m_conv_norm_b, m_w_out, m_ln1_g, m_ln1_b, m_ffn_w_up, m_ffn_conv_w, m_ffn_conv_b, m_ffn_w_down, m_ln2_g, m_ln2_b, v_w_ada, v_b_ada, v_w_in, v_lru_conv_w, v_lru_conv_b, v_lru_w_r, v_lru_b_r, v_lru_w_i, v_lru_b_i, v_lru_lambda, v_conv_w, v_conv_b, v_conv_norm_g, v_conv_norm_b, v_w_out, v_ln1_g, v_ln1_b, v_ffn_w_up, v_ffn_conv_w, v_ffn_conv_b, v_ffn_w_down, v_ln2_g, v_ln2_b):
    given = dict(locals())
    wt = {n: given[n] for n in _WEIGHTS}
    mo = {n: given["m_" + n] for n in _WEIGHTS}
    vo = {n: given["v_" + n] for n in _WEIGHTS}
    bl, s_len, d = x.shape
    wd = d // 2
    tokens = bl * s_len
    xi, yi, ci = _pos()
    kme = 2 * xi + yi
    kidx = jnp.reshape(kme, (1,)).astype(jnp.int32)
    cidx = jnp.reshape(ci, (1,)).astype(jnp.int32)

    nk = w_ada.shape[2]
    c8 = jnp.pad(c, ((0, 8 - bl), (0, 0)))
    c_all, mod8, win, wout_s, lcw_s, cw_s, fcw_s = _ada_fwd(
        c8, w_ada[0], lax.dynamic_slice(b_ada, (0, kme * nk), (1, nk)),
        [_mx(w_in[0]), _mx(w_out[0]), lru_conv_w[0], conv_w[0], ffn_conv_w[0]])
    mod3 = mod8[:bl].reshape(bl, 1, 6 * d)
    wout = wout_s.reshape(d, d)
    f = 4 * ffn_w_down.shape[1]
    unshard = lambda t: jnp.transpose(t, (1, 0, 2)).reshape(t.shape[1], -1)
    lcw, cw, fcw = unshard(lcw_s), unshard(cw_s), unshard(fcw_s)
    wr_bd, wi_bd = _mx(_block_diag(lru_w_r[0])), _mx(_block_diag(lru_w_i[0]))
    seg = _block_diag(jnp.ones((_N_HEADS, wd // _N_HEADS, wd // _N_HEADS), jnp.bfloat16))
    mixer_small = (lcw, lru_conv_b, wr_bd, wi_bd, lru_b_r, lru_b_i, lru_lambda, cw, conv_b, conv_norm_g, conv_norm_b, seg, wout, ln1_g)

    proj, h, mix, x1, u1, y, vbc, lru, wup, wdn_s = _mix_fwd(x, mod3, win, *mixer_small, ln1_b, [_mx(ffn_w_up[0]), _mx(ffn_w_down[0])])
    wdn = wdn_s.reshape(f, d)
    u2, hh, fact, gc_all, dz2, loss_acc, dln2, dgt2 = _ffn_fwd(x1, mod3, wup, fcw, ffn_conv_b, wdn, ln2_g, ln2_b, loss_target)
    dx1, dy2, dh, dfc, dmod2 = _ffn_bwd(dz2, x1, hh, gc_all, mod3, wup, wdn, fcw, ffn_conv_b)

    flat = lambda t: t.reshape(tokens, t.shape[-1])
    fc = wup.shape[2]
    g_up = _wgrad(flat(u2), flat(dh), d, fc, 1, 4, 0, 0, "wgrad_up")
    g_dn, r_up = _wgrad(flat(fact), flat(dy2), fc, d, f // fc, 1, 0, 0, "wgrad_down",
                        exchange=([g_up], _pair_out_shapes([g_up]), _pair_plan, _pair_sems(1)))
    g_dn = g_dn.reshape(4, f // 4, d)
    s_up, r_dn = _pair_add(g_up, r_up, cidx, "grad_pair_add_ffn_w_up",
                           exchange=([g_dn], _pair_out_shapes([g_dn]), _pair_plan, _pair_sems(1)))
    ffn_sum = [s_up, _pair_add(g_dn, r_dn, cidx, "grad_pair_add_ffn_w_down")]
    grad_x, dproj, dmix, xcg, vecw, dlcw, dcw, dln1, dmod1, *ffn_recv = _mix_bwd(
        dx1, x, mix, proj, h, vbc, lru, mod3, win, *mixer_small, ffn_sum)
    g_ri = _wgrad(flat(xcg), flat(xcg), wd, wd, 1, 2, 0, 1, "wgrad_gates")
    dh_ = wd // _N_HEADS
    on_diagonal = jnp.eye(_N_HEADS, dtype=_F32)[None, :, None, :, None]
    g_ri = jnp.sum(g_ri.reshape(2, _N_HEADS, dh_, _N_HEADS, dh_) * on_diagonal, axis=3)

    dmod = jnp.concatenate([dmod1.reshape(bl, 3 * d), dmod2.reshape(bl, 2 * d), dgt2.reshape(bl, d)], axis=1)
    pieces = [vecw, dlcw, dcw, jnp.concatenate([dln1, dln2], axis=0), dfc, g_ri, loss_acc[:, 0:128],
              jnp.pad(dmod, ((0, 8 - bl), (0, 0)))]
    shapes = [p.shape for p in pieces]
    pack = _pack(pieces)
    g_in, every = _wgrad(flat(u1), flat(dproj), d, wd, 1, 4, 0, 0, "wgrad_in", exchange=(
        [pack], [jax.ShapeDtypeStruct((8,) + pack.shape, _F32)], _dev_gather_plan, _dev_gather_sems(1)))
    wire_shape = lambda t: [jax.ShapeDtypeStruct((3,) + t.shape[1:], t.dtype)]
    ffn_half = [_chip_add(s, r, kidx, "grad_chip_add_" + n) for s, r, n in zip(ffn_sum, ffn_recv, _BIG[2:])]
    same = lambda ts: [jax.ShapeDtypeStruct(t.shape, t.dtype) for t in ts]
    r_in, = _pair_exchange([g_in], "grad_pair_exchange_w_in")
    s_in, wire_in, *ffn_theirs = _pair_add(g_in, r_in, cidx, "grad_pair_add_w_in", jnp.bfloat16,
                                           exchange=(ffn_half, same(ffn_half), _pair_swap_plan, _pair_sems(2)))
    g_out, recv_in = _wgrad(flat(y), flat(dmix), d, d, 1, 1, 0, 0, "wgrad_out", exchange=(
        [wire_in], wire_shape(wire_in), _chip_reduce_plan, _chip_reduce_sems(1)))
    g_out = g_out.reshape(4, d // 4, d)
    r_out, = _pair_exchange([g_out], "grad_pair_exchange_w_out")
    s_out, wire_out = _pair_add(g_out, r_out, cidx, "grad_pair_add_w_out", jnp.bfloat16)
    recv_out, = _chip_exchange([wire_out])
    mix_half = [_chip_add(s, r, kidx, "grad_chip_add_" + n) for s, r, n in zip([s_in, s_out], [recv_in, recv_out], _BIG)]
    half, other = mix_half + ffn_half, list(_pair_swap(mix_half, "grad_pair_swap")) + ffn_theirs
    grads, deltas, new_m, new_v = {}, {}, {}, {}
    for n, mine, theirs in zip(_BIG, half, other):
        g, dl, mm, vv = _adamw_big(wt[n][0], mine, theirs, mo[n][0], vo[n][0], cidx, "adamw_" + n)
        grads[n], deltas[n], new_m[n], new_v[n] = g[None], dl[None], mm[None], vv[None]

    vecw, dlcw, dcw, dln, dfc, g_ri, loss_sum, dmod_sum = _unpack(_small_sum(every), shapes)
    loss = 0.5 * loss_sum[0, 0] / d
    dmod_all = _unpack(every, shapes)[-1].reshape(64, 6 * d)

    g_ada, dl, mm, vv = _ada_bwd(c_all, lax.dynamic_slice(dmod_all, (0, kme * nk), (64, nk)), w_ada[0], m_w_ada[0], v_w_ada[0])
    grads['w_ada'], deltas['w_ada'], new_m['w_ada'], new_v['w_ada'] = g_ada[None], dl[None], mm[None], vv[None]

    shard = lambda t, width: lax.dynamic_slice(t, (0, kme * width), (t.shape[0], width))
    small = {
        'b_ada': dmod_sum, 'lru_conv_w': shard(dlcw, wd // 4), 'lru_conv_b': vecw[0:1], 'lru_w_r': g_ri[0], 'lru_b_r': vecw[1:2],
        'lru_w_i': g_ri[1], 'lru_b_i': vecw[2:3], 'lru_lambda': vecw[3:4], 'conv_w': shard(dcw, wd // 4), 'conv_b': vecw[4:5],
        'conv_norm_g': vecw[5:6], 'conv_norm_b': vecw[6:7], 'ln1_g': dln[0:1], 'ln1_b': dln[1:2],
        'ffn_conv_w': shard(dfc[0:3], f // 4), 'ffn_conv_b': dfc[3:4], 'ln2_g': dln[2:3], 'ln2_b': dln[3:4]}
    names = list(small)
    gs = [small[n] if n == 'b_ada' else small[n].reshape(wt[n].shape) for n in names]
    gs, dls, mms, vvs = _adamw_small([wt[n] for n in names], gs, [mo[n] for n in names], [vo[n] for n in names])
    for n, g, dl, mm, vv in zip(names, gs, dls, mms, vvs):
        grads[n], deltas[n], new_m[n], new_v[n] = g, dl, mm, vv

    return (loss, grad_x, *[grads[n] for n in _WEIGHTS], *[deltas[n] for n in _WEIGHTS],
            *[new_m[n] for n in _WEIGHTS], *[new_v[n] for n in _WEIGHTS])
```

```python
import functools
import itertools
import math

import jax
import jax.numpy as jnp
from jax import lax
from jax.experimental import pallas as pl
from jax.experimental.pallas import tpu as pltpu

_MXU_DT = jnp.bfloat16
_F32 = jnp.float32
_VMEM_LIMIT = 56 * 1024 * 1024
_TT_MIX = 256
_TT_MIX_FWD = 512
_TT_FFN = 256
_TK_WGRAD = 2048
_HALO = 32

_LRU_C = 8.0
_LN_EPS = 1e-5
_N_HEADS = 8
_DEPTH = 1
_ALPHA = (2 * _DEPTH) ** 0.25
_ADAM_LR, _ADAM_B1, _ADAM_B2, _ADAM_EPS, _ADAM_WD, _ADAM_STEP = 0.001, 0.9, 0.999, 1e-08, 0.01, 10

_MESH = pl.DeviceIdType.MESH
_CHIP_DELTAS = ((1, 0), (0, 1), (1, 1))


def _cparams(sem):
    return pltpu.CompilerParams(dimension_semantics=sem, vmem_limit_bytes=_VMEM_LIMIT)


def _resident(shape):
    nd = len(shape)
    return pl.BlockSpec(shape, lambda *_: (0,) * nd, pipeline_mode=pl.Buffered(1))


def _dot(a, b):
    return jnp.dot(a, b, preferred_element_type=_F32)


def _dot_nt(a, b):
    return lax.dot_general(a, b, (((1,), (1,)), ((), ())), preferred_element_type=_F32)


def _dot_tn(a, b):
    return lax.dot_general(a, b, (((0,), (0,)), ((), ())), preferred_element_type=_F32)


def _mx(v):
    return v.astype(_MXU_DT)


def _expm1(v):
    series = v * (1.0 + v * (1.0 / 2 + v * (1.0 / 6 + v * (1.0 / 24 + v * (1.0 / 120)))))
    return jnp.where(jnp.abs(v) < 0.0625, series, jnp.exp(v) - 1.0)


def _softplus(z):
    e = jnp.exp(-jnp.abs(z))
    u = 1.0 + e
    log1p = jnp.where(u == 1.0, e, jnp.log(u) * e / jnp.where(u == 1.0, 1.0, u - 1.0))
    return jnp.maximum(z, 0.0) + log1p


_GELU_C = math.sqrt(2.0 / math.pi)


def _gelu_and_grad(v):
    t = jnp.tanh(_GELU_C * (v + 0.044715 * v * v * v))
    val = 0.5 * v * (1.0 + t)
    grad = 0.5 * (1.0 + t) + 0.5 * v * (1.0 - t * t) * _GELU_C * (1.0 + 3 * 0.044715 * v * v)
    return val, grad


def _seg_sum(v, seg, passes=3):
    hi = v.astype(jnp.bfloat16)
    r1 = v - hi.astype(_F32)
    mid = r1.astype(jnp.bfloat16)
    out = _dot(hi, seg) + _dot(mid, seg)
    if passes == 3:
        out = out + _dot((r1 - mid.astype(_F32)).astype(jnp.bfloat16), seg)
    return out


_SCAN_BLOCK = 32


def _scan_fwd(a, u, h0):
    n = a.shape[0]
    blk = min(_SCAN_BLOCK, n)
    sub = lax.broadcasted_iota(jnp.int32, a.shape, 0) % blk
    h, d = u, 1
    while d < blk:
        keep = sub >= d
        h = a * jnp.where(keep, pltpu.roll(h, d, 0), 0.0) + h
        a = a * jnp.where(keep, pltpu.roll(a, d, 0), 1.0)
        d *= 2
    out, carry = [], h0
    for b in range(n // blk):
        rows = slice(b * blk, (b + 1) * blk)
        out.append(h[rows] + a[rows] * carry)
        carry = out[-1][blk - 1:blk, :]
    return jnp.concatenate(out, axis=0)


def _scan_rev(c, g, g_end):
    n = c.shape[0]
    blk = min(_SCAN_BLOCK, n)
    sub = lax.broadcasted_iota(jnp.int32, c.shape, 0) % blk
    d = 1
    while d < blk:
        keep = sub < blk - d
        g = c * jnp.where(keep, pltpu.roll(g, n - d, 0), 0.0) + g
        c = c * jnp.where(keep, pltpu.roll(c, n - d, 0), 1.0)
        d *= 2
    out, carry = [None] * (n // blk), g_end
    for b in reversed(range(n // blk)):
        rows = slice(b * blk, (b + 1) * blk)
        out[b] = g[rows] + c[rows] * carry
        carry = out[b][0:1, :]
    return jnp.concatenate(out, axis=0)


def _layer_norm_stats(z):
    mu = jnp.mean(z, axis=-1, keepdims=True)
    zc = z - mu
    var = jnp.mean(zc * zc, axis=-1, keepdims=True)
    rstd = lax.rsqrt(var + _LN_EPS)
    return zc * rstd, rstd


def _layer_norm_bwd(dn, n, rstd):
    return rstd * (dn - jnp.mean(dn, axis=-1, keepdims=True) - n * jnp.mean(dn * n, axis=-1, keepdims=True))


def _rowsum(v):
    return jnp.sum(v, axis=0, keepdims=True)


def _rowsum_halving(v):
    n = v.shape[0]
    while n > 8 and n % 16 == 0:
        v = v[:n // 2] + v[n // 2:]
        n //= 2
    return jnp.sum(v, axis=0, keepdims=True)


def _fused_exchange(body, n_in, n_out, n_scratch, n_xin, n_xout, plan, grid):
    def wrapped(*refs):
        o0 = n_in + n_xin
        s0 = o0 + n_out + n_xout
        start, finish = plan(refs[n_in:o0], refs[o0 + n_out:s0], *refs[s0 + n_scratch:])
        step = 0
        for axis, size in enumerate(grid):
            step = step * size + pl.program_id(axis)

        @pl.when(step == 0)
        def _():
            start()

        body(*refs[:n_in], *refs[o0:o0 + n_out], *refs[s0:s0 + n_scratch])

        @pl.when(step == math.prod(grid) - 1)
        def _():
            finish()

    return wrapped


def _lru_gates(xc, wr_ref, wi_ref, br_ref, bi_ref, lam_ref):
    xcb = _mx(xc)
    r = jax.nn.sigmoid(_dot(xcb, wr_ref[...]) + br_ref[...])
    i = jax.nn.sigmoid(_dot(xcb, wi_ref[...]) + bi_ref[...])
    sp = _softplus(-lam_ref[...])
    log_a = -_LRU_C * r * sp
    a = jnp.exp(log_a)
    mult = jnp.sqrt(-_expm1(2.0 * log_a))
    return r, i, sp, a, mult


def _conv_taps(ext_ref, w_ref, first, n_taps, tt):
    acc = w_ref[0:1, :] * ext_ref[pl.ds(first, tt), :]
    for k in range(1, n_taps):
        acc = acc + w_ref[k:k + 1, :] * ext_ref[pl.ds(first + k, tt), :]
    return acc


def _make_shifted(ext_ref, sh_ref):
    n = sh_ref.shape[1]
    for r in range(1, 8):
        sh_ref[r - 1] = ext_ref[pl.ds(r, n), :]


def _tap(ext_ref, sh_ref, off, tt):
    base = (off // 8) * 8
    if off % 8 == 0:
        return ext_ref[pl.ds(base, tt), :]
    return sh_ref[off % 8 - 1, pl.ds(base, tt), :]


def _conv_taps_shifted(ext_ref, sh_ref, w_ref, first, n_taps, tt):
    acc = w_ref[0:1, :] * _tap(ext_ref, sh_ref, first, tt)
    for k in range(1, n_taps):
        acc = acc + w_ref[k:k + 1, :] * _tap(ext_ref, sh_ref, first + k, tt)
    return acc


def _mix_fwd(x, mod3, win, lcw, lcb, wr_bd, wi_bd, b_r, b_i, lam, cw, cb, ng, nb, seg, wout, ln1g, ln1b, shards):
    bl, s_len, d = x.shape
    w = d // 2
    tt = min(_TT_MIX_FWD, s_len)
    ns = s_len // tt
    kc = cw.shape[0]

    def body(x_ref, mod_ref, win_ref, lcw_ref, lcb_ref, wr_ref, wi_ref, br_ref, bi_ref, lam_ref, cw_ref, cb_ref,
             ng_ref, nb_ref, seg_ref, wout_ref, g1_ref, b1_ref,
             proj_ref, h_ref, mix_ref, x1_ref, u1_ref, y_ref, vbc_ref, lru_ref, ext4, ext31, sh31, hcar):
        @pl.when(pl.program_id(1) == 0)
        def _():
            ext4[0:8, :] = jnp.zeros((8, w), _F32)
            ext31[0:_HALO, :] = jnp.zeros((_HALO, w), _F32)
            hcar[...] = jnp.zeros_like(hcar)

        xt = x_ref[...]
        sh1, sc1, gt1 = mod_ref[:, 0:d], mod_ref[:, d:2 * d], mod_ref[:, 2 * d:3 * d]
        u1 = _mx(xt * (1.0 + sc1) + sh1)
        u1_ref[...] = u1
        xa, ga, vb, gb = (_dot(u1, win_ref[k]) for k in range(4))
        proj_ref[:, 0:w] = xa
        proj_ref[:, w:2 * w] = ga
        proj_ref[:, 2 * w:3 * w] = vb
        proj_ref[:, 3 * w:4 * w] = gb

        ext4[8:8 + tt, :] = xa
        xc = lcb_ref[...] + _conv_taps(ext4, lcw_ref, 5, 4, tt)
        ext4[0:8, :] = xa[tt - 8:tt, :]
        r, i, sp, a, mult = _lru_gates(xc, wr_ref, wi_ref, br_ref, bi_ref, lam_ref)
        for k, val in enumerate((xc, r, i, a, mult)):
            lru_ref[:, k * w:(k + 1) * w] = val
        h = _scan_fwd(a, mult * (i * xc), hcar[0:1, :])
        hcar[0:1, :] = h[tt - 1:tt, :]
        h_ref[...] = h
        gelu, _ = _gelu_and_grad(ga)
        y_ref[:, 0:w] = _mx(gelu * h)

        vbg = vb * jax.nn.sigmoid(gb)
        ext31[_HALO:_HALO + tt, :] = vbg
        _make_shifted(ext31, sh31)
        vbc = cb_ref[...] + _conv_taps_shifted(ext31, sh31, cw_ref, _HALO - (kc - 1), kc, tt)
        vbc_ref[...] = vbc
        ext31[0:_HALO, :] = vbg[tt - _HALO:tt, :]
        inv = 1.0 / (w // _N_HEADS)
        zc = vbc - _seg_sum(vbc, seg_ref[...]) * inv
        n = zc * lax.rsqrt(_seg_sum(zc * zc, seg_ref[...], 2) * inv + _LN_EPS)
        pre = n * ng_ref[...] + nb_ref[...]
        y_ref[:, w:2 * w] = _mx(pre * jax.nn.sigmoid(pre))

        mix = _dot(y_ref[...], wout_ref[...])
        mix_ref[...] = mix
        n1, _ = _layer_norm_stats(_ALPHA * xt + (1.0 + gt1) * mix)
        x1_ref[...] = n1 * g1_ref[...] + b1_ref[...]

    tok = lambda c: pl.BlockSpec((None, tt, c), lambda b, s: (b, s, 0))
    smalls = [lcw, lcb, wr_bd, wi_bd, b_r, b_i, lam, cw, cb, ng, nb, seg, wout, ln1g, ln1b]
    nx = len(shards)
    return pl.pallas_call(
        _fused_exchange(body, 3 + len(smalls), 8, 4, nx, nx, _gather_plan, (bl, ns)), grid=(bl, ns),
        in_specs=[tok(d), pl.BlockSpec((None, 1, 6 * d), lambda b, s: (b, 0, 0)), _resident(win.shape)]
        + [_resident(t.shape) for t in smalls] + [_HBM] * nx,
        out_specs=[tok(4 * w), tok(w), tok(d), tok(d), tok(d), tok(d), tok(w), tok(5 * w)] + [_HBM] * nx,
        out_shape=[jax.ShapeDtypeStruct((bl, s_len, 4 * w), _F32), jax.ShapeDtypeStruct((bl, s_len, w), _F32),
                   jax.ShapeDtypeStruct((bl, s_len, d), _F32), jax.ShapeDtypeStruct((bl, s_len, d), _F32),
                   jax.ShapeDtypeStruct((bl, s_len, d), _MXU_DT), jax.ShapeDtypeStruct((bl, s_len, d), _MXU_DT),
                   jax.ShapeDtypeStruct((bl, s_len, w), _F32), jax.ShapeDtypeStruct((bl, s_len, 5 * w), _F32)]
        + [jax.ShapeDtypeStruct((4,) + t.shape, t.dtype) for t in shards],
        scratch_shapes=[pltpu.VMEM((tt + 8, w), _F32), pltpu.VMEM((tt + _HALO, w), _F32),
                        pltpu.VMEM((7, tt + _HALO - 8, w), _F32), pltpu.VMEM((8, w), _F32)] + _gather_sems(nx),
        compiler_params=_cparams(("arbitrary", "arbitrary")), name="mix_fwd",
    )(x, mod3, win, *smalls, *shards)


def _ffn_fwd(x1, mod3, wup, fcw, fcb, wdn, ln2g, ln2b, target):
    bl, s_len, d = x1.shape
    nch, _, fc = wup.shape
    nch //= 2
    f = nch * fc
    tt = min(_TT_FFN, s_len)
    ns = s_len // tt

    def body(x1_ref, mod_ref, wup_ref, fcw_ref, fcb_ref, wdn_ref, g2_ref, b2_ref, tgt_ref,
             u2_ref, hh_ref, f_ref, gc_ref, dz2_ref, loss_ref, dln2_ref, dgt2_ref, ext3):
        first_tile = pl.program_id(1) == 0

        @pl.when(first_tile)
        def _():
            ext3[:, 0:8, :] = jnp.zeros((nch, 8, fc), _F32)
            dgt2_ref[...] = jnp.zeros_like(dgt2_ref)

        @pl.when(first_tile & (pl.program_id(0) == 0))
        def _():
            loss_ref[...] = jnp.zeros_like(loss_ref)
            dln2_ref[...] = jnp.zeros_like(dln2_ref)

        x1t = x1_ref[...]
        sh2, sc2, gt2 = mod_ref[:, 3 * d:4 * d], mod_ref[:, 4 * d:5 * d], mod_ref[:, 5 * d:6 * d]
        u2 = _mx(x1t * (1.0 + sc2) + sh2)
        u2_ref[...] = u2
        y2 = jnp.zeros((tt, d), _F32)
        for j in range(nch):
            lanes = slice(j * fc, (j + 1) * fc)
            v = _dot(u2, wup_ref[j])
            g = _dot(u2, wup_ref[nch + j])
            hh_ref[:, lanes] = v.astype(hh_ref.dtype)
            hh_ref[:, f + j * fc:f + (j + 1) * fc] = g.astype(hh_ref.dtype)
            ext = ext3.at[j]
            ext[8:8 + tt, :] = g
            gc = fcb_ref[:, lanes] + sum(fcw_ref[k:k + 1, lanes] * ext[pl.ds(6 + k, tt), :] for k in range(3))
            gc_ref[:, lanes] = gc
            ext[0:8, :] = g[tt - 8:tt, :]
            fj = _mx(gc * jax.nn.sigmoid(gc) * v)
            f_ref[:, lanes] = fj
            y2 = y2 + _dot(fj, wdn_ref[lanes, :])

        n2, rstd = _layer_norm_stats(_ALPHA * x1t + (1.0 + gt2) * y2)
        err = n2 * g2_ref[...] + b2_ref[...] - tgt_ref[...]
        loss_ref[...] += jnp.sum(_rowsum(err * err), axis=1, keepdims=True)
        dout = err * (1.0 / d)
        dln2_ref[0:1, :] += _rowsum(dout * n2)
        dln2_ref[1:2, :] += _rowsum(dout)
        dz2 = _layer_norm_bwd(dout * g2_ref[...], n2, rstd)
        dz2_ref[...] = dz2
        dgt2_ref[...] += _rowsum(dz2 * y2)

    tok = lambda c: pl.BlockSpec((None, tt, c), lambda b, s: (b, s, 0))
    acc = lambda r: pl.BlockSpec((r, d), lambda b, s: (0, 0))
    smalls = [fcw, fcb, wdn, ln2g, ln2b]
    return pl.pallas_call(
        body, grid=(bl, ns),
        in_specs=[tok(d), pl.BlockSpec((None, 1, 6 * d), lambda b, s: (b, 0, 0)), _resident(wup.shape)]
        + [_resident(t.shape) for t in smalls] + [tok(d)],
        out_specs=[tok(d), tok(2 * f), tok(f), tok(f), tok(d), acc(1), acc(2), pl.BlockSpec((None, 1, d), lambda b, s: (b, 0, 0))],
        out_shape=[jax.ShapeDtypeStruct((bl, s_len, d), _MXU_DT), jax.ShapeDtypeStruct((bl, s_len, 2 * f), _F32),
                   jax.ShapeDtypeStruct((bl, s_len, f), _MXU_DT), jax.ShapeDtypeStruct((bl, s_len, f), _F32),
                   jax.ShapeDtypeStruct((bl, s_len, d), _F32), jax.ShapeDtypeStruct((1, d), _F32), jax.ShapeDtypeStruct((2, d), _F32),
                   jax.ShapeDtypeStruct((bl, 1, d), _F32)],
        scratch_shapes=[pltpu.VMEM((nch, tt + 8, fc), _F32)],
        compiler_params=_cparams(("arbitrary", "arbitrary")), name="ffn_fwd",
    )(x1, mod3, wup, *smalls, target)


def _ffn_bwd(dz2, x1, hh, gc_all, mod3, wup, wdn, fcw, fcb):
    bl, s_len, d = x1.shape
    nch, _, fc = wup.shape
    nch //= 2
    f = nch * fc
    tt = min(_TT_FFN, s_len)
    ns = s_len // tt

    def body(dz2_ref, x1_ref, hh_ref, gc_ref, mod_ref, wup_ref, wdn_ref, fcw_ref, fcb_ref,
             dx1_ref, dy2_ref, dh_ref, dfc_ref, dmod_ref, dext, dcar):
        @pl.when(pl.program_id(1) == 0)
        def _():
            dcar[...] = jnp.zeros_like(dcar)
            dmod_ref[...] = jnp.zeros_like(dmod_ref)

        @pl.when((pl.program_id(1) == 0) & (pl.program_id(0) == 0))
        def _():
            dfc_ref[...] = jnp.zeros_like(dfc_ref)

        sc2, gt2 = mod_ref[:, 4 * d:5 * d], mod_ref[:, 5 * d:6 * d]
        dz2t = dz2_ref[...]
        dy2 = _mx((1.0 + gt2) * dz2t)
        dy2_ref[...] = dy2
        du2 = jnp.zeros((tt, d), _F32)
        for j in range(nch):
            lanes = slice(j * fc, (j + 1) * fc)
            glanes = slice(f + j * fc, f + (j + 1) * fc)
            v = hh_ref[:, lanes].astype(_F32)
            g = hh_ref[:, glanes].astype(_F32)
            gc = gc_ref[:, lanes]
            sg = jax.nn.sigmoid(gc)
            df = _dot_nt(dy2, wdn_ref[lanes, :])
            dv = df * (gc * sg)
            dgc = df * v * (sg * (1.0 + gc * (1.0 - sg)))
            dfc_ref[3:4, lanes] += _rowsum_halving(dgc)
            dext[0:tt, :] = dgc
            dext[tt:tt + 8, :] = dcar[j]
            dcar[j] = dgc[0:8, :]
            dg = jnp.zeros((tt, fc), _F32)
            for k in range(3):
                shifted = dext[pl.ds(2 - k, tt), :]
                dg = dg + fcw_ref[k:k + 1, lanes] * shifted
                dfc_ref[k:k + 1, lanes] += _rowsum_halving(shifted * g)
            dvb, dgb = _mx(dv), _mx(dg)
            dh_ref[:, lanes] = dvb
            dh_ref[:, glanes] = dgb
            du2 = du2 + _dot_nt(dvb, wup_ref[j]) + _dot_nt(dgb, wup_ref[nch + j])

        dx1_ref[...] = _ALPHA * dz2t + du2 * (1.0 + sc2)
        dmod_ref[0:1, :] += _rowsum_halving(du2)
        dmod_ref[1:2, :] += _rowsum_halving(du2 * x1_ref[...])

    tok = lambda c: pl.BlockSpec((None, tt, c), lambda b, i: (b, ns - 1 - i, 0))
    return pl.pallas_call(
        body, grid=(bl, ns),
        in_specs=[tok(d), tok(d), tok(2 * f), tok(f), pl.BlockSpec((None, 1, 6 * d), lambda b, i: (b, 0, 0)),
                  _resident(wup.shape), _resident(wdn.shape), _resident(fcw.shape), _resident(fcb.shape)],
        out_specs=[tok(d), tok(d), tok(2 * f), pl.BlockSpec((4, f), lambda b, i: (0, 0)),
                   pl.BlockSpec((None, 2, d), lambda b, i: (b, 0, 0))],
        out_shape=[jax.ShapeDtypeStruct((bl, s_len, d), _F32), jax.ShapeDtypeStruct((bl, s_len, d), _MXU_DT),
                   jax.ShapeDtypeStruct((bl, s_len, 2 * f), _MXU_DT), jax.ShapeDtypeStruct((4, f), _F32),
                   jax.ShapeDtypeStruct((bl, 2, d), _F32)],
        scratch_shapes=[pltpu.VMEM((tt + 8, fc), _F32), pltpu.VMEM((nch, 8, fc), _F32)],
        compiler_params=_cparams(("arbitrary", "arbitrary")), name="ffn_bwd",
    )(dz2, x1, hh, gc_all, mod3, wup, wdn, fcw, fcb)


def _mix_bwd(dx1, x, mix, proj, h, vbc, lru, mod3, win, lcw, lcb, wr_bd, wi_bd, b_r, b_i, lam, cw, cb, ng, nb, seg, wout, ln1g, chip_sums):
    bl, s_len, d = x.shape
    w = d // 2
    tt = min(_TT_MIX, s_len)
    ns = s_len // tt
    kc = cw.shape[0]

    def body(dx1_ref, x_ref, mix_ref, proj_ref, phalo_ref, h_ref, hhalo_ref, vbc_ref, lru_ref, mod_ref, win_ref, lcw_ref, lcb_ref,
             wr_ref, wi_ref, br_ref, bi_ref, lam_ref, cw_ref, cb_ref, ng_ref, nb_ref, seg_ref, wout_ref, g1_ref,
             gx_ref, dproj_ref, dmix_ref, xcg_ref, vecw_ref, dlcw_ref, dcw_ref, dln1_ref, dmod_ref,
             ext4, ext31, dext4, dext31, sh31, dsh31, car4, car31, gcar):
        s = ns - 1 - pl.program_id(1)
        first = s == 0

        @pl.when(pl.program_id(1) == 0)
        def _():
            car4[...] = jnp.zeros_like(car4)
            car31[...] = jnp.zeros_like(car31)
            gcar[...] = jnp.zeros_like(gcar)
            dmod_ref[...] = jnp.zeros_like(dmod_ref)

        @pl.when((pl.program_id(1) == 0) & (pl.program_id(0) == 0))
        def _():
            for ref in (vecw_ref, dlcw_ref, dcw_ref, dln1_ref):
                ref[...] = jnp.zeros_like(ref)

        xt, mixt = x_ref[...], mix_ref[...]
        sh1, sc1, gt1 = mod_ref[:, 0:d], mod_ref[:, d:2 * d], mod_ref[:, 2 * d:3 * d]

        n1, rstd1 = _layer_norm_stats(_ALPHA * xt + (1.0 + gt1) * mixt)
        dx1t = dx1_ref[...]
        dln1_ref[0:1, :] += _rowsum(dx1t * n1)
        dln1_ref[1:2, :] += _rowsum(dx1t)
        dz1 = _layer_norm_bwd(dx1t * g1_ref[...], n1, rstd1)
        dmod_ref[2:3, :] += _rowsum(dz1 * mixt)
        dmix = _mx((1.0 + gt1) * dz1)
        dmix_ref[...] = dmix
        dya = _dot_nt(dmix, wout_ref[0:w, :])
        dyb = _dot_nt(dmix, wout_ref[w:2 * w, :])

        xa, ga = proj_ref[:, 0:w], proj_ref[:, w:2 * w]
        vb, gb = proj_ref[:, 2 * w:3 * w], proj_ref[:, 3 * w:4 * w]

        sgb = jax.nn.sigmoid(gb)
        vbg = vb * sgb
        hv, hg = phalo_ref[:, 2 * w:3 * w], phalo_ref[:, 3 * w:4 * w]
        ext31[0:_HALO, :] = jnp.where(first, 0.0, hv * jax.nn.sigmoid(hg))
        ext31[_HALO:_HALO + tt, :] = vbg
        _make_shifted(ext31, sh31)
        vbc = vbc_ref[...]
        inv = 1.0 / (w // _N_HEADS)
        zc = vbc - _seg_sum(vbc, seg_ref[...]) * inv
        rstd = lax.rsqrt(_seg_sum(zc * zc, seg_ref[...], 2) * inv + _LN_EPS)
        n = zc * rstd
        pre = n * ng_ref[...] + nb_ref[...]
        sgp = jax.nn.sigmoid(pre)
        dpre = dyb * (sgp * (1.0 + pre * (1.0 - sgp)))
        vecw_ref[5:6, :] += _rowsum(dpre * n)
        vecw_ref[6:7, :] += _rowsum(dpre)
        dn = dpre * ng_ref[...]
        dvbc = rstd * (dn - _seg_sum(dn, seg_ref[...], 2) * inv - n * (_seg_sum(dn * n, seg_ref[...], 2) * inv))
        vecw_ref[4:5, :] += _rowsum(dvbc)
        dext31[0:tt, :] = dvbc
        dext31[tt:tt + _HALO, :] = car31[...]
        car31[...] = dvbc[0:_HALO, :]
        _make_shifted(dext31, dsh31)
        dvbg = jnp.zeros((tt, w), _F32)
        for k in range(kc):
            dvbg = dvbg + cw_ref[k:k + 1, :] * _tap(dext31, dsh31, kc - 1 - k, tt)
            dcw_ref[k:k + 1, :] += _rowsum(dvbc * _tap(ext31, sh31, _HALO - (kc - 1) + k, tt))
        dproj_ref[:, 2 * w:3 * w] = _mx(dvbg * sgb)
        dproj_ref[:, 3 * w:4 * w] = _mx(dvbg * vb * (sgb * (1.0 - sgb)))

        ext4[0:8, :] = jnp.where(first, 0.0, phalo_ref[_HALO - 8:_HALO, 0:w])
        ext4[8:8 + tt, :] = xa
        xc, r, i, a, mult = (lru_ref[:, k * w:(k + 1) * w] for k in range(5))
        xcg_ref[:, 0:w] = _mx(xc)
        sp = _softplus(-lam_ref[...])
        ht = h_ref[...]
        row = lax.broadcasted_iota(jnp.int32, (tt, w), 0)
        h_before = jnp.where(first, 0.0, hhalo_ref[7:8, :])
        hprev = jnp.where(row == 0, h_before, pltpu.roll(ht, 1, 0))
        gelu, dgelu = _gelu_and_grad(ga)
        dproj_ref[:, w:2 * w] = _mx(dya * ht * dgelu)
        dh = dya * gelu
        coef = jnp.where(row == tt - 1, 1.0, pltpu.roll(a, tt - 1, 0))
        big_g = _scan_rev(coef, dh, gcar[0:1, :])
        gcar[0:1, :] = a[0:1, :] * big_g[0:1, :]
        da = big_g * hprev
        ixc = i * xc
        dlog_a = da * a - (big_g * ixc) * (a * a / mult)
        di = big_g * mult * xc
        dxc = big_g * mult * i
        vecw_ref[3:4, :] += _rowsum(dlog_a * r) * (_LRU_C * jax.nn.sigmoid(-lam_ref[...]))
        dgr_f = dlog_a * (-_LRU_C * sp) * (r * (1.0 - r))
        dgi_f = di * (i * (1.0 - i))
        vecw_ref[1:2, :] += _rowsum(dgr_f)
        vecw_ref[2:3, :] += _rowsum(dgi_f)
        dgr, dgi = _mx(dgr_f), _mx(dgi_f)
        xcg_ref[:, w:2 * w] = dgr
        xcg_ref[:, 2 * w:3 * w] = dgi
        dxc = dxc + _dot_nt(dgr, wr_ref[...]) + _dot_nt(dgi, wi_ref[...])
        vecw_ref[0:1, :] += _rowsum(dxc)
        dext4[0:tt, :] = dxc
        dext4[tt:tt + 8, :] = car4[...]
        car4[...] = dxc[0:8, :]
        dxa = jnp.zeros((tt, w), _F32)
        for k in range(4):
            dxa = dxa + lcw_ref[k:k + 1, :] * dext4[pl.ds(3 - k, tt), :]
            dlcw_ref[k:k + 1, :] += _rowsum(dxc * ext4[pl.ds(5 + k, tt), :])
        dproj_ref[:, 0:w] = _mx(dxa)

        du1 = sum(_dot_nt(dproj_ref[:, k * w:(k + 1) * w], win_ref[k]) for k in range(4))
        gx_ref[...] = _ALPHA * dz1 + du1 * (1.0 + sc1)
        dmod_ref[0:1, :] += _rowsum(du1)
        dmod_ref[1:2, :] += _rowsum(du1 * xt)

    tok = lambda c: pl.BlockSpec((None, tt, c), lambda b, i: (b, ns - 1 - i, 0))
    halo = lambda rows, c: pl.BlockSpec(
        (None, rows, c), lambda b, i: (b, jnp.maximum((ns - 1 - i) * (tt // rows) - 1, 0), 0))
    accw = lambda r, c: pl.BlockSpec((r, c), lambda b, i: (0, 0))
    smalls = [lcw, lcb, wr_bd, wi_bd, b_r, b_i, lam, cw, cb, ng, nb, seg, wout, ln1g]
    nx = len(chip_sums)
    return pl.pallas_call(
        _fused_exchange(body, 11 + len(smalls), 9, 9, nx, nx, _chip_reduce_plan, (bl, ns)), grid=(bl, ns),
        in_specs=[tok(d), tok(d), tok(d), tok(4 * w), halo(_HALO, 4 * w), tok(w), halo(8, w), tok(w), tok(5 * w),
                  pl.BlockSpec((None, 1, 6 * d), lambda b, i: (b, 0, 0)), _resident(win.shape)]
        + [_resident(t.shape) for t in smalls] + [_HBM] * nx,
        out_specs=[tok(d), tok(4 * w), tok(d), tok(3 * w), accw(8, w), accw(4, w), accw(kc, w), accw(2, d),
                   pl.BlockSpec((None, 3, d), lambda b, i: (b, 0, 0))] + [_HBM] * nx,
        out_shape=[jax.ShapeDtypeStruct((bl, s_len, d), _F32), jax.ShapeDtypeStruct((bl, s_len, 4 * w), _MXU_DT),
                   jax.ShapeDtypeStruct((bl, s_len, d), _MXU_DT), jax.ShapeDtypeStruct((bl, s_len, 3 * w), _MXU_DT),
                   jax.ShapeDtypeStruct((8, w), _F32), jax.ShapeDtypeStruct((4, w), _F32),
                   jax.ShapeDtypeStruct((kc, w), _F32), jax.ShapeDtypeStruct((2, d), _F32),
                   jax.ShapeDtypeStruct((bl, 3, d), _F32)]
        + [jax.ShapeDtypeStruct((3,) + t.shape[1:], t.dtype) for t in chip_sums],
        scratch_shapes=[pltpu.VMEM((tt + 8, w), _F32), pltpu.VMEM((tt + _HALO, w), _F32),
                        pltpu.VMEM((tt + 8, w), _F32), pltpu.VMEM((tt + _HALO, w), _F32),
                        pltpu.VMEM((7, tt + _HALO - 8, w), _F32), pltpu.VMEM((7, tt + _HALO - 8, w), _F32),
                        pltpu.VMEM((8, w), _F32), pltpu.VMEM((_HALO, w), _F32), pltpu.VMEM((8, w), _F32)]
        + _chip_reduce_sems(nx),
        compiler_params=_cparams(("arbitrary", "arbitrary")), name="mix_bwd",
    )(dx1, x, mix, proj, proj, h, h, vbc, lru, mod3, win, *smalls, *chip_sums)


def _wgrad(a, b, ma, nbw, na, nb, a_off, b_off, name, exchange=None):
    t = a.shape[0]
    tk = min(_TK_WGRAD, t)
    grid = (na * nb, t // tk)

    def body(a_ref, b_ref, o_ref):
        @pl.when(pl.program_id(1) == 0)
        def _():
            o_ref[...] = jnp.zeros_like(o_ref)
        o_ref[...] += _dot_tn(a_ref[...], b_ref[...])

    xin, xshapes, plan, sems = exchange if exchange else ([], [], None, [])
    nx = len(xin)
    res = pl.pallas_call(
        _fused_exchange(body, 2, 1, 0, nx, len(xshapes), plan, grid) if exchange else body, grid=grid,
        in_specs=[pl.BlockSpec((tk, ma), lambda j, k: (k, j // nb + a_off)),
                  pl.BlockSpec((tk, nbw), lambda j, k: (k, j % nb + b_off))] + [_HBM] * nx,
        out_specs=[pl.BlockSpec((None, ma, nbw), lambda j, k: (j, 0, 0))] + [_HBM] * len(xshapes),
        out_shape=[jax.ShapeDtypeStruct((na * nb, ma, nbw), _F32)] + list(xshapes),
        scratch_shapes=list(sems),
        compiler_params=_cparams(("arbitrary", "arbitrary")), name=name,
    )(a, b, *xin)
    return res if exchange else res[0]


_DEV_DELTAS = tuple(dl for dl in itertools.product((0, 1), repeat=3) if any(dl))
_HBM = pl.BlockSpec(memory_space=pltpu.HBM)
_VMEM = pl.BlockSpec(memory_space=pltpu.VMEM)


def _pos():
    return lax.axis_index("x"), lax.axis_index("y"), lax.axis_index("c")


def _flip(v, delta):
    return 1 - v if delta else v


def _remote(src, dst, ssem, rsem, dev):
    return pltpu.make_async_remote_copy(src_ref=src, dst_ref=dst, send_sem=ssem, recv_sem=rsem,
                                        device_id=dev, device_id_type=_MESH)


def _rows(ref, idx, n):
    return ref.at[pl.ds(pl.multiple_of(idx * n, 8), n)]


def _ada_fwd(c8, w_ada_k, b_ada_k, shards):
    rows, d = c8.shape
    nk = w_ada_k.shape[1]
    n = len(shards)

    def body(*refs):
        c_ref, w_ref, b_ref = refs[:3]
        call_ref, mod_ref = refs[3 + n:5 + n]
        modloc, modrcv, s1, r1, s2, r2 = refs[5 + 2 * n:11 + 2 * n]
        gather_start, gather_finish = _gather_plan(refs[3:3 + n], refs[5 + n:5 + 2 * n], *refs[11 + 2 * n:14 + 2 * n],
                                                   fsem=refs[14 + 2 * n], frsem=refs[15 + 2 * n], bounce=refs[16 + 2 * n:])
        gather_start()
        xi, yi, ci = _pos()
        me, kme = 4 * xi + 2 * yi + ci, 2 * xi + yi
        call_ref[pl.ds(pl.multiple_of(me * rows, 8), rows), :] = c_ref[...]
        sends = []
        for p, (dx, dy, dc) in enumerate(_DEV_DELTAS):
            cp = _remote(c_ref, _rows(call_ref, me, rows), s1.at[p], r1.at[p], (_flip(xi, dx), _flip(yi, dy), _flip(ci, dc)))
            cp.start()
            sends.append(cp)
        for p, (dx, dy, dc) in enumerate(_DEV_DELTAS):
            src = 4 * _flip(xi, dx) + 2 * _flip(yi, dy) + _flip(ci, dc)
            _remote(c_ref, _rows(call_ref, src, rows), s1.at[p], r1.at[p], (xi, yi, ci)).wait_recv()
        for cp in sends:
            cp.wait_send()

        ca = call_ref[...]
        modloc[...] = _dot(_mx(ca * jax.nn.sigmoid(ca)), _mx(w_ref[...])) + b_ref[...]
        modrcv[kme] = modloc[pl.ds(pl.multiple_of(me * rows, 8), rows), :]
        sends = []
        for j, (dx, dy) in enumerate(_CHIP_DELTAS):
            tx, ty = _flip(xi, dx), _flip(yi, dy)
            cp = _remote(_rows(modloc, 4 * tx + 2 * ty + ci, rows), modrcv.at[kme], s2.at[j], r2.at[j], (tx, ty, ci))
            cp.start()
            sends.append(cp)
        for j, (dx, dy) in enumerate(_CHIP_DELTAS):
            ksrc = 2 * _flip(xi, dx) + _flip(yi, dy)
            _remote(_rows(modloc, me, rows), modrcv.at[ksrc], s2.at[j], r2.at[j], (xi, yi, ci)).wait_recv()
        for cp in sends:
            cp.wait_send()
        for j in range(4):
            mod_ref[:, j * nk:(j + 1) * nk] = modrcv[j]
        gather_finish()

    return pl.pallas_call(
        body, in_specs=[_VMEM, _VMEM, _VMEM] + [_HBM] * n, out_specs=[_VMEM, _VMEM] + [_HBM] * n,
        out_shape=[jax.ShapeDtypeStruct((8 * rows, d), _F32), jax.ShapeDtypeStruct((rows, 4 * nk), _F32)]
        + [jax.ShapeDtypeStruct((4,) + a.shape, a.dtype) for a in shards],
        scratch_shapes=[pltpu.VMEM((8 * rows, nk), _F32), pltpu.VMEM((4, rows, nk), _F32),
                        pltpu.SemaphoreType.DMA((7,)), pltpu.SemaphoreType.DMA((7,)),
                        pltpu.SemaphoreType.DMA((3,)), pltpu.SemaphoreType.DMA((3,))]
        + _gather_sems(n) + [pltpu.SemaphoreType.DMA((3, n)), pltpu.SemaphoreType.DMA((3, n))]
        + [pltpu.VMEM(a.shape, a.dtype) for a in shards],
        compiler_params=pltpu.CompilerParams(vmem_limit_bytes=_VMEM_LIMIT), name="ada_fwd",
    )(c8, w_ada_k, b_ada_k, *shards)


def _gather_sems(n):
    return [pltpu.SemaphoreType.DMA((3, n)), pltpu.SemaphoreType.DMA((3, n)), pltpu.SemaphoreType.DMA((n,))]


def _gather_plan(ins, outs, ssem, rsem, lsem, bounce=(), fsem=None, frsem=None):
    n = len(ins)
    xi, yi, ci = _pos()
    kme = 2 * xi + yi
    split = [fsem is not None and ins[a].shape[0] % 32 == 0 for a in range(n)]

    def half(ref, a, which):
        r2 = ins[a].shape[0] // 2
        return ref.at[pl.ds(pl.multiple_of(which * r2, 16), r2)]

    staged = [pltpu.make_async_copy(ins[a], bounce[a], lsem.at[a]) for a in range(len(bounce))]
    local = [pltpu.make_async_copy(bounce[a] if bounce else ins[a], outs[a].at[kme], lsem.at[a]) for a in range(n)]
    sends, recvs, forwards, handed = [], [], [], []
    for j, (dx, dy) in enumerate(_CHIP_DELTAS):
        tx, ty = _flip(xi, dx), _flip(yi, dy)
        for a in range(n):
            sems = (ssem.at[j, a], rsem.at[j, a])
            landing = outs[a].at[2 * tx + ty]
            if split[a]:
                sends.append(_remote(half(ins[a], a, ci), half(outs[a].at[kme], a, ci), *sems, (tx, ty, ci)))
                recvs.append(_remote(half(ins[a], a, ci), half(landing, a, ci), *sems, (xi, yi, ci)))
                fsems = (fsem.at[j, a], frsem.at[j, a])
                forwards.append(_remote(half(landing, a, ci), half(landing, a, ci), *fsems, (xi, yi, 1 - ci)))
                handed.append(_remote(half(ins[a], a, 1 - ci), half(landing, a, 1 - ci), *fsems, (xi, yi, ci)))
            else:
                sends.append(_remote(ins[a], outs[a].at[kme], *sems, (tx, ty, ci)))
                recvs.append(_remote(ins[a], landing, *sems, (xi, yi, ci)))
                forwards.append(None)

    def start():
        for cp in sends + staged:
            cp.start()
        for cp in staged:
            cp.wait()
        for cp in local:
            cp.start()

    def finish():
        for arrived, forward in zip(recvs, forwards):
            arrived.wait_recv()
            if forward is not None:
                forward.start()
        for cp in handed:
            cp.wait_recv()
        for cp in sends + [f for f in forwards if f is not None]:
            cp.wait_send()
        for cp in local:
            cp.wait()

    return start, finish


def _dev_gather_sems(n):
    return [pltpu.SemaphoreType.DMA((7, n)), pltpu.SemaphoreType.DMA((7, n)), pltpu.SemaphoreType.DMA((n,))]


def _dev_gather_plan(ins, outs, ssem, rsem, lsem):
    n = len(ins)
    xi, yi, ci = _pos()
    me, sibling = 4 * xi + 2 * yi + ci, (xi, yi, 1 - ci)
    local = [pltpu.make_async_copy(ins[a], outs[a].at[me], lsem.at[a]) for a in range(n)]
    sends = [_remote(ins[a], outs[a].at[me], ssem.at[0, a], rsem.at[0, a], sibling) for a in range(n)]
    handed = [_remote(ins[a], outs[a].at[4 * xi + 2 * yi + 1 - ci], ssem.at[0, a], rsem.at[0, a], (xi, yi, ci))
              for a in range(n)]
    arrivals, forwards = [], []
    for j, (dx, dy) in enumerate(_CHIP_DELTAS):
        tx, ty = _flip(xi, dx), _flip(yi, dy)
        for a in range(n):
            over_ici, over_d2d = (ssem.at[1 + j, a], rsem.at[1 + j, a]), (ssem.at[4 + j, a], rsem.at[4 + j, a])
            landing = outs[a].at[4 * tx + 2 * ty + ci]
            sends.append(_remote(ins[a], outs[a].at[me], *over_ici, (tx, ty, ci)))
            arrivals.append(_remote(ins[a], landing, *over_ici, (xi, yi, ci)))
            forwards.append(_remote(landing, landing, *over_d2d, sibling))
            handed.append(_remote(ins[a], outs[a].at[4 * tx + 2 * ty + 1 - ci], *over_d2d, (xi, yi, ci)))

    def start():
        for cp in local + sends:
            cp.start()

    def finish():
        for arrived, forward in zip(arrivals, forwards):
            arrived.wait_recv()
            forward.start()
        for cp in handed:
            cp.wait_recv()
        for cp in sends + forwards:
            cp.wait_send()
        for cp in local:
            cp.wait()

    return start, finish


def _pair_sems(n):
    return [pltpu.SemaphoreType.DMA((n,)), pltpu.SemaphoreType.DMA((n,))]


def _pair_plan(ins, outs, ssem, rsem):
    xi, yi, ci = _pos()
    sends = []
    for a in range(len(ins)):
        r2 = ins[a].shape[1] // 2
        src = ins[a].at[:, pl.ds(pl.multiple_of((1 - ci) * r2, 8), r2), :]
        sends.append(_remote(src, outs[a], ssem.at[a], rsem.at[a], (xi, yi, 1 - ci)))

    def start():
        for cp in sends:
            cp.start()

    def finish():
        for cp in sends:
            cp.wait_recv()
        for cp in sends:
            cp.wait_send()

    return start, finish


def _chip_reduce_sems(n):
    return [pltpu.SemaphoreType.DMA((3, n)), pltpu.SemaphoreType.DMA((3, n))]


def _chip_reduce_plan(ins, outs, ssem, rsem):
    xi, yi, ci = _pos()
    sends = []
    for j, (dx, dy) in enumerate(_CHIP_DELTAS):
        tx, ty = _flip(xi, dx), _flip(yi, dy)
        sends += [_remote(ins[a].at[2 * tx + ty], outs[a].at[j], ssem.at[j, a], rsem.at[j, a], (tx, ty, ci))
                  for a in range(len(ins))]

    def start():
        for cp in sends:
            cp.start()

    def finish():
        for cp in sends:
            cp.wait_recv()
        for cp in sends:
            cp.wait_send()

    return start, finish


def _pair_exchange(gs, name):
    n = len(gs)

    def body(*refs):
        start, finish = _pair_plan(refs[:n], refs[n:2 * n], *refs[2 * n:])
        start()
        finish()

    return pl.pallas_call(
        body, in_specs=[_HBM] * n, out_specs=[_HBM] * n, out_shape=_pair_out_shapes(gs),
        scratch_shapes=_pair_sems(n), name=name,
    )(*gs)


def _pair_out_shapes(gs):
    return [jax.ShapeDtypeStruct((g.shape[0], g.shape[1] // 2, g.shape[2]), g.dtype) for g in gs]


def _row_tile(r):
    return max(t for t in range(8, min(r, 256) + 1, 8) if r % t == 0)


def _pair_add(g, r, cidx, name, wire_dtype=None, exchange=None):
    nk, r2, c = r.shape
    tr = _row_tile(r2)
    nt = r2 // tr
    xin, xshapes, plan, sems = exchange if exchange else ([], [], None, [])
    nx = len(xin)

    def body(c_ref, g_ref, r_ref, *o_refs):
        s = g_ref[...] + r_ref[...]
        for o_ref in o_refs:
            o_ref[...] = s.astype(o_ref.dtype)

    out_spec = pl.BlockSpec((None, tr, c), lambda k, i, cr: (k, i, 0))
    dtypes = [_F32] + ([wire_dtype] if wire_dtype else [])
    res = pl.pallas_call(
        _fused_exchange(body, 3, len(dtypes), 0, nx, len(xshapes), plan, (nk, nt)) if exchange else body,
        grid_spec=pltpu.PrefetchScalarGridSpec(
            num_scalar_prefetch=1, grid=(nk, nt),
            in_specs=[pl.BlockSpec((None, tr, c), lambda k, i, cr: (k, cr[0] * nt + i, 0)), out_spec] + [_HBM] * nx,
            out_specs=[out_spec] * len(dtypes) + [_HBM] * len(xshapes), scratch_shapes=list(sems)),
        out_shape=[jax.ShapeDtypeStruct(r.shape, dt) for dt in dtypes] + list(xshapes),
        compiler_params=_cparams(("arbitrary", "arbitrary")), name=name,
    )(cidx, g, r, *xin)
    return res if wire_dtype or exchange else res[0]


def _chip_exchange(ss):
    n = len(ss)

    def body(*refs):
        start, finish = _chip_reduce_plan(refs[:n], refs[n:2 * n], *refs[2 * n:])
        start()
        finish()

    return pl.pallas_call(
        body, in_specs=[_HBM] * n, out_specs=[_HBM] * n,
        out_shape=[jax.ShapeDtypeStruct((3,) + s.shape[1:], s.dtype) for s in ss],
        scratch_shapes=_chip_reduce_sems(n), name="grad_chip_exchange",
    )(*ss)


def _chip_add(s, r, kidx, name):
    _, r2, c = r.shape
    tr = _row_tile(r2)

    def body(k_ref, s_ref, r_ref, o_ref):
        o_ref[...] = ((s_ref[...] + r_ref[0].astype(_F32)) + r_ref[1].astype(_F32)) + r_ref[2].astype(_F32)

    return pl.pallas_call(
        body, grid_spec=pltpu.PrefetchScalarGridSpec(
            num_scalar_prefetch=1, grid=(r2 // tr,),
            in_specs=[pl.BlockSpec((None, tr, c), lambda i, kr: (kr[0], i, 0)),
                      pl.BlockSpec((3, tr, c), lambda i, kr: (0, i, 0))],
            out_specs=pl.BlockSpec((tr, c), lambda i, kr: (i, 0))),
        out_shape=jax.ShapeDtypeStruct((r2, c), _F32),
        compiler_params=_cparams(("arbitrary",)), name=name,
    )(kidx, s, r)


def _pair_swap_plan(ins, outs, ssem, rsem):
    xi, yi, ci = _pos()
    sends = [_remote(ins[a], outs[a], ssem.at[a], rsem.at[a], (xi, yi, 1 - ci)) for a in range(len(ins))]

    def start():
        for cp in sends:
            cp.start()

    def finish():
        for cp in sends:
            cp.wait_recv()
        for cp in sends:
            cp.wait_send()

    return start, finish


def _pair_swap(hs, name):
    n = len(hs)

    def body(*refs):
        start, finish = _pair_swap_plan(refs[:n], refs[n:2 * n], *refs[2 * n:])
        start()
        finish()

    return pl.pallas_call(
        body, in_specs=[_HBM] * n, out_specs=[_HBM] * n,
        out_shape=[jax.ShapeDtypeStruct(h.shape, h.dtype) for h in hs],
        scratch_shapes=[pltpu.SemaphoreType.DMA((n,)), pltpu.SemaphoreType.DMA((n,))], name=name,
    )(*hs)


def _small_sum(every):
    def body(all_ref, sum_ref):
        tot = all_ref[0]
        for dev in range(1, 8):
            tot = tot + all_ref[dev]
        sum_ref[...] = tot

    return pl.pallas_call(
        body, in_specs=[_VMEM], out_specs=_VMEM, out_shape=jax.ShapeDtypeStruct(every.shape[1:], _F32),
        compiler_params=pltpu.CompilerParams(vmem_limit_bytes=_VMEM_LIMIT), name="small_sum",
    )(every)


def _adamw(w, g, m, v):
    m = _ADAM_B1 * m + (1.0 - _ADAM_B1) * g
    v = _ADAM_B2 * v + (1.0 - _ADAM_B2) * (g * g)
    m_hat = m / (1.0 - _ADAM_B1 ** _ADAM_STEP)
    v_hat = v / (1.0 - _ADAM_B2 ** _ADAM_STEP)
    return -_ADAM_LR * (m_hat / (jnp.sqrt(v_hat) + _ADAM_EPS) + _ADAM_WD * w), m, v


def _adamw_big(w, g_mine, g_theirs, m, v, cidx, name):
    r, c = w.shape
    tr = _row_tile(r // 2)
    nt = r // 2 // tr

    def body(c_ref, w_ref, gm_ref, gt_ref, m_ref, v_ref, g_ref, d_ref, mo_ref, vo_ref):
        g = jnp.where(pl.program_id(0) // nt == c_ref[0], gm_ref[...], gt_ref[...])
        g_ref[...] = g
        d_ref[...], mo_ref[...], vo_ref[...] = _adamw(w_ref[...], g, m_ref[...], v_ref[...])

    spec = pl.BlockSpec((tr, c), lambda i, cr: (i, 0))
    half = pl.BlockSpec((tr, c), lambda i, cr: (i % nt, 0))
    return pl.pallas_call(
        body, grid_spec=pltpu.PrefetchScalarGridSpec(
            num_scalar_prefetch=1, grid=(2 * nt,), in_specs=[spec, half, half, spec, spec], out_specs=[spec] * 4),
        out_shape=[jax.ShapeDtypeStruct((r, c), _F32)] * 4,
        compiler_params=_cparams(("arbitrary",)), name=name,
    )(cidx, w, g_mine, g_theirs, m, v)


def _adamw_small(ws, gs, ms, vs):
    n = len(ws)
    summed = [i for i in range(n) if gs[i].shape != ws[i].shape]

    def body(*refs):
        w_r, g_r, m_r, v_r = (refs[i * n:(i + 1) * n] for i in range(4))
        outs = refs[4 * n:]
        for i in range(n):
            g = g_r[i][...]
            if i in summed:
                g = _rowsum(g)
                outs[3 * n + summed.index(i)][...] = g
            outs[i][...], outs[n + i][...], outs[2 * n + i][...] = _adamw(w_r[i][...], g, m_r[i][...], v_r[i][...])

    shapes = [jax.ShapeDtypeStruct(w.shape, _F32) for w in ws]
    res = pl.pallas_call(
        body, in_specs=[_VMEM] * (4 * n), out_specs=[_VMEM] * (3 * n + len(summed)),
        out_shape=shapes * 3 + [shapes[i] for i in summed],
        compiler_params=pltpu.CompilerParams(vmem_limit_bytes=_VMEM_LIMIT), name="adamw_small",
    )(*ws, *gs, *ms, *vs)
    gs = list(gs)
    for pos, i in enumerate(summed):
        gs[i] = res[3 * n + pos]
    return gs, res[:n], res[n:2 * n], res[2 * n:3 * n]


def _ada_bwd(c_all, dmod_k, w, m, v):
    d, nk = w.shape
    tn = 512 if nk % 512 == 0 else nk

    def body(c_ref, dm_ref, w_ref, m_ref, v_ref, g_ref, d_ref, mo_ref, vo_ref):
        ca = c_ref[...]
        g = _dot_tn(_mx(ca * jax.nn.sigmoid(ca)), _mx(dm_ref[...]))
        g_ref[...] = g
        d_ref[...], mo_ref[...], vo_ref[...] = _adamw(w_ref[...], g, m_ref[...], v_ref[...])

    col = pl.BlockSpec((d, tn), lambda j: (0, j))
    return pl.pallas_call(
        body, grid=(nk // tn,),
        in_specs=[pl.BlockSpec(c_all.shape, lambda j: (0, 0)), pl.BlockSpec((c_all.shape[0], tn), lambda j: (0, j)),
                  col, col, col],
        out_specs=[col] * 4, out_shape=[jax.ShapeDtypeStruct((d, nk), _F32)] * 4,
        compiler_params=_cparams(("arbitrary",)), name="ada_bwd",
    )(c_all, dmod_k, w, m, v)


def _block_diag(wh):
    hn, dh, _ = wh.shape
    eye = jnp.eye(hn, dtype=wh.dtype)
    return (eye[:, None, :, None] * wh[:, :, None, :]).reshape(hn * dh, hn * dh)


def _pack(pieces):
    out = []
    for p in pieces:
        flat = p.reshape(-1, 128)
        out.append(jnp.pad(flat, ((0, (-flat.shape[0]) % 8), (0, 0))))
    return jnp.concatenate(out, axis=0)


def _unpack(pack, shapes):
    out, off = [], 0
    for shp in shapes:
        rows = math.prod(shp) // 128
        out.append(pack[..., off:off + rows, :].reshape(pack.shape[:-2] + tuple(shp)))
        off += rows + (-rows) % 8
    return out


_WEIGHTS = ('w_ada', 'b_ada', 'w_in', 'lru_conv_w', 'lru_conv_b', 'lru_w_r', 'lru_b_r', 'lru_w_i', 'lru_b_i', 'lru_lambda',
            'conv_w', 'conv_b', 'conv_norm_g', 'conv_norm_b', 'w_out', 'ln1_g', 'ln1_b', 'ffn_w_up', 'ffn_conv_w',
            'ffn_conv_b', 'ffn_w_down', 'ln2_g', 'ln2_b')
_BIG = ('w_in', 'w_out', 'ffn_w_up', 'ffn_w_down')


def kernel(x, c, w_ada, b_ada, w_in, lru_conv_w, lru_conv_b, lru_w_r, lru_b_r, lru_w_i, lru_b_i, lru_lambda, conv_w, conv_b, conv_norm_g, conv_norm_b, w_out, ln1_g, ln1_b, ffn_w_up, ffn_conv_w, ffn_conv_b, ffn_w_down, ln2_g, ln2_b, loss_target, m_w_ada, m_b_ada, m_w_in, m_lru_conv_w, m_lru_conv_b, m_lru_w_r, m_lru_b_r, m_lru_w_i, m_lru_b_i, m_lru_lambda, m_conv_w, m_conv_b, m_conv_norm_g, m_conv_norm_b, m_w_out, m_ln1_g, m_ln1_b, m_ffn_w_up, m_ffn_conv_w, m_ffn_conv_b, m_ffn_w_down, m_ln2_g, m_ln2_b, v_w_ada, v_b_ada, v_w_in, v_lru_conv_w, v_lru_conv_b, v_lru_w_r, v_lru_b_r, v_lru_w_i, v_lru_b_i, v_lru_lambda, v_conv_w, v_conv_b, v_conv_norm_g, v_conv_norm_b, v_w_out, v_ln1_g, v_ln1_b, v_ffn_w_up, v_ffn_conv_w, v_ffn_conv_b, v_ffn_w_down, v_ln2_g, v_ln2_b):
    given = dict(locals())
    wt = {n: given[n] for n in _WEIGHTS}
    mo = {n: given["m_" + n] for n in _WEIGHTS}
    vo = {n: given["v_" + n] for n in _WEIGHTS}
    bl, s_len, d = x.shape
    wd = d // 2
    tokens = bl * s_len
    xi, yi, ci = _pos()
    kme = 2 * xi + yi
    kidx = jnp.reshape(kme, (1,)).astype(jnp.int32)
    cidx = jnp.reshape(ci, (1,)).astype(jnp.int32)

    nk = w_ada.shape[2]
    c8 = jnp.pad(c, ((0, 8 - bl), (0, 0)))
    c_all, mod8, win, wout_s, lcw_s, cw_s, fcw_s = _ada_fwd(
        c8, w_ada[0], lax.dynamic_slice(b_ada, (0, kme * nk), (1, nk)),
        [_mx(w_in[0]), _mx(w_out[0]), lru_conv_w[0], conv_w[0], ffn_conv_w[0]])
    mod3 = mod8[:bl].reshape(bl, 1, 6 * d)
    wout = wout_s.reshape(d, d)
    f = 4 * ffn_w_down.shape[1]
    unshard = lambda t: jnp.transpose(t, (1, 0, 2)).reshape(t.shape[1], -1)
    lcw, cw, fcw = unshard(lcw_s), unshard(cw_s), unshard(fcw_s)
    wr_bd, wi_bd = _mx(_block_diag(lru_w_r[0])), _mx(_block_diag(lru_w_i[0]))
    seg = _block_diag(jnp.ones((_N_HEADS, wd // _N_HEADS, wd // _N_HEADS), jnp.bfloat16))
    mixer_small = (lcw, lru_conv_b, wr_bd, wi_bd, lru_b_r, lru_b_i, lru_lambda, cw, conv_b, conv_norm_g, conv_norm_b, seg, wout, ln1_g)

    proj, h, mix, x1, u1, y, vbc, lru, wup, wdn_s = _mix_fwd(x, mod3, win, *mixer_small, ln1_b, [_mx(ffn_w_up[0]), _mx(ffn_w_down[0])])
    wdn = wdn_s.reshape(f, d)
    u2, hh, fact, gc_all, dz2, loss_acc, dln2, dgt2 = _ffn_fwd(x1, mod3, wup, fcw, ffn_conv_b, wdn, ln2_g, ln2_b, loss_target)
    dx1, dy2, dh, dfc, dmod2 = _ffn_bwd(dz2, x1, hh, gc_all, mod3, wup, wdn, fcw, ffn_conv_b)

    flat = lambda t: t.reshape(tokens, t.shape[-1])
    fc = wup.shape[2]
    g_up = _wgrad(flat(u2), flat(dh), d, fc, 1, 4, 0, 0, "wgrad_up")
    g_dn, r_up = _wgrad(flat(fact), flat(dy2), fc, d, f // fc, 1, 0, 0, "wgrad_down",
                        exchange=([g_up], _pair_out_shapes([g_up]), _pair_plan, _pair_sems(1)))
    g_dn = g_dn.reshape(4, f // 4, d)
    s_up, r_dn = _pair_add(g_up, r_up, cidx, "grad_pair_add_ffn_w_up",
                           exchange=([g_dn], _pair_out_shapes([g_dn]), _pair_plan, _pair_sems(1)))
    ffn_sum = [s_up, _pair_add(g_dn, r_dn, cidx, "grad_pair_add_ffn_w_down")]
    grad_x, dproj, dmix, xcg, vecw, dlcw, dcw, dln1, dmod1, *ffn_recv = _mix_bwd(
        dx1, x, mix, proj, h, vbc, lru, mod3, win, *mixer_small, ffn_sum)
    g_ri = _wgrad(flat(xcg), flat(xcg), wd, wd, 1, 2, 0, 1, "wgrad_gates")
    dh_ = wd // _N_HEADS
    on_diagonal = jnp.eye(_N_HEADS, dtype=_F32)[None, :, None, :, None]
    g_ri = jnp.sum(g_ri.reshape(2, _N_HEADS, dh_, _N_HEADS, dh_) * on_diagonal, axis=3)

    dmod = jnp.concatenate([dmod1.reshape(bl, 3 * d), dmod2.reshape(bl, 2 * d), dgt2.reshape(bl, d)], axis=1)
    pieces = [vecw, dlcw, dcw, jnp.concatenate([dln1, dln2], axis=0), dfc, g_ri, loss_acc[:, 0:128],
              jnp.pad(dmod, ((0, 8 - bl), (0, 0)))]
    shapes = [p.shape for p in pieces]
    pack = _pack(pieces)
    g_in, every = _wgrad(flat(u1), flat(dproj), d, wd, 1, 4, 0, 0, "wgrad_in", exchange=(
        [pack], [jax.ShapeDtypeStruct((8,) + pack.shape, _F32)], _dev_gather_plan, _dev_gather_sems(1)))
    wire_shape = lambda t: [jax.ShapeDtypeStruct((3,) + t.shape[1:], t.dtype)]
    ffn_half = [_chip_add(s, r, kidx, "grad_chip_add_" + n) for s, r, n in zip(ffn_sum, ffn_recv, _BIG[2:])]
    same = lambda ts: [jax.ShapeDtypeStruct(t.shape, t.dtype) for t in ts]
    r_in, = _pair_exchange([g_in], "grad_pair_exchange_w_in")
    s_in, wire_in, *ffn_theirs = _pair_add(g_in, r_in, cidx, "grad_pair_add_w_in", jnp.bfloat16,
                                           exchange=(ffn_half, same(ffn_half), _pair_swap_plan, _pair_sems(2)))
    g_out, recv_in = _wgrad(flat(y), flat(dmix), d, d, 1, 1, 0, 0, "wgrad_out", exchange=(
        [wire_in], wire_shape(wire_in), _chip_reduce_plan, _chip_reduce_sems(1)))
    g_out = g_out.reshape(4, d // 4, d)
    r_out, = _pair_exchange([g_out], "grad_pair_exchange_w_out")
    s_out, wire_out = _pair_add(g_out, r_out, cidx, "grad_pair_add_w_out", jnp.bfloat16)
    recv_out, = _chip_exchange([wire_out])
    mix_half = [_chip_add(s, r, kidx, "grad_chip_add_" + n) for s, r, n in zip([s_in, s_out], [recv_in, recv_out], _BIG)]
    half, other = mix_half + ffn_half, list(_pair_swap(mix_half, "grad_pair_swap")) + ffn_theirs
    grads, deltas, new_m, new_v = {}, {}, {}, {}
    for n, mine, theirs in zip(_BIG, half, other):
        g, dl, mm, vv = _adamw_big(wt[n][0], mine, theirs, mo[n][0], vo[n][0], cidx, "adamw_" + n)
        grads[n], deltas[n], new_m[n], new_v[n] = g[None], dl[None], mm[None], vv[None]

    vecw, dlcw, dcw, dln, dfc, g_ri, loss_sum, dmod_sum = _unpack(_small_sum(every), shapes)
    loss = 0.5 * loss_sum[0, 0] / d
    dmod_all = _unpack(every, shapes)[-1].reshape(64, 6 * d)

    g_ada, dl, mm, vv = _ada_bwd(c_all, lax.dynamic_slice(dmod_all, (0, kme * nk), (64, nk)), w_ada[0], m_w_ada[0], v_w_ada[0])
    grads['w_ada'], deltas['w_ada'], new_m['w_ada'], new_v['w_ada'] = g_ada[None], dl[None], mm[None], vv[None]

    shard = lambda t, width: lax.dynamic_slice(t, (0, kme * width), (t.shape[0], width))
    small = {
        'b_ada': dmod_sum, 'lru_conv_w': shard(dlcw, wd // 4), 'lru_conv_b': vecw[0:1], 'lru_w_r': g_ri[0], 'lru_b_r': vecw[1:2],
        'lru_w_i': g_ri[1], 'lru_b_i': vecw[2:3], 'lru_lambda': vecw[3:4], 'conv_w': shard(dcw, wd // 4), 'conv_b': vecw[4:5],
        'conv_norm_g': vecw[5:6], 'conv_norm_b': vecw[6:7], 'ln1_g': dln[0:1], 'ln1_b': dln[1:2],
        'ffn_conv_w': shard(dfc[0:3], f // 4), 'ffn_conv_b': dfc[3:4], 'ln2_g': dln[2:3], 'ln2_b': dln[3:4]}
    names = list(small)
    gs = [small[n] if n == 'b_ada' else small[n].reshape(wt[n].shape) for n in names]
    gs, dls, mms, vvs = _adamw_small([wt[n] for n in names], gs, [mo[n] for n in names], [vo[n] for n in names])
    for n, g, dl, mm, vv in zip(names, gs, dls, mms, vvs):
        grads[n], deltas[n], new_m[n], new_v[n] = g, dl, mm, vv

    return (loss, grad_x, *[grads[n] for n in _WEIGHTS], *[deltas[n] for n in _WEIGHTS],
            *[new_m[n] for n in _WEIGHTS], *[new_v[n] for n in _WEIGHTS])
```

```python
import functools
import itertools
import math

import jax
import jax.numpy as jnp
from jax import lax
from jax.experimental import pallas as pl
from jax.experimental.pallas import tpu as pltpu

_MXU_DT = jnp.bfloat16
_F32 = jnp.float32
_VMEM_LIMIT = 56 * 1024 * 1024
_TT_MIX = 256
_TT_MIX_FWD = 512
_TT_FFN = 256
_TK_WGRAD = 2048
_HALO = 32

_LRU_C = 8.0
_LN_EPS = 1e-5
_N_HEADS = 8
_DEPTH = 1
_ALPHA = (2 * _DEPTH) ** 0.25
_ADAM_LR, _ADAM_B1, _ADAM_B2, _ADAM_EPS, _ADAM_WD, _ADAM_STEP = 0.001, 0.9, 0.999, 1e-08, 0.01, 10

_MESH = pl.DeviceIdType.MESH
_CHIP_DELTAS = ((1, 0), (0, 1), (1, 1))


def _cparams(sem):
    return pltpu.CompilerParams(dimension_semantics=sem, vmem_limit_bytes=_VMEM_LIMIT)


def _resident(shape):
    nd = len(shape)
    return pl.BlockSpec(shape, lambda *_: (0,) * nd, pipeline_mode=pl.Buffered(1))


def _dot(a, b):
    return jnp.dot(a, b, preferred_element_type=_F32)


def _dot_nt(a, b):
    return lax.dot_general(a, b, (((1,), (1,)), ((), ())), preferred_element_type=_F32)


def _dot_tn(a, b):
    return lax.dot_general(a, b, (((0,), (0,)), ((), ())), preferred_element_type=_F32)


def _mx(v):
    return v.astype(_MXU_DT)


def _expm1(v):
    series = v * (1.0 + v * (1.0 / 2 + v * (1.0 / 6 + v * (1.0 / 24 + v * (1.0 / 120)))))
    return jnp.where(jnp.abs(v) < 0.0625, series, jnp.exp(v) - 1.0)


def _softplus(z):
    e = jnp.exp(-jnp.abs(z))
    u = 1.0 + e
    log1p = jnp.where(u == 1.0, e, jnp.log(u) * e / jnp.where(u == 1.0, 1.0, u - 1.0))
    return jnp.maximum(z, 0.0) + log1p


_GELU_C = math.sqrt(2.0 / math.pi)


def _gelu_and_grad(v):
    t = jnp.tanh(_GELU_C * (v + 0.044715 * v * v * v))
    val = 0.5 * v * (1.0 + t)
    grad = 0.5 * (1.0 + t) + 0.5 * v * (1.0 - t * t) * _GELU_C * (1.0 + 3 * 0.044715 * v * v)
    return val, grad


def _seg_sum(v, seg, passes=3):
    hi = v.astype(jnp.bfloat16)
    r1 = v - hi.astype(_F32)
    mid = r1.astype(jnp.bfloat16)
    out = _dot(hi, seg) + _dot(mid, seg)
    if passes == 3:
        out = out + _dot((r1 - mid.astype(_F32)).astype(jnp.bfloat16), seg)
    return out


_SCAN_BLOCK = 32


def _scan_fwd(a, u, h0):
    n = a.shape[0]
    blk = min(_SCAN_BLOCK, n)
    sub = lax.broadcasted_iota(jnp.int32, a.shape, 0) % blk
    h, d = u, 1
    while d < blk:
        keep = sub >= d
        h = a * jnp.where(keep, pltpu.roll(h, d, 0), 0.0) + h
        a = a * jnp.where(keep, pltpu.roll(a, d, 0), 1.0)
        d *= 2
    out, carry = [], h0
    for b in range(n // blk):
        rows = slice(b * blk, (b + 1) * blk)
        out.append(h[rows] + a[rows] * carry)
        carry = out[-1][blk - 1:blk, :]
    return jnp.concatenate(out, axis=0)


def _scan_rev(c, g, g_end):
    n = c.shape[0]
    blk = min(_SCAN_BLOCK, n)
    sub = lax.broadcasted_iota(jnp.int32, c.shape, 0) % blk
    d = 1
    while d < blk:
        keep = sub < blk - d
        g = c * jnp.where(keep, pltpu.roll(g, n - d, 0), 0.0) + g
        c = c * jnp.where(keep, pltpu.roll(c, n - d, 0), 1.0)
        d *= 2
    out, carry = [None] * (n // blk), g_end
    for b in reversed(range(n // blk)):
        rows = slice(b * blk, (b + 1) * blk)
        out[b] = g[rows] + c[rows] * carry
        carry = out[b][0:1, :]
    return jnp.concatenate(out, axis=0)


def _layer_norm_stats(z):
    mu = jnp.mean(z, axis=-1, keepdims=True)
    zc = z - mu
    var = jnp.mean(zc * zc, axis=-1, keepdims=True)
    rstd = lax.rsqrt(var + _LN_EPS)
    return zc * rstd, rstd


def _layer_norm_bwd(dn, n, rstd):
    return rstd * (dn - jnp.mean(dn, axis=-1, keepdims=True) - n * jnp.mean(dn * n, axis=-1, keepdims=True))


def _rowsum(v):
    return jnp.sum(v, axis=0, keepdims=True)


def _rowsum_halving(v):
    n = v.shape[0]
    while n > 8 and n % 16 == 0:
        v = v[:n // 2] + v[n // 2:]
        n //= 2
    return jnp.sum(v, axis=0, keepdims=True)


def _fused_exchange(body, n_in, n_out, n_scratch, n_xin, n_xout, plan, grid):
    def wrapped(*refs):
        o0 = n_in + n_xin
        s0 = o0 + n_out + n_xout
        start, *relay, finish = plan(refs[n_in:o0], refs[o0 + n_out:s0], *refs[s0 + n_scratch:])
        step = 0
        for axis, size in enumerate(grid):
            step = step * size + pl.program_id(axis)

        @pl.when(step == 0)
        def _():
            start()

        for stage in relay:
            @pl.when(step == math.prod(grid) // 2)
            def _():
                stage()

        body(*refs[:n_in], *refs[o0:o0 + n_out], *refs[s0:s0 + n_scratch])

        @pl.when(step == math.prod(grid) - 1)
        def _():
            finish()

    return wrapped


def _lru_gates(xc, wr_ref, wi_ref, br_ref, bi_ref, lam_ref):
    xcb = _mx(xc)
    r = jax.nn.sigmoid(_dot(xcb, wr_ref[...]) + br_ref[...])
    i = jax.nn.sigmoid(_dot(xcb, wi_ref[...]) + bi_ref[...])
    sp = _softplus(-lam_ref[...])
    log_a = -_LRU_C * r * sp
    a = jnp.exp(log_a)
    mult = jnp.sqrt(-_expm1(2.0 * log_a))
    return r, i, sp, a, mult


def _conv_taps(ext_ref, w_ref, first, n_taps, tt):
    acc = w_ref[0:1, :] * ext_ref[pl.ds(first, tt), :]
    for k in range(1, n_taps):
        acc = acc + w_ref[k:k + 1, :] * ext_ref[pl.ds(first + k, tt), :]
    return acc


def _make_shifted(ext_ref, sh_ref):
    n = sh_ref.shape[1]
    for r in range(1, 8):
        sh_ref[r - 1] = ext_ref[pl.ds(r, n), :]


def _tap(ext_ref, sh_ref, off, tt):
    base = (off // 8) * 8
    if off % 8 == 0:
        return ext_ref[pl.ds(base, tt), :]
    return sh_ref[off % 8 - 1, pl.ds(base, tt), :]


def _conv_taps_shifted(ext_ref, sh_ref, w_ref, first, n_taps, tt):
    acc = w_ref[0:1, :] * _tap(ext_ref, sh_ref, first, tt)
    for k in range(1, n_taps):
        acc = acc + w_ref[k:k + 1, :] * _tap(ext_ref, sh_ref, first + k, tt)
    return acc


def _mix_fwd(x, mod3, win, lcw, lcb, wr_bd, wi_bd, b_r, b_i, lam, cw, cb, ng, nb, seg, wout, ln1g, ln1b, shards):
    bl, s_len, d = x.shape
    w = d // 2
    tt = min(_TT_MIX_FWD, s_len)
    ns = s_len // tt
    kc = cw.shape[0]

    def body(x_ref, mod_ref, win_ref, lcw_ref, lcb_ref, wr_ref, wi_ref, br_ref, bi_ref, lam_ref, cw_ref, cb_ref,
             ng_ref, nb_ref, seg_ref, wout_ref, g1_ref, b1_ref,
             proj_ref, h_ref, mix_ref, x1_ref, u1_ref, y_ref, vbc_ref, lru_ref, ext4, ext31, sh31, hcar):
        @pl.when(pl.program_id(1) == 0)
        def _():
            ext4[0:8, :] = jnp.zeros((8, w), _F32)
            ext31[0:_HALO, :] = jnp.zeros((_HALO, w), _F32)
            hcar[...] = jnp.zeros_like(hcar)

        xt = x_ref[...]
        sh1, sc1, gt1 = mod_ref[:, 0:d], mod_ref[:, d:2 * d], mod_ref[:, 2 * d:3 * d]
        u1 = _mx(xt * (1.0 + sc1) + sh1)
        u1_ref[...] = u1
        xa, ga, vb, gb = (_dot(u1, win_ref[k]) for k in range(4))
        proj_ref[:, 0:w] = xa
        proj_ref[:, w:2 * w] = ga
        proj_ref[:, 2 * w:3 * w] = vb
        proj_ref[:, 3 * w:4 * w] = gb

        ext4[8:8 + tt, :] = xa
        xc = lcb_ref[...] + _conv_taps(ext4, lcw_ref, 5, 4, tt)
        ext4[0:8, :] = xa[tt - 8:tt, :]
        r, i, sp, a, mult = _lru_gates(xc, wr_ref, wi_ref, br_ref, bi_ref, lam_ref)
        for k, val in enumerate((xc, r, i, a, mult)):
            lru_ref[:, k * w:(k + 1) * w] = val
        h = _scan_fwd(a, mult * (i * xc), hcar[0:1, :])
        hcar[0:1, :] = h[tt - 1:tt, :]
        h_ref[...] = h
        gelu, _ = _gelu_and_grad(ga)
        y_ref[:, 0:w] = _mx(gelu * h)

        vbg = vb * jax.nn.sigmoid(gb)
        ext31[_HALO:_HALO + tt, :] = vbg
        _make_shifted(ext31, sh31)
        vbc = cb_ref[...] + _conv_taps_shifted(ext31, sh31, cw_ref, _HALO - (kc - 1), kc, tt)
        vbc_ref[...] = vbc
        ext31[0:_HALO, :] = vbg[tt - _HALO:tt, :]
        inv = 1.0 / (w // _N_HEADS)
        zc = vbc - _seg_sum(vbc, seg_ref[...]) * inv
        n = zc * lax.rsqrt(_seg_sum(zc * zc, seg_ref[...], 2) * inv + _LN_EPS)
        pre = n * ng_ref[...] + nb_ref[...]
        y_ref[:, w:2 * w] = _mx(pre * jax.nn.sigmoid(pre))

        mix = _dot(y_ref[...], wout_ref[...])
        mix_ref[...] = mix
        n1, _ = _layer_norm_stats(_ALPHA * xt + (1.0 + gt1) * mix)
        x1_ref[...] = n1 * g1_ref[...] + b1_ref[...]

    tok = lambda c: pl.BlockSpec((None, tt, c), lambda b, s: (b, s, 0))
    smalls = [lcw, lcb, wr_bd, wi_bd, b_r, b_i, lam, cw, cb, ng, nb, seg, wout, ln1g, ln1b]
    nx = len(shards)
    return pl.pallas_call(
        _fused_exchange(body, 3 + len(smalls), 8, 4, nx, nx, _gather_plan, (bl, ns)), grid=(bl, ns),
        in_specs=[tok(d), pl.BlockSpec((None, 1, 6 * d), lambda b, s: (b, 0, 0)), _resident(win.shape)]
        + [_resident(t.shape) for t in smalls] + [_HBM] * nx,
        out_specs=[tok(4 * w), tok(w), tok(d), tok(d), tok(d), tok(d), tok(w), tok(5 * w)] + [_HBM] * nx,
        out_shape=[jax.ShapeDtypeStruct((bl, s_len, 4 * w), _F32), jax.ShapeDtypeStruct((bl, s_len, w), _F32),
                   jax.ShapeDtypeStruct((bl, s_len, d), _F32), jax.ShapeDtypeStruct((bl, s_len, d), _F32),
                   jax.ShapeDtypeStruct((bl, s_len, d), _MXU_DT), jax.ShapeDtypeStruct((bl, s_len, d), _MXU_DT),
                   jax.ShapeDtypeStruct((bl, s_len, w), _F32), jax.ShapeDtypeStruct((bl, s_len, 5 * w), _F32)]
        + [jax.ShapeDtypeStruct((4,) + t.shape, t.dtype) for t in shards],
        scratch_shapes=[pltpu.VMEM((tt + 8, w), _F32), pltpu.VMEM((tt + _HALO, w), _F32),
                        pltpu.VMEM((7, tt + _HALO - 8, w), _F32), pltpu.VMEM((8, w), _F32)] + _gather_sems(nx),
        compiler_params=_cparams(("arbitrary", "arbitrary")), name="mix_fwd",
    )(x, mod3, win, *smalls, *shards)


def _ffn_fwd(x1, mod3, wup, fcw, fcb, wdn, ln2g, ln2b, target):
    bl, s_len, d = x1.shape
    nch, _, fc = wup.shape
    nch //= 2
    f = nch * fc
    tt = min(_TT_FFN, s_len)
    ns = s_len // tt

    def body(x1_ref, mod_ref, wup_ref, fcw_ref, fcb_ref, wdn_ref, g2_ref, b2_ref, tgt_ref,
             u2_ref, hh_ref, f_ref, gc_ref, dz2_ref, loss_ref, dln2_ref, dgt2_ref, ext3):
        first_tile = pl.program_id(1) == 0

        @pl.when(first_tile)
        def _():
            ext3[:, 0:8, :] = jnp.zeros((nch, 8, fc), _F32)
            dgt2_ref[...] = jnp.zeros_like(dgt2_ref)

        @pl.when(first_tile & (pl.program_id(0) == 0))
        def _():
            loss_ref[...] = jnp.zeros_like(loss_ref)
            dln2_ref[...] = jnp.zeros_like(dln2_ref)

        x1t = x1_ref[...]
        sh2, sc2, gt2 = mod_ref[:, 3 * d:4 * d], mod_ref[:, 4 * d:5 * d], mod_ref[:, 5 * d:6 * d]
        u2 = _mx(x1t * (1.0 + sc2) + sh2)
        u2_ref[...] = u2
        y2 = jnp.zeros((tt, d), _F32)
        for j in range(nch):
            lanes = slice(j * fc, (j + 1) * fc)
            v = _dot(u2, wup_ref[j])
            g = _dot(u2, wup_ref[nch + j])
            hh_ref[:, lanes] = v.astype(hh_ref.dtype)
            hh_ref[:, f + j * fc:f + (j + 1) * fc] = g.astype(hh_ref.dtype)
            ext = ext3.at[j]
            ext[8:8 + tt, :] = g
            gc = fcb_ref[:, lanes] + sum(fcw_ref[k:k + 1, lanes] * ext[pl.ds(6 + k, tt), :] for k in range(3))
            gc_ref[:, lanes] = gc
            ext[0:8, :] = g[tt - 8:tt, :]
            fj = _mx(gc * jax.nn.sigmoid(gc) * v)
            f_ref[:, lanes] = fj
            y2 = y2 + _dot(fj, wdn_ref[lanes, :])

        n2, rstd = _layer_norm_stats(_ALPHA * x1t + (1.0 + gt2) * y2)
        err = n2 * g2_ref[...] + b2_ref[...] - tgt_ref[...]
        loss_ref[...] += jnp.sum(_rowsum(err * err), axis=1, keepdims=True)
        dout = err * (1.0 / d)
        dln2_ref[0:1, :] += _rowsum(dout * n2)
        dln2_ref[1:2, :] += _rowsum(dout)
        dz2 = _layer_norm_bwd(dout * g2_ref[...], n2, rstd)
        dz2_ref[...] = dz2
        dgt2_ref[...] += _rowsum(dz2 * y2)

    tok = lambda c: pl.BlockSpec((None, tt, c), lambda b, s: (b, s, 0))
    acc = lambda r: pl.BlockSpec((r, d), lambda b, s: (0, 0))
    smalls = [fcw, fcb, wdn, ln2g, ln2b]
    return pl.pallas_call(
        body, grid=(bl, ns),
        in_specs=[tok(d), pl.BlockSpec((None, 1, 6 * d), lambda b, s: (b, 0, 0)), _resident(wup.shape)]
        + [_resident(t.shape) for t in smalls] + [tok(d)],
        out_specs=[tok(d), tok(2 * f), tok(f), tok(f), tok(d), acc(1), acc(2), pl.BlockSpec((None, 1, d), lambda b, s: (b, 0, 0))],
        out_shape=[jax.ShapeDtypeStruct((bl, s_len, d), _MXU_DT), jax.ShapeDtypeStruct((bl, s_len, 2 * f), _F32),
                   jax.ShapeDtypeStruct((bl, s_len, f), _MXU_DT), jax.ShapeDtypeStruct((bl, s_len, f), _F32),
                   jax.ShapeDtypeStruct((bl, s_len, d), _F32), jax.ShapeDtypeStruct((1, d), _F32), jax.ShapeDtypeStruct((2, d), _F32),
                   jax.ShapeDtypeStruct((bl, 1, d), _F32)],
        scratch_shapes=[pltpu.VMEM((nch, tt + 8, fc), _F32)],
        compiler_params=_cparams(("arbitrary", "arbitrary")), name="ffn_fwd",
    )(x1, mod3, wup, *smalls, target)


def _ffn_bwd(dz2, x1, hh, gc_all, mod3, wup, wdn, fcw, fcb):
    bl, s_len, d = x1.shape
    nch, _, fc = wup.shape
    nch //= 2
    f = nch * fc
    tt = min(_TT_FFN, s_len)
    ns = s_len // tt

    def body(dz2_ref, x1_ref, hh_ref, gc_ref, mod_ref, wup_ref, wdn_ref, fcw_ref, fcb_ref,
             dx1_ref, dy2_ref, dh_ref, dfc_ref, dmod_ref, dext, dcar):
        @pl.when(pl.program_id(1) == 0)
        def _():
            dcar[...] = jnp.zeros_like(dcar)
            dmod_ref[...] = jnp.zeros_like(dmod_ref)

        @pl.when((pl.program_id(1) == 0) & (pl.program_id(0) == 0))
        def _():
            dfc_ref[...] = jnp.zeros_like(dfc_ref)

        sc2, gt2 = mod_ref[:, 4 * d:5 * d], mod_ref[:, 5 * d:6 * d]
        dz2t = dz2_ref[...]
        dy2 = _mx((1.0 + gt2) * dz2t)
        dy2_ref[...] = dy2
        du2 = jnp.zeros((tt, d), _F32)
        for j in range(nch):
            lanes = slice(j * fc, (j + 1) * fc)
            glanes = slice(f + j * fc, f + (j + 1) * fc)
            v = hh_ref[:, lanes].astype(_F32)
            g = hh_ref[:, glanes].astype(_F32)
            gc = gc_ref[:, lanes]
            sg = jax.nn.sigmoid(gc)
            df = _dot_nt(dy2, wdn_ref[lanes, :])
            dv = df * (gc * sg)
            dgc = df * v * (sg * (1.0 + gc * (1.0 - sg)))
            dfc_ref[3:4, lanes] += _rowsum_halving(dgc)
            dext[0:tt, :] = dgc
            dext[tt:tt + 8, :] = dcar[j]
            dcar[j] = dgc[0:8, :]
            dg = jnp.zeros((tt, fc), _F32)
            for k in range(3):
                shifted = dext[pl.ds(2 - k, tt), :]
                dg = dg + fcw_ref[k:k + 1, lanes] * shifted
                dfc_ref[k:k + 1, lanes] += _rowsum_halving(shifted * g)
            dvb, dgb = _mx(dv), _mx(dg)
            dh_ref[:, lanes] = dvb
            dh_ref[:, glanes] = dgb
            du2 = du2 + _dot_nt(dvb, wup_ref[j]) + _dot_nt(dgb, wup_ref[nch + j])

        dx1_ref[...] = _ALPHA * dz2t + du2 * (1.0 + sc2)
        dmod_ref[0:1, :] += _rowsum_halving(du2)
        dmod_ref[1:2, :] += _rowsum_halving(du2 * x1_ref[...])

    tok = lambda c: pl.BlockSpec((None, tt, c), lambda b, i: (b, ns - 1 - i, 0))
    return pl.pallas_call(
        body, grid=(bl, ns),
        in_specs=[tok(d), tok(d), tok(2 * f), tok(f), pl.BlockSpec((None, 1, 6 * d), lambda b, i: (b, 0, 0)),
                  _resident(wup.shape), _resident(wdn.shape), _resident(fcw.shape), _resident(fcb.shape)],
        out_specs=[tok(d), tok(d), tok(2 * f), pl.BlockSpec((4, f), lambda b, i: (0, 0)),
                   pl.BlockSpec((None, 2, d), lambda b, i: (b, 0, 0))],
        out_shape=[jax.ShapeDtypeStruct((bl, s_len, d), _F32), jax.ShapeDtypeStruct((bl, s_len, d), _MXU_DT),
                   jax.ShapeDtypeStruct((bl, s_len, 2 * f), _MXU_DT), jax.ShapeDtypeStruct((4, f), _F32),
                   jax.ShapeDtypeStruct((bl, 2, d), _F32)],
        scratch_shapes=[pltpu.VMEM((tt + 8, fc), _F32), pltpu.VMEM((nch, 8, fc), _F32)],
        compiler_params=_cparams(("arbitrary", "arbitrary")), name="ffn_bwd",
    )(dz2, x1, hh, gc_all, mod3, wup, wdn, fcw, fcb)


def _mix_bwd(dx1, x, mix, proj, h, vbc, lru, mod3, win, lcw, lcb, wr_bd, wi_bd, b_r, b_i, lam, cw, cb, ng, nb, seg, wout, ln1g, chip_sums):
    bl, s_len, d = x.shape
    w = d // 2
    tt = min(_TT_MIX, s_len)
    ns = s_len // tt
    kc = cw.shape[0]

    def body(dx1_ref, x_ref, mix_ref, proj_ref, phalo_ref, h_ref, hhalo_ref, vbc_ref, lru_ref, mod_ref, win_ref, lcw_ref, lcb_ref,
             wr_ref, wi_ref, br_ref, bi_ref, lam_ref, cw_ref, cb_ref, ng_ref, nb_ref, seg_ref, wout_ref, g1_ref,
             gx_ref, dproj_ref, dmix_ref, xcg_ref, vecw_ref, dlcw_ref, dcw_ref, dln1_ref, dmod_ref,
             ext4, ext31, dext4, dext31, sh31, dsh31, car4, car31, gcar):
        s = ns - 1 - pl.program_id(1)
        first = s == 0

        @pl.when(pl.program_id(1) == 0)
        def _():
            car4[...] = jnp.zeros_like(car4)
            car31[...] = jnp.zeros_like(car31)
            gcar[...] = jnp.zeros_like(gcar)
            dmod_ref[...] = jnp.zeros_like(dmod_ref)

        @pl.when((pl.program_id(1) == 0) & (pl.program_id(0) == 0))
        def _():
            for ref in (vecw_ref, dlcw_ref, dcw_ref, dln1_ref):
                ref[...] = jnp.zeros_like(ref)

        xt, mixt = x_ref[...], mix_ref[...]
        sh1, sc1, gt1 = mod_ref[:, 0:d], mod_ref[:, d:2 * d], mod_ref[:, 2 * d:3 * d]

        n1, rstd1 = _layer_norm_stats(_ALPHA * xt + (1.0 + gt1) * mixt)
        dx1t = dx1_ref[...]
        dln1_ref[0:1, :] += _rowsum(dx1t * n1)
        dln1_ref[1:2, :] += _rowsum(dx1t)
        dz1 = _layer_norm_bwd(dx1t * g1_ref[...], n1, rstd1)
        dmod_ref[2:3, :] += _rowsum(dz1 * mixt)
        dmix = _mx((1.0 + gt1) * dz1)
        dmix_ref[...] = dmix
        dya = _dot_nt(dmix, wout_ref[0:w, :])
        dyb = _dot_nt(dmix, wout_ref[w:2 * w, :])

        xa, ga = proj_ref[:, 0:w], proj_ref[:, w:2 * w]
        vb, gb = proj_ref[:, 2 * w:3 * w], proj_ref[:, 3 * w:4 * w]

        sgb = jax.nn.sigmoid(gb)
        vbg = vb * sgb
        hv, hg = phalo_ref[:, 2 * w:3 * w], phalo_ref[:, 3 * w:4 * w]
        ext31[0:_HALO, :] = jnp.where(first, 0.0, hv * jax.nn.sigmoid(hg))
        ext31[_HALO:_HALO + tt, :] = vbg
        _make_shifted(ext31, sh31)
        vbc = vbc_ref[...]
        inv = 1.0 / (w // _N_HEADS)
        zc = vbc - _seg_sum(vbc, seg_ref[...]) * inv
        rstd = lax.rsqrt(_seg_sum(zc * zc, seg_ref[...], 2) * inv + _LN_EPS)
        n = zc * rstd
        pre = n * ng_ref[...] + nb_ref[...]
        sgp = jax.nn.sigmoid(pre)
        dpre = dyb * (sgp * (1.0 + pre * (1.0 - sgp)))
        vecw_ref[5:6, :] += _rowsum(dpre * n)
        vecw_ref[6:7, :] += _rowsum(dpre)
        dn = dpre * ng_ref[...]
        dvbc = rstd * (dn - _seg_sum(dn, seg_ref[...], 2) * inv - n * (_seg_sum(dn * n, seg_ref[...], 2) * inv))
        vecw_ref[4:5, :] += _rowsum(dvbc)
        dext31[0:tt, :] = dvbc
        dext31[tt:tt + _HALO, :] = car31[...]
        car31[...] = dvbc[0:_HALO, :]
        _make_shifted(dext31, dsh31)
        dvbg = jnp.zeros((tt, w), _F32)
        for k in range(kc):
            dvbg = dvbg + cw_ref[k:k + 1, :] * _tap(dext31, dsh31, kc - 1 - k, tt)
            dcw_ref[k:k + 1, :] += _rowsum(dvbc * _tap(ext31, sh31, _HALO - (kc - 1) + k, tt))
        dproj_ref[:, 2 * w:3 * w] = _mx(dvbg * sgb)
        dproj_ref[:, 3 * w:4 * w] = _mx(dvbg * vb * (sgb * (1.0 - sgb)))

        ext4[0:8, :] = jnp.where(first, 0.0, phalo_ref[_HALO - 8:_HALO, 0:w])
        ext4[8:8 + tt, :] = xa
        xc, r, i, a, mult = (lru_ref[:, k * w:(k + 1) * w] for k in range(5))
        xcg_ref[:, 0:w] = _mx(xc)
        sp = _softplus(-lam_ref[...])
        ht = h_ref[...]
        row = lax.broadcasted_iota(jnp.int32, (tt, w), 0)
        h_before = jnp.where(first, 0.0, hhalo_ref[7:8, :])
        hprev = jnp.where(row == 0, h_before, pltpu.roll(ht, 1, 0))
        gelu, dgelu = _gelu_and_grad(ga)
        dproj_ref[:, w:2 * w] = _mx(dya * ht * dgelu)
        dh = dya * gelu
        coef = jnp.where(row == tt - 1, 1.0, pltpu.roll(a, tt - 1, 0))
        big_g = _scan_rev(coef, dh, gcar[0:1, :])
        gcar[0:1, :] = a[0:1, :] * big_g[0:1, :]
        da = big_g * hprev
        ixc = i * xc
        dlog_a = da * a - (big_g * ixc) * (a * a / mult)
        di = big_g * mult * xc
        dxc = big_g * mult * i
        vecw_ref[3:4, :] += _rowsum(dlog_a * r) * (_LRU_C * jax.nn.sigmoid(-lam_ref[...]))
        dgr_f = dlog_a * (-_LRU_C * sp) * (r * (1.0 - r))
        dgi_f = di * (i * (1.0 - i))
        vecw_ref[1:2, :] += _rowsum(dgr_f)
        vecw_ref[2:3, :] += _rowsum(dgi_f)
        dgr, dgi = _mx(dgr_f), _mx(dgi_f)
        xcg_ref[:, w:2 * w] = dgr
        xcg_ref[:, 2 * w:3 * w] = dgi
        dxc = dxc + _dot_nt(dgr, wr_ref[...]) + _dot_nt(dgi, wi_ref[...])
        vecw_ref[0:1, :] += _rowsum(dxc)
        dext4[0:tt, :] = dxc
        dext4[tt:tt + 8, :] = car4[...]
        car4[...] = dxc[0:8, :]
        dxa = jnp.zeros((tt, w), _F32)
        for k in range(4):
            dxa = dxa + lcw_ref[k:k + 1, :] * dext4[pl.ds(3 - k, tt), :]
            dlcw_ref[k:k + 1, :] += _rowsum(dxc * ext4[pl.ds(5 + k, tt), :])
        dproj_ref[:, 0:w] = _mx(dxa)

        du1 = sum(_dot_nt(dproj_ref[:, k * w:(k + 1) * w], win_ref[k]) for k in range(4))
        gx_ref[...] = _ALPHA * dz1 + du1 * (1.0 + sc1)
        dmod_ref[0:1, :] += _rowsum(du1)
        dmod_ref[1:2, :] += _rowsum(du1 * xt)

    tok = lambda c: pl.BlockSpec((None, tt, c), lambda b, i: (b, ns - 1 - i, 0))
    halo = lambda rows, c: pl.BlockSpec(
        (None, rows, c), lambda b, i: (b, jnp.maximum((ns - 1 - i) * (tt // rows) - 1, 0), 0))
    accw = lambda r, c: pl.BlockSpec((r, c), lambda b, i: (0, 0))
    smalls = [lcw, lcb, wr_bd, wi_bd, b_r, b_i, lam, cw, cb, ng, nb, seg, wout, ln1g]
    nx = len(chip_sums)
    return pl.pallas_call(
        _fused_exchange(body, 11 + len(smalls), 9, 9, nx, nx, _chip_reduce_plan, (bl, ns)), grid=(bl, ns),
        in_specs=[tok(d), tok(d), tok(d), tok(4 * w), halo(_HALO, 4 * w), tok(w), halo(8, w), tok(w), tok(5 * w),
                  pl.BlockSpec((None, 1, 6 * d), lambda b, i: (b, 0, 0)), _resident(win.shape)]
        + [_resident(t.shape) for t in smalls] + [_HBM] * nx,
        out_specs=[tok(d), tok(4 * w), tok(d), tok(3 * w), accw(8, w), accw(4, w), accw(kc, w), accw(2, d),
                   pl.BlockSpec((None, 3, d), lambda b, i: (b, 0, 0))] + [_HBM] * nx,
        out_shape=[jax.ShapeDtypeStruct((bl, s_len, d), _F32), jax.ShapeDtypeStruct((bl, s_len, 4 * w), _MXU_DT),
                   jax.ShapeDtypeStruct((bl, s_len, d), _MXU_DT), jax.ShapeDtypeStruct((bl, s_len, 3 * w), _MXU_DT),
                   jax.ShapeDtypeStruct((8, w), _F32), jax.ShapeDtypeStruct((4, w), _F32),
                   jax.ShapeDtypeStruct((kc, w), _F32), jax.ShapeDtypeStruct((2, d), _F32),
                   jax.ShapeDtypeStruct((bl, 3, d), _F32)]
        + [jax.ShapeDtypeStruct((3,) + t.shape[1:], t.dtype) for t in chip_sums],
        scratch_shapes=[pltpu.VMEM((tt + 8, w), _F32), pltpu.VMEM((tt + _HALO, w), _F32),
                        pltpu.VMEM((tt + 8, w), _F32), pltpu.VMEM((tt + _HALO, w), _F32),
                        pltpu.VMEM((7, tt + _HALO - 8, w), _F32), pltpu.VMEM((7, tt + _HALO - 8, w), _F32),
                        pltpu.VMEM((8, w), _F32), pltpu.VMEM((_HALO, w), _F32), pltpu.VMEM((8, w), _F32)]
        + _chip_reduce_sems(nx),
        compiler_params=_cparams(("arbitrary", "arbitrary")), name="mix_bwd",
    )(dx1, x, mix, proj, proj, h, h, vbc, lru, mod3, win, *smalls, *chip_sums)


def _wgrad(a, b, ma, nbw, na, nb, a_off, b_off, name, exchange=None):
    t = a.shape[0]
    tk = min(_TK_WGRAD, t)
    grid = (na * nb, t // tk)

    def body(a_ref, b_ref, o_ref):
        @pl.when(pl.program_id(1) == 0)
        def _():
            o_ref[...] = jnp.zeros_like(o_ref)
        o_ref[...] += _dot_tn(a_ref[...], b_ref[...])

    xin, xshapes, plan, sems = exchange if exchange else ([], [], None, [])
    nx = len(xin)
    res = pl.pallas_call(
        _fused_exchange(body, 2, 1, 0, nx, len(xshapes), plan, grid) if exchange else body, grid=grid,
        in_specs=[pl.BlockSpec((tk, ma), lambda j, k: (k, j // nb + a_off)),
                  pl.BlockSpec((tk, nbw), lambda j, k: (k, j % nb + b_off))] + [_HBM] * nx,
        out_specs=[pl.BlockSpec((None, ma, nbw), lambda j, k: (j, 0, 0))] + [_HBM] * len(xshapes),
        out_shape=[jax.ShapeDtypeStruct((na * nb, ma, nbw), _F32)] + list(xshapes),
        scratch_shapes=list(sems),
        compiler_params=_cparams(("arbitrary", "arbitrary")), name=name,
    )(a, b, *xin)
    return res if exchange else res[0]


_DEV_DELTAS = tuple(dl for dl in itertools.product((0, 1), repeat=3) if any(dl))
_HBM = pl.BlockSpec(memory_space=pltpu.HBM)
_VMEM = pl.BlockSpec(memory_space=pltpu.VMEM)


def _pos():
    return lax.axis_index("x"), lax.axis_index("y"), lax.axis_index("c")


def _flip(v, delta):
    return 1 - v if delta else v


def _remote(src, dst, ssem, rsem, dev):
    return pltpu.make_async_remote_copy(src_ref=src, dst_ref=dst, send_sem=ssem, recv_sem=rsem,
                                        device_id=dev, device_id_type=_MESH)


def _rows(ref, idx, n):
    return ref.at[pl.ds(pl.multiple_of(idx * n, 8), n)]


def _ada_fwd(c8, w_ada_k, b_ada_k, shards):
    rows, d = c8.shape
    nk = w_ada_k.shape[1]
    n = len(shards)

    def body(*refs):
        c_ref, w_ref, b_ref = refs[:3]
        call_ref, mod_ref = refs[3 + n:5 + n]
        modloc, modrcv, s1, r1, s2, r2 = refs[5 + 2 * n:11 + 2 * n]
        gather_start, gather_finish = _gather_plan(refs[3:3 + n], refs[5 + n:5 + 2 * n], *refs[11 + 2 * n:14 + 2 * n],
                                                   fsem=refs[14 + 2 * n], frsem=refs[15 + 2 * n], bounce=refs[16 + 2 * n:])
        gather_start()
        xi, yi, ci = _pos()
        me, kme = 4 * xi + 2 * yi + ci, 2 * xi + yi
        call_ref[pl.ds(pl.multiple_of(me * rows, 8), rows), :] = c_ref[...]
        sends = []
        for p, (dx, dy, dc) in enumerate(_DEV_DELTAS):
            cp = _remote(c_ref, _rows(call_ref, me, rows), s1.at[p], r1.at[p], (_flip(xi, dx), _flip(yi, dy), _flip(ci, dc)))
            cp.start()
            sends.append(cp)
        for p, (dx, dy, dc) in enumerate(_DEV_DELTAS):
            src = 4 * _flip(xi, dx) + 2 * _flip(yi, dy) + _flip(ci, dc)
            _remote(c_ref, _rows(call_ref, src, rows), s1.at[p], r1.at[p], (xi, yi, ci)).wait_recv()
        for cp in sends:
            cp.wait_send()

        ca = call_ref[...]
        modloc[...] = _dot(_mx(ca * jax.nn.sigmoid(ca)), _mx(w_ref[...])) + b_ref[...]
        modrcv[kme] = modloc[pl.ds(pl.multiple_of(me * rows, 8), rows), :]
        sends = []
        for j, (dx, dy) in enumerate(_CHIP_DELTAS):
            tx, ty = _flip(xi, dx), _flip(yi, dy)
            cp = _remote(_rows(modloc, 4 * tx + 2 * ty + ci, rows), modrcv.at[kme], s2.at[j], r2.at[j], (tx, ty, ci))
            cp.start()
            sends.append(cp)
        for j, (dx, dy) in enumerate(_CHIP_DELTAS):
            ksrc = 2 * _flip(xi, dx) + _flip(yi, dy)
            _remote(_rows(modloc, me, rows), modrcv.at[ksrc], s2.at[j], r2.at[j], (xi, yi, ci)).wait_recv()
        for cp in sends:
            cp.wait_send()
        for j in range(4):
            mod_ref[:, j * nk:(j + 1) * nk] = modrcv[j]
        gather_finish()

    return pl.pallas_call(
        body, in_specs=[_VMEM, _VMEM, _VMEM] + [_HBM] * n, out_specs=[_VMEM, _VMEM] + [_HBM] * n,
        out_shape=[jax.ShapeDtypeStruct((8 * rows, d), _F32), jax.ShapeDtypeStruct((rows, 4 * nk), _F32)]
        + [jax.ShapeDtypeStruct((4,) + a.shape, a.dtype) for a in shards],
        scratch_shapes=[pltpu.VMEM((8 * rows, nk), _F32), pltpu.VMEM((4, rows, nk), _F32),
                        pltpu.SemaphoreType.DMA((7,)), pltpu.SemaphoreType.DMA((7,)),
                        pltpu.SemaphoreType.DMA((3,)), pltpu.SemaphoreType.DMA((3,))]
        + _gather_sems(n) + [pltpu.SemaphoreType.DMA((3, n)), pltpu.SemaphoreType.DMA((3, n))]
        + [pltpu.VMEM(a.shape, a.dtype) for a in shards],
        compiler_params=pltpu.CompilerParams(vmem_limit_bytes=_VMEM_LIMIT), name="ada_fwd",
    )(c8, w_ada_k, b_ada_k, *shards)


def _gather_sems(n):
    return [pltpu.SemaphoreType.DMA((3, n)), pltpu.SemaphoreType.DMA((3, n)), pltpu.SemaphoreType.DMA((n,))]


def _gather_plan(ins, outs, ssem, rsem, lsem, bounce=(), fsem=None, frsem=None):
    n = len(ins)
    xi, yi, ci = _pos()
    kme = 2 * xi + yi
    split = [fsem is not None and ins[a].shape[0] % 32 == 0 for a in range(n)]

    def half(ref, a, which):
        r2 = ins[a].shape[0] // 2
        return ref.at[pl.ds(pl.multiple_of(which * r2, 16), r2)]

    staged = [pltpu.make_async_copy(ins[a], bounce[a], lsem.at[a]) for a in range(len(bounce))]
    local = [pltpu.make_async_copy(bounce[a] if bounce else ins[a], outs[a].at[kme], lsem.at[a]) for a in range(n)]
    sends, recvs, forwards, handed = [], [], [], []
    for j, (dx, dy) in enumerate(_CHIP_DELTAS):
        tx, ty = _flip(xi, dx), _flip(yi, dy)
        for a in range(n):
            sems = (ssem.at[j, a], rsem.at[j, a])
            landing = outs[a].at[2 * tx + ty]
            if split[a]:
                sends.append(_remote(half(ins[a], a, ci), half(outs[a].at[kme], a, ci), *sems, (tx, ty, ci)))
                recvs.append(_remote(half(ins[a], a, ci), half(landing, a, ci), *sems, (xi, yi, ci)))
                fsems = (fsem.at[j, a], frsem.at[j, a])
                forwards.append(_remote(half(landing, a, ci), half(landing, a, ci), *fsems, (xi, yi, 1 - ci)))
                handed.append(_remote(half(ins[a], a, 1 - ci), half(landing, a, 1 - ci), *fsems, (xi, yi, ci)))
            else:
                sends.append(_remote(ins[a], outs[a].at[kme], *sems, (tx, ty, ci)))
                recvs.append(_remote(ins[a], landing, *sems, (xi, yi, ci)))
                forwards.append(None)

    def start():
        for cp in sends + staged:
            cp.start()
        for cp in staged:
            cp.wait()
        for cp in local:
            cp.start()

    def finish():
        for arrived, forward in zip(recvs, forwards):
            arrived.wait_recv()
            if forward is not None:
                forward.start()
        for cp in handed:
            cp.wait_recv()
        for cp in sends + [f for f in forwards if f is not None]:
            cp.wait_send()
        for cp in local:
            cp.wait()

    return start, finish


def _dev_gather_sems(n):
    return [pltpu.SemaphoreType.DMA((7, n)), pltpu.SemaphoreType.DMA((7, n)), pltpu.SemaphoreType.DMA((n,))]


def _dev_gather_plan(ins, outs, ssem, rsem, lsem):
    n = len(ins)
    xi, yi, ci = _pos()
    me, sibling = 4 * xi + 2 * yi + ci, (xi, yi, 1 - ci)
    local = [pltpu.make_async_copy(ins[a], outs[a].at[me], lsem.at[a]) for a in range(n)]
    sends = [_remote(ins[a], outs[a].at[me], ssem.at[0, a], rsem.at[0, a], sibling) for a in range(n)]
    handed = [_remote(ins[a], outs[a].at[4 * xi + 2 * yi + 1 - ci], ssem.at[0, a], rsem.at[0, a], (xi, yi, ci))
              for a in range(n)]
    arrivals, forwards = [], []
    for j, (dx, dy) in enumerate(_CHIP_DELTAS):
        tx, ty = _flip(xi, dx), _flip(yi, dy)
        for a in range(n):
            over_ici, over_d2d = (ssem.at[1 + j, a], rsem.at[1 + j, a]), (ssem.at[4 + j, a], rsem.at[4 + j, a])
            landing = outs[a].at[4 * tx + 2 * ty + ci]
            sends.append(_remote(ins[a], outs[a].at[me], *over_ici, (tx, ty, ci)))
            arrivals.append(_remote(ins[a], landing, *over_ici, (xi, yi, ci)))
            forwards.append(_remote(landing, landing, *over_d2d, sibling))
            handed.append(_remote(ins[a], outs[a].at[4 * tx + 2 * ty + 1 - ci], *over_d2d, (xi, yi, ci)))

    def start():
        for cp in local + sends:
            cp.start()

    def relay():
        for arrived, forward in zip(arrivals, forwards):
            arrived.wait_recv()
            forward.start()

    def finish():
        for cp in handed:
            cp.wait_recv()
        for cp in sends + forwards:
            cp.wait_send()
        for cp in local:
            cp.wait()

    return start, relay, finish


def _pair_sems(n):
    return [pltpu.SemaphoreType.DMA((n,)), pltpu.SemaphoreType.DMA((n,))]


def _pair_plan(ins, outs, ssem, rsem):
    xi, yi, ci = _pos()
    sends = []
    for a in range(len(ins)):
        r2 = ins[a].shape[1] // 2
        src = ins[a].at[:, pl.ds(pl.multiple_of((1 - ci) * r2, 8), r2), :]
        sends.append(_remote(src, outs[a], ssem.at[a], rsem.at[a], (xi, yi, 1 - ci)))

    def start():
        for cp in sends:
            cp.start()

    def finish():
        for cp in sends:
            cp.wait_recv()
        for cp in sends:
            cp.wait_send()

    return start, finish


def _chip_reduce_sems(n):
    return [pltpu.SemaphoreType.DMA((3, n)), pltpu.SemaphoreType.DMA((3, n))]


def _chip_reduce_plan(ins, outs, ssem, rsem):
    xi, yi, ci = _pos()
    sends = []
    for j, (dx, dy) in enumerate(_CHIP_DELTAS):
        tx, ty = _flip(xi, dx), _flip(yi, dy)
        sends += [_remote(ins[a].at[2 * tx + ty], outs[a].at[j], ssem.at[j, a], rsem.at[j, a], (tx, ty, ci))
                  for a in range(len(ins))]

    def start():
        for cp in sends:
            cp.start()

    def finish():
        for cp in sends:
            cp.wait_recv()
        for cp in sends:
            cp.wait_send()

    return start, finish


def _pair_exchange(gs, name):
    n = len(gs)

    def body(*refs):
        start, finish = _pair_plan(refs[:n], refs[n:2 * n], *refs[2 * n:])
        start()
        finish()

    return pl.pallas_call(
        body, in_specs=[_HBM] * n, out_specs=[_HBM] * n, out_shape=_pair_out_shapes(gs),
        scratch_shapes=_pair_sems(n), name=name,
    )(*gs)


def _pair_out_shapes(gs):
    return [jax.ShapeDtypeStruct((g.shape[0], g.shape[1] // 2, g.shape[2]), g.dtype) for g in gs]


def _row_tile(r):
    return max(t for t in range(8, min(r, 256) + 1, 8) if r % t == 0)


def _pair_add(g, r, cidx, name, wire_dtype=None, exchange=None):
    nk, r2, c = r.shape
    tr = _row_tile(r2)
    nt = r2 // tr
    xin, xshapes, plan, sems = exchange if exchange else ([], [], None, [])
    nx = len(xin)

    def body(c_ref, g_ref, r_ref, *o_refs):
        s = g_ref[...] + r_ref[...]
        for o_ref in o_refs:
            o_ref[...] = s.astype(o_ref.dtype)

    out_spec = pl.BlockSpec((None, tr, c), lambda k, i, cr: (k, i, 0))
    dtypes = [_F32] + ([wire_dtype] if wire_dtype else [])
    res = pl.pallas_call(
        _fused_exchange(body, 3, len(dtypes), 0, nx, len(xshapes), plan, (nk, nt)) if exchange else body,
        grid_spec=pltpu.PrefetchScalarGridSpec(
            num_scalar_prefetch=1, grid=(nk, nt),
            in_specs=[pl.BlockSpec((None, tr, c), lambda k, i, cr: (k, cr[0] * nt + i, 0)), out_spec] + [_HBM] * nx,
            out_specs=[out_spec] * len(dtypes) + [_HBM] * len(xshapes), scratch_shapes=list(sems)),
        out_shape=[jax.ShapeDtypeStruct(r.shape, dt) for dt in dtypes] + list(xshapes),
        compiler_params=_cparams(("arbitrary", "arbitrary")), name=name,
    )(cidx, g, r, *xin)
    return res if wire_dtype or exchange else res[0]


def _chip_exchange(ss):
    n = len(ss)

    def body(*refs):
        start, finish = _chip_reduce_plan(refs[:n], refs[n:2 * n], *refs[2 * n:])
        start()
        finish()

    return pl.pallas_call(
        body, in_specs=[_HBM] * n, out_specs=[_HBM] * n,
        out_shape=[jax.ShapeDtypeStruct((3,) + s.shape[1:], s.dtype) for s in ss],
        scratch_shapes=_chip_reduce_sems(n), name="grad_chip_exchange",
    )(*ss)


def _chip_add(s, r, kidx, name):
    _, r2, c = r.shape
    tr = _row_tile(r2)

    def body(k_ref, s_ref, r_ref, o_ref):
        o_ref[...] = ((s_ref[...] + r_ref[0].astype(_F32)) + r_ref[1].astype(_F32)) + r_ref[2].astype(_F32)

    return pl.pallas_call(
        body, grid_spec=pltpu.PrefetchScalarGridSpec(
            num_scalar_prefetch=1, grid=(r2 // tr,),
            in_specs=[pl.BlockSpec((None, tr, c), lambda i, kr: (kr[0], i, 0)),
                      pl.BlockSpec((3, tr, c), lambda i, kr: (0, i, 0))],
            out_specs=pl.BlockSpec((tr, c), lambda i, kr: (i, 0))),
        out_shape=jax.ShapeDtypeStruct((r2, c), _F32),
        compiler_params=_cparams(("arbitrary",)), name=name,
    )(kidx, s, r)


def _pair_swap_plan(ins, outs, ssem, rsem):
    xi, yi, ci = _pos()
    sends = [_remote(ins[a], outs[a], ssem.at[a], rsem.at[a], (xi, yi, 1 - ci)) for a in range(len(ins))]

    def start():
        for cp in sends:
            cp.start()

    def finish():
        for cp in sends:
            cp.wait_recv()
        for cp in sends:
            cp.wait_send()

    return start, finish


def _pair_swap(hs, name):
    n = len(hs)

    def body(*refs):
        start, finish = _pair_swap_plan(refs[:n], refs[n:2 * n], *refs[2 * n:])
        start()
        finish()

    return pl.pallas_call(
        body, in_specs=[_HBM] * n, out_specs=[_HBM] * n,
        out_shape=[jax.ShapeDtypeStruct(h.shape, h.dtype) for h in hs],
        scratch_shapes=[pltpu.SemaphoreType.DMA((n,)), pltpu.SemaphoreType.DMA((n,))], name=name,
    )(*hs)


def _small_sum(every):
    def body(all_ref, sum_ref):
        tot = all_ref[0]
        for dev in range(1, 8):
            tot = tot + all_ref[dev]
        sum_ref[...] = tot

    return pl.pallas_call(
        body, in_specs=[_VMEM], out_specs=_VMEM, out_shape=jax.ShapeDtypeStruct(every.shape[1:], _F32),
        compiler_params=pltpu.CompilerParams(vmem_limit_bytes=_VMEM_LIMIT), name="small_sum",
    )(every)


def _adamw(w, g, m, v):
    m = _ADAM_B1 * m + (1.0 - _ADAM_B1) * g
    v = _ADAM_B2 * v + (1.0 - _ADAM_B2) * (g * g)
    m_hat = m / (1.0 - _ADAM_B1 ** _ADAM_STEP)
    v_hat = v / (1.0 - _ADAM_B2 ** _ADAM_STEP)
    return -_ADAM_LR * (m_hat / (jnp.sqrt(v_hat) + _ADAM_EPS) + _ADAM_WD * w), m, v


def _adamw_big(w, g_mine, g_theirs, m, v, cidx, name):
    r, c = w.shape
    tr = _row_tile(r // 2)
    nt = r // 2 // tr

    def body(c_ref, w_ref, gm_ref, gt_ref, m_ref, v_ref, g_ref, d_ref, mo_ref, vo_ref):
        g = jnp.where(pl.program_id(0) // nt == c_ref[0], gm_ref[...], gt_ref[...])
        g_ref[...] = g
        d_ref[...], mo_ref[...], vo_ref[...] = _adamw(w_ref[...], g, m_ref[...], v_ref[...])

    spec = pl.BlockSpec((tr, c), lambda i, cr: (i, 0))
    half = pl.BlockSpec((tr, c), lambda i, cr: (i % nt, 0))
    return pl.pallas_call(
        body, grid_spec=pltpu.PrefetchScalarGridSpec(
            num_scalar_prefetch=1, grid=(2 * nt,), in_specs=[spec, half, half, spec, spec], out_specs=[spec] * 4),
        out_shape=[jax.ShapeDtypeStruct((r, c), _F32)] * 4,
        compiler_params=_cparams(("arbitrary",)), name=name,
    )(cidx, w, g_mine, g_theirs, m, v)


def _adamw_small(ws, gs, ms, vs):
    n = len(ws)
    summed = [i for i in range(n) if gs[i].shape != ws[i].shape]

    def body(*refs):
        w_r, g_r, m_r, v_r = (refs[i * n:(i + 1) * n] for i in range(4))
        outs = refs[4 * n:]
        for i in range(n):
            g = g_r[i][...]
            if i in summed:
                g = _rowsum(g)
                outs[3 * n + summed.index(i)][...] = g
            outs[i][...], outs[n + i][...], outs[2 * n + i][...] = _adamw(w_r[i][...], g, m_r[i][...], v_r[i][...])

    shapes = [jax.ShapeDtypeStruct(w.shape, _F32) for w in ws]
    res = pl.pallas_call(
        body, in_specs=[_VMEM] * (4 * n), out_specs=[_VMEM] * (3 * n + len(summed)),
        out_shape=shapes * 3 + [shapes[i] for i in summed],
        compiler_params=pltpu.CompilerParams(vmem_limit_bytes=_VMEM_LIMIT), name="adamw_small",
    )(*ws, *gs, *ms, *vs)
    gs = list(gs)
    for pos, i in enumerate(summed):
        gs[i] = res[3 * n + pos]
    return gs, res[:n], res[n:2 * n], res[2 * n:3 * n]


def _ada_bwd(c_all, dmod_k, w, m, v):
    d, nk = w.shape
    tn = 512 if nk % 512 == 0 else nk

    def body(c_ref, dm_ref, w_ref, m_ref, v_ref, g_ref, d_ref, mo_ref, vo_ref):
        ca = c_ref[...]
        g = _dot_tn(_mx(ca * jax.nn.sigmoid(ca)), _mx(dm_ref[...]))
        g_ref[...] = g
        d_ref[...], mo_ref[...], vo_ref[...] = _adamw(w_ref[...], g, m_ref[...], v_ref[...])

    col = pl.BlockSpec((d, tn), lambda j: (0, j))
    return pl.pallas_call(
        body, grid=(nk // tn,),
        in_specs=[pl.BlockSpec(c_all.shape, lambda j: (0, 0)), pl.BlockSpec((c_all.shape[0], tn), lambda j: (0, j)),
                  col, col, col],
        out_specs=[col] * 4, out_shape=[jax.ShapeDtypeStruct((d, nk), _F32)] * 4,
        compiler_params=_cparams(("arbitrary",)), name="ada_bwd",
    )(c_all, dmod_k, w, m, v)


def _block_diag(wh):
    hn, dh, _ = wh.shape
    eye = jnp.eye(hn, dtype=wh.dtype)
    return (eye[:, None, :, None] * wh[:, :, None, :]).reshape(hn * dh, hn * dh)


def _pack(pieces):
    out = []
    for p in pieces:
        flat = p.reshape(-1, 128)
        out.append(jnp.pad(flat, ((0, (-flat.shape[0]) % 8), (0, 0))))
    return jnp.concatenate(out, axis=0)


def _unpack(pack, shapes):
    out, off = [], 0
    for shp in shapes:
        rows = math.prod(shp) // 128
        out.append(pack[..., off:off + rows, :].reshape(pack.shape[:-2] + tuple(shp)))
        off += rows + (-rows) % 8
    return out


_WEIGHTS = ('w_ada', 'b_ada', 'w_in', 'lru_conv_w', 'lru_conv_b', 'lru_w_r', 'lru_b_r', 'lru_w_i', 'lru_b_i', 'lru_lambda',
            'conv_w', 'conv_b', 'conv_norm_g', 'conv_norm_b', 'w_out', 'ln1_g', 'ln1_b', 'ffn_w_up', 'ffn_conv_w',
            'ffn_conv_b', 'ffn_w_down', 'ln2_g', 'ln2_b')
_BIG = ('w_in', 'w_out', 'ffn_w_up', 'ffn_w_down')


def kernel(x, c, w_ada, b_ada, w_in, lru_conv_w, lru_conv_b, lru_w_r, lru_b_r, lru_w_i, lru_b_i, lru_lambda, conv_w, conv_b, conv_norm_g, conv_norm_b, w_out, ln1_g, ln1_b, ffn_w_up, ffn_conv_w, ffn_conv_b, ffn_w_down, ln2_g, ln2_b, loss_target, m_w_ada, m_b_ada, m_w_in, m_lru_conv_w, m_lru_conv_b, m_lru_w_r, m_lru_b_r, m_lru_w_i, m_lru_b_i, m_lru_lambda, m_conv_w, m_conv_b, m_conv_norm_g, m_conv_norm_b, m_w_out, m_ln1_g, m_ln1_b, m_ffn_w_up, m_ffn_conv_w, m_ffn_conv_b, m_ffn_w_down, m_ln2_g, m_ln2_b, v_w_ada, v_b_ada, v_w_in, v_lru_conv_w, v_lru_conv_b, v_lru_w_r, v_lru_b_r, v_lru_w_i, v_lru_b_i, v_lru_lambda, v_conv_w, v_conv_b, v_conv_norm_g, v_conv_norm_b, v_w_out, v_ln1_g, v_ln1_b, v_ffn_w_up, v_ffn_conv_w, v_ffn_conv_b, v_ffn_w_down, v_ln2_g, v_ln2_b):
    given = dict(locals())
    wt = {n: given[n] for n in _WEIGHTS}
    mo = {n: given["m_" + n] for n in _WEIGHTS}
    vo = {n: given["v_" + n] for n in _WEIGHTS}
    bl, s_len, d = x.shape
    wd = d // 2
    tokens = bl * s_len
    xi, yi, ci = _pos()
    kme = 2 * xi + yi
    kidx = jnp.reshape(kme, (1,)).astype(jnp.int32)
    cidx = jnp.reshape(ci, (1,)).astype(jnp.int32)

    nk = w_ada.shape[2]
    c8 = jnp.pad(c, ((0, 8 - bl), (0, 0)))
    c_all, mod8, win, wout_s, lcw_s, cw_s, fcw_s = _ada_fwd(
        c8, w_ada[0], lax.dynamic_slice(b_ada, (0, kme * nk), (1, nk)),
        [_mx(w_in[0]), _mx(w_out[0]), lru_conv_w[0], conv_w[0], ffn_conv_w[0]])
    mod3 = mod8[:bl].reshape(bl, 1, 6 * d)
    wout = wout_s.reshape(d, d)
    f = 4 * ffn_w_down.shape[1]
    unshard = lambda t: jnp.transpose(t, (1, 0, 2)).reshape(t.shape[1], -1)
    lcw, cw, fcw = unshard(lcw_s), unshard(cw_s), unshard(fcw_s)
    wr_bd, wi_bd = _mx(_block_diag(lru_w_r[0])), _mx(_block_diag(lru_w_i[0]))
    seg = _block_diag(jnp.ones((_N_HEADS, wd // _N_HEADS, wd // _N_HEADS), jnp.bfloat16))
    mixer_small = (lcw, lru_conv_b, wr_bd, wi_bd, lru_b_r, lru_b_i, lru_lambda, cw, conv_b, conv_norm_g, conv_norm_b, seg, wout, ln1_g)

    proj, h, mix, x1, u1, y, vbc, lru, wup, wdn_s = _mix_fwd(x, mod3, win, *mixer_small, ln1_b, [_mx(ffn_w_up[0]), _mx(ffn_w_down[0])])
    wdn = wdn_s.reshape(f, d)
    u2, hh, fact, gc_all, dz2, loss_acc, dln2, dgt2 = _ffn_fwd(x1, mod3, wup, fcw, ffn_conv_b, wdn, ln2_g, ln2_b, loss_target)
    dx1, dy2, dh, dfc, dmod2 = _ffn_bwd(dz2, x1, hh, gc_all, mod3, wup, wdn, fcw, ffn_conv_b)

    flat = lambda t: t.reshape(tokens, t.shape[-1])
    fc = wup.shape[2]
    g_up = _wgrad(flat(u2), flat(dh), d, fc, 1, 4, 0, 0, "wgrad_up")
    g_dn, r_up = _wgrad(flat(fact), flat(dy2), fc, d, f // fc, 1, 0, 0, "wgrad_down",
                        exchange=([g_up], _pair_out_shapes([g_up]), _pair_plan, _pair_sems(1)))
    g_dn = g_dn.reshape(4, f // 4, d)
    s_up, r_dn = _pair_add(g_up, r_up, cidx, "grad_pair_add_ffn_w_up",
                           exchange=([g_dn], _pair_out_shapes([g_dn]), _pair_plan, _pair_sems(1)))
    ffn_sum = [s_up, _pair_add(g_dn, r_dn, cidx, "grad_pair_add_ffn_w_down")]
    grad_x, dproj, dmix, xcg, vecw, dlcw, dcw, dln1, dmod1, *ffn_recv = _mix_bwd(
        dx1, x, mix, proj, h, vbc, lru, mod3, win, *mixer_small, ffn_sum)
    g_ri = _wgrad(flat(xcg), flat(xcg), wd, wd, 1, 2, 0, 1, "wgrad_gates")
    dh_ = wd // _N_HEADS
    on_diagonal = jnp.eye(_N_HEADS, dtype=_F32)[None, :, None, :, None]
    g_ri = jnp.sum(g_ri.reshape(2, _N_HEADS, dh_, _N_HEADS, dh_) * on_diagonal, axis=3)

    dmod = jnp.concatenate([dmod1.reshape(bl, 3 * d), dmod2.reshape(bl, 2 * d), dgt2.reshape(bl, d)], axis=1)
    pieces = [vecw, dlcw, dcw, jnp.concatenate([dln1, dln2], axis=0), dfc, g_ri, loss_acc[:, 0:128],
              jnp.pad(dmod, ((0, 8 - bl), (0, 0)))]
    shapes = [p.shape for p in pieces]
    pack = _pack(pieces)
    g_in, every = _wgrad(flat(u1), flat(dproj), d, wd, 1, 4, 0, 0, "wgrad_in", exchange=(
        [pack], [jax.ShapeDtypeStruct((8,) + pack.shape, _F32)], _dev_gather_plan, _dev_gather_sems(1)))
    wire_shape = lambda t: [jax.ShapeDtypeStruct((3,) + t.shape[1:], t.dtype)]
    ffn_half = [_chip_add(s, r, kidx, "grad_chip_add_" + n) for s, r, n in zip(ffn_sum, ffn_recv, _BIG[2:])]
    same = lambda ts: [jax.ShapeDtypeStruct(t.shape, t.dtype) for t in ts]
    r_in, = _pair_exchange([g_in], "grad_pair_exchange_w_in")
    s_in, wire_in, *ffn_theirs = _pair_add(g_in, r_in, cidx, "grad_pair_add_w_in", jnp.bfloat16,
                                           exchange=(ffn_half, same(ffn_half), _pair_swap_plan, _pair_sems(2)))
    g_out, recv_in = _wgrad(flat(y), flat(dmix), d, d, 1, 1, 0, 0, "wgrad_out", exchange=(
        [wire_in], wire_shape(wire_in), _chip_reduce_plan, _chip_reduce_sems(1)))
    g_out = g_out.reshape(4, d // 4, d)
    r_out, = _pair_exchange([g_out], "grad_pair_exchange_w_out")
    s_out, wire_out = _pair_add(g_out, r_out, cidx, "grad_pair_add_w_out", jnp.bfloat16)
    recv_out, = _chip_exchange([wire_out])
    mix_half = [_chip_add(s, r, kidx, "grad_chip_add_" + n) for s, r, n in zip([s_in, s_out], [recv_in, recv_out], _BIG)]
    half, other = mix_half + ffn_half, list(_pair_swap(mix_half, "grad_pair_swap")) + ffn_theirs
    grads, deltas, new_m, new_v = {}, {}, {}, {}
    for n, mine, theirs in zip(_BIG, half, other):
        g, dl, mm, vv = _adamw_big(wt[n][0], mine, theirs, mo[n][0], vo[n][0], cidx, "adamw_" + n)
        grads[n], deltas[n], new_m[n], new_v[n] = g[None], dl[None], mm[None], vv[None]

    vecw, dlcw, dcw, dln, dfc, g_ri, loss_sum, dmod_sum = _unpack(_small_sum(every), shapes)
    loss = 0.5 * loss_sum[0, 0] / d
    dmod_all = _unpack(every, shapes)[-1].reshape(64, 6 * d)

    g_ada, dl, mm, vv = _ada_bwd(c_all, lax.dynamic_slice(dmod_all, (0, kme * nk), (64, nk)), w_ada[0], m_w_ada[0], v_w_ada[0])
    grads['w_ada'], deltas['w_ada'], new_m['w_ada'], new_v['w_ada'] = g_ada[None], dl[None], mm[None], vv[None]

    shard = lambda t, width: lax.dynamic_slice(t, (0, kme * width), (t.shape[0], width))
    small = {
        'b_ada': dmod_sum, 'lru_conv_w': shard(dlcw, wd // 4), 'lru_conv_b': vecw[0:1], 'lru_w_r': g_ri[0], 'lru_b_r': vecw[1:2],
        'lru_w_i': g_ri[1], 'lru_b_i': vecw[2:3], 'lru_lambda': vecw[3:4], 'conv_w': shard(dcw, wd // 4), 'conv_b': vecw[4:5],
        'conv_norm_g': vecw[5:6], 'conv_norm_b': vecw[6:7], 'ln1_g': dln[0:1], 'ln1_b': dln[1:2],
        'ffn_conv_w': shard(dfc[0:3], f // 4), 'ffn_conv_b': dfc[3:4], 'ln2_g': dln[2:3], 'ln2_b': dln[3:4]}
    names = list(small)
    gs = [small[n] if n == 'b_ada' else small[n].reshape(wt[n].shape) for n in names]
    gs, dls, mms, vvs = _adamw_small([wt[n] for n in names], gs, [mo[n] for n in names], [vo[n] for n in names])
    for n, g, dl, mm, vv in zip(names, gs, dls, mms, vvs):
        grads[n], deltas[n], new_m[n], new_v[n] = g, dl, mm, vv

    return (loss, grad_x, *[grads[n] for n in _WEIGHTS], *[deltas[n] for n in _WEIGHTS],
            *[new_m[n] for n in _WEIGHTS], *[new_v[n] for n in _WEIGHTS])
```

```python
import functools
import itertools
import math

import jax
import jax.numpy as jnp
from jax import lax
from jax.experimental import pallas as pl
from jax.experimental.pallas import tpu as pltpu

_MXU_DT = jnp.bfloat16
_F32 = jnp.float32
_VMEM_LIMIT = 56 * 1024 * 1024
_TT_MIX = 256
_TT_MIX_FWD = 512
_TT_FFN = 256
_TK_WGRAD = 2048
_HALO = 32

_LRU_C = 8.0
_LN_EPS = 1e-5
_N_HEADS = 8
_DEPTH = 1
_ALPHA = (2 * _DEPTH) ** 0.25
_ADAM_LR, _ADAM_B1, _ADAM_B2, _ADAM_EPS, _ADAM_WD, _ADAM_STEP = 0.001, 0.9, 0.999, 1e-08, 0.01, 10

_MESH = pl.DeviceIdType.MESH
_CHIP_DELTAS = ((1, 0), (0, 1), (1, 1))


def _cparams(sem):
    return pltpu.CompilerParams(dimension_semantics=sem, vmem_limit_bytes=_VMEM_LIMIT)


def _resident(shape):
    nd = len(shape)
    return pl.BlockSpec(shape, lambda *_: (0,) * nd, pipeline_mode=pl.Buffered(1))


def _dot(a, b):
    return jnp.dot(a, b, preferred_element_type=_F32)


def _dot_nt(a, b):
    return lax.dot_general(a, b, (((1,), (1,)), ((), ())), preferred_element_type=_F32)


def _dot_tn(a, b):
    return lax.dot_general(a, b, (((0,), (0,)), ((), ())), preferred_element_type=_F32)


def _mx(v):
    return v.astype(_MXU_DT)


def _expm1(v):
    series = v * (1.0 + v * (1.0 / 2 + v * (1.0 / 6)))
    return jnp.where(jnp.abs(v) < 0.015625, series, jnp.exp(v) - 1.0)


def _softplus(z):
    e = jnp.exp(-jnp.abs(z))
    u = 1.0 + e
    log1p = jnp.where(u == 1.0, e, jnp.log(u) * e / jnp.where(u == 1.0, 1.0, u - 1.0))
    return jnp.maximum(z, 0.0) + log1p


_GELU_C = math.sqrt(2.0 / math.pi)


def _gelu_and_grad(v):
    t = jnp.tanh(_GELU_C * (v + 0.044715 * v * v * v))
    val = 0.5 * v * (1.0 + t)
    grad = 0.5 * (1.0 + t) + 0.5 * v * (1.0 - t * t) * _GELU_C * (1.0 + 3 * 0.044715 * v * v)
    return val, grad


def _seg_sum(v, seg, passes=3):
    hi = v.astype(jnp.bfloat16)
    r1 = v - hi.astype(_F32)
    mid = r1.astype(jnp.bfloat16)
    out = _dot(hi, seg) + _dot(mid, seg)
    if passes == 3:
        out = out + _dot((r1 - mid.astype(_F32)).astype(jnp.bfloat16), seg)
    return out


_SCAN_BLOCK = 32


def _scan_fwd(a, u, h0):
    n = a.shape[0]
    blk = min(_SCAN_BLOCK, n)
    sub = lax.broadcasted_iota(jnp.int32, a.shape, 0) % blk
    h, d = u, 1
    while d < blk:
        keep = sub >= d
        h = a * jnp.where(keep, pltpu.roll(h, d, 0), 0.0) + h
        a = a * jnp.where(keep, pltpu.roll(a, d, 0), 1.0)
        d *= 2
    out, carry = [], h0
    for b in range(n // blk):
        rows = slice(b * blk, (b + 1) * blk)
        out.append(h[rows] + a[rows] * carry)
        carry = out[-1][blk - 1:blk, :]
    return jnp.concatenate(out, axis=0)


def _scan_rev(c, g, g_end):
    n = c.shape[0]
    blk = min(_SCAN_BLOCK, n)
    sub = lax.broadcasted_iota(jnp.int32, c.shape, 0) % blk
    d = 1
    while d < blk:
        keep = sub < blk - d
        g = c * jnp.where(keep, pltpu.roll(g, n - d, 0), 0.0) + g
        c = c * jnp.where(keep, pltpu.roll(c, n - d, 0), 1.0)
        d *= 2
    out, carry = [None] * (n // blk), g_end
    for b in reversed(range(n // blk)):
        rows = slice(b * blk, (b + 1) * blk)
        out[b] = g[rows] + c[rows] * carry
        carry = out[b][0:1, :]
    return jnp.concatenate(out, axis=0)


def _layer_norm_stats(z):
    mu = jnp.mean(z, axis=-1, keepdims=True)
    zc = z - mu
    var = jnp.mean(zc * zc, axis=-1, keepdims=True)
    rstd = lax.rsqrt(var + _LN_EPS)
    return zc * rstd, rstd


def _layer_norm_bwd(dn, n, rstd):
    return rstd * (dn - jnp.mean(dn, axis=-1, keepdims=True) - n * jnp.mean(dn * n, axis=-1, keepdims=True))


def _rowsum(v):
    return jnp.sum(v, axis=0, keepdims=True)


def _rowsum_halving(v):
    n = v.shape[0]
    while n > 8 and n % 16 == 0:
        v = v[:n // 2] + v[n // 2:]
        n //= 2
    return jnp.sum(v, axis=0, keepdims=True)


def _fused_exchange(body, n_in, n_out, n_scratch, n_xin, n_xout, plan, grid):
    def wrapped(*refs):
        o0 = n_in + n_xin
        s0 = o0 + n_out + n_xout
        start, finish = plan(refs[n_in:o0], refs[o0 + n_out:s0], *refs[s0 + n_scratch:])
        step = 0
        for axis, size in enumerate(grid):
            step = step * size + pl.program_id(axis)

        @pl.when(step == 0)
        def _():
            start()

        body(*refs[:n_in], *refs[o0:o0 + n_out], *refs[s0:s0 + n_scratch])

        @pl.when(step == math.prod(grid) - 1)
        def _():
            finish()

    return wrapped


def _lru_gates(xc, wr_ref, wi_ref, br_ref, bi_ref, lam_ref):
    xcb = _mx(xc)
    r = jax.nn.sigmoid(_dot(xcb, wr_ref[...]) + br_ref[...])
    i = jax.nn.sigmoid(_dot(xcb, wi_ref[...]) + bi_ref[...])
    sp = _softplus(-lam_ref[...])
    log_a = -_LRU_C * r * sp
    a = jnp.exp(log_a)
    mult = jnp.sqrt(-_expm1(2.0 * log_a))
    return r, i, sp, a, mult


def _conv_taps(ext_ref, w_ref, first, n_taps, tt):
    acc = w_ref[0:1, :] * ext_ref[pl.ds(first, tt), :]
    for k in range(1, n_taps):
        acc = acc + w_ref[k:k + 1, :] * ext_ref[pl.ds(first + k, tt), :]
    return acc


def _make_shifted(ext_ref, sh_ref):
    n = sh_ref.shape[1]
    for r in range(1, 8):
        sh_ref[r - 1] = ext_ref[pl.ds(r, n), :]


def _tap(ext_ref, sh_ref, off, tt):
    base = (off // 8) * 8
    if off % 8 == 0:
        return ext_ref[pl.ds(base, tt), :]
    return sh_ref[off % 8 - 1, pl.ds(base, tt), :]


def _conv_taps_shifted(ext_ref, sh_ref, w_ref, first, n_taps, tt):
    acc = w_ref[0:1, :] * _tap(ext_ref, sh_ref, first, tt)
    for k in range(1, n_taps):
        acc = acc + w_ref[k:k + 1, :] * _tap(ext_ref, sh_ref, first + k, tt)
    return acc


def _mix_fwd(x, mod3, win, lcw, lcb, wr_bd, wi_bd, b_r, b_i, lam, cw, cb, ng, nb, seg, wout, ln1g, ln1b, shards):
    bl, s_len, d = x.shape
    w = d // 2
    tt = min(_TT_MIX_FWD, s_len)
    ns = s_len // tt
    kc = cw.shape[0]

    def body(x_ref, mod_ref, win_ref, lcw_ref, lcb_ref, wr_ref, wi_ref, br_ref, bi_ref, lam_ref, cw_ref, cb_ref,
             ng_ref, nb_ref, seg_ref, wout_ref, g1_ref, b1_ref,
             proj_ref, h_ref, mix_ref, x1_ref, u1_ref, y_ref, vbc_ref, lru_ref, ext4, ext31, sh31, hcar):
        @pl.when(pl.program_id(1) == 0)
        def _():
            ext4[0:8, :] = jnp.zeros((8, w), _F32)
            ext31[0:_HALO, :] = jnp.zeros((_HALO, w), _F32)
            hcar[...] = jnp.zeros_like(hcar)

        xt = x_ref[...]
        sh1, sc1, gt1 = mod_ref[:, 0:d], mod_ref[:, d:2 * d], mod_ref[:, 2 * d:3 * d]
        u1 = _mx(xt * (1.0 + sc1) + sh1)
        u1_ref[...] = u1
        xa, ga, vb, gb = (_dot(u1, win_ref[k]) for k in range(4))
        proj_ref[:, 0:w] = xa
        proj_ref[:, w:2 * w] = ga
        proj_ref[:, 2 * w:3 * w] = vb
        proj_ref[:, 3 * w:4 * w] = gb

        ext4[8:8 + tt, :] = xa
        xc = lcb_ref[...] + _conv_taps(ext4, lcw_ref, 5, 4, tt)
        ext4[0:8, :] = xa[tt - 8:tt, :]
        r, i, sp, a, mult = _lru_gates(xc, wr_ref, wi_ref, br_ref, bi_ref, lam_ref)
        for k, val in enumerate((xc, r, i, a, mult)):
            lru_ref[:, k * w:(k + 1) * w] = val
        h = _scan_fwd(a, mult * (i * xc), hcar[0:1, :])
        hcar[0:1, :] = h[tt - 1:tt, :]
        h_ref[...] = h
        gelu, _ = _gelu_and_grad(ga)
        y_ref[:, 0:w] = _mx(gelu * h)

        vbg = vb * jax.nn.sigmoid(gb)
        ext31[_HALO:_HALO + tt, :] = vbg
        _make_shifted(ext31, sh31)
        vbc = cb_ref[...] + _conv_taps_shifted(ext31, sh31, cw_ref, _HALO - (kc - 1), kc, tt)
        vbc_ref[...] = vbc
        ext31[0:_HALO, :] = vbg[tt - _HALO:tt, :]
        inv = 1.0 / (w // _N_HEADS)
        zc = vbc - _seg_sum(vbc, seg_ref[...]) * inv
        n = zc * lax.rsqrt(_seg_sum(zc * zc, seg_ref[...], 2) * inv + _LN_EPS)
        pre = n * ng_ref[...] + nb_ref[...]
        y_ref[:, w:2 * w] = _mx(pre * jax.nn.sigmoid(pre))

        mix = _dot(y_ref[...], wout_ref[...])
        mix_ref[...] = mix
        n1, _ = _layer_norm_stats(_ALPHA * xt + (1.0 + gt1) * mix)
        x1_ref[...] = n1 * g1_ref[...] + b1_ref[...]

    tok = lambda c: pl.BlockSpec((None, tt, c), lambda b, s: (b, s, 0))
    smalls = [lcw, lcb, wr_bd, wi_bd, b_r, b_i, lam, cw, cb, ng, nb, seg, wout, ln1g, ln1b]
    nx = len(shards)
    return pl.pallas_call(
        _fused_exchange(body, 3 + len(smalls), 8, 4, nx, nx, _gather_plan, (bl, ns)), grid=(bl, ns),
        in_specs=[tok(d), pl.BlockSpec((None, 1, 6 * d), lambda b, s: (b, 0, 0)), _resident(win.shape)]
        + [_resident(t.shape) for t in smalls] + [_HBM] * nx,
        out_specs=[tok(4 * w), tok(w), tok(d), tok(d), tok(d), tok(d), tok(w), tok(5 * w)] + [_HBM] * nx,
        out_shape=[jax.ShapeDtypeStruct((bl, s_len, 4 * w), _F32), jax.ShapeDtypeStruct((bl, s_len, w), _F32),
                   jax.ShapeDtypeStruct((bl, s_len, d), _F32), jax.ShapeDtypeStruct((bl, s_len, d), _F32),
                   jax.ShapeDtypeStruct((bl, s_len, d), _MXU_DT), jax.ShapeDtypeStruct((bl, s_len, d), _MXU_DT),
                   jax.ShapeDtypeStruct((bl, s_len, w), _F32), jax.ShapeDtypeStruct((bl, s_len, 5 * w), _F32)]
        + [jax.ShapeDtypeStruct((4,) + t.shape, t.dtype) for t in shards],
        scratch_shapes=[pltpu.VMEM((tt + 8, w), _F32), pltpu.VMEM((tt + _HALO, w), _F32),
                        pltpu.VMEM((7, tt + _HALO - 8, w), _F32), pltpu.VMEM((8, w), _F32)] + _gather_sems(nx),
        compiler_params=_cparams(("arbitrary", "arbitrary")), name="mix_fwd",
    )(x, mod3, win, *smalls, *shards)


def _ffn_fwd(x1, mod3, wup, fcw, fcb, wdn, ln2g, ln2b, target):
    bl, s_len, d = x1.shape
    nch, _, fc = wup.shape
    nch //= 2
    f = nch * fc
    tt = min(_TT_FFN, s_len)
    ns = s_len // tt

    def body(x1_ref, mod_ref, wup_ref, fcw_ref, fcb_ref, wdn_ref, g2_ref, b2_ref, tgt_ref,
             u2_ref, hh_ref, f_ref, gc_ref, dz2_ref, loss_ref, dln2_ref, dgt2_ref, ext3):
        first_tile = pl.program_id(1) == 0

        @pl.when(first_tile)
        def _():
            ext3[:, 0:8, :] = jnp.zeros((nch, 8, fc), _F32)
            dgt2_ref[...] = jnp.zeros_like(dgt2_ref)

        @pl.when(first_tile & (pl.program_id(0) == 0))
        def _():
            loss_ref[...] = jnp.zeros_like(loss_ref)
            dln2_ref[...] = jnp.zeros_like(dln2_ref)

        x1t = x1_ref[...]
        sh2, sc2, gt2 = mod_ref[:, 3 * d:4 * d], mod_ref[:, 4 * d:5 * d], mod_ref[:, 5 * d:6 * d]
        u2 = _mx(x1t * (1.0 + sc2) + sh2)
        u2_ref[...] = u2
        y2 = jnp.zeros((tt, d), _F32)
        for j in range(nch):
            lanes = slice(j * fc, (j + 1) * fc)
            v = _dot(u2, wup_ref[j])
            g = _dot(u2, wup_ref[nch + j])
            hh_ref[:, lanes] = v.astype(hh_ref.dtype)
            hh_ref[:, f + j * fc:f + (j + 1) * fc] = g.astype(hh_ref.dtype)
            ext = ext3.at[j]
            ext[8:8 + tt, :] = g
            gc = fcb_ref[:, lanes] + sum(fcw_ref[k:k + 1, lanes] * ext[pl.ds(6 + k, tt), :] for k in range(3))
            gc_ref[:, lanes] = gc
            ext[0:8, :] = g[tt - 8:tt, :]
            fj = _mx(gc * jax.nn.sigmoid(gc) * v)
            f_ref[:, lanes] = fj
            y2 = y2 + _dot(fj, wdn_ref[lanes, :])

        n2, rstd = _layer_norm_stats(_ALPHA * x1t + (1.0 + gt2) * y2)
        err = n2 * g2_ref[...] + b2_ref[...] - tgt_ref[...]
        loss_ref[...] += jnp.sum(_rowsum(err * err), axis=1, keepdims=True)
        dout = err * (1.0 / d)
        dln2_ref[0:1, :] += _rowsum(dout * n2)
        dln2_ref[1:2, :] += _rowsum(dout)
        dz2 = _layer_norm_bwd(dout * g2_ref[...], n2, rstd)
        dz2_ref[...] = dz2
        dgt2_ref[...] += _rowsum(dz2 * y2)

    tok = lambda c: pl.BlockSpec((None, tt, c), lambda b, s: (b, s, 0))
    acc = lambda r: pl.BlockSpec((r, d), lambda b, s: (0, 0))
    smalls = [fcw, fcb, wdn, ln2g, ln2b]
    return pl.pallas_call(
        body, grid=(bl, ns),
        in_specs=[tok(d), pl.BlockSpec((None, 1, 6 * d), lambda b, s: (b, 0, 0)), _resident(wup.shape)]
        + [_resident(t.shape) for t in smalls] + [tok(d)],
        out_specs=[tok(d), tok(2 * f), tok(f), tok(f), tok(d), acc(1), acc(2), pl.BlockSpec((None, 1, d), lambda b, s: (b, 0, 0))],
        out_shape=[jax.ShapeDtypeStruct((bl, s_len, d), _MXU_DT), jax.ShapeDtypeStruct((bl, s_len, 2 * f), _F32),
                   jax.ShapeDtypeStruct((bl, s_len, f), _MXU_DT), jax.ShapeDtypeStruct((bl, s_len, f), _F32),
                   jax.ShapeDtypeStruct((bl, s_len, d), _F32), jax.ShapeDtypeStruct((1, d), _F32), jax.ShapeDtypeStruct((2, d), _F32),
                   jax.ShapeDtypeStruct((bl, 1, d), _F32)],
        scratch_shapes=[pltpu.VMEM((nch, tt + 8, fc), _F32)],
        compiler_params=_cparams(("arbitrary", "arbitrary")), name="ffn_fwd",
    )(x1, mod3, wup, *smalls, target)


def _ffn_bwd(dz2, x1, hh, gc_all, mod3, wup, wdn, fcw, fcb):
    bl, s_len, d = x1.shape
    nch, _, fc = wup.shape
    nch //= 2
    f = nch * fc
    tt = min(_TT_FFN, s_len)
    ns = s_len // tt

    def body(dz2_ref, x1_ref, hh_ref, gc_ref, mod_ref, wup_ref, wdn_ref, fcw_ref, fcb_ref,
             dx1_ref, dy2_ref, dh_ref, dfc_ref, dmod_ref, dext, dcar):
        @pl.when(pl.program_id(1) == 0)
        def _():
            dcar[...] = jnp.zeros_like(dcar)
            dmod_ref[...] = jnp.zeros_like(dmod_ref)

        @pl.when((pl.program_id(1) == 0) & (pl.program_id(0) == 0))
        def _():
            dfc_ref[...] = jnp.zeros_like(dfc_ref)

        sc2, gt2 = mod_ref[:, 4 * d:5 * d], mod_ref[:, 5 * d:6 * d]
        dz2t = dz2_ref[...]
        dy2 = _mx((1.0 + gt2) * dz2t)
        dy2_ref[...] = dy2
        du2 = jnp.zeros((tt, d), _F32)
        for j in range(nch):
            lanes = slice(j * fc, (j + 1) * fc)
            glanes = slice(f + j * fc, f + (j + 1) * fc)
            v = hh_ref[:, lanes].astype(_F32)
            g = hh_ref[:, glanes].astype(_F32)
            gc = gc_ref[:, lanes]
            sg = jax.nn.sigmoid(gc)
            df = _dot_nt(dy2, wdn_ref[lanes, :])
            dv = df * (gc * sg)
            dgc = df * v * (sg * (1.0 + gc * (1.0 - sg)))
            dfc_ref[3:4, lanes] += _rowsum_halving(dgc)
            dext[0:tt, :] = dgc
            dext[tt:tt + 8, :] = dcar[j]
            dcar[j] = dgc[0:8, :]
            dg = jnp.zeros((tt, fc), _F32)
            for k in range(3):
                shifted = dext[pl.ds(2 - k, tt), :]
                dg = dg + fcw_ref[k:k + 1, lanes] * shifted
                dfc_ref[k:k + 1, lanes] += _rowsum_halving(shifted * g)
            dvb, dgb = _mx(dv), _mx(dg)
            dh_ref[:, lanes] = dvb
            dh_ref[:, glanes] = dgb
            du2 = du2 + _dot_nt(dvb, wup_ref[j]) + _dot_nt(dgb, wup_ref[nch + j])

        dx1_ref[...] = _ALPHA * dz2t + du2 * (1.0 + sc2)
        dmod_ref[0:1, :] += _rowsum_halving(du2)
        dmod_ref[1:2, :] += _rowsum_halving(du2 * x1_ref[...])

    tok = lambda c: pl.BlockSpec((None, tt, c), lambda b, i: (b, ns - 1 - i, 0))
    return pl.pallas_call(
        body, grid=(bl, ns),
        in_specs=[tok(d), tok(d), tok(2 * f), tok(f), pl.BlockSpec((None, 1, 6 * d), lambda b, i: (b, 0, 0)),
                  _resident(wup.shape), _resident(wdn.shape), _resident(fcw.shape), _resident(fcb.shape)],
        out_specs=[tok(d), tok(d), tok(2 * f), pl.BlockSpec((4, f), lambda b, i: (0, 0)),
                   pl.BlockSpec((None, 2, d), lambda b, i: (b, 0, 0))],
        out_shape=[jax.ShapeDtypeStruct((bl, s_len, d), _F32), jax.ShapeDtypeStruct((bl, s_len, d), _MXU_DT),
                   jax.ShapeDtypeStruct((bl, s_len, 2 * f), _MXU_DT), jax.ShapeDtypeStruct((4, f), _F32),
                   jax.ShapeDtypeStruct((bl, 2, d), _F32)],
        scratch_shapes=[pltpu.VMEM((tt + 8, fc), _F32), pltpu.VMEM((nch, 8, fc), _F32)],
        compiler_params=_cparams(("arbitrary", "arbitrary")), name="ffn_bwd",
    )(dz2, x1, hh, gc_all, mod3, wup, wdn, fcw, fcb)


def _mix_bwd(dx1, x, mix, proj, h, vbc, lru, mod3, win, lcw, lcb, wr_bd, wi_bd, b_r, b_i, lam, cw, cb, ng, nb, seg, wout, ln1g, chip_sums):
    bl, s_len, d = x.shape
    w = d // 2
    tt = min(_TT_MIX, s_len)
    ns = s_len // tt
    kc = cw.shape[0]

    def body(dx1_ref, x_ref, mix_ref, proj_ref, phalo_ref, h_ref, hhalo_ref, vbc_ref, lru_ref, mod_ref, win_ref, lcw_ref, lcb_ref,
             wr_ref, wi_ref, br_ref, bi_ref, lam_ref, cw_ref, cb_ref, ng_ref, nb_ref, seg_ref, wout_ref, g1_ref,
             gx_ref, dproj_ref, dmix_ref, xcg_ref, vecw_ref, dlcw_ref, dcw_ref, dln1_ref, dmod_ref,
             ext4, ext31, dext4, dext31, sh31, dsh31, car4, car31, gcar):
        s = ns - 1 - pl.program_id(1)
        first = s == 0

        @pl.when(pl.program_id(1) == 0)
        def _():
            car4[...] = jnp.zeros_like(car4)
            car31[...] = jnp.zeros_like(car31)
            gcar[...] = jnp.zeros_like(gcar)
            dmod_ref[...] = jnp.zeros_like(dmod_ref)

        @pl.when((pl.program_id(1) == 0) & (pl.program_id(0) == 0))
        def _():
            for ref in (vecw_ref, dlcw_ref, dcw_ref, dln1_ref):
                ref[...] = jnp.zeros_like(ref)

        xt, mixt = x_ref[...], mix_ref[...]
        sh1, sc1, gt1 = mod_ref[:, 0:d], mod_ref[:, d:2 * d], mod_ref[:, 2 * d:3 * d]

        n1, rstd1 = _layer_norm_stats(_ALPHA * xt + (1.0 + gt1) * mixt)
        dx1t = dx1_ref[...]
        dln1_ref[0:1, :] += _rowsum(dx1t * n1)
        dln1_ref[1:2, :] += _rowsum(dx1t)
        dz1 = _layer_norm_bwd(dx1t * g1_ref[...], n1, rstd1)
        dmod_ref[2:3, :] += _rowsum(dz1 * mixt)
        dmix = _mx((1.0 + gt1) * dz1)
        dmix_ref[...] = dmix
        dya = _dot_nt(dmix, wout_ref[0:w, :])
        dyb = _dot_nt(dmix, wout_ref[w:2 * w, :])

        xa, ga = proj_ref[:, 0:w], proj_ref[:, w:2 * w]
        vb, gb = proj_ref[:, 2 * w:3 * w], proj_ref[:, 3 * w:4 * w]

        sgb = jax.nn.sigmoid(gb)
        vbg = vb * sgb
        hv, hg = phalo_ref[:, 2 * w:3 * w], phalo_ref[:, 3 * w:4 * w]
        ext31[0:_HALO, :] = jnp.where(first, 0.0, hv * jax.nn.sigmoid(hg))
        ext31[_HALO:_HALO + tt, :] = vbg
        _make_shifted(ext31, sh31)
        vbc = vbc_ref[...]
        inv = 1.0 / (w // _N_HEADS)
        zc = vbc - _seg_sum(vbc, seg_ref[...]) * inv
        rstd = lax.rsqrt(_seg_sum(zc * zc, seg_ref[...], 2) * inv + _LN_EPS)
        n = zc * rstd
        pre = n * ng_ref[...] + nb_ref[...]
        sgp = jax.nn.sigmoid(pre)
        dpre = dyb * (sgp * (1.0 + pre * (1.0 - sgp)))
        vecw_ref[5:6, :] += _rowsum(dpre * n)
        vecw_ref[6:7, :] += _rowsum(dpre)
        dn = dpre * ng_ref[...]
        dvbc = rstd * (dn - _seg_sum(dn, seg_ref[...], 2) * inv - n * (_seg_sum(dn * n, seg_ref[...], 2) * inv))
        vecw_ref[4:5, :] += _rowsum(dvbc)
        dext31[0:tt, :] = dvbc
        dext31[tt:tt + _HALO, :] = car31[...]
        car31[...] = dvbc[0:_HALO, :]
        _make_shifted(dext31, dsh31)
        dvbg = jnp.zeros((tt, w), _F32)
        for k in range(kc):
            dvbg = dvbg + cw_ref[k:k + 1, :] * _tap(dext31, dsh31, kc - 1 - k, tt)
            dcw_ref[k:k + 1, :] += _rowsum(dvbc * _tap(ext31, sh31, _HALO - (kc - 1) + k, tt))
        dproj_ref[:, 2 * w:3 * w] = _mx(dvbg * sgb)
        dproj_ref[:, 3 * w:4 * w] = _mx(dvbg * vb * (sgb * (1.0 - sgb)))

        ext4[0:8, :] = jnp.where(first, 0.0, phalo_ref[_HALO - 8:_HALO, 0:w])
        ext4[8:8 + tt, :] = xa
        xc, r, i, a, mult = (lru_ref[:, k * w:(k + 1) * w] for k in range(5))
        xcg_ref[:, 0:w] = _mx(xc)
        sp = _softplus(-lam_ref[...])
        ht = h_ref[...]
        row = lax.broadcasted_iota(jnp.int32, (tt, w), 0)
        h_before = jnp.where(first, 0.0, hhalo_ref[7:8, :])
        hprev = jnp.where(row == 0, h_before, pltpu.roll(ht, 1, 0))
        gelu, dgelu = _gelu_and_grad(ga)
        dproj_ref[:, w:2 * w] = _mx(dya * ht * dgelu)
        dh = dya * gelu
        coef = jnp.where(row == tt - 1, 1.0, pltpu.roll(a, tt - 1, 0))
        big_g = _scan_rev(coef, dh, gcar[0:1, :])
        gcar[0:1, :] = a[0:1, :] * big_g[0:1, :]
        da = big_g * hprev
        ixc = i * xc
        dlog_a = da * a - (big_g * ixc) * (a * a / mult)
        di = big_g * mult * xc
        dxc = big_g * mult * i
        vecw_ref[3:4, :] += _rowsum(dlog_a * r) * (_LRU_C * jax.nn.sigmoid(-lam_ref[...]))
        dgr_f = dlog_a * (-_LRU_C * sp) * (r * (1.0 - r))
        dgi_f = di * (i * (1.0 - i))
        vecw_ref[1:2, :] += _rowsum(dgr_f)
        vecw_ref[2:3, :] += _rowsum(dgi_f)
        dgr, dgi = _mx(dgr_f), _mx(dgi_f)
        xcg_ref[:, w:2 * w] = dgr
        xcg_ref[:, 2 * w:3 * w] = dgi
        dxc = dxc + _dot_nt(dgr, wr_ref[...]) + _dot_nt(dgi, wi_ref[...])
        vecw_ref[0:1, :] += _rowsum(dxc)
        dext4[0:tt, :] = dxc
        dext4[tt:tt + 8, :] = car4[...]
        car4[...] = dxc[0:8, :]
        dxa = jnp.zeros((tt, w), _F32)
        for k in range(4):
            dxa = dxa + lcw_ref[k:k + 1, :] * dext4[pl.ds(3 - k, tt), :]
            dlcw_ref[k:k + 1, :] += _rowsum(dxc * ext4[pl.ds(5 + k, tt), :])
        dproj_ref[:, 0:w] = _mx(dxa)

        du1 = sum(_dot_nt(dproj_ref[:, k * w:(k + 1) * w], win_ref[k]) for k in range(4))
        gx_ref[...] = _ALPHA * dz1 + du1 * (1.0 + sc1)
        dmod_ref[0:1, :] += _rowsum(du1)
        dmod_ref[1:2, :] += _rowsum(du1 * xt)

    tok = lambda c: pl.BlockSpec((None, tt, c), lambda b, i: (b, ns - 1 - i, 0))
    halo = lambda rows, c: pl.BlockSpec(
        (None, rows, c), lambda b, i: (b, jnp.maximum((ns - 1 - i) * (tt // rows) - 1, 0), 0))
    accw = lambda r, c: pl.BlockSpec((r, c), lambda b, i: (0, 0))
    smalls = [lcw, lcb, wr_bd, wi_bd, b_r, b_i, lam, cw, cb, ng, nb, seg, wout, ln1g]
    nx = len(chip_sums)
    return pl.pallas_call(
        _fused_exchange(body, 11 + len(smalls), 9, 9, nx, nx, _chip_reduce_plan, (bl, ns)), grid=(bl, ns),
        in_specs=[tok(d), tok(d), tok(d), tok(4 * w), halo(_HALO, 4 * w), tok(w), halo(8, w), tok(w), tok(5 * w),
                  pl.BlockSpec((None, 1, 6 * d), lambda b, i: (b, 0, 0)), _resident(win.shape)]
        + [_resident(t.shape) for t in smalls] + [_HBM] * nx,
        out_specs=[tok(d), tok(4 * w), tok(d), tok(3 * w), accw(8, w), accw(4, w), accw(kc, w), accw(2, d),
                   pl.BlockSpec((None, 3, d), lambda b, i: (b, 0, 0))] + [_HBM] * nx,
        out_shape=[jax.ShapeDtypeStruct((bl, s_len, d), _F32), jax.ShapeDtypeStruct((bl, s_len, 4 * w), _MXU_DT),
                   jax.ShapeDtypeStruct((bl, s_len, d), _MXU_DT), jax.ShapeDtypeStruct((bl, s_len, 3 * w), _MXU_DT),
                   jax.ShapeDtypeStruct((8, w), _F32), jax.ShapeDtypeStruct((4, w), _F32),
                   jax.ShapeDtypeStruct((kc, w), _F32), jax.ShapeDtypeStruct((2, d), _F32),
                   jax.ShapeDtypeStruct((bl, 3, d), _F32)]
        + [jax.ShapeDtypeStruct((3,) + t.shape[1:], t.dtype) for t in chip_sums],
        scratch_shapes=[pltpu.VMEM((tt + 8, w), _F32), pltpu.VMEM((tt + _HALO, w), _F32),
                        pltpu.VMEM((tt + 8, w), _F32), pltpu.VMEM((tt + _HALO, w), _F32),
                        pltpu.VMEM((7, tt + _HALO - 8, w), _F32), pltpu.VMEM((7, tt + _HALO - 8, w), _F32),
                        pltpu.VMEM((8, w), _F32), pltpu.VMEM((_HALO, w), _F32), pltpu.VMEM((8, w), _F32)]
        + _chip_reduce_sems(nx),
        compiler_params=_cparams(("arbitrary", "arbitrary")), name="mix_bwd",
    )(dx1, x, mix, proj, proj, h, h, vbc, lru, mod3, win, *smalls, *chip_sums)


def _wgrad(a, b, ma, nbw, na, nb, a_off, b_off, name, exchange=None):
    t = a.shape[0]
    tk = min(_TK_WGRAD, t)
    grid = (na * nb, t // tk)

    def body(a_ref, b_ref, o_ref):
        @pl.when(pl.program_id(1) == 0)
        def _():
            o_ref[...] = jnp.zeros_like(o_ref)
        o_ref[...] += _dot_tn(a_ref[...], b_ref[...])

    xin, xshapes, plan, sems = exchange if exchange else ([], [], None, [])
    nx = len(xin)
    res = pl.pallas_call(
        _fused_exchange(body, 2, 1, 0, nx, len(xshapes), plan, grid) if exchange else body, grid=grid,
        in_specs=[pl.BlockSpec((tk, ma), lambda j, k: (k, j // nb + a_off)),
                  pl.BlockSpec((tk, nbw), lambda j, k: (k, j % nb + b_off))] + [_HBM] * nx,
        out_specs=[pl.BlockSpec((None, ma, nbw), lambda j, k: (j, 0, 0))] + [_HBM] * len(xshapes),
        out_shape=[jax.ShapeDtypeStruct((na * nb, ma, nbw), _F32)] + list(xshapes),
        scratch_shapes=list(sems),
        compiler_params=_cparams(("arbitrary", "arbitrary")), name=name,
    )(a, b, *xin)
    return res if exchange else res[0]


_DEV_DELTAS = tuple(dl for dl in itertools.product((0, 1), repeat=3) if any(dl))
_HBM = pl.BlockSpec(memory_space=pltpu.HBM)
_VMEM = pl.BlockSpec(memory_space=pltpu.VMEM)


def _pos():
    return lax.axis_index("x"), lax.axis_index("y"), lax.axis_index("c")


def _flip(v, delta):
    return 1 - v if delta else v


def _remote(src, dst, ssem, rsem, dev):
    return pltpu.make_async_remote_copy(src_ref=src, dst_ref=dst, send_sem=ssem, recv_sem=rsem,
                                        device_id=dev, device_id_type=_MESH)


def _rows(ref, idx, n):
    return ref.at[pl.ds(pl.multiple_of(idx * n, 8), n)]


def _ada_fwd(c8, w_ada_k, b_ada_k, shards):
    rows, d = c8.shape
    nk = w_ada_k.shape[1]
    n = len(shards)

    def body(*refs):
        c_ref, w_ref, b_ref = refs[:3]
        call_ref, mod_ref = refs[3 + n:5 + n]
        modloc, modrcv, s1, r1, s2, r2 = refs[5 + 2 * n:11 + 2 * n]
        gather_start, gather_finish = _gather_plan(refs[3:3 + n], refs[5 + n:5 + 2 * n], *refs[11 + 2 * n:14 + 2 * n],
                                                   fsem=refs[14 + 2 * n], frsem=refs[15 + 2 * n], bounce=refs[16 + 2 * n:])
        gather_start()
        xi, yi, ci = _pos()
        me, kme = 4 * xi + 2 * yi + ci, 2 * xi + yi
        call_ref[pl.ds(pl.multiple_of(me * rows, 8), rows), :] = c_ref[...]
        sends = []
        for p, (dx, dy, dc) in enumerate(_DEV_DELTAS):
            cp = _remote(c_ref, _rows(call_ref, me, rows), s1.at[p], r1.at[p], (_flip(xi, dx), _flip(yi, dy), _flip(ci, dc)))
            cp.start()
            sends.append(cp)
        for p, (dx, dy, dc) in enumerate(_DEV_DELTAS):
            src = 4 * _flip(xi, dx) + 2 * _flip(yi, dy) + _flip(ci, dc)
            _remote(c_ref, _rows(call_ref, src, rows), s1.at[p], r1.at[p], (xi, yi, ci)).wait_recv()
        for cp in sends:
            cp.wait_send()

        ca = call_ref[...]
        modloc[...] = _dot(_mx(ca * jax.nn.sigmoid(ca)), _mx(w_ref[...])) + b_ref[...]
        modrcv[kme] = modloc[pl.ds(pl.multiple_of(me * rows, 8), rows), :]
        sends = []
        for j, (dx, dy) in enumerate(_CHIP_DELTAS):
            tx, ty = _flip(xi, dx), _flip(yi, dy)
            cp = _remote(_rows(modloc, 4 * tx + 2 * ty + ci, rows), modrcv.at[kme], s2.at[j], r2.at[j], (tx, ty, ci))
            cp.start()
            sends.append(cp)
        for j, (dx, dy) in enumerate(_CHIP_DELTAS):
            ksrc = 2 * _flip(xi, dx) + _flip(yi, dy)
            _remote(_rows(modloc, me, rows), modrcv.at[ksrc], s2.at[j], r2.at[j], (xi, yi, ci)).wait_recv()
        for cp in sends:
            cp.wait_send()
        for j in range(4):
            mod_ref[:, j * nk:(j + 1) * nk] = modrcv[j]
        gather_finish()

    return pl.pallas_call(
        body, in_specs=[_VMEM, _VMEM, _VMEM] + [_HBM] * n, out_specs=[_VMEM, _VMEM] + [_HBM] * n,
        out_shape=[jax.ShapeDtypeStruct((8 * rows, d), _F32), jax.ShapeDtypeStruct((rows, 4 * nk), _F32)]
        + [jax.ShapeDtypeStruct((4,) + a.shape, a.dtype) for a in shards],
        scratch_shapes=[pltpu.VMEM((8 * rows, nk), _F32), pltpu.VMEM((4, rows, nk), _F32),
                        pltpu.SemaphoreType.DMA((7,)), pltpu.SemaphoreType.DMA((7,)),
                        pltpu.SemaphoreType.DMA((3,)), pltpu.SemaphoreType.DMA((3,))]
        + _gather_sems(n) + [pltpu.SemaphoreType.DMA((3, n)), pltpu.SemaphoreType.DMA((3, n))]
        + [pltpu.VMEM(a.shape, a.dtype) for a in shards],
        compiler_params=pltpu.CompilerParams(vmem_limit_bytes=_VMEM_LIMIT), name="ada_fwd",
    )(c8, w_ada_k, b_ada_k, *shards)


def _gather_sems(n):
    return [pltpu.SemaphoreType.DMA((3, n)), pltpu.SemaphoreType.DMA((3, n)), pltpu.SemaphoreType.DMA((n,))]


def _gather_plan(ins, outs, ssem, rsem, lsem, bounce=(), fsem=None, frsem=None):
    n = len(ins)
    xi, yi, ci = _pos()
    kme = 2 * xi + yi
    split = [fsem is not None and ins[a].shape[0] % 32 == 0 for a in range(n)]

    def half(ref, a, which):
        r2 = ins[a].shape[0] // 2
        return ref.at[pl.ds(pl.multiple_of(which * r2, 16), r2)]

    staged = [pltpu.make_async_copy(ins[a], bounce[a], lsem.at[a]) for a in range(len(bounce))]
    local = [pltpu.make_async_copy(bounce[a] if bounce else ins[a], outs[a].at[kme], lsem.at[a]) for a in range(n)]
    sends, recvs, forwards, handed = [], [], [], []
    for j, (dx, dy) in enumerate(_CHIP_DELTAS):
        tx, ty = _flip(xi, dx), _flip(yi, dy)
        for a in range(n):
            sems = (ssem.at[j, a], rsem.at[j, a])
            landing = outs[a].at[2 * tx + ty]
            if split[a]:
                sends.append(_remote(half(ins[a], a, ci), half(outs[a].at[kme], a, ci), *sems, (tx, ty, ci)))
                recvs.append(_remote(half(ins[a], a, ci), half(landing, a, ci), *sems, (xi, yi, ci)))
                fsems = (fsem.at[j, a], frsem.at[j, a])
                forwards.append(_remote(half(landing, a, ci), half(landing, a, ci), *fsems, (xi, yi, 1 - ci)))
                handed.append(_remote(half(ins[a], a, 1 - ci), half(landing, a, 1 - ci), *fsems, (xi, yi, ci)))
            else:
                sends.append(_remote(ins[a], outs[a].at[kme], *sems, (tx, ty, ci)))
                recvs.append(_remote(ins[a], landing, *sems, (xi, yi, ci)))
                forwards.append(None)

    def start():
        for cp in sends + staged:
            cp.start()
        for cp in staged:
            cp.wait()
        for cp in local:
            cp.start()

    def finish():
        for arrived, forward in zip(recvs, forwards):
            arrived.wait_recv()
            if forward is not None:
                forward.start()
        for cp in handed:
            cp.wait_recv()
        for cp in sends + [f for f in forwards if f is not None]:
            cp.wait_send()
        for cp in local:
            cp.wait()

    return start, finish


def _dev_gather_sems(n):
    return [pltpu.SemaphoreType.DMA((7, n)), pltpu.SemaphoreType.DMA((7, n)), pltpu.SemaphoreType.DMA((n,))]


def _dev_gather_plan(ins, outs, ssem, rsem, lsem):
    n = len(ins)
    xi, yi, ci = _pos()
    me, sibling = 4 * xi + 2 * yi + ci, (xi, yi, 1 - ci)
    local = [pltpu.make_async_copy(ins[a], outs[a].at[me], lsem.at[a]) for a in range(n)]
    sends = [_remote(ins[a], outs[a].at[me], ssem.at[0, a], rsem.at[0, a], sibling) for a in range(n)]
    handed = [_remote(ins[a], outs[a].at[4 * xi + 2 * yi + 1 - ci], ssem.at[0, a], rsem.at[0, a], (xi, yi, ci))
              for a in range(n)]
    arrivals, forwards = [], []
    for j, (dx, dy) in enumerate(_CHIP_DELTAS):
        tx, ty = _flip(xi, dx), _flip(yi, dy)
        for a in range(n):
            over_ici, over_d2d = (ssem.at[1 + j, a], rsem.at[1 + j, a]), (ssem.at[4 + j, a], rsem.at[4 + j, a])
            landing = outs[a].at[4 * tx + 2 * ty + ci]
            sends.append(_remote(ins[a], outs[a].at[me], *over_ici, (tx, ty, ci)))
            arrivals.append(_remote(ins[a], landing, *over_ici, (xi, yi, ci)))
            forwards.append(_remote(landing, landing, *over_d2d, sibling))
            handed.append(_remote(ins[a], outs[a].at[4 * tx + 2 * ty + 1 - ci], *over_d2d, (xi, yi, ci)))

    def start():
        for cp in local + sends:
            cp.start()

    def finish():
        for arrived, forward in zip(arrivals, forwards):
            arrived.wait_recv()
            forward.start()
        for cp in handed:
            cp.wait_recv()
        for cp in sends + forwards:
            cp.wait_send()
        for cp in local:
            cp.wait()

    return start, finish


def _pair_sems(n):
    return [pltpu.SemaphoreType.DMA((n,)), pltpu.SemaphoreType.DMA((n,))]


def _pair_plan(ins, outs, ssem, rsem):
    xi, yi, ci = _pos()
    sends = []
    for a in range(len(ins)):
        r2 = ins[a].shape[1] // 2
        src = ins[a].at[:, pl.ds(pl.multiple_of((1 - ci) * r2, 8), r2), :]
        sends.append(_remote(src, outs[a], ssem.at[a], rsem.at[a], (xi, yi, 1 - ci)))

    def start():
        for cp in sends:
            cp.start()

    def finish():
        for cp in sends:
            cp.wait_recv()
        for cp in sends:
            cp.wait_send()

    return start, finish


def _chip_reduce_sems(n):
    return [pltpu.SemaphoreType.DMA((3, n)), pltpu.SemaphoreType.DMA((3, n))]


def _chip_reduce_plan(ins, outs, ssem, rsem):
    xi, yi, ci = _pos()
    sends = []
    for j, (dx, dy) in enumerate(_CHIP_DELTAS):
        tx, ty = _flip(xi, dx), _flip(yi, dy)
        sends += [_remote(ins[a].at[2 * tx + ty], outs[a].at[j], ssem.at[j, a], rsem.at[j, a], (tx, ty, ci))
                  for a in range(len(ins))]

    def start():
        for cp in sends:
            cp.start()

    def finish():
        for cp in sends:
            cp.wait_recv()
        for cp in sends:
            cp.wait_send()

    return start, finish


def _pair_exchange(gs, name):
    n = len(gs)

    def body(*refs):
        start, finish = _pair_plan(refs[:n], refs[n:2 * n], *refs[2 * n:])
        start()
        finish()

    return pl.pallas_call(
        body, in_specs=[_HBM] * n, out_specs=[_HBM] * n, out_shape=_pair_out_shapes(gs),
        scratch_shapes=_pair_sems(n), name=name,
    )(*gs)


def _pair_out_shapes(gs):
    return [jax.ShapeDtypeStruct((g.shape[0], g.shape[1] // 2, g.shape[2]), g.dtype) for g in gs]


def _row_tile(r):
    return max(t for t in range(8, min(r, 256) + 1, 8) if r % t == 0)


def _pair_add(g, r, cidx, name, wire_dtype=None, exchange=None):
    nk, r2, c = r.shape
    tr = _row_tile(r2)
    nt = r2 // tr
    xin, xshapes, plan, sems = exchange if exchange else ([], [], None, [])
    nx = len(xin)

    def body(c_ref, g_ref, r_ref, *o_refs):
        s = g_ref[...] + r_ref[...]
        for o_ref in o_refs:
            o_ref[...] = s.astype(o_ref.dtype)

    out_spec = pl.BlockSpec((None, tr, c), lambda k, i, cr: (k, i, 0))
    dtypes = [_F32] + ([wire_dtype] if wire_dtype else [])
    res = pl.pallas_call(
        _fused_exchange(body, 3, len(dtypes), 0, nx, len(xshapes), plan, (nk, nt)) if exchange else body,
        grid_spec=pltpu.PrefetchScalarGridSpec(
            num_scalar_prefetch=1, grid=(nk, nt),
            in_specs=[pl.BlockSpec((None, tr, c), lambda k, i, cr: (k, cr[0] * nt + i, 0)), out_spec] + [_HBM] * nx,
            out_specs=[out_spec] * len(dtypes) + [_HBM] * len(xshapes), scratch_shapes=list(sems)),
        out_shape=[jax.ShapeDtypeStruct(r.shape, dt) for dt in dtypes] + list(xshapes),
        compiler_params=_cparams(("arbitrary", "arbitrary")), name=name,
    )(cidx, g, r, *xin)
    return res if wire_dtype or exchange else res[0]


def _chip_exchange(ss):
    n = len(ss)

    def body(*refs):
        start, finish = _chip_reduce_plan(refs[:n], refs[n:2 * n], *refs[2 * n:])
        start()
        finish()

    return pl.pallas_call(
        body, in_specs=[_HBM] * n, out_specs=[_HBM] * n,
        out_shape=[jax.ShapeDtypeStruct((3,) + s.shape[1:], s.dtype) for s in ss],
        scratch_shapes=_chip_reduce_sems(n), name="grad_chip_exchange",
    )(*ss)


def _chip_add(s, r, kidx, name):
    _, r2, c = r.shape
    tr = _row_tile(r2)

    def body(k_ref, s_ref, r_ref, o_ref):
        o_ref[...] = ((s_ref[...] + r_ref[0].astype(_F32)) + r_ref[1].astype(_F32)) + r_ref[2].astype(_F32)

    return pl.pallas_call(
        body, grid_spec=pltpu.PrefetchScalarGridSpec(
            num_scalar_prefetch=1, grid=(r2 // tr,),
            in_specs=[pl.BlockSpec((None, tr, c), lambda i, kr: (kr[0], i, 0)),
                      pl.BlockSpec((3, tr, c), lambda i, kr: (0, i, 0))],
            out_specs=pl.BlockSpec((tr, c), lambda i, kr: (i, 0))),
        out_shape=jax.ShapeDtypeStruct((r2, c), _F32),
        compiler_params=_cparams(("arbitrary",)), name=name,
    )(kidx, s, r)


def _pair_swap_plan(ins, outs, ssem, rsem):
    xi, yi, ci = _pos()
    sends = [_remote(ins[a], outs[a], ssem.at[a], rsem.at[a], (xi, yi, 1 - ci)) for a in range(len(ins))]

    def start():
        for cp in sends:
            cp.start()

    def finish():
        for cp in sends:
            cp.wait_recv()
        for cp in sends:
            cp.wait_send()

    return start, finish


def _pair_swap(hs, name):
    n = len(hs)

    def body(*refs):
        start, finish = _pair_swap_plan(refs[:n], refs[n:2 * n], *refs[2 * n:])
        start()
        finish()

    return pl.pallas_call(
        body, in_specs=[_HBM] * n, out_specs=[_HBM] * n,
        out_shape=[jax.ShapeDtypeStruct(h.shape, h.dtype) for h in hs],
        scratch_shapes=[pltpu.SemaphoreType.DMA((n,)), pltpu.SemaphoreType.DMA((n,))], name=name,
    )(*hs)


def _small_sum(every):
    def body(all_ref, sum_ref):
        tot = all_ref[0]
        for dev in range(1, 8):
            tot = tot + all_ref[dev]
        sum_ref[...] = tot

    return pl.pallas_call(
        body, in_specs=[_VMEM], out_specs=_VMEM, out_shape=jax.ShapeDtypeStruct(every.shape[1:], _F32),
        compiler_params=pltpu.CompilerParams(vmem_limit_bytes=_VMEM_LIMIT), name="small_sum",
    )(every)


def _adamw(w, g, m, v):
    m = _ADAM_B1 * m + (1.0 - _ADAM_B1) * g
    v = _ADAM_B2 * v + (1.0 - _ADAM_B2) * (g * g)
    m_hat = m / (1.0 - _ADAM_B1 ** _ADAM_STEP)
    v_hat = v / (1.0 - _ADAM_B2 ** _ADAM_STEP)
    return -_ADAM_LR * (m_hat / (jnp.sqrt(v_hat) + _ADAM_EPS) + _ADAM_WD * w), m, v


def _adamw_big(w, g_mine, g_theirs, m, v, cidx, name):
    r, c = w.shape
    tr = _row_tile(r // 2)
    nt = r // 2 // tr

    def body(c_ref, w_ref, gm_ref, gt_ref, m_ref, v_ref, g_ref, d_ref, mo_ref, vo_ref):
        g = jnp.where(pl.program_id(0) // nt == c_ref[0], gm_ref[...], gt_ref[...])
        g_ref[...] = g
        d_ref[...], mo_ref[...], vo_ref[...] = _adamw(w_ref[...], g, m_ref[...], v_ref[...])

    spec = pl.BlockSpec((tr, c), lambda i, cr: (i, 0))
    half = pl.BlockSpec((tr, c), lambda i, cr: (i % nt, 0))
    return pl.pallas_call(
        body, grid_spec=pltpu.PrefetchScalarGridSpec(
            num_scalar_prefetch=1, grid=(2 * nt,), in_specs=[spec, half, half, spec, spec], out_specs=[spec] * 4),
        out_shape=[jax.ShapeDtypeStruct((r, c), _F32)] * 4,
        compiler_params=_cparams(("arbitrary",)), name=name,
    )(cidx, w, g_mine, g_theirs, m, v)


def _adamw_small(ws, gs, ms, vs):
    n = len(ws)
    summed = [i for i in range(n) if gs[i].shape != ws[i].shape]

    def body(*refs):
        w_r, g_r, m_r, v_r = (refs[i * n:(i + 1) * n] for i in range(4))
        outs = refs[4 * n:]
        for i in range(n):
            g = g_r[i][...]
            if i in summed:
                g = _rowsum(g)
                outs[3 * n + summed.index(i)][...] = g
            outs[i][...], outs[n + i][...], outs[2 * n + i][...] = _adamw(w_r[i][...], g, m_r[i][...], v_r[i][...])

    shapes = [jax.ShapeDtypeStruct(w.shape, _F32) for w in ws]
    res = pl.pallas_call(
        body, in_specs=[_VMEM] * (4 * n), out_specs=[_VMEM] * (3 * n + len(summed)),
        out_shape=shapes * 3 + [shapes[i] for i in summed],
        compiler_params=pltpu.CompilerParams(vmem_limit_bytes=_VMEM_LIMIT), name="adamw_small",
    )(*ws, *gs, *ms, *vs)
    gs = list(gs)
    for pos, i in enumerate(summed):
        gs[i] = res[3 * n + pos]
    return gs, res[:n], res[n:2 * n], res[2 * n:3 * n]


def _ada_bwd(c_all, dmod_k, w, m, v):
    d, nk = w.shape
    tn = 512 if nk % 512 == 0 else nk

    def body(c_ref, dm_ref, w_ref, m_ref, v_ref, g_ref, d_ref, mo_ref, vo_ref):
        ca = c_ref[...]
        g = _dot_tn(_mx(ca * jax.nn.sigmoid(ca)), _mx(dm_ref[...]))
        g_ref[...] = g
        d_ref[...], mo_ref[...], vo_ref[...] = _adamw(w_ref[...], g, m_ref[...], v_ref[...])

    col = pl.BlockSpec((d, tn), lambda j: (0, j))
    return pl.pallas_call(
        body, grid=(nk // tn,),
        in_specs=[pl.BlockSpec(c_all.shape, lambda j: (0, 0)), pl.BlockSpec((c_all.shape[0], tn), lambda j: (0, j)),
                  col, col, col],
        out_specs=[col] * 4, out_shape=[jax.ShapeDtypeStruct((d, nk), _F32)] * 4,
        compiler_params=_cparams(("arbitrary",)), name="ada_bwd",
    )(c_all, dmod_k, w, m, v)


def _block_diag(wh):
    hn, dh, _ = wh.shape
    eye = jnp.eye(hn, dtype=wh.dtype)
    return (eye[:, None, :, None] * wh[:, :, None, :]).reshape(hn * dh, hn * dh)


def _pack(pieces):
    out = []
    for p in pieces:
        flat = p.reshape(-1, 128)
        out.append(jnp.pad(flat, ((0, (-flat.shape[0]) % 8), (0, 0))))
    return jnp.concatenate(out, axis=0)


def _unpack(pack, shapes):
    out, off = [], 0
    for shp in shapes:
        rows = math.prod(shp) // 128
        out.append(pack[..., off:off + rows, :].reshape(pack.shape[:-2] + tuple(shp)))
        off += rows + (-rows) % 8
    return out


_WEIGHTS = ('w_ada', 'b_ada', 'w_in', 'lru_conv_w', 'lru_conv_b', 'lru_w_r', 'lru_b_r', 'lru_w_i', 'lru_b_i', 'lru_lambda',
            'conv_w', 'conv_b', 'conv_norm_g', 'conv_norm_b', 'w_out', 'ln1_g', 'ln1_b', 'ffn_w_up', 'ffn_conv_w',
            'ffn_conv_b', 'ffn_w_down', 'ln2_g', 'ln2_b')
_BIG = ('w_in', 'w_out', 'ffn_w_up', 'ffn_w_down')


def kernel(x, c, w_ada, b_ada, w_in, lru_conv_w, lru_conv_b, lru_w_r, lru_b_r, lru_w_i, lru_b_i, lru_lambda, conv_w, conv_b, conv_norm_g, conv_norm_b, w_out, ln1_g, ln1_b, ffn_w_up, ffn_conv_w, ffn_conv_b, ffn_w_down, ln2_g, ln2_b, loss_target, m_w_ada, m_b_ada, m_w_in, m_lru_conv_w, m_lru_conv_b, m_lru_w_r, m_lru_b_r, m_lru_w_i, m_lru_b_i, m_lru_lambda, m_conv_w, m_conv_b, m_conv_norm_g, m_conv_norm_b, m_w_out, m_ln1_g, m_ln1_b, m_ffn_w_up, m_ffn_conv_w, m_ffn_conv_b, m_ffn_w_down, m_ln2_g, m_ln2_b, v_w_ada, v_b_ada, v_w_in, v_lru_conv_w, v_lru_conv_b, v_lru_w_r, v_lru_b_r, v_lru_w_i, v_lru_b_i, v_lru_lambda, v_conv_w, v_conv_b, v_conv_norm_g, v_conv_norm_b, v_w_out, v_ln1_g, v_ln1_b, v_ffn_w_up, v_ffn_conv_w, v_ffn_conv_b, v_ffn_w_down, v_ln2_g, v_ln2_b):
    given = dict(locals())
    wt = {n: given[n] for n in _WEIGHTS}
    mo = {n: given["m_" + n] for n in _WEIGHTS}
    vo = {n: given["v_" + n] for n in _WEIGHTS}
    bl, s_len, d = x.shape
    wd = d // 2
    tokens = bl * s_len
    xi, yi, ci = _pos()
    kme = 2 * xi + yi
    kidx = jnp.reshape(kme, (1,)).astype(jnp.int32)
    cidx = jnp.reshape(ci, (1,)).astype(jnp.int32)

    nk = w_ada.shape[2]
    c8 = jnp.pad(c, ((0, 8 - bl), (0, 0)))
    c_all, mod8, win, wout_s, lcw_s, cw_s, fcw_s = _ada_fwd(
        c8, w_ada[0], lax.dynamic_slice(b_ada, (0, kme * nk), (1, nk)),
        [_mx(w_in[0]), _mx(w_out[0]), lru_conv_w[0], conv_w[0], ffn_conv_w[0]])
    mod3 = mod8[:bl].reshape(bl, 1, 6 * d)
    wout = wout_s.reshape(d, d)
    f = 4 * ffn_w_down.shape[1]
    unshard = lambda t: jnp.transpose(t, (1, 0, 2)).reshape(t.shape[1], -1)
    lcw, cw, fcw = unshard(lcw_s), unshard(cw_s), unshard(fcw_s)
    wr_bd, wi_bd = _mx(_block_diag(lru_w_r[0])), _mx(_block_diag(lru_w_i[0]))
    seg = _block_diag(jnp.ones((_N_HEADS, wd // _N_HEADS, wd // _N_HEADS), jnp.bfloat16))
    mixer_small = (lcw, lru_conv_b, wr_bd, wi_bd, lru_b_r, lru_b_i, lru_lambda, cw, conv_b, conv_norm_g, conv_norm_b, seg, wout, ln1_g)

    proj, h, mix, x1, u1, y, vbc, lru, wup, wdn_s = _mix_fwd(x, mod3, win, *mixer_small, ln1_b, [_mx(ffn_w_up[0]), _mx(ffn_w_down[0])])
    wdn = wdn_s.reshape(f, d)
    u2, hh, fact, gc_all, dz2, loss_acc, dln2, dgt2 = _ffn_fwd(x1, mod3, wup, fcw, ffn_conv_b, wdn, ln2_g, ln2_b, loss_target)
    dx1, dy2, dh, dfc, dmod2 = _ffn_bwd(dz2, x1, hh, gc_all, mod3, wup, wdn, fcw, ffn_conv_b)

    flat = lambda t: t.reshape(tokens, t.shape[-1])
    fc = wup.shape[2]
    g_up = _wgrad(flat(u2), flat(dh), d, fc, 1, 4, 0, 0, "wgrad_up")
    g_dn, r_up = _wgrad(flat(fact), flat(dy2), fc, d, f // fc, 1, 0, 0, "wgrad_down",
                        exchange=([g_up], _pair_out_shapes([g_up]), _pair_plan, _pair_sems(1)))
    g_dn = g_dn.reshape(4, f // 4, d)
    s_up, r_dn = _pair_add(g_up, r_up, cidx, "grad_pair_add_ffn_w_up",
                           exchange=([g_dn], _pair_out_shapes([g_dn]), _pair_plan, _pair_sems(1)))
    ffn_sum = [s_up, _pair_add(g_dn, r_dn, cidx, "grad_pair_add_ffn_w_down")]
    grad_x, dproj, dmix, xcg, vecw, dlcw, dcw, dln1, dmod1, *ffn_recv = _mix_bwd(
        dx1, x, mix, proj, h, vbc, lru, mod3, win, *mixer_small, ffn_sum)
    g_ri = _wgrad(flat(xcg), flat(xcg), wd, wd, 1, 2, 0, 1, "wgrad_gates")
    dh_ = wd // _N_HEADS
    on_diagonal = jnp.eye(_N_HEADS, dtype=_F32)[None, :, None, :, None]
    g_ri = jnp.sum(g_ri.reshape(2, _N_HEADS, dh_, _N_HEADS, dh_) * on_diagonal, axis=3)

    dmod = jnp.concatenate([dmod1.reshape(bl, 3 * d), dmod2.reshape(bl, 2 * d), dgt2.reshape(bl, d)], axis=1)
    pieces = [vecw, dlcw, dcw, jnp.concatenate([dln1, dln2], axis=0), dfc, g_ri, loss_acc[:, 0:128],
              jnp.pad(dmod, ((0, 8 - bl), (0, 0)))]
    shapes = [p.shape for p in pieces]
    pack = _pack(pieces)
    g_in, every = _wgrad(flat(u1), flat(dproj), d, wd, 1, 4, 0, 0, "wgrad_in", exchange=(
        [pack], [jax.ShapeDtypeStruct((8,) + pack.shape, _F32)], _dev_gather_plan, _dev_gather_sems(1)))
    wire_shape = lambda t: [jax.ShapeDtypeStruct((3,) + t.shape[1:], t.dtype)]
    ffn_half = [_chip_add(s, r, kidx, "grad_chip_add_" + n) for s, r, n in zip(ffn_sum, ffn_recv, _BIG[2:])]
    same = lambda ts: [jax.ShapeDtypeStruct(t.shape, t.dtype) for t in ts]
    r_in, = _pair_exchange([g_in], "grad_pair_exchange_w_in")
    s_in, wire_in, *ffn_theirs = _pair_add(g_in, r_in, cidx, "grad_pair_add_w_in", jnp.bfloat16,
                                           exchange=(ffn_half, same(ffn_half), _pair_swap_plan, _pair_sems(2)))
    g_out, recv_in = _wgrad(flat(y), flat(dmix), d, d, 1, 1, 0, 0, "wgrad_out", exchange=(
        [wire_in], wire_shape(wire_in), _chip_reduce_plan, _chip_reduce_sems(1)))
    g_out = g_out.reshape(4, d // 4, d)
    r_out, = _pair_exchange([g_out], "grad_pair_exchange_w_out")
    s_out, wire_out = _pair_add(g_out, r_out, cidx, "grad_pair_add_w_out", jnp.bfloat16)
    recv_out, = _chip_exchange([wire_out])
    mix_half = [_chip_add(s, r, kidx, "grad_chip_add_" + n) for s, r, n in zip([s_in, s_out], [recv_in, recv_out], _BIG)]
    half, other = mix_half + ffn_half, list(_pair_swap(mix_half, "grad_pair_swap")) + ffn_theirs
    grads, deltas, new_m, new_v = {}, {}, {}, {}
    for n, mine, theirs in zip(_BIG, half, other):
        g, dl, mm, vv = _adamw_big(wt[n][0], mine, theirs, mo[n][0], vo[n][0], cidx, "adamw_" + n)
        grads[n], deltas[n], new_m[n], new_v[n] = g[None], dl[None], mm[None], vv[None]

    vecw, dlcw, dcw, dln, dfc, g_ri, loss_sum, dmod_sum = _unpack(_small_sum(every), shapes)
    loss = 0.5 * loss_sum[0, 0] / d
    dmod_all = _unpack(every, shapes)[-1].reshape(64, 6 * d)

    g_ada, dl, mm, vv = _ada_bwd(c_all, lax.dynamic_slice(dmod_all, (0, kme * nk), (64, nk)), w_ada[0], m_w_ada[0], v_w_ada[0])
    grads['w_ada'], deltas['w_ada'], new_m['w_ada'], new_v['w_ada'] = g_ada[None], dl[None], mm[None], vv[None]

    shard = lambda t, width: lax.dynamic_slice(t, (0, kme * width), (t.shape[0], width))
    small = {
        'b_ada': dmod_sum, 'lru_conv_w': shard(dlcw, wd // 4), 'lru_conv_b': vecw[0:1], 'lru_w_r': g_ri[0], 'lru_b_r': vecw[1:2],
        'lru_w_i': g_ri[1], 'lru_b_i': vecw[2:3], 'lru_lambda': vecw[3:4], 'conv_w': shard(dcw, wd // 4), 'conv_b': vecw[4:5],
        'conv_norm_g': vecw[5:6], 'conv_norm_b': vecw[6:7], 'ln1_g': dln[0:1], 'ln1_b': dln[1:2],
        'ffn_conv_w': shard(dfc[0:3], f // 4), 'ffn_conv_b': dfc[3:4], 'ln2_g': dln[2:3], 'ln2_b': dln[3:4]}
    names = list(small)
    gs = [small[n] if n == 'b_ada' else small[n].reshape(wt[n].shape) for n in names]
    gs, dls, mms, vvs = _adamw_small([wt[n] for n in names], gs, [mo[n] for n in names], [vo[n] for n in names])
    for n, g, dl, mm, vv in zip(names, gs, dls, mms, vvs):
        grads[n], deltas[n], new_m[n], new_v[n] = g, dl, mm, vv

    return (loss, grad_x, *[grads[n] for n in _WEIGHTS], *[deltas[n] for n in _WEIGHTS],
            *[new_m[n] for n in _WEIGHTS], *[new_v[n] for n in _WEIGHTS])
```

```python
import functools
import itertools
import math

import jax
import jax.numpy as jnp
from jax import lax
from jax.experimental import pallas as pl
from jax.experimental.pallas import tpu as pltpu

_MXU_DT = jnp.bfloat16
_F32 = jnp.float32
_VMEM_LIMIT = 56 * 1024 * 1024
_TT_MIX = 256
_TT_MIX_FWD = 512
_TT_FFN = 256
_TK_WGRAD = 2048
_HALO = 32

_LRU_C = 8.0
_LN_EPS = 1e-5
_N_HEADS = 8
_DEPTH = 1
_ALPHA = (2 * _DEPTH) ** 0.25
_ADAM_LR, _ADAM_B1, _ADAM_B2, _ADAM_EPS, _ADAM_WD, _ADAM_STEP = 0.001, 0.9, 0.999, 1e-08, 0.01, 10

_MESH = pl.DeviceIdType.MESH
_CHIP_DELTAS = ((1, 0), (0, 1), (1, 1))


def _cparams(sem):
    return pltpu.CompilerParams(dimension_semantics=sem, vmem_limit_bytes=_VMEM_LIMIT)


def _resident(shape):
    nd = len(shape)
    return pl.BlockSpec(shape, lambda *_: (0,) * nd, pipeline_mode=pl.Buffered(1))


def _dot(a, b):
    return jnp.dot(a, b, preferred_element_type=_F32)


def _dot_nt(a, b):
    return lax.dot_general(a, b, (((1,), (1,)), ((), ())), preferred_element_type=_F32)


def _dot_tn(a, b):
    return lax.dot_general(a, b, (((0,), (0,)), ((), ())), preferred_element_type=_F32)


def _mx(v):
    return v.astype(_MXU_DT)


def _expm1(v):
    series = v * (1.0 + v * (1.0 / 2 + v * (1.0 / 6)))
    return jnp.where(jnp.abs(v) < 0.015625, series, jnp.exp(v) - 1.0)


def _softplus(z):
    e = jnp.exp(-jnp.abs(z))
    u = 1.0 + e
    log1p = jnp.where(u == 1.0, e, jnp.log(u) * e / jnp.where(u == 1.0, 1.0, u - 1.0))
    return jnp.maximum(z, 0.0) + log1p


_GELU_C = math.sqrt(2.0 / math.pi)


def _gelu_and_grad(v):
    t = jnp.tanh(_GELU_C * (v + 0.044715 * v * v * v))
    val = 0.5 * v * (1.0 + t)
    grad = 0.5 * (1.0 + t) + 0.5 * v * (1.0 - t * t) * _GELU_C * (1.0 + 3 * 0.044715 * v * v)
    return val, grad


def _seg_sum(v, seg, passes=3):
    hi = v.astype(jnp.bfloat16)
    r1 = v - hi.astype(_F32)
    mid = r1.astype(jnp.bfloat16)
    out = _dot(hi, seg) + _dot(mid, seg)
    if passes == 3:
        out = out + _dot((r1 - mid.astype(_F32)).astype(jnp.bfloat16), seg)
    return out


_SCAN_BLOCK = 32


def _scan_fwd(a, u, h0):
    n = a.shape[0]
    blk = min(_SCAN_BLOCK, n)
    sub = lax.broadcasted_iota(jnp.int32, a.shape, 0) % blk
    h, d = u, 1
    while d < blk:
        keep = sub >= d
        h = a * jnp.where(keep, pltpu.roll(h, d, 0), 0.0) + h
        a = a * jnp.where(keep, pltpu.roll(a, d, 0), 1.0)
        d *= 2
    out, carry = [], h0
    for b in range(n // blk):
        rows = slice(b * blk, (b + 1) * blk)
        out.append(h[rows] + a[rows] * carry)
        carry = out[-1][blk - 1:blk, :]
    return jnp.concatenate(out, axis=0)


def _scan_rev(c, g, g_end):
    n = c.shape[0]
    blk = min(_SCAN_BLOCK, n)
    sub = lax.broadcasted_iota(jnp.int32, c.shape, 0) % blk
    d = 1
    while d < blk:
        keep = sub < blk - d
        g = c * jnp.where(keep, pltpu.roll(g, n - d, 0), 0.0) + g
        c = c * jnp.where(keep, pltpu.roll(c, n - d, 0), 1.0)
        d *= 2
    out, carry = [None] * (n // blk), g_end
    for b in reversed(range(n // blk)):
        rows = slice(b * blk, (b + 1) * blk)
        out[b] = g[rows] + c[rows] * carry
        carry = out[b][0:1, :]
    return jnp.concatenate(out, axis=0)


def _layer_norm_stats(z):
    mu = jnp.mean(z, axis=-1, keepdims=True)
    zc = z - mu
    var = jnp.mean(zc * zc, axis=-1, keepdims=True)
    rstd = lax.rsqrt(var + _LN_EPS)
    return zc * rstd, rstd


def _layer_norm_bwd(dn, n, rstd):
    return rstd * (dn - jnp.mean(dn, axis=-1, keepdims=True) - n * jnp.mean(dn * n, axis=-1, keepdims=True))


def _rowsum(v):
    return jnp.sum(v, axis=0, keepdims=True)


def _rowsum_halving(v):
    n = v.shape[0]
    while n > 8 and n % 16 == 0:
        v = v[:n // 2] + v[n // 2:]
        n //= 2
    return jnp.sum(v, axis=0, keepdims=True)


def _fused_exchange(body, n_in, n_out, n_scratch, n_xin, n_xout, plan, grid):
    def wrapped(*refs):
        o0 = n_in + n_xin
        s0 = o0 + n_out + n_xout
        start, finish = plan(refs[n_in:o0], refs[o0 + n_out:s0], *refs[s0 + n_scratch:])
        step = 0
        for axis, size in enumerate(grid):
            step = step * size + pl.program_id(axis)

        @pl.when(step == 0)
        def _():
            start()

        body(*refs[:n_in], *refs[o0:o0 + n_out], *refs[s0:s0 + n_scratch])

        @pl.when(step == math.prod(grid) - 1)
        def _():
            finish()

    return wrapped


def _lru_gates(xc, wr_ref, wi_ref, br_ref, bi_ref, lam_ref):
    xcb = _mx(xc)
    r = jax.nn.sigmoid(_dot(xcb, wr_ref[...]) + br_ref[...])
    i = jax.nn.sigmoid(_dot(xcb, wi_ref[...]) + bi_ref[...])
    sp = _softplus(-lam_ref[...])
    log_a = -_LRU_C * r * sp
    a = jnp.exp(log_a)
    mult = jnp.sqrt(-_expm1(2.0 * log_a))
    return r, i, sp, a, mult


def _conv_taps(ext_ref, w_ref, first, n_taps, tt):
    acc = w_ref[0:1, :] * ext_ref[pl.ds(first, tt), :]
    for k in range(1, n_taps):
        acc = acc + w_ref[k:k + 1, :] * ext_ref[pl.ds(first + k, tt), :]
    return acc


def _make_shifted(ext_ref, sh_ref):
    n = sh_ref.shape[1]
    for r in range(1, 8):
        sh_ref[r - 1] = ext_ref[pl.ds(r, n), :]


def _tap(ext_ref, sh_ref, off, tt):
    base = (off // 8) * 8
    if off % 8 == 0:
        return ext_ref[pl.ds(base, tt), :]
    return sh_ref[off % 8 - 1, pl.ds(base, tt), :]


def _conv_taps_shifted(ext_ref, sh_ref, w_ref, first, n_taps, tt):
    acc = w_ref[0:1, :] * _tap(ext_ref, sh_ref, first, tt)
    for k in range(1, n_taps):
        acc = acc + w_ref[k:k + 1, :] * _tap(ext_ref, sh_ref, first + k, tt)
    return acc


def _mix_fwd(x, mod3, win, lcw, lcb, wr_bd, wi_bd, b_r, b_i, lam, cw, cb, ng, nb, seg, wout, ln1g, ln1b, shards):
    bl, s_len, d = x.shape
    w = d // 2
    tt = min(_TT_MIX_FWD, s_len)
    ns = s_len // tt
    kc = cw.shape[0]

    def body(x_ref, mod_ref, win_ref, lcw_ref, lcb_ref, wr_ref, wi_ref, br_ref, bi_ref, lam_ref, cw_ref, cb_ref,
             ng_ref, nb_ref, seg_ref, wout_ref, g1_ref, b1_ref,
             proj_ref, h_ref, mix_ref, x1_ref, u1_ref, y_ref, vbc_ref, lru_ref, ext4, ext31, sh31, hcar):
        @pl.when(pl.program_id(1) == 0)
        def _():
            ext4[0:8, :] = jnp.zeros((8, w), _F32)
            ext31[0:_HALO, :] = jnp.zeros((_HALO, w), _F32)
            hcar[...] = jnp.zeros_like(hcar)

        xt = x_ref[...]
        sh1, sc1, gt1 = mod_ref[:, 0:d], mod_ref[:, d:2 * d], mod_ref[:, 2 * d:3 * d]
        u1 = _mx(xt * (1.0 + sc1) + sh1)
        u1_ref[...] = u1
        xa, ga, vb, gb = (_dot(u1, win_ref[k]) for k in range(4))
        proj_ref[:, 0:w] = xa
        proj_ref[:, w:2 * w] = ga
        proj_ref[:, 2 * w:3 * w] = vb
        proj_ref[:, 3 * w:4 * w] = gb

        ext4[8:8 + tt, :] = xa
        xc = lcb_ref[...] + _conv_taps(ext4, lcw_ref, 5, 4, tt)
        ext4[0:8, :] = xa[tt - 8:tt, :]
        r, i, sp, a, mult = _lru_gates(xc, wr_ref, wi_ref, br_ref, bi_ref, lam_ref)
        for k, val in enumerate((xc, r, i, a, mult)):
            lru_ref[:, k * w:(k + 1) * w] = val
        h = _scan_fwd(a, mult * (i * xc), hcar[0:1, :])
        hcar[0:1, :] = h[tt - 1:tt, :]
        h_ref[...] = h
        gelu, _ = _gelu_and_grad(ga)
        y_ref[:, 0:w] = _mx(gelu * h)

        vbg = vb * jax.nn.sigmoid(gb)
        ext31[_HALO:_HALO + tt, :] = vbg
        _make_shifted(ext31, sh31)
        vbc = cb_ref[...] + _conv_taps_shifted(ext31, sh31, cw_ref, _HALO - (kc - 1), kc, tt)
        vbc_ref[...] = vbc
        ext31[0:_HALO, :] = vbg[tt - _HALO:tt, :]
        inv = 1.0 / (w // _N_HEADS)
        zc = vbc - _seg_sum(vbc, seg_ref[...]) * inv
        n = zc * lax.rsqrt(_seg_sum(zc * zc, seg_ref[...], 2) * inv + _LN_EPS)
        pre = n * ng_ref[...] + nb_ref[...]
        y_ref[:, w:2 * w] = _mx(pre * jax.nn.sigmoid(pre))

        mix = _dot(y_ref[...], wout_ref[...])
        mix_ref[...] = mix
        n1, _ = _layer_norm_stats(_ALPHA * xt + (1.0 + gt1) * mix)
        x1_ref[...] = n1 * g1_ref[...] + b1_ref[...]

    tok = lambda c: pl.BlockSpec((None, tt, c), lambda b, s: (b, s, 0))
    smalls = [lcw, lcb, wr_bd, wi_bd, b_r, b_i, lam, cw, cb, ng, nb, seg, wout, ln1g, ln1b]
    nx = len(shards)
    return pl.pallas_call(
        _fused_exchange(body, 3 + len(smalls), 8, 4, nx, nx, _gather_plan, (bl, ns)), grid=(bl, ns),
        in_specs=[tok(d), pl.BlockSpec((None, 1, 6 * d), lambda b, s: (b, 0, 0)), _resident(win.shape)]
        + [_resident(t.shape) for t in smalls] + [_HBM] * nx,
        out_specs=[tok(4 * w), tok(w), tok(d), tok(d), tok(d), tok(d), tok(w), tok(5 * w)] + [_HBM] * nx,
        out_shape=[jax.ShapeDtypeStruct((bl, s_len, 4 * w), _F32), jax.ShapeDtypeStruct((bl, s_len, w), _F32),
                   jax.ShapeDtypeStruct((bl, s_len, d), _F32), jax.ShapeDtypeStruct((bl, s_len, d), _F32),
                   jax.ShapeDtypeStruct((bl, s_len, d), _MXU_DT), jax.ShapeDtypeStruct((bl, s_len, d), _MXU_DT),
                   jax.ShapeDtypeStruct((bl, s_len, w), _F32), jax.ShapeDtypeStruct((bl, s_len, 5 * w), _F32)]
        + [jax.ShapeDtypeStruct((4,) + t.shape, t.dtype) for t in shards],
        scratch_shapes=[pltpu.VMEM((tt + 8, w), _F32), pltpu.VMEM((tt + _HALO, w), _F32),
                        pltpu.VMEM((7, tt + _HALO - 8, w), _F32), pltpu.VMEM((8, w), _F32)] + _gather_sems(nx),
        compiler_params=_cparams(("arbitrary", "arbitrary")), name="mix_fwd",
    )(x, mod3, win, *smalls, *shards)


def _ffn_fwd(x1, mod3, wup, fcw, fcb, wdn, ln2g, ln2b, target):
    bl, s_len, d = x1.shape
    nch, _, fc = wup.shape
    nch //= 2
    f = nch * fc
    tt = min(_TT_FFN, s_len)
    ns = s_len // tt

    def body(x1_ref, mod_ref, wup_ref, fcw_ref, fcb_ref, wdn_ref, g2_ref, b2_ref, tgt_ref,
             u2_ref, hh_ref, f_ref, gc_ref, dz2_ref, loss_ref, dln2_ref, dgt2_ref, ext3):
        first_tile = pl.program_id(1) == 0

        @pl.when(first_tile)
        def _():
            ext3[:, 0:8, :] = jnp.zeros((nch, 8, fc), _F32)
            dgt2_ref[...] = jnp.zeros_like(dgt2_ref)

        @pl.when(first_tile & (pl.program_id(0) == 0))
        def _():
            loss_ref[...] = jnp.zeros_like(loss_ref)
            dln2_ref[...] = jnp.zeros_like(dln2_ref)

        x1t = x1_ref[...]
        sh2, sc2, gt2 = mod_ref[:, 3 * d:4 * d], mod_ref[:, 4 * d:5 * d], mod_ref[:, 5 * d:6 * d]
        u2 = _mx(x1t * (1.0 + sc2) + sh2)
        u2_ref[...] = u2
        y2 = jnp.zeros((tt, d), _F32)
        for j in range(nch):
            lanes = slice(j * fc, (j + 1) * fc)
            v = _dot(u2, wup_ref[j])
            g = _dot(u2, wup_ref[nch + j])
            hh_ref[:, lanes] = v.astype(hh_ref.dtype)
            hh_ref[:, f + j * fc:f + (j + 1) * fc] = g.astype(hh_ref.dtype)
            ext = ext3.at[j]
            row = lax.broadcasted_iota(jnp.int32, (tt, fc), 0)
            last, before = ext[7:8, :], ext[6:7, :]
            back1 = jnp.where(row == 0, last, pltpu.roll(g, 1, 0))
            back2 = jnp.where(row == 0, before, jnp.where(row == 1, last, pltpu.roll(g, 2, 0)))
            gc = (fcb_ref[:, lanes] + fcw_ref[2:3, lanes] * g + fcw_ref[1:2, lanes] * back1
                  + fcw_ref[0:1, lanes] * back2)
            gc_ref[:, lanes] = gc
            ext[0:8, :] = g[tt - 8:tt, :]
            fj = _mx(gc * jax.nn.sigmoid(gc) * v)
            f_ref[:, lanes] = fj
            y2 = y2 + _dot(fj, wdn_ref[lanes, :])

        n2, rstd = _layer_norm_stats(_ALPHA * x1t + (1.0 + gt2) * y2)
        err = n2 * g2_ref[...] + b2_ref[...] - tgt_ref[...]
        loss_ref[...] += jnp.sum(_rowsum(err * err), axis=1, keepdims=True)
        dout = err * (1.0 / d)
        dln2_ref[0:1, :] += _rowsum(dout * n2)
        dln2_ref[1:2, :] += _rowsum(dout)
        dz2 = _layer_norm_bwd(dout * g2_ref[...], n2, rstd)
        dz2_ref[...] = dz2
        dgt2_ref[...] += _rowsum(dz2 * y2)

    tok = lambda c: pl.BlockSpec((None, tt, c), lambda b, s: (b, s, 0))
    acc = lambda r: pl.BlockSpec((r, d), lambda b, s: (0, 0))
    smalls = [fcw, fcb, wdn, ln2g, ln2b]
    return pl.pallas_call(
        body, grid=(bl, ns),
        in_specs=[tok(d), pl.BlockSpec((None, 1, 6 * d), lambda b, s: (b, 0, 0)), _resident(wup.shape)]
        + [_resident(t.shape) for t in smalls] + [tok(d)],
        out_specs=[tok(d), tok(2 * f), tok(f), tok(f), tok(d), acc(1), acc(2), pl.BlockSpec((None, 1, d), lambda b, s: (b, 0, 0))],
        out_shape=[jax.ShapeDtypeStruct((bl, s_len, d), _MXU_DT), jax.ShapeDtypeStruct((bl, s_len, 2 * f), _F32),
                   jax.ShapeDtypeStruct((bl, s_len, f), _MXU_DT), jax.ShapeDtypeStruct((bl, s_len, f), _F32),
                   jax.ShapeDtypeStruct((bl, s_len, d), _F32), jax.ShapeDtypeStruct((1, d), _F32), jax.ShapeDtypeStruct((2, d), _F32),
                   jax.ShapeDtypeStruct((bl, 1, d), _F32)],
        scratch_shapes=[pltpu.VMEM((nch, tt + 8, fc), _F32)],
        compiler_params=_cparams(("arbitrary", "arbitrary")), name="ffn_fwd",
    )(x1, mod3, wup, *smalls, target)


def _ffn_bwd(dz2, x1, hh, gc_all, mod3, wup, wdn, fcw, fcb):
    bl, s_len, d = x1.shape
    nch, _, fc = wup.shape
    nch //= 2
    f = nch * fc
    tt = min(_TT_FFN, s_len)
    ns = s_len // tt

    def body(dz2_ref, x1_ref, hh_ref, gc_ref, mod_ref, wup_ref, wdn_ref, fcw_ref, fcb_ref,
             dx1_ref, dy2_ref, dh_ref, dfc_ref, dmod_ref, dext, dcar):
        @pl.when(pl.program_id(1) == 0)
        def _():
            dcar[...] = jnp.zeros_like(dcar)
            dmod_ref[...] = jnp.zeros_like(dmod_ref)

        @pl.when((pl.program_id(1) == 0) & (pl.program_id(0) == 0))
        def _():
            dfc_ref[...] = jnp.zeros_like(dfc_ref)

        sc2, gt2 = mod_ref[:, 4 * d:5 * d], mod_ref[:, 5 * d:6 * d]
        dz2t = dz2_ref[...]
        dy2 = _mx((1.0 + gt2) * dz2t)
        dy2_ref[...] = dy2
        du2 = jnp.zeros((tt, d), _F32)
        for j in range(nch):
            lanes = slice(j * fc, (j + 1) * fc)
            glanes = slice(f + j * fc, f + (j + 1) * fc)
            v = hh_ref[:, lanes].astype(_F32)
            g = hh_ref[:, glanes].astype(_F32)
            gc = gc_ref[:, lanes]
            sg = jax.nn.sigmoid(gc)
            df = _dot_nt(dy2, wdn_ref[lanes, :])
            dv = df * (gc * sg)
            dgc = df * v * (sg * (1.0 + gc * (1.0 - sg)))
            dfc_ref[3:4, lanes] += _rowsum_halving(dgc)
            dext[0:tt, :] = dgc
            dext[tt:tt + 8, :] = dcar[j]
            dcar[j] = dgc[0:8, :]
            dg = jnp.zeros((tt, fc), _F32)
            for k in range(3):
                shifted = dext[pl.ds(2 - k, tt), :]
                dg = dg + fcw_ref[k:k + 1, lanes] * shifted
                dfc_ref[k:k + 1, lanes] += _rowsum_halving(shifted * g)
            dvb, dgb = _mx(dv), _mx(dg)
            dh_ref[:, lanes] = dvb
            dh_ref[:, glanes] = dgb
            du2 = du2 + _dot_nt(dvb, wup_ref[j]) + _dot_nt(dgb, wup_ref[nch + j])

        dx1_ref[...] = _ALPHA * dz2t + du2 * (1.0 + sc2)
        dmod_ref[0:1, :] += _rowsum_halving(du2)
        dmod_ref[1:2, :] += _rowsum_halving(du2 * x1_ref[...])

    tok = lambda c: pl.BlockSpec((None, tt, c), lambda b, i: (b, ns - 1 - i, 0))
    return pl.pallas_call(
        body, grid=(bl, ns),
        in_specs=[tok(d), tok(d), tok(2 * f), tok(f), pl.BlockSpec((None, 1, 6 * d), lambda b, i: (b, 0, 0)),
                  _resident(wup.shape), _resident(wdn.shape), _resident(fcw.shape), _resident(fcb.shape)],
        out_specs=[tok(d), tok(d), tok(2 * f), pl.BlockSpec((4, f), lambda b, i: (0, 0)),
                   pl.BlockSpec((None, 2, d), lambda b, i: (b, 0, 0))],
        out_shape=[jax.ShapeDtypeStruct((bl, s_len, d), _F32), jax.ShapeDtypeStruct((bl, s_len, d), _MXU_DT),
                   jax.ShapeDtypeStruct((bl, s_len, 2 * f), _MXU_DT), jax.ShapeDtypeStruct((4, f), _F32),
                   jax.ShapeDtypeStruct((bl, 2, d), _F32)],
        scratch_shapes=[pltpu.VMEM((tt + 8, fc), _F32), pltpu.VMEM((nch, 8, fc), _F32)],
        compiler_params=_cparams(("arbitrary", "arbitrary")), name="ffn_bwd",
    )(dz2, x1, hh, gc_all, mod3, wup, wdn, fcw, fcb)


def _mix_bwd(dx1, x, mix, proj, h, vbc, lru, mod3, win, lcw, lcb, wr_bd, wi_bd, b_r, b_i, lam, cw, cb, ng, nb, seg, wout, ln1g, chip_sums):
    bl, s_len, d = x.shape
    w = d // 2
    tt = min(_TT_MIX, s_len)
    ns = s_len // tt
    kc = cw.shape[0]

    def body(dx1_ref, x_ref, mix_ref, proj_ref, phalo_ref, h_ref, hhalo_ref, vbc_ref, lru_ref, mod_ref, win_ref, lcw_ref, lcb_ref,
             wr_ref, wi_ref, br_ref, bi_ref, lam_ref, cw_ref, cb_ref, ng_ref, nb_ref, seg_ref, wout_ref, g1_ref,
             gx_ref, dproj_ref, dmix_ref, xcg_ref, vecw_ref, dlcw_ref, dcw_ref, dln1_ref, dmod_ref,
             ext4, ext31, dext4, dext31, sh31, dsh31, car4, car31, gcar):
        s = ns - 1 - pl.program_id(1)
        first = s == 0

        @pl.when(pl.program_id(1) == 0)
        def _():
            car4[...] = jnp.zeros_like(car4)
            car31[...] = jnp.zeros_like(car31)
            gcar[...] = jnp.zeros_like(gcar)
            dmod_ref[...] = jnp.zeros_like(dmod_ref)

        @pl.when((pl.program_id(1) == 0) & (pl.program_id(0) == 0))
        def _():
            for ref in (vecw_ref, dlcw_ref, dcw_ref, dln1_ref):
                ref[...] = jnp.zeros_like(ref)

        xt, mixt = x_ref[...], mix_ref[...]
        sh1, sc1, gt1 = mod_ref[:, 0:d], mod_ref[:, d:2 * d], mod_ref[:, 2 * d:3 * d]

        n1, rstd1 = _layer_norm_stats(_ALPHA * xt + (1.0 + gt1) * mixt)
        dx1t = dx1_ref[...]
        dln1_ref[0:1, :] += _rowsum(dx1t * n1)
        dln1_ref[1:2, :] += _rowsum(dx1t)
        dz1 = _layer_norm_bwd(dx1t * g1_ref[...], n1, rstd1)
        dmod_ref[2:3, :] += _rowsum(dz1 * mixt)
        dmix = _mx((1.0 + gt1) * dz1)
        dmix_ref[...] = dmix
        dya = _dot_nt(dmix, wout_ref[0:w, :])
        dyb = _dot_nt(dmix, wout_ref[w:2 * w, :])

        xa, ga = proj_ref[:, 0:w], proj_ref[:, w:2 * w]
        vb, gb = proj_ref[:, 2 * w:3 * w], proj_ref[:, 3 * w:4 * w]

        sgb = jax.nn.sigmoid(gb)
        vbg = vb * sgb
        hv, hg = phalo_ref[:, 2 * w:3 * w], phalo_ref[:, 3 * w:4 * w]
        ext31[0:_HALO, :] = jnp.where(first, 0.0, hv * jax.nn.sigmoid(hg))
        ext31[_HALO:_HALO + tt, :] = vbg
        _make_shifted(ext31, sh31)
        vbc = vbc_ref[...]
        inv = 1.0 / (w // _N_HEADS)
        zc = vbc - _seg_sum(vbc, seg_ref[...]) * inv
        rstd = lax.rsqrt(_seg_sum(zc * zc, seg_ref[...], 2) * inv + _LN_EPS)
        n = zc * rstd
        pre = n * ng_ref[...] + nb_ref[...]
        sgp = jax.nn.sigmoid(pre)
        dpre = dyb * (sgp * (1.0 + pre * (1.0 - sgp)))
        vecw_ref[5:6, :] += _rowsum(dpre * n)
        vecw_ref[6:7, :] += _rowsum(dpre)
        dn = dpre * ng_ref[...]
        dvbc = rstd * (dn - _seg_sum(dn, seg_ref[...], 2) * inv - n * (_seg_sum(dn * n, seg_ref[...], 2) * inv))
        vecw_ref[4:5, :] += _rowsum(dvbc)
        dext31[0:tt, :] = dvbc
        dext31[tt:tt + _HALO, :] = car31[...]
        car31[...] = dvbc[0:_HALO, :]
        _make_shifted(dext31, dsh31)
        dvbg = jnp.zeros((tt, w), _F32)
        for k in range(kc):
            dvbg = dvbg + cw_ref[k:k + 1, :] * _tap(dext31, dsh31, kc - 1 - k, tt)
            dcw_ref[k:k + 1, :] += _rowsum(dvbc * _tap(ext31, sh31, _HALO - (kc - 1) + k, tt))
        dproj_ref[:, 2 * w:3 * w] = _mx(dvbg * sgb)
        dproj_ref[:, 3 * w:4 * w] = _mx(dvbg * vb * (sgb * (1.0 - sgb)))

        ext4[0:8, :] = jnp.where(first, 0.0, phalo_ref[_HALO - 8:_HALO, 0:w])
        ext4[8:8 + tt, :] = xa
        xc, r, i, a, mult = (lru_ref[:, k * w:(k + 1) * w] for k in range(5))
        xcg_ref[:, 0:w] = _mx(xc)
        sp = _softplus(-lam_ref[...])
        ht = h_ref[...]
        row = lax.broadcasted_iota(jnp.int32, (tt, w), 0)
        h_before = jnp.where(first, 0.0, hhalo_ref[7:8, :])
        hprev = jnp.where(row == 0, h_before, pltpu.roll(ht, 1, 0))
        gelu, dgelu = _gelu_and_grad(ga)
        dproj_ref[:, w:2 * w] = _mx(dya * ht * dgelu)
        dh = dya * gelu
        coef = jnp.where(row == tt - 1, 1.0, pltpu.roll(a, tt - 1, 0))
        big_g = _scan_rev(coef, dh, gcar[0:1, :])
        gcar[0:1, :] = a[0:1, :] * big_g[0:1, :]
        da = big_g * hprev
        ixc = i * xc
        dlog_a = da * a - (big_g * ixc) * (a * a / mult)
        di = big_g * mult * xc
        dxc = big_g * mult * i
        vecw_ref[3:4, :] += _rowsum(dlog_a * r) * (_LRU_C * jax.nn.sigmoid(-lam_ref[...]))
        dgr_f = dlog_a * (-_LRU_C * sp) * (r * (1.0 - r))
        dgi_f = di * (i * (1.0 - i))
        vecw_ref[1:2, :] += _rowsum(dgr_f)
        vecw_ref[2:3, :] += _rowsum(dgi_f)
        dgr, dgi = _mx(dgr_f), _mx(dgi_f)
        xcg_ref[:, w:2 * w] = dgr
        xcg_ref[:, 2 * w:3 * w] = dgi
        dxc = dxc + _dot_nt(dgr, wr_ref[...]) + _dot_nt(dgi, wi_ref[...])
        vecw_ref[0:1, :] += _rowsum(dxc)
        dext4[0:tt, :] = dxc
        dext4[tt:tt + 8, :] = car4[...]
        car4[...] = dxc[0:8, :]
        dxa = jnp.zeros((tt, w), _F32)
        for k in range(4):
            dxa = dxa + lcw_ref[k:k + 1, :] * dext4[pl.ds(3 - k, tt), :]
            dlcw_ref[k:k + 1, :] += _rowsum(dxc * ext4[pl.ds(5 + k, tt), :])
        dproj_ref[:, 0:w] = _mx(dxa)

        du1 = sum(_dot_nt(dproj_ref[:, k * w:(k + 1) * w], win_ref[k]) for k in range(4))
        gx_ref[...] = _ALPHA * dz1 + du1 * (1.0 + sc1)
        dmod_ref[0:1, :] += _rowsum(du1)
        dmod_ref[1:2, :] += _rowsum(du1 * xt)

    tok = lambda c: pl.BlockSpec((None, tt, c), lambda b, i: (b, ns - 1 - i, 0))
    halo = lambda rows, c: pl.BlockSpec(
        (None, rows, c), lambda b, i: (b, jnp.maximum((ns - 1 - i) * (tt // rows) - 1, 0), 0))
    accw = lambda r, c: pl.BlockSpec((r, c), lambda b, i: (0, 0))
    smalls = [lcw, lcb, wr_bd, wi_bd, b_r, b_i, lam, cw, cb, ng, nb, seg, wout, ln1g]
    nx = len(chip_sums)
    return pl.pallas_call(
        _fused_exchange(body, 11 + len(smalls), 9, 9, nx, nx, _chip_reduce_plan, (bl, ns)), grid=(bl, ns),
        in_specs=[tok(d), tok(d), tok(d), tok(4 * w), halo(_HALO, 4 * w), tok(w), halo(8, w), tok(w), tok(5 * w),
                  pl.BlockSpec((None, 1, 6 * d), lambda b, i: (b, 0, 0)), _resident(win.shape)]
        + [_resident(t.shape) for t in smalls] + [_HBM] * nx,
        out_specs=[tok(d), tok(4 * w), tok(d), tok(3 * w), accw(8, w), accw(4, w), accw(kc, w), accw(2, d),
                   pl.BlockSpec((None, 3, d), lambda b, i: (b, 0, 0))] + [_HBM] * nx,
        out_shape=[jax.ShapeDtypeStruct((bl, s_len, d), _F32), jax.ShapeDtypeStruct((bl, s_len, 4 * w), _MXU_DT),
                   jax.ShapeDtypeStruct((bl, s_len, d), _MXU_DT), jax.ShapeDtypeStruct((bl, s_len, 3 * w), _MXU_DT),
                   jax.ShapeDtypeStruct((8, w), _F32), jax.ShapeDtypeStruct((4, w), _F32),
                   jax.ShapeDtypeStruct((kc, w), _F32), jax.ShapeDtypeStruct((2, d), _F32),
                   jax.ShapeDtypeStruct((bl, 3, d), _F32)]
        + [jax.ShapeDtypeStruct((3,) + t.shape[1:], t.dtype) for t in chip_sums],
        scratch_shapes=[pltpu.VMEM((tt + 8, w), _F32), pltpu.VMEM((tt + _HALO, w), _F32),
                        pltpu.VMEM((tt + 8, w), _F32), pltpu.VMEM((tt + _HALO, w), _F32),
                        pltpu.VMEM((7, tt + _HALO - 8, w), _F32), pltpu.VMEM((7, tt + _HALO - 8, w), _F32),
                        pltpu.VMEM((8, w), _F32), pltpu.VMEM((_HALO, w), _F32), pltpu.VMEM((8, w), _F32)]
        + _chip_reduce_sems(nx),
        compiler_params=_cparams(("arbitrary", "arbitrary")), name="mix_bwd",
    )(dx1, x, mix, proj, proj, h, h, vbc, lru, mod3, win, *smalls, *chip_sums)


def _wgrad(a, b, ma, nbw, na, nb, a_off, b_off, name, exchange=None):
    t = a.shape[0]
    tk = min(_TK_WGRAD, t)
    grid = (na * nb, t // tk)

    def body(a_ref, b_ref, o_ref):
        @pl.when(pl.program_id(1) == 0)
        def _():
            o_ref[...] = jnp.zeros_like(o_ref)
        o_ref[...] += _dot_tn(a_ref[...], b_ref[...])

    xin, xshapes, plan, sems = exchange if exchange else ([], [], None, [])
    nx = len(xin)
    res = pl.pallas_call(
        _fused_exchange(body, 2, 1, 0, nx, len(xshapes), plan, grid) if exchange else body, grid=grid,
        in_specs=[pl.BlockSpec((tk, ma), lambda j, k: (k, j // nb + a_off)),
                  pl.BlockSpec((tk, nbw), lambda j, k: (k, j % nb + b_off))] + [_HBM] * nx,
        out_specs=[pl.BlockSpec((None, ma, nbw), lambda j, k: (j, 0, 0))] + [_HBM] * len(xshapes),
        out_shape=[jax.ShapeDtypeStruct((na * nb, ma, nbw), _F32)] + list(xshapes),
        scratch_shapes=list(sems),
        compiler_params=_cparams(("arbitrary", "arbitrary")), name=name,
    )(a, b, *xin)
    return res if exchange else res[0]


_DEV_DELTAS = tuple(dl for dl in itertools.product((0, 1), repeat=3) if any(dl))
_HBM = pl.BlockSpec(memory_space=pltpu.HBM)
_VMEM = pl.BlockSpec(memory_space=pltpu.VMEM)


def _pos():
    return lax.axis_index("x"), lax.axis_index("y"), lax.axis_index("c")


def _flip(v, delta):
    return 1 - v if delta else v


def _remote(src, dst, ssem, rsem, dev):
    return pltpu.make_async_remote_copy(src_ref=src, dst_ref=dst, send_sem=ssem, recv_sem=rsem,
                                        device_id=dev, device_id_type=_MESH)


def _rows(ref, idx, n):
    return ref.at[pl.ds(pl.multiple_of(idx * n, 8), n)]


def _ada_fwd(c8, w_ada_k, b_ada_k, shards):
    rows, d = c8.shape
    nk = w_ada_k.shape[1]
    n = len(shards)

    def body(*refs):
        c_ref, w_ref, b_ref = refs[:3]
        call_ref, mod_ref = refs[3 + n:5 + n]
        modloc, modrcv, s1, r1, s2, r2 = refs[5 + 2 * n:11 + 2 * n]
        gather_start, gather_finish = _gather_plan(refs[3:3 + n], refs[5 + n:5 + 2 * n], *refs[11 + 2 * n:14 + 2 * n],
                                                   fsem=refs[14 + 2 * n], frsem=refs[15 + 2 * n], bounce=refs[16 + 2 * n:])
        gather_start()
        xi, yi, ci = _pos()
        me, kme = 4 * xi + 2 * yi + ci, 2 * xi + yi
        call_ref[pl.ds(pl.multiple_of(me * rows, 8), rows), :] = c_ref[...]
        sends = []
        for p, (dx, dy, dc) in enumerate(_DEV_DELTAS):
            cp = _remote(c_ref, _rows(call_ref, me, rows), s1.at[p], r1.at[p], (_flip(xi, dx), _flip(yi, dy), _flip(ci, dc)))
            cp.start()
            sends.append(cp)
        for p, (dx, dy, dc) in enumerate(_DEV_DELTAS):
            src = 4 * _flip(xi, dx) + 2 * _flip(yi, dy) + _flip(ci, dc)
            _remote(c_ref, _rows(call_ref, src, rows), s1.at[p], r1.at[p], (xi, yi, ci)).wait_recv()
        for cp in sends:
            cp.wait_send()

        ca = call_ref[...]
        modloc[...] = _dot(_mx(ca * jax.nn.sigmoid(ca)), _mx(w_ref[...])) + b_ref[...]
        modrcv[kme] = modloc[pl.ds(pl.multiple_of(me * rows, 8), rows), :]
        sends = []
        for j, (dx, dy) in enumerate(_CHIP_DELTAS):
            tx, ty = _flip(xi, dx), _flip(yi, dy)
            cp = _remote(_rows(modloc, 4 * tx + 2 * ty + ci, rows), modrcv.at[kme], s2.at[j], r2.at[j], (tx, ty, ci))
            cp.start()
            sends.append(cp)
        for j, (dx, dy) in enumerate(_CHIP_DELTAS):
            ksrc = 2 * _flip(xi, dx) + _flip(yi, dy)
            _remote(_rows(modloc, me, rows), modrcv.at[ksrc], s2.at[j], r2.at[j], (xi, yi, ci)).wait_recv()
        for cp in sends:
            cp.wait_send()
        for j in range(4):
            mod_ref[:, j * nk:(j + 1) * nk] = modrcv[j]
        gather_finish()

    return pl.pallas_call(
        body, in_specs=[_VMEM, _VMEM, _VMEM] + [_HBM] * n, out_specs=[_VMEM, _VMEM] + [_HBM] * n,
        out_shape=[jax.ShapeDtypeStruct((8 * rows, d), _F32), jax.ShapeDtypeStruct((rows, 4 * nk), _F32)]
        + [jax.ShapeDtypeStruct((4,) + a.shape, a.dtype) for a in shards],
        scratch_shapes=[pltpu.VMEM((8 * rows, nk), _F32), pltpu.VMEM((4, rows, nk), _F32),
                        pltpu.SemaphoreType.DMA((7,)), pltpu.SemaphoreType.DMA((7,)),
                        pltpu.SemaphoreType.DMA((3,)), pltpu.SemaphoreType.DMA((3,))]
        + _gather_sems(n) + [pltpu.SemaphoreType.DMA((3, n)), pltpu.SemaphoreType.DMA((3, n))]
        + [pltpu.VMEM(a.shape, a.dtype) for a in shards],
        compiler_params=pltpu.CompilerParams(vmem_limit_bytes=_VMEM_LIMIT), name="ada_fwd",
    )(c8, w_ada_k, b_ada_k, *shards)


def _gather_sems(n):
    return [pltpu.SemaphoreType.DMA((3, n)), pltpu.SemaphoreType.DMA((3, n)), pltpu.SemaphoreType.DMA((n,))]


def _gather_plan(ins, outs, ssem, rsem, lsem, bounce=(), fsem=None, frsem=None):
    n = len(ins)
    xi, yi, ci = _pos()
    kme = 2 * xi + yi
    split = [fsem is not None and ins[a].shape[0] % 32 == 0 for a in range(n)]

    def half(ref, a, which):
        r2 = ins[a].shape[0] // 2
        return ref.at[pl.ds(pl.multiple_of(which * r2, 16), r2)]

    staged = [pltpu.make_async_copy(ins[a], bounce[a], lsem.at[a]) for a in range(len(bounce))]
    local = [pltpu.make_async_copy(bounce[a] if bounce else ins[a], outs[a].at[kme], lsem.at[a]) for a in range(n)]
    sends, recvs, forwards, handed = [], [], [], []
    for j, (dx, dy) in enumerate(_CHIP_DELTAS):
        tx, ty = _flip(xi, dx), _flip(yi, dy)
        for a in range(n):
            sems = (ssem.at[j, a], rsem.at[j, a])
            landing = outs[a].at[2 * tx + ty]
            if split[a]:
                sends.append(_remote(half(ins[a], a, ci), half(outs[a].at[kme], a, ci), *sems, (tx, ty, ci)))
                recvs.append(_remote(half(ins[a], a, ci), half(landing, a, ci), *sems, (xi, yi, ci)))
                fsems = (fsem.at[j, a], frsem.at[j, a])
                forwards.append(_remote(half(landing, a, ci), half(landing, a, ci), *fsems, (xi, yi, 1 - ci)))
                handed.append(_remote(half(ins[a], a, 1 - ci), half(landing, a, 1 - ci), *fsems, (xi, yi, ci)))
            else:
                sends.append(_remote(ins[a], outs[a].at[kme], *sems, (tx, ty, ci)))
                recvs.append(_remote(ins[a], landing, *sems, (xi, yi, ci)))
                forwards.append(None)

    def start():
        for cp in sends + staged:
            cp.start()
        for cp in staged:
            cp.wait()
        for cp in local:
            cp.start()

    def finish():
        for arrived, forward in zip(recvs, forwards):
            arrived.wait_recv()
            if forward is not None:
                forward.start()
        for cp in handed:
            cp.wait_recv()
        for cp in sends + [f for f in forwards if f is not None]:
            cp.wait_send()
        for cp in local:
            cp.wait()

    return start, finish


def _dev_gather_sems(n):
    return [pltpu.SemaphoreType.DMA((7, n)), pltpu.SemaphoreType.DMA((7, n)), pltpu.SemaphoreType.DMA((n,))]


def _dev_gather_plan(ins, outs, ssem, rsem, lsem):
    n = len(ins)
    xi, yi, ci = _pos()
    me, sibling = 4 * xi + 2 * yi + ci, (xi, yi, 1 - ci)
    local = [pltpu.make_async_copy(ins[a], outs[a].at[me], lsem.at[a]) for a in range(n)]
    sends = [_remote(ins[a], outs[a].at[me], ssem.at[0, a], rsem.at[0, a], sibling) for a in range(n)]
    handed = [_remote(ins[a], outs[a].at[4 * xi + 2 * yi + 1 - ci], ssem.at[0, a], rsem.at[0, a], (xi, yi, ci))
              for a in range(n)]
    arrivals, forwards = [], []
    for j, (dx, dy) in enumerate(_CHIP_DELTAS):
        tx, ty = _flip(xi, dx), _flip(yi, dy)
        for a in range(n):
            over_ici, over_d2d = (ssem.at[1 + j, a], rsem.at[1 + j, a]), (ssem.at[4 + j, a], rsem.at[4 + j, a])
            landing = outs[a].at[4 * tx + 2 * ty + ci]
            sends.append(_remote(ins[a], outs[a].at[me], *over_ici, (tx, ty, ci)))
            arrivals.append(_remote(ins[a], landing, *over_ici, (xi, yi, ci)))
            forwards.append(_remote(landing, landing, *over_d2d, sibling))
            handed.append(_remote(ins[a], outs[a].at[4 * tx + 2 * ty + 1 - ci], *over_d2d, (xi, yi, ci)))

    def start():
        for cp in local + sends:
            cp.start()

    def finish():
        for arrived, forward in zip(arrivals, forwards):
            arrived.wait_recv()
            forward.start()
        for cp in handed:
            cp.wait_recv()
        for cp in sends + forwards:
            cp.wait_send()
        for cp in local:
            cp.wait()

    return start, finish


def _pair_sems(n):
    return [pltpu.SemaphoreType.DMA((n,)), pltpu.SemaphoreType.DMA((n,))]


def _pair_plan(ins, outs, ssem, rsem):
    xi, yi, ci = _pos()
    sends = []
    for a in range(len(ins)):
        r2 = ins[a].shape[1] // 2
        src = ins[a].at[:, pl.ds(pl.multiple_of((1 - ci) * r2, 8), r2), :]
        sends.append(_remote(src, outs[a], ssem.at[a], rsem.at[a], (xi, yi, 1 - ci)))

    def start():
        for cp in sends:
            cp.start()

    def finish():
        for cp in sends:
            cp.wait_recv()
        for cp in sends:
            cp.wait_send()

    return start, finish


def _chip_reduce_sems(n):
    return [pltpu.SemaphoreType.DMA((3, n)), pltpu.SemaphoreType.DMA((3, n))]


def _chip_reduce_plan(ins, outs, ssem, rsem):
    xi, yi, ci = _pos()
    sends = []
    for j, (dx, dy) in enumerate(_CHIP_DELTAS):
        tx, ty = _flip(xi, dx), _flip(yi, dy)
        sends += [_remote(ins[a].at[2 * tx + ty], outs[a].at[j], ssem.at[j, a], rsem.at[j, a], (tx, ty, ci))
                  for a in range(len(ins))]

    def start():
        for cp in sends:
            cp.start()

    def finish():
        for cp in sends:
            cp.wait_recv()
        for cp in sends:
            cp.wait_send()

    return start, finish


def _pair_exchange(gs, name):
    n = len(gs)

    def body(*refs):
        start, finish = _pair_plan(refs[:n], refs[n:2 * n], *refs[2 * n:])
        start()
        finish()

    return pl.pallas_call(
        body, in_specs=[_HBM] * n, out_specs=[_HBM] * n, out_shape=_pair_out_shapes(gs),
        scratch_shapes=_pair_sems(n), name=name,
    )(*gs)


def _pair_out_shapes(gs):
    return [jax.ShapeDtypeStruct((g.shape[0], g.shape[1] // 2, g.shape[2]), g.dtype) for g in gs]


def _row_tile(r):
    return max(t for t in range(8, min(r, 256) + 1, 8) if r % t == 0)


def _pair_add(g, r, cidx, name, wire_dtype=None, exchange=None):
    nk, r2, c = r.shape
    tr = _row_tile(r2)
    nt = r2 // tr
    xin, xshapes, plan, sems = exchange if exchange else ([], [], None, [])
    nx = len(xin)

    def body(c_ref, g_ref, r_ref, *o_refs):
        s = g_ref[...] + r_ref[...]
        for o_ref in o_refs:
            o_ref[...] = s.astype(o_ref.dtype)

    out_spec = pl.BlockSpec((None, tr, c), lambda k, i, cr: (k, i, 0))
    dtypes = [_F32] + ([wire_dtype] if wire_dtype else [])
    res = pl.pallas_call(
        _fused_exchange(body, 3, len(dtypes), 0, nx, len(xshapes), plan, (nk, nt)) if exchange else body,
        grid_spec=pltpu.PrefetchScalarGridSpec(
            num_scalar_prefetch=1, grid=(nk, nt),
            in_specs=[pl.BlockSpec((None, tr, c), lambda k, i, cr: (k, cr[0] * nt + i, 0)), out_spec] + [_HBM] * nx,
            out_specs=[out_spec] * len(dtypes) + [_HBM] * len(xshapes), scratch_shapes=list(sems)),
        out_shape=[jax.ShapeDtypeStruct(r.shape, dt) for dt in dtypes] + list(xshapes),
        compiler_params=_cparams(("arbitrary", "arbitrary")), name=name,
    )(cidx, g, r, *xin)
    return res if wire_dtype or exchange else res[0]


def _chip_exchange(ss):
    n = len(ss)

    def body(*refs):
        start, finish = _chip_reduce_plan(refs[:n], refs[n:2 * n], *refs[2 * n:])
        start()
        finish()

    return pl.pallas_call(
        body, in_specs=[_HBM] * n, out_specs=[_HBM] * n,
        out_shape=[jax.ShapeDtypeStruct((3,) + s.shape[1:], s.dtype) for s in ss],
        scratch_shapes=_chip_reduce_sems(n), name="grad_chip_exchange",
    )(*ss)


def _chip_add(s, r, kidx, name):
    _, r2, c = r.shape
    tr = _row_tile(r2)

    def body(k_ref, s_ref, r_ref, o_ref):
        o_ref[...] = ((s_ref[...] + r_ref[0].astype(_F32)) + r_ref[1].astype(_F32)) + r_ref[2].astype(_F32)

    return pl.pallas_call(
        body, grid_spec=pltpu.PrefetchScalarGridSpec(
            num_scalar_prefetch=1, grid=(r2 // tr,),
            in_specs=[pl.BlockSpec((None, tr, c), lambda i, kr: (kr[0], i, 0)),
                      pl.BlockSpec((3, tr, c), lambda i, kr: (0, i, 0))],
            out_specs=pl.BlockSpec((tr, c), lambda i, kr: (i, 0))),
        out_shape=jax.ShapeDtypeStruct((r2, c), _F32),
        compiler_params=_cparams(("arbitrary",)), name=name,
    )(kidx, s, r)


def _pair_swap_plan(ins, outs, ssem, rsem):
    xi, yi, ci = _pos()
    sends = [_remote(ins[a], outs[a], ssem.at[a], rsem.at[a], (xi, yi, 1 - ci)) for a in range(len(ins))]

    def start():
        for cp in sends:
            cp.start()

    def finish():
        for cp in sends:
            cp.wait_recv()
        for cp in sends:
            cp.wait_send()

    return start, finish


def _pair_swap(hs, name):
    n = len(hs)

    def body(*refs):
        start, finish = _pair_swap_plan(refs[:n], refs[n:2 * n], *refs[2 * n:])
        start()
        finish()

    return pl.pallas_call(
        body, in_specs=[_HBM] * n, out_specs=[_HBM] * n,
        out_shape=[jax.ShapeDtypeStruct(h.shape, h.dtype) for h in hs],
        scratch_shapes=[pltpu.SemaphoreType.DMA((n,)), pltpu.SemaphoreType.DMA((n,))], name=name,
    )(*hs)


def _small_sum(every):
    def body(all_ref, sum_ref):
        tot = all_ref[0]
        for dev in range(1, 8):
            tot = tot + all_ref[dev]
        sum_ref[...] = tot

    return pl.pallas_call(
        body, in_specs=[_VMEM], out_specs=_VMEM, out_shape=jax.ShapeDtypeStruct(every.shape[1:], _F32),
        compiler_params=pltpu.CompilerParams(vmem_limit_bytes=_VMEM_LIMIT), name="small_sum",
    )(every)


def _adamw(w, g, m, v):
    m = _ADAM_B1 * m + (1.0 - _ADAM_B1) * g
    v = _ADAM_B2 * v + (1.0 - _ADAM_B2) * (g * g)
    m_hat = m / (1.0 - _ADAM_B1 ** _ADAM_STEP)
    v_hat = v / (1.0 - _ADAM_B2 ** _ADAM_STEP)
    return -_ADAM_LR * (m_hat / (jnp.sqrt(v_hat) + _ADAM_EPS) + _ADAM_WD * w), m, v


def _adamw_big(w, g_mine, g_theirs, m, v, cidx, name):
    r, c = w.shape
    tr = _row_tile(r // 2)
    nt = r // 2 // tr

    def body(c_ref, w_ref, gm_ref, gt_ref, m_ref, v_ref, g_ref, d_ref, mo_ref, vo_ref):
        g = jnp.where(pl.program_id(0) // nt == c_ref[0], gm_ref[...], gt_ref[...])
        g_ref[...] = g
        d_ref[...], mo_ref[...], vo_ref[...] = _adamw(w_ref[...], g, m_ref[...], v_ref[...])

    spec = pl.BlockSpec((tr, c), lambda i, cr: (i, 0))
    half = pl.BlockSpec((tr, c), lambda i, cr: (i % nt, 0))
    return pl.pallas_call(
        body, grid_spec=pltpu.PrefetchScalarGridSpec(
            num_scalar_prefetch=1, grid=(2 * nt,), in_specs=[spec, half, half, spec, spec], out_specs=[spec] * 4),
        out_shape=[jax.ShapeDtypeStruct((r, c), _F32)] * 4,
        compiler_params=_cparams(("arbitrary",)), name=name,
    )(cidx, w, g_mine, g_theirs, m, v)


def _adamw_small(ws, gs, ms, vs):
    n = len(ws)
    summed = [i for i in range(n) if gs[i].shape != ws[i].shape]

    def body(*refs):
        w_r, g_r, m_r, v_r = (refs[i * n:(i + 1) * n] for i in range(4))
        outs = refs[4 * n:]
        for i in range(n):
            g = g_r[i][...]
            if i in summed:
                g = _rowsum(g)
                outs[3 * n + summed.index(i)][...] = g
            outs[i][...], outs[n + i][...], outs[2 * n + i][...] = _adamw(w_r[i][...], g, m_r[i][...], v_r[i][...])

    shapes = [jax.ShapeDtypeStruct(w.shape, _F32) for w in ws]
    res = pl.pallas_call(
        body, in_specs=[_VMEM] * (4 * n), out_specs=[_VMEM] * (3 * n + len(summed)),
        out_shape=shapes * 3 + [shapes[i] for i in summed],
        compiler_params=pltpu.CompilerParams(vmem_limit_bytes=_VMEM_LIMIT), name="adamw_small",
    )(*ws, *gs, *ms, *vs)
    gs = list(gs)
    for pos, i in enumerate(summed):
        gs[i] = res[3 * n + pos]
    return gs, res[:n], res[n:2 * n], res[2 * n:3 * n]


def _ada_bwd(c_all, dmod_k, w, m, v):
    d, nk = w.shape
    tn = 512 if nk % 512 == 0 else nk

    def body(c_ref, dm_ref, w_ref, m_ref, v_ref, g_ref, d_ref, mo_ref, vo_ref):
        ca = c_ref[...]
        g = _dot_tn(_mx(ca * jax.nn.sigmoid(ca)), _mx(dm_ref[...]))
        g_ref[...] = g
        d_ref[...], mo_ref[...], vo_ref[...] = _adamw(w_ref[...], g, m_ref[...], v_ref[...])

    col = pl.BlockSpec((d, tn), lambda j: (0, j))
    return pl.pallas_call(
        body, grid=(nk // tn,),
        in_specs=[pl.BlockSpec(c_all.shape, lambda j: (0, 0)), pl.BlockSpec((c_all.shape[0], tn), lambda j: (0, j)),
                  col, col, col],
        out_specs=[col] * 4, out_shape=[jax.ShapeDtypeStruct((d, nk), _F32)] * 4,
        compiler_params=_cparams(("arbitrary",)), name="ada_bwd",
    )(c_all, dmod_k, w, m, v)


def _block_diag(wh):
    hn, dh, _ = wh.shape
    eye = jnp.eye(hn, dtype=wh.dtype)
    return (eye[:, None, :, None] * wh[:, :, None, :]).reshape(hn * dh, hn * dh)


def _pack(pieces):
    out = []
    for p in pieces:
        flat = p.reshape(-1, 128)
        out.append(jnp.pad(flat, ((0, (-flat.shape[0]) % 8), (0, 0))))
    return jnp.concatenate(out, axis=0)


def _unpack(pack, shapes):
    out, off = [], 0
    for shp in shapes:
        rows = math.prod(shp) // 128
        out.append(pack[..., off:off + rows, :].reshape(pack.shape[:-2] + tuple(shp)))
        off += rows + (-rows) % 8
    return out


_WEIGHTS = ('w_ada', 'b_ada', 'w_in', 'lru_conv_w', 'lru_conv_b', 'lru_w_r', 'lru_b_r', 'lru_w_i', 'lru_b_i', 'lru_lambda',
            'conv_w', 'conv_b', 'conv_norm_g', 'conv_norm_b', 'w_out', 'ln1_g', 'ln1_b', 'ffn_w_up', 'ffn_conv_w',
            'ffn_conv_b', 'ffn_w_down', 'ln2_g', 'ln2_b')
_BIG = ('w_in', 'w_out', 'ffn_w_up', 'ffn_w_down')


def kernel(x, c, w_ada, b_ada, w_in, lru_conv_w, lru_conv_b, lru_w_r, lru_b_r, lru_w_i, lru_b_i, lru_lambda, conv_w, conv_b, conv_norm_g, conv_norm_b, w_out, ln1_g, ln1_b, ffn_w_up, ffn_conv_w, ffn_conv_b, ffn_w_down, ln2_g, ln2_b, loss_target, m_w_ada, m_b_ada, m_w_in, m_lru_conv_w, m_lru_conv_b, m_lru_w_r, m_lru_b_r, m_lru_w_i, m_lru_b_i, m_lru_lambda, m_conv_w, m_conv_b, m_conv_norm_g, m_conv_norm_b, m_w_out, m_ln1_g, m_ln1_b, m_ffn_w_up, m_ffn_conv_w, m_ffn_conv_b, m_ffn_w_down, m_ln2_g, m_ln2_b, v_w_ada, v_b_ada, v_w_in, v_lru_conv_w, v_lru_conv_b, v_lru_w_r, v_lru_b_r, v_lru_w_i, v_lru_b_i, v_lru_lambda, v_conv_w, v_conv_b, v_conv_norm_g, v_conv_norm_b, v_w_out, v_ln1_g, v_ln1_b, v_ffn_w_up, v_ffn_conv_w, v_ffn_conv_b, v_ffn_w_down, v_ln2_g, v_ln2_b):
    given = dict(locals())
    wt = {n: given[n] for n in _WEIGHTS}
    mo = {n: given["m_" + n] for n in _WEIGHTS}
    vo = {n: given["v_" + n] for n in _WEIGHTS}
    bl, s_len, d = x.shape
    wd = d // 2
    tokens = bl * s_len
    xi, yi, ci = _pos()
    kme = 2 * xi + yi
    kidx = jnp.reshape(kme, (1,)).astype(jnp.int32)
    cidx = jnp.reshape(ci, (1,)).astype(jnp.int32)

    nk = w_ada.shape[2]
    c8 = jnp.pad(c, ((0, 8 - bl), (0, 0)))
    c_all, mod8, win, wout_s, lcw_s, cw_s, fcw_s = _ada_fwd(
        c8, w_ada[0], lax.dynamic_slice(b_ada, (0, kme * nk), (1, nk)),
        [_mx(w_in[0]), _mx(w_out[0]), lru_conv_w[0], conv_w[0], ffn_conv_w[0]])
    mod3 = mod8[:bl].reshape(bl, 1, 6 * d)
    wout = wout_s.reshape(d, d)
    f = 4 * ffn_w_down.shape[1]
    unshard = lambda t: jnp.transpose(t, (1, 0, 2)).reshape(t.shape[1], -1)
    lcw, cw, fcw = unshard(lcw_s), unshard(cw_s), unshard(fcw_s)
    wr_bd, wi_bd = _mx(_block_diag(lru_w_r[0])), _mx(_block_diag(lru_w_i[0]))
    seg = _block_diag(jnp.ones((_N_HEADS, wd // _N_HEADS, wd // _N_HEADS), jnp.bfloat16))
    mixer_small = (lcw, lru_conv_b, wr_bd, wi_bd, lru_b_r, lru_b_i, lru_lambda, cw, conv_b, conv_norm_g, conv_norm_b, seg, wout, ln1_g)

    proj, h, mix, x1, u1, y, vbc, lru, wup, wdn_s = _mix_fwd(x, mod3, win, *mixer_small, ln1_b, [_mx(ffn_w_up[0]), _mx(ffn_w_down[0])])
    wdn = wdn_s.reshape(f, d)
    u2, hh, fact, gc_all, dz2, loss_acc, dln2, dgt2 = _ffn_fwd(x1, mod3, wup, fcw, ffn_conv_b, wdn, ln2_g, ln2_b, loss_target)
    dx1, dy2, dh, dfc, dmod2 = _ffn_bwd(dz2, x1, hh, gc_all, mod3, wup, wdn, fcw, ffn_conv_b)

    flat = lambda t: t.reshape(tokens, t.shape[-1])
    fc = wup.shape[2]
    g_up = _wgrad(flat(u2), flat(dh), d, fc, 1, 4, 0, 0, "wgrad_up")
    g_dn, r_up = _wgrad(flat(fact), flat(dy2), fc, d, f // fc, 1, 0, 0, "wgrad_down",
                        exchange=([g_up], _pair_out_shapes([g_up]), _pair_plan, _pair_sems(1)))
    g_dn = g_dn.reshape(4, f // 4, d)
    s_up, r_dn = _pair_add(g_up, r_up, cidx, "grad_pair_add_ffn_w_up",
                           exchange=([g_dn], _pair_out_shapes([g_dn]), _pair_plan, _pair_sems(1)))
    ffn_sum = [s_up, _pair_add(g_dn, r_dn, cidx, "grad_pair_add_ffn_w_down")]
    grad_x, dproj, dmix, xcg, vecw, dlcw, dcw, dln1, dmod1, *ffn_recv = _mix_bwd(
        dx1, x, mix, proj, h, vbc, lru, mod3, win, *mixer_small, ffn_sum)
    g_ri = _wgrad(flat(xcg), flat(xcg), wd, wd, 1, 2, 0, 1, "wgrad_gates")
    dh_ = wd // _N_HEADS
    on_diagonal = jnp.eye(_N_HEADS, dtype=_F32)[None, :, None, :, None]
    g_ri = jnp.sum(g_ri.reshape(2, _N_HEADS, dh_, _N_HEADS, dh_) * on_diagonal, axis=3)

    dmod = jnp.concatenate([dmod1.reshape(bl, 3 * d), dmod2.reshape(bl, 2 * d), dgt2.reshape(bl, d)], axis=1)
    pieces = [vecw, dlcw, dcw, jnp.concatenate([dln1, dln2], axis=0), dfc, g_ri, loss_acc[:, 0:128],
              jnp.pad(dmod, ((0, 8 - bl), (0, 0)))]
    shapes = [p.shape for p in pieces]
    pack = _pack(pieces)
    g_in, every = _wgrad(flat(u1), flat(dproj), d, wd, 1, 4, 0, 0, "wgrad_in", exchange=(
        [pack], [jax.ShapeDtypeStruct((8,) + pack.shape, _F32)], _dev_gather_plan, _dev_gather_sems(1)))
    wire_shape = lambda t: [jax.ShapeDtypeStruct((3,) + t.shape[1:], t.dtype)]
    ffn_half = [_chip_add(s, r, kidx, "grad_chip_add_" + n) for s, r, n in zip(ffn_sum, ffn_recv, _BIG[2:])]
    same = lambda ts: [jax.ShapeDtypeStruct(t.shape, t.dtype) for t in ts]
    r_in, = _pair_exchange([g_in], "grad_pair_exchange_w_in")
    s_in, wire_in, *ffn_theirs = _pair_add(g_in, r_in, cidx, "grad_pair_add_w_in", jnp.bfloat16,
                                           exchange=(ffn_half, same(ffn_half), _pair_swap_plan, _pair_sems(2)))
    g_out, recv_in = _wgrad(flat(y), flat(dmix), d, d, 1, 1, 0, 0, "wgrad_out", exchange=(
        [wire_in], wire_shape(wire_in), _chip_reduce_plan, _chip_reduce_sems(1)))
    g_out = g_out.reshape(4, d // 4, d)
    r_out, = _pair_exchange([g_out], "grad_pair_exchange_w_out")
    s_out, wire_out = _pair_add(g_out, r_out, cidx, "grad_pair_add_w_out", jnp.bfloat16)
    recv_out, = _chip_exchange([wire_out])
    mix_half = [_chip_add(s, r, kidx, "grad_chip_add_" + n) for s, r, n in zip([s_in, s_out], [recv_in, recv_out], _BIG)]
    half, other = mix_half + ffn_half, list(_pair_swap(mix_half, "grad_pair_swap")) + ffn_theirs
    grads, deltas, new_m, new_v = {}, {}, {}, {}
    for n, mine, theirs in zip(_BIG, half, other):
        g, dl, mm, vv = _adamw_big(wt[n][0], mine, theirs, mo[n][0], vo[n][0], cidx, "adamw_" + n)
        grads[n], deltas[n], new_m[n], new_v[n] = g[None], dl[None], mm[None], vv[None]

    vecw, dlcw, dcw, dln, dfc, g_ri, loss_sum, dmod_sum = _unpack(_small_sum(every), shapes)
    loss = 0.5 * loss_sum[0, 0] / d
    dmod_all = _unpack(every, shapes)[-1].reshape(64, 6 * d)

    g_ada, dl, mm, vv = _ada_bwd(c_all, lax.dynamic_slice(dmod_all, (0, kme * nk), (64, nk)), w_ada[0], m_w_ada[0], v_w_ada[0])
    grads['w_ada'], deltas['w_ada'], new_m['w_ada'], new_v['w_ada'] = g_ada[None], dl[None], mm[None], vv[None]

    shard = lambda t, width: lax.dynamic_slice(t, (0, kme * width), (t.shape[0], width))
    small = {
        'b_ada': dmod_sum, 'lru_conv_w': shard(dlcw, wd // 4), 'lru_conv_b': vecw[0:1], 'lru_w_r': g_ri[0], 'lru_b_r': vecw[1:2],
        'lru_w_i': g_ri[1], 'lru_b_i': vecw[2:3], 'lru_lambda': vecw[3:4], 'conv_w': shard(dcw, wd // 4), 'conv_b': vecw[4:5],
        'conv_norm_g': vecw[5:6], 'conv_norm_b': vecw[6:7], 'ln1_g': dln[0:1], 'ln1_b': dln[1:2],
        'ffn_conv_w': shard(dfc[0:3], f // 4), 'ffn_conv_b': dfc[3:4], 'ln2_g': dln[2:3], 'ln2_b': dln[3:4]}
    names = list(small)
    gs = [small[n] if n == 'b_ada' else small[n].reshape(wt[n].shape) for n in names]
    gs, dls, mms, vvs = _adamw_small([wt[n] for n in names], gs, [mo[n] for n in names], [vo[n] for n in names])
    for n, g, dl, mm, vv in zip(names, gs, dls, mms, vvs):
        grads[n], deltas[n], new_m[n], new_v[n] = g, dl, mm, vv

    return (loss, grad_x, *[grads[n] for n in _WEIGHTS], *[deltas[n] for n in _WEIGHTS],
            *[new_m[n] for n in _WEIGHTS], *[new_v[n] for n in _WEIGHTS])
```
